```python
import math
import jax, jax.numpy as jnp
from jax import lax
import numpy as np

D_MODEL = 2048
BATCH = 8
SEQ = 4096
DEPTH = 2

PL_DIM = 256
D_FF = 4 * D_MODEL
NORM_EPS = 1e-6
N_EVEN = (DEPTH + 1) // 2
N_ODD = DEPTH // 2
S5_WIDTH = D_MODEL // 4
S5_GROUP = 16
S5_GROUPS = S5_WIDTH // S5_GROUP
S5_STATE = 64
SSD_WIDTH = D_MODEL - S5_WIDTH
SSD_HEAD_DIM = 64
SSD_HEADS = SSD_WIDTH // SSD_HEAD_DIM
SSD_GROUPS = 4
SSD_STATE = 128
SSD_CONV = 4
SSD_CHUNK = 128
SSD_CONV_DIM = SSD_WIDTH + 2 * SSD_GROUPS * SSD_STATE
EVEN_IN = S5_WIDTH + SSD_WIDTH + SSD_CONV_DIM + SSD_HEADS
EVEN_MIX = S5_WIDTH + SSD_WIDTH
RWKV_WIDTH = D_MODEL // 2
RWKV_HEAD_DIM = 64
RWKV_HEADS = RWKV_WIDTH // RWKV_HEAD_DIM
RWKV_DECAY_LORA = 96
RWKV_AAA_LORA = 96
RWKV_GATE_LORA = 256
RWKV_GN_EPS = 64e-5
RWKV_IN = 3 * RWKV_WIDTH + RWKV_DECAY_LORA + RWKV_AAA_LORA + RWKV_GATE_LORA
LRU_WIDTH = D_MODEL - RWKV_WIDTH
LRU_BLOCKS = 16
LRU_BLOCK = LRU_WIDTH // LRU_BLOCKS
LRU_CONV = 4
LRU_C = 8.0
ODD_IN = RWKV_IN + 2 * LRU_WIDTH
ODD_MIX = RWKV_WIDTH + LRU_WIDTH

kernel_name = 'hybrid_s5_ssd_rwkv7_rglru_trunk'


def rmsnorm(x, g):
    xf = x.astype(jnp.float32)
    y = xf * lax.rsqrt(jnp.mean(xf * xf, axis=-1, keepdims=True) + NORM_EPS)
    return (y * g.astype(jnp.float32)).astype(x.dtype)


def causal_dwconv(x, w, b):
    k = w.shape[0]
    y = lax.conv_general_dilated(x, w.astype(x.dtype)[:, None, :], window_strides=(1,),
                                 padding=[(k - 1, 0)], dimension_numbers=('NWC', 'WIO', 'NWC'),
                                 feature_group_count=x.shape[-1])
    return y + b.astype(x.dtype)


def token_shift(x):
    return jnp.pad(x, ((0, 0), (1, 0), (0, 0)))[:, :-1]


def s5_mixer(u, lam_re, lam_im, log_step, b_re, b_im, c_re, c_im, d_skip, glu_w, glu_b):
    f32 = jnp.float32
    bsz, seq, _ = u.shape
    uf = u.astype(f32)
    ug = uf.reshape(bsz, seq, S5_GROUPS, S5_GROUP)
    step = jnp.exp(log_step.astype(f32))[:, None]
    lr, li = lam_re.astype(f32), lam_im.astype(f32)
    mag = jnp.exp(lr * step)
    abar_re, abar_im = mag * jnp.cos(li * step), mag * jnp.sin(li * step)
    den = lr * lr + li * li
    nr = abar_re - 1.0
    coef_re = ((nr * lr + abar_im * li) / den)[..., None]
    coef_im = ((abar_im * lr - nr * li) / den)[..., None]
    br, bi = b_re.astype(f32), b_im.astype(f32)
    bbar_re = coef_re * br - coef_im * bi
    bbar_im = coef_re * bi + coef_im * br
    bu_re = jnp.einsum('bsgh,gph->bsgp', ug, bbar_re)
    bu_im = jnp.einsum('bsgh,gph->bsgp', ug, bbar_im)
    a_re = jnp.broadcast_to(abar_re, bu_re.shape)
    a_im = jnp.broadcast_to(abar_im, bu_re.shape)

    def combine(e1, e2):
        a1r, a1i, b1r, b1i = e1
        a2r, a2i, b2r, b2i = e2
        return (a2r * a1r - a2i * a1i, a2r * a1i + a2i * a1r,
                a2r * b1r - a2i * b1i + b2r, a2r * b1i + a2i * b1r + b2i)

    _, _, xr, xi = lax.associative_scan(combine, (a_re, a_im, bu_re, bu_im), axis=1)
    y = (jnp.einsum('ghp,bsgp->bsgh', c_re.astype(f32), xr)
         - jnp.einsum('ghp,bsgp->bsgh', c_im.astype(f32), xi))
    y = y.reshape(bsz, seq, S5_WIDTH) + d_skip.astype(f32) * uf
    act = jax.nn.gelu(y)
    return act * jax.nn.sigmoid(act @ glu_w.astype(f32) + glu_b.astype(f32))


def ssd_chunked(x, da, bm, cm):
    bsz, seq, nh, hd = x.shape
    nc, L, g = seq // SSD_CHUNK, SSD_CHUNK, SSD_GROUPS
    j = nh // g
    x = x.reshape(bsz, nc, L, g, j, hd)
    bm = bm.reshape(bsz, nc, L, g, SSD_STATE)
    cm = cm.reshape(bsz, nc, L, g, SSD_STATE)
    a_cum = jnp.cumsum(da.reshape(bsz, nc, L, g, j).transpose(0, 3, 4, 1, 2), axis=-1)
    mask = jnp.tril(jnp.ones((L, L), dtype=bool))
    seg = a_cum[..., :, None] - a_cum[..., None, :]
    decay = jnp.exp(jnp.where(mask, seg, -jnp.inf))
    scores = jnp.einsum('bclgn,bcsgn->bgcls', cm, bm)
    y_diag = jnp.einsum('bgjcls,bcsgjp->bclgjp', scores[:, :, None] * decay, x)
    decay_states = jnp.exp(a_cum[..., -1:] - a_cum).transpose(0, 3, 4, 1, 2)[..., None]
    states = jnp.einsum('bclgn,bclgjp->bcgjpn', bm, x * decay_states)
    chunk_decay = jnp.exp(a_cum[..., -1])

    def step(carry, inp):
        st, dec = inp
        return carry * dec[..., None, None] + st, carry

    init = jnp.zeros((bsz, g, j, hd, SSD_STATE), x.dtype)
    _, prev = lax.scan(step, init, (jnp.moveaxis(states, 1, 0), jnp.moveaxis(chunk_decay, -1, 0)))
    prev = jnp.moveaxis(prev, 0, 1)
    y_off = (jnp.einsum('bclgn,bcgjpn->bclgjp', cm, prev)
             * jnp.exp(a_cum).transpose(0, 3, 4, 1, 2)[..., None])
    return (y_diag + y_off).reshape(bsz, seq, nh, hd)


def ssd_mixer(z, xbc, dt_raw, conv_w, conv_b, dt_bias, a_log, d_skip, norm_g):
    f32 = jnp.float32
    bsz, seq, _ = z.shape
    xbc = jax.nn.silu(causal_dwconv(xbc, conv_w, conv_b).astype(f32))
    xs, bm, cm = jnp.split(xbc, [SSD_WIDTH, SSD_WIDTH + SSD_GROUPS * SSD_STATE], axis=-1)
    xs = xs.reshape(bsz, seq, SSD_HEADS, SSD_HEAD_DIM)
    bm = bm.reshape(bsz, seq, SSD_GROUPS, SSD_STATE)
    cm = cm.reshape(bsz, seq, SSD_GROUPS, SSD_STATE)
    dt = jax.nn.softplus(dt_raw.astype(f32) + dt_bias.astype(f32))
    da = dt * (-jnp.exp(a_log.astype(f32)))
    y = ssd_chunked(xs * dt[..., None], da, bm, cm) + xs * d_skip.astype(f32)[:, None]
    y = y.reshape(bsz, seq, SSD_WIDTH) * jax.nn.silu(z.astype(f32))
    y = y.reshape(bsz, seq, SSD_GROUPS, SSD_WIDTH // SSD_GROUPS)
    y = y * lax.rsqrt(jnp.mean(y * y, axis=-1, keepdims=True) + NORM_EPS)
    return y.reshape(bsz, seq, SSD_WIDTH) * norm_g.astype(f32)


def rwkv7_mixer(f, mu, w0, w_up, a0, a_up, g_up, k_k, k_a, r_k, ln_g, ln_b):
    f32 = jnp.float32
    f = f.astype(f32)
    f = f + (token_shift(f) - f) * mu.astype(f32)
    W = RWKV_WIDTH
    r, k, v, wl, al, gl = jnp.split(
        f, [W, 2 * W, 3 * W, 3 * W + RWKV_DECAY_LORA, 3 * W + RWKV_DECAY_LORA + RWKV_AAA_LORA], axis=-1)
    w = -jax.nn.softplus(-(w0.astype(f32) + jnp.tanh(wl) @ w_up.astype(f32))) - 0.5
    decay = jnp.exp(-jnp.exp(w))
    a = jax.nn.sigmoid(a0.astype(f32) + al @ a_up.astype(f32))
    g = jax.nn.sigmoid(gl) @ g_up.astype(f32)
    kk = k * k_k.astype(f32)
    k = k * (1.0 + (a - 1.0) * k_a.astype(f32))
    bsz, seq, _ = f.shape
    hs = lambda t: t.reshape(bsz, seq, RWKV_HEADS, RWKV_HEAD_DIM)
    r, k, v, kk, a, decay = hs(r), hs(k), hs(v), hs(kk), hs(a), hs(decay)
    kk = kk * lax.rsqrt(jnp.maximum(jnp.sum(kk * kk, axis=-1, keepdims=True), 1e-24))

    def step(state, inp):
        r_t, w_t, k_t, v_t, kk_t, a_t = inp
        sa = jnp.einsum('bhvk,bhk->bhv', state, -kk_t)
        state = (state * w_t[:, :, None, :] + sa[..., None] * (kk_t * a_t)[:, :, None, :]
                 + v_t[..., None] * k_t[:, :, None, :])
        return state, jnp.einsum('bhvk,bhk->bhv', state, r_t)

    tm = lambda t: jnp.moveaxis(t, 1, 0)
    init = jnp.zeros((bsz, RWKV_HEADS, RWKV_HEAD_DIM, RWKV_HEAD_DIM), f32)
    _, y = lax.scan(step, init, (tm(r), tm(decay), tm(k), tm(v), tm(kk), tm(a)))
    y = jnp.moveaxis(y, 0, 1)
    mean = jnp.mean(y, axis=-1, keepdims=True)
    var = jnp.mean(jnp.square(y - mean), axis=-1, keepdims=True)
    y = ((y - mean) * lax.rsqrt(var + RWKV_GN_EPS)).reshape(bsz, seq, W)
    y = y * ln_g.astype(f32) + ln_b.astype(f32)
    bonus = jnp.sum(r * k * r_k.astype(f32), axis=-1, keepdims=True) * v
    return (y + bonus.reshape(bsz, seq, W)) * g


def rglru_mixer(xl, gl, conv_w, conv_b, w_a, b_a, w_x, b_x, lam):
    f32 = jnp.float32
    bsz, seq, _ = xl.shape
    xc = causal_dwconv(xl, conv_w, conv_b).astype(f32)
    xb = xc.reshape(bsz, seq, LRU_BLOCKS, LRU_BLOCK)
    gate_r = jax.nn.sigmoid(jnp.einsum('bshi,hij->bshj', xb, w_a.astype(f32)) + b_a.astype(f32))
    gate_i = jax.nn.sigmoid(jnp.einsum('bshi,hij->bshj', xb, w_x.astype(f32)) + b_x.astype(f32))
    log_a = -LRU_C * gate_r * jax.nn.softplus(-lam.astype(f32))
    a = jnp.exp(log_a)
    mult = jnp.sqrt(jnp.maximum(-jnp.expm1(2.0 * log_a), 0.0))
    mult = mult.at[:, 0].set(1.0)
    bx = xb * gate_i * mult

    def combine(e1, e2):
        a1, b1 = e1
        a2, b2 = e2
        return a1 * a2, a2 * b1 + b2

    _, hseq = lax.associative_scan(combine, (a, bx), axis=1)
    return hseq.reshape(bsz, seq, LRU_WIDTH) * jax.nn.gelu(gl.astype(f32))


def even_mixer(hn, in_proj, out_proj, lam_re, lam_im, log_step, b_re, b_im, c_re, c_im, s5_d,
               glu_w, glu_b, conv_w, conv_b, dt_bias, a_log, ssd_d, ssd_norm):
    proj = hn @ in_proj
    u, z, xbc, dt_raw = jnp.split(
        proj, [S5_WIDTH, S5_WIDTH + SSD_WIDTH, S5_WIDTH + SSD_WIDTH + SSD_CONV_DIM], axis=-1)
    y_a = s5_mixer(u, lam_re, lam_im, log_step, b_re, b_im, c_re, c_im, s5_d, glu_w, glu_b)
    y_b = ssd_mixer(z, xbc, dt_raw, conv_w, conv_b, dt_bias, a_log, ssd_d, ssd_norm)
    y = jnp.concatenate([y_a, y_b], axis=-1).astype(hn.dtype)
    return y @ out_proj


def odd_mixer(hn, in_proj, out_proj, mu, w0, w_up, a0, a_up, g_up, k_k, k_a, r_k, ln_g, ln_b,
              conv_w, conv_b, w_a, b_a, w_x, b_x, lam):
    proj = hn @ in_proj
    rw, xl, gl = jnp.split(proj, [RWKV_IN, RWKV_IN + LRU_WIDTH], axis=-1)
    y_c = rwkv7_mixer(rw, mu, w0, w_up, a0, a_up, g_up, k_k, k_a, r_k, ln_g, ln_b)
    y_d = rglru_mixer(xl, gl, conv_w, conv_b, w_a, b_a, w_x, b_x, lam)
    y = jnp.concatenate([y_c, y_d], axis=-1).astype(hn.dtype)
    return y @ out_proj


def squared_relu_mlp(h, w1, w2):
    return jnp.square(jax.nn.relu(h @ w1)) @ w2


def _fwd_setup_inputs(seed: int = 0) -> dict:
    key = jax.random.key(seed)
    ks = iter(jax.random.split(key, 64))
    f32 = jnp.float32

    def nrm(shape, scale):
        return scale * jax.random.normal(next(ks), shape, f32)

    def unif(shape, lo, hi):
        return jax.random.uniform(next(ks), shape, f32, lo, hi)

    D = D_MODEL
    ne, no = N_EVEN, N_ODD
    x = nrm((BATCH, SEQ, D), 1.0)
    p = nrm((DEPTH, BATCH, SEQ, PL_DIM), 1.0)
    norm_mix = 1.0 + nrm((DEPTH, D), 0.02)
    norm_ffn = 1.0 + nrm((DEPTH, D), 0.02)
    norm_pl = 1.0 + nrm((DEPTH, D), 0.02)
    mlp_w1 = nrm((DEPTH, D, D_FF), D ** -0.5)
    mlp_w2 = nrm((DEPTH, D_FF, D), D_FF ** -0.5)
    pl_proj = nrm((DEPTH, PL_DIM, D), PL_DIM ** -0.5)
    pl_gate = nrm((DEPTH, D, D), D ** -0.5)
    e_in_proj = nrm((ne, D, EVEN_IN), D ** -0.5)
    e_out_proj = nrm((ne, EVEN_MIX, D), EVEN_MIX ** -0.5)
    s5_lam_re = -0.5 + nrm((ne, S5_GROUPS, S5_STATE), 0.01)
    s5_lam_im = math.pi * jnp.arange(S5_STATE, dtype=f32) + nrm((ne, S5_GROUPS, S5_STATE), 0.01)
    s5_log_step = unif((ne, S5_GROUPS), math.log(1e-3), math.log(1e-1))
    s5_b_re = nrm((ne, S5_GROUPS, S5_STATE, S5_GROUP), (2 * S5_GROUP) ** -0.5)
    s5_b_im = nrm((ne, S5_GROUPS, S5_STATE, S5_GROUP), (2 * S5_GROUP) ** -0.5)
    s5_c_re = nrm((ne, S5_GROUPS, S5_GROUP, S5_STATE), (2 * S5_STATE) ** -0.5)
    s5_c_im = nrm((ne, S5_GROUPS, S5_GROUP, S5_STATE), (2 * S5_STATE) ** -0.5)
    s5_d = nrm((ne, S5_WIDTH), 0.5)
    s5_glu_w = nrm((ne, S5_WIDTH, S5_WIDTH), S5_WIDTH ** -0.5)
    s5_glu_b = nrm((ne, S5_WIDTH), 0.02)
    ssd_conv_w = nrm((ne, SSD_CONV, SSD_CONV_DIM), SSD_CONV ** -0.5)
    ssd_conv_b = nrm((ne, SSD_CONV_DIM), 0.02)
    dt0 = jnp.exp(unif((ne, SSD_HEADS), math.log(1e-3), math.log(1e-1)))
    ssd_dt_bias = dt0 + jnp.log(-jnp.expm1(-dt0))
    ssd_a_log = jnp.log(unif((ne, SSD_HEADS), 1.0, 16.0))
    ssd_d = 1.0 + nrm((ne, SSD_HEADS), 0.02)
    ssd_norm = 1.0 + nrm((ne, SSD_WIDTH), 0.02)
    o_in_proj = nrm((no, D, ODD_IN), D ** -0.5)
    o_out_proj = nrm((no, ODD_MIX, D), ODD_MIX ** -0.5)
    rwkv_mu = unif((no, RWKV_IN), 0.0, 1.0)
    rwkv_w0 = jnp.linspace(-6.0, -1.0, RWKV_WIDTH, dtype=f32) + nrm((no, RWKV_WIDTH), 0.1)
    rwkv_w_up = nrm((no, RWKV_DECAY_LORA, RWKV_WIDTH), 0.5 * RWKV_DECAY_LORA ** -0.5)
    rwkv_a0 = nrm((no, RWKV_WIDTH), 0.1)
    rwkv_a_up = nrm((no, RWKV_AAA_LORA, RWKV_WIDTH), 0.5 * RWKV_AAA_LORA ** -0.5)
    rwkv_g_up = nrm((no, RWKV_GATE_LORA, RWKV_WIDTH), RWKV_GATE_LORA ** -0.5)
    rwkv_k_k = 0.85 + nrm((no, RWKV_WIDTH), 0.02)
    rwkv_k_a = 1.0 + nrm((no, RWKV_WIDTH), 0.02)
    rwkv_r_k = nrm((no, RWKV_HEADS, RWKV_HEAD_DIM), 0.1)
    rwkv_ln_g = 1.0 + nrm((no, RWKV_WIDTH), 0.02)
    rwkv_ln_b = nrm((no, RWKV_WIDTH), 0.02)
    lru_conv_w = nrm((no, LRU_CONV, LRU_WIDTH), LRU_CONV ** -0.5)
    lru_conv_b = nrm((no, LRU_WIDTH), 0.02)
    lru_w_a = nrm((no, LRU_BLOCKS, LRU_BLOCK, LRU_BLOCK), LRU_BLOCK ** -0.5)
    lru_b_a = nrm((no, LRU_BLOCKS, LRU_BLOCK), 0.02)
    lru_w_x = nrm((no, LRU_BLOCKS, LRU_BLOCK, LRU_BLOCK), LRU_BLOCK ** -0.5)
    lru_b_x = nrm((no, LRU_BLOCKS, LRU_BLOCK), 0.02)
    a_pow = unif((no, LRU_BLOCKS, LRU_BLOCK), 0.9, 0.999)
    a_base = a_pow ** (1.0 / LRU_C)
    lru_lam = jnp.log(a_base) - jnp.log1p(-a_base)
    norm_final = 1.0 + nrm((D,), 0.02)
    return {'x': x, 'p': p, 'norm_mix': norm_mix, 'norm_ffn': norm_ffn, 'norm_pl': norm_pl,
            'mlp_w1': mlp_w1, 'mlp_w2': mlp_w2, 'pl_proj': pl_proj, 'pl_gate': pl_gate,
            'e_in_proj': e_in_proj, 'e_out_proj': e_out_proj,
            's5_lam_re': s5_lam_re, 's5_lam_im': s5_lam_im, 's5_log_step': s5_log_step,
            's5_b_re': s5_b_re, 's5_b_im': s5_b_im, 's5_c_re': s5_c_re, 's5_c_im': s5_c_im,
            's5_d': s5_d, 's5_glu_w': s5_glu_w, 's5_glu_b': s5_glu_b,
            'ssd_conv_w': ssd_conv_w, 'ssd_conv_b': ssd_conv_b, 'ssd_dt_bias': ssd_dt_bias,
            'ssd_a_log': ssd_a_log, 'ssd_d': ssd_d, 'ssd_norm': ssd_norm,
            'o_in_proj': o_in_proj, 'o_out_proj': o_out_proj,
            'rwkv_mu': rwkv_mu, 'rwkv_w0': rwkv_w0, 'rwkv_w_up': rwkv_w_up, 'rwkv_a0': rwkv_a0,
            'rwkv_a_up': rwkv_a_up, 'rwkv_g_up': rwkv_g_up, 'rwkv_k_k': rwkv_k_k, 'rwkv_k_a': rwkv_k_a,
            'rwkv_r_k': rwkv_r_k, 'rwkv_ln_g': rwkv_ln_g, 'rwkv_ln_b': rwkv_ln_b,
            'lru_conv_w': lru_conv_w, 'lru_conv_b': lru_conv_b, 'lru_w_a': lru_w_a, 'lru_b_a': lru_b_a,
            'lru_w_x': lru_w_x, 'lru_b_x': lru_b_x, 'lru_lam': lru_lam, 'norm_final': norm_final}


def _fwd_reference(x, p, norm_mix, norm_ffn, norm_pl, mlp_w1, mlp_w2, pl_proj, pl_gate,
              e_in_proj, e_out_proj, s5_lam_re, s5_lam_im, s5_log_step, s5_b_re, s5_b_im,
              s5_c_re, s5_c_im, s5_d, s5_glu_w, s5_glu_b, ssd_conv_w, ssd_conv_b, ssd_dt_bias,
              ssd_a_log, ssd_d, ssd_norm, o_in_proj, o_out_proj, rwkv_mu, rwkv_w0, rwkv_w_up,
              rwkv_a0, rwkv_a_up, rwkv_g_up, rwkv_k_k, rwkv_k_a, rwkv_r_k, rwkv_ln_g, rwkv_ln_b,
              lru_conv_w, lru_conv_b, lru_w_a, lru_b_a, lru_w_x, lru_b_x, lru_lam, norm_final):
    h = x
    for i in range(DEPTH):
        hn = rmsnorm(h, norm_mix[i])
        j = i // 2
        if i % 2 == 0:
            mix = even_mixer(hn, e_in_proj[j], e_out_proj[j], s5_lam_re[j], s5_lam_im[j],
                             s5_log_step[j], s5_b_re[j], s5_b_im[j], s5_c_re[j], s5_c_im[j],
                             s5_d[j], s5_glu_w[j], s5_glu_b[j], ssd_conv_w[j], ssd_conv_b[j],
                             ssd_dt_bias[j], ssd_a_log[j], ssd_d[j], ssd_norm[j])
        else:
            mix = odd_mixer(hn, o_in_proj[j], o_out_proj[j], rwkv_mu[j], rwkv_w0[j], rwkv_w_up[j],
                            rwkv_a0[j], rwkv_a_up[j], rwkv_g_up[j], rwkv_k_k[j], rwkv_k_a[j],
                            rwkv_r_k[j], rwkv_ln_g[j], rwkv_ln_b[j], lru_conv_w[j], lru_conv_b[j],
                            lru_w_a[j], lru_b_a[j], lru_w_x[j], lru_b_x[j], lru_lam[j])
        h = h + mix
        h = h + squared_relu_mlp(rmsnorm(h, norm_ffn[i]), mlp_w1[i], mlp_w2[i])
        gate = jax.nn.sigmoid(rmsnorm(h, norm_pl[i]) @ pl_gate[i])
        h = h + gate * (p[i] @ pl_proj[i])
    return rmsnorm(h, norm_final)


import jax as _jax
import jax.numpy as _jnp

TWIN_FORMAT = 'train_step'
FWD_PARAMS = ['x', 'p', 'norm_mix', 'norm_ffn', 'norm_pl', 'mlp_w1', 'mlp_w2', 'pl_proj', 'pl_gate', 'e_in_proj', 'e_out_proj', 's5_lam_re', 's5_lam_im', 's5_log_step', 's5_b_re', 's5_b_im', 's5_c_re', 's5_c_im', 's5_d', 's5_glu_w', 's5_glu_b', 'ssd_conv_w', 'ssd_conv_b', 'ssd_dt_bias', 'ssd_a_log', 'ssd_d', 'ssd_norm', 'o_in_proj', 'o_out_proj', 'rwkv_mu', 'rwkv_w0', 'rwkv_w_up', 'rwkv_a0', 'rwkv_a_up', 'rwkv_g_up', 'rwkv_k_k', 'rwkv_k_a', 'rwkv_r_k', 'rwkv_ln_g', 'rwkv_ln_b', 'lru_conv_w', 'lru_conv_b', 'lru_w_a', 'lru_b_a', 'lru_w_x', 'lru_b_x', 'lru_lam', 'norm_final']
TWIN_WEIGHTS = ['norm_mix', 'norm_ffn', 'norm_pl', 'mlp_w1', 'mlp_w2', 'pl_proj', 'pl_gate', 'e_in_proj', 'e_out_proj', 's5_lam_re', 's5_lam_im', 's5_log_step', 's5_b_re', 's5_b_im', 's5_c_re', 's5_c_im', 's5_d', 's5_glu_w', 's5_glu_b', 'ssd_conv_w', 'ssd_conv_b', 'ssd_dt_bias', 'ssd_a_log', 'ssd_d', 'ssd_norm', 'o_in_proj', 'o_out_proj', 'rwkv_mu', 'rwkv_w0', 'rwkv_w_up', 'rwkv_a0', 'rwkv_a_up', 'rwkv_g_up', 'rwkv_k_k', 'rwkv_k_a', 'rwkv_r_k', 'rwkv_ln_g', 'rwkv_ln_b', 'lru_conv_w', 'lru_conv_b', 'lru_w_a', 'lru_b_a', 'lru_w_x', 'lru_b_x', 'lru_lam', 'norm_final']
TWIN_DIFF_INPUT = 'x'
TWIN_INPUTS = ['x', 'p', 'norm_mix', 'norm_ffn', 'norm_pl', 'mlp_w1', 'mlp_w2', 'pl_proj', 'pl_gate', 'e_in_proj', 'e_out_proj', 's5_lam_re', 's5_lam_im', 's5_log_step', 's5_b_re', 's5_b_im', 's5_c_re', 's5_c_im', 's5_d', 's5_glu_w', 's5_glu_b', 'ssd_conv_w', 'ssd_conv_b', 'ssd_dt_bias', 'ssd_a_log', 'ssd_d', 'ssd_norm', 'o_in_proj', 'o_out_proj', 'rwkv_mu', 'rwkv_w0', 'rwkv_w_up', 'rwkv_a0', 'rwkv_a_up', 'rwkv_g_up', 'rwkv_k_k', 'rwkv_k_a', 'rwkv_r_k', 'rwkv_ln_g', 'rwkv_ln_b', 'lru_conv_w', 'lru_conv_b', 'lru_w_a', 'lru_b_a', 'lru_w_x', 'lru_b_x', 'lru_lam', 'norm_final', 'loss_target', 'm_norm_mix', 'm_norm_ffn', 'm_norm_pl', 'm_mlp_w1', 'm_mlp_w2', 'm_pl_proj', 'm_pl_gate', 'm_e_in_proj', 'm_e_out_proj', 'm_s5_lam_re', 'm_s5_lam_im', 'm_s5_log_step', 'm_s5_b_re', 'm_s5_b_im', 'm_s5_c_re', 'm_s5_c_im', 'm_s5_d', 'm_s5_glu_w', 'm_s5_glu_b', 'm_ssd_conv_w', 'm_ssd_conv_b', 'm_ssd_dt_bias', 'm_ssd_a_log', 'm_ssd_d', 'm_ssd_norm', 'm_o_in_proj', 'm_o_out_proj', 'm_rwkv_mu', 'm_rwkv_w0', 'm_rwkv_w_up', 'm_rwkv_a0', 'm_rwkv_a_up', 'm_rwkv_g_up', 'm_rwkv_k_k', 'm_rwkv_k_a', 'm_rwkv_r_k', 'm_rwkv_ln_g', 'm_rwkv_ln_b', 'm_lru_conv_w', 'm_lru_conv_b', 'm_lru_w_a', 'm_lru_b_a', 'm_lru_w_x', 'm_lru_b_x', 'm_lru_lam', 'm_norm_final', 'v_norm_mix', 'v_norm_ffn', 'v_norm_pl', 'v_mlp_w1', 'v_mlp_w2', 'v_pl_proj', 'v_pl_gate', 'v_e_in_proj', 'v_e_out_proj', 'v_s5_lam_re', 'v_s5_lam_im', 'v_s5_log_step', 'v_s5_b_re', 'v_s5_b_im', 'v_s5_c_re', 'v_s5_c_im', 'v_s5_d', 'v_s5_glu_w', 'v_s5_glu_b', 'v_ssd_conv_w', 'v_ssd_conv_b', 'v_ssd_dt_bias', 'v_ssd_a_log', 'v_ssd_d', 'v_ssd_norm', 'v_o_in_proj', 'v_o_out_proj', 'v_rwkv_mu', 'v_rwkv_w0', 'v_rwkv_w_up', 'v_rwkv_a0', 'v_rwkv_a_up', 'v_rwkv_g_up', 'v_rwkv_k_k', 'v_rwkv_k_a', 'v_rwkv_r_k', 'v_rwkv_ln_g', 'v_rwkv_ln_b', 'v_lru_conv_w', 'v_lru_conv_b', 'v_lru_w_a', 'v_lru_b_a', 'v_lru_w_x', 'v_lru_b_x', 'v_lru_lam', 'v_norm_final']
TWIN_OUTPUTS = ['loss', 'grad_x', 'grad_norm_mix', 'grad_norm_ffn', 'grad_norm_pl', 'grad_mlp_w1', 'grad_mlp_w2', 'grad_pl_proj', 'grad_pl_gate', 'grad_e_in_proj', 'grad_e_out_proj', 'grad_s5_lam_re', 'grad_s5_lam_im', 'grad_s5_log_step', 'grad_s5_b_re', 'grad_s5_b_im', 'grad_s5_c_re', 'grad_s5_c_im', 'grad_s5_d', 'grad_s5_glu_w', 'grad_s5_glu_b', 'grad_ssd_conv_w', 'grad_ssd_conv_b', 'grad_ssd_dt_bias', 'grad_ssd_a_log', 'grad_ssd_d', 'grad_ssd_norm', 'grad_o_in_proj', 'grad_o_out_proj', 'grad_rwkv_mu', 'grad_rwkv_w0', 'grad_rwkv_w_up', 'grad_rwkv_a0', 'grad_rwkv_a_up', 'grad_rwkv_g_up', 'grad_rwkv_k_k', 'grad_rwkv_k_a', 'grad_rwkv_r_k', 'grad_rwkv_ln_g', 'grad_rwkv_ln_b', 'grad_lru_conv_w', 'grad_lru_conv_b', 'grad_lru_w_a', 'grad_lru_b_a', 'grad_lru_w_x', 'grad_lru_b_x', 'grad_lru_lam', 'grad_norm_final', 'delta_norm_mix', 'delta_norm_ffn', 'delta_norm_pl', 'delta_mlp_w1', 'delta_mlp_w2', 'delta_pl_proj', 'delta_pl_gate', 'delta_e_in_proj', 'delta_e_out_proj', 'delta_s5_lam_re', 'delta_s5_lam_im', 'delta_s5_log_step', 'delta_s5_b_re', 'delta_s5_b_im', 'delta_s5_c_re', 'delta_s5_c_im', 'delta_s5_d', 'delta_s5_glu_w', 'delta_s5_glu_b', 'delta_ssd_conv_w', 'delta_ssd_conv_b', 'delta_ssd_dt_bias', 'delta_ssd_a_log', 'delta_ssd_d', 'delta_ssd_norm', 'delta_o_in_proj', 'delta_o_out_proj', 'delta_rwkv_mu', 'delta_rwkv_w0', 'delta_rwkv_w_up', 'delta_rwkv_a0', 'delta_rwkv_a_up', 'delta_rwkv_g_up', 'delta_rwkv_k_k', 'delta_rwkv_k_a', 'delta_rwkv_r_k', 'delta_rwkv_ln_g', 'delta_rwkv_ln_b', 'delta_lru_conv_w', 'delta_lru_conv_b', 'delta_lru_w_a', 'delta_lru_b_a', 'delta_lru_w_x', 'delta_lru_b_x', 'delta_lru_lam', 'delta_norm_final', 'new_m_norm_mix', 'new_m_norm_ffn', 'new_m_norm_pl', 'new_m_mlp_w1', 'new_m_mlp_w2', 'new_m_pl_proj', 'new_m_pl_gate', 'new_m_e_in_proj', 'new_m_e_out_proj', 'new_m_s5_lam_re', 'new_m_s5_lam_im', 'new_m_s5_log_step', 'new_m_s5_b_re', 'new_m_s5_b_im', 'new_m_s5_c_re', 'new_m_s5_c_im', 'new_m_s5_d', 'new_m_s5_glu_w', 'new_m_s5_glu_b', 'new_m_ssd_conv_w', 'new_m_ssd_conv_b', 'new_m_ssd_dt_bias', 'new_m_ssd_a_log', 'new_m_ssd_d', 'new_m_ssd_norm', 'new_m_o_in_proj', 'new_m_o_out_proj', 'new_m_rwkv_mu', 'new_m_rwkv_w0', 'new_m_rwkv_w_up', 'new_m_rwkv_a0', 'new_m_rwkv_a_up', 'new_m_rwkv_g_up', 'new_m_rwkv_k_k', 'new_m_rwkv_k_a', 'new_m_rwkv_r_k', 'new_m_rwkv_ln_g', 'new_m_rwkv_ln_b', 'new_m_lru_conv_w', 'new_m_lru_conv_b', 'new_m_lru_w_a', 'new_m_lru_b_a', 'new_m_lru_w_x', 'new_m_lru_b_x', 'new_m_lru_lam', 'new_m_norm_final', 'new_v_norm_mix', 'new_v_norm_ffn', 'new_v_norm_pl', 'new_v_mlp_w1', 'new_v_mlp_w2', 'new_v_pl_proj', 'new_v_pl_gate', 'new_v_e_in_proj', 'new_v_e_out_proj', 'new_v_s5_lam_re', 'new_v_s5_lam_im', 'new_v_s5_log_step', 'new_v_s5_b_re', 'new_v_s5_b_im', 'new_v_s5_c_re', 'new_v_s5_c_im', 'new_v_s5_d', 'new_v_s5_glu_w', 'new_v_s5_glu_b', 'new_v_ssd_conv_w', 'new_v_ssd_conv_b', 'new_v_ssd_dt_bias', 'new_v_ssd_a_log', 'new_v_ssd_d', 'new_v_ssd_norm', 'new_v_o_in_proj', 'new_v_o_out_proj', 'new_v_rwkv_mu', 'new_v_rwkv_w0', 'new_v_rwkv_w_up', 'new_v_rwkv_a0', 'new_v_rwkv_a_up', 'new_v_rwkv_g_up', 'new_v_rwkv_k_k', 'new_v_rwkv_k_a', 'new_v_rwkv_r_k', 'new_v_rwkv_ln_g', 'new_v_rwkv_ln_b', 'new_v_lru_conv_w', 'new_v_lru_conv_b', 'new_v_lru_w_a', 'new_v_lru_b_a', 'new_v_lru_w_x', 'new_v_lru_b_x', 'new_v_lru_lam', 'new_v_norm_final']
TWIN_LEAF_KINDS = {'loss': 'loss', 'grad_x': 'grad_x', 'grad_norm_mix': 'grad_w', 'grad_norm_ffn': 'grad_w', 'grad_norm_pl': 'grad_w', 'grad_mlp_w1': 'grad_w', 'grad_mlp_w2': 'grad_w', 'grad_pl_proj': 'grad_w', 'grad_pl_gate': 'grad_w', 'grad_e_in_proj': 'grad_w', 'grad_e_out_proj': 'grad_w', 'grad_s5_lam_re': 'grad_w', 'grad_s5_lam_im': 'grad_w', 'grad_s5_log_step': 'grad_w', 'grad_s5_b_re': 'grad_w', 'grad_s5_b_im': 'grad_w', 'grad_s5_c_re': 'grad_w', 'grad_s5_c_im': 'grad_w', 'grad_s5_d': 'grad_w', 'grad_s5_glu_w': 'grad_w', 'grad_s5_glu_b': 'grad_w', 'grad_ssd_conv_w': 'grad_w', 'grad_ssd_conv_b': 'grad_w', 'grad_ssd_dt_bias': 'grad_w', 'grad_ssd_a_log': 'grad_w', 'grad_ssd_d': 'grad_w', 'grad_ssd_norm': 'grad_w', 'grad_o_in_proj': 'grad_w', 'grad_o_out_proj': 'grad_w', 'grad_rwkv_mu': 'grad_w', 'grad_rwkv_w0': 'grad_w', 'grad_rwkv_w_up': 'grad_w', 'grad_rwkv_a0': 'grad_w', 'grad_rwkv_a_up': 'grad_w', 'grad_rwkv_g_up': 'grad_w', 'grad_rwkv_k_k': 'grad_w', 'grad_rwkv_k_a': 'grad_w', 'grad_rwkv_r_k': 'grad_w', 'grad_rwkv_ln_g': 'grad_w', 'grad_rwkv_ln_b': 'grad_w', 'grad_lru_conv_w': 'grad_w', 'grad_lru_conv_b': 'grad_w', 'grad_lru_w_a': 'grad_w', 'grad_lru_b_a': 'grad_w', 'grad_lru_w_x': 'grad_w', 'grad_lru_b_x': 'grad_w', 'grad_lru_lam': 'grad_w', 'grad_norm_final': 'grad_w', 'delta_norm_mix': 'delta_w', 'delta_norm_ffn': 'delta_w', 'delta_norm_pl': 'delta_w', 'delta_mlp_w1': 'delta_w', 'delta_mlp_w2': 'delta_w', 'delta_pl_proj': 'delta_w', 'delta_pl_gate': 'delta_w', 'delta_e_in_proj': 'delta_w', 'delta_e_out_proj': 'delta_w', 'delta_s5_lam_re': 'delta_w', 'delta_s5_lam_im': 'delta_w', 'delta_s5_log_step': 'delta_w', 'delta_s5_b_re': 'delta_w', 'delta_s5_b_im': 'delta_w', 'delta_s5_c_re': 'delta_w', 'delta_s5_c_im': 'delta_w', 'delta_s5_d': 'delta_w', 'delta_s5_glu_w': 'delta_w', 'delta_s5_glu_b': 'delta_w', 'delta_ssd_conv_w': 'delta_w', 'delta_ssd_conv_b': 'delta_w', 'delta_ssd_dt_bias': 'delta_w', 'delta_ssd_a_log': 'delta_w', 'delta_ssd_d': 'delta_w', 'delta_ssd_norm': 'delta_w', 'delta_o_in_proj': 'delta_w', 'delta_o_out_proj': 'delta_w', 'delta_rwkv_mu': 'delta_w', 'delta_rwkv_w0': 'delta_w', 'delta_rwkv_w_up': 'delta_w', 'delta_rwkv_a0': 'delta_w', 'delta_rwkv_a_up': 'delta_w', 'delta_rwkv_g_up': 'delta_w', 'delta_rwkv_k_k': 'delta_w', 'delta_rwkv_k_a': 'delta_w', 'delta_rwkv_r_k': 'delta_w', 'delta_rwkv_ln_g': 'delta_w', 'delta_rwkv_ln_b': 'delta_w', 'delta_lru_conv_w': 'delta_w', 'delta_lru_conv_b': 'delta_w', 'delta_lru_w_a': 'delta_w', 'delta_lru_b_a': 'delta_w', 'delta_lru_w_x': 'delta_w', 'delta_lru_b_x': 'delta_w', 'delta_lru_lam': 'delta_w', 'delta_norm_final': 'delta_w', 'new_m_norm_mix': 'new_m', 'new_m_norm_ffn': 'new_m', 'new_m_norm_pl': 'new_m', 'new_m_mlp_w1': 'new_m', 'new_m_mlp_w2': 'new_m', 'new_m_pl_proj': 'new_m', 'new_m_pl_gate': 'new_m', 'new_m_e_in_proj': 'new_m', 'new_m_e_out_proj': 'new_m', 'new_m_s5_lam_re': 'new_m', 'new_m_s5_lam_im': 'new_m', 'new_m_s5_log_step': 'new_m', 'new_m_s5_b_re': 'new_m', 'new_m_s5_b_im': 'new_m', 'new_m_s5_c_re': 'new_m', 'new_m_s5_c_im': 'new_m', 'new_m_s5_d': 'new_m', 'new_m_s5_glu_w': 'new_m', 'new_m_s5_glu_b': 'new_m', 'new_m_ssd_conv_w': 'new_m', 'new_m_ssd_conv_b': 'new_m', 'new_m_ssd_dt_bias': 'new_m', 'new_m_ssd_a_log': 'new_m', 'new_m_ssd_d': 'new_m', 'new_m_ssd_norm': 'new_m', 'new_m_o_in_proj': 'new_m', 'new_m_o_out_proj': 'new_m', 'new_m_rwkv_mu': 'new_m', 'new_m_rwkv_w0': 'new_m', 'new_m_rwkv_w_up': 'new_m', 'new_m_rwkv_a0': 'new_m', 'new_m_rwkv_a_up': 'new_m', 'new_m_rwkv_g_up': 'new_m', 'new_m_rwkv_k_k': 'new_m', 'new_m_rwkv_k_a': 'new_m', 'new_m_rwkv_r_k': 'new_m', 'new_m_rwkv_ln_g': 'new_m', 'new_m_rwkv_ln_b': 'new_m', 'new_m_lru_conv_w': 'new_m', 'new_m_lru_conv_b': 'new_m', 'new_m_lru_w_a': 'new_m', 'new_m_lru_b_a': 'new_m', 'new_m_lru_w_x': 'new_m', 'new_m_lru_b_x': 'new_m', 'new_m_lru_lam': 'new_m', 'new_m_norm_final': 'new_m', 'new_v_norm_mix': 'new_v', 'new_v_norm_ffn': 'new_v', 'new_v_norm_pl': 'new_v', 'new_v_mlp_w1': 'new_v', 'new_v_mlp_w2': 'new_v', 'new_v_pl_proj': 'new_v', 'new_v_pl_gate': 'new_v', 'new_v_e_in_proj': 'new_v', 'new_v_e_out_proj': 'new_v', 'new_v_s5_lam_re': 'new_v', 'new_v_s5_lam_im': 'new_v', 'new_v_s5_log_step': 'new_v', 'new_v_s5_b_re': 'new_v', 'new_v_s5_b_im': 'new_v', 'new_v_s5_c_re': 'new_v', 'new_v_s5_c_im': 'new_v', 'new_v_s5_d': 'new_v', 'new_v_s5_glu_w': 'new_v', 'new_v_s5_glu_b': 'new_v', 'new_v_ssd_conv_w': 'new_v', 'new_v_ssd_conv_b': 'new_v', 'new_v_ssd_dt_bias': 'new_v', 'new_v_ssd_a_log': 'new_v', 'new_v_ssd_d': 'new_v', 'new_v_ssd_norm': 'new_v', 'new_v_o_in_proj': 'new_v', 'new_v_o_out_proj': 'new_v', 'new_v_rwkv_mu': 'new_v', 'new_v_rwkv_w0': 'new_v', 'new_v_rwkv_w_up': 'new_v', 'new_v_rwkv_a0': 'new_v', 'new_v_rwkv_a_up': 'new_v', 'new_v_rwkv_g_up': 'new_v', 'new_v_rwkv_k_k': 'new_v', 'new_v_rwkv_k_a': 'new_v', 'new_v_rwkv_r_k': 'new_v', 'new_v_rwkv_ln_g': 'new_v', 'new_v_rwkv_ln_b': 'new_v', 'new_v_lru_conv_w': 'new_v', 'new_v_lru_conv_b': 'new_v', 'new_v_lru_w_a': 'new_v', 'new_v_lru_b_a': 'new_v', 'new_v_lru_w_x': 'new_v', 'new_v_lru_b_x': 'new_v', 'new_v_lru_lam': 'new_v', 'new_v_norm_final': 'new_v'}


def _forward(args):
    return _fwd_reference(*[args[k] for k in FWD_PARAMS])


def _output_shape():
    def fwd():
        inp = _fwd_setup_inputs(0)
        return _fwd_reference(*[inp[k] for k in FWD_PARAMS])
    out = _jax.eval_shape(fwd)
    return out.shape, out.dtype

N_MICROBATCH = 1
ADAM_LR = 0.001
ADAM_B1 = 0.9
ADAM_B2 = 0.999
ADAM_EPS = 1e-08
ADAM_WD = 0.01
ADAM_STEP = 10
PER_EXAMPLE_BATCH_AXIS = {'x': 0, 'p': 1, 'loss_target': 0}
SHARED_INPUTS = []
_WEIGHT_DTYPES = {'norm_mix': _jnp.float32, 'norm_ffn': _jnp.float32, 'norm_pl': _jnp.float32, 'mlp_w1': _jnp.float32, 'mlp_w2': _jnp.float32, 'pl_proj': _jnp.float32, 'pl_gate': _jnp.float32, 'e_in_proj': _jnp.float32, 'e_out_proj': _jnp.float32, 's5_lam_re': _jnp.float32, 's5_lam_im': _jnp.float32, 's5_log_step': _jnp.float32, 's5_b_re': _jnp.float32, 's5_b_im': _jnp.float32, 's5_c_re': _jnp.float32, 's5_c_im': _jnp.float32, 's5_d': _jnp.float32, 's5_glu_w': _jnp.float32, 's5_glu_b': _jnp.float32, 'ssd_conv_w': _jnp.float32, 'ssd_conv_b': _jnp.float32, 'ssd_dt_bias': _jnp.float32, 'ssd_a_log': _jnp.float32, 'ssd_d': _jnp.float32, 'ssd_norm': _jnp.float32, 'o_in_proj': _jnp.float32, 'o_out_proj': _jnp.float32, 'rwkv_mu': _jnp.float32, 'rwkv_w0': _jnp.float32, 'rwkv_w_up': _jnp.float32, 'rwkv_a0': _jnp.float32, 'rwkv_a_up': _jnp.float32, 'rwkv_g_up': _jnp.float32, 'rwkv_k_k': _jnp.float32, 'rwkv_k_a': _jnp.float32, 'rwkv_r_k': _jnp.float32, 'rwkv_ln_g': _jnp.float32, 'rwkv_ln_b': _jnp.float32, 'lru_conv_w': _jnp.float32, 'lru_conv_b': _jnp.float32, 'lru_w_a': _jnp.float32, 'lru_b_a': _jnp.float32, 'lru_w_x': _jnp.float32, 'lru_b_x': _jnp.float32, 'lru_lam': _jnp.float32, 'norm_final': _jnp.float32}
MOMENT_SCALE = {'norm_mix': 8.560899e-02, 'norm_ffn': 6.334477e-02, 'norm_pl': 9.600384e-03, 'mlp_w1': 3.167487e-02, 'mlp_w2': 7.033358e-02, 'pl_proj': 2.392653e-02, 'pl_gate': 9.482925e-03, 'e_in_proj': 6.561912e-02, 'e_out_proj': 6.754170e-02, 's5_lam_re': 1.142130e-03, 's5_lam_im': 1.287335e-03, 's5_log_step': 1.042739e+00, 's5_b_re': 8.005930e-04, 's5_b_im': 7.942618e-04, 's5_c_re': 1.610645e-03, 's5_c_im': 1.571917e-03, 's5_d': 3.012225e-02, 's5_glu_w': 1.803049e-03, 's5_glu_b': 7.718983e-03, 'ssd_conv_w': 6.234604e-02, 'ssd_conv_b': 8.234261e-02, 'ssd_dt_bias': 1.651673e-01, 'ssd_a_log': 3.911416e-01, 'ssd_d': 5.219893e-01, 'ssd_norm': 7.999508e-02, 'o_in_proj': 4.270945e-02, 'o_out_proj': 4.944940e-02, 'rwkv_mu': 3.769891e-02, 'rwkv_w0': 1.215513e-02, 'rwkv_w_up': 2.926997e-03, 'rwkv_a0': 1.113683e-02, 'rwkv_a_up': 1.005531e-02, 'rwkv_g_up': 2.381350e-02, 'rwkv_k_k': 4.514874e-02, 'rwkv_k_a': 4.315947e-02, 'rwkv_r_k': 5.674046e-02, 'rwkv_ln_g': 2.371101e-02, 'rwkv_ln_b': 3.404156e-02, 'lru_conv_w': 8.038846e-02, 'lru_conv_b': 2.412670e-01, 'lru_w_a': 9.399460e-03, 'lru_b_a': 1.634791e-02, 'lru_w_x': 1.872164e-02, 'lru_b_x': 3.242948e-02, 'lru_lam': 4.296609e-02, 'norm_final': 1.621427e+01}


def _to_microbatches(a, axis):
    t = _jnp.moveaxis(a, axis, 0)
    t = t.reshape((N_MICROBATCH, t.shape[0] // N_MICROBATCH) + t.shape[1:])
    return _jnp.moveaxis(t, 1, axis + 1)


def setup_inputs(seed: int = 0) -> dict:
    inp = _fwd_setup_inputs(seed)
    key = _jax.random.fold_in(_jax.random.key(seed), 7919)
    shape, _ = _output_shape()
    out = dict(inp)
    out["loss_target"] = _jax.random.normal(_jax.random.fold_in(key, 0), shape, _jnp.float32)
    for i, name in enumerate(TWIN_WEIGHTS):
        w = inp[name].astype(_jnp.float32)
        if MOMENT_SCALE is None:
            s = _jnp.sqrt(_jnp.mean(_jnp.square(w)) + 1e-30)
        else:
            s = MOMENT_SCALE[name]
        km, kv = _jax.random.split(_jax.random.fold_in(key, i + 1))
        out[name] = w
        out["m_" + name] = s * _jax.random.normal(km, w.shape, _jnp.float32)
        out["v_" + name] = (s * s) * _jax.random.uniform(kv, w.shape, _jnp.float32, 0.5, 1.5)
    if N_MICROBATCH > 1:
        for name, axis in PER_EXAMPLE_BATCH_AXIS.items():
            out[name] = _to_microbatches(out[name], axis)
    return {'x': out['x'], 'p': out['p'], 'norm_mix': out['norm_mix'], 'norm_ffn': out['norm_ffn'], 'norm_pl': out['norm_pl'], 'mlp_w1': out['mlp_w1'], 'mlp_w2': out['mlp_w2'], 'pl_proj': out['pl_proj'], 'pl_gate': out['pl_gate'], 'e_in_proj': out['e_in_proj'], 'e_out_proj': out['e_out_proj'], 's5_lam_re': out['s5_lam_re'], 's5_lam_im': out['s5_lam_im'], 's5_log_step': out['s5_log_step'], 's5_b_re': out['s5_b_re'], 's5_b_im': out['s5_b_im'], 's5_c_re': out['s5_c_re'], 's5_c_im': out['s5_c_im'], 's5_d': out['s5_d'], 's5_glu_w': out['s5_glu_w'], 's5_glu_b': out['s5_glu_b'], 'ssd_conv_w': out['ssd_conv_w'], 'ssd_conv_b': out['ssd_conv_b'], 'ssd_dt_bias': out['ssd_dt_bias'], 'ssd_a_log': out['ssd_a_log'], 'ssd_d': out['ssd_d'], 'ssd_norm': out['ssd_norm'], 'o_in_proj': out['o_in_proj'], 'o_out_proj': out['o_out_proj'], 'rwkv_mu': out['rwkv_mu'], 'rwkv_w0': out['rwkv_w0'], 'rwkv_w_up': out['rwkv_w_up'], 'rwkv_a0': out['rwkv_a0'], 'rwkv_a_up': out['rwkv_a_up'], 'rwkv_g_up': out['rwkv_g_up'], 'rwkv_k_k': out['rwkv_k_k'], 'rwkv_k_a': out['rwkv_k_a'], 'rwkv_r_k': out['rwkv_r_k'], 'rwkv_ln_g': out['rwkv_ln_g'], 'rwkv_ln_b': out['rwkv_ln_b'], 'lru_conv_w': out['lru_conv_w'], 'lru_conv_b': out['lru_conv_b'], 'lru_w_a': out['lru_w_a'], 'lru_b_a': out['lru_b_a'], 'lru_w_x': out['lru_w_x'], 'lru_b_x': out['lru_b_x'], 'lru_lam': out['lru_lam'], 'norm_final': out['norm_final'], 'loss_target': out['loss_target'], 'm_norm_mix': out['m_norm_mix'], 'm_norm_ffn': out['m_norm_ffn'], 'm_norm_pl': out['m_norm_pl'], 'm_mlp_w1': out['m_mlp_w1'], 'm_mlp_w2': out['m_mlp_w2'], 'm_pl_proj': out['m_pl_proj'], 'm_pl_gate': out['m_pl_gate'], 'm_e_in_proj': out['m_e_in_proj'], 'm_e_out_proj': out['m_e_out_proj'], 'm_s5_lam_re': out['m_s5_lam_re'], 'm_s5_lam_im': out['m_s5_lam_im'], 'm_s5_log_step': out['m_s5_log_step'], 'm_s5_b_re': out['m_s5_b_re'], 'm_s5_b_im': out['m_s5_b_im'], 'm_s5_c_re': out['m_s5_c_re'], 'm_s5_c_im': out['m_s5_c_im'], 'm_s5_d': out['m_s5_d'], 'm_s5_glu_w': out['m_s5_glu_w'], 'm_s5_glu_b': out['m_s5_glu_b'], 'm_ssd_conv_w': out['m_ssd_conv_w'], 'm_ssd_conv_b': out['m_ssd_conv_b'], 'm_ssd_dt_bias': out['m_ssd_dt_bias'], 'm_ssd_a_log': out['m_ssd_a_log'], 'm_ssd_d': out['m_ssd_d'], 'm_ssd_norm': out['m_ssd_norm'], 'm_o_in_proj': out['m_o_in_proj'], 'm_o_out_proj': out['m_o_out_proj'], 'm_rwkv_mu': out['m_rwkv_mu'], 'm_rwkv_w0': out['m_rwkv_w0'], 'm_rwkv_w_up': out['m_rwkv_w_up'], 'm_rwkv_a0': out['m_rwkv_a0'], 'm_rwkv_a_up': out['m_rwkv_a_up'], 'm_rwkv_g_up': out['m_rwkv_g_up'], 'm_rwkv_k_k': out['m_rwkv_k_k'], 'm_rwkv_k_a': out['m_rwkv_k_a'], 'm_rwkv_r_k': out['m_rwkv_r_k'], 'm_rwkv_ln_g': out['m_rwkv_ln_g'], 'm_rwkv_ln_b': out['m_rwkv_ln_b'], 'm_lru_conv_w': out['m_lru_conv_w'], 'm_lru_conv_b': out['m_lru_conv_b'], 'm_lru_w_a': out['m_lru_w_a'], 'm_lru_b_a': out['m_lru_b_a'], 'm_lru_w_x': out['m_lru_w_x'], 'm_lru_b_x': out['m_lru_b_x'], 'm_lru_lam': out['m_lru_lam'], 'm_norm_final': out['m_norm_final'], 'v_norm_mix': out['v_norm_mix'], 'v_norm_ffn': out['v_norm_ffn'], 'v_norm_pl': out['v_norm_pl'], 'v_mlp_w1': out['v_mlp_w1'], 'v_mlp_w2': out['v_mlp_w2'], 'v_pl_proj': out['v_pl_proj'], 'v_pl_gate': out['v_pl_gate'], 'v_e_in_proj': out['v_e_in_proj'], 'v_e_out_proj': out['v_e_out_proj'], 'v_s5_lam_re': out['v_s5_lam_re'], 'v_s5_lam_im': out['v_s5_lam_im'], 'v_s5_log_step': out['v_s5_log_step'], 'v_s5_b_re': out['v_s5_b_re'], 'v_s5_b_im': out['v_s5_b_im'], 'v_s5_c_re': out['v_s5_c_re'], 'v_s5_c_im': out['v_s5_c_im'], 'v_s5_d': out['v_s5_d'], 'v_s5_glu_w': out['v_s5_glu_w'], 'v_s5_glu_b': out['v_s5_glu_b'], 'v_ssd_conv_w': out['v_ssd_conv_w'], 'v_ssd_conv_b': out['v_ssd_conv_b'], 'v_ssd_dt_bias': out['v_ssd_dt_bias'], 'v_ssd_a_log': out['v_ssd_a_log'], 'v_ssd_d': out['v_ssd_d'], 'v_ssd_norm': out['v_ssd_norm'], 'v_o_in_proj': out['v_o_in_proj'], 'v_o_out_proj': out['v_o_out_proj'], 'v_rwkv_mu': out['v_rwkv_mu'], 'v_rwkv_w0': out['v_rwkv_w0'], 'v_rwkv_w_up': out['v_rwkv_w_up'], 'v_rwkv_a0': out['v_rwkv_a0'], 'v_rwkv_a_up': out['v_rwkv_a_up'], 'v_rwkv_g_up': out['v_rwkv_g_up'], 'v_rwkv_k_k': out['v_rwkv_k_k'], 'v_rwkv_k_a': out['v_rwkv_k_a'], 'v_rwkv_r_k': out['v_rwkv_r_k'], 'v_rwkv_ln_g': out['v_rwkv_ln_g'], 'v_rwkv_ln_b': out['v_rwkv_ln_b'], 'v_lru_conv_w': out['v_lru_conv_w'], 'v_lru_conv_b': out['v_lru_conv_b'], 'v_lru_w_a': out['v_lru_w_a'], 'v_lru_b_a': out['v_lru_b_a'], 'v_lru_w_x': out['v_lru_w_x'], 'v_lru_b_x': out['v_lru_b_x'], 'v_lru_lam': out['v_lru_lam'], 'v_norm_final': out['v_norm_final']}


def _loss(weights, diff, rest, loss_target):
    with _jax.named_scope("forward"):
        args = {**rest, TWIN_DIFF_INPUT: diff, **{k: w.astype(_WEIGHT_DTYPES[k]) for k, w in weights.items()}}
        y = _forward(args)
    with _jax.named_scope("loss_head"):
        err = _jnp.square(y.astype(_jnp.float32) - loss_target)
        return 0.5 * _jnp.sum(_jnp.mean(err, axis=-1)) if err.ndim else 0.5 * err


def _adamw(w, g, m, v):
    m = ADAM_B1 * m + (1.0 - ADAM_B1) * g
    v = ADAM_B2 * v + (1.0 - ADAM_B2) * _jnp.square(g)
    m_hat = m / (1.0 - ADAM_B1 ** ADAM_STEP)
    v_hat = v / (1.0 - ADAM_B2 ** ADAM_STEP)
    delta = -ADAM_LR * (m_hat / (_jnp.sqrt(v_hat) + ADAM_EPS) + ADAM_WD * w)
    return delta, m, v


def reference(x, p, norm_mix, norm_ffn, norm_pl, mlp_w1, mlp_w2, pl_proj, pl_gate, e_in_proj, e_out_proj, s5_lam_re, s5_lam_im, s5_log_step, s5_b_re, s5_b_im, s5_c_re, s5_c_im, s5_d, s5_glu_w, s5_glu_b, ssd_conv_w, ssd_conv_b, ssd_dt_bias, ssd_a_log, ssd_d, ssd_norm, o_in_proj, o_out_proj, rwkv_mu, rwkv_w0, rwkv_w_up, rwkv_a0, rwkv_a_up, rwkv_g_up, rwkv_k_k, rwkv_k_a, rwkv_r_k, rwkv_ln_g, rwkv_ln_b, lru_conv_w, lru_conv_b, lru_w_a, lru_b_a, lru_w_x, lru_b_x, lru_lam, norm_final, loss_target, m_norm_mix, m_norm_ffn, m_norm_pl, m_mlp_w1, m_mlp_w2, m_pl_proj, m_pl_gate, m_e_in_proj, m_e_out_proj, m_s5_lam_re, m_s5_lam_im, m_s5_log_step, m_s5_b_re, m_s5_b_im, m_s5_c_re, m_s5_c_im, m_s5_d, m_s5_glu_w, m_s5_glu_b, m_ssd_conv_w, m_ssd_conv_b, m_ssd_dt_bias, m_ssd_a_log, m_ssd_d, m_ssd_norm, m_o_in_proj, m_o_out_proj, m_rwkv_mu, m_rwkv_w0, m_rwkv_w_up, m_rwkv_a0, m_rwkv_a_up, m_rwkv_g_up, m_rwkv_k_k, m_rwkv_k_a, m_rwkv_r_k, m_rwkv_ln_g, m_rwkv_ln_b, m_lru_conv_w, m_lru_conv_b, m_lru_w_a, m_lru_b_a, m_lru_w_x, m_lru_b_x, m_lru_lam, m_norm_final, v_norm_mix, v_norm_ffn, v_norm_pl, v_mlp_w1, v_mlp_w2, v_pl_proj, v_pl_gate, v_e_in_proj, v_e_out_proj, v_s5_lam_re, v_s5_lam_im, v_s5_log_step, v_s5_b_re, v_s5_b_im, v_s5_c_re, v_s5_c_im, v_s5_d, v_s5_glu_w, v_s5_glu_b, v_ssd_conv_w, v_ssd_conv_b, v_ssd_dt_bias, v_ssd_a_log, v_ssd_d, v_ssd_norm, v_o_in_proj, v_o_out_proj, v_rwkv_mu, v_rwkv_w0, v_rwkv_w_up, v_rwkv_a0, v_rwkv_a_up, v_rwkv_g_up, v_rwkv_k_k, v_rwkv_k_a, v_rwkv_r_k, v_rwkv_ln_g, v_rwkv_ln_b, v_lru_conv_w, v_lru_conv_b, v_lru_w_a, v_lru_b_a, v_lru_w_x, v_lru_b_x, v_lru_lam, v_norm_final):
    given = dict(x=x, p=p, norm_mix=norm_mix, norm_ffn=norm_ffn, norm_pl=norm_pl, mlp_w1=mlp_w1, mlp_w2=mlp_w2, pl_proj=pl_proj, pl_gate=pl_gate, e_in_proj=e_in_proj, e_out_proj=e_out_proj, s5_lam_re=s5_lam_re, s5_lam_im=s5_lam_im, s5_log_step=s5_log_step, s5_b_re=s5_b_re, s5_b_im=s5_b_im, s5_c_re=s5_c_re, s5_c_im=s5_c_im, s5_d=s5_d, s5_glu_w=s5_glu_w, s5_glu_b=s5_glu_b, ssd_conv_w=ssd_conv_w, ssd_conv_b=ssd_conv_b, ssd_dt_bias=ssd_dt_bias, ssd_a_log=ssd_a_log, ssd_d=ssd_d, ssd_norm=ssd_norm, o_in_proj=o_in_proj, o_out_proj=o_out_proj, rwkv_mu=rwkv_mu, rwkv_w0=rwkv_w0, rwkv_w_up=rwkv_w_up, rwkv_a0=rwkv_a0, rwkv_a_up=rwkv_a_up, rwkv_g_up=rwkv_g_up, rwkv_k_k=rwkv_k_k, rwkv_k_a=rwkv_k_a, rwkv_r_k=rwkv_r_k, rwkv_ln_g=rwkv_ln_g, rwkv_ln_b=rwkv_ln_b, lru_conv_w=lru_conv_w, lru_conv_b=lru_conv_b, lru_w_a=lru_w_a, lru_b_a=lru_b_a, lru_w_x=lru_w_x, lru_b_x=lru_b_x, lru_lam=lru_lam, norm_final=norm_final, loss_target=loss_target, m_norm_mix=m_norm_mix, m_norm_ffn=m_norm_ffn, m_norm_pl=m_norm_pl, m_mlp_w1=m_mlp_w1, m_mlp_w2=m_mlp_w2, m_pl_proj=m_pl_proj, m_pl_gate=m_pl_gate, m_e_in_proj=m_e_in_proj, m_e_out_proj=m_e_out_proj, m_s5_lam_re=m_s5_lam_re, m_s5_lam_im=m_s5_lam_im, m_s5_log_step=m_s5_log_step, m_s5_b_re=m_s5_b_re, m_s5_b_im=m_s5_b_im, m_s5_c_re=m_s5_c_re, m_s5_c_im=m_s5_c_im, m_s5_d=m_s5_d, m_s5_glu_w=m_s5_glu_w, m_s5_glu_b=m_s5_glu_b, m_ssd_conv_w=m_ssd_conv_w, m_ssd_conv_b=m_ssd_conv_b, m_ssd_dt_bias=m_ssd_dt_bias, m_ssd_a_log=m_ssd_a_log, m_ssd_d=m_ssd_d, m_ssd_norm=m_ssd_norm, m_o_in_proj=m_o_in_proj, m_o_out_proj=m_o_out_proj, m_rwkv_mu=m_rwkv_mu, m_rwkv_w0=m_rwkv_w0, m_rwkv_w_up=m_rwkv_w_up, m_rwkv_a0=m_rwkv_a0, m_rwkv_a_up=m_rwkv_a_up, m_rwkv_g_up=m_rwkv_g_up, m_rwkv_k_k=m_rwkv_k_k, m_rwkv_k_a=m_rwkv_k_a, m_rwkv_r_k=m_rwkv_r_k, m_rwkv_ln_g=m_rwkv_ln_g, m_rwkv_ln_b=m_rwkv_ln_b, m_lru_conv_w=m_lru_conv_w, m_lru_conv_b=m_lru_conv_b, m_lru_w_a=m_lru_w_a, m_lru_b_a=m_lru_b_a, m_lru_w_x=m_lru_w_x, m_lru_b_x=m_lru_b_x, m_lru_lam=m_lru_lam, m_norm_final=m_norm_final, v_norm_mix=v_norm_mix, v_norm_ffn=v_norm_ffn, v_norm_pl=v_norm_pl, v_mlp_w1=v_mlp_w1, v_mlp_w2=v_mlp_w2, v_pl_proj=v_pl_proj, v_pl_gate=v_pl_gate, v_e_in_proj=v_e_in_proj, v_e_out_proj=v_e_out_proj, v_s5_lam_re=v_s5_lam_re, v_s5_lam_im=v_s5_lam_im, v_s5_log_step=v_s5_log_step, v_s5_b_re=v_s5_b_re, v_s5_b_im=v_s5_b_im, v_s5_c_re=v_s5_c_re, v_s5_c_im=v_s5_c_im, v_s5_d=v_s5_d, v_s5_glu_w=v_s5_glu_w, v_s5_glu_b=v_s5_glu_b, v_ssd_conv_w=v_ssd_conv_w, v_ssd_conv_b=v_ssd_conv_b, v_ssd_dt_bias=v_ssd_dt_bias, v_ssd_a_log=v_ssd_a_log, v_ssd_d=v_ssd_d, v_ssd_norm=v_ssd_norm, v_o_in_proj=v_o_in_proj, v_o_out_proj=v_o_out_proj, v_rwkv_mu=v_rwkv_mu, v_rwkv_w0=v_rwkv_w0, v_rwkv_w_up=v_rwkv_w_up, v_rwkv_a0=v_rwkv_a0, v_rwkv_a_up=v_rwkv_a_up, v_rwkv_g_up=v_rwkv_g_up, v_rwkv_k_k=v_rwkv_k_k, v_rwkv_k_a=v_rwkv_k_a, v_rwkv_r_k=v_rwkv_r_k, v_rwkv_ln_g=v_rwkv_ln_g, v_rwkv_ln_b=v_rwkv_ln_b, v_lru_conv_w=v_lru_conv_w, v_lru_conv_b=v_lru_conv_b, v_lru_w_a=v_lru_w_a, v_lru_b_a=v_lru_b_a, v_lru_w_x=v_lru_w_x, v_lru_b_x=v_lru_b_x, v_lru_lam=v_lru_lam, v_norm_final=v_norm_final)
    weights = {n: given[n] for n in TWIN_WEIGHTS}
    shared = {n: given[n] for n in SHARED_INPUTS}
    per_example = {n: given[n] for n in ['x', 'p']}
    grad_fn = _jax.value_and_grad(_loss, argnums=(0, 1))

    def one_microbatch(ex, loss_target):
        ex = dict(ex)
        diff = ex.pop(TWIN_DIFF_INPUT)
        return grad_fn(weights, diff, {**shared, **ex}, loss_target)

    if N_MICROBATCH == 1:
        loss, (grad_w, grad_x) = one_microbatch(per_example, given["loss_target"])
    else:
        def body(carry, xs):
            loss_sum, grad_sum = carry
            l_k, (gw_k, gx_k) = one_microbatch(xs[0], xs[1])
            with _jax.named_scope("update"):
                return (loss_sum + l_k, _jax.tree.map(_jnp.add, grad_sum, gw_k)), gx_k

        init = (_jnp.zeros((), _jnp.float32), _jax.tree.map(_jnp.zeros_like, weights))
        (loss, grad_w), grad_x = _jax.lax.scan(body, init, (per_example, given["loss_target"]))
    with _jax.named_scope("update"):
        delta_w, new_m, new_v = {}, {}, {}
        for n in TWIN_WEIGHTS:
            delta_w[n], new_m[n], new_v[n] = _adamw(weights[n], grad_w[n], given["m_" + n], given["v_" + n])
    return (loss, grad_x, *[grad_w[n] for n in TWIN_WEIGHTS], *[delta_w[n] for n in TWIN_WEIGHTS],
            *[new_m[n] for n in TWIN_WEIGHTS], *[new_v[n] for n in TWIN_WEIGHTS])
```

```python
import functools
import math

import jax
import jax.numpy as jnp
from jax import lax
from jax.experimental import pallas as pl
from jax.experimental.pallas import tpu as pltpu

F32 = jnp.float32
BF16 = jnp.bfloat16
HI = lax.Precision.HIGHEST
MESH = pl.DeviceIdType.MESH
SDS = jax.ShapeDtypeStruct
VMEM_LIMIT = 56 * 1024 * 1024
ANY = pl.BlockSpec(memory_space=pl.ANY)

D = 2048
PL_DIM = 256
D_FF = 4 * D
EPS = 1e-6
S5_W, S5_G, S5_GROUPS, S5_P = 512, 16, 32, 64
S5_N = S5_GROUPS * S5_P
SSD_W, SSD_HD, SSD_H, SSD_NG, SSD_N, SSD_L = 1536, 64, 24, 4, 128, 128
SSD_CONV = SSD_W + 2 * SSD_NG * SSD_N
EVEN_IN = S5_W + SSD_W + SSD_CONV + SSD_H
EVEN_PAD = 5120
RW_W, RW_H, RW_HD = 1024, 16, 64
RW_LORA = 96
RW_GATE = 256
RW_IN = 3 * RW_W + 2 * RW_LORA + RW_GATE
RW_PAD = 3584
LRU_W, LRU_B = 1024, 16
ODD_IN = RW_IN + 2 * LRU_W
ODD_PAD = RW_PAD + 2 * LRU_W
GN_EPS = 64e-5
LRU_C = 8.0
ADAM_LR, ADAM_B1, ADAM_B2, ADAM_EPS, ADAM_WD, ADAM_STEP = 0.001, 0.9, 0.999, 1e-08, 0.01, 10


def _cparams(sem=("arbitrary",)):
    return pltpu.CompilerParams(dimension_semantics=sem, vmem_limit_bytes=VMEM_LIMIT)


def _dot16(a, b, dims=(((1,), (0,)), ((), ()))):
    return lax.dot_general(a.astype(BF16), b.astype(BF16), dims, preferred_element_type=F32)


def _dot32(a, b, dims=(((1,), (0,)), ((), ()))):
    return lax.dot_general(a.astype(F32), b.astype(F32), dims, precision=HI, preferred_element_type=F32)


NT = (((1,), (1,)), ((), ()))
TN = (((0,), (0,)), ((), ()))


def _tile(dim, target):
    if dim <= target:
        return dim
    t = target - target % 128
    while t > 128 and dim % t:
        t -= 128
    assert dim % t == 0, (dim, target)
    return t


def _mm(a, b, *, ta=False, tb=False, add=None, out_dtype=F32, tm=512, tn=512, tk=1024, name):
    m, k = (a.shape[1], a.shape[0]) if ta else a.shape
    n = b.shape[0] if tb else b.shape[1]
    assert (b.shape[1] if tb else b.shape[0]) == k, (a.shape, b.shape, ta, tb)
    tm, tn, tk = _tile(m, tm), _tile(n, tn), _tile(k, tk)
    nk = k // tk
    dims = (((0 if ta else 1,), (1 if tb else 0,)), ((), ()))
    has_add = add is not None

    def body(*refs):
        a_ref, b_ref = refs[:2]
        o_ref, acc_ref = refs[-2:]
        kk = pl.program_id(2)

        @pl.when(kk == 0)
        def _():
            acc_ref[...] = refs[2][...].astype(F32) if has_add else jnp.zeros_like(acc_ref)

        acc_ref[...] += _dot16(a_ref[...], b_ref[...], dims)

        @pl.when(kk == nk - 1)
        def _():
            o_ref[...] = acc_ref[...].astype(o_ref.dtype)

    a_spec = pl.BlockSpec((tk, tm), lambda i, j, q: (q, i)) if ta else pl.BlockSpec((tm, tk), lambda i, j, q: (i, q))
    b_spec = pl.BlockSpec((tn, tk), lambda i, j, q: (j, q)) if tb else pl.BlockSpec((tk, tn), lambda i, j, q: (q, j))
    o_spec = pl.BlockSpec((tm, tn), lambda i, j, q: (i, j))
    return pl.pallas_call(
        body,
        grid=(m // tm, n // tn, nk),
        in_specs=[a_spec, b_spec] + ([o_spec] if has_add else []),
        out_specs=o_spec,
        out_shape=SDS((m, n), out_dtype),
        scratch_shapes=[pltpu.VMEM((tm, tn), F32)],
        compiler_params=_cparams(("parallel", "parallel", "arbitrary")),
        name=name,
    )(a, b, *([add] if has_add else []))


def _single(fn, consts, *, name):
    outs = jax.eval_shape(fn, *[SDS(c.shape, F32) for c in consts])
    n_in = len(consts)

    def body(*refs):
        res = fn(*[r[...] for r in refs[:n_in]])
        for o_ref, v in zip(refs[n_in:], res):
            o_ref[...] = v

    return pl.pallas_call(body, out_shape=[SDS(o.shape, F32) for o in outs],
                          compiler_params=pltpu.CompilerParams(vmem_limit_bytes=VMEM_LIMIT), name=name)(*consts)


def _single_vjp(fn, consts, cots, *, name):
    n_in = len(consts)

    def body(*refs):
        _, pull = jax.vjp(fn, *[r[...] for r in refs[:n_in]])
        grads = pull(tuple(r[...] for r in refs[n_in:n_in + len(cots)]))
        for o_ref, v in zip(refs[n_in + len(cots):], grads):
            o_ref[...] = v

    return pl.pallas_call(body, out_shape=[SDS(c.shape, F32) for c in consts],
                          compiler_params=pltpu.CompilerParams(vmem_limit_bytes=VMEM_LIMIT), name=name)(*consts, *cots)


def _full_spec(shape):
    nd = len(shape)
    return pl.BlockSpec(shape, lambda i, _n=nd: (0,) * _n)


def _stage_shapes(fn, rows, consts, tb, pos):
    rs = [SDS((tb, r.shape[1]), F32) for r in rows]
    cs = [SDS(c.shape, F32) for c in consts]
    f = (lambda *a: fn(jnp.int32(0), *a)) if pos else fn
    return jax.eval_shape(f, *rs, *cs)


def _stage(fn, rows, consts, *, tb, name, out_dtypes, n_acc=0, pos=False):
    t = rows[0].shape[0]
    assert t % tb == 0
    outs = _stage_shapes(fn, rows, consts, tb, pos)
    n_out = len(outs)
    n_row = n_out - n_acc
    n_in = len(rows) + len(consts)

    def body(*refs):
        i = pl.program_id(0)
        vals = [r[...].astype(F32) for r in refs[:n_in]]
        res = fn(i * tb, *vals) if pos else fn(*vals)
        out_refs = refs[n_in:]
        for q in range(n_row):
            out_refs[q][...] = res[q].astype(out_refs[q].dtype)
        for q in range(n_row, n_out):
            @pl.when(i == 0)
            def _(q=q):
                out_refs[q][...] = jnp.zeros_like(out_refs[q])

            out_refs[q][...] += res[q]

    in_specs = [pl.BlockSpec((tb, r.shape[1]), lambda i: (i, 0)) for r in rows] + [_full_spec(c.shape) for c in consts]
    out_specs = [pl.BlockSpec((tb, o.shape[1]), lambda i: (i, 0)) for o in outs[:n_row]] + [_full_spec(o.shape) for o in outs[n_row:]]
    out_shape = [SDS((t, o.shape[1]), dt) for o, dt in zip(outs[:n_row], out_dtypes)] + [SDS(o.shape, F32) for o in outs[n_row:]]
    return pl.pallas_call(
        body, grid=(t // tb,), in_specs=in_specs, out_specs=out_specs, out_shape=out_shape,
        compiler_params=_cparams(), name=name,
    )(*rows, *consts)


def _stage_vjp(fn, rows, consts, cots, *, tb, name, drow, dconst, drow_dtypes=None, acc_cots=(), pos=False):
    t = rows[0].shape[0]
    assert t % tb == 0
    n_rows, n_consts, n_cots, n_acc = len(rows), len(consts), len(cots), len(acc_cots)
    n_in = n_rows + n_consts + n_cots + n_acc
    drow_dtypes = drow_dtypes or [F32] * len(drow)

    def body(*refs):
        i = pl.program_id(0)
        vals = [r[...].astype(F32) for r in refs[:n_in]]
        rv, cv = vals[:n_rows], vals[n_rows:n_rows + n_consts]
        ct = tuple(vals[n_rows + n_consts:])

        def f(*dargs):
            r2, c2 = list(rv), list(cv)
            for q, idx in enumerate(drow):
                r2[idx] = dargs[q]
            for q, idx in enumerate(dconst):
                c2[idx] = dargs[len(drow) + q]
            return fn(i * tb, *r2, *c2) if pos else fn(*r2, *c2)

        _, pull = jax.vjp(f, *[rv[q] for q in drow], *[cv[q] for q in dconst])
        grads = pull(ct)
        out_refs = refs[n_in:]
        for q in range(len(drow)):
            out_refs[q][...] = grads[q].astype(out_refs[q].dtype)
        for q in range(len(drow), len(drow) + len(dconst)):
            @pl.when(i == 0)
            def _(q=q):
                out_refs[q][...] = jnp.zeros_like(out_refs[q])

            out_refs[q][...] += grads[q]

    in_specs = ([pl.BlockSpec((tb, r.shape[1]), lambda i: (i, 0)) for r in rows] + [_full_spec(c.shape) for c in consts]
                + [pl.BlockSpec((tb, c.shape[1]), lambda i: (i, 0)) for c in cots] + [_full_spec(c.shape) for c in acc_cots])
    out_specs = ([pl.BlockSpec((tb, rows[q].shape[1]), lambda i: (i, 0)) for q in drow]
                 + [_full_spec(consts[q].shape) for q in dconst])
    out_shape = ([SDS(rows[q].shape, dt) for q, dt in zip(drow, drow_dtypes)]
                 + [SDS(consts[q].shape, F32) for q in dconst])
    return pl.pallas_call(
        body, grid=(t // tb,), in_specs=in_specs, out_specs=out_specs, out_shape=out_shape,
        compiler_params=_cparams(), name=name,
    )(*rows, *consts, *cots, *acc_cots)


def _conv_fwd(x, w, b, *, tb, name):
    t, c = x.shape
    r8 = tb // 8

    def body(x_ref, p_ref, w_ref, b_ref, o_ref):
        i = pl.program_id(0)
        x_ = x_ref[...]
        p_ = jnp.where(i > 0, p_ref[...], 0.0)
        w_ = w_ref[...]
        row = lax.broadcasted_iota(jnp.int32, x_.shape, 0)
        row8 = lax.broadcasted_iota(jnp.int32, p_.shape, 0)
        acc = x_ * w_[3:4, :] + b_ref[...]
        head = jnp.zeros_like(p_)
        for j in (1, 2, 3):
            wj = w_[3 - j:4 - j, :]
            acc += jnp.where(row >= j, pltpu.roll(x_, j, 0), 0.0) * wj
            head += jnp.where(row8 < j, pltpu.roll(p_, j, 0), 0.0) * wj
        o_ref[...] = acc
        o_ref[0:8, :] += head

    return pl.pallas_call(
        body, grid=(t // tb,),
        in_specs=[pl.BlockSpec((tb, c), lambda i: (i, 0)),
                  pl.BlockSpec((8, c), lambda i: (jnp.maximum(i * r8 - 1, 0), 0)),
                  _full_spec(w.shape), _full_spec(b.shape)],
        out_specs=pl.BlockSpec((tb, c), lambda i: (i, 0)),
        out_shape=SDS((t, c), F32), compiler_params=_cparams(), name=name,
    )(x, x, w, b)


def _conv_bwd(x, w, dy, *, tb, name):
    t, c = x.shape
    r8 = tb // 8
    nb = t // tb

    def body(x_ref, p_ref, w_ref, g_ref, n_ref, dx_ref, dw_ref, db_ref):
        i = pl.program_id(0)
        x_ = x_ref[...]
        p_ = jnp.where(i > 0, p_ref[...], 0.0)
        g_ = g_ref[...]
        n_ = jnp.where(i < nb - 1, n_ref[...], 0.0)
        w_ = w_ref[...]
        row = lax.broadcasted_iota(jnp.int32, x_.shape, 0)
        row8 = lax.broadcasted_iota(jnp.int32, p_.shape, 0)
        g8 = g_[0:8, :]
        dx = g_ * w_[3:4, :]
        tail = jnp.zeros_like(n_)
        dws = [jnp.sum(g_ * x_, axis=0, keepdims=True)]
        for j in (1, 2, 3):
            wj = w_[3 - j:4 - j, :]
            dx += jnp.where(row < tb - j, pltpu.roll(g_, tb - j, 0), 0.0) * wj
            tail += jnp.where(row8 >= 8 - j, pltpu.roll(n_, 8 - j, 0), 0.0) * wj
            xs = jnp.where(row >= j, pltpu.roll(x_, j, 0), 0.0)
            ps = jnp.where(row8 < j, pltpu.roll(p_, j, 0), 0.0)
            dws.append(jnp.sum(g_ * xs, axis=0, keepdims=True) + jnp.sum(g8 * ps, axis=0, keepdims=True))
        dx_ref[...] = dx
        dx_ref[tb - 8:tb, :] += tail

        @pl.when(i == 0)
        def _():
            dw_ref[...] = jnp.zeros_like(dw_ref)
            db_ref[...] = jnp.zeros_like(db_ref)

        for j in range(4):
            dw_ref[3 - j:4 - j, :] += dws[j]
        db_ref[...] += jnp.sum(g_, axis=0, keepdims=True)

    return pl.pallas_call(
        body, grid=(nb,),
        in_specs=[pl.BlockSpec((tb, c), lambda i: (i, 0)),
                  pl.BlockSpec((8, c), lambda i: (jnp.maximum(i * r8 - 1, 0), 0)),
                  _full_spec(w.shape),
                  pl.BlockSpec((tb, c), lambda i: (i, 0)),
                  pl.BlockSpec((8, c), lambda i: (jnp.minimum((i + 1) * r8, t // 8 - 1), 0))],
        out_specs=[pl.BlockSpec((tb, c), lambda i: (i, 0)), _full_spec((8, c)), _full_spec((1, c))],
        out_shape=[SDS((t, c), F32), SDS((8, c), F32), SDS((1, c), F32)],
        compiler_params=_cparams(), name=name,
    )(x, x, w, dy, dy)


def _lru_scan_fwd(a, b, *, tb, name):
    t, c = a.shape

    def body(a_ref, b_ref, h_ref, st_ref):
        @pl.when(pl.program_id(0) == 0)
        def _():
            st_ref[...] = jnp.zeros_like(st_ref)

        def step(s, h):
            h = a_ref[pl.ds(s, 1), :] * h + b_ref[pl.ds(s, 1), :]
            h_ref[pl.ds(s, 1), :] = h
            return h

        st_ref[...] = lax.fori_loop(0, tb, step, st_ref[...], unroll=8)

    blk = pl.BlockSpec((tb, c), lambda i: (i, 0))
    return pl.pallas_call(
        body, grid=(t // tb,), in_specs=[blk, blk], out_specs=blk, out_shape=SDS((t, c), F32),
        scratch_shapes=[pltpu.VMEM((1, c), F32)], compiler_params=_cparams(), name=name,
    )(a, b)


def _lru_scan_bwd(a, h, dh, *, tb, name):
    t, c = a.shape
    nb = t // tb
    r8 = tb // 8

    def body(a_ref, h_ref, p_ref, g_ref, da_ref, db_ref, st_ref):
        i = pl.program_id(0)

        @pl.when(i == 0)
        def _():
            st_ref[...] = jnp.zeros_like(st_ref)

        hprev0 = jnp.where(i < nb - 1, p_ref[7:8, :], 0.0)

        def step(q, carry):
            s = tb - 1 - q
            g = g_ref[pl.ds(s, 1), :] + carry
            hp = h_ref[pl.ds(jnp.maximum(s - 1, 0), 1), :]
            hp = jnp.where(s > 0, hp, hprev0)
            db_ref[pl.ds(s, 1), :] = g
            da_ref[pl.ds(s, 1), :] = g * hp
            return a_ref[pl.ds(s, 1), :] * g

        st_ref[...] = lax.fori_loop(0, tb, step, st_ref[...], unroll=8)

    rev = pl.BlockSpec((tb, c), lambda i: (nb - 1 - i, 0))
    prev = pl.BlockSpec((8, c), lambda i: (jnp.maximum((nb - 1 - i) * r8 - 1, 0), 0))
    return pl.pallas_call(
        body, grid=(nb,), in_specs=[rev, rev, prev, rev], out_specs=[rev, rev],
        out_shape=[SDS((t, c), F32), SDS((t, c), F32)],
        scratch_shapes=[pltpu.VMEM((1, c), F32)], compiler_params=_cparams(), name=name,
    )(a, h, h, dh)


def _s5_scan_fwd(ar, ai, br, bi, *, tb, name):
    t, c = br.shape

    def body(ar_ref, ai_ref, br_ref, bi_ref, xr_ref, xi_ref, sr_ref, si_ref):
        @pl.when(pl.program_id(0) == 0)
        def _():
            sr_ref[...] = jnp.zeros_like(sr_ref)
            si_ref[...] = jnp.zeros_like(si_ref)

        ar_, ai_ = ar_ref[...], ai_ref[...]

        def step(s, carry):
            xr, xi = carry
            nr = ar_ * xr - ai_ * xi + br_ref[pl.ds(s, 1), :]
            ni = ar_ * xi + ai_ * xr + bi_ref[pl.ds(s, 1), :]
            xr_ref[pl.ds(s, 1), :] = nr
            xi_ref[pl.ds(s, 1), :] = ni
            return nr, ni

        xr, xi = lax.fori_loop(0, tb, step, (sr_ref[...], si_ref[...]), unroll=8)
        sr_ref[...] = xr
        si_ref[...] = xi

    blk = pl.BlockSpec((tb, c), lambda i: (i, 0))
    one = _full_spec((1, c))
    return pl.pallas_call(
        body, grid=(t // tb,), in_specs=[one, one, blk, blk], out_specs=[blk, blk],
        out_shape=[SDS((t, c), F32), SDS((t, c), F32)],
        scratch_shapes=[pltpu.VMEM((1, c), F32), pltpu.VMEM((1, c), F32)], compiler_params=_cparams(), name=name,
    )(ar, ai, br, bi)


def _s5_scan_bwd(ar, ai, xr, xi, dxr, dxi, *, tb, name):
    t, c = xr.shape
    nb = t // tb
    r8 = tb // 8

    def body(ar_ref, ai_ref, xr_ref, xi_ref, pr_ref, pi_ref, gr_ref, gi_ref,
             dbr_ref, dbi_ref, dar_ref, dai_ref, cr_ref, ci_ref):
        i = pl.program_id(0)

        @pl.when(i == 0)
        def _():
            cr_ref[...] = jnp.zeros_like(cr_ref)
            ci_ref[...] = jnp.zeros_like(ci_ref)
            dar_ref[...] = jnp.zeros_like(dar_ref)
            dai_ref[...] = jnp.zeros_like(dai_ref)

        ar_, ai_ = ar_ref[...], ai_ref[...]
        first = i == nb - 1
        pr0 = jnp.where(first, 0.0, pr_ref[7:8, :])
        pi0 = jnp.where(first, 0.0, pi_ref[7:8, :])

        def step(q, carry):
            cr, ci, dar, dai = carry
            s = tb - 1 - q
            gr = gr_ref[pl.ds(s, 1), :] + cr
            gi = gi_ref[pl.ds(s, 1), :] + ci
            sp = jnp.maximum(s - 1, 0)
            xpr = jnp.where(s > 0, xr_ref[pl.ds(sp, 1), :], pr0)
            xpi = jnp.where(s > 0, xi_ref[pl.ds(sp, 1), :], pi0)
            dbr_ref[pl.ds(s, 1), :] = gr
            dbi_ref[pl.ds(s, 1), :] = gi
            dar = dar + gr * xpr + gi * xpi
            dai = dai - gr * xpi + gi * xpr
            return ar_ * gr + ai_ * gi, ar_ * gi - ai_ * gr, dar, dai

        cr, ci, dar, dai = lax.fori_loop(0, tb, step, (cr_ref[...], ci_ref[...], dar_ref[...], dai_ref[...]), unroll=8)
        cr_ref[...] = cr
        ci_ref[...] = ci
        dar_ref[...] = dar
        dai_ref[...] = dai

    rev = pl.BlockSpec((tb, c), lambda i: (nb - 1 - i, 0))
    prev = pl.BlockSpec((8, c), lambda i: (jnp.maximum((nb - 1 - i) * r8 - 1, 0), 0))
    one = _full_spec((1, c))
    return pl.pallas_call(
        body, grid=(nb,), in_specs=[one, one, rev, rev, prev, prev, rev, rev], out_specs=[rev, rev, one, one],
        out_shape=[SDS((t, c), F32), SDS((t, c), F32), SDS((1, c), F32), SDS((1, c), F32)],
        scratch_shapes=[pltpu.VMEM((1, c), F32), pltpu.VMEM((1, c), F32)], compiler_params=_cparams(), name=name,
    )(ar, ai, xr, xi, xr, xi, dxr, dxi)


RW_PAIRS = RW_H // 2


def _pair_consts():
    sub = lax.broadcasted_iota(jnp.int32, (64, 128), 0)
    lane = lax.broadcasted_iota(jnp.int32, (64, 128), 1)
    eye2 = ((lane & 63) == sub).astype(F32)
    r2 = lax.broadcasted_iota(jnp.int32, (128, 128), 0)
    c2 = lax.broadcasted_iota(jnp.int32, (128, 128), 1)
    bsel = ((r2 >> 6) == (c2 >> 6)).astype(BF16)
    return eye2, bsel


def _segsum(x, bsel):
    bits = lax.bitcast_convert_type(x, jnp.int32)
    hi = lax.bitcast_convert_type(bits & jnp.int32(-65536), F32)
    lo = (x - hi).astype(BF16)
    return (jnp.dot(hi.astype(BF16), bsel, preferred_element_type=F32) + jnp.dot(lo, bsel, preferred_element_type=F32))


def _put_row(ref, s, sl, val):
    base = pl.multiple_of((s >> 3) * 8, 8)
    sub = lax.broadcasted_iota(jnp.int32, (8, 128), 0)
    ref[pl.ds(base, 8), sl] = jnp.where(sub == (s & 7), val, ref[pl.ds(base, 8), sl])


def _rwkv_step(s_prev, wr, kr, vr, kkr, ar, eye2, bsel):
    sa = -_segsum(s_prev * kkr, bsel)
    vb = _segsum(eye2 * vr, bsel)
    return s_prev * wr + sa * (kkr * ar) + vb * kr, sa, vb


def _rwkv_scan_fwd(r, w, k, v, kk, a, *, lc, name):
    t = r.shape[0]
    nc = t // lc

    def body(r_ref, w_ref, k_ref, v_ref, kk_ref, a_ref, y_ref, ck_ref, st_ref, row_ref):
        @pl.when(pl.program_id(0) == 0)
        def _():
            st_ref[...] = jnp.zeros_like(st_ref)

        ck_ref[0] = st_ref[...]
        y_ref[...] = jnp.zeros_like(y_ref)
        eye2, bsel = _pair_consts()
        srcs = (r_ref, w_ref, k_ref, v_ref, kk_ref, a_ref)

        def step(s, carry):
            for n, ref in enumerate(srcs):
                row_ref[n:n + 1, :] = ref[pl.ds(s, 1), :]
            for q in range(RW_PAIRS):
                sl = slice(q * 128, (q + 1) * 128)
                rd = lambda n: row_ref[n:n + 1, sl]
                sn, _, _ = _rwkv_step(st_ref[q], rd(1), rd(2), rd(3), rd(4), rd(5), eye2, bsel)
                st_ref[q] = sn
                yb = _segsum(sn * rd(0), bsel)
                _put_row(y_ref, s, sl, jnp.sum(eye2 * yb, axis=0, keepdims=True))
            return carry

        lax.fori_loop(0, lc, step, 0)

    blk = pl.BlockSpec((lc, RW_W), lambda i: (i, 0))
    return pl.pallas_call(
        body, grid=(nc,), in_specs=[blk] * 6,
        out_specs=[blk, pl.BlockSpec((1, RW_PAIRS, 64, 128), lambda i: (i, 0, 0, 0))],
        out_shape=[SDS((t, RW_W), F32), SDS((nc, RW_PAIRS, 64, 128), F32)],
        scratch_shapes=[pltpu.VMEM((RW_PAIRS, 64, 128), F32), pltpu.VMEM((8, RW_W), F32)],
        compiler_params=_cparams(), name=name,
    )(r, w, k, v, kk, a)


def _rwkv_scan_bwd(r, w, k, v, kk, a, ck, dy, *, lc, name):
    t = r.shape[0]
    nc = t // lc

    def body(r_ref, w_ref, k_ref, v_ref, kk_ref, a_ref, ck_ref, dy_ref,
             dr_ref, dw_ref, dk_ref, dv_ref, dkk_ref, da_ref, hist_ref, ds_ref, row_ref):
        @pl.when(pl.program_id(0) == 0)
        def _():
            ds_ref[...] = jnp.zeros_like(ds_ref)

        eye2, bsel = _pair_consts()
        hist_ref[0] = ck_ref[0]
        for o_ref in (dr_ref, dw_ref, dk_ref, dv_ref, dkk_ref, da_ref):
            o_ref[...] = jnp.zeros_like(o_ref)
        srcs = (r_ref, w_ref, k_ref, v_ref, kk_ref, a_ref, dy_ref)

        def stage_rows(s):
            for n, ref in enumerate(srcs):
                row_ref[n:n + 1, :] = ref[pl.ds(s, 1), :]

        def fwd(s, carry):
            stage_rows(s)
            for q in range(RW_PAIRS):
                sl = slice(q * 128, (q + 1) * 128)
                rd = lambda n: row_ref[n:n + 1, sl]
                sn, _, _ = _rwkv_step(hist_ref[s, q], rd(1), rd(2), rd(3), rd(4), rd(5), eye2, bsel)
                hist_ref[s + 1, q] = sn
            return carry

        lax.fori_loop(0, lc, fwd, 0)

        def bwd(j, carry):
            s = lc - 1 - j
            stage_rows(s)
            for q in range(RW_PAIRS):
                sl = slice(q * 128, (q + 1) * 128)
                rd = lambda n: row_ref[n:n + 1, sl]
                rr, wr, kr, vr, kkr, ar = rd(0), rd(1), rd(2), rd(3), rd(4), rd(5)
                s_prev, s_cur = hist_ref[s, q], hist_ref[s + 1, q]
                colsum = lambda z: jnp.sum(z, axis=0, keepdims=True)
                dyb = _segsum(eye2 * rd(6), bsel)
                d_s = ds_ref[q] + dyb * rr
                _put_row(dr_ref, s, sl, colsum(s_cur * dyb))
                sa = -_segsum(s_prev * kkr, bsel)
                vb = _segsum(eye2 * vr, bsel)
                dsa = _segsum(d_s * (kkr * ar), bsel)
                _put_row(dw_ref, s, sl, colsum(d_s * s_prev))
                db = colsum(d_s * sa)
                _put_row(dv_ref, s, sl, colsum(eye2 * _segsum(d_s * kr, bsel)))
                _put_row(dk_ref, s, sl, colsum(d_s * vb))
                _put_row(dkk_ref, s, sl, db * ar - colsum(s_prev * dsa))
                _put_row(da_ref, s, sl, db * kkr)
                ds_ref[q] = d_s * wr - dsa * kkr
            return carry

        lax.fori_loop(0, lc, bwd, 0)

    rev = pl.BlockSpec((lc, RW_W), lambda i: (nc - 1 - i, 0))
    return pl.pallas_call(
        body, grid=(nc,),
        in_specs=[rev] * 6 + [pl.BlockSpec((1, RW_PAIRS, 64, 128), lambda i: (nc - 1 - i, 0, 0, 0)), rev],
        out_specs=[rev] * 6, out_shape=[SDS((t, RW_W), F32)] * 6,
        scratch_shapes=[pltpu.VMEM((lc + 1, RW_PAIRS, 64, 128), F32), pltpu.VMEM((RW_PAIRS, 64, 128), F32),
                        pltpu.VMEM((8, RW_W), F32)],
        compiler_params=_cparams(), name=name,
    )(r, w, k, v, kk, a, ck, dy)


SSD_PAIRS = SSD_H // 2


def _ssd_chunk(states, xdt, da, bm, cm):
    ln = SSD_L
    row = lax.broadcasted_iota(jnp.int32, (ln, ln), 0)
    col = lax.broadcasted_iota(jnp.int32, (ln, ln), 1)
    causal = row >= col
    acum = _dot32(causal.astype(F32), da)
    acum_t = _dot32(da, (row <= col).astype(F32), TN)
    sub = lax.broadcasted_iota(jnp.int32, (128, 128), 0)
    lane = lax.broadcasted_iota(jnp.int32, (128, 128), 1)
    ys, new_states = [], []
    for q in range(SSD_PAIRS):
        g = q // (SSD_PAIRS // SSD_NG)
        bg = bm[:, g * SSD_N:(g + 1) * SSD_N]
        cg = cm[:, g * SSD_N:(g + 1) * SSD_N]
        xq = xdt[:, q * 128:(q + 1) * 128]
        scores = _dot16(cg, bg, NT)
        aexp = _dot32(acum, (sub == 2 * q + (lane >> 6)).astype(F32))
        tot = aexp[ln - 1:ln, :]
        yh = []
        for h in (2 * q, 2 * q + 1):
            seg = _dot32(acum, (sub == h).astype(F32)) - acum_t[h:h + 1, :]
            yh.append(_dot16(scores * jnp.exp(jnp.where(causal, seg, -1e30)), xq))
        y = jnp.where(lane < 64, yh[0], yh[1]) + _dot16(cg, states[q]) * jnp.exp(aexp)
        new = _dot16(bg, xq * jnp.exp(tot - aexp), TN)
        ys.append(y)
        new_states.append(states[q] * jnp.exp(tot) + new)
    return jnp.concatenate(ys, axis=1), new_states


def _ssd_fwd(xdt, da, bm, cm, *, name):
    t = xdt.shape[0]
    nc = t // SSD_L

    def body(x_ref, a_ref, b_ref, c_ref, y_ref, ck_ref, st_ref):
        @pl.when(pl.program_id(0) == 0)
        def _():
            st_ref[...] = jnp.zeros_like(st_ref)

        ck_ref[0] = st_ref[...]
        y, new = _ssd_chunk([st_ref[q] for q in range(SSD_PAIRS)], x_ref[...], a_ref[...], b_ref[...], c_ref[...])
        y_ref[...] = y
        for q in range(SSD_PAIRS):
            st_ref[q] = new[q]

    blk = lambda wd: pl.BlockSpec((SSD_L, wd), lambda i: (i, 0))
    return pl.pallas_call(
        body, grid=(nc,), in_specs=[blk(SSD_W), blk(128), blk(512), blk(512)],
        out_specs=[blk(SSD_W), pl.BlockSpec((1, SSD_PAIRS, 128, 128), lambda i: (i, 0, 0, 0))],
        out_shape=[SDS((t, SSD_W), F32), SDS((nc, SSD_PAIRS, 128, 128), F32)],
        scratch_shapes=[pltpu.VMEM((SSD_PAIRS, 128, 128), F32)], compiler_params=_cparams(), name=name,
    )(xdt, da, bm, cm)


def _ssd_bwd(xdt, da, bm, cm, ck, dy, *, name):
    t = xdt.shape[0]
    nc = t // SSD_L

    def body(x_ref, a_ref, b_ref, c_ref, ck_ref, dy_ref, dx_ref, dda_ref, db_ref, dc_ref, ds_ref):
        @pl.when(pl.program_id(0) == 0)
        def _():
            ds_ref[...] = jnp.zeros_like(ds_ref)

        _, pull = jax.vjp(_ssd_chunk, [ck_ref[0, q] for q in range(SSD_PAIRS)], x_ref[...], a_ref[...], b_ref[...], c_ref[...])
        dst, dx, dda, db, dc = pull((dy_ref[...], [ds_ref[q] for q in range(SSD_PAIRS)]))
        dx_ref[...] = dx
        dda_ref[...] = dda
        db_ref[...] = db
        dc_ref[...] = dc
        for q in range(SSD_PAIRS):
            ds_ref[q] = dst[q]

    rev = lambda wd: pl.BlockSpec((SSD_L, wd), lambda i: (nc - 1 - i, 0))
    return pl.pallas_call(
        body, grid=(nc,),
        in_specs=[rev(SSD_W), rev(128), rev(512), rev(512),
                  pl.BlockSpec((1, SSD_PAIRS, 128, 128), lambda i: (nc - 1 - i, 0, 0, 0)), rev(SSD_W)],
        out_specs=[rev(SSD_W), rev(128), rev(512), rev(512)],
        out_shape=[SDS((t, SSD_W), F32), SDS((t, 128), F32), SDS((t, 512), F32), SDS((t, 512), F32)],
        scratch_shapes=[pltpu.VMEM((SSD_PAIRS, 128, 128), F32)], compiler_params=_cparams(), name=name,
    )(xdt, da, bm, cm, ck, dy)


def _iota(shape, dim):
    return lax.broadcasted_iota(jnp.int32, shape, dim)


def _rms(x, g):
    return x * lax.rsqrt(jnp.mean(x * x, axis=-1, keepdims=True) + EPS) * g


def _head_sel(width, shift):
    return ((_iota((width, 128), 0) >> shift) == _iota((width, 128), 1)).astype(F32)


def _head_sum(x, shift=6):
    sel = _head_sel(x.shape[1], shift)
    return _dot32(_dot32(x, sel), sel, NT)


def _head_expand(x, width, shift=6):
    return _dot32(x, _head_sel(width, shift), NT)


def f_norm(h, g):
    return (_rms(h, g),)


def f_norm_pass(h, g):
    return _rms(h, g), h


def f_add_norm(h, m, g):
    h1 = h + m
    return h1, _rms(h1, g)


def f_relu2(u):
    r = jnp.maximum(u, 0.0)
    return (r * r,)


def f_plgate(h2, gl, pp):
    return (h2 + jax.nn.sigmoid(gl) * pp,)


def f_loss(h, tgt, g):
    err = _rms(h, g) - tgt
    part = 0.5 * jnp.sum(jnp.mean(err * err, axis=-1, keepdims=True), axis=0, keepdims=True)
    return (jnp.broadcast_to(part, (8, 128)),)


def f_s5_prep(lam_re, lam_im, lstep, bre_t, bim_t, cre_t, cim_t):
    step = jnp.exp(_dot32(lstep, _head_sel(S5_N, 6), NT)[0:1, :])
    mag = jnp.exp(lam_re * step)
    abar_re, abar_im = mag * jnp.cos(lam_im * step), mag * jnp.sin(lam_im * step)
    den = lam_re * lam_re + lam_im * lam_im
    nr = abar_re - 1.0
    coef_re = (nr * lam_re + abar_im * lam_im) / den
    coef_im = (abar_im * lam_re - nr * lam_im) / den
    bbar_re = coef_re * bre_t - coef_im * bim_t
    bbar_im = coef_re * bim_t + coef_im * bre_t
    rep = ((_iota((S5_W, S5_G), 0) & (S5_G - 1)) == _iota((S5_W, S5_G), 1)).astype(F32)
    blk = ((_iota((S5_W, S5_N), 0) >> 4) == (_iota((S5_W, S5_N), 1) >> 6)).astype(F32)
    blk_t = ((_iota((S5_N, S5_W), 0) >> 6) == (_iota((S5_N, S5_W), 1) >> 4)).astype(F32)
    wb_re, wb_im = _dot32(rep, bbar_re) * blk, _dot32(rep, bbar_im) * blk
    wc_re, wc_im = _dot32(cre_t, rep, NT) * blk_t, _dot32(cim_t, rep, NT) * blk_t
    return abar_re, abar_im, wb_re, wb_im, wc_re, wc_im


def f_s5_post(xr, xi, u, wc_re, wc_im, d_skip, glu_w, glu_b):
    y = _dot16(xr, wc_re) - _dot16(xi, wc_im) + d_skip * u
    act = jax.nn.gelu(y)
    return (act * jax.nn.sigmoid(_dot16(act, glu_w) + glu_b),)


def f_ssd_pre(xc, dtr, dt_bias, a_log):
    act = jax.nn.silu(xc)
    heads = _iota(dtr.shape, 1) < SSD_H
    dt = jnp.where(heads, jax.nn.softplus(dtr + dt_bias), 0.0)
    da = dt * (-jnp.exp(a_log))
    xdt = act[:, :SSD_W] * _head_expand(dt, SSD_W)
    return xdt, da, act[:, SSD_W:SSD_W + 512], act[:, SSD_W + 512:]


def f_ssd_pre_pass(xc, dtr, dt_bias, a_log):
    return f_ssd_pre(xc, dtr, dt_bias, a_log) + (xc,)


def f_ssd_post(y, xc, z, d_skip, norm_g):
    xs = jax.nn.silu(xc[:, :SSD_W])
    y = (y + xs * _head_expand(d_skip, SSD_W)) * jax.nn.silu(z)
    gw = SSD_W // SSD_NG
    parts = []
    for g in range(SSD_NG):
        seg = y[:, g * gw:(g + 1) * gw]
        parts.append(seg * lax.rsqrt(jnp.mean(seg * seg, axis=-1, keepdims=True) + EPS))
    return (jnp.concatenate(parts, axis=1) * norm_g,)


def f_rwkv_pre(f, w0, w_up, a0, a_up, g_up, k_k, k_a):
    r, k, v = f[:, 0:1024], f[:, 1024:2048], f[:, 2048:3072]
    wl, al, gl = f[:, 3072:3200], f[:, 3200:3328], f[:, 3328:3584]
    w = -jax.nn.softplus(-(w0 + _dot16(jnp.tanh(wl), w_up))) - 0.5
    decay = jnp.exp(-jnp.exp(w))
    a = jax.nn.sigmoid(a0 + _dot16(al, a_up))
    g = _dot16(jax.nn.sigmoid(gl), g_up)
    kk = k * k_k
    k2 = k * (1.0 + (a - 1.0) * k_a)
    kkn = kk * lax.rsqrt(jnp.maximum(_head_sum(kk * kk), 1e-24))
    return r, decay, k2, v, kkn, a, g


def f_rwkv_pre_pass(f, w0, w_up, a0, a_up, g_up, k_k, k_a):
    out = f_rwkv_pre(f, w0, w_up, a0, a_up, g_up, k_k, k_a)
    return out + (out[0], out[2], out[3])


def f_rwkv_post(y, r, k2, v, g, ln_g, ln_b, r_k):
    mean = _head_sum(y) * (1.0 / RW_HD)
    yc = y - mean
    var = _head_sum(yc * yc) * (1.0 / RW_HD)
    yn = yc * lax.rsqrt(var + GN_EPS) * ln_g + ln_b
    bonus = _head_sum(r * k2 * r_k) * v
    return ((yn + bonus) * g,)


def _neg_expm1(y):
    series = -y * (1.0 + y * (0.5 + y * (1.0 / 6.0 + y * (1.0 / 24.0 + y * (1.0 / 120.0)))))
    return jnp.where(y > -0.1, series, 1.0 - jnp.exp(y))


def f_lru_pre(t0, xc, w_a, b_a, w_x, b_x, lam):
    gate_r = jax.nn.sigmoid(_dot16(xc, w_a) + b_a)
    gate_i = jax.nn.sigmoid(_dot16(xc, w_x) + b_x)
    log_a = -LRU_C * gate_r * jax.nn.softplus(-lam)
    mult = jnp.sqrt(jnp.maximum(_neg_expm1(2.0 * log_a), 0.0))
    mult = jnp.where(_iota(xc.shape, 0) + t0 == 0, 1.0, mult)
    return jnp.exp(log_a), xc * gate_i * mult


def f_lru_post(h, gl):
    return (h * jax.nn.gelu(gl),)


TB = 256
TBH = 128
SCAN_TB = 256
RW_LC = 32


def _even_fwd(hn, w, tag):
    n = lambda s: f"{tag}_{s}"
    u = _mm(hn, w["in_u"], name=n("proj_u"))
    z = _mm(hn, w["in_z"], name=n("proj_z"))
    xbc = _mm(hn, w["in_xbc"], name=n("proj_xbc"))
    dtr = _mm(hn, w["in_dt"], name=n("proj_dt"))
    bu_re = _mm(u, w["wb_re"], name=n("s5_bu_re"))
    bu_im = _mm(u, w["wb_im"], name=n("s5_bu_im"))
    xr, xi = _s5_scan_fwd(w["abar_re"], w["abar_im"], bu_re, bu_im, tb=SCAN_TB, name=n("s5_scan"))
    s5c = [w["wc_re"], w["wc_im"], w["s5_d"], w["glu_w"], w["glu_b"]]
    (ya,) = _stage(f_s5_post, [xr, xi, u], s5c, tb=TB, name=n("s5_post"), out_dtypes=[BF16])
    xc = _conv_fwd(xbc, w["ssd_conv_w"], w["ssd_conv_b"], tb=TB, name=n("ssd_conv"))
    xdt, da, bm, cm = _stage(f_ssd_pre, [xc, dtr], [w["dt_bias"], w["a_log"]], tb=TB, name=n("ssd_pre"),
                             out_dtypes=[F32] * 4)
    y, ck = _ssd_fwd(xdt, da, bm, cm, name=n("ssd_scan"))
    (yb,) = _stage(f_ssd_post, [y, xc, z], [w["ssd_d"], w["ssd_norm"]], tb=TB, name=n("ssd_post"), out_dtypes=[BF16])
    mo = _mm(ya, w["out_a"], name=n("out_a"))
    mo = _mm(yb, w["out_b"], add=mo, name=n("out_b"))
    res = dict(u=u, z=z, xbc=xbc, dtr=dtr, xr=xr, xi=xi, ya=ya, xc=xc, xdt=xdt, da=da, bm=bm, cm=cm, y=y, ck=ck, yb=yb)
    return mo, res


def _even_bwd(dmo, hn, w, r, tag):
    n = lambda s: f"{tag}_{s}"
    g = {}
    g["out_a"] = _mm(r["ya"], dmo, ta=True, name=n("d_out_a"))
    g["out_b"] = _mm(r["yb"], dmo, ta=True, name=n("d_out_b"))
    dya = _mm(dmo, w["out_a"], tb=True, name=n("dya"))
    dyb = _mm(dmo, w["out_b"], tb=True, name=n("dyb"))
    dy, dxc1, dz, g["ssd_d"], g["ssd_norm"] = _stage_vjp(
        f_ssd_post, [r["y"], r["xc"], r["z"]], [w["ssd_d"], w["ssd_norm"]], [dyb], tb=TBH, name=n("ssd_post_b"),
        drow=[0, 1, 2], dconst=[0, 1])
    dxdt, dda, dbm, dcm = _ssd_bwd(r["xdt"], r["da"], r["bm"], r["cm"], r["ck"], dy, name=n("ssd_scan_b"))
    dxc, ddtr, g["dt_bias"], g["a_log"] = _stage_vjp(
        f_ssd_pre_pass, [r["xc"], r["dtr"]], [w["dt_bias"], w["a_log"]], [dxdt, dda, dbm, dcm, dxc1], tb=TBH,
        name=n("ssd_pre_b"), drow=[0, 1], dconst=[0, 1])
    dxbc, g["ssd_conv_w"], g["ssd_conv_b"] = _conv_bwd(r["xbc"], w["ssd_conv_w"], dxc, tb=TB, name=n("ssd_conv_b"))
    s5c = [w["wc_re"], w["wc_im"], w["s5_d"], w["glu_w"], w["glu_b"]]
    dxr, dxi, du1, g["wc_re"], g["wc_im"], g["s5_d"], g["glu_w"], g["glu_b"] = _stage_vjp(
        f_s5_post, [r["xr"], r["xi"], r["u"]], s5c, [dya], tb=TBH, name=n("s5_post_b"),
        drow=[0, 1, 2], dconst=[0, 1, 2, 3, 4])
    dbr, dbi, g["abar_re"], g["abar_im"] = _s5_scan_bwd(w["abar_re"], w["abar_im"], r["xr"], r["xi"], dxr, dxi,
                                                         tb=SCAN_TB, name=n("s5_scan_b"))
    g["wb_re"] = _mm(r["u"], dbr, ta=True, name=n("d_wb_re"))
    g["wb_im"] = _mm(r["u"], dbi, ta=True, name=n("d_wb_im"))
    du = _mm(dbr, w["wb_re"], tb=True, add=du1, name=n("du_re"))
    du = _mm(dbi, w["wb_im"], tb=True, add=du, name=n("du_im"))
    segs = (("in_u", du), ("in_z", dz), ("in_xbc", dxbc), ("in_dt", ddtr))
    dhn = None
    for key, dseg in segs:
        g[key] = _mm(hn, dseg, ta=True, name=n("d_" + key))
        dhn = _mm(dseg, w[key], tb=True, add=dhn, name=n("dhn_" + key))
    return dhn, g


def _odd_fwd(hn, w, tag):
    n = lambda s: f"{tag}_{s}"
    rw = _mm(hn, w["in_rw"], name=n("proj_rw"))
    xl = _mm(hn, w["in_xl"], name=n("proj_xl"))
    gl = _mm(hn, w["in_gl"], name=n("proj_gl"))
    f = _conv_fwd(rw, w["mix_w"], w["mix_b"], tb=TB, name=n("rwkv_shift"))
    rc = [w[k] for k in ("w0", "w_up", "a0", "a_up", "g_up", "k_k", "k_a")]
    r_, dec, k2, v, kkn, a, gate = _stage(f_rwkv_pre, [f], rc, tb=TB, name=n("rwkv_pre"), out_dtypes=[F32] * 7)
    y, ck = _rwkv_scan_fwd(r_, dec, k2, v, kkn, a, lc=RW_LC, name=n("rwkv_scan"))
    (yc,) = _stage(f_rwkv_post, [y, r_, k2, v, gate], [w["ln_g"], w["ln_b"], w["r_k"]], tb=TB, name=n("rwkv_post"),
                   out_dtypes=[BF16])
    xc = _conv_fwd(xl, w["lru_conv_w"], w["lru_conv_b"], tb=TB, name=n("lru_conv"))
    lc = [w[k] for k in ("lru_wa", "lru_b_a", "lru_wx", "lru_b_x", "lru_lam")]
    a_l, bx = _stage(f_lru_pre, [xc], lc, tb=TB, name=n("lru_pre"), out_dtypes=[F32] * 2, pos=True)
    h = _lru_scan_fwd(a_l, bx, tb=SCAN_TB, name=n("lru_scan"))
    (yd,) = _stage(f_lru_post, [h, gl], [], tb=TB, name=n("lru_post"), out_dtypes=[BF16])
    mo = _mm(yc, w["out_a"], name=n("out_a"))
    mo = _mm(yd, w["out_b"], add=mo, name=n("out_b"))
    res = dict(rw=rw, xl=xl, gl=gl, f=f, r=r_, dec=dec, k2=k2, v=v, kkn=kkn, a=a, gate=gate, y=y, ck=ck, yc=yc,
               xc=xc, a_l=a_l, h=h, yd=yd)
    return mo, res


def _odd_bwd(dmo, hn, w, r, tag):
    n = lambda s: f"{tag}_{s}"
    g = {}
    g["out_a"] = _mm(r["yc"], dmo, ta=True, name=n("d_out_a"))
    g["out_b"] = _mm(r["yd"], dmo, ta=True, name=n("d_out_b"))
    dyc = _mm(dmo, w["out_a"], tb=True, name=n("dyc"))
    dyd = _mm(dmo, w["out_b"], tb=True, name=n("dyd"))
    dh, dgl = _stage_vjp(f_lru_post, [r["h"], r["gl"]], [], [dyd], tb=TB, name=n("lru_post_b"), drow=[0, 1], dconst=[])
    da_l, dbx = _lru_scan_bwd(r["a_l"], r["h"], dh, tb=SCAN_TB, name=n("lru_scan_b"))
    lc = [w[k] for k in ("lru_wa", "lru_b_a", "lru_wx", "lru_b_x", "lru_lam")]
    dxc, g["lru_wa"], g["lru_b_a"], g["lru_wx"], g["lru_b_x"], g["lru_lam"] = _stage_vjp(
        f_lru_pre, [r["xc"]], lc, [da_l, dbx], tb=TBH, name=n("lru_pre_b"), drow=[0], dconst=[0, 1, 2, 3, 4], pos=True)
    dxl, g["lru_conv_w"], g["lru_conv_b"] = _conv_bwd(r["xl"], w["lru_conv_w"], dxc, tb=TB, name=n("lru_conv_b"))
    dy, dr1, dk1, dv1, dgate, g["ln_g"], g["ln_b"], g["r_k"] = _stage_vjp(
        f_rwkv_post, [r["y"], r["r"], r["k2"], r["v"], r["gate"]], [w["ln_g"], w["ln_b"], w["r_k"]], [dyc], tb=TBH,
        name=n("rwkv_post_b"), drow=[0, 1, 2, 3, 4], dconst=[0, 1, 2])
    dr2, ddec, dk2, dv2, dkkn, da = _rwkv_scan_bwd(r["r"], r["dec"], r["k2"], r["v"], r["kkn"], r["a"], r["ck"], dy,
                                                   lc=RW_LC, name=n("rwkv_scan_b"))
    rc = [w[k] for k in ("w0", "w_up", "a0", "a_up", "g_up", "k_k", "k_a")]
    df, g["w0"], g["w_up"], g["a0"], g["a_up"], g["g_up"], g["k_k"], g["k_a"] = _stage_vjp(
        f_rwkv_pre_pass, [r["f"]], rc, [dr2, ddec, dk2, dv2, dkkn, da, dgate, dr1, dk1, dv1], tb=TBH,
        name=n("rwkv_pre_b"), drow=[0], dconst=[0, 1, 2, 3, 4, 5, 6])
    drw, g["mix_w"], _ = _conv_bwd(r["rw"], w["mix_w"], df, tb=TB, name=n("rwkv_shift_b"))
    segs = (("in_rw", drw), ("in_xl", dxl), ("in_gl", dgl))
    dhn = None
    for key, dseg in segs:
        g[key] = _mm(hn, dseg, ta=True, name=n("d_" + key))
        dhn = _mm(dseg, w[key], tb=True, add=dhn, name=n("dhn_" + key))
    return dhn, g


def _layer_fwd(h, p_i, w, odd, tag):
    n = lambda s: f"{tag}_{s}"
    (hn,) = _stage(f_norm, [h], [w["norm_mix"]], tb=TB, name=n("norm_mix"), out_dtypes=[BF16])
    mo, mres = (_odd_fwd if odd else _even_fwd)(hn, w, tag)
    h1, hf = _stage(f_add_norm, [h, mo], [w["norm_ffn"]], tb=TB, name=n("norm_ffn"), out_dtypes=[F32, BF16])
    u = _mm(hf, w["mlp_w1"], name=n("mlp_up"))
    (act,) = _stage(f_relu2, [u], [], tb=TBH, name=n("mlp_act"), out_dtypes=[BF16])
    m2 = _mm(act, w["mlp_w2"], name=n("mlp_down"))
    h2, hp = _stage(f_add_norm, [h1, m2], [w["norm_pl"]], tb=TB, name=n("norm_pl"), out_dtypes=[F32, BF16])
    gl = _mm(hp, w["pl_gate"], name=n("pl_gate"))
    pp = _mm(p_i, w["pl_proj"], name=n("pl_proj"))
    (h3,) = _stage(f_plgate, [h2, gl, pp], [], tb=TB, name=n("pl_mix"), out_dtypes=[F32])
    res = dict(h=h, hn=hn, mo=mo, mix=mres, h1=h1, hf=hf, u=u, act=act, m2=m2, h2=h2, hp=hp, gl=gl, pp=pp)
    return h3, res


def _layer_bwd(dh3, p_i, w, r, odd, tag):
    n = lambda s: f"{tag}_{s}"
    g = {}
    dh2, dgl, dpp = _stage_vjp(f_plgate, [r["h2"], r["gl"], r["pp"]], [], [dh3], tb=TB, name=n("pl_mix_b"),
                               drow=[0, 1, 2], dconst=[])
    g["pl_proj"] = _mm(p_i, dpp, ta=True, name=n("d_pl_proj"))
    g["pl_gate"] = _mm(r["hp"], dgl, ta=True, name=n("d_pl_gate"))
    dhp = _mm(dgl, w["pl_gate"], tb=True, name=n("dhp"))
    dh1, dm2, g["norm_pl"] = _stage_vjp(f_add_norm, [r["h1"], r["m2"]], [w["norm_pl"]], [dh2, dhp], tb=TB,
                                        name=n("norm_pl_b"), drow=[0, 1], dconst=[0])
    g["mlp_w2"] = _mm(r["act"], dm2, ta=True, name=n("d_mlp_w2"))
    dact = _mm(dm2, w["mlp_w2"], tb=True, name=n("dact"))
    (du,) = _stage_vjp(f_relu2, [r["u"]], [], [dact], tb=TBH, name=n("mlp_act_b"), drow=[0], dconst=[],
                       drow_dtypes=[BF16])
    g["mlp_w1"] = _mm(r["hf"], du, ta=True, name=n("d_mlp_w1"))
    dhf = _mm(du, w["mlp_w1"], tb=True, name=n("dhf"))
    dh, dmo, g["norm_ffn"] = _stage_vjp(f_add_norm, [r["h"], r["mo"]], [w["norm_ffn"]], [dh1, dhf], tb=TB,
                                        name=n("norm_ffn_b"), drow=[0, 1], dconst=[0])
    dhn, gm = (_odd_bwd if odd else _even_bwd)(dmo, r["hn"], w, r["mix"], tag)
    g.update(gm)
    dh0, g["norm_mix"] = _stage_vjp(f_norm_pass, [r["h"]], [w["norm_mix"]], [dhn, dh], tb=TB, name=n("norm_mix_b"),
                                    drow=[0], dconst=[0])
    return dh0, g


def _pad_to(a, size, axis):
    pad = [(0, 0)] * a.ndim
    pad[axis] = (0, size - a.shape[axis])
    return jnp.pad(a, pad)


def _rw_pad(a):
    return jnp.concatenate([a[..., :3072], _pad_to(a[..., 3072:3168], 128, -1), _pad_to(a[..., 3168:3264], 128, -1),
                            a[..., 3264:3520]], axis=-1)


def _rw_unpad(a):
    return jnp.concatenate([a[..., :3072], a[..., 3072:3168], a[..., 3200:3296], a[..., 3328:3584]], axis=-1)


def _block_diag(w):
    nb, bs, _ = w.shape
    eye = jnp.eye(nb, dtype=w.dtype)
    return (w[:, :, None, :] * eye[:, None, :, None]).reshape(nb * bs, nb * bs)


def _diag_blocks(w):
    nb = LRU_B
    bs = w.shape[0] // nb
    return jnp.stack([w[h * bs:(h + 1) * bs, h * bs:(h + 1) * bs] for h in range(nb)])


def _s5_prep_inputs(fw):
    lstep = jnp.broadcast_to(_pad_to(fw["s5_log_step"].astype(F32), 128, 1), (8, 128))
    t16 = lambda b: jnp.transpose(b[0], (2, 0, 1)).reshape(S5_G, S5_N)
    tc = lambda c: jnp.transpose(c[0], (0, 2, 1)).reshape(S5_N, S5_G)
    return [fw["s5_lam_re"].reshape(1, S5_N), fw["s5_lam_im"].reshape(1, S5_N), lstep,
            t16(fw["s5_b_re"]), t16(fw["s5_b_im"]), tc(fw["s5_c_re"]), tc(fw["s5_c_im"])]


def _layer_weights(fw, i):
    w = {k: fw[k][i:i + 1] for k in ("norm_mix", "norm_ffn", "norm_pl")}
    for k in ("mlp_w1", "mlp_w2", "pl_proj", "pl_gate"):
        w[k] = fw[k][i]
    return w


def _even_weights(fw, prep):
    w = _layer_weights(fw, 0)
    ein, eout = fw["e_in_proj"][0], fw["e_out_proj"][0]
    w.update(in_u=ein[:, :512], in_z=ein[:, 512:2048], in_xbc=ein[:, 2048:4608], in_dt=_pad_to(ein[:, 4608:], 128, 1),
             out_a=eout[:512], out_b=eout[512:])
    abar_re, abar_im, wb_re, wb_im, wc_re, wc_im = prep
    w.update(abar_re=abar_re, abar_im=abar_im, wb_re=wb_re, wb_im=wb_im, wc_re=wc_re.astype(BF16), wc_im=wc_im.astype(BF16),
             s5_d=fw["s5_d"], glu_w=fw["s5_glu_w"][0], glu_b=fw["s5_glu_b"],
             ssd_conv_w=_pad_to(fw["ssd_conv_w"][0], 8, 0), ssd_conv_b=fw["ssd_conv_b"],
             dt_bias=_pad_to(fw["ssd_dt_bias"], 128, 1), a_log=_pad_to(fw["ssd_a_log"], 128, 1),
             ssd_d=_pad_to(fw["ssd_d"], 128, 1), ssd_norm=fw["ssd_norm"])
    return w


def _odd_weights(fw):
    w = _layer_weights(fw, 1)
    oin, oout = fw["o_in_proj"][0], fw["o_out_proj"][0]
    mu = _rw_pad(fw["rwkv_mu"])
    zero = jnp.zeros_like(mu)
    w.update(in_rw=_rw_pad(oin[:, :RW_IN]), in_xl=oin[:, RW_IN:RW_IN + LRU_W], in_gl=oin[:, RW_IN + LRU_W:],
             out_a=oout[:RW_W], out_b=oout[RW_W:],
             mix_w=jnp.concatenate([zero, zero, mu, 1.0 - mu, zero, zero, zero, zero], axis=0), mix_b=zero,
             w0=fw["rwkv_w0"], w_up=_pad_to(fw["rwkv_w_up"][0], 128, 0), a0=fw["rwkv_a0"],
             a_up=_pad_to(fw["rwkv_a_up"][0], 128, 0), g_up=fw["rwkv_g_up"][0], k_k=fw["rwkv_k_k"], k_a=fw["rwkv_k_a"],
             r_k=fw["rwkv_r_k"].reshape(1, RW_W), ln_g=fw["rwkv_ln_g"], ln_b=fw["rwkv_ln_b"],
             lru_conv_w=_pad_to(fw["lru_conv_w"][0], 8, 0), lru_conv_b=fw["lru_conv_b"],
             lru_wa=_block_diag(fw["lru_w_a"][0]).astype(BF16), lru_b_a=fw["lru_b_a"].reshape(1, LRU_W),
             lru_wx=_block_diag(fw["lru_w_x"][0]).astype(BF16), lru_b_x=fw["lru_b_x"].reshape(1, LRU_W),
             lru_lam=fw["lru_lam"].reshape(1, LRU_W))
    return w


def _global_grads(g0, g1, s5_grads, d_norm_final):
    out = {k: jnp.concatenate([g0[k], g1[k]], axis=0) for k in ("norm_mix", "norm_ffn", "norm_pl")}
    for k in ("mlp_w1", "mlp_w2", "pl_proj", "pl_gate"):
        out[k] = jnp.stack([g0[k], g1[k]])
    out["e_in_proj"] = jnp.concatenate([g0["in_u"], g0["in_z"], g0["in_xbc"], g0["in_dt"][:, :SSD_H]], axis=1)[None]
    out["e_out_proj"] = jnp.concatenate([g0["out_a"], g0["out_b"]], axis=0)[None]
    d_lam_re, d_lam_im, d_lstep, d_bre, d_bim, d_cre, d_cim = s5_grads
    out["s5_lam_re"] = d_lam_re.reshape(1, S5_GROUPS, S5_P)
    out["s5_lam_im"] = d_lam_im.reshape(1, S5_GROUPS, S5_P)
    out["s5_log_step"] = d_lstep[0:1, :S5_GROUPS]
    unb = lambda b: jnp.transpose(b.reshape(S5_G, S5_GROUPS, S5_P), (1, 2, 0))[None]
    unc = lambda c: jnp.transpose(c.reshape(S5_GROUPS, S5_P, S5_G), (0, 2, 1))[None]
    out.update(s5_b_re=unb(d_bre), s5_b_im=unb(d_bim), s5_c_re=unc(d_cre), s5_c_im=unc(d_cim),
               s5_d=g0["s5_d"], s5_glu_w=g0["glu_w"][None], s5_glu_b=g0["glu_b"],
               ssd_conv_w=g0["ssd_conv_w"][None, :4], ssd_conv_b=g0["ssd_conv_b"], ssd_dt_bias=g0["dt_bias"][:, :SSD_H],
               ssd_a_log=g0["a_log"][:, :SSD_H], ssd_d=g0["ssd_d"][:, :SSD_H], ssd_norm=g0["ssd_norm"])
    out["o_in_proj"] = jnp.concatenate([_rw_unpad(g1["in_rw"]), g1["in_xl"], g1["in_gl"]], axis=1)[None]
    out["o_out_proj"] = jnp.concatenate([g1["out_a"], g1["out_b"]], axis=0)[None]
    out.update(rwkv_mu=_rw_unpad(g1["mix_w"][2:3] - g1["mix_w"][3:4]), rwkv_w0=g1["w0"], rwkv_w_up=g1["w_up"][None, :RW_LORA],
               rwkv_a0=g1["a0"], rwkv_a_up=g1["a_up"][None, :RW_LORA], rwkv_g_up=g1["g_up"][None], rwkv_k_k=g1["k_k"],
               rwkv_k_a=g1["k_a"], rwkv_r_k=g1["r_k"].reshape(1, RW_H, RW_HD), rwkv_ln_g=g1["ln_g"], rwkv_ln_b=g1["ln_b"],
               lru_conv_w=g1["lru_conv_w"][None, :4], lru_conv_b=g1["lru_conv_b"],
               lru_w_a=_diag_blocks(g1["lru_wa"])[None], lru_b_a=g1["lru_b_a"].reshape(1, LRU_B, 64),
               lru_w_x=_diag_blocks(g1["lru_wx"])[None], lru_b_x=g1["lru_b_x"].reshape(1, LRU_B, 64),
               lru_lam=g1["lru_lam"].reshape(1, LRU_B, 64), norm_final=d_norm_final.reshape(D))
    return out


def _local_step(x, p, target, fw):
    prep_in = _s5_prep_inputs(fw)
    prep = _single(f_s5_prep, prep_in, name="s5_prep")
    w0, w1 = _even_weights(fw, prep), _odd_weights(fw)
    h1, r0 = _layer_fwd(x, p[0], w0, False, "l0")
    h2, r1 = _layer_fwd(h1, p[1], w1, True, "l1")
    gf = fw["norm_final"].reshape(1, D)
    (loss8,) = _stage(f_loss, [h2, target], [gf], tb=TB, name="loss", out_dtypes=[], n_acc=1)
    one = jnp.zeros((8, 128), F32).at[0, 0].set(1.0)
    dh2, d_gf = _stage_vjp(f_loss, [h2, target], [gf], [], tb=TB, name="loss_b", drow=[0], dconst=[0], acc_cots=[one])
    dh1, g1 = _layer_bwd(dh2, p[1], w1, r1, True, "l1")
    dx, g0 = _layer_bwd(dh1, p[0], w0, r0, False, "l0")
    cots = [g0[k] for k in ("abar_re", "abar_im", "wb_re", "wb_im", "wc_re", "wc_im")]
    s5_grads = _single_vjp(f_s5_prep, prep_in, cots, name="s5_prep_b")
    return loss8[0, 0], dx, _global_grads(g0, g1, s5_grads, d_gf)


def _xyc():
    return lax.axis_index("x"), lax.axis_index("y"), lax.axis_index("c")


def _flip(v, bit):
    return 1 - v if bit else v


def _remote(src, dst, send_sems, recv_sems, k, dev):
    return pltpu.make_async_remote_copy(src_ref=src, dst_ref=dst, send_sem=send_sems.at[k], recv_sem=recv_sems.at[k],
                                        device_id=dev, device_id_type=MESH)


def _dma_scratch(n_remote, n_local):
    return [pltpu.SemaphoreType.DMA((n_remote,)), pltpu.SemaphoreType.DMA((n_remote,)), pltpu.SemaphoreType.DMA((n_local,))]


CHIP_FLIPS = ((1, 0), (0, 1), (1, 1))


def _gather_chips(arrs, *, name):
    n = len(arrs)

    def body(*refs):
        ins, outs = refs[:n], refs[n:2 * n]
        send_sems, recv_sems, loc_sems = refs[2 * n:]
        x, y, c = _xyc()
        sends, locs = [], []
        for a in range(n):
            mine = outs[a].at[2 * x + y]
            lc = pltpu.make_async_copy(ins[a], mine, loc_sems.at[a])
            lc.start()
            locs.append(lc)
            for j, (fx, fy) in enumerate(CHIP_FLIPS):
                cp = _remote(ins[a], mine, send_sems, recv_sems, 3 * a + j, (_flip(x, fx), _flip(y, fy), c))
                cp.start()
                sends.append(cp)
        for a in range(n):
            for j, (fx, fy) in enumerate(CHIP_FLIPS):
                px, py = _flip(x, fx), _flip(y, fy)
                _remote(ins[a], outs[a].at[2 * px + py], send_sems, recv_sems, 3 * a + j, (px, py, c)).wait_recv()
        for cp in sends:
            cp.wait_send()
        for lc in locs:
            lc.wait()

    return pl.pallas_call(
        body, out_shape=[SDS((4,) + a.shape, a.dtype) for a in arrs], in_specs=[ANY] * n, out_specs=[ANY] * n,
        scratch_shapes=_dma_scratch(3 * n, n), name=name,
    )(*arrs)


def _gather_all(arr, *, name):
    def body(in_ref, out_ref, send_sems, recv_sems, loc_sems):
        x, y, c = _xyc()
        mine = out_ref.at[4 * x + 2 * y + c]
        lc = pltpu.make_async_copy(in_ref, mine, loc_sems.at[0])
        lc.start()
        sends = []
        for k in range(1, 8):
            dev = (_flip(x, k >> 2 & 1), _flip(y, k >> 1 & 1), _flip(c, k & 1))
            cp = _remote(in_ref, mine, send_sems, recv_sems, k - 1, dev)
            cp.start()
            sends.append(cp)
        for k in range(1, 8):
            px, py, pc = _flip(x, k >> 2 & 1), _flip(y, k >> 1 & 1), _flip(c, k & 1)
            _remote(in_ref, out_ref.at[4 * px + 2 * py + pc], send_sems, recv_sems, k - 1, (px, py, pc)).wait_recv()
        for cp in sends:
            cp.wait_send()
        lc.wait()

    return pl.pallas_call(
        body, out_shape=SDS((8,) + arr.shape, arr.dtype), in_specs=[ANY], out_specs=ANY,
        scratch_shapes=_dma_scratch(7, 1), name=name,
    )(arr)


def _pair_halves(arrs, *, name):
    n = len(arrs)

    def body(*refs):
        ins, outs = refs[:n], refs[n:2 * n]
        send_sems, recv_sems, loc_sems = refs[2 * n:]
        x, y, c = _xyc()
        sib = (x, y, 1 - c)
        sends, locs = [], []
        for a in range(n):
            for k in range(4):
                lc = pltpu.make_async_copy(ins[a].at[k, c], outs[a].at[0, k], loc_sems.at[4 * a + k])
                lc.start()
                locs.append(lc)
                cp = _remote(ins[a].at[k, 1 - c], outs[a].at[1, k], send_sems, recv_sems, 4 * a + k, sib)
                cp.start()
                sends.append(cp)
        for a in range(n):
            for k in range(4):
                _remote(ins[a].at[k, 1 - c], outs[a].at[1, k], send_sems, recv_sems, 4 * a + k, sib).wait_recv()
        for cp in sends:
            cp.wait_send()
        for lc in locs:
            lc.wait()

    return pl.pallas_call(
        body, out_shape=[SDS((2, 4) + a.shape[2:], a.dtype) for a in arrs], in_specs=[ANY] * n, out_specs=[ANY] * n,
        scratch_shapes=_dma_scratch(4 * n, 4 * n), name=name,
    )(*arrs)


def _scatter_chips(arrs, *, name):
    n = len(arrs)

    def body(*refs):
        ins, outs = refs[:n], refs[n:2 * n]
        send_sems, recv_sems, loc_sems = refs[2 * n:]
        x, y, c = _xyc()
        chip = 2 * x + y
        sends, locs = [], []
        for a in range(n):
            lc = pltpu.make_async_copy(ins[a].at[chip], outs[a].at[chip], loc_sems.at[a])
            lc.start()
            locs.append(lc)
            for j, (fx, fy) in enumerate(CHIP_FLIPS):
                px, py = _flip(x, fx), _flip(y, fy)
                cp = _remote(ins[a].at[2 * px + py], outs[a].at[chip], send_sems, recv_sems, 3 * a + j, (px, py, c))
                cp.start()
                sends.append(cp)
        for a in range(n):
            for j, (fx, fy) in enumerate(CHIP_FLIPS):
                px, py = _flip(x, fx), _flip(y, fy)
                _remote(ins[a].at[chip], outs[a].at[2 * px + py], send_sems, recv_sems, 3 * a + j, (px, py, c)).wait_recv()
        for cp in sends:
            cp.wait_send()
        for lc in locs:
            lc.wait()

    return pl.pallas_call(
        body, out_shape=[SDS(a.shape, a.dtype) for a in arrs], in_specs=[ANY] * n, out_specs=[ANY] * n,
        scratch_shapes=_dma_scratch(3 * n, n), name=name,
    )(*arrs)


def _join_halves(arrs, *, name):
    n = len(arrs)

    def body(*refs):
        ins, outs = refs[:n], refs[n:2 * n]
        send_sems, recv_sems, loc_sems = refs[2 * n:]
        x, y, c = _xyc()
        sib = (x, y, 1 - c)
        sends, locs = [], []
        for a in range(n):
            lc = pltpu.make_async_copy(ins[a], outs[a].at[c], loc_sems.at[a])
            lc.start()
            locs.append(lc)
            cp = _remote(ins[a], outs[a].at[c], send_sems, recv_sems, a, sib)
            cp.start()
            sends.append(cp)
        for a in range(n):
            _remote(ins[a], outs[a].at[1 - c], send_sems, recv_sems, a, sib).wait_recv()
        for cp in sends:
            cp.wait_send()
        for lc in locs:
            lc.wait()

    return pl.pallas_call(
        body, out_shape=[SDS((2,) + a.shape, a.dtype) for a in arrs], in_specs=[ANY] * n, out_specs=[ANY] * n,
        scratch_shapes=_dma_scratch(n, n), name=name,
    )(*arrs)


def _sum_lead(x, *, tb, name):
    k, r, c = x.shape
    tb = min(tb, r)
    assert r % tb == 0

    def body(x_ref, o_ref):
        acc = x_ref[0]
        for q in range(1, k):
            acc = acc + x_ref[q]
        o_ref[...] = acc

    return pl.pallas_call(
        body, grid=(r // tb,), in_specs=[pl.BlockSpec((k, tb, c), lambda i: (0, i, 0))],
        out_specs=pl.BlockSpec((tb, c), lambda i: (i, 0)), out_shape=SDS((r, c), x.dtype),
        compiler_params=_cparams(), name=name,
    )(x)


def f_adamw(w, g, m, v):
    m = ADAM_B1 * m + (1.0 - ADAM_B1) * g
    v = ADAM_B2 * v + (1.0 - ADAM_B2) * (g * g)
    m_hat = m / (1.0 - ADAM_B1 ** ADAM_STEP)
    v_hat = v / (1.0 - ADAM_B2 ** ADAM_STEP)
    return -ADAM_LR * (m_hat / (jnp.sqrt(v_hat) + ADAM_EPS) + ADAM_WD * w), m, v


def _adamw(w, g, m, v, *, name):
    shape = w.shape
    two = lambda a: a.reshape(-1, shape[-1])
    rows = two(w).shape[0]
    tb = 256 if rows % 256 == 0 else rows
    outs = _stage(f_adamw, [two(w), two(g), two(m), two(v)], [], tb=tb, name=name, out_dtypes=[F32] * 3)
    return [o.reshape(shape) for o in outs]


def _pack(arrs):
    flat = jnp.concatenate([a.astype(F32).reshape(-1) for a in arrs])
    size = -(-flat.shape[0] // 1024) * 1024
    return _pad_to(flat, size, 0).reshape(-1, 128)


def _unpack(buf, shapes):
    flat = buf.reshape(-1)
    out, off = [], 0
    for s in shapes:
        n = math.prod(s)
        out.append(flat[off:off + n].reshape(s))
        off += n
    return out


WEIGHTS = ("norm_mix", "norm_ffn", "norm_pl", "mlp_w1", "mlp_w2", "pl_proj", "pl_gate", "e_in_proj", "e_out_proj",
           "s5_lam_re", "s5_lam_im", "s5_log_step", "s5_b_re", "s5_b_im", "s5_c_re", "s5_c_im", "s5_d", "s5_glu_w",
           "s5_glu_b", "ssd_conv_w", "ssd_conv_b", "ssd_dt_bias", "ssd_a_log", "ssd_d", "ssd_norm", "o_in_proj",
           "o_out_proj", "rwkv_mu", "rwkv_w0", "rwkv_w_up", "rwkv_a0", "rwkv_a_up", "rwkv_g_up", "rwkv_k_k", "rwkv_k_a",
           "rwkv_r_k", "rwkv_ln_g", "rwkv_ln_b", "lru_conv_w", "lru_conv_b", "lru_w_a", "lru_b_a", "lru_w_x", "lru_b_x",
           "lru_lam", "norm_final")
BIG = ("mlp_w1", "mlp_w2", "pl_proj", "pl_gate", "e_in_proj", "e_out_proj", "o_in_proj", "o_out_proj")
SHARD_AXIS = {"mlp_w1": 2, "mlp_w2": 1, "pl_proj": 2, "pl_gate": 1, "e_in_proj": 2, "e_out_proj": 1, "s5_glu_w": 1,
              "ssd_conv_w": 2, "o_in_proj": 2, "o_out_proj": 1, "rwkv_mu": 1, "rwkv_w0": 1, "rwkv_w_up": 2, "rwkv_a0": 1,
              "rwkv_a_up": 2, "rwkv_g_up": 2, "rwkv_k_k": 1, "rwkv_k_a": 1, "rwkv_ln_g": 1, "rwkv_ln_b": 1,
              "lru_conv_w": 2, "lru_conv_b": 1}
SMALL = tuple(n for n in WEIGHTS if n not in BIG)
SMALL_SHARDED = tuple(n for n in SMALL if n in SHARD_AXIS)


def _gather_weights(w):
    shapes = [w[n].shape for n in SMALL_SHARDED]
    got = _gather_chips([w[n].astype(BF16) for n in BIG] + [_pack([w[n] for n in SMALL_SHARDED])], name="gather_weights")
    fw = {n: w[n] for n in SMALL if n not in SHARD_AXIS}
    for n, g in zip(BIG, got[:-1]):
        fw[n] = jnp.concatenate([g[k] for k in range(4)], axis=SHARD_AXIS[n])
    parts = [_unpack(got[-1][k], shapes) for k in range(4)]
    for i, n in enumerate(SMALL_SHARDED):
        fw[n] = jnp.concatenate([parts[k][i] for k in range(4)], axis=SHARD_AXIS[n])
    return fw


def _reduce_big(grads, w):
    stacks = []
    for n in BIG:
        cols = w[n].shape[-1]
        stacks.append(jnp.stack(jnp.split(grads[n], 4, axis=SHARD_AXIS[n])).reshape(4, 2, -1, cols))
    pairs = _pair_halves(stacks, name="reduce_pair")
    sums = [_sum_lead(h.reshape(2, -1, h.shape[-1]), tb=512, name=f"reduce_pair_sum_{n}").reshape(h.shape[1:])
            for n, h in zip(BIG, pairs)]
    landed = _scatter_chips(sums, name="reduce_chips")
    halves = [_sum_lead(h, tb=512, name=f"reduce_chips_sum_{n}") for n, h in zip(BIG, landed)]
    joined = _join_halves(halves, name="reduce_join")
    return {n: j.reshape(w[n].shape) for n, j in zip(BIG, joined)}


def _reduce_small(grads, w, chip):
    shapes = [grads[n].shape for n in SMALL]
    packed = _pack([grads[n] for n in SMALL])
    total = _sum_lead(_gather_all(packed, name="reduce_small"), tb=packed.shape[0], name="reduce_small_sum")
    out = {}
    for n, g in zip(SMALL, _unpack(total, shapes)):
        if n in SHARD_AXIS:
            ax = SHARD_AXIS[n]
            size = w[n].shape[ax]
            g = lax.dynamic_slice_in_dim(g, chip * size, size, axis=ax)
        out[n] = g
    return out


def kernel(x, p, norm_mix, norm_ffn, norm_pl, mlp_w1, mlp_w2, pl_proj, pl_gate, e_in_proj, e_out_proj, s5_lam_re, s5_lam_im, s5_log_step, s5_b_re, s5_b_im, s5_c_re, s5_c_im, s5_d, s5_glu_w, s5_glu_b, ssd_conv_w, ssd_conv_b, ssd_dt_bias, ssd_a_log, ssd_d, ssd_norm, o_in_proj, o_out_proj, rwkv_mu, rwkv_w0, rwkv_w_up, rwkv_a0, rwkv_a_up, rwkv_g_up, rwkv_k_k, rwkv_k_a, rwkv_r_k, rwkv_ln_g, rwkv_ln_b, lru_conv_w, lru_conv_b, lru_w_a, lru_b_a, lru_w_x, lru_b_x, lru_lam, norm_final, loss_target, m_norm_mix, m_norm_ffn, m_norm_pl, m_mlp_w1, m_mlp_w2, m_pl_proj, m_pl_gate, m_e_in_proj, m_e_out_proj, m_s5_lam_re, m_s5_lam_im, m_s5_log_step, m_s5_b_re, m_s5_b_im, m_s5_c_re, m_s5_c_im, m_s5_d, m_s5_glu_w, m_s5_glu_b, m_ssd_conv_w, m_ssd_conv_b, m_ssd_dt_bias, m_ssd_a_log, m_ssd_d, m_ssd_norm, m_o_in_proj, m_o_out_proj, m_rwkv_mu, m_rwkv_w0, m_rwkv_w_up, m_rwkv_a0, m_rwkv_a_up, m_rwkv_g_up, m_rwkv_k_k, m_rwkv_k_a, m_rwkv_r_k, m_rwkv_ln_g, m_rwkv_ln_b, m_lru_conv_w, m_lru_conv_b, m_lru_w_a, m_lru_b_a, m_lru_w_x, m_lru_b_x, m_lru_lam, m_norm_final, v_norm_mix, v_norm_ffn, v_norm_pl, v_mlp_w1, v_mlp_w2, v_pl_proj, v_pl_gate, v_e_in_proj, v_e_out_proj, v_s5_lam_re, v_s5_lam_im, v_s5_log_step, v_s5_b_re, v_s5_b_im, v_s5_c_re, v_s5_c_im, v_s5_d, v_s5_glu_w, v_s5_glu_b, v_ssd_conv_w, v_ssd_conv_b, v_ssd_dt_bias, v_ssd_a_log, v_ssd_d, v_ssd_norm, v_o_in_proj, v_o_out_proj, v_rwkv_mu, v_rwkv_w0, v_rwkv_w_up, v_rwkv_a0, v_rwkv_a_up, v_rwkv_g_up, v_rwkv_k_k, v_rwkv_k_a, v_rwkv_r_k, v_rwkv_ln_g, v_rwkv_ln_b, v_lru_conv_w, v_lru_conv_b, v_lru_w_a, v_lru_b_a, v_lru_w_x, v_lru_b_x, v_lru_lam, v_norm_final):
    given = dict(locals())
    w = {n: given[n] for n in WEIGHTS}
    m = {n: given["m_" + n] for n in WEIGHTS}
    v = {n: given["v_" + n] for n in WEIGHTS}
    chip = 2 * lax.axis_index("x") + lax.axis_index("y")

    fw = _gather_weights(w)
    loss, dx, grads = _local_step(x[0], p[:, 0], loss_target[0], fw)
    loss = lax.psum(loss, ("x", "y", "c"))

    g = _reduce_big(grads, w)
    g.update(_reduce_small(grads, w, chip))

    delta, new_m, new_v = {}, {}, {}
    for n in BIG:
        delta[n], new_m[n], new_v[n] = _adamw(w[n], g[n], m[n], v[n], name=f"adamw_{n}")
    shapes = [w[n].shape for n in SMALL]
    packed = [_pack([d[n] for n in SMALL]) for d in (w, g, m, v)]
    for d, buf in zip((delta, new_m, new_v), _adamw(*packed, name="adamw_small")):
        d.update(zip(SMALL, _unpack(buf, shapes)))
    return (loss, dx[None], *[g[n] for n in WEIGHTS], *[delta[n] for n in WEIGHTS],
            *[new_m[n] for n in WEIGHTS], *[new_v[n] for n in WEIGHTS])
```

```python
import functools
import math

import jax
import jax.numpy as jnp
from jax import lax
from jax.experimental import pallas as pl
from jax.experimental.pallas import tpu as pltpu

F32 = jnp.float32
BF16 = jnp.bfloat16
HI = lax.Precision.HIGHEST
MESH = pl.DeviceIdType.MESH
SDS = jax.ShapeDtypeStruct
VMEM_LIMIT = 56 * 1024 * 1024
ANY = pl.BlockSpec(memory_space=pl.ANY)

D = 2048
PL_DIM = 256
D_FF = 4 * D
EPS = 1e-6
S5_W, S5_G, S5_GROUPS, S5_P = 512, 16, 32, 64
S5_N = S5_GROUPS * S5_P
SSD_W, SSD_HD, SSD_H, SSD_NG, SSD_N, SSD_L = 1536, 64, 24, 4, 128, 128
SSD_CONV = SSD_W + 2 * SSD_NG * SSD_N
EVEN_IN = S5_W + SSD_W + SSD_CONV + SSD_H
EVEN_PAD = 5120
RW_W, RW_H, RW_HD = 1024, 16, 64
RW_LORA = 96
RW_GATE = 256
RW_IN = 3 * RW_W + 2 * RW_LORA + RW_GATE
RW_PAD = 3584
LRU_W, LRU_B = 1024, 16
ODD_IN = RW_IN + 2 * LRU_W
ODD_PAD = RW_PAD + 2 * LRU_W
GN_EPS = 64e-5
LRU_C = 8.0
ADAM_LR, ADAM_B1, ADAM_B2, ADAM_EPS, ADAM_WD, ADAM_STEP = 0.001, 0.9, 0.999, 1e-08, 0.01, 10


def _cparams(sem=("arbitrary",)):
    return pltpu.CompilerParams(dimension_semantics=sem, vmem_limit_bytes=VMEM_LIMIT)


def _dot16(a, b, dims=(((1,), (0,)), ((), ()))):
    return lax.dot_general(a.astype(BF16), b.astype(BF16), dims, preferred_element_type=F32)


def _dot32(a, b, dims=(((1,), (0,)), ((), ()))):
    return lax.dot_general(a.astype(F32), b.astype(F32), dims, precision=HI, preferred_element_type=F32)


NT = (((1,), (1,)), ((), ()))
TN = (((0,), (0,)), ((), ()))


def _tile(dim, target):
    if dim <= target:
        return dim
    t = target - target % 128
    while t > 128 and dim % t:
        t -= 128
    assert dim % t == 0, (dim, target)
    return t


def _mm(a, b, *, ta=False, tb=False, add=None, out_dtype=F32, tm=512, tn=512, tk=1024, name):
    m, k = (a.shape[1], a.shape[0]) if ta else a.shape
    n = b.shape[0] if tb else b.shape[1]
    assert (b.shape[1] if tb else b.shape[0]) == k, (a.shape, b.shape, ta, tb)
    tm, tn, tk = _tile(m, tm), _tile(n, tn), _tile(k, tk)
    nk = k // tk
    dims = (((0 if ta else 1,), (1 if tb else 0,)), ((), ()))
    has_add = add is not None

    def body(*refs):
        a_ref, b_ref = refs[:2]
        o_ref, acc_ref = refs[-2:]
        kk = pl.program_id(2)

        @pl.when(kk == 0)
        def _():
            acc_ref[...] = refs[2][...].astype(F32) if has_add else jnp.zeros_like(acc_ref)

        acc_ref[...] += _dot16(a_ref[...], b_ref[...], dims)

        @pl.when(kk == nk - 1)
        def _():
            o_ref[...] = acc_ref[...].astype(o_ref.dtype)

    a_spec = pl.BlockSpec((tk, tm), lambda i, j, q: (q, i)) if ta else pl.BlockSpec((tm, tk), lambda i, j, q: (i, q))
    b_spec = pl.BlockSpec((tn, tk), lambda i, j, q: (j, q)) if tb else pl.BlockSpec((tk, tn), lambda i, j, q: (q, j))
    o_spec = pl.BlockSpec((tm, tn), lambda i, j, q: (i, j))
    return pl.pallas_call(
        body,
        grid=(m // tm, n // tn, nk),
        in_specs=[a_spec, b_spec] + ([o_spec] if has_add else []),
        out_specs=o_spec,
        out_shape=SDS((m, n), out_dtype),
        scratch_shapes=[pltpu.VMEM((tm, tn), F32)],
        compiler_params=_cparams(("parallel", "parallel", "arbitrary")),
        name=name,
    )(a, b, *([add] if has_add else []))


def _single(fn, consts, *, name):
    outs = jax.eval_shape(fn, *[SDS(c.shape, F32) for c in consts])
    n_in = len(consts)

    def body(*refs):
        res = fn(*[r[...] for r in refs[:n_in]])
        for o_ref, v in zip(refs[n_in:], res):
            o_ref[...] = v

    return pl.pallas_call(body, out_shape=[SDS(o.shape, F32) for o in outs],
                          compiler_params=pltpu.CompilerParams(vmem_limit_bytes=VMEM_LIMIT), name=name)(*consts)


def _single_vjp(fn, consts, cots, *, name):
    n_in = len(consts)

    def body(*refs):
        _, pull = jax.vjp(fn, *[r[...] for r in refs[:n_in]])
        grads = pull(tuple(r[...] for r in refs[n_in:n_in + len(cots)]))
        for o_ref, v in zip(refs[n_in + len(cots):], grads):
            o_ref[...] = v

    return pl.pallas_call(body, out_shape=[SDS(c.shape, F32) for c in consts],
                          compiler_params=pltpu.CompilerParams(vmem_limit_bytes=VMEM_LIMIT), name=name)(*consts, *cots)


def _full_spec(shape):
    nd = len(shape)
    return pl.BlockSpec(shape, lambda i, _n=nd: (0,) * _n)


def _stage_shapes(fn, rows, consts, tb, pos):
    rs = [SDS((tb, r.shape[1]), F32) for r in rows]
    cs = [SDS(c.shape, F32) for c in consts]
    f = (lambda *a: fn(jnp.int32(0), *a)) if pos else fn
    return jax.eval_shape(f, *rs, *cs)


def _stage(fn, rows, consts, *, tb, name, out_dtypes, n_acc=0, pos=False):
    t = rows[0].shape[0]
    assert t % tb == 0
    outs = _stage_shapes(fn, rows, consts, tb, pos)
    n_out = len(outs)
    n_row = n_out - n_acc
    n_in = len(rows) + len(consts)

    def body(*refs):
        i = pl.program_id(0)
        vals = [r[...].astype(F32) for r in refs[:n_in]]
        res = fn(i * tb, *vals) if pos else fn(*vals)
        out_refs = refs[n_in:]
        for q in range(n_row):
            out_refs[q][...] = res[q].astype(out_refs[q].dtype)
        for q in range(n_row, n_out):
            @pl.when(i == 0)
            def _(q=q):
                out_refs[q][...] = jnp.zeros_like(out_refs[q])

            out_refs[q][...] += res[q]

    in_specs = [pl.BlockSpec((tb, r.shape[1]), lambda i: (i, 0)) for r in rows] + [_full_spec(c.shape) for c in consts]
    out_specs = [pl.BlockSpec((tb, o.shape[1]), lambda i: (i, 0)) for o in outs[:n_row]] + [_full_spec(o.shape) for o in outs[n_row:]]
    out_shape = [SDS((t, o.shape[1]), dt) for o, dt in zip(outs[:n_row], out_dtypes)] + [SDS(o.shape, F32) for o in outs[n_row:]]
    return pl.pallas_call(
        body, grid=(t // tb,), in_specs=in_specs, out_specs=out_specs, out_shape=out_shape,
        compiler_params=_cparams(), name=name,
    )(*rows, *consts)


def _stage_vjp(fn, rows, consts, cots, *, tb, name, drow, dconst, drow_dtypes=None, acc_cots=(), pos=False):
    t = rows[0].shape[0]
    assert t % tb == 0
    n_rows, n_consts, n_cots, n_acc = len(rows), len(consts), len(cots), len(acc_cots)
    n_in = n_rows + n_consts + n_cots + n_acc
    drow_dtypes = drow_dtypes or [F32] * len(drow)

    def body(*refs):
        i = pl.program_id(0)
        vals = [r[...].astype(F32) for r in refs[:n_in]]
        rv, cv = vals[:n_rows], vals[n_rows:n_rows + n_consts]
        ct = tuple(vals[n_rows + n_consts:])

        def f(*dargs):
            r2, c2 = list(rv), list(cv)
            for q, idx in enumerate(drow):
                r2[idx] = dargs[q]
            for q, idx in enumerate(dconst):
                c2[idx] = dargs[len(drow) + q]
            return fn(i * tb, *r2, *c2) if pos else fn(*r2, *c2)

        _, pull = jax.vjp(f, *[rv[q] for q in drow], *[cv[q] for q in dconst])
        grads = pull(ct)
        out_refs = refs[n_in:]
        for q in range(len(drow)):
            out_refs[q][...] = grads[q].astype(out_refs[q].dtype)
        for q in range(len(drow), len(drow) + len(dconst)):
            @pl.when(i == 0)
            def _(q=q):
                out_refs[q][...] = jnp.zeros_like(out_refs[q])

            out_refs[q][...] += grads[q]

    in_specs = ([pl.BlockSpec((tb, r.shape[1]), lambda i: (i, 0)) for r in rows] + [_full_spec(c.shape) for c in consts]
                + [pl.BlockSpec((tb, c.shape[1]), lambda i: (i, 0)) for c in cots] + [_full_spec(c.shape) for c in acc_cots])
    out_specs = ([pl.BlockSpec((tb, rows[q].shape[1]), lambda i: (i, 0)) for q in drow]
                 + [_full_spec(consts[q].shape) for q in dconst])
    out_shape = ([SDS(rows[q].shape, dt) for q, dt in zip(drow, drow_dtypes)]
                 + [SDS(consts[q].shape, F32) for q in dconst])
    return pl.pallas_call(
        body, grid=(t // tb,), in_specs=in_specs, out_specs=out_specs, out_shape=out_shape,
        compiler_params=_cparams(), name=name,
    )(*rows, *consts, *cots, *acc_cots)


def _conv_fwd(x, w, b, *, tb, name):
    t, c = x.shape
    r8 = tb // 8

    def body(x_ref, p_ref, w_ref, b_ref, o_ref):
        i = pl.program_id(0)
        x_ = x_ref[...]
        p_ = jnp.where(i > 0, p_ref[...], 0.0)
        w_ = w_ref[...]
        row = lax.broadcasted_iota(jnp.int32, x_.shape, 0)
        row8 = lax.broadcasted_iota(jnp.int32, p_.shape, 0)
        acc = x_ * w_[3:4, :] + b_ref[...]
        head = jnp.zeros_like(p_)
        for j in (1, 2, 3):
            wj = w_[3 - j:4 - j, :]
            acc += jnp.where(row >= j, pltpu.roll(x_, j, 0), 0.0) * wj
            head += jnp.where(row8 < j, pltpu.roll(p_, j, 0), 0.0) * wj
        o_ref[...] = acc
        o_ref[0:8, :] += head

    return pl.pallas_call(
        body, grid=(t // tb,),
        in_specs=[pl.BlockSpec((tb, c), lambda i: (i, 0)),
                  pl.BlockSpec((8, c), lambda i: (jnp.maximum(i * r8 - 1, 0), 0)),
                  _full_spec(w.shape), _full_spec(b.shape)],
        out_specs=pl.BlockSpec((tb, c), lambda i: (i, 0)),
        out_shape=SDS((t, c), F32), compiler_params=_cparams(), name=name,
    )(x, x, w, b)


def _conv_bwd(x, w, dy, *, tb, name):
    t, c = x.shape
    r8 = tb // 8
    nb = t // tb

    def body(x_ref, p_ref, w_ref, g_ref, n_ref, dx_ref, dw_ref, db_ref):
        i = pl.program_id(0)
        x_ = x_ref[...]
        p_ = jnp.where(i > 0, p_ref[...], 0.0)
        g_ = g_ref[...]
        n_ = jnp.where(i < nb - 1, n_ref[...], 0.0)
        w_ = w_ref[...]
        row = lax.broadcasted_iota(jnp.int32, x_.shape, 0)
        row8 = lax.broadcasted_iota(jnp.int32, p_.shape, 0)
        g8 = g_[0:8, :]
        dx = g_ * w_[3:4, :]
        tail = jnp.zeros_like(n_)
        dws = [jnp.sum(g_ * x_, axis=0, keepdims=True)]
        for j in (1, 2, 3):
            wj = w_[3 - j:4 - j, :]
            dx += jnp.where(row < tb - j, pltpu.roll(g_, tb - j, 0), 0.0) * wj
            tail += jnp.where(row8 >= 8 - j, pltpu.roll(n_, 8 - j, 0), 0.0) * wj
            xs = jnp.where(row >= j, pltpu.roll(x_, j, 0), 0.0)
            ps = jnp.where(row8 < j, pltpu.roll(p_, j, 0), 0.0)
            dws.append(jnp.sum(g_ * xs, axis=0, keepdims=True) + jnp.sum(g8 * ps, axis=0, keepdims=True))
        dx_ref[...] = dx
        dx_ref[tb - 8:tb, :] += tail

        @pl.when(i == 0)
        def _():
            dw_ref[...] = jnp.zeros_like(dw_ref)
            db_ref[...] = jnp.zeros_like(db_ref)

        for j in range(4):
            dw_ref[3 - j:4 - j, :] += dws[j]
        db_ref[...] += jnp.sum(g_, axis=0, keepdims=True)

    return pl.pallas_call(
        body, grid=(nb,),
        in_specs=[pl.BlockSpec((tb, c), lambda i: (i, 0)),
                  pl.BlockSpec((8, c), lambda i: (jnp.maximum(i * r8 - 1, 0), 0)),
                  _full_spec(w.shape),
                  pl.BlockSpec((tb, c), lambda i: (i, 0)),
                  pl.BlockSpec((8, c), lambda i: (jnp.minimum((i + 1) * r8, t // 8 - 1), 0))],
        out_specs=[pl.BlockSpec((tb, c), lambda i: (i, 0)), _full_spec((8, c)), _full_spec((1, c))],
        out_shape=[SDS((t, c), F32), SDS((8, c), F32), SDS((1, c), F32)],
        compiler_params=_cparams(), name=name,
    )(x, x, w, dy, dy)


def _lru_scan_fwd(a, b, *, tb, name):
    t, c = a.shape

    def body(a_ref, b_ref, h_ref, st_ref):
        @pl.when(pl.program_id(0) == 0)
        def _():
            st_ref[...] = jnp.zeros_like(st_ref)

        def step(s, h):
            h = a_ref[pl.ds(s, 1), :] * h + b_ref[pl.ds(s, 1), :]
            h_ref[pl.ds(s, 1), :] = h
            return h

        st_ref[...] = lax.fori_loop(0, tb, step, st_ref[...], unroll=8)

    blk = pl.BlockSpec((tb, c), lambda i: (i, 0))
    return pl.pallas_call(
        body, grid=(t // tb,), in_specs=[blk, blk], out_specs=blk, out_shape=SDS((t, c), F32),
        scratch_shapes=[pltpu.VMEM((1, c), F32)], compiler_params=_cparams(), name=name,
    )(a, b)


def _lru_scan_bwd(a, h, dh, *, tb, name):
    t, c = a.shape
    nb = t // tb
    r8 = tb // 8

    def body(a_ref, h_ref, p_ref, g_ref, da_ref, db_ref, st_ref):
        i = pl.program_id(0)

        @pl.when(i == 0)
        def _():
            st_ref[...] = jnp.zeros_like(st_ref)

        hprev0 = jnp.where(i < nb - 1, p_ref[7:8, :], 0.0)

        def step(q, carry):
            s = tb - 1 - q
            g = g_ref[pl.ds(s, 1), :] + carry
            hp = h_ref[pl.ds(jnp.maximum(s - 1, 0), 1), :]
            hp = jnp.where(s > 0, hp, hprev0)
            db_ref[pl.ds(s, 1), :] = g
            da_ref[pl.ds(s, 1), :] = g * hp
            return a_ref[pl.ds(s, 1), :] * g

        st_ref[...] = lax.fori_loop(0, tb, step, st_ref[...], unroll=8)

    rev = pl.BlockSpec((tb, c), lambda i: (nb - 1 - i, 0))
    prev = pl.BlockSpec((8, c), lambda i: (jnp.maximum((nb - 1 - i) * r8 - 1, 0), 0))
    return pl.pallas_call(
        body, grid=(nb,), in_specs=[rev, rev, prev, rev], out_specs=[rev, rev],
        out_shape=[SDS((t, c), F32), SDS((t, c), F32)],
        scratch_shapes=[pltpu.VMEM((1, c), F32)], compiler_params=_cparams(), name=name,
    )(a, h, h, dh)


def _s5_scan_fwd(ar, ai, br, bi, *, tb, name):
    t, c = br.shape

    def body(ar_ref, ai_ref, br_ref, bi_ref, xr_ref, xi_ref, sr_ref, si_ref):
        @pl.when(pl.program_id(0) == 0)
        def _():
            sr_ref[...] = jnp.zeros_like(sr_ref)
            si_ref[...] = jnp.zeros_like(si_ref)

        ar_, ai_ = ar_ref[...], ai_ref[...]

        def step(s, carry):
            xr, xi = carry
            nr = ar_ * xr - ai_ * xi + br_ref[pl.ds(s, 1), :]
            ni = ar_ * xi + ai_ * xr + bi_ref[pl.ds(s, 1), :]
            xr_ref[pl.ds(s, 1), :] = nr
            xi_ref[pl.ds(s, 1), :] = ni
            return nr, ni

        xr, xi = lax.fori_loop(0, tb, step, (sr_ref[...], si_ref[...]), unroll=8)
        sr_ref[...] = xr
        si_ref[...] = xi

    blk = pl.BlockSpec((tb, c), lambda i: (i, 0))
    one = _full_spec((1, c))
    return pl.pallas_call(
        body, grid=(t // tb,), in_specs=[one, one, blk, blk], out_specs=[blk, blk],
        out_shape=[SDS((t, c), F32), SDS((t, c), F32)],
        scratch_shapes=[pltpu.VMEM((1, c), F32), pltpu.VMEM((1, c), F32)], compiler_params=_cparams(), name=name,
    )(ar, ai, br, bi)


def _s5_scan_bwd(ar, ai, xr, xi, dxr, dxi, *, tb, name):
    t, c = xr.shape
    nb = t // tb
    r8 = tb // 8

    def body(ar_ref, ai_ref, xr_ref, xi_ref, pr_ref, pi_ref, gr_ref, gi_ref,
             dbr_ref, dbi_ref, dar_ref, dai_ref, cr_ref, ci_ref):
        i = pl.program_id(0)

        @pl.when(i == 0)
        def _():
            cr_ref[...] = jnp.zeros_like(cr_ref)
            ci_ref[...] = jnp.zeros_like(ci_ref)
            dar_ref[...] = jnp.zeros_like(dar_ref)
            dai_ref[...] = jnp.zeros_like(dai_ref)

        ar_, ai_ = ar_ref[...], ai_ref[...]
        first = i == nb - 1
        pr0 = jnp.where(first, 0.0, pr_ref[7:8, :])
        pi0 = jnp.where(first, 0.0, pi_ref[7:8, :])

        def step(q, carry):
            cr, ci, dar, dai = carry
            s = tb - 1 - q
            gr = gr_ref[pl.ds(s, 1), :] + cr
            gi = gi_ref[pl.ds(s, 1), :] + ci
            sp = jnp.maximum(s - 1, 0)
            xpr = jnp.where(s > 0, xr_ref[pl.ds(sp, 1), :], pr0)
            xpi = jnp.where(s > 0, xi_ref[pl.ds(sp, 1), :], pi0)
            dbr_ref[pl.ds(s, 1), :] = gr
            dbi_ref[pl.ds(s, 1), :] = gi
            dar = dar + gr * xpr + gi * xpi
            dai = dai - gr * xpi + gi * xpr
            return ar_ * gr + ai_ * gi, ar_ * gi - ai_ * gr, dar, dai

        cr, ci, dar, dai = lax.fori_loop(0, tb, step, (cr_ref[...], ci_ref[...], dar_ref[...], dai_ref[...]), unroll=8)
        cr_ref[...] = cr
        ci_ref[...] = ci
        dar_ref[...] = dar
        dai_ref[...] = dai

    rev = pl.BlockSpec((tb, c), lambda i: (nb - 1 - i, 0))
    prev = pl.BlockSpec((8, c), lambda i: (jnp.maximum((nb - 1 - i) * r8 - 1, 0), 0))
    one = _full_spec((1, c))
    return pl.pallas_call(
        body, grid=(nb,), in_specs=[one, one, rev, rev, prev, prev, rev, rev], out_specs=[rev, rev, one, one],
        out_shape=[SDS((t, c), F32), SDS((t, c), F32), SDS((1, c), F32), SDS((1, c), F32)],
        scratch_shapes=[pltpu.VMEM((1, c), F32), pltpu.VMEM((1, c), F32)], compiler_params=_cparams(), name=name,
    )(ar, ai, xr, xi, xr, xi, dxr, dxi)


RW_PAIRS = RW_H // 2


def _pair_consts():
    sub = lax.broadcasted_iota(jnp.int32, (64, 128), 0)
    lane = lax.broadcasted_iota(jnp.int32, (64, 128), 1)
    eye2 = ((lane & 63) == sub).astype(F32)
    r2 = lax.broadcasted_iota(jnp.int32, (128, 128), 0)
    c2 = lax.broadcasted_iota(jnp.int32, (128, 128), 1)
    bsel = ((r2 >> 6) == (c2 >> 6)).astype(BF16)
    return eye2, bsel


def _segsum(x, bsel):
    bits = lax.bitcast_convert_type(x, jnp.int32)
    hi = lax.bitcast_convert_type(bits & jnp.int32(-65536), F32)
    lo = (x - hi).astype(BF16)
    return (jnp.dot(hi.astype(BF16), bsel, preferred_element_type=F32) + jnp.dot(lo, bsel, preferred_element_type=F32))


def _bc(x8):
    return jnp.stack([jnp.broadcast_to(x8[q:q + 1, :], (64, 128)) for q in range(RW_PAIRS)])


def _seg3(x3, bsel):
    return _segsum(x3.reshape(RW_PAIRS * 64, 128), bsel).reshape(RW_PAIRS, 64, 128)


def _rwkv_scan_fwd(r, w, k, v, kk, a, *, lc, name):
    t = r.shape[0]
    nc = t // lc

    def body(r_ref, w_ref, k_ref, v_ref, kk_ref, a_ref, y_ref, ck_ref, st_ref, vb_ref, hist_ref):
        @pl.when(pl.program_id(0) == 0)
        def _():
            st_ref[...] = jnp.zeros_like(st_ref)

        ck_ref[0] = st_ref[...]
        eye2, bsel = _pair_consts()

        def values(s, c):
            vb_ref[s] = _seg3(eye2[None] * _bc(v_ref[s]), bsel)
            return c

        lax.fori_loop(0, lc, values, 0, unroll=2)

        def step(s, c):
            st = st_ref[...]
            kk8 = kk_ref[s]
            sa = -_seg3(st * _bc(kk8), bsel)
            st = st * _bc(w_ref[s]) + sa * _bc(kk8 * a_ref[s]) + vb_ref[s] * _bc(k_ref[s])
            st_ref[...] = st
            hist_ref[s] = st
            return c

        lax.fori_loop(0, lc, step, 0)

        def read(s, c):
            yb = _seg3(hist_ref[s] * _bc(r_ref[s]), bsel)
            y_ref[s] = jnp.sum(eye2[None] * yb, axis=1)
            return c

        lax.fori_loop(0, lc, read, 0, unroll=2)

    blk = pl.BlockSpec((lc, RW_PAIRS, 128), lambda i: (i, 0, 0))
    return pl.pallas_call(
        body, grid=(nc,), in_specs=[blk] * 6,
        out_specs=[blk, pl.BlockSpec((1, RW_PAIRS, 64, 128), lambda i: (i, 0, 0, 0))],
        out_shape=[SDS((t, RW_PAIRS, 128), F32), SDS((nc, RW_PAIRS, 64, 128), F32)],
        scratch_shapes=[pltpu.VMEM((RW_PAIRS, 64, 128), F32), pltpu.VMEM((lc, RW_PAIRS, 64, 128), F32),
                        pltpu.VMEM((lc, RW_PAIRS, 64, 128), F32)],
        compiler_params=_cparams(), name=name,
    )(r, w, k, v, kk, a)


def _rwkv_scan_bwd(r, w, k, v, kk, a, ck, dy, *, lc, name):
    t = r.shape[0]
    nc = t // lc

    def body(r_ref, w_ref, k_ref, v_ref, kk_ref, a_ref, ck_ref, dy_ref,
             dr_ref, dw_ref, dk_ref, dv_ref, dkk_ref, da_ref,
             ds_ref, vb_ref, dyb_ref, hist_ref, sa_ref, dsh_ref, dsa_ref):
        @pl.when(pl.program_id(0) == 0)
        def _():
            ds_ref[...] = jnp.zeros_like(ds_ref)

        eye2, bsel = _pair_consts()

        def columns(s, c):
            vb_ref[s] = _seg3(eye2[None] * _bc(v_ref[s]), bsel)
            dyb_ref[s] = _seg3(eye2[None] * _bc(dy_ref[s]), bsel)
            return c

        lax.fori_loop(0, lc, columns, 0, unroll=2)
        hist_ref[0] = ck_ref[0]

        def fwd(s, c):
            st = hist_ref[s]
            kk8 = kk_ref[s]
            sa = -_seg3(st * _bc(kk8), bsel)
            sa_ref[s] = sa
            hist_ref[s + 1] = st * _bc(w_ref[s]) + sa * _bc(kk8 * a_ref[s]) + vb_ref[s] * _bc(k_ref[s])
            return c

        lax.fori_loop(0, lc, fwd, 0)

        def back(j, c):
            s = lc - 1 - j
            kk8 = kk_ref[s]
            d_s = ds_ref[...] + dyb_ref[s] * _bc(r_ref[s])
            dsa = _seg3(d_s * _bc(kk8 * a_ref[s]), bsel)
            dsh_ref[s] = d_s
            dsa_ref[s] = dsa
            ds_ref[...] = d_s * _bc(w_ref[s]) - dsa * _bc(kk8)
            return c

        lax.fori_loop(0, lc, back, 0)

        def grads(s, c):
            s_prev, s_cur, d_s, dsa = hist_ref[s], hist_ref[s + 1], dsh_ref[s], dsa_ref[s]
            col = lambda z: jnp.sum(z, axis=1)
            db = col(d_s * sa_ref[s])
            dr_ref[s] = col(s_cur * dyb_ref[s])
            dw_ref[s] = col(d_s * s_prev)
            dv_ref[s] = col(eye2[None] * _seg3(d_s * _bc(k_ref[s]), bsel))
            dk_ref[s] = col(d_s * vb_ref[s])
            dkk_ref[s] = db * a_ref[s] - col(s_prev * dsa)
            da_ref[s] = db * kk_ref[s]
            return c

        lax.fori_loop(0, lc, grads, 0)

    rev = pl.BlockSpec((lc, RW_PAIRS, 128), lambda i: (nc - 1 - i, 0, 0))
    big = lambda n: pltpu.VMEM((n, RW_PAIRS, 64, 128), F32)
    return pl.pallas_call(
        body, grid=(nc,),
        in_specs=[rev] * 6 + [pl.BlockSpec((1, RW_PAIRS, 64, 128), lambda i: (nc - 1 - i, 0, 0, 0)), rev],
        out_specs=[rev] * 6, out_shape=[SDS((t, RW_PAIRS, 128), F32)] * 6,
        scratch_shapes=[pltpu.VMEM((RW_PAIRS, 64, 128), F32), big(lc), big(lc), big(lc + 1), big(lc), big(lc), big(lc)],
        compiler_params=_cparams(), name=name,
    )(r, w, k, v, kk, a, ck, dy)


SSD_PAIRS = SSD_H // 2


def _ssd_chunk(states, xdt, da, bm, cm):
    ln = SSD_L
    row = lax.broadcasted_iota(jnp.int32, (ln, ln), 0)
    col = lax.broadcasted_iota(jnp.int32, (ln, ln), 1)
    causal = row >= col
    acum = _dot32(causal.astype(F32), da)
    acum_t = _dot32(da, (row <= col).astype(F32), TN)
    sub = lax.broadcasted_iota(jnp.int32, (128, 128), 0)
    lane = lax.broadcasted_iota(jnp.int32, (128, 128), 1)
    ys, new_states = [], []
    for q in range(SSD_PAIRS):
        g = q // (SSD_PAIRS // SSD_NG)
        bg = bm[:, g * SSD_N:(g + 1) * SSD_N]
        cg = cm[:, g * SSD_N:(g + 1) * SSD_N]
        xq = xdt[:, q * 128:(q + 1) * 128]
        scores = _dot16(cg, bg, NT)
        aexp = _dot32(acum, (sub == 2 * q + (lane >> 6)).astype(F32))
        tot = aexp[ln - 1:ln, :]
        yh = []
        for h in (2 * q, 2 * q + 1):
            seg = _dot32(acum, (sub == h).astype(F32)) - acum_t[h:h + 1, :]
            yh.append(_dot16(scores * jnp.exp(jnp.where(causal, seg, -1e30)), xq))
        y = jnp.where(lane < 64, yh[0], yh[1]) + _dot16(cg, states[q]) * jnp.exp(aexp)
        new = _dot16(bg, xq * jnp.exp(tot - aexp), TN)
        ys.append(y)
        new_states.append(states[q] * jnp.exp(tot) + new)
    return jnp.concatenate(ys, axis=1), new_states


def _ssd_fwd(xdt, da, bm, cm, *, name):
    t = xdt.shape[0]
    nc = t // SSD_L

    def body(x_ref, a_ref, b_ref, c_ref, y_ref, ck_ref, st_ref):
        @pl.when(pl.program_id(0) == 0)
        def _():
            st_ref[...] = jnp.zeros_like(st_ref)

        ck_ref[0] = st_ref[...]
        y, new = _ssd_chunk([st_ref[q] for q in range(SSD_PAIRS)], x_ref[...], a_ref[...], b_ref[...], c_ref[...])
        y_ref[...] = y
        for q in range(SSD_PAIRS):
            st_ref[q] = new[q]

    blk = lambda wd: pl.BlockSpec((SSD_L, wd), lambda i: (i, 0))
    return pl.pallas_call(
        body, grid=(nc,), in_specs=[blk(SSD_W), blk(128), blk(512), blk(512)],
        out_specs=[blk(SSD_W), pl.BlockSpec((1, SSD_PAIRS, 128, 128), lambda i: (i, 0, 0, 0))],
        out_shape=[SDS((t, SSD_W), F32), SDS((nc, SSD_PAIRS, 128, 128), F32)],
        scratch_shapes=[pltpu.VMEM((SSD_PAIRS, 128, 128), F32)], compiler_params=_cparams(), name=name,
    )(xdt, da, bm, cm)


def _ssd_bwd(xdt, da, bm, cm, ck, dy, *, name):
    t = xdt.shape[0]
    nc = t // SSD_L

    def body(x_ref, a_ref, b_ref, c_ref, ck_ref, dy_ref, dx_ref, dda_ref, db_ref, dc_ref, ds_ref):
        @pl.when(pl.program_id(0) == 0)
        def _():
            ds_ref[...] = jnp.zeros_like(ds_ref)

        _, pull = jax.vjp(_ssd_chunk, [ck_ref[0, q] for q in range(SSD_PAIRS)], x_ref[...], a_ref[...], b_ref[...], c_ref[...])
        dst, dx, dda, db, dc = pull((dy_ref[...], [ds_ref[q] for q in range(SSD_PAIRS)]))
        dx_ref[...] = dx
        dda_ref[...] = dda
        db_ref[...] = db
        dc_ref[...] = dc
        for q in range(SSD_PAIRS):
            ds_ref[q] = dst[q]

    rev = lambda wd: pl.BlockSpec((SSD_L, wd), lambda i: (nc - 1 - i, 0))
    return pl.pallas_call(
        body, grid=(nc,),
        in_specs=[rev(SSD_W), rev(128), rev(512), rev(512),
                  pl.BlockSpec((1, SSD_PAIRS, 128, 128), lambda i: (nc - 1 - i, 0, 0, 0)), rev(SSD_W)],
        out_specs=[rev(SSD_W), rev(128), rev(512), rev(512)],
        out_shape=[SDS((t, SSD_W), F32), SDS((t, 128), F32), SDS((t, 512), F32), SDS((t, 512), F32)],
        scratch_shapes=[pltpu.VMEM((SSD_PAIRS, 128, 128), F32)], compiler_params=_cparams(), name=name,
    )(xdt, da, bm, cm, ck, dy)


def _iota(shape, dim):
    return lax.broadcasted_iota(jnp.int32, shape, dim)


def _rms(x, g):
    return x * lax.rsqrt(jnp.mean(x * x, axis=-1, keepdims=True) + EPS) * g


def _head_sel(width, shift):
    return ((_iota((width, 128), 0) >> shift) == _iota((width, 128), 1)).astype(F32)


def _head_sum(x, shift=6):
    sel = _head_sel(x.shape[1], shift)
    return _dot32(_dot32(x, sel), sel, NT)


def _head_expand(x, width, shift=6):
    return _dot32(x, _head_sel(width, shift), NT)


def f_norm(h, g):
    return (_rms(h, g),)


def f_norm_pass(h, g):
    return _rms(h, g), h


def f_add_norm(h, m, g):
    h1 = h + m
    return h1, _rms(h1, g)


def f_relu2(u):
    r = jnp.maximum(u, 0.0)
    return (r * r,)


def f_plgate(h2, gl, pp):
    return (h2 + jax.nn.sigmoid(gl) * pp,)


def f_loss(h, tgt, g):
    err = _rms(h, g) - tgt
    part = 0.5 * jnp.sum(jnp.mean(err * err, axis=-1, keepdims=True), axis=0, keepdims=True)
    return (jnp.broadcast_to(part, (8, 128)),)


def f_s5_prep(lam_re, lam_im, lstep, bre_t, bim_t, cre_t, cim_t):
    step = jnp.exp(_dot32(lstep, _head_sel(S5_N, 6), NT)[0:1, :])
    mag = jnp.exp(lam_re * step)
    abar_re, abar_im = mag * jnp.cos(lam_im * step), mag * jnp.sin(lam_im * step)
    den = lam_re * lam_re + lam_im * lam_im
    nr = abar_re - 1.0
    coef_re = (nr * lam_re + abar_im * lam_im) / den
    coef_im = (abar_im * lam_re - nr * lam_im) / den
    bbar_re = coef_re * bre_t - coef_im * bim_t
    bbar_im = coef_re * bim_t + coef_im * bre_t
    rep = ((_iota((S5_W, S5_G), 0) & (S5_G - 1)) == _iota((S5_W, S5_G), 1)).astype(F32)
    blk = ((_iota((S5_W, S5_N), 0) >> 4) == (_iota((S5_W, S5_N), 1) >> 6)).astype(F32)
    blk_t = ((_iota((S5_N, S5_W), 0) >> 6) == (_iota((S5_N, S5_W), 1) >> 4)).astype(F32)
    wb_re, wb_im = _dot32(rep, bbar_re) * blk, _dot32(rep, bbar_im) * blk
    wc_re, wc_im = _dot32(cre_t, rep, NT) * blk_t, _dot32(cim_t, rep, NT) * blk_t
    return abar_re, abar_im, wb_re, wb_im, wc_re, wc_im


def f_s5_post(xr, xi, u, wc_re, wc_im, d_skip, glu_w, glu_b):
    y = _dot16(xr, wc_re) - _dot16(xi, wc_im) + d_skip * u
    act = jax.nn.gelu(y)
    return (act * jax.nn.sigmoid(_dot16(act, glu_w) + glu_b),)


def f_ssd_pre(xc, dtr, dt_bias, a_log):
    act = jax.nn.silu(xc)
    heads = _iota(dtr.shape, 1) < SSD_H
    dt = jnp.where(heads, jax.nn.softplus(dtr + dt_bias), 0.0)
    da = dt * (-jnp.exp(a_log))
    xdt = act[:, :SSD_W] * _head_expand(dt, SSD_W)
    return xdt, da, act[:, SSD_W:SSD_W + 512], act[:, SSD_W + 512:]


def f_ssd_pre_pass(xc, dtr, dt_bias, a_log):
    return f_ssd_pre(xc, dtr, dt_bias, a_log) + (xc,)


def f_ssd_post(y, xc, z, d_skip, norm_g):
    xs = jax.nn.silu(xc[:, :SSD_W])
    y = (y + xs * _head_expand(d_skip, SSD_W)) * jax.nn.silu(z)
    gw = SSD_W // SSD_NG
    parts = []
    for g in range(SSD_NG):
        seg = y[:, g * gw:(g + 1) * gw]
        parts.append(seg * lax.rsqrt(jnp.mean(seg * seg, axis=-1, keepdims=True) + EPS))
    return (jnp.concatenate(parts, axis=1) * norm_g,)


def f_rwkv_pre(f, w0, w_up, a0, a_up, g_up, k_k, k_a):
    r, k, v = f[:, 0:1024], f[:, 1024:2048], f[:, 2048:3072]
    wl, al, gl = f[:, 3072:3200], f[:, 3200:3328], f[:, 3328:3584]
    w = -jax.nn.softplus(-(w0 + _dot16(jnp.tanh(wl), w_up))) - 0.5
    decay = jnp.exp(-jnp.exp(w))
    a = jax.nn.sigmoid(a0 + _dot16(al, a_up))
    g = _dot16(jax.nn.sigmoid(gl), g_up)
    kk = k * k_k
    k2 = k * (1.0 + (a - 1.0) * k_a)
    kkn = kk * lax.rsqrt(jnp.maximum(_head_sum(kk * kk), 1e-24))
    return r, decay, k2, v, kkn, a, g


def f_rwkv_pre_pass(f, w0, w_up, a0, a_up, g_up, k_k, k_a):
    out = f_rwkv_pre(f, w0, w_up, a0, a_up, g_up, k_k, k_a)
    return out + (out[0], out[2], out[3])


def f_rwkv_post(y, r, k2, v, g, ln_g, ln_b, r_k):
    mean = _head_sum(y) * (1.0 / RW_HD)
    yc = y - mean
    var = _head_sum(yc * yc) * (1.0 / RW_HD)
    yn = yc * lax.rsqrt(var + GN_EPS) * ln_g + ln_b
    bonus = _head_sum(r * k2 * r_k) * v
    return ((yn + bonus) * g,)


def _neg_expm1(y):
    series = -y * (1.0 + y * (0.5 + y * (1.0 / 6.0 + y * (1.0 / 24.0 + y * (1.0 / 120.0)))))
    return jnp.where(y > -0.1, series, 1.0 - jnp.exp(y))


def f_lru_pre(t0, xc, w_a, b_a, w_x, b_x, lam):
    gate_r = jax.nn.sigmoid(_dot16(xc, w_a) + b_a)
    gate_i = jax.nn.sigmoid(_dot16(xc, w_x) + b_x)
    log_a = -LRU_C * gate_r * jax.nn.softplus(-lam)
    mult = jnp.sqrt(jnp.maximum(_neg_expm1(2.0 * log_a), 0.0))
    mult = jnp.where(_iota(xc.shape, 0) + t0 == 0, 1.0, mult)
    return jnp.exp(log_a), xc * gate_i * mult


def f_lru_post(h, gl):
    return (h * jax.nn.gelu(gl),)


TB = 256
TBH = 128
SCAN_TB = 256
RW_LC = 16


def _even_fwd(hn, w, tag):
    n = lambda s: f"{tag}_{s}"
    u = _mm(hn, w["in_u"], name=n("proj_u"))
    z = _mm(hn, w["in_z"], name=n("proj_z"))
    xbc = _mm(hn, w["in_xbc"], name=n("proj_xbc"))
    dtr = _mm(hn, w["in_dt"], name=n("proj_dt"))
    bu_re = _mm(u, w["wb_re"], name=n("s5_bu_re"))
    bu_im = _mm(u, w["wb_im"], name=n("s5_bu_im"))
    xr, xi = _s5_scan_fwd(w["abar_re"], w["abar_im"], bu_re, bu_im, tb=SCAN_TB, name=n("s5_scan"))
    s5c = [w["wc_re"], w["wc_im"], w["s5_d"], w["glu_w"], w["glu_b"]]
    (ya,) = _stage(f_s5_post, [xr, xi, u], s5c, tb=TB, name=n("s5_post"), out_dtypes=[BF16])
    xc = _conv_fwd(xbc, w["ssd_conv_w"], w["ssd_conv_b"], tb=TB, name=n("ssd_conv"))
    xdt, da, bm, cm = _stage(f_ssd_pre, [xc, dtr], [w["dt_bias"], w["a_log"]], tb=TB, name=n("ssd_pre"),
                             out_dtypes=[F32] * 4)
    y, ck = _ssd_fwd(xdt, da, bm, cm, name=n("ssd_scan"))
    (yb,) = _stage(f_ssd_post, [y, xc, z], [w["ssd_d"], w["ssd_norm"]], tb=TB, name=n("ssd_post"), out_dtypes=[BF16])
    mo = _mm(ya, w["out_a"], name=n("out_a"))
    mo = _mm(yb, w["out_b"], add=mo, name=n("out_b"))
    res = dict(u=u, z=z, xbc=xbc, dtr=dtr, xr=xr, xi=xi, ya=ya, xc=xc, xdt=xdt, da=da, bm=bm, cm=cm, y=y, ck=ck, yb=yb)
    return mo, res


def _even_bwd(dmo, hn, w, r, tag):
    n = lambda s: f"{tag}_{s}"
    g = {}
    g["out_a"] = _mm(r["ya"], dmo, ta=True, name=n("d_out_a"))
    g["out_b"] = _mm(r["yb"], dmo, ta=True, name=n("d_out_b"))
    dya = _mm(dmo, w["out_a"], tb=True, name=n("dya"))
    dyb = _mm(dmo, w["out_b"], tb=True, name=n("dyb"))
    dy, dxc1, dz, g["ssd_d"], g["ssd_norm"] = _stage_vjp(
        f_ssd_post, [r["y"], r["xc"], r["z"]], [w["ssd_d"], w["ssd_norm"]], [dyb], tb=TBH, name=n("ssd_post_b"),
        drow=[0, 1, 2], dconst=[0, 1])
    dxdt, dda, dbm, dcm = _ssd_bwd(r["xdt"], r["da"], r["bm"], r["cm"], r["ck"], dy, name=n("ssd_scan_b"))
    dxc, ddtr, g["dt_bias"], g["a_log"] = _stage_vjp(
        f_ssd_pre_pass, [r["xc"], r["dtr"]], [w["dt_bias"], w["a_log"]], [dxdt, dda, dbm, dcm, dxc1], tb=TBH,
        name=n("ssd_pre_b"), drow=[0, 1], dconst=[0, 1])
    dxbc, g["ssd_conv_w"], g["ssd_conv_b"] = _conv_bwd(r["xbc"], w["ssd_conv_w"], dxc, tb=TB, name=n("ssd_conv_b"))
    s5c = [w["wc_re"], w["wc_im"], w["s5_d"], w["glu_w"], w["glu_b"]]
    dxr, dxi, du1, g["wc_re"], g["wc_im"], g["s5_d"], g["glu_w"], g["glu_b"] = _stage_vjp(
        f_s5_post, [r["xr"], r["xi"], r["u"]], s5c, [dya], tb=TBH, name=n("s5_post_b"),
        drow=[0, 1, 2], dconst=[0, 1, 2, 3, 4])
    dbr, dbi, g["abar_re"], g["abar_im"] = _s5_scan_bwd(w["abar_re"], w["abar_im"], r["xr"], r["xi"], dxr, dxi,
                                                         tb=SCAN_TB, name=n("s5_scan_b"))
    g["wb_re"] = _mm(r["u"], dbr, ta=True, name=n("d_wb_re"))
    g["wb_im"] = _mm(r["u"], dbi, ta=True, name=n("d_wb_im"))
    du = _mm(dbr, w["wb_re"], tb=True, add=du1, name=n("du_re"))
    du = _mm(dbi, w["wb_im"], tb=True, add=du, name=n("du_im"))
    segs = (("in_u", du), ("in_z", dz), ("in_xbc", dxbc), ("in_dt", ddtr))
    dhn = None
    for key, dseg in segs:
        g[key] = _mm(hn, dseg, ta=True, name=n("d_" + key))
        dhn = _mm(dseg, w[key], tb=True, add=dhn, name=n("dhn_" + key))
    return dhn, g


def _odd_fwd(hn, w, tag):
    n = lambda s: f"{tag}_{s}"
    rw = _mm(hn, w["in_rw"], name=n("proj_rw"))
    xl = _mm(hn, w["in_xl"], name=n("proj_xl"))
    gl = _mm(hn, w["in_gl"], name=n("proj_gl"))
    f = _conv_fwd(rw, w["mix_w"], w["mix_b"], tb=TB, name=n("rwkv_shift"))
    rc = [w[k] for k in ("w0", "w_up", "a0", "a_up", "g_up", "k_k", "k_a")]
    r_, dec, k2, v, kkn, a, gate = _stage(f_rwkv_pre, [f], rc, tb=TB, name=n("rwkv_pre"), out_dtypes=[F32] * 7)
    t3 = lambda z: z.reshape(-1, RW_PAIRS, 128)
    y, ck = _rwkv_scan_fwd(t3(r_), t3(dec), t3(k2), t3(v), t3(kkn), t3(a), lc=RW_LC, name=n("rwkv_scan"))
    y = y.reshape(-1, RW_W)
    (yc,) = _stage(f_rwkv_post, [y, r_, k2, v, gate], [w["ln_g"], w["ln_b"], w["r_k"]], tb=TB, name=n("rwkv_post"),
                   out_dtypes=[BF16])
    xc = _conv_fwd(xl, w["lru_conv_w"], w["lru_conv_b"], tb=TB, name=n("lru_conv"))
    lc = [w[k] for k in ("lru_wa", "lru_b_a", "lru_wx", "lru_b_x", "lru_lam")]
    a_l, bx = _stage(f_lru_pre, [xc], lc, tb=TB, name=n("lru_pre"), out_dtypes=[F32] * 2, pos=True)
    h = _lru_scan_fwd(a_l, bx, tb=SCAN_TB, name=n("lru_scan"))
    (yd,) = _stage(f_lru_post, [h, gl], [], tb=TB, name=n("lru_post"), out_dtypes=[BF16])
    mo = _mm(yc, w["out_a"], name=n("out_a"))
    mo = _mm(yd, w["out_b"], add=mo, name=n("out_b"))
    res = dict(rw=rw, xl=xl, gl=gl, f=f, r=r_, dec=dec, k2=k2, v=v, kkn=kkn, a=a, gate=gate, y=y, ck=ck, yc=yc,
               xc=xc, a_l=a_l, h=h, yd=yd)
    return mo, res


def _odd_bwd(dmo, hn, w, r, tag):
    n = lambda s: f"{tag}_{s}"
    g = {}
    g["out_a"] = _mm(r["yc"], dmo, ta=True, name=n("d_out_a"))
    g["out_b"] = _mm(r["yd"], dmo, ta=True, name=n("d_out_b"))
    dyc = _mm(dmo, w["out_a"], tb=True, name=n("dyc"))
    dyd = _mm(dmo, w["out_b"], tb=True, name=n("dyd"))
    dh, dgl = _stage_vjp(f_lru_post, [r["h"], r["gl"]], [], [dyd], tb=TB, name=n("lru_post_b"), drow=[0, 1], dconst=[])
    da_l, dbx = _lru_scan_bwd(r["a_l"], r["h"], dh, tb=SCAN_TB, name=n("lru_scan_b"))
    lc = [w[k] for k in ("lru_wa", "lru_b_a", "lru_wx", "lru_b_x", "lru_lam")]
    dxc, g["lru_wa"], g["lru_b_a"], g["lru_wx"], g["lru_b_x"], g["lru_lam"] = _stage_vjp(
        f_lru_pre, [r["xc"]], lc, [da_l, dbx], tb=TBH, name=n("lru_pre_b"), drow=[0], dconst=[0, 1, 2, 3, 4], pos=True)
    dxl, g["lru_conv_w"], g["lru_conv_b"] = _conv_bwd(r["xl"], w["lru_conv_w"], dxc, tb=TB, name=n("lru_conv_b"))
    dy, dr1, dk1, dv1, dgate, g["ln_g"], g["ln_b"], g["r_k"] = _stage_vjp(
        f_rwkv_post, [r["y"], r["r"], r["k2"], r["v"], r["gate"]], [w["ln_g"], w["ln_b"], w["r_k"]], [dyc], tb=TBH,
        name=n("rwkv_post_b"), drow=[0, 1, 2, 3, 4], dconst=[0, 1, 2])
    t3 = lambda z: z.reshape(-1, RW_PAIRS, 128)
    dr2, ddec, dk2, dv2, dkkn, da = [z.reshape(-1, RW_W) for z in _rwkv_scan_bwd(
        t3(r["r"]), t3(r["dec"]), t3(r["k2"]), t3(r["v"]), t3(r["kkn"]), t3(r["a"]), r["ck"], t3(dy),
        lc=RW_LC, name=n("rwkv_scan_b"))]
    rc = [w[k] for k in ("w0", "w_up", "a0", "a_up", "g_up", "k_k", "k_a")]
    df, g["w0"], g["w_up"], g["a0"], g["a_up"], g["g_up"], g["k_k"], g["k_a"] = _stage_vjp(
        f_rwkv_pre_pass, [r["f"]], rc, [dr2, ddec, dk2, dv2, dkkn, da, dgate, dr1, dk1, dv1], tb=TBH,
        name=n("rwkv_pre_b"), drow=[0], dconst=[0, 1, 2, 3, 4, 5, 6])
    drw, g["mix_w"], _ = _conv_bwd(r["rw"], w["mix_w"], df, tb=TB, name=n("rwkv_shift_b"))
    segs = (("in_rw", drw), ("in_xl", dxl), ("in_gl", dgl))
    dhn = None
    for key, dseg in segs:
        g[key] = _mm(hn, dseg, ta=True, name=n("d_" + key))
        dhn = _mm(dseg, w[key], tb=True, add=dhn, name=n("dhn_" + key))
    return dhn, g


def _layer_fwd(h, p_i, w, odd, tag):
    n = lambda s: f"{tag}_{s}"
    (hn,) = _stage(f_norm, [h], [w["norm_mix"]], tb=TB, name=n("norm_mix"), out_dtypes=[BF16])
    mo, mres = (_odd_fwd if odd else _even_fwd)(hn, w, tag)
    h1, hf = _stage(f_add_norm, [h, mo], [w["norm_ffn"]], tb=TB, name=n("norm_ffn"), out_dtypes=[F32, BF16])
    u = _mm(hf, w["mlp_w1"], name=n("mlp_up"))
    (act,) = _stage(f_relu2, [u], [], tb=TBH, name=n("mlp_act"), out_dtypes=[BF16])
    m2 = _mm(act, w["mlp_w2"], name=n("mlp_down"))
    h2, hp = _stage(f_add_norm, [h1, m2], [w["norm_pl"]], tb=TB, name=n("norm_pl"), out_dtypes=[F32, BF16])
    gl = _mm(hp, w["pl_gate"], name=n("pl_gate"))
    pp = _mm(p_i, w["pl_proj"], name=n("pl_proj"))
    (h3,) = _stage(f_plgate, [h2, gl, pp], [], tb=TB, name=n("pl_mix"), out_dtypes=[F32])
    res = dict(h=h, hn=hn, mo=mo, mix=mres, h1=h1, hf=hf, u=u, act=act, m2=m2, h2=h2, hp=hp, gl=gl, pp=pp)
    return h3, res


def _layer_bwd(dh3, p_i, w, r, odd, tag):
    n = lambda s: f"{tag}_{s}"
    g = {}
    dh2, dgl, dpp = _stage_vjp(f_plgate, [r["h2"], r["gl"], r["pp"]], [], [dh3], tb=TB, name=n("pl_mix_b"),
                               drow=[0, 1, 2], dconst=[])
    g["pl_proj"] = _mm(p_i, dpp, ta=True, name=n("d_pl_proj"))
    g["pl_gate"] = _mm(r["hp"], dgl, ta=True, name=n("d_pl_gate"))
    dhp = _mm(dgl, w["pl_gate"], tb=True, name=n("dhp"))
    dh1, dm2, g["norm_pl"] = _stage_vjp(f_add_norm, [r["h1"], r["m2"]], [w["norm_pl"]], [dh2, dhp], tb=TB,
                                        name=n("norm_pl_b"), drow=[0, 1], dconst=[0])
    g["mlp_w2"] = _mm(r["act"], dm2, ta=True, name=n("d_mlp_w2"))
    dact = _mm(dm2, w["mlp_w2"], tb=True, name=n("dact"))
    (du,) = _stage_vjp(f_relu2, [r["u"]], [], [dact], tb=TBH, name=n("mlp_act_b"), drow=[0], dconst=[],
                       drow_dtypes=[BF16])
    g["mlp_w1"] = _mm(r["hf"], du, ta=True, name=n("d_mlp_w1"))
    dhf = _mm(du, w["mlp_w1"], tb=True, name=n("dhf"))
    dh, dmo, g["norm_ffn"] = _stage_vjp(f_add_norm, [r["h"], r["mo"]], [w["norm_ffn"]], [dh1, dhf], tb=TB,
                                        name=n("norm_ffn_b"), drow=[0, 1], dconst=[0])
    dhn, gm = (_odd_bwd if odd else _even_bwd)(dmo, r["hn"], w, r["mix"], tag)
    g.update(gm)
    dh0, g["norm_mix"] = _stage_vjp(f_norm_pass, [r["h"]], [w["norm_mix"]], [dhn, dh], tb=TB, name=n("norm_mix_b"),
                                    drow=[0], dconst=[0])
    return dh0, g


def _pad_to(a, size, axis):
    pad = [(0, 0)] * a.ndim
    pad[axis] = (0, size - a.shape[axis])
    return jnp.pad(a, pad)


def _rw_pad(a):
    return jnp.concatenate([a[..., :3072], _pad_to(a[..., 3072:3168], 128, -1), _pad_to(a[..., 3168:3264], 128, -1),
                            a[..., 3264:3520]], axis=-1)


def _rw_unpad(a):
    return jnp.concatenate([a[..., :3072], a[..., 3072:3168], a[..., 3200:3296], a[..., 3328:3584]], axis=-1)


def _block_diag(w):
    nb, bs, _ = w.shape
    eye = jnp.eye(nb, dtype=w.dtype)
    return (w[:, :, None, :] * eye[:, None, :, None]).reshape(nb * bs, nb * bs)


def _diag_blocks(w):
    nb = LRU_B
    bs = w.shape[0] // nb
    return jnp.stack([w[h * bs:(h + 1) * bs, h * bs:(h + 1) * bs] for h in range(nb)])


def _s5_prep_inputs(fw):
    lstep = jnp.broadcast_to(_pad_to(fw["s5_log_step"].astype(F32), 128, 1), (8, 128))
    t16 = lambda b: jnp.transpose(b[0], (2, 0, 1)).reshape(S5_G, S5_N)
    tc = lambda c: jnp.transpose(c[0], (0, 2, 1)).reshape(S5_N, S5_G)
    return [fw["s5_lam_re"].reshape(1, S5_N), fw["s5_lam_im"].reshape(1, S5_N), lstep,
            t16(fw["s5_b_re"]), t16(fw["s5_b_im"]), tc(fw["s5_c_re"]), tc(fw["s5_c_im"])]


def _layer_weights(fw, i):
    w = {k: fw[k][i:i + 1] for k in ("norm_mix", "norm_ffn", "norm_pl")}
    for k in ("mlp_w1", "mlp_w2", "pl_proj", "pl_gate"):
        w[k] = fw[k][i]
    return w


def _even_weights(fw, prep):
    w = _layer_weights(fw, 0)
    ein, eout = fw["e_in_proj"][0], fw["e_out_proj"][0]
    w.update(in_u=ein[:, :512], in_z=ein[:, 512:2048], in_xbc=ein[:, 2048:4608], in_dt=_pad_to(ein[:, 4608:], 128, 1),
             out_a=eout[:512], out_b=eout[512:])
    abar_re, abar_im, wb_re, wb_im, wc_re, wc_im = prep
    w.update(abar_re=abar_re, abar_im=abar_im, wb_re=wb_re, wb_im=wb_im, wc_re=wc_re.astype(BF16), wc_im=wc_im.astype(BF16),
             s5_d=fw["s5_d"], glu_w=fw["s5_glu_w"][0], glu_b=fw["s5_glu_b"],
             ssd_conv_w=_pad_to(fw["ssd_conv_w"][0], 8, 0), ssd_conv_b=fw["ssd_conv_b"],
             dt_bias=_pad_to(fw["ssd_dt_bias"], 128, 1), a_log=_pad_to(fw["ssd_a_log"], 128, 1),
             ssd_d=_pad_to(fw["ssd_d"], 128, 1), ssd_norm=fw["ssd_norm"])
    return w


def _odd_weights(fw):
    w = _layer_weights(fw, 1)
    oin, oout = fw["o_in_proj"][0], fw["o_out_proj"][0]
    mu = _rw_pad(fw["rwkv_mu"])
    zero = jnp.zeros_like(mu)
    w.update(in_rw=_rw_pad(oin[:, :RW_IN]), in_xl=oin[:, RW_IN:RW_IN + LRU_W], in_gl=oin[:, RW_IN + LRU_W:],
             out_a=oout[:RW_W], out_b=oout[RW_W:],
             mix_w=jnp.concatenate([zero, zero, mu, 1.0 - mu, zero, zero, zero, zero], axis=0), mix_b=zero,
             w0=fw["rwkv_w0"], w_up=_pad_to(fw["rwkv_w_up"][0], 128, 0), a0=fw["rwkv_a0"],
             a_up=_pad_to(fw["rwkv_a_up"][0], 128, 0), g_up=fw["rwkv_g_up"][0], k_k=fw["rwkv_k_k"], k_a=fw["rwkv_k_a"],
             r_k=fw["rwkv_r_k"].reshape(1, RW_W), ln_g=fw["rwkv_ln_g"], ln_b=fw["rwkv_ln_b"],
             lru_conv_w=_pad_to(fw["lru_conv_w"][0], 8, 0), lru_conv_b=fw["lru_conv_b"],
             lru_wa=_block_diag(fw["lru_w_a"][0]).astype(BF16), lru_b_a=fw["lru_b_a"].reshape(1, LRU_W),
             lru_wx=_block_diag(fw["lru_w_x"][0]).astype(BF16), lru_b_x=fw["lru_b_x"].reshape(1, LRU_W),
             lru_lam=fw["lru_lam"].reshape(1, LRU_W))
    return w


def _global_grads(g0, g1, s5_grads, d_norm_final):
    out = {k: jnp.concatenate([g0[k], g1[k]], axis=0) for k in ("norm_mix", "norm_ffn", "norm_pl")}
    for k in ("mlp_w1", "mlp_w2", "pl_proj", "pl_gate"):
        out[k] = jnp.stack([g0[k], g1[k]])
    out["e_in_proj"] = jnp.concatenate([g0["in_u"], g0["in_z"], g0["in_xbc"], g0["in_dt"][:, :SSD_H]], axis=1)[None]
    out["e_out_proj"] = jnp.concatenate([g0["out_a"], g0["out_b"]], axis=0)[None]
    d_lam_re, d_lam_im, d_lstep, d_bre, d_bim, d_cre, d_cim = s5_grads
    out["s5_lam_re"] = d_lam_re.reshape(1, S5_GROUPS, S5_P)
    out["s5_lam_im"] = d_lam_im.reshape(1, S5_GROUPS, S5_P)
    out["s5_log_step"] = d_lstep[0:1, :S5_GROUPS]
    unb = lambda b: jnp.transpose(b.reshape(S5_G, S5_GROUPS, S5_P), (1, 2, 0))[None]
    unc = lambda c: jnp.transpose(c.reshape(S5_GROUPS, S5_P, S5_G), (0, 2, 1))[None]
    out.update(s5_b_re=unb(d_bre), s5_b_im=unb(d_bim), s5_c_re=unc(d_cre), s5_c_im=unc(d_cim),
               s5_d=g0["s5_d"], s5_glu_w=g0["glu_w"][None], s5_glu_b=g0["glu_b"],
               ssd_conv_w=g0["ssd_conv_w"][None, :4], ssd_conv_b=g0["ssd_conv_b"], ssd_dt_bias=g0["dt_bias"][:, :SSD_H],
               ssd_a_log=g0["a_log"][:, :SSD_H], ssd_d=g0["ssd_d"][:, :SSD_H], ssd_norm=g0["ssd_norm"])
    out["o_in_proj"] = jnp.concatenate([_rw_unpad(g1["in_rw"]), g1["in_xl"], g1["in_gl"]], axis=1)[None]
    out["o_out_proj"] = jnp.concatenate([g1["out_a"], g1["out_b"]], axis=0)[None]
    out.update(rwkv_mu=_rw_unpad(g1["mix_w"][2:3] - g1["mix_w"][3:4]), rwkv_w0=g1["w0"], rwkv_w_up=g1["w_up"][None, :RW_LORA],
               rwkv_a0=g1["a0"], rwkv_a_up=g1["a_up"][None, :RW_LORA], rwkv_g_up=g1["g_up"][None], rwkv_k_k=g1["k_k"],
               rwkv_k_a=g1["k_a"], rwkv_r_k=g1["r_k"].reshape(1, RW_H, RW_HD), rwkv_ln_g=g1["ln_g"], rwkv_ln_b=g1["ln_b"],
               lru_conv_w=g1["lru_conv_w"][None, :4], lru_conv_b=g1["lru_conv_b"],
               lru_w_a=_diag_blocks(g1["lru_wa"])[None], lru_b_a=g1["lru_b_a"].reshape(1, LRU_B, 64),
               lru_w_x=_diag_blocks(g1["lru_wx"])[None], lru_b_x=g1["lru_b_x"].reshape(1, LRU_B, 64),
               lru_lam=g1["lru_lam"].reshape(1, LRU_B, 64), norm_final=d_norm_final.reshape(D))
    return out


def _local_step(x, p, target, fw):
    prep_in = _s5_prep_inputs(fw)
    prep = _single(f_s5_prep, prep_in, name="s5_prep")
    w0, w1 = _even_weights(fw, prep), _odd_weights(fw)
    h1, r0 = _layer_fwd(x, p[0], w0, False, "l0")
    h2, r1 = _layer_fwd(h1, p[1], w1, True, "l1")
    gf = fw["norm_final"].reshape(1, D)
    (loss8,) = _stage(f_loss, [h2, target], [gf], tb=TB, name="loss", out_dtypes=[], n_acc=1)
    one = jnp.zeros((8, 128), F32).at[0, 0].set(1.0)
    dh2, d_gf = _stage_vjp(f_loss, [h2, target], [gf], [], tb=TB, name="loss_b", drow=[0], dconst=[0], acc_cots=[one])
    dh1, g1 = _layer_bwd(dh2, p[1], w1, r1, True, "l1")
    dx, g0 = _layer_bwd(dh1, p[0], w0, r0, False, "l0")
    cots = [g0[k] for k in ("abar_re", "abar_im", "wb_re", "wb_im", "wc_re", "wc_im")]
    s5_grads = _single_vjp(f_s5_prep, prep_in, cots, name="s5_prep_b")
    return loss8[0, 0], dx, _global_grads(g0, g1, s5_grads, d_gf)


def _xyc():
    return lax.axis_index("x"), lax.axis_index("y"), lax.axis_index("c")


def _flip(v, bit):
    return 1 - v if bit else v


def _remote(src, dst, send_sems, recv_sems, k, dev):
    return pltpu.make_async_remote_copy(src_ref=src, dst_ref=dst, send_sem=send_sems.at[k], recv_sem=recv_sems.at[k],
                                        device_id=dev, device_id_type=MESH)


def _dma_scratch(n_remote, n_local):
    return [pltpu.SemaphoreType.DMA((n_remote,)), pltpu.SemaphoreType.DMA((n_remote,)), pltpu.SemaphoreType.DMA((n_local,))]


CHIP_FLIPS = ((1, 0), (0, 1), (1, 1))


def _gather_chips(arrs, *, name):
    n = len(arrs)

    def body(*refs):
        ins, outs = refs[:n], refs[n:2 * n]
        send_sems, recv_sems, loc_sems = refs[2 * n:]
        x, y, c = _xyc()
        sends, locs = [], []
        for a in range(n):
            mine = outs[a].at[2 * x + y]
            lc = pltpu.make_async_copy(ins[a], mine, loc_sems.at[a])
            lc.start()
            locs.append(lc)
            for j, (fx, fy) in enumerate(CHIP_FLIPS):
                cp = _remote(ins[a], mine, send_sems, recv_sems, 3 * a + j, (_flip(x, fx), _flip(y, fy), c))
                cp.start()
                sends.append(cp)
        for a in range(n):
            for j, (fx, fy) in enumerate(CHIP_FLIPS):
                px, py = _flip(x, fx), _flip(y, fy)
                _remote(ins[a], outs[a].at[2 * px + py], send_sems, recv_sems, 3 * a + j, (px, py, c)).wait_recv()
        for cp in sends:
            cp.wait_send()
        for lc in locs:
            lc.wait()

    return pl.pallas_call(
        body, out_shape=[SDS((4,) + a.shape, a.dtype) for a in arrs], in_specs=[ANY] * n, out_specs=[ANY] * n,
        scratch_shapes=_dma_scratch(3 * n, n), name=name,
    )(*arrs)


def _gather_all(arr, *, name):
    def body(in_ref, out_ref, send_sems, recv_sems, loc_sems):
        x, y, c = _xyc()
        mine = out_ref.at[4 * x + 2 * y + c]
        lc = pltpu.make_async_copy(in_ref, mine, loc_sems.at[0])
        lc.start()
        sends = []
        for k in range(1, 8):
            dev = (_flip(x, k >> 2 & 1), _flip(y, k >> 1 & 1), _flip(c, k & 1))
            cp = _remote(in_ref, mine, send_sems, recv_sems, k - 1, dev)
            cp.start()
            sends.append(cp)
        for k in range(1, 8):
            px, py, pc = _flip(x, k >> 2 & 1), _flip(y, k >> 1 & 1), _flip(c, k & 1)
            _remote(in_ref, out_ref.at[4 * px + 2 * py + pc], send_sems, recv_sems, k - 1, (px, py, pc)).wait_recv()
        for cp in sends:
            cp.wait_send()
        lc.wait()

    return pl.pallas_call(
        body, out_shape=SDS((8,) + arr.shape, arr.dtype), in_specs=[ANY], out_specs=ANY,
        scratch_shapes=_dma_scratch(7, 1), name=name,
    )(arr)


def _dma_sems(n):
    return [pltpu.SemaphoreType.DMA((n,)), pltpu.SemaphoreType.DMA((n,))]


def _send_halves(arrs, *, name):
    n = len(arrs)

    def body(*refs):
        ins, outs = refs[:n], refs[n:2 * n]
        send_sems, recv_sems = refs[2 * n:]
        x, y, c = _xyc()
        copies = [_remote(ins[a].at[k, 1 - c], outs[a].at[k], send_sems, recv_sems, 4 * a + k, (x, y, 1 - c))
                  for a in range(n) for k in range(4)]
        for cp in copies:
            cp.start()
        for cp in copies:
            cp.wait_recv()
        for cp in copies:
            cp.wait_send()

    return pl.pallas_call(
        body, out_shape=[SDS((4,) + a.shape[2:], a.dtype) for a in arrs], in_specs=[ANY] * n, out_specs=[ANY] * n,
        scratch_shapes=_dma_sems(4 * n), name=name,
    )(*arrs)


def _add_half(g, recv, c_vec, *, tb, name):
    _, _, rh, cols = g.shape
    tb = min(tb, rh)

    def body(c_ref, g_ref, r_ref, o_ref):
        o_ref[...] = g_ref[...] + r_ref[...]

    return pl.pallas_call(
        body,
        grid_spec=pltpu.PrefetchScalarGridSpec(
            num_scalar_prefetch=1, grid=(4, rh // tb),
            in_specs=[pl.BlockSpec((None, None, tb, cols), lambda k, i, c_ref: (k, c_ref[0], i, 0)),
                      pl.BlockSpec((None, tb, cols), lambda k, i, c_ref: (k, i, 0))],
            out_specs=pl.BlockSpec((None, tb, cols), lambda k, i, c_ref: (k, i, 0))),
        out_shape=SDS((4, rh, cols), g.dtype), compiler_params=_cparams(("arbitrary", "arbitrary")), name=name,
    )(c_vec, g, recv)


def _scatter_chips(arrs, *, name):
    n = len(arrs)

    def body(*refs):
        ins, outs = refs[:n], refs[n:2 * n]
        send_sems, recv_sems = refs[2 * n:]
        x, y, c = _xyc()
        copies = []
        for a in range(n):
            for j, (fx, fy) in enumerate(CHIP_FLIPS):
                px, py = _flip(x, fx), _flip(y, fy)
                copies.append(_remote(ins[a].at[2 * px + py], outs[a].at[j], send_sems, recv_sems, 3 * a + j, (px, py, c)))
        for cp in copies:
            cp.start()
        for cp in copies:
            cp.wait_recv()
        for cp in copies:
            cp.wait_send()

    return pl.pallas_call(
        body, out_shape=[SDS((3,) + a.shape[1:], a.dtype) for a in arrs], in_specs=[ANY] * n, out_specs=[ANY] * n,
        scratch_shapes=_dma_sems(3 * n), name=name,
    )(*arrs)


def _sum_chips(p, landed, chip_vec, *, tb, name):
    _, rh, cols = p.shape
    tb = min(tb, rh)

    def body(chip_ref, p_ref, l_ref, o_ref):
        o_ref[...] = ((p_ref[...] + l_ref[0]) + l_ref[1]) + l_ref[2]

    return pl.pallas_call(
        body,
        grid_spec=pltpu.PrefetchScalarGridSpec(
            num_scalar_prefetch=1, grid=(rh // tb,),
            in_specs=[pl.BlockSpec((None, tb, cols), lambda i, chip_ref: (chip_ref[0], i, 0)),
                      pl.BlockSpec((3, tb, cols), lambda i, chip_ref: (0, i, 0))],
            out_specs=pl.BlockSpec((tb, cols), lambda i, chip_ref: (i, 0))),
        out_shape=SDS((rh, cols), p.dtype), compiler_params=_cparams(), name=name,
    )(chip_vec, p, landed)


def _swap_halves(arrs, *, name):
    n = len(arrs)

    def body(*refs):
        ins, outs = refs[:n], refs[n:2 * n]
        send_sems, recv_sems = refs[2 * n:]
        x, y, c = _xyc()
        copies = [_remote(ins[a], outs[a], send_sems, recv_sems, a, (x, y, 1 - c)) for a in range(n)]
        for cp in copies:
            cp.start()
        for cp in copies:
            cp.wait_recv()
        for cp in copies:
            cp.wait_send()

    return pl.pallas_call(
        body, out_shape=[SDS(a.shape, a.dtype) for a in arrs], in_specs=[ANY] * n, out_specs=[ANY] * n,
        scratch_shapes=_dma_sems(n), name=name,
    )(*arrs)


def _join_halves(mine, theirs, c_vec, *, tb, name):
    rh, cols = mine.shape
    tb = min(tb, rh)

    def body(c_ref, m_ref, t_ref, o_ref):
        o_ref[...] = jnp.where(pl.program_id(0) == c_ref[0], m_ref[...], t_ref[...])

    blk = pl.BlockSpec((tb, cols), lambda h, i, c_ref: (i, 0))
    return pl.pallas_call(
        body,
        grid_spec=pltpu.PrefetchScalarGridSpec(
            num_scalar_prefetch=1, grid=(2, rh // tb), in_specs=[blk, blk],
            out_specs=pl.BlockSpec((None, tb, cols), lambda h, i, c_ref: (h, i, 0))),
        out_shape=SDS((2, rh, cols), mine.dtype), compiler_params=_cparams(("arbitrary", "arbitrary")), name=name,
    )(c_vec, mine, theirs)


def _sum_lead(x, *, tb, name):
    k, r, c = x.shape
    tb = min(tb, r)
    assert r % tb == 0

    def body(x_ref, o_ref):
        acc = x_ref[0]
        for q in range(1, k):
            acc = acc + x_ref[q]
        o_ref[...] = acc

    return pl.pallas_call(
        body, grid=(r // tb,), in_specs=[pl.BlockSpec((k, tb, c), lambda i: (0, i, 0))],
        out_specs=pl.BlockSpec((tb, c), lambda i: (i, 0)), out_shape=SDS((r, c), x.dtype),
        compiler_params=_cparams(), name=name,
    )(x)


def f_adamw(w, g, m, v):
    m = ADAM_B1 * m + (1.0 - ADAM_B1) * g
    v = ADAM_B2 * v + (1.0 - ADAM_B2) * (g * g)
    m_hat = m / (1.0 - ADAM_B1 ** ADAM_STEP)
    v_hat = v / (1.0 - ADAM_B2 ** ADAM_STEP)
    return -ADAM_LR * (m_hat / (jnp.sqrt(v_hat) + ADAM_EPS) + ADAM_WD * w), m, v


def _adamw(w, g, m, v, *, name):
    shape = w.shape
    two = lambda a: a.reshape(-1, shape[-1])
    rows = two(w).shape[0]
    tb = 256 if rows % 256 == 0 else rows
    outs = _stage(f_adamw, [two(w), two(g), two(m), two(v)], [], tb=tb, name=name, out_dtypes=[F32] * 3)
    return [o.reshape(shape) for o in outs]


def _pack(arrs):
    flat = jnp.concatenate([a.astype(F32).reshape(-1) for a in arrs])
    size = -(-flat.shape[0] // 1024) * 1024
    return _pad_to(flat, size, 0).reshape(-1, 128)


def _unpack(buf, shapes):
    flat = buf.reshape(-1)
    out, off = [], 0
    for s in shapes:
        n = math.prod(s)
        out.append(flat[off:off + n].reshape(s))
        off += n
    return out


WEIGHTS = ("norm_mix", "norm_ffn", "norm_pl", "mlp_w1", "mlp_w2", "pl_proj", "pl_gate", "e_in_proj", "e_out_proj",
           "s5_lam_re", "s5_lam_im", "s5_log_step", "s5_b_re", "s5_b_im", "s5_c_re", "s5_c_im", "s5_d", "s5_glu_w",
           "s5_glu_b", "ssd_conv_w", "ssd_conv_b", "ssd_dt_bias", "ssd_a_log", "ssd_d", "ssd_norm", "o_in_proj",
           "o_out_proj", "rwkv_mu", "rwkv_w0", "rwkv_w_up", "rwkv_a0", "rwkv_a_up", "rwkv_g_up", "rwkv_k_k", "rwkv_k_a",
           "rwkv_r_k", "rwkv_ln_g", "rwkv_ln_b", "lru_conv_w", "lru_conv_b", "lru_w_a", "lru_b_a", "lru_w_x", "lru_b_x",
           "lru_lam", "norm_final")
BIG = ("mlp_w1", "mlp_w2", "pl_proj", "pl_gate", "e_in_proj", "e_out_proj", "o_in_proj", "o_out_proj")
SHARD_AXIS = {"mlp_w1": 2, "mlp_w2": 1, "pl_proj": 2, "pl_gate": 1, "e_in_proj": 2, "e_out_proj": 1, "s5_glu_w": 1,
              "ssd_conv_w": 2, "o_in_proj": 2, "o_out_proj": 1, "rwkv_mu": 1, "rwkv_w0": 1, "rwkv_w_up": 2, "rwkv_a0": 1,
              "rwkv_a_up": 2, "rwkv_g_up": 2, "rwkv_k_k": 1, "rwkv_k_a": 1, "rwkv_ln_g": 1, "rwkv_ln_b": 1,
              "lru_conv_w": 2, "lru_conv_b": 1}
SMALL = tuple(n for n in WEIGHTS if n not in BIG)
SMALL_SHARDED = tuple(n for n in SMALL if n in SHARD_AXIS)


def _gather_weights(w):
    shapes = [w[n].shape for n in SMALL_SHARDED]
    got = _gather_chips([w[n].astype(BF16) for n in BIG] + [_pack([w[n] for n in SMALL_SHARDED])], name="gather_weights")
    fw = {n: w[n] for n in SMALL if n not in SHARD_AXIS}
    for n, g in zip(BIG, got[:-1]):
        fw[n] = jnp.concatenate([g[k] for k in range(4)], axis=SHARD_AXIS[n])
    parts = [_unpack(got[-1][k], shapes) for k in range(4)]
    for i, n in enumerate(SMALL_SHARDED):
        fw[n] = jnp.concatenate([parts[k][i] for k in range(4)], axis=SHARD_AXIS[n])
    return fw


def _reduce_big(grads, w):
    stacks = []
    for n in BIG:
        cols = w[n].shape[-1]
        stacks.append(jnp.stack(jnp.split(grads[n], 4, axis=SHARD_AXIS[n])).reshape(4, 2, -1, cols))
    c_vec = lax.axis_index("c").astype(jnp.int32).reshape(1)
    chip_vec = (2 * lax.axis_index("x") + lax.axis_index("y")).astype(jnp.int32).reshape(1)
    got = _send_halves(stacks, name="reduce_pair")
    sums = [_add_half(s, r, c_vec, tb=512, name=f"reduce_pair_sum_{n}") for n, s, r in zip(BIG, stacks, got)]
    landed = _scatter_chips(sums, name="reduce_chips")
    halves = [_sum_chips(p, l, chip_vec, tb=256, name=f"reduce_chips_sum_{n}") for n, p, l in zip(BIG, sums, landed)]
    theirs = _swap_halves(halves, name="reduce_swap")
    return {n: _join_halves(h, t, c_vec, tb=512, name=f"reduce_join_{n}").reshape(w[n].shape)
            for n, h, t in zip(BIG, halves, theirs)}


def _reduce_small(grads, w, chip):
    shapes = [grads[n].shape for n in SMALL]
    packed = _pack([grads[n] for n in SMALL])
    total = _sum_lead(_gather_all(packed, name="reduce_small"), tb=packed.shape[0], name="reduce_small_sum")
    out = {}
    for n, g in zip(SMALL, _unpack(total, shapes)):
        if n in SHARD_AXIS:
            ax = SHARD_AXIS[n]
            size = w[n].shape[ax]
            g = lax.dynamic_slice_in_dim(g, chip * size, size, axis=ax)
        out[n] = g
    return out


def kernel(x, p, norm_mix, norm_ffn, norm_pl, mlp_w1, mlp_w2, pl_proj, pl_gate, e_in_proj, e_out_proj, s5_lam_re, s5_lam_im, s5_log_step, s5_b_re, s5_b_im, s5_c_re, s5_c_im, s5_d, s5_glu_w, s5_glu_b, ssd_conv_w, ssd_conv_b, ssd_dt_bias, ssd_a_log, ssd_d, ssd_norm, o_in_proj, o_out_proj, rwkv_mu, rwkv_w0, rwkv_w_up, rwkv_a0, rwkv_a_up, rwkv_g_up, rwkv_k_k, rwkv_k_a, rwkv_r_k, rwkv_ln_g, rwkv_ln_b, lru_conv_w, lru_conv_b, lru_w_a, lru_b_a, lru_w_x, lru_b_x, lru_lam, norm_final, loss_target, m_norm_mix, m_norm_ffn, m_norm_pl, m_mlp_w1, m_mlp_w2, m_pl_proj, m_pl_gate, m_e_in_proj, m_e_out_proj, m_s5_lam_re, m_s5_lam_im, m_s5_log_step, m_s5_b_re, m_s5_b_im, m_s5_c_re, m_s5_c_im, m_s5_d, m_s5_glu_w, m_s5_glu_b, m_ssd_conv_w, m_ssd_conv_b, m_ssd_dt_bias, m_ssd_a_log, m_ssd_d, m_ssd_norm, m_o_in_proj, m_o_out_proj, m_rwkv_mu, m_rwkv_w0, m_rwkv_w_up, m_rwkv_a0, m_rwkv_a_up, m_rwkv_g_up, m_rwkv_k_k, m_rwkv_k_a, m_rwkv_r_k, m_rwkv_ln_g, m_rwkv_ln_b, m_lru_conv_w, m_lru_conv_b, m_lru_w_a, m_lru_b_a, m_lru_w_x, m_lru_b_x, m_lru_lam, m_norm_final, v_norm_mix, v_norm_ffn, v_norm_pl, v_mlp_w1, v_mlp_w2, v_pl_proj, v_pl_gate, v_e_in_proj, v_e_out_proj, v_s5_lam_re, v_s5_lam_im, v_s5_log_step, v_s5_b_re, v_s5_b_im, v_s5_c_re, v_s5_c_im, v_s5_d, v_s5_glu_w, v_s5_glu_b, v_ssd_conv_w, v_ssd_conv_b, v_ssd_dt_bias, v_ssd_a_log, v_ssd_d, v_ssd_norm, v_o_in_proj, v_o_out_proj, v_rwkv_mu, v_rwkv_w0, v_rwkv_w_up, v_rwkv_a0, v_rwkv_a_up, v_rwkv_g_up, v_rwkv_k_k, v_rwkv_k_a, v_rwkv_r_k, v_rwkv_ln_g, v_rwkv_ln_b, v_lru_conv_w, v_lru_conv_b, v_lru_w_a, v_lru_b_a, v_lru_w_x, v_lru_b_x, v_lru_lam, v_norm_final):
    given = dict(locals())
    w = {n: given[n] for n in WEIGHTS}
    m = {n: given["m_" + n] for n in WEIGHTS}
    v = {n: given["v_" + n] for n in WEIGHTS}
    chip = 2 * lax.axis_index("x") + lax.axis_index("y")

    fw = _gather_weights(w)
    loss, dx, grads = _local_step(x[0], p[:, 0], loss_target[0], fw)
    loss = lax.psum(loss, ("x", "y", "c"))

    g = _reduce_big(grads, w)
    g.update(_reduce_small(grads, w, chip))

    delta, new_m, new_v = {}, {}, {}
    for n in BIG:
        delta[n], new_m[n], new_v[n] = _adamw(w[n], g[n], m[n], v[n], name=f"adamw_{n}")
    shapes = [w[n].shape for n in SMALL]
    packed = [_pack([d[n] for n in SMALL]) for d in (w, g, m, v)]
    for d, buf in zip((delta, new_m, new_v), _adamw(*packed, name="adamw_small")):
        d.update(zip(SMALL, _unpack(buf, shapes)))
    return (loss, dx[None], *[g[n] for n in WEIGHTS], *[delta[n] for n in WEIGHTS],
            *[new_m[n] for n in WEIGHTS], *[new_v[n] for n in WEIGHTS])
```

```python
import functools
import math

import jax
import jax.numpy as jnp
from jax import lax
from jax.experimental import pallas as pl
from jax.experimental.pallas import tpu as pltpu

F32 = jnp.float32
BF16 = jnp.bfloat16
HI = lax.Precision.HIGHEST
MESH = pl.DeviceIdType.MESH
SDS = jax.ShapeDtypeStruct
VMEM_LIMIT = 56 * 1024 * 1024
ANY = pl.BlockSpec(memory_space=pl.ANY)

D = 2048
PL_DIM = 256
D_FF = 4 * D
EPS = 1e-6
S5_W, S5_G, S5_GROUPS, S5_P = 512, 16, 32, 64
S5_N = S5_GROUPS * S5_P
SSD_W, SSD_HD, SSD_H, SSD_NG, SSD_N, SSD_L = 1536, 64, 24, 4, 128, 128
SSD_CONV = SSD_W + 2 * SSD_NG * SSD_N
EVEN_IN = S5_W + SSD_W + SSD_CONV + SSD_H
EVEN_PAD = 5120
RW_W, RW_H, RW_HD = 1024, 16, 64
RW_LORA = 96
RW_GATE = 256
RW_IN = 3 * RW_W + 2 * RW_LORA + RW_GATE
RW_PAD = 3584
LRU_W, LRU_B = 1024, 16
ODD_IN = RW_IN + 2 * LRU_W
ODD_PAD = RW_PAD + 2 * LRU_W
GN_EPS = 64e-5
LRU_C = 8.0
ADAM_LR, ADAM_B1, ADAM_B2, ADAM_EPS, ADAM_WD, ADAM_STEP = 0.001, 0.9, 0.999, 1e-08, 0.01, 10


def _cparams(sem=("arbitrary",)):
    return pltpu.CompilerParams(dimension_semantics=sem, vmem_limit_bytes=VMEM_LIMIT)


def _dot16(a, b, dims=(((1,), (0,)), ((), ()))):
    return lax.dot_general(a.astype(BF16), b.astype(BF16), dims, preferred_element_type=F32)


def _dot32(a, b, dims=(((1,), (0,)), ((), ()))):
    return lax.dot_general(a.astype(F32), b.astype(F32), dims, precision=HI, preferred_element_type=F32)


NT = (((1,), (1,)), ((), ()))
TN = (((0,), (0,)), ((), ()))


def _tile(dim, target):
    if dim <= target:
        return dim
    t = target - target % 128
    while t > 128 and dim % t:
        t -= 128
    assert dim % t == 0, (dim, target)
    return t


def _mm(a, b, *, ta=False, tb=False, add=None, out_dtype=F32, tm=1024, tn=1024, tk=1024, name,
        epilogue=None, extra=(), out_dtypes=None):
    m, k = (a.shape[1], a.shape[0]) if ta else a.shape
    n = b.shape[0] if tb else b.shape[1]
    assert (b.shape[1] if tb else b.shape[0]) == k, (a.shape, b.shape, ta, tb)
    tm, tn, tk = _tile(m, tm), _tile(n, tn), _tile(k, tk)
    nk = k // tk
    dims = (((0 if ta else 1,), (1 if tb else 0,)), ((), ()))
    ins = [a, b] + ([add] if add is not None else []) + list(extra)
    out_dtypes = out_dtypes or [out_dtype]
    n_in, n_out = len(ins), len(out_dtypes)

    def body(*refs):
        a_ref, b_ref = refs[:2]
        out_refs, acc_ref = refs[n_in:n_in + n_out], refs[-1]
        kk = pl.program_id(2)

        @pl.when(kk == 0)
        def _():
            acc_ref[...] = refs[2][...].astype(F32) if add is not None else jnp.zeros_like(acc_ref)

        acc_ref[...] += _dot16(a_ref[...], b_ref[...], dims)

        @pl.when(kk == nk - 1)
        def _():
            acc = acc_ref[...]
            res = epilogue(acc, *[r[...] for r in refs[n_in - len(extra):n_in]]) if epilogue else (acc,)
            for o_ref, val in zip(out_refs, res):
                o_ref[...] = val.astype(o_ref.dtype)

    a_spec = pl.BlockSpec((tk, tm), lambda i, j, q: (q, i)) if ta else pl.BlockSpec((tm, tk), lambda i, j, q: (i, q))
    b_spec = pl.BlockSpec((tn, tk), lambda i, j, q: (j, q)) if tb else pl.BlockSpec((tk, tn), lambda i, j, q: (q, j))
    o_spec = pl.BlockSpec((tm, tn), lambda i, j, q: (i, j))
    outs = pl.pallas_call(
        body,
        grid=(m // tm, n // tn, nk),
        in_specs=[a_spec, b_spec] + [o_spec] * (n_in - 2),
        out_specs=[o_spec] * n_out,
        out_shape=[SDS((m, n), dt) for dt in out_dtypes],
        scratch_shapes=[pltpu.VMEM((tm, tn), F32)],
        compiler_params=_cparams(("parallel", "parallel", "arbitrary")),
        name=name,
    )(*ins)
    return outs if epilogue else outs[0]


def _single(fn, consts, *, name):
    outs = jax.eval_shape(fn, *[SDS(c.shape, F32) for c in consts])
    n_in = len(consts)

    def body(*refs):
        res = fn(*[r[...] for r in refs[:n_in]])
        for o_ref, v in zip(refs[n_in:], res):
            o_ref[...] = v

    return pl.pallas_call(body, out_shape=[SDS(o.shape, F32) for o in outs],
                          compiler_params=pltpu.CompilerParams(vmem_limit_bytes=VMEM_LIMIT), name=name)(*consts)


def _single_vjp(fn, consts, cots, *, name):
    n_in = len(consts)

    def body(*refs):
        _, pull = jax.vjp(fn, *[r[...] for r in refs[:n_in]])
        grads = pull(tuple(r[...] for r in refs[n_in:n_in + len(cots)]))
        for o_ref, v in zip(refs[n_in + len(cots):], grads):
            o_ref[...] = v

    return pl.pallas_call(body, out_shape=[SDS(c.shape, F32) for c in consts],
                          compiler_params=pltpu.CompilerParams(vmem_limit_bytes=VMEM_LIMIT), name=name)(*consts, *cots)


def _full_spec(shape):
    nd = len(shape)
    return pl.BlockSpec(shape, lambda i, _n=nd: (0,) * _n)


def _stage_shapes(fn, rows, consts, tb, pos):
    rs = [SDS((tb, r.shape[1]), F32) for r in rows]
    cs = [SDS(c.shape, F32) for c in consts]
    f = (lambda *a: fn(jnp.int32(0), *a)) if pos else fn
    return jax.eval_shape(f, *rs, *cs)


def _stage(fn, rows, consts, *, tb, name, out_dtypes, n_acc=0, pos=False):
    t = rows[0].shape[0]
    assert t % tb == 0
    outs = _stage_shapes(fn, rows, consts, tb, pos)
    n_out = len(outs)
    n_row = n_out - n_acc
    n_in = len(rows) + len(consts)

    def body(*refs):
        i = pl.program_id(0)
        vals = [r[...].astype(F32) for r in refs[:n_in]]
        res = fn(i * tb, *vals) if pos else fn(*vals)
        out_refs = refs[n_in:]
        for q in range(n_row):
            out_refs[q][...] = res[q].astype(out_refs[q].dtype)
        for q in range(n_row, n_out):
            @pl.when(i == 0)
            def _(q=q):
                out_refs[q][...] = jnp.zeros_like(out_refs[q])

            out_refs[q][...] += res[q]

    in_specs = [pl.BlockSpec((tb, r.shape[1]), lambda i: (i, 0)) for r in rows] + [_full_spec(c.shape) for c in consts]
    out_specs = [pl.BlockSpec((tb, o.shape[1]), lambda i: (i, 0)) for o in outs[:n_row]] + [_full_spec(o.shape) for o in outs[n_row:]]
    out_shape = [SDS((t, o.shape[1]), dt) for o, dt in zip(outs[:n_row], out_dtypes)] + [SDS(o.shape, F32) for o in outs[n_row:]]
    return pl.pallas_call(
        body, grid=(t // tb,), in_specs=in_specs, out_specs=out_specs, out_shape=out_shape,
        compiler_params=_cparams(), name=name,
    )(*rows, *consts)


def _stage_vjp(fn, rows, consts, cots, *, tb, name, drow, dconst, drow_dtypes=None, acc_cots=(), pos=False):
    t = rows[0].shape[0]
    assert t % tb == 0
    n_rows, n_consts, n_cots, n_acc = len(rows), len(consts), len(cots), len(acc_cots)
    n_in = n_rows + n_consts + n_cots + n_acc
    drow_dtypes = drow_dtypes or [F32] * len(drow)

    def body(*refs):
        i = pl.program_id(0)
        vals = [r[...].astype(F32) for r in refs[:n_in]]
        rv, cv = vals[:n_rows], vals[n_rows:n_rows + n_consts]
        ct = tuple(vals[n_rows + n_consts:])

        def f(*dargs):
            r2, c2 = list(rv), list(cv)
            for q, idx in enumerate(drow):
                r2[idx] = dargs[q]
            for q, idx in enumerate(dconst):
                c2[idx] = dargs[len(drow) + q]
            return fn(i * tb, *r2, *c2) if pos else fn(*r2, *c2)

        _, pull = jax.vjp(f, *[rv[q] for q in drow], *[cv[q] for q in dconst])
        grads = pull(ct)
        out_refs = refs[n_in:]
        for q in range(len(drow)):
            out_refs[q][...] = grads[q].astype(out_refs[q].dtype)
        for q in range(len(drow), len(drow) + len(dconst)):
            @pl.when(i == 0)
            def _(q=q):
                out_refs[q][...] = jnp.zeros_like(out_refs[q])

            out_refs[q][...] += grads[q]

    in_specs = ([pl.BlockSpec((tb, r.shape[1]), lambda i: (i, 0)) for r in rows] + [_full_spec(c.shape) for c in consts]
                + [pl.BlockSpec((tb, c.shape[1]), lambda i: (i, 0)) for c in cots] + [_full_spec(c.shape) for c in acc_cots])
    out_specs = ([pl.BlockSpec((tb, rows[q].shape[1]), lambda i: (i, 0)) for q in drow]
                 + [_full_spec(consts[q].shape) for q in dconst])
    out_shape = ([SDS(rows[q].shape, dt) for q, dt in zip(drow, drow_dtypes)]
                 + [SDS(consts[q].shape, F32) for q in dconst])
    return pl.pallas_call(
        body, grid=(t // tb,), in_specs=in_specs, out_specs=out_specs, out_shape=out_shape,
        compiler_params=_cparams(), name=name,
    )(*rows, *consts, *cots, *acc_cots)


def _conv_fwd(x, w, b, *, tb, name):
    t, c = x.shape
    r8 = tb // 8

    def body(x_ref, p_ref, w_ref, b_ref, o_ref):
        i = pl.program_id(0)
        x_ = x_ref[...]
        p_ = jnp.where(i > 0, p_ref[...], 0.0)
        w_ = w_ref[...]
        row = lax.broadcasted_iota(jnp.int32, x_.shape, 0)
        row8 = lax.broadcasted_iota(jnp.int32, p_.shape, 0)
        acc = x_ * w_[3:4, :] + b_ref[...]
        head = jnp.zeros_like(p_)
        for j in (1, 2, 3):
            wj = w_[3 - j:4 - j, :]
            acc += jnp.where(row >= j, pltpu.roll(x_, j, 0), 0.0) * wj
            head += jnp.where(row8 < j, pltpu.roll(p_, j, 0), 0.0) * wj
        o_ref[...] = acc
        o_ref[0:8, :] += head

    return pl.pallas_call(
        body, grid=(t // tb,),
        in_specs=[pl.BlockSpec((tb, c), lambda i: (i, 0)),
                  pl.BlockSpec((8, c), lambda i: (jnp.maximum(i * r8 - 1, 0), 0)),
                  _full_spec(w.shape), _full_spec(b.shape)],
        out_specs=pl.BlockSpec((tb, c), lambda i: (i, 0)),
        out_shape=SDS((t, c), F32), compiler_params=_cparams(), name=name,
    )(x, x, w, b)


def _conv_bwd(x, w, dy, *, tb, name):
    t, c = x.shape
    r8 = tb // 8
    nb = t // tb

    def body(x_ref, p_ref, w_ref, g_ref, n_ref, dx_ref, dw_ref, db_ref):
        i = pl.program_id(0)
        x_ = x_ref[...]
        p_ = jnp.where(i > 0, p_ref[...], 0.0)
        g_ = g_ref[...]
        n_ = jnp.where(i < nb - 1, n_ref[...], 0.0)
        w_ = w_ref[...]
        row = lax.broadcasted_iota(jnp.int32, x_.shape, 0)
        row8 = lax.broadcasted_iota(jnp.int32, p_.shape, 0)
        g8 = g_[0:8, :]
        dx = g_ * w_[3:4, :]
        tail = jnp.zeros_like(n_)
        dws = [jnp.sum(g_ * x_, axis=0, keepdims=True)]
        for j in (1, 2, 3):
            wj = w_[3 - j:4 - j, :]
            dx += jnp.where(row < tb - j, pltpu.roll(g_, tb - j, 0), 0.0) * wj
            tail += jnp.where(row8 >= 8 - j, pltpu.roll(n_, 8 - j, 0), 0.0) * wj
            xs = jnp.where(row >= j, pltpu.roll(x_, j, 0), 0.0)
            ps = jnp.where(row8 < j, pltpu.roll(p_, j, 0), 0.0)
            dws.append(jnp.sum(g_ * xs, axis=0, keepdims=True) + jnp.sum(g8 * ps, axis=0, keepdims=True))
        dx_ref[...] = dx
        dx_ref[tb - 8:tb, :] += tail

        @pl.when(i == 0)
        def _():
            dw_ref[...] = jnp.zeros_like(dw_ref)
            db_ref[...] = jnp.zeros_like(db_ref)

        for j in range(4):
            dw_ref[3 - j:4 - j, :] += dws[j]
        db_ref[...] += jnp.sum(g_, axis=0, keepdims=True)

    return pl.pallas_call(
        body, grid=(nb,),
        in_specs=[pl.BlockSpec((tb, c), lambda i: (i, 0)),
                  pl.BlockSpec((8, c), lambda i: (jnp.maximum(i * r8 - 1, 0), 0)),
                  _full_spec(w.shape),
                  pl.BlockSpec((tb, c), lambda i: (i, 0)),
                  pl.BlockSpec((8, c), lambda i: (jnp.minimum((i + 1) * r8, t // 8 - 1), 0))],
        out_specs=[pl.BlockSpec((tb, c), lambda i: (i, 0)), _full_spec((8, c)), _full_spec((1, c))],
        out_shape=[SDS((t, c), F32), SDS((8, c), F32), SDS((1, c), F32)],
        compiler_params=_cparams(), name=name,
    )(x, x, w, dy, dy)


def _lru_scan_fwd(a, b, *, tb, name):
    t, c = a.shape

    def body(a_ref, b_ref, h_ref, st_ref):
        @pl.when(pl.program_id(0) == 0)
        def _():
            st_ref[...] = jnp.zeros_like(st_ref)

        def step(s, h):
            h = a_ref[pl.ds(s, 1), :] * h + b_ref[pl.ds(s, 1), :]
            h_ref[pl.ds(s, 1), :] = h
            return h

        st_ref[...] = lax.fori_loop(0, tb, step, st_ref[...], unroll=8)

    blk = pl.BlockSpec((tb, c), lambda i: (i, 0))
    return pl.pallas_call(
        body, grid=(t // tb,), in_specs=[blk, blk], out_specs=blk, out_shape=SDS((t, c), F32),
        scratch_shapes=[pltpu.VMEM((1, c), F32)], compiler_params=_cparams(), name=name,
    )(a, b)


def _lru_scan_bwd(a, h, dh, *, tb, name):
    t, c = a.shape
    nb = t // tb
    r8 = tb // 8

    def body(a_ref, h_ref, p_ref, g_ref, da_ref, db_ref, st_ref):
        i = pl.program_id(0)

        @pl.when(i == 0)
        def _():
            st_ref[...] = jnp.zeros_like(st_ref)

        hprev0 = jnp.where(i < nb - 1, p_ref[7:8, :], 0.0)

        def step(q, carry):
            s = tb - 1 - q
            g = g_ref[pl.ds(s, 1), :] + carry
            hp = h_ref[pl.ds(jnp.maximum(s - 1, 0), 1), :]
            hp = jnp.where(s > 0, hp, hprev0)
            db_ref[pl.ds(s, 1), :] = g
            da_ref[pl.ds(s, 1), :] = g * hp
            return a_ref[pl.ds(s, 1), :] * g

        st_ref[...] = lax.fori_loop(0, tb, step, st_ref[...], unroll=8)

    rev = pl.BlockSpec((tb, c), lambda i: (nb - 1 - i, 0))
    prev = pl.BlockSpec((8, c), lambda i: (jnp.maximum((nb - 1 - i) * r8 - 1, 0), 0))
    return pl.pallas_call(
        body, grid=(nb,), in_specs=[rev, rev, prev, rev], out_specs=[rev, rev],
        out_shape=[SDS((t, c), F32), SDS((t, c), F32)],
        scratch_shapes=[pltpu.VMEM((1, c), F32)], compiler_params=_cparams(), name=name,
    )(a, h, h, dh)


def _s5_scan_fwd(ar, ai, br, bi, *, tb, name):
    t, c = br.shape

    def body(ar_ref, ai_ref, br_ref, bi_ref, xr_ref, xi_ref, sr_ref, si_ref):
        @pl.when(pl.program_id(0) == 0)
        def _():
            sr_ref[...] = jnp.zeros_like(sr_ref)
            si_ref[...] = jnp.zeros_like(si_ref)

        ar_, ai_ = ar_ref[...], ai_ref[...]

        def step(s, carry):
            xr, xi = carry
            nr = ar_ * xr - ai_ * xi + br_ref[pl.ds(s, 1), :]
            ni = ar_ * xi + ai_ * xr + bi_ref[pl.ds(s, 1), :]
            xr_ref[pl.ds(s, 1), :] = nr
            xi_ref[pl.ds(s, 1), :] = ni
            return nr, ni

        xr, xi = lax.fori_loop(0, tb, step, (sr_ref[...], si_ref[...]), unroll=8)
        sr_ref[...] = xr
        si_ref[...] = xi

    blk = pl.BlockSpec((tb, c), lambda i: (i, 0))
    one = _full_spec((1, c))
    return pl.pallas_call(
        body, grid=(t // tb,), in_specs=[one, one, blk, blk], out_specs=[blk, blk],
        out_shape=[SDS((t, c), F32), SDS((t, c), F32)],
        scratch_shapes=[pltpu.VMEM((1, c), F32), pltpu.VMEM((1, c), F32)], compiler_params=_cparams(), name=name,
    )(ar, ai, br, bi)


def _s5_scan_bwd(ar, ai, xr, xi, dxr, dxi, *, tb, name):
    t, c = xr.shape
    nb = t // tb
    r8 = tb // 8

    def body(ar_ref, ai_ref, xr_ref, xi_ref, pr_ref, pi_ref, gr_ref, gi_ref,
             dbr_ref, dbi_ref, dar_ref, dai_ref, cr_ref, ci_ref):
        i = pl.program_id(0)

        @pl.when(i == 0)
        def _():
            cr_ref[...] = jnp.zeros_like(cr_ref)
            ci_ref[...] = jnp.zeros_like(ci_ref)
            dar_ref[...] = jnp.zeros_like(dar_ref)
            dai_ref[...] = jnp.zeros_like(dai_ref)

        ar_, ai_ = ar_ref[...], ai_ref[...]
        first = i == nb - 1
        pr0 = jnp.where(first, 0.0, pr_ref[7:8, :])
        pi0 = jnp.where(first, 0.0, pi_ref[7:8, :])

        def step(q, carry):
            cr, ci, dar, dai = carry
            s = tb - 1 - q
            gr = gr_ref[pl.ds(s, 1), :] + cr
            gi = gi_ref[pl.ds(s, 1), :] + ci
            sp = jnp.maximum(s - 1, 0)
            xpr = jnp.where(s > 0, xr_ref[pl.ds(sp, 1), :], pr0)
            xpi = jnp.where(s > 0, xi_ref[pl.ds(sp, 1), :], pi0)
            dbr_ref[pl.ds(s, 1), :] = gr
            dbi_ref[pl.ds(s, 1), :] = gi
            dar = dar + gr * xpr + gi * xpi
            dai = dai - gr * xpi + gi * xpr
            return ar_ * gr + ai_ * gi, ar_ * gi - ai_ * gr, dar, dai

        cr, ci, dar, dai = lax.fori_loop(0, tb, step, (cr_ref[...], ci_ref[...], dar_ref[...], dai_ref[...]), unroll=8)
        cr_ref[...] = cr
        ci_ref[...] = ci
        dar_ref[...] = dar
        dai_ref[...] = dai

    rev = pl.BlockSpec((tb, c), lambda i: (nb - 1 - i, 0))
    prev = pl.BlockSpec((8, c), lambda i: (jnp.maximum((nb - 1 - i) * r8 - 1, 0), 0))
    one = _full_spec((1, c))
    return pl.pallas_call(
        body, grid=(nb,), in_specs=[one, one, rev, rev, prev, prev, rev, rev], out_specs=[rev, rev, one, one],
        out_shape=[SDS((t, c), F32), SDS((t, c), F32), SDS((1, c), F32), SDS((1, c), F32)],
        scratch_shapes=[pltpu.VMEM((1, c), F32), pltpu.VMEM((1, c), F32)], compiler_params=_cparams(), name=name,
    )(ar, ai, xr, xi, xr, xi, dxr, dxi)


RW_PAIRS = RW_H // 2


def _pair_consts():
    sub = lax.broadcasted_iota(jnp.int32, (64, 128), 0)
    lane = lax.broadcasted_iota(jnp.int32, (64, 128), 1)
    eye2 = ((lane & 63) == sub).astype(F32)
    r2 = lax.broadcasted_iota(jnp.int32, (128, 128), 0)
    c2 = lax.broadcasted_iota(jnp.int32, (128, 128), 1)
    bsel = ((r2 >> 6) == (c2 >> 6)).astype(BF16)
    return eye2, bsel


def _segsum(x, bsel):
    bits = lax.bitcast_convert_type(x, jnp.int32)
    hi = lax.bitcast_convert_type(bits & jnp.int32(-65536), F32)
    lo = (x - hi).astype(BF16)
    return (jnp.dot(hi.astype(BF16), bsel, preferred_element_type=F32) + jnp.dot(lo, bsel, preferred_element_type=F32))


def _bc(x8):
    return jnp.stack([jnp.broadcast_to(x8[q:q + 1, :], (64, 128)) for q in range(RW_PAIRS)])


def _seg3(x3, bsel):
    return _segsum(x3.reshape(RW_PAIRS * 64, 128), bsel).reshape(RW_PAIRS, 64, 128)


def _rwkv_scan_fwd(r, w, k, v, kk, a, *, lc, name):
    t = r.shape[0]
    nc = t // lc

    def body(r_ref, w_ref, k_ref, v_ref, kk_ref, a_ref, y_ref, ck_ref, st_ref, vb_ref, hist_ref):
        @pl.when(pl.program_id(0) == 0)
        def _():
            st_ref[...] = jnp.zeros_like(st_ref)

        ck_ref[0] = st_ref[...]
        eye2, bsel = _pair_consts()

        def values(s, c):
            vb_ref[s] = _seg3(eye2[None] * _bc(v_ref[s]), bsel)
            return c

        lax.fori_loop(0, lc, values, 0, unroll=2)

        def step(s, c):
            st = st_ref[...]
            kk8 = kk_ref[s]
            sa = -_seg3(st * _bc(kk8), bsel)
            st = st * _bc(w_ref[s]) + sa * _bc(kk8 * a_ref[s]) + vb_ref[s] * _bc(k_ref[s])
            st_ref[...] = st
            hist_ref[s] = st
            return c

        lax.fori_loop(0, lc, step, 0)

        def read(s, c):
            yb = _seg3(hist_ref[s] * _bc(r_ref[s]), bsel)
            y_ref[s] = jnp.sum(eye2[None] * yb, axis=1)
            return c

        lax.fori_loop(0, lc, read, 0, unroll=2)

    blk = pl.BlockSpec((lc, RW_PAIRS, 128), lambda i: (i, 0, 0))
    return pl.pallas_call(
        body, grid=(nc,), in_specs=[blk] * 6,
        out_specs=[blk, pl.BlockSpec((1, RW_PAIRS, 64, 128), lambda i: (i, 0, 0, 0))],
        out_shape=[SDS((t, RW_PAIRS, 128), F32), SDS((nc, RW_PAIRS, 64, 128), F32)],
        scratch_shapes=[pltpu.VMEM((RW_PAIRS, 64, 128), F32), pltpu.VMEM((lc, RW_PAIRS, 64, 128), F32),
                        pltpu.VMEM((lc, RW_PAIRS, 64, 128), F32)],
        compiler_params=_cparams(), name=name,
    )(r, w, k, v, kk, a)


def _rwkv_scan_bwd(r, w, k, v, kk, a, ck, dy, *, lc, name):
    t = r.shape[0]
    nc = t // lc

    def body(r_ref, w_ref, k_ref, v_ref, kk_ref, a_ref, ck_ref, dy_ref,
             dr_ref, dw_ref, dk_ref, dv_ref, dkk_ref, da_ref,
             ds_ref, vb_ref, dyb_ref, hist_ref, sa_ref, dsh_ref, dsa_ref):
        @pl.when(pl.program_id(0) == 0)
        def _():
            ds_ref[...] = jnp.zeros_like(ds_ref)

        eye2, bsel = _pair_consts()

        def columns(s, c):
            vb_ref[s] = _seg3(eye2[None] * _bc(v_ref[s]), bsel)
            dyb_ref[s] = _seg3(eye2[None] * _bc(dy_ref[s]), bsel)
            return c

        lax.fori_loop(0, lc, columns, 0, unroll=2)
        hist_ref[0] = ck_ref[0]

        def fwd(s, c):
            st = hist_ref[s]
            kk8 = kk_ref[s]
            sa = -_seg3(st * _bc(kk8), bsel)
            sa_ref[s] = sa
            hist_ref[s + 1] = st * _bc(w_ref[s]) + sa * _bc(kk8 * a_ref[s]) + vb_ref[s] * _bc(k_ref[s])
            return c

        lax.fori_loop(0, lc, fwd, 0)

        def back(j, c):
            s = lc - 1 - j
            kk8 = kk_ref[s]
            d_s = ds_ref[...] + dyb_ref[s] * _bc(r_ref[s])
            dsa = _seg3(d_s * _bc(kk8 * a_ref[s]), bsel)
            dsh_ref[s] = d_s
            dsa_ref[s] = dsa
            ds_ref[...] = d_s * _bc(w_ref[s]) - dsa * _bc(kk8)
            return c

        lax.fori_loop(0, lc, back, 0)

        def grads(s, c):
            s_prev, s_cur, d_s, dsa = hist_ref[s], hist_ref[s + 1], dsh_ref[s], dsa_ref[s]
            col = lambda z: jnp.sum(z, axis=1)
            db = col(d_s * sa_ref[s])
            dr_ref[s] = col(s_cur * dyb_ref[s])
            dw_ref[s] = col(d_s * s_prev)
            dv_ref[s] = col(eye2[None] * _seg3(d_s * _bc(k_ref[s]), bsel))
            dk_ref[s] = col(d_s * vb_ref[s])
            dkk_ref[s] = db * a_ref[s] - col(s_prev * dsa)
            da_ref[s] = db * kk_ref[s]
            return c

        lax.fori_loop(0, lc, grads, 0)

    rev = pl.BlockSpec((lc, RW_PAIRS, 128), lambda i: (nc - 1 - i, 0, 0))
    big = lambda n: pltpu.VMEM((n, RW_PAIRS, 64, 128), F32)
    return pl.pallas_call(
        body, grid=(nc,),
        in_specs=[rev] * 6 + [pl.BlockSpec((1, RW_PAIRS, 64, 128), lambda i: (nc - 1 - i, 0, 0, 0)), rev],
        out_specs=[rev] * 6, out_shape=[SDS((t, RW_PAIRS, 128), F32)] * 6,
        scratch_shapes=[pltpu.VMEM((RW_PAIRS, 64, 128), F32), big(lc), big(lc), big(lc + 1), big(lc), big(lc), big(lc)],
        compiler_params=_cparams(), name=name,
    )(r, w, k, v, kk, a, ck, dy)


SSD_PAIRS = SSD_H // 2


def _ssd_chunk(states, xdt, da, bm, cm):
    ln = SSD_L
    row = lax.broadcasted_iota(jnp.int32, (ln, ln), 0)
    col = lax.broadcasted_iota(jnp.int32, (ln, ln), 1)
    causal = row >= col
    acum = _dot32(causal.astype(F32), da)
    acum_t = _dot32(da, (row <= col).astype(F32), TN)
    sub = lax.broadcasted_iota(jnp.int32, (128, 128), 0)
    lane = lax.broadcasted_iota(jnp.int32, (128, 128), 1)
    ys, new_states = [], []
    for q in range(SSD_PAIRS):
        g = q // (SSD_PAIRS // SSD_NG)
        bg = bm[:, g * SSD_N:(g + 1) * SSD_N]
        cg = cm[:, g * SSD_N:(g + 1) * SSD_N]
        xq = xdt[:, q * 128:(q + 1) * 128]
        scores = _dot16(cg, bg, NT)
        aexp = _dot32(acum, (sub == 2 * q + (lane >> 6)).astype(F32))
        tot = aexp[ln - 1:ln, :]
        yh = []
        for h in (2 * q, 2 * q + 1):
            seg = _dot32(acum, (sub == h).astype(F32)) - acum_t[h:h + 1, :]
            yh.append(_dot16(scores * jnp.exp(jnp.where(causal, seg, -1e30)), xq))
        y = jnp.where(lane < 64, yh[0], yh[1]) + _dot16(cg, states[q]) * jnp.exp(aexp)
        new = _dot16(bg, xq * jnp.exp(tot - aexp), TN)
        ys.append(y)
        new_states.append(states[q] * jnp.exp(tot) + new)
    return jnp.concatenate(ys, axis=1), new_states


def _ssd_fwd(xdt, da, bm, cm, *, name):
    t = xdt.shape[0]
    nc = t // SSD_L

    def body(x_ref, a_ref, b_ref, c_ref, y_ref, ck_ref, st_ref):
        @pl.when(pl.program_id(0) == 0)
        def _():
            st_ref[...] = jnp.zeros_like(st_ref)

        ck_ref[0] = st_ref[...]
        y, new = _ssd_chunk([st_ref[q] for q in range(SSD_PAIRS)], x_ref[...], a_ref[...], b_ref[...], c_ref[...])
        y_ref[...] = y
        for q in range(SSD_PAIRS):
            st_ref[q] = new[q]

    blk = lambda wd: pl.BlockSpec((SSD_L, wd), lambda i: (i, 0))
    return pl.pallas_call(
        body, grid=(nc,), in_specs=[blk(SSD_W), blk(128), blk(512), blk(512)],
        out_specs=[blk(SSD_W), pl.BlockSpec((1, SSD_PAIRS, 128, 128), lambda i: (i, 0, 0, 0))],
        out_shape=[SDS((t, SSD_W), F32), SDS((nc, SSD_PAIRS, 128, 128), F32)],
        scratch_shapes=[pltpu.VMEM((SSD_PAIRS, 128, 128), F32)], compiler_params=_cparams(), name=name,
    )(xdt, da, bm, cm)


def _ssd_bwd(xdt, da, bm, cm, ck, dy, *, name):
    t = xdt.shape[0]
    nc = t // SSD_L

    def body(x_ref, a_ref, b_ref, c_ref, ck_ref, dy_ref, dx_ref, dda_ref, db_ref, dc_ref, ds_ref):
        @pl.when(pl.program_id(0) == 0)
        def _():
            ds_ref[...] = jnp.zeros_like(ds_ref)

        _, pull = jax.vjp(_ssd_chunk, [ck_ref[0, q] for q in range(SSD_PAIRS)], x_ref[...], a_ref[...], b_ref[...], c_ref[...])
        dst, dx, dda, db, dc = pull((dy_ref[...], [ds_ref[q] for q in range(SSD_PAIRS)]))
        dx_ref[...] = dx
        dda_ref[...] = dda
        db_ref[...] = db
        dc_ref[...] = dc
        for q in range(SSD_PAIRS):
            ds_ref[q] = dst[q]

    rev = lambda wd: pl.BlockSpec((SSD_L, wd), lambda i: (nc - 1 - i, 0))
    return pl.pallas_call(
        body, grid=(nc,),
        in_specs=[rev(SSD_W), rev(128), rev(512), rev(512),
                  pl.BlockSpec((1, SSD_PAIRS, 128, 128), lambda i: (nc - 1 - i, 0, 0, 0)), rev(SSD_W)],
        out_specs=[rev(SSD_W), rev(128), rev(512), rev(512)],
        out_shape=[SDS((t, SSD_W), F32), SDS((t, 128), F32), SDS((t, 512), F32), SDS((t, 512), F32)],
        scratch_shapes=[pltpu.VMEM((SSD_PAIRS, 128, 128), F32)], compiler_params=_cparams(), name=name,
    )(xdt, da, bm, cm, ck, dy)


def _iota(shape, dim):
    return lax.broadcasted_iota(jnp.int32, shape, dim)


def _rms(x, g):
    return x * lax.rsqrt(jnp.mean(x * x, axis=-1, keepdims=True) + EPS) * g


def _head_sel(width, shift):
    return ((_iota((width, 128), 0) >> shift) == _iota((width, 128), 1)).astype(F32)


def _head_sum(x, shift=6):
    sel = _head_sel(x.shape[1], shift)
    return _dot32(_dot32(x, sel), sel, NT)


def _head_expand(x, width, shift=6):
    return _dot32(x, _head_sel(width, shift), NT)


def f_norm(h, g):
    return (_rms(h, g),)


def f_norm_pass(h, g):
    return _rms(h, g), h


def f_add_norm(h, m, g):
    h1 = h + m
    return h1, _rms(h1, g)


def f_relu2(u):
    r = jnp.maximum(u, 0.0)
    return (r * r,)


def f_plgate(h2, gl, pp):
    return (h2 + jax.nn.sigmoid(gl) * pp,)


def f_loss(h, tgt, g):
    err = _rms(h, g) - tgt
    part = 0.5 * jnp.sum(jnp.mean(err * err, axis=-1, keepdims=True), axis=0, keepdims=True)
    return (jnp.broadcast_to(part, (8, 128)),)


def f_s5_prep(lam_re, lam_im, lstep, bre_t, bim_t, cre_t, cim_t):
    step = jnp.exp(_dot32(lstep, _head_sel(S5_N, 6), NT)[0:1, :])
    mag = jnp.exp(lam_re * step)
    abar_re, abar_im = mag * jnp.cos(lam_im * step), mag * jnp.sin(lam_im * step)
    den = lam_re * lam_re + lam_im * lam_im
    nr = abar_re - 1.0
    coef_re = (nr * lam_re + abar_im * lam_im) / den
    coef_im = (abar_im * lam_re - nr * lam_im) / den
    bbar_re = coef_re * bre_t - coef_im * bim_t
    bbar_im = coef_re * bim_t + coef_im * bre_t
    rep = ((_iota((S5_W, S5_G), 0) & (S5_G - 1)) == _iota((S5_W, S5_G), 1)).astype(F32)
    blk = ((_iota((S5_W, S5_N), 0) >> 4) == (_iota((S5_W, S5_N), 1) >> 6)).astype(F32)
    blk_t = ((_iota((S5_N, S5_W), 0) >> 6) == (_iota((S5_N, S5_W), 1) >> 4)).astype(F32)
    wb_re, wb_im = _dot32(rep, bbar_re) * blk, _dot32(rep, bbar_im) * blk
    wc_re, wc_im = _dot32(cre_t, rep, NT) * blk_t, _dot32(cim_t, rep, NT) * blk_t
    return abar_re, abar_im, wb_re, wb_im, wc_re, wc_im


def f_s5_post(xr, xi, u, wc_re, wc_im, d_skip, glu_w, glu_b):
    y = _dot16(xr, wc_re) - _dot16(xi, wc_im) + d_skip * u
    act = jax.nn.gelu(y)
    return (act * jax.nn.sigmoid(_dot16(act, glu_w) + glu_b),)


def f_ssd_pre(xc, dtr, dt_bias, a_log):
    act = jax.nn.silu(xc)
    heads = _iota(dtr.shape, 1) < SSD_H
    dt = jnp.where(heads, jax.nn.softplus(dtr + dt_bias), 0.0)
    da = dt * (-jnp.exp(a_log))
    xdt = act[:, :SSD_W] * _head_expand(dt, SSD_W)
    return xdt, da, act[:, SSD_W:SSD_W + 512], act[:, SSD_W + 512:]


def f_ssd_pre_pass(xc, dtr, dt_bias, a_log):
    return f_ssd_pre(xc, dtr, dt_bias, a_log) + (xc,)


def f_ssd_post(y, xc, z, d_skip, norm_g):
    xs = jax.nn.silu(xc[:, :SSD_W])
    y = (y + xs * _head_expand(d_skip, SSD_W)) * jax.nn.silu(z)
    gw = SSD_W // SSD_NG
    parts = []
    for g in range(SSD_NG):
        seg = y[:, g * gw:(g + 1) * gw]
        parts.append(seg * lax.rsqrt(jnp.mean(seg * seg, axis=-1, keepdims=True) + EPS))
    return (jnp.concatenate(parts, axis=1) * norm_g,)


def f_rwkv_pre(f, w0, w_up, a0, a_up, g_up, k_k, k_a):
    r, k, v = f[:, 0:1024], f[:, 1024:2048], f[:, 2048:3072]
    wl, al, gl = f[:, 3072:3200], f[:, 3200:3328], f[:, 3328:3584]
    w = -jax.nn.softplus(-(w0 + _dot16(jnp.tanh(wl), w_up))) - 0.5
    decay = jnp.exp(-jnp.exp(w))
    a = jax.nn.sigmoid(a0 + _dot16(al, a_up))
    g = _dot16(jax.nn.sigmoid(gl), g_up)
    kk = k * k_k
    k2 = k * (1.0 + (a - 1.0) * k_a)
    kkn = kk * lax.rsqrt(jnp.maximum(_head_sum(kk * kk), 1e-24))
    return r, decay, k2, v, kkn, a, g


def f_rwkv_pre_pass(f, w0, w_up, a0, a_up, g_up, k_k, k_a):
    out = f_rwkv_pre(f, w0, w_up, a0, a_up, g_up, k_k, k_a)
    return out + (out[0], out[2], out[3])


def f_rwkv_post(y, r, k2, v, g, ln_g, ln_b, r_k):
    mean = _head_sum(y) * (1.0 / RW_HD)
    yc = y - mean
    var = _head_sum(yc * yc) * (1.0 / RW_HD)
    yn = yc * lax.rsqrt(var + GN_EPS) * ln_g + ln_b
    bonus = _head_sum(r * k2 * r_k) * v
    return ((yn + bonus) * g,)


def _neg_expm1(y):
    series = -y * (1.0 + y * (0.5 + y * (1.0 / 6.0 + y * (1.0 / 24.0 + y * (1.0 / 120.0)))))
    return jnp.where(y > -0.1, series, 1.0 - jnp.exp(y))


def f_lru_pre(t0, xc, w_a, b_a, w_x, b_x, lam):
    gate_r = jax.nn.sigmoid(_dot16(xc, w_a) + b_a)
    gate_i = jax.nn.sigmoid(_dot16(xc, w_x) + b_x)
    log_a = -LRU_C * gate_r * jax.nn.softplus(-lam)
    mult = jnp.sqrt(jnp.maximum(_neg_expm1(2.0 * log_a), 0.0))
    mult = jnp.where(_iota(xc.shape, 0) + t0 == 0, 1.0, mult)
    return jnp.exp(log_a), xc * gate_i * mult


def f_lru_post(h, gl):
    return (h * jax.nn.gelu(gl),)


TB = 256
TBH = 128
SCAN_TB = 256
RW_LC = 16


def _even_fwd(hn, w, tag):
    n = lambda s: f"{tag}_{s}"
    u = _mm(hn, w["in_u"], name=n("proj_u"))
    z = _mm(hn, w["in_z"], name=n("proj_z"))
    xbc = _mm(hn, w["in_xbc"], name=n("proj_xbc"))
    dtr = _mm(hn, w["in_dt"], name=n("proj_dt"))
    bu_re = _mm(u, w["wb_re"], name=n("s5_bu_re"))
    bu_im = _mm(u, w["wb_im"], name=n("s5_bu_im"))
    xr, xi = _s5_scan_fwd(w["abar_re"], w["abar_im"], bu_re, bu_im, tb=SCAN_TB, name=n("s5_scan"))
    s5c = [w["wc_re"], w["wc_im"], w["s5_d"], w["glu_w"], w["glu_b"]]
    (ya,) = _stage(f_s5_post, [xr, xi, u], s5c, tb=TB, name=n("s5_post"), out_dtypes=[BF16])
    xc = _conv_fwd(xbc, w["ssd_conv_w"], w["ssd_conv_b"], tb=TB, name=n("ssd_conv"))
    xdt, da, bm, cm = _stage(f_ssd_pre, [xc, dtr], [w["dt_bias"], w["a_log"]], tb=TB, name=n("ssd_pre"),
                             out_dtypes=[F32] * 4)
    y, ck = _ssd_fwd(xdt, da, bm, cm, name=n("ssd_scan"))
    (yb,) = _stage(f_ssd_post, [y, xc, z], [w["ssd_d"], w["ssd_norm"]], tb=TB, name=n("ssd_post"), out_dtypes=[BF16])
    mo = _mm(ya, w["out_a"], name=n("out_a"))
    mo = _mm(yb, w["out_b"], add=mo, name=n("out_b"))
    res = dict(u=u, z=z, xbc=xbc, dtr=dtr, xr=xr, xi=xi, ya=ya, xc=xc, xdt=xdt, da=da, bm=bm, cm=cm, y=y, ck=ck, yb=yb)
    return mo, res


def _even_bwd(dmo, hn, w, r, tag):
    n = lambda s: f"{tag}_{s}"
    g = {}
    g["out_a"] = _mm(r["ya"], dmo, ta=True, name=n("d_out_a"))
    g["out_b"] = _mm(r["yb"], dmo, ta=True, name=n("d_out_b"))
    dya = _mm(dmo, w["out_a"], tb=True, name=n("dya"))
    dyb = _mm(dmo, w["out_b"], tb=True, name=n("dyb"))
    dy, dxc1, dz, g["ssd_d"], g["ssd_norm"] = _stage_vjp(
        f_ssd_post, [r["y"], r["xc"], r["z"]], [w["ssd_d"], w["ssd_norm"]], [dyb], tb=TBH, name=n("ssd_post_b"),
        drow=[0, 1, 2], dconst=[0, 1])
    dxdt, dda, dbm, dcm = _ssd_bwd(r["xdt"], r["da"], r["bm"], r["cm"], r["ck"], dy, name=n("ssd_scan_b"))
    dxc, ddtr, g["dt_bias"], g["a_log"] = _stage_vjp(
        f_ssd_pre_pass, [r["xc"], r["dtr"]], [w["dt_bias"], w["a_log"]], [dxdt, dda, dbm, dcm, dxc1], tb=TBH,
        name=n("ssd_pre_b"), drow=[0, 1], dconst=[0, 1])
    dxbc, g["ssd_conv_w"], g["ssd_conv_b"] = _conv_bwd(r["xbc"], w["ssd_conv_w"], dxc, tb=TB, name=n("ssd_conv_b"))
    s5c = [w["wc_re"], w["wc_im"], w["s5_d"], w["glu_w"], w["glu_b"]]
    dxr, dxi, du1, g["wc_re"], g["wc_im"], g["s5_d"], g["glu_w"], g["glu_b"] = _stage_vjp(
        f_s5_post, [r["xr"], r["xi"], r["u"]], s5c, [dya], tb=TBH, name=n("s5_post_b"),
        drow=[0, 1, 2], dconst=[0, 1, 2, 3, 4])
    dbr, dbi, g["abar_re"], g["abar_im"] = _s5_scan_bwd(w["abar_re"], w["abar_im"], r["xr"], r["xi"], dxr, dxi,
                                                         tb=SCAN_TB, name=n("s5_scan_b"))
    g["wb_re"] = _mm(r["u"], dbr, ta=True, name=n("d_wb_re"))
    g["wb_im"] = _mm(r["u"], dbi, ta=True, name=n("d_wb_im"))
    du = _mm(dbr, w["wb_re"], tb=True, add=du1, name=n("du_re"))
    du = _mm(dbi, w["wb_im"], tb=True, add=du, name=n("du_im"))
    segs = (("in_u", du), ("in_z", dz), ("in_xbc", dxbc), ("in_dt", ddtr))
    dhn = None
    for key, dseg in segs:
        g[key] = _mm(hn, dseg, ta=True, name=n("d_" + key))
        dhn = _mm(dseg, w[key], tb=True, add=dhn, name=n("dhn_" + key))
    return dhn, g


def _odd_fwd(hn, w, tag):
    n = lambda s: f"{tag}_{s}"
    rw = _mm(hn, w["in_rw"], name=n("proj_rw"))
    xl = _mm(hn, w["in_xl"], name=n("proj_xl"))
    gl = _mm(hn, w["in_gl"], name=n("proj_gl"))
    f = _conv_fwd(rw, w["mix_w"], w["mix_b"], tb=TB, name=n("rwkv_shift"))
    rc = [w[k] for k in ("w0", "w_up", "a0", "a_up", "g_up", "k_k", "k_a")]
    r_, dec, k2, v, kkn, a, gate = _stage(f_rwkv_pre, [f], rc, tb=TB, name=n("rwkv_pre"), out_dtypes=[F32] * 7)
    t3 = lambda z: z.reshape(-1, RW_PAIRS, 128)
    y, ck = _rwkv_scan_fwd(t3(r_), t3(dec), t3(k2), t3(v), t3(kkn), t3(a), lc=RW_LC, name=n("rwkv_scan"))
    y = y.reshape(-1, RW_W)
    (yc,) = _stage(f_rwkv_post, [y, r_, k2, v, gate], [w["ln_g"], w["ln_b"], w["r_k"]], tb=TB, name=n("rwkv_post"),
                   out_dtypes=[BF16])
    xc = _conv_fwd(xl, w["lru_conv_w"], w["lru_conv_b"], tb=TB, name=n("lru_conv"))
    lc = [w[k] for k in ("lru_wa", "lru_b_a", "lru_wx", "lru_b_x", "lru_lam")]
    a_l, bx = _stage(f_lru_pre, [xc], lc, tb=TB, name=n("lru_pre"), out_dtypes=[F32] * 2, pos=True)
    h = _lru_scan_fwd(a_l, bx, tb=SCAN_TB, name=n("lru_scan"))
    (yd,) = _stage(f_lru_post, [h, gl], [], tb=TB, name=n("lru_post"), out_dtypes=[BF16])
    mo = _mm(yc, w["out_a"], name=n("out_a"))
    mo = _mm(yd, w["out_b"], add=mo, name=n("out_b"))
    res = dict(rw=rw, xl=xl, gl=gl, f=f, r=r_, dec=dec, k2=k2, v=v, kkn=kkn, a=a, gate=gate, y=y, ck=ck, yc=yc,
               xc=xc, a_l=a_l, h=h, yd=yd)
    return mo, res


def _odd_bwd(dmo, hn, w, r, tag):
    n = lambda s: f"{tag}_{s}"
    g = {}
    g["out_a"] = _mm(r["yc"], dmo, ta=True, name=n("d_out_a"))
    g["out_b"] = _mm(r["yd"], dmo, ta=True, name=n("d_out_b"))
    dyc = _mm(dmo, w["out_a"], tb=True, name=n("dyc"))
    dyd = _mm(dmo, w["out_b"], tb=True, name=n("dyd"))
    dh, dgl = _stage_vjp(f_lru_post, [r["h"], r["gl"]], [], [dyd], tb=TB, name=n("lru_post_b"), drow=[0, 1], dconst=[])
    da_l, dbx = _lru_scan_bwd(r["a_l"], r["h"], dh, tb=SCAN_TB, name=n("lru_scan_b"))
    lc = [w[k] for k in ("lru_wa", "lru_b_a", "lru_wx", "lru_b_x", "lru_lam")]
    dxc, g["lru_wa"], g["lru_b_a"], g["lru_wx"], g["lru_b_x"], g["lru_lam"] = _stage_vjp(
        f_lru_pre, [r["xc"]], lc, [da_l, dbx], tb=TBH, name=n("lru_pre_b"), drow=[0], dconst=[0, 1, 2, 3, 4], pos=True)
    dxl, g["lru_conv_w"], g["lru_conv_b"] = _conv_bwd(r["xl"], w["lru_conv_w"], dxc, tb=TB, name=n("lru_conv_b"))
    dy, dr1, dk1, dv1, dgate, g["ln_g"], g["ln_b"], g["r_k"] = _stage_vjp(
        f_rwkv_post, [r["y"], r["r"], r["k2"], r["v"], r["gate"]], [w["ln_g"], w["ln_b"], w["r_k"]], [dyc], tb=TBH,
        name=n("rwkv_post_b"), drow=[0, 1, 2, 3, 4], dconst=[0, 1, 2])
    t3 = lambda z: z.reshape(-1, RW_PAIRS, 128)
    dr2, ddec, dk2, dv2, dkkn, da = [z.reshape(-1, RW_W) for z in _rwkv_scan_bwd(
        t3(r["r"]), t3(r["dec"]), t3(r["k2"]), t3(r["v"]), t3(r["kkn"]), t3(r["a"]), r["ck"], t3(dy),
        lc=RW_LC, name=n("rwkv_scan_b"))]
    rc = [w[k] for k in ("w0", "w_up", "a0", "a_up", "g_up", "k_k", "k_a")]
    df, g["w0"], g["w_up"], g["a0"], g["a_up"], g["g_up"], g["k_k"], g["k_a"] = _stage_vjp(
        f_rwkv_pre_pass, [r["f"]], rc, [dr2, ddec, dk2, dv2, dkkn, da, dgate, dr1, dk1, dv1], tb=TBH,
        name=n("rwkv_pre_b"), drow=[0], dconst=[0, 1, 2, 3, 4, 5, 6])
    drw, g["mix_w"], _ = _conv_bwd(r["rw"], w["mix_w"], df, tb=TB, name=n("rwkv_shift_b"))
    segs = (("in_rw", drw), ("in_xl", dxl), ("in_gl", dgl))
    dhn = None
    for key, dseg in segs:
        g[key] = _mm(hn, dseg, ta=True, name=n("d_" + key))
        dhn = _mm(dseg, w[key], tb=True, add=dhn, name=n("dhn_" + key))
    return dhn, g


def _layer_fwd(h, p_i, w, odd, tag):
    n = lambda s: f"{tag}_{s}"
    (hn,) = _stage(f_norm, [h], [w["norm_mix"]], tb=TB, name=n("norm_mix"), out_dtypes=[BF16])
    mo, mres = (_odd_fwd if odd else _even_fwd)(hn, w, tag)
    h1, hf = _stage(f_add_norm, [h, mo], [w["norm_ffn"]], tb=TB, name=n("norm_ffn"), out_dtypes=[F32, BF16])
    u, act = _mm(hf, w["mlp_w1"], name=n("mlp_up"), epilogue=lambda acc: (acc,) + f_relu2(acc), out_dtypes=[F32, BF16])
    m2 = _mm(act, w["mlp_w2"], name=n("mlp_down"))
    h2, hp = _stage(f_add_norm, [h1, m2], [w["norm_pl"]], tb=TB, name=n("norm_pl"), out_dtypes=[F32, BF16])
    gl = _mm(hp, w["pl_gate"], name=n("pl_gate"))
    pp = _mm(p_i, w["pl_proj"], name=n("pl_proj"))
    (h3,) = _stage(f_plgate, [h2, gl, pp], [], tb=TB, name=n("pl_mix"), out_dtypes=[F32])
    res = dict(h=h, hn=hn, mo=mo, mix=mres, h1=h1, hf=hf, u=u, act=act, m2=m2, h2=h2, hp=hp, gl=gl, pp=pp)
    return h3, res


def _layer_bwd(dh3, p_i, w, r, odd, tag):
    n = lambda s: f"{tag}_{s}"
    g = {}
    dh2, dgl, dpp = _stage_vjp(f_plgate, [r["h2"], r["gl"], r["pp"]], [], [dh3], tb=TB, name=n("pl_mix_b"),
                               drow=[0, 1, 2], dconst=[])
    g["pl_proj"] = _mm(p_i, dpp, ta=True, name=n("d_pl_proj"))
    g["pl_gate"] = _mm(r["hp"], dgl, ta=True, name=n("d_pl_gate"))
    dhp = _mm(dgl, w["pl_gate"], tb=True, name=n("dhp"))
    dh1, dm2, g["norm_pl"] = _stage_vjp(f_add_norm, [r["h1"], r["m2"]], [w["norm_pl"]], [dh2, dhp], tb=TB,
                                        name=n("norm_pl_b"), drow=[0, 1], dconst=[0])
    g["mlp_w2"] = _mm(r["act"], dm2, ta=True, name=n("d_mlp_w2"))
    (du,) = _mm(dm2, w["mlp_w2"], tb=True, name=n("dact"), extra=[r["u"]], out_dtypes=[BF16],
                epilogue=lambda acc, u: (acc * (2.0 * jnp.maximum(u, 0.0)),))
    g["mlp_w1"] = _mm(r["hf"], du, ta=True, name=n("d_mlp_w1"))
    dhf = _mm(du, w["mlp_w1"], tb=True, name=n("dhf"))
    dh, dmo, g["norm_ffn"] = _stage_vjp(f_add_norm, [r["h"], r["mo"]], [w["norm_ffn"]], [dh1, dhf], tb=TB,
                                        name=n("norm_ffn_b"), drow=[0, 1], dconst=[0])
    dhn, gm = (_odd_bwd if odd else _even_bwd)(dmo, r["hn"], w, r["mix"], tag)
    g.update(gm)
    dh0, g["norm_mix"] = _stage_vjp(f_norm_pass, [r["h"]], [w["norm_mix"]], [dhn, dh], tb=TB, name=n("norm_mix_b"),
                                    drow=[0], dconst=[0])
    return dh0, g


def _pad_to(a, size, axis):
    pad = [(0, 0)] * a.ndim
    pad[axis] = (0, size - a.shape[axis])
    return jnp.pad(a, pad)


def _rw_pad(a):
    return jnp.concatenate([a[..., :3072], _pad_to(a[..., 3072:3168], 128, -1), _pad_to(a[..., 3168:3264], 128, -1),
                            a[..., 3264:3520]], axis=-1)


def _rw_unpad(a):
    return jnp.concatenate([a[..., :3072], a[..., 3072:3168], a[..., 3200:3296], a[..., 3328:3584]], axis=-1)


def _block_diag(w):
    nb, bs, _ = w.shape
    eye = jnp.eye(nb, dtype=w.dtype)
    return (w[:, :, None, :] * eye[:, None, :, None]).reshape(nb * bs, nb * bs)


def _diag_blocks(w):
    nb = LRU_B
    bs = w.shape[0] // nb
    return jnp.stack([w[h * bs:(h + 1) * bs, h * bs:(h + 1) * bs] for h in range(nb)])


def _s5_prep_inputs(fw):
    lstep = jnp.broadcast_to(_pad_to(fw["s5_log_step"].astype(F32), 128, 1), (8, 128))
    t16 = lambda b: jnp.transpose(b[0], (2, 0, 1)).reshape(S5_G, S5_N)
    tc = lambda c: jnp.transpose(c[0], (0, 2, 1)).reshape(S5_N, S5_G)
    return [fw["s5_lam_re"].reshape(1, S5_N), fw["s5_lam_im"].reshape(1, S5_N), lstep,
            t16(fw["s5_b_re"]), t16(fw["s5_b_im"]), tc(fw["s5_c_re"]), tc(fw["s5_c_im"])]


def _layer_weights(fw, i):
    w = {k: fw[k][i:i + 1] for k in ("norm_mix", "norm_ffn", "norm_pl")}
    for k in ("mlp_w1", "mlp_w2", "pl_proj", "pl_gate"):
        w[k] = fw[k][i]
    return w


def _even_weights(fw, prep):
    w = _layer_weights(fw, 0)
    ein, eout = fw["e_in_proj"][0], fw["e_out_proj"][0]
    w.update(in_u=ein[:, :512], in_z=ein[:, 512:2048], in_xbc=ein[:, 2048:4608], in_dt=_pad_to(ein[:, 4608:], 128, 1),
             out_a=eout[:512], out_b=eout[512:])
    abar_re, abar_im, wb_re, wb_im, wc_re, wc_im = prep
    w.update(abar_re=abar_re, abar_im=abar_im, wb_re=wb_re, wb_im=wb_im, wc_re=wc_re.astype(BF16), wc_im=wc_im.astype(BF16),
             s5_d=fw["s5_d"], glu_w=fw["s5_glu_w"][0], glu_b=fw["s5_glu_b"],
             ssd_conv_w=_pad_to(fw["ssd_conv_w"][0], 8, 0), ssd_conv_b=fw["ssd_conv_b"],
             dt_bias=_pad_to(fw["ssd_dt_bias"], 128, 1), a_log=_pad_to(fw["ssd_a_log"], 128, 1),
             ssd_d=_pad_to(fw["ssd_d"], 128, 1), ssd_norm=fw["ssd_norm"])
    return w


def _odd_weights(fw):
    w = _layer_weights(fw, 1)
    oin, oout = fw["o_in_proj"][0], fw["o_out_proj"][0]
    mu = _rw_pad(fw["rwkv_mu"])
    zero = jnp.zeros_like(mu)
    w.update(in_rw=_rw_pad(oin[:, :RW_IN]), in_xl=oin[:, RW_IN:RW_IN + LRU_W], in_gl=oin[:, RW_IN + LRU_W:],
             out_a=oout[:RW_W], out_b=oout[RW_W:],
             mix_w=jnp.concatenate([zero, zero, mu, 1.0 - mu, zero, zero, zero, zero], axis=0), mix_b=zero,
             w0=fw["rwkv_w0"], w_up=_pad_to(fw["rwkv_w_up"][0], 128, 0), a0=fw["rwkv_a0"],
             a_up=_pad_to(fw["rwkv_a_up"][0], 128, 0), g_up=fw["rwkv_g_up"][0], k_k=fw["rwkv_k_k"], k_a=fw["rwkv_k_a"],
             r_k=fw["rwkv_r_k"].reshape(1, RW_W), ln_g=fw["rwkv_ln_g"], ln_b=fw["rwkv_ln_b"],
             lru_conv_w=_pad_to(fw["lru_conv_w"][0], 8, 0), lru_conv_b=fw["lru_conv_b"],
             lru_wa=_block_diag(fw["lru_w_a"][0]).astype(BF16), lru_b_a=fw["lru_b_a"].reshape(1, LRU_W),
             lru_wx=_block_diag(fw["lru_w_x"][0]).astype(BF16), lru_b_x=fw["lru_b_x"].reshape(1, LRU_W),
             lru_lam=fw["lru_lam"].reshape(1, LRU_W))
    return w


def _global_grads(g0, g1, s5_grads, d_norm_final):
    out = {k: jnp.concatenate([g0[k], g1[k]], axis=0) for k in ("norm_mix", "norm_ffn", "norm_pl")}
    for k in ("mlp_w1", "mlp_w2", "pl_proj", "pl_gate"):
        out[k] = jnp.stack([g0[k], g1[k]])
    out["e_in_proj"] = jnp.concatenate([g0["in_u"], g0["in_z"], g0["in_xbc"], g0["in_dt"][:, :SSD_H]], axis=1)[None]
    out["e_out_proj"] = jnp.concatenate([g0["out_a"], g0["out_b"]], axis=0)[None]
    d_lam_re, d_lam_im, d_lstep, d_bre, d_bim, d_cre, d_cim = s5_grads
    out["s5_lam_re"] = d_lam_re.reshape(1, S5_GROUPS, S5_P)
    out["s5_lam_im"] = d_lam_im.reshape(1, S5_GROUPS, S5_P)
    out["s5_log_step"] = d_lstep[0:1, :S5_GROUPS]
    unb = lambda b: jnp.transpose(b.reshape(S5_G, S5_GROUPS, S5_P), (1, 2, 0))[None]
    unc = lambda c: jnp.transpose(c.reshape(S5_GROUPS, S5_P, S5_G), (0, 2, 1))[None]
    out.update(s5_b_re=unb(d_bre), s5_b_im=unb(d_bim), s5_c_re=unc(d_cre), s5_c_im=unc(d_cim),
               s5_d=g0["s5_d"], s5_glu_w=g0["glu_w"][None], s5_glu_b=g0["glu_b"],
               ssd_conv_w=g0["ssd_conv_w"][None, :4], ssd_conv_b=g0["ssd_conv_b"], ssd_dt_bias=g0["dt_bias"][:, :SSD_H],
               ssd_a_log=g0["a_log"][:, :SSD_H], ssd_d=g0["ssd_d"][:, :SSD_H], ssd_norm=g0["ssd_norm"])
    out["o_in_proj"] = jnp.concatenate([_rw_unpad(g1["in_rw"]), g1["in_xl"], g1["in_gl"]], axis=1)[None]
    out["o_out_proj"] = jnp.concatenate([g1["out_a"], g1["out_b"]], axis=0)[None]
    out.update(rwkv_mu=_rw_unpad(g1["mix_w"][2:3] - g1["mix_w"][3:4]), rwkv_w0=g1["w0"], rwkv_w_up=g1["w_up"][None, :RW_LORA],
               rwkv_a0=g1["a0"], rwkv_a_up=g1["a_up"][None, :RW_LORA], rwkv_g_up=g1["g_up"][None], rwkv_k_k=g1["k_k"],
               rwkv_k_a=g1["k_a"], rwkv_r_k=g1["r_k"].reshape(1, RW_H, RW_HD), rwkv_ln_g=g1["ln_g"], rwkv_ln_b=g1["ln_b"],
               lru_conv_w=g1["lru_conv_w"][None, :4], lru_conv_b=g1["lru_conv_b"],
               lru_w_a=_diag_blocks(g1["lru_wa"])[None], lru_b_a=g1["lru_b_a"].reshape(1, LRU_B, 64),
               lru_w_x=_diag_blocks(g1["lru_wx"])[None], lru_b_x=g1["lru_b_x"].reshape(1, LRU_B, 64),
               lru_lam=g1["lru_lam"].reshape(1, LRU_B, 64), norm_final=d_norm_final.reshape(D))
    return out


def _local_step(x, p, target, fw):
    prep_in = _s5_prep_inputs(fw)
    prep = _single(f_s5_prep, prep_in, name="s5_prep")
    w0, w1 = _even_weights(fw, prep), _odd_weights(fw)
    h1, r0 = _layer_fwd(x, p[0], w0, False, "l0")
    h2, r1 = _layer_fwd(h1, p[1], w1, True, "l1")
    gf = fw["norm_final"].reshape(1, D)
    (loss8,) = _stage(f_loss, [h2, target], [gf], tb=TB, name="loss", out_dtypes=[], n_acc=1)
    one = jnp.zeros((8, 128), F32).at[0, 0].set(1.0)
    dh2, d_gf = _stage_vjp(f_loss, [h2, target], [gf], [], tb=TB, name="loss_b", drow=[0], dconst=[0], acc_cots=[one])
    dh1, g1 = _layer_bwd(dh2, p[1], w1, r1, True, "l1")
    dx, g0 = _layer_bwd(dh1, p[0], w0, r0, False, "l0")
    cots = [g0[k] for k in ("abar_re", "abar_im", "wb_re", "wb_im", "wc_re", "wc_im")]
    s5_grads = _single_vjp(f_s5_prep, prep_in, cots, name="s5_prep_b")
    return loss8[0, 0], dx, _global_grads(g0, g1, s5_grads, d_gf)


def _xyc():
    return lax.axis_index("x"), lax.axis_index("y"), lax.axis_index("c")


def _flip(v, bit):
    return 1 - v if bit else v


def _remote(src, dst, send_sems, recv_sems, k, dev):
    return pltpu.make_async_remote_copy(src_ref=src, dst_ref=dst, send_sem=send_sems.at[k], recv_sem=recv_sems.at[k],
                                        device_id=dev, device_id_type=MESH)


def _dma_scratch(n_remote, n_local):
    return [pltpu.SemaphoreType.DMA((n_remote,)), pltpu.SemaphoreType.DMA((n_remote,)), pltpu.SemaphoreType.DMA((n_local,))]


CHIP_FLIPS = ((1, 0), (0, 1), (1, 1))


def _gather_chips(arrs, *, name):
    n = len(arrs)

    def body(*refs):
        ins, outs = refs[:n], refs[n:2 * n]
        send_sems, recv_sems = refs[2 * n:]
        x, y, c = _xyc()
        chip, sib = 2 * x + y, (x, y, 1 - c)
        peers = [(_flip(x, fx), _flip(y, fy)) for fx, fy in CHIP_FLIPS]
        first = [_remote(ins[a].at[c], outs[a].at[chip, c], send_sems, recv_sems, 6 * a + j, (px, py, c))
                 for a in range(n) for j, (px, py) in enumerate(peers)]
        for cp in first:
            cp.start()
        passed = []
        for a in range(n):
            for j, (px, py) in enumerate(peers):
                landed = outs[a].at[2 * px + py, c]
                _remote(ins[a].at[c], landed, send_sems, recv_sems, 6 * a + j, (px, py, c)).wait_recv()
                cp = _remote(landed, landed, send_sems, recv_sems, 6 * a + 3 + j, sib)
                cp.start()
                passed.append(cp)
        for a in range(n):
            for j, (px, py) in enumerate(peers):
                other = outs[a].at[2 * px + py, 1 - c]
                _remote(other, other, send_sems, recv_sems, 6 * a + 3 + j, sib).wait_recv()
        for cp in first + passed:
            cp.wait_send()

    return pl.pallas_call(
        body, out_shape=[SDS((4,) + a.shape, a.dtype) for a in arrs], in_specs=[ANY] * n, out_specs=[ANY] * n,
        scratch_shapes=_dma_sems(6 * n), name=name,
    )(*arrs)


def _gather_all(arr, *, name):
    def body(in_ref, out_ref, send_sems, recv_sems, loc_sems):
        x, y, c = _xyc()
        mine = out_ref.at[4 * x + 2 * y + c]
        lc = pltpu.make_async_copy(in_ref, mine, loc_sems.at[0])
        lc.start()
        sends = []
        for k in range(1, 8):
            dev = (_flip(x, k >> 2 & 1), _flip(y, k >> 1 & 1), _flip(c, k & 1))
            cp = _remote(in_ref, mine, send_sems, recv_sems, k - 1, dev)
            cp.start()
            sends.append(cp)
        for k in range(1, 8):
            px, py, pc = _flip(x, k >> 2 & 1), _flip(y, k >> 1 & 1), _flip(c, k & 1)
            _remote(in_ref, out_ref.at[4 * px + 2 * py + pc], send_sems, recv_sems, k - 1, (px, py, pc)).wait_recv()
        for cp in sends:
            cp.wait_send()
        lc.wait()

    return pl.pallas_call(
        body, out_shape=SDS((8,) + arr.shape, arr.dtype), in_specs=[ANY], out_specs=ANY,
        scratch_shapes=_dma_scratch(7, 1), name=name,
    )(arr)


def _dma_sems(n):
    return [pltpu.SemaphoreType.DMA((n,)), pltpu.SemaphoreType.DMA((n,))]


def _send_halves(arrs, *, name):
    n = len(arrs)

    def body(*refs):
        ins, outs = refs[:n], refs[n:2 * n]
        send_sems, recv_sems = refs[2 * n:]
        x, y, c = _xyc()
        copies = [_remote(ins[a].at[k, 1 - c], outs[a].at[k], send_sems, recv_sems, 4 * a + k, (x, y, 1 - c))
                  for a in range(n) for k in range(4)]
        for cp in copies:
            cp.start()
        for cp in copies:
            cp.wait_recv()
        for cp in copies:
            cp.wait_send()

    return pl.pallas_call(
        body, out_shape=[SDS((4,) + a.shape[2:], a.dtype) for a in arrs], in_specs=[ANY] * n, out_specs=[ANY] * n,
        scratch_shapes=_dma_sems(4 * n), name=name,
    )(*arrs)


def _add_half(g, recv, c_vec, *, tb, out_dtype, name):
    _, _, rh, cols = g.shape
    tb = min(tb, rh)

    def body(c_ref, g_ref, r_ref, o_ref):
        o_ref[...] = (g_ref[...] + r_ref[...]).astype(o_ref.dtype)

    return pl.pallas_call(
        body,
        grid_spec=pltpu.PrefetchScalarGridSpec(
            num_scalar_prefetch=1, grid=(4, rh // tb),
            in_specs=[pl.BlockSpec((None, None, tb, cols), lambda k, i, c_ref: (k, c_ref[0], i, 0)),
                      pl.BlockSpec((None, tb, cols), lambda k, i, c_ref: (k, i, 0))],
            out_specs=pl.BlockSpec((None, tb, cols), lambda k, i, c_ref: (k, i, 0))),
        out_shape=SDS((4, rh, cols), out_dtype), compiler_params=_cparams(("arbitrary", "arbitrary")), name=name,
    )(c_vec, g, recv)


def _scatter_chips(arrs, *, name):
    n = len(arrs)

    def body(*refs):
        ins, outs = refs[:n], refs[n:2 * n]
        send_sems, recv_sems = refs[2 * n:]
        x, y, c = _xyc()
        copies = []
        for a in range(n):
            for j, (fx, fy) in enumerate(CHIP_FLIPS):
                px, py = _flip(x, fx), _flip(y, fy)
                copies.append(_remote(ins[a].at[2 * px + py], outs[a].at[j], send_sems, recv_sems, 3 * a + j, (px, py, c)))
        for cp in copies:
            cp.start()
        for cp in copies:
            cp.wait_recv()
        for cp in copies:
            cp.wait_send()

    return pl.pallas_call(
        body, out_shape=[SDS((3,) + a.shape[1:], a.dtype) for a in arrs], in_specs=[ANY] * n, out_specs=[ANY] * n,
        scratch_shapes=_dma_sems(3 * n), name=name,
    )(*arrs)


def _sum_chips(p, landed, chip_vec, *, tb, name):
    _, rh, cols = p.shape
    tb = min(tb, rh)

    def body(chip_ref, p_ref, l_ref, o_ref):
        f = lambda z: z.astype(F32)
        o_ref[...] = ((f(p_ref[...]) + f(l_ref[0])) + f(l_ref[1])) + f(l_ref[2])

    return pl.pallas_call(
        body,
        grid_spec=pltpu.PrefetchScalarGridSpec(
            num_scalar_prefetch=1, grid=(rh // tb,),
            in_specs=[pl.BlockSpec((None, tb, cols), lambda i, chip_ref: (chip_ref[0], i, 0)),
                      pl.BlockSpec((3, tb, cols), lambda i, chip_ref: (0, i, 0))],
            out_specs=pl.BlockSpec((tb, cols), lambda i, chip_ref: (i, 0))),
        out_shape=SDS((rh, cols), F32), compiler_params=_cparams(), name=name,
    )(chip_vec, p, landed)


def _swap_halves(arrs, *, name):
    n = len(arrs)

    def body(*refs):
        ins, outs = refs[:n], refs[n:2 * n]
        send_sems, recv_sems = refs[2 * n:]
        x, y, c = _xyc()
        copies = [_remote(ins[a], outs[a], send_sems, recv_sems, a, (x, y, 1 - c)) for a in range(n)]
        for cp in copies:
            cp.start()
        for cp in copies:
            cp.wait_recv()
        for cp in copies:
            cp.wait_send()

    return pl.pallas_call(
        body, out_shape=[SDS(a.shape, a.dtype) for a in arrs], in_specs=[ANY] * n, out_specs=[ANY] * n,
        scratch_shapes=_dma_sems(n), name=name,
    )(*arrs)


def _join_halves(mine, theirs, c_vec, *, tb, name):
    rh, cols = mine.shape
    tb = min(tb, rh)

    def body(c_ref, m_ref, t_ref, o_ref):
        o_ref[...] = jnp.where(pl.program_id(0) == c_ref[0], m_ref[...], t_ref[...])

    blk = pl.BlockSpec((tb, cols), lambda h, i, c_ref: (i, 0))
    return pl.pallas_call(
        body,
        grid_spec=pltpu.PrefetchScalarGridSpec(
            num_scalar_prefetch=1, grid=(2, rh // tb), in_specs=[blk, blk],
            out_specs=pl.BlockSpec((None, tb, cols), lambda h, i, c_ref: (h, i, 0))),
        out_shape=SDS((2, rh, cols), mine.dtype), compiler_params=_cparams(("arbitrary", "arbitrary")), name=name,
    )(c_vec, mine, theirs)


def _sum_lead(x, *, tb, name):
    k, r, c = x.shape
    tb = min(tb, r)
    assert r % tb == 0

    def body(x_ref, o_ref):
        acc = x_ref[0]
        for q in range(1, k):
            acc = acc + x_ref[q]
        o_ref[...] = acc

    return pl.pallas_call(
        body, grid=(r // tb,), in_specs=[pl.BlockSpec((k, tb, c), lambda i: (0, i, 0))],
        out_specs=pl.BlockSpec((tb, c), lambda i: (i, 0)), out_shape=SDS((r, c), x.dtype),
        compiler_params=_cparams(), name=name,
    )(x)


def f_adamw(w, g, m, v):
    m = ADAM_B1 * m + (1.0 - ADAM_B1) * g
    v = ADAM_B2 * v + (1.0 - ADAM_B2) * (g * g)
    m_hat = m / (1.0 - ADAM_B1 ** ADAM_STEP)
    v_hat = v / (1.0 - ADAM_B2 ** ADAM_STEP)
    return -ADAM_LR * (m_hat / (jnp.sqrt(v_hat) + ADAM_EPS) + ADAM_WD * w), m, v


def _adamw(w, g, m, v, *, name):
    shape = w.shape
    two = lambda a: a.reshape(-1, shape[-1])
    rows = two(w).shape[0]
    tb = 256 if rows % 256 == 0 else rows
    outs = _stage(f_adamw, [two(w), two(g), two(m), two(v)], [], tb=tb, name=name, out_dtypes=[F32] * 3)
    return [o.reshape(shape) for o in outs]


def _pack(arrs, rows=8):
    flat = jnp.concatenate([a.astype(F32).reshape(-1) for a in arrs])
    size = -(-flat.shape[0] // (rows * 128)) * (rows * 128)
    return _pad_to(flat, size, 0).reshape(-1, 128)


def _unpack(buf, shapes):
    flat = buf.reshape(-1)
    out, off = [], 0
    for s in shapes:
        n = math.prod(s)
        out.append(flat[off:off + n].reshape(s))
        off += n
    return out


WEIGHTS = ("norm_mix", "norm_ffn", "norm_pl", "mlp_w1", "mlp_w2", "pl_proj", "pl_gate", "e_in_proj", "e_out_proj",
           "s5_lam_re", "s5_lam_im", "s5_log_step", "s5_b_re", "s5_b_im", "s5_c_re", "s5_c_im", "s5_d", "s5_glu_w",
           "s5_glu_b", "ssd_conv_w", "ssd_conv_b", "ssd_dt_bias", "ssd_a_log", "ssd_d", "ssd_norm", "o_in_proj",
           "o_out_proj", "rwkv_mu", "rwkv_w0", "rwkv_w_up", "rwkv_a0", "rwkv_a_up", "rwkv_g_up", "rwkv_k_k", "rwkv_k_a",
           "rwkv_r_k", "rwkv_ln_g", "rwkv_ln_b", "lru_conv_w", "lru_conv_b", "lru_w_a", "lru_b_a", "lru_w_x", "lru_b_x",
           "lru_lam", "norm_final")
BIG = ("mlp_w1", "mlp_w2", "pl_proj", "pl_gate", "e_in_proj", "e_out_proj", "o_in_proj", "o_out_proj")
SHARD_AXIS = {"mlp_w1": 2, "mlp_w2": 1, "pl_proj": 2, "pl_gate": 1, "e_in_proj": 2, "e_out_proj": 1, "s5_glu_w": 1,
              "ssd_conv_w": 2, "o_in_proj": 2, "o_out_proj": 1, "rwkv_mu": 1, "rwkv_w0": 1, "rwkv_w_up": 2, "rwkv_a0": 1,
              "rwkv_a_up": 2, "rwkv_g_up": 2, "rwkv_k_k": 1, "rwkv_k_a": 1, "rwkv_ln_g": 1, "rwkv_ln_b": 1,
              "lru_conv_w": 2, "lru_conv_b": 1}
SMALL = tuple(n for n in WEIGHTS if n not in BIG)
SMALL_SHARDED = tuple(n for n in SMALL if n in SHARD_AXIS)


def _gather_weights(w):
    shapes = [w[n].shape for n in SMALL_SHARDED]
    chip = 2 * lax.axis_index("x") + lax.axis_index("y")
    mine = [w[n].astype(BF16) for n in BIG] + [_pack([w[n] for n in SMALL_SHARDED], rows=16)]
    got = _gather_chips([a.reshape(2, -1, a.shape[-1]) for a in mine], name="gather_weights")
    got = [lax.dynamic_update_index_in_dim(g.reshape((4,) + a.shape), a, chip, 0) for g, a in zip(got, mine)]
    fw = {n: w[n] for n in SMALL if n not in SHARD_AXIS}
    for n, g in zip(BIG, got[:-1]):
        fw[n] = jnp.concatenate([g[k] for k in range(4)], axis=SHARD_AXIS[n])
    parts = [_unpack(got[-1][k], shapes) for k in range(4)]
    for i, n in enumerate(SMALL_SHARDED):
        fw[n] = jnp.concatenate([parts[k][i] for k in range(4)], axis=SHARD_AXIS[n])
    return fw


def _reduce_big(grads, w):
    stacks = []
    for n in BIG:
        cols = w[n].shape[-1]
        stacks.append(jnp.stack(jnp.split(grads[n], 4, axis=SHARD_AXIS[n])).reshape(4, 2, -1, cols))
    c_vec = lax.axis_index("c").astype(jnp.int32).reshape(1)
    chip_vec = (2 * lax.axis_index("x") + lax.axis_index("y")).astype(jnp.int32).reshape(1)
    got = _send_halves(stacks, name="reduce_pair")
    sums = [_add_half(s, r, c_vec, tb=512, out_dtype=BF16, name=f"reduce_pair_sum_{n}")
            for n, s, r in zip(BIG, stacks, got)]
    landed = _scatter_chips(sums, name="reduce_chips")
    halves = [_sum_chips(p, l, chip_vec, tb=256, name=f"reduce_chips_sum_{n}") for n, p, l in zip(BIG, sums, landed)]
    theirs = _swap_halves(halves, name="reduce_swap")
    return {n: _join_halves(h, t, c_vec, tb=512, name=f"reduce_join_{n}").reshape(w[n].shape)
            for n, h, t in zip(BIG, halves, theirs)}


def _reduce_small(grads, w, chip, loss):
    shapes = [grads[n].shape for n in SMALL] + [(1,)]
    packed = _pack([grads[n] for n in SMALL] + [loss.reshape(1)])
    total = _sum_lead(_gather_all(packed, name="reduce_small"), tb=packed.shape[0], name="reduce_small_sum")
    *parts, loss_sum = _unpack(total, shapes)
    out = {}
    for n, g in zip(SMALL, parts):
        if n in SHARD_AXIS:
            ax = SHARD_AXIS[n]
            size = w[n].shape[ax]
            g = lax.dynamic_slice_in_dim(g, chip * size, size, axis=ax)
        out[n] = g
    return out, loss_sum[0]


def kernel(x, p, norm_mix, norm_ffn, norm_pl, mlp_w1, mlp_w2, pl_proj, pl_gate, e_in_proj, e_out_proj, s5_lam_re, s5_lam_im, s5_log_step, s5_b_re, s5_b_im, s5_c_re, s5_c_im, s5_d, s5_glu_w, s5_glu_b, ssd_conv_w, ssd_conv_b, ssd_dt_bias, ssd_a_log, ssd_d, ssd_norm, o_in_proj, o_out_proj, rwkv_mu, rwkv_w0, rwkv_w_up, rwkv_a0, rwkv_a_up, rwkv_g_up, rwkv_k_k, rwkv_k_a, rwkv_r_k, rwkv_ln_g, rwkv_ln_b, lru_conv_w, lru_conv_b, lru_w_a, lru_b_a, lru_w_x, lru_b_x, lru_lam, norm_final, loss_target, m_norm_mix, m_norm_ffn, m_norm_pl, m_mlp_w1, m_mlp_w2, m_pl_proj, m_pl_gate, m_e_in_proj, m_e_out_proj, m_s5_lam_re, m_s5_lam_im, m_s5_log_step, m_s5_b_re, m_s5_b_im, m_s5_c_re, m_s5_c_im, m_s5_d, m_s5_glu_w, m_s5_glu_b, m_ssd_conv_w, m_ssd_conv_b, m_ssd_dt_bias, m_ssd_a_log, m_ssd_d, m_ssd_norm, m_o_in_proj, m_o_out_proj, m_rwkv_mu, m_rwkv_w0, m_rwkv_w_up, m_rwkv_a0, m_rwkv_a_up, m_rwkv_g_up, m_rwkv_k_k, m_rwkv_k_a, m_rwkv_r_k, m_rwkv_ln_g, m_rwkv_ln_b, m_lru_conv_w, m_lru_conv_b, m_lru_w_a, m_lru_b_a, m_lru_w_x, m_lru_b_x, m_lru_lam, m_norm_final, v_norm_mix, v_norm_ffn, v_norm_pl, v_mlp_w1, v_mlp_w2, v_pl_proj, v_pl_gate, v_e_in_proj, v_e_out_proj, v_s5_lam_re, v_s5_lam_im, v_s5_log_step, v_s5_b_re, v_s5_b_im, v_s5_c_re, v_s5_c_im, v_s5_d, v_s5_glu_w, v_s5_glu_b, v_ssd_conv_w, v_ssd_conv_b, v_ssd_dt_bias, v_ssd_a_log, v_ssd_d, v_ssd_norm, v_o_in_proj, v_o_out_proj, v_rwkv_mu, v_rwkv_w0, v_rwkv_w_up, v_rwkv_a0, v_rwkv_a_up, v_rwkv_g_up, v_rwkv_k_k, v_rwkv_k_a, v_rwkv_r_k, v_rwkv_ln_g, v_rwkv_ln_b, v_lru_conv_w, v_lru_conv_b, v_lru_w_a, v_lru_b_a, v_lru_w_x, v_lru_b_x, v_lru_lam, v_norm_final):
    given = dict(locals())
    w = {n: given[n] for n in WEIGHTS}
    m = {n: given["m_" + n] for n in WEIGHTS}
    v = {n: given["v_" + n] for n in WEIGHTS}
    chip = 2 * lax.axis_index("x") + lax.axis_index("y")

    fw = _gather_weights(w)
    loss, dx, grads = _local_step(x[0], p[:, 0], loss_target[0], fw)
    g = _reduce_big(grads, w)
    g_small, loss = _reduce_small(grads, w, chip, loss)
    g.update(g_small)

    delta, new_m, new_v = {}, {}, {}
    for n in BIG:
        delta[n], new_m[n], new_v[n] = _adamw(w[n], g[n], m[n], v[n], name=f"adamw_{n}")
    shapes = [w[n].shape for n in SMALL]
    packed = [_pack([d[n] for n in SMALL]) for d in (w, g, m, v)]
    for d, buf in zip((delta, new_m, new_v), _adamw(*packed, name="adamw_small")):
        d.update(zip(SMALL, _unpack(buf, shapes)))
    return (loss, dx[None], *[g[n] for n in WEIGHTS], *[delta[n] for n in WEIGHTS],
            *[new_m[n] for n in WEIGHTS], *[new_v[n] for n in WEIGHTS])
```

```python
import functools
import math

import jax
import jax.numpy as jnp
from jax import lax
from jax.experimental import pallas as pl
from jax.experimental.pallas import tpu as pltpu

F32 = jnp.float32
BF16 = jnp.bfloat16
HI = lax.Precision.HIGHEST
MESH = pl.DeviceIdType.MESH
SDS = jax.ShapeDtypeStruct
VMEM_LIMIT = 56 * 1024 * 1024
ANY = pl.BlockSpec(memory_space=pl.ANY)

D = 2048
PL_DIM = 256
D_FF = 4 * D
EPS = 1e-6
S5_W, S5_G, S5_GROUPS, S5_P = 512, 16, 32, 64
S5_N = S5_GROUPS * S5_P
SSD_W, SSD_HD, SSD_H, SSD_NG, SSD_N, SSD_L = 1536, 64, 24, 4, 128, 128
SSD_CONV = SSD_W + 2 * SSD_NG * SSD_N
EVEN_IN = S5_W + SSD_W + SSD_CONV + SSD_H
EVEN_PAD = 5120
RW_W, RW_H, RW_HD = 1024, 16, 64
RW_LORA = 96
RW_GATE = 256
RW_IN = 3 * RW_W + 2 * RW_LORA + RW_GATE
RW_PAD = 3584
LRU_W, LRU_B = 1024, 16
ODD_IN = RW_IN + 2 * LRU_W
ODD_PAD = RW_PAD + 2 * LRU_W
GN_EPS = 64e-5
LRU_C = 8.0
ADAM_LR, ADAM_B1, ADAM_B2, ADAM_EPS, ADAM_WD, ADAM_STEP = 0.001, 0.9, 0.999, 1e-08, 0.01, 10


def _cparams(sem=("arbitrary",)):
    return pltpu.CompilerParams(dimension_semantics=sem, vmem_limit_bytes=VMEM_LIMIT)


def _dot16(a, b, dims=(((1,), (0,)), ((), ()))):
    return lax.dot_general(a.astype(BF16), b.astype(BF16), dims, preferred_element_type=F32)


def _dot32(a, b, dims=(((1,), (0,)), ((), ()))):
    return lax.dot_general(a.astype(F32), b.astype(F32), dims, precision=HI, preferred_element_type=F32)


NT = (((1,), (1,)), ((), ()))
TN = (((0,), (0,)), ((), ()))


def _tile(dim, target):
    if dim <= target:
        return dim
    t = target - target % 128
    while t > 128 and dim % t:
        t -= 128
    assert dim % t == 0, (dim, target)
    return t


def _mm(a, b, *, ta=False, tb=False, add=None, out_dtype=F32, tm=1024, tn=1024, tk=1024, name,
        epilogue=None, extra=(), out_dtypes=None):
    m, k = (a.shape[1], a.shape[0]) if ta else a.shape
    n = b.shape[0] if tb else b.shape[1]
    assert (b.shape[1] if tb else b.shape[0]) == k, (a.shape, b.shape, ta, tb)
    tm, tn, tk = _tile(m, tm), _tile(n, tn), _tile(k, tk)
    nk = k // tk
    dims = (((0 if ta else 1,), (1 if tb else 0,)), ((), ()))
    ins = [a, b] + ([add] if add is not None else []) + list(extra)
    out_dtypes = out_dtypes or [out_dtype]
    n_in, n_out = len(ins), len(out_dtypes)

    def body(*refs):
        a_ref, b_ref = refs[:2]
        out_refs, acc_ref = refs[n_in:n_in + n_out], refs[-1]
        kk = pl.program_id(2)

        @pl.when(kk == 0)
        def _():
            acc_ref[...] = refs[2][...].astype(F32) if add is not None else jnp.zeros_like(acc_ref)

        acc_ref[...] += _dot16(a_ref[...], b_ref[...], dims)

        @pl.when(kk == nk - 1)
        def _():
            acc = acc_ref[...]
            res = epilogue(acc, *[r[...] for r in refs[n_in - len(extra):n_in]]) if epilogue else (acc,)
            for o_ref, val in zip(out_refs, res):
                o_ref[...] = val.astype(o_ref.dtype)

    a_spec = pl.BlockSpec((tk, tm), lambda i, j, q: (q, i)) if ta else pl.BlockSpec((tm, tk), lambda i, j, q: (i, q))
    b_spec = pl.BlockSpec((tn, tk), lambda i, j, q: (j, q)) if tb else pl.BlockSpec((tk, tn), lambda i, j, q: (q, j))
    o_spec = pl.BlockSpec((tm, tn), lambda i, j, q: (i, j))
    outs = pl.pallas_call(
        body,
        grid=(m // tm, n // tn, nk),
        in_specs=[a_spec, b_spec] + [o_spec] * (n_in - 2),
        out_specs=[o_spec] * n_out,
        out_shape=[SDS((m, n), dt) for dt in out_dtypes],
        scratch_shapes=[pltpu.VMEM((tm, tn), F32)],
        compiler_params=_cparams(("parallel", "parallel", "arbitrary")),
        name=name,
    )(*ins)
    return outs if epilogue else outs[0]


def _mm_grad(x, dy, *, layer, cols_cut, shard, prev, name):
    t = x.shape[0]
    r, c = shard
    tm, tn, tk = _tile(r, 1024), _tile(c, 1024), _tile(t, 1024)
    nk = t // tk
    if cols_cut:
        assert x.shape[1] == r and dy.shape[1] == 4 * c
        per = c // tn
        omap = lambda i, j, q: (j // per, layer, i, j % per)
    else:
        assert x.shape[1] == 4 * r and dy.shape[1] == c
        per = r // tm
        omap = lambda i, j, q: (i // per, layer, i % per, j)

    def body(*refs):
        x_ref, dy_ref = refs[:2]
        o_ref, acc_ref = refs[-2:]
        kk = pl.program_id(2)

        @pl.when(kk == 0)
        def _():
            acc_ref[...] = jnp.zeros_like(acc_ref)

        acc_ref[...] += _dot16(x_ref[...], dy_ref[...], TN)

        @pl.when(kk == nk - 1)
        def _():
            o_ref[...] = acc_ref[...]

    return pl.pallas_call(
        body,
        grid=(x.shape[1] // tm, dy.shape[1] // tn, nk),
        in_specs=[pl.BlockSpec((tk, tm), lambda i, j, q: (q, i)), pl.BlockSpec((tk, tn), lambda i, j, q: (q, j))]
        + ([ANY] if prev is not None else []),
        out_specs=pl.BlockSpec((None, None, tm, tn), omap),
        out_shape=SDS((4, 2, r, c), F32),
        scratch_shapes=[pltpu.VMEM((tm, tn), F32)],
        input_output_aliases={2: 0} if prev is not None else {},
        compiler_params=_cparams(("parallel", "parallel", "arbitrary")),
        name=name,
    )(x, dy, *([prev] if prev is not None else []))


def _single(fn, consts, *, name):
    outs = jax.eval_shape(fn, *[SDS(c.shape, F32) for c in consts])
    n_in = len(consts)

    def body(*refs):
        res = fn(*[r[...] for r in refs[:n_in]])
        for o_ref, v in zip(refs[n_in:], res):
            o_ref[...] = v

    return pl.pallas_call(body, out_shape=[SDS(o.shape, F32) for o in outs],
                          compiler_params=pltpu.CompilerParams(vmem_limit_bytes=VMEM_LIMIT), name=name)(*consts)


def _single_vjp(fn, consts, cots, *, name):
    n_in = len(consts)

    def body(*refs):
        _, pull = jax.vjp(fn, *[r[...] for r in refs[:n_in]])
        grads = pull(tuple(r[...] for r in refs[n_in:n_in + len(cots)]))
        for o_ref, v in zip(refs[n_in + len(cots):], grads):
            o_ref[...] = v

    return pl.pallas_call(body, out_shape=[SDS(c.shape, F32) for c in consts],
                          compiler_params=pltpu.CompilerParams(vmem_limit_bytes=VMEM_LIMIT), name=name)(*consts, *cots)


def _full_spec(shape):
    nd = len(shape)
    return pl.BlockSpec(shape, lambda i, _n=nd: (0,) * _n)


def _stage_shapes(fn, rows, consts, tb, pos):
    rs = [SDS((tb, r.shape[1]), F32) for r in rows]
    cs = [SDS(c.shape, F32) for c in consts]
    f = (lambda *a: fn(jnp.int32(0), *a)) if pos else fn
    return jax.eval_shape(f, *rs, *cs)


def _stage(fn, rows, consts, *, tb, name, out_dtypes, n_acc=0, pos=False):
    t = rows[0].shape[0]
    assert t % tb == 0
    outs = _stage_shapes(fn, rows, consts, tb, pos)
    n_out = len(outs)
    n_row = n_out - n_acc
    n_in = len(rows) + len(consts)

    def body(*refs):
        i = pl.program_id(0)
        vals = [r[...].astype(F32) for r in refs[:n_in]]
        res = fn(i * tb, *vals) if pos else fn(*vals)
        out_refs = refs[n_in:]
        for q in range(n_row):
            out_refs[q][...] = res[q].astype(out_refs[q].dtype)
        for q in range(n_row, n_out):
            @pl.when(i == 0)
            def _(q=q):
                out_refs[q][...] = jnp.zeros_like(out_refs[q])

            out_refs[q][...] += res[q]

    in_specs = [pl.BlockSpec((tb, r.shape[1]), lambda i: (i, 0)) for r in rows] + [_full_spec(c.shape) for c in consts]
    out_specs = [pl.BlockSpec((tb, o.shape[1]), lambda i: (i, 0)) for o in outs[:n_row]] + [_full_spec(o.shape) for o in outs[n_row:]]
    out_shape = [SDS((t, o.shape[1]), dt) for o, dt in zip(outs[:n_row], out_dtypes)] + [SDS(o.shape, F32) for o in outs[n_row:]]
    return pl.pallas_call(
        body, grid=(t // tb,), in_specs=in_specs, out_specs=out_specs, out_shape=out_shape,
        compiler_params=_cparams(), name=name,
    )(*rows, *consts)


def _stage_vjp(fn, rows, consts, cots, *, tb, name, drow, dconst, drow_dtypes=None, acc_cots=(), pos=False):
    t = rows[0].shape[0]
    assert t % tb == 0
    n_rows, n_consts, n_cots, n_acc = len(rows), len(consts), len(cots), len(acc_cots)
    n_in = n_rows + n_consts + n_cots + n_acc
    drow_dtypes = drow_dtypes or [F32] * len(drow)

    def body(*refs):
        i = pl.program_id(0)
        vals = [r[...].astype(F32) for r in refs[:n_in]]
        rv, cv = vals[:n_rows], vals[n_rows:n_rows + n_consts]
        ct = tuple(vals[n_rows + n_consts:])

        def f(*dargs):
            r2, c2 = list(rv), list(cv)
            for q, idx in enumerate(drow):
                r2[idx] = dargs[q]
            for q, idx in enumerate(dconst):
                c2[idx] = dargs[len(drow) + q]
            return fn(i * tb, *r2, *c2) if pos else fn(*r2, *c2)

        _, pull = jax.vjp(f, *[rv[q] for q in drow], *[cv[q] for q in dconst])
        grads = pull(ct)
        out_refs = refs[n_in:]
        for q in range(len(drow)):
            out_refs[q][...] = grads[q].astype(out_refs[q].dtype)
        for q in range(len(drow), len(drow) + len(dconst)):
            @pl.when(i == 0)
            def _(q=q):
                out_refs[q][...] = jnp.zeros_like(out_refs[q])

            out_refs[q][...] += grads[q]

    in_specs = ([pl.BlockSpec((tb, r.shape[1]), lambda i: (i, 0)) for r in rows] + [_full_spec(c.shape) for c in consts]
                + [pl.BlockSpec((tb, c.shape[1]), lambda i: (i, 0)) for c in cots] + [_full_spec(c.shape) for c in acc_cots])
    out_specs = ([pl.BlockSpec((tb, rows[q].shape[1]), lambda i: (i, 0)) for q in drow]
                 + [_full_spec(consts[q].shape) for q in dconst])
    out_shape = ([SDS(rows[q].shape, dt) for q, dt in zip(drow, drow_dtypes)]
                 + [SDS(consts[q].shape, F32) for q in dconst])
    return pl.pallas_call(
        body, grid=(t // tb,), in_specs=in_specs, out_specs=out_specs, out_shape=out_shape,
        compiler_params=_cparams(), name=name,
    )(*rows, *consts, *cots, *acc_cots)


def _conv_fwd(x, w, b, *, tb, name):
    t, c = x.shape
    r8 = tb // 8

    def body(x_ref, p_ref, w_ref, b_ref, o_ref):
        i = pl.program_id(0)
        x_ = x_ref[...]
        p_ = jnp.where(i > 0, p_ref[...], 0.0)
        w_ = w_ref[...]
        row = lax.broadcasted_iota(jnp.int32, x_.shape, 0)
        row8 = lax.broadcasted_iota(jnp.int32, p_.shape, 0)
        acc = x_ * w_[3:4, :] + b_ref[...]
        head = jnp.zeros_like(p_)
        for j in (1, 2, 3):
            wj = w_[3 - j:4 - j, :]
            acc += jnp.where(row >= j, pltpu.roll(x_, j, 0), 0.0) * wj
            head += jnp.where(row8 < j, pltpu.roll(p_, j, 0), 0.0) * wj
        o_ref[...] = acc
        o_ref[0:8, :] += head

    return pl.pallas_call(
        body, grid=(t // tb,),
        in_specs=[pl.BlockSpec((tb, c), lambda i: (i, 0)),
                  pl.BlockSpec((8, c), lambda i: (jnp.maximum(i * r8 - 1, 0), 0)),
                  _full_spec(w.shape), _full_spec(b.shape)],
        out_specs=pl.BlockSpec((tb, c), lambda i: (i, 0)),
        out_shape=SDS((t, c), F32), compiler_params=_cparams(), name=name,
    )(x, x, w, b)


def _conv_bwd(x, w, dy, *, tb, name):
    t, c = x.shape
    r8 = tb // 8
    nb = t // tb

    def body(x_ref, p_ref, w_ref, g_ref, n_ref, dx_ref, dw_ref, db_ref):
        i = pl.program_id(0)
        x_ = x_ref[...]
        p_ = jnp.where(i > 0, p_ref[...], 0.0)
        g_ = g_ref[...]
        n_ = jnp.where(i < nb - 1, n_ref[...], 0.0)
        w_ = w_ref[...]
        row = lax.broadcasted_iota(jnp.int32, x_.shape, 0)
        row8 = lax.broadcasted_iota(jnp.int32, p_.shape, 0)
        g8 = g_[0:8, :]
        dx = g_ * w_[3:4, :]
        tail = jnp.zeros_like(n_)
        dws = [jnp.sum(g_ * x_, axis=0, keepdims=True)]
        for j in (1, 2, 3):
            wj = w_[3 - j:4 - j, :]
            dx += jnp.where(row < tb - j, pltpu.roll(g_, tb - j, 0), 0.0) * wj
            tail += jnp.where(row8 >= 8 - j, pltpu.roll(n_, 8 - j, 0), 0.0) * wj
            xs = jnp.where(row >= j, pltpu.roll(x_, j, 0), 0.0)
            ps = jnp.where(row8 < j, pltpu.roll(p_, j, 0), 0.0)
            dws.append(jnp.sum(g_ * xs, axis=0, keepdims=True) + jnp.sum(g8 * ps, axis=0, keepdims=True))
        dx_ref[...] = dx
        dx_ref[tb - 8:tb, :] += tail

        @pl.when(i == 0)
        def _():
            dw_ref[...] = jnp.zeros_like(dw_ref)
            db_ref[...] = jnp.zeros_like(db_ref)

        for j in range(4):
            dw_ref[3 - j:4 - j, :] += dws[j]
        db_ref[...] += jnp.sum(g_, axis=0, keepdims=True)

    return pl.pallas_call(
        body, grid=(nb,),
        in_specs=[pl.BlockSpec((tb, c), lambda i: (i, 0)),
                  pl.BlockSpec((8, c), lambda i: (jnp.maximum(i * r8 - 1, 0), 0)),
                  _full_spec(w.shape),
                  pl.BlockSpec((tb, c), lambda i: (i, 0)),
                  pl.BlockSpec((8, c), lambda i: (jnp.minimum((i + 1) * r8, t // 8 - 1), 0))],
        out_specs=[pl.BlockSpec((tb, c), lambda i: (i, 0)), _full_spec((8, c)), _full_spec((1, c))],
        out_shape=[SDS((t, c), F32), SDS((8, c), F32), SDS((1, c), F32)],
        compiler_params=_cparams(), name=name,
    )(x, x, w, dy, dy)


def _lru_scan_fwd(a, b, *, tb, name):
    t, c = a.shape

    def body(a_ref, b_ref, h_ref, st_ref):
        @pl.when(pl.program_id(0) == 0)
        def _():
            st_ref[...] = jnp.zeros_like(st_ref)

        def step(s, h):
            h = a_ref[pl.ds(s, 1), :] * h + b_ref[pl.ds(s, 1), :]
            h_ref[pl.ds(s, 1), :] = h
            return h

        st_ref[...] = lax.fori_loop(0, tb, step, st_ref[...], unroll=8)

    blk = pl.BlockSpec((tb, c), lambda i: (i, 0))
    return pl.pallas_call(
        body, grid=(t // tb,), in_specs=[blk, blk], out_specs=blk, out_shape=SDS((t, c), F32),
        scratch_shapes=[pltpu.VMEM((1, c), F32)], compiler_params=_cparams(), name=name,
    )(a, b)


def _lru_scan_bwd(a, h, dh, *, tb, name):
    t, c = a.shape
    nb = t // tb
    r8 = tb // 8

    def body(a_ref, h_ref, p_ref, g_ref, da_ref, db_ref, st_ref):
        i = pl.program_id(0)

        @pl.when(i == 0)
        def _():
            st_ref[...] = jnp.zeros_like(st_ref)

        hprev0 = jnp.where(i < nb - 1, p_ref[7:8, :], 0.0)

        def step(q, carry):
            s = tb - 1 - q
            g = g_ref[pl.ds(s, 1), :] + carry
            hp = h_ref[pl.ds(jnp.maximum(s - 1, 0), 1), :]
            hp = jnp.where(s > 0, hp, hprev0)
            db_ref[pl.ds(s, 1), :] = g
            da_ref[pl.ds(s, 1), :] = g * hp
            return a_ref[pl.ds(s, 1), :] * g

        st_ref[...] = lax.fori_loop(0, tb, step, st_ref[...], unroll=8)

    rev = pl.BlockSpec((tb, c), lambda i: (nb - 1 - i, 0))
    prev = pl.BlockSpec((8, c), lambda i: (jnp.maximum((nb - 1 - i) * r8 - 1, 0), 0))
    return pl.pallas_call(
        body, grid=(nb,), in_specs=[rev, rev, prev, rev], out_specs=[rev, rev],
        out_shape=[SDS((t, c), F32), SDS((t, c), F32)],
        scratch_shapes=[pltpu.VMEM((1, c), F32)], compiler_params=_cparams(), name=name,
    )(a, h, h, dh)


def _s5_scan_fwd(ar, ai, br, bi, *, tb, name):
    t, c = br.shape

    def body(ar_ref, ai_ref, br_ref, bi_ref, xr_ref, xi_ref, sr_ref, si_ref):
        @pl.when(pl.program_id(0) == 0)
        def _():
            sr_ref[...] = jnp.zeros_like(sr_ref)
            si_ref[...] = jnp.zeros_like(si_ref)

        ar_, ai_ = ar_ref[...], ai_ref[...]

        def step(s, carry):
            xr, xi = carry
            nr = ar_ * xr - ai_ * xi + br_ref[pl.ds(s, 1), :]
            ni = ar_ * xi + ai_ * xr + bi_ref[pl.ds(s, 1), :]
            xr_ref[pl.ds(s, 1), :] = nr
            xi_ref[pl.ds(s, 1), :] = ni
            return nr, ni

        xr, xi = lax.fori_loop(0, tb, step, (sr_ref[...], si_ref[...]), unroll=8)
        sr_ref[...] = xr
        si_ref[...] = xi

    blk = pl.BlockSpec((tb, c), lambda i: (i, 0))
    one = _full_spec((1, c))
    return pl.pallas_call(
        body, grid=(t // tb,), in_specs=[one, one, blk, blk], out_specs=[blk, blk],
        out_shape=[SDS((t, c), F32), SDS((t, c), F32)],
        scratch_shapes=[pltpu.VMEM((1, c), F32), pltpu.VMEM((1, c), F32)], compiler_params=_cparams(), name=name,
    )(ar, ai, br, bi)


def _s5_scan_bwd(ar, ai, xr, xi, dxr, dxi, *, tb, name):
    t, c = xr.shape
    nb = t // tb
    r8 = tb // 8

    def body(ar_ref, ai_ref, xr_ref, xi_ref, pr_ref, pi_ref, gr_ref, gi_ref,
             dbr_ref, dbi_ref, dar_ref, dai_ref, cr_ref, ci_ref):
        i = pl.program_id(0)

        @pl.when(i == 0)
        def _():
            cr_ref[...] = jnp.zeros_like(cr_ref)
            ci_ref[...] = jnp.zeros_like(ci_ref)
            dar_ref[...] = jnp.zeros_like(dar_ref)
            dai_ref[...] = jnp.zeros_like(dai_ref)

        ar_, ai_ = ar_ref[...], ai_ref[...]
        first = i == nb - 1
        pr0 = jnp.where(first, 0.0, pr_ref[7:8, :])
        pi0 = jnp.where(first, 0.0, pi_ref[7:8, :])

        def step(q, carry):
            cr, ci, dar, dai = carry
            s = tb - 1 - q
            gr = gr_ref[pl.ds(s, 1), :] + cr
            gi = gi_ref[pl.ds(s, 1), :] + ci
            sp = jnp.maximum(s - 1, 0)
            xpr = jnp.where(s > 0, xr_ref[pl.ds(sp, 1), :], pr0)
            xpi = jnp.where(s > 0, xi_ref[pl.ds(sp, 1), :], pi0)
            dbr_ref[pl.ds(s, 1), :] = gr
            dbi_ref[pl.ds(s, 1), :] = gi
            dar = dar + gr * xpr + gi * xpi
            dai = dai - gr * xpi + gi * xpr
            return ar_ * gr + ai_ * gi, ar_ * gi - ai_ * gr, dar, dai

        cr, ci, dar, dai = lax.fori_loop(0, tb, step, (cr_ref[...], ci_ref[...], dar_ref[...], dai_ref[...]), unroll=8)
        cr_ref[...] = cr
        ci_ref[...] = ci
        dar_ref[...] = dar
        dai_ref[...] = dai

    rev = pl.BlockSpec((tb, c), lambda i: (nb - 1 - i, 0))
    prev = pl.BlockSpec((8, c), lambda i: (jnp.maximum((nb - 1 - i) * r8 - 1, 0), 0))
    one = _full_spec((1, c))
    return pl.pallas_call(
        body, grid=(nb,), in_specs=[one, one, rev, rev, prev, prev, rev, rev], out_specs=[rev, rev, one, one],
        out_shape=[SDS((t, c), F32), SDS((t, c), F32), SDS((1, c), F32), SDS((1, c), F32)],
        scratch_shapes=[pltpu.VMEM((1, c), F32), pltpu.VMEM((1, c), F32)], compiler_params=_cparams(), name=name,
    )(ar, ai, xr, xi, xr, xi, dxr, dxi)


RW_PAIRS = RW_H // 2


def _pair_consts():
    sub = lax.broadcasted_iota(jnp.int32, (64, 128), 0)
    lane = lax.broadcasted_iota(jnp.int32, (64, 128), 1)
    eye2 = ((lane & 63) == sub).astype(F32)
    r2 = lax.broadcasted_iota(jnp.int32, (128, 128), 0)
    c2 = lax.broadcasted_iota(jnp.int32, (128, 128), 1)
    bsel = ((r2 >> 6) == (c2 >> 6)).astype(BF16)
    return eye2, bsel


def _segsum(x, bsel):
    rows = x.shape[0]
    bits = lax.bitcast_convert_type(x, jnp.int32)
    hi = lax.bitcast_convert_type(bits & jnp.int32(-65536), F32)
    both = jnp.concatenate([hi.astype(BF16), (x - hi).astype(BF16)], axis=0)
    res = jnp.dot(both, bsel, preferred_element_type=F32)
    return res[:rows] + res[rows:]


def _bc(x8):
    return jnp.stack([jnp.broadcast_to(x8[q:q + 1, :], (64, 128)) for q in range(RW_PAIRS)])


def _seg3(x3, bsel):
    return _segsum(x3.reshape(RW_PAIRS * 64, 128), bsel).reshape(RW_PAIRS, 64, 128)


def _rwkv_scan_fwd(r, w, k, v, kk, a, *, lc, name):
    t = r.shape[0]
    nc = t // lc

    def body(r_ref, w_ref, k_ref, v_ref, kk_ref, a_ref, y_ref, ck_ref, st_ref):
        @pl.when(pl.program_id(0) == 0)
        def _():
            st_ref[...] = jnp.zeros_like(st_ref)

        ck_ref[0] = st_ref[...]
        eye2, bsel = _pair_consts()
        column = lambda ref, s: _seg3(eye2[None] * _bc(ref[s]), bsel)
        read = lambda st, s: jnp.sum(eye2[None] * _seg3(st * _bc(r_ref[s]), bsel), axis=1)

        def step(s, carry):
            st, vb = carry
            kk8 = kk_ref[s]
            sa = -_seg3(st * _bc(kk8), bsel)
            vb_next = column(v_ref, jnp.minimum(s + 1, lc - 1))
            before = jnp.maximum(s - 1, 0)
            y_ref[before] = read(st, before)
            return st * _bc(w_ref[s]) + sa * _bc(kk8 * a_ref[s]) + vb * _bc(k_ref[s]), vb_next

        st, _ = lax.fori_loop(0, lc, step, (st_ref[...], column(v_ref, 0)))
        y_ref[lc - 1] = read(st, lc - 1)
        st_ref[...] = st

    blk = pl.BlockSpec((lc, RW_PAIRS, 128), lambda i: (i, 0, 0))
    return pl.pallas_call(
        body, grid=(nc,), in_specs=[blk] * 6,
        out_specs=[blk, pl.BlockSpec((1, RW_PAIRS, 64, 128), lambda i: (i, 0, 0, 0))],
        out_shape=[SDS((t, RW_PAIRS, 128), F32), SDS((nc, RW_PAIRS, 64, 128), F32)],
        scratch_shapes=[pltpu.VMEM((RW_PAIRS, 64, 128), F32)],
        compiler_params=_cparams(), name=name,
    )(r, w, k, v, kk, a)


def _rwkv_scan_bwd(r, w, k, v, kk, a, ck, dy, *, lc, name):
    t = r.shape[0]
    nc = t // lc

    def body(r_ref, w_ref, k_ref, v_ref, kk_ref, a_ref, ck_ref, dy_ref,
             dr_ref, dw_ref, dk_ref, dv_ref, dkk_ref, da_ref,
             ds_ref, vb_ref, dyb_ref, hist_ref, sa_ref):
        @pl.when(pl.program_id(0) == 0)
        def _():
            ds_ref[...] = jnp.zeros_like(ds_ref)

        eye2, bsel = _pair_consts()
        column = lambda ref, s: _seg3(eye2[None] * _bc(ref[s]), bsel)
        hist_ref[0] = ck_ref[0]

        def fwd(s, carry):
            st, vb = carry
            kk8 = kk_ref[s]
            sa = -_seg3(st * _bc(kk8), bsel)
            vb_next = column(v_ref, jnp.minimum(s + 1, lc - 1))
            dyb_ref[s] = column(dy_ref, s)
            vb_ref[s] = vb
            sa_ref[s] = sa
            st = st * _bc(w_ref[s]) + sa * _bc(kk8 * a_ref[s]) + vb * _bc(k_ref[s])
            hist_ref[s + 1] = st
            return st, vb_next

        lax.fori_loop(0, lc, fwd, (ck_ref[0], column(v_ref, 0)))

        def grads(s, d_s, dsa):
            s_prev, s_cur = hist_ref[s], hist_ref[s + 1]
            col = lambda z: jnp.sum(z, axis=1)
            db = col(d_s * sa_ref[s])
            dr_ref[s] = col(s_cur * dyb_ref[s])
            dw_ref[s] = col(d_s * s_prev)
            dv_ref[s] = col(eye2[None] * _seg3(d_s * _bc(k_ref[s]), bsel))
            dk_ref[s] = col(d_s * vb_ref[s])
            dkk_ref[s] = db * a_ref[s] - col(s_prev * dsa)
            da_ref[s] = db * kk_ref[s]

        def back(j, carry):
            ds, d_after, dsa_after = carry
            s = lc - 1 - j
            kk8 = kk_ref[s]
            d_s = ds + dyb_ref[s] * _bc(r_ref[s])
            dsa = _seg3(d_s * _bc(kk8 * a_ref[s]), bsel)
            grads(jnp.minimum(s + 1, lc - 1), d_after, dsa_after)
            return d_s * _bc(w_ref[s]) - dsa * _bc(kk8), d_s, dsa

        zero = jnp.zeros((RW_PAIRS, 64, 128), F32)
        ds, d_first, dsa_first = lax.fori_loop(0, lc, back, (ds_ref[...], zero, zero))
        grads(0, d_first, dsa_first)
        ds_ref[...] = ds

    rev = pl.BlockSpec((lc, RW_PAIRS, 128), lambda i: (nc - 1 - i, 0, 0))
    big = lambda n: pltpu.VMEM((n, RW_PAIRS, 64, 128), F32)
    return pl.pallas_call(
        body, grid=(nc,),
        in_specs=[rev] * 6 + [pl.BlockSpec((1, RW_PAIRS, 64, 128), lambda i: (nc - 1 - i, 0, 0, 0)), rev],
        out_specs=[rev] * 6, out_shape=[SDS((t, RW_PAIRS, 128), F32)] * 6,
        scratch_shapes=[pltpu.VMEM((RW_PAIRS, 64, 128), F32), big(lc), big(lc), big(lc + 1), big(lc)],
        compiler_params=_cparams(), name=name,
    )(r, w, k, v, kk, a, ck, dy)


SSD_PAIRS = SSD_H // 2


def _ssd_chunk(states, xdt, da, bm, cm):
    ln = SSD_L
    row = lax.broadcasted_iota(jnp.int32, (ln, ln), 0)
    col = lax.broadcasted_iota(jnp.int32, (ln, ln), 1)
    causal = row >= col
    acum = _dot32(causal.astype(F32), da)
    acum_t = _dot32(da, (row <= col).astype(F32), TN)
    sub = lax.broadcasted_iota(jnp.int32, (128, 128), 0)
    lane = lax.broadcasted_iota(jnp.int32, (128, 128), 1)
    ys, new_states = [], []
    for q in range(SSD_PAIRS):
        g = q // (SSD_PAIRS // SSD_NG)
        bg = bm[:, g * SSD_N:(g + 1) * SSD_N]
        cg = cm[:, g * SSD_N:(g + 1) * SSD_N]
        xq = xdt[:, q * 128:(q + 1) * 128]
        scores = _dot16(cg, bg, NT)
        aexp = _dot32(acum, (sub == 2 * q + (lane >> 6)).astype(F32))
        tot = aexp[ln - 1:ln, :]
        yh = []
        for h in (2 * q, 2 * q + 1):
            seg = _dot32(acum, (sub == h).astype(F32)) - acum_t[h:h + 1, :]
            yh.append(_dot16(scores * jnp.exp(jnp.where(causal, seg, -1e30)), xq))
        y = jnp.where(lane < 64, yh[0], yh[1]) + _dot16(cg, states[q]) * jnp.exp(aexp)
        new = _dot16(bg, xq * jnp.exp(tot - aexp), TN)
        ys.append(y)
        new_states.append(states[q] * jnp.exp(tot) + new)
    return jnp.concatenate(ys, axis=1), new_states


def _ssd_fwd(xdt, da, bm, cm, *, name):
    t = xdt.shape[0]
    nc = t // SSD_L

    def body(x_ref, a_ref, b_ref, c_ref, y_ref, ck_ref, st_ref):
        @pl.when(pl.program_id(0) == 0)
        def _():
            st_ref[...] = jnp.zeros_like(st_ref)

        ck_ref[0] = st_ref[...]
        y, new = _ssd_chunk([st_ref[q] for q in range(SSD_PAIRS)], x_ref[...], a_ref[...], b_ref[...], c_ref[...])
        y_ref[...] = y
        for q in range(SSD_PAIRS):
            st_ref[q] = new[q]

    blk = lambda wd: pl.BlockSpec((SSD_L, wd), lambda i: (i, 0))
    return pl.pallas_call(
        body, grid=(nc,), in_specs=[blk(SSD_W), blk(128), blk(512), blk(512)],
        out_specs=[blk(SSD_W), pl.BlockSpec((1, SSD_PAIRS, 128, 128), lambda i: (i, 0, 0, 0))],
        out_shape=[SDS((t, SSD_W), F32), SDS((nc, SSD_PAIRS, 128, 128), F32)],
        scratch_shapes=[pltpu.VMEM((SSD_PAIRS, 128, 128), F32)], compiler_params=_cparams(), name=name,
    )(xdt, da, bm, cm)


def _ssd_bwd(xdt, da, bm, cm, ck, dy, *, name):
    t = xdt.shape[0]
    nc = t // SSD_L

    def body(x_ref, a_ref, b_ref, c_ref, ck_ref, dy_ref, dx_ref, dda_ref, db_ref, dc_ref, ds_ref):
        @pl.when(pl.program_id(0) == 0)
        def _():
            ds_ref[...] = jnp.zeros_like(ds_ref)

        _, pull = jax.vjp(_ssd_chunk, [ck_ref[0, q] for q in range(SSD_PAIRS)], x_ref[...], a_ref[...], b_ref[...], c_ref[...])
        dst, dx, dda, db, dc = pull((dy_ref[...], [ds_ref[q] for q in range(SSD_PAIRS)]))
        dx_ref[...] = dx
        dda_ref[...] = dda
        db_ref[...] = db
        dc_ref[...] = dc
        for q in range(SSD_PAIRS):
            ds_ref[q] = dst[q]

    rev = lambda wd: pl.BlockSpec((SSD_L, wd), lambda i: (nc - 1 - i, 0))
    return pl.pallas_call(
        body, grid=(nc,),
        in_specs=[rev(SSD_W), rev(128), rev(512), rev(512),
                  pl.BlockSpec((1, SSD_PAIRS, 128, 128), lambda i: (nc - 1 - i, 0, 0, 0)), rev(SSD_W)],
        out_specs=[rev(SSD_W), rev(128), rev(512), rev(512)],
        out_shape=[SDS((t, SSD_W), F32), SDS((t, 128), F32), SDS((t, 512), F32), SDS((t, 512), F32)],
        scratch_shapes=[pltpu.VMEM((SSD_PAIRS, 128, 128), F32)], compiler_params=_cparams(), name=name,
    )(xdt, da, bm, cm, ck, dy)


def _iota(shape, dim):
    return lax.broadcasted_iota(jnp.int32, shape, dim)


def _rms(x, g):
    return x * lax.rsqrt(jnp.mean(x * x, axis=-1, keepdims=True) + EPS) * g


def _head_sel(width, shift):
    return ((_iota((width, 128), 0) >> shift) == _iota((width, 128), 1)).astype(F32)


def _head_sum(x, shift=6):
    sel = _head_sel(x.shape[1], shift)
    return _dot32(_dot32(x, sel), sel, NT)


def _head_expand(x, width, shift=6):
    return _dot32(x, _head_sel(width, shift), NT)


def f_norm(h, g):
    return (_rms(h, g),)


def f_norm_pass(h, g):
    return _rms(h, g), h


def f_add_norm(h, m, g):
    h1 = h + m
    return h1, _rms(h1, g)


def f_relu2(u):
    r = jnp.maximum(u, 0.0)
    return (r * r,)


def f_plgate(h2, gl, pp):
    return (h2 + jax.nn.sigmoid(gl) * pp,)


def f_loss(h, tgt, g):
    err = _rms(h, g) - tgt
    part = 0.5 * jnp.sum(jnp.mean(err * err, axis=-1, keepdims=True), axis=0, keepdims=True)
    return (jnp.broadcast_to(part, (8, 128)),)


def f_s5_prep(lam_re, lam_im, lstep, bre_t, bim_t, cre_t, cim_t):
    step = jnp.exp(_dot32(lstep, _head_sel(S5_N, 6), NT)[0:1, :])
    mag = jnp.exp(lam_re * step)
    abar_re, abar_im = mag * jnp.cos(lam_im * step), mag * jnp.sin(lam_im * step)
    den = lam_re * lam_re + lam_im * lam_im
    nr = abar_re - 1.0
    coef_re = (nr * lam_re + abar_im * lam_im) / den
    coef_im = (abar_im * lam_re - nr * lam_im) / den
    bbar_re = coef_re * bre_t - coef_im * bim_t
    bbar_im = coef_re * bim_t + coef_im * bre_t
    rep = ((_iota((S5_W, S5_G), 0) & (S5_G - 1)) == _iota((S5_W, S5_G), 1)).astype(F32)
    blk = ((_iota((S5_W, S5_N), 0) >> 4) == (_iota((S5_W, S5_N), 1) >> 6)).astype(F32)
    blk_t = ((_iota((S5_N, S5_W), 0) >> 6) == (_iota((S5_N, S5_W), 1) >> 4)).astype(F32)
    wb_re, wb_im = _dot32(rep, bbar_re) * blk, _dot32(rep, bbar_im) * blk
    wc_re, wc_im = _dot32(cre_t, rep, NT) * blk_t, _dot32(cim_t, rep, NT) * blk_t
    return abar_re, abar_im, wb_re, wb_im, wc_re, wc_im


def f_s5_post(xr, xi, u, wc_re, wc_im, d_skip, glu_w, glu_b):
    y = _dot16(xr, wc_re) - _dot16(xi, wc_im) + d_skip * u
    act = jax.nn.gelu(y)
    return (act * jax.nn.sigmoid(_dot16(act, glu_w) + glu_b),)


def f_ssd_pre(xc, dtr, dt_bias, a_log):
    act = jax.nn.silu(xc)
    heads = _iota(dtr.shape, 1) < SSD_H
    dt = jnp.where(heads, jax.nn.softplus(dtr + dt_bias), 0.0)
    da = dt * (-jnp.exp(a_log))
    xdt = act[:, :SSD_W] * _head_expand(dt, SSD_W)
    return xdt, da, act[:, SSD_W:SSD_W + 512], act[:, SSD_W + 512:]


def f_ssd_pre_pass(xc, dtr, dt_bias, a_log):
    return f_ssd_pre(xc, dtr, dt_bias, a_log) + (xc,)


def f_ssd_post(y, xc, z, d_skip, norm_g):
    xs = jax.nn.silu(xc[:, :SSD_W])
    y = (y + xs * _head_expand(d_skip, SSD_W)) * jax.nn.silu(z)
    gw = SSD_W // SSD_NG
    parts = []
    for g in range(SSD_NG):
        seg = y[:, g * gw:(g + 1) * gw]
        parts.append(seg * lax.rsqrt(jnp.mean(seg * seg, axis=-1, keepdims=True) + EPS))
    return (jnp.concatenate(parts, axis=1) * norm_g,)


def f_rwkv_pre(f, w0, w_up, a0, a_up, g_up, k_k, k_a):
    r, k, v = f[:, 0:1024], f[:, 1024:2048], f[:, 2048:3072]
    wl, al, gl = f[:, 3072:3200], f[:, 3200:3328], f[:, 3328:3584]
    w = -jax.nn.softplus(-(w0 + _dot16(jnp.tanh(wl), w_up))) - 0.5
    decay = jnp.exp(-jnp.exp(w))
    a = jax.nn.sigmoid(a0 + _dot16(al, a_up))
    g = _dot16(jax.nn.sigmoid(gl), g_up)
    kk = k * k_k
    k2 = k * (1.0 + (a - 1.0) * k_a)
    kkn = kk * lax.rsqrt(jnp.maximum(_head_sum(kk * kk), 1e-24))
    return r, decay, k2, v, kkn, a, g


def f_rwkv_pre_pass(f, w0, w_up, a0, a_up, g_up, k_k, k_a):
    out = f_rwkv_pre(f, w0, w_up, a0, a_up, g_up, k_k, k_a)
    return out + (out[0], out[2], out[3])


def f_rwkv_post(y, r, k2, v, g, ln_g, ln_b, r_k):
    mean = _head_sum(y) * (1.0 / RW_HD)
    yc = y - mean
    var = _head_sum(yc * yc) * (1.0 / RW_HD)
    yn = yc * lax.rsqrt(var + GN_EPS) * ln_g + ln_b
    bonus = _head_sum(r * k2 * r_k) * v
    return ((yn + bonus) * g,)


def _neg_expm1(y):
    series = -y * (1.0 + y * (0.5 + y * (1.0 / 6.0 + y * (1.0 / 24.0 + y * (1.0 / 120.0)))))
    return jnp.where(y > -0.1, series, 1.0 - jnp.exp(y))


def f_lru_pre(t0, xc, w_a, b_a, w_x, b_x, lam):
    gate_r = jax.nn.sigmoid(_dot16(xc, w_a) + b_a)
    gate_i = jax.nn.sigmoid(_dot16(xc, w_x) + b_x)
    log_a = -LRU_C * gate_r * jax.nn.softplus(-lam)
    mult = jnp.sqrt(jnp.maximum(_neg_expm1(2.0 * log_a), 0.0))
    mult = jnp.where(_iota(xc.shape, 0) + t0 == 0, 1.0, mult)
    return jnp.exp(log_a), xc * gate_i * mult


def f_lru_post(h, gl):
    return (h * jax.nn.gelu(gl),)


TB = 256
TBH = 128
SCAN_TB = 256
RW_LC = 32


def _even_fwd(hn, w, tag):
    n = lambda s: f"{tag}_{s}"
    u = _mm(hn, w["in_u"], name=n("proj_u"))
    z = _mm(hn, w["in_z"], name=n("proj_z"))
    xbc = _mm(hn, w["in_xbc"], name=n("proj_xbc"))
    dtr = _mm(hn, w["in_dt"], name=n("proj_dt"))
    bu_re = _mm(u, w["wb_re"], name=n("s5_bu_re"))
    bu_im = _mm(u, w["wb_im"], name=n("s5_bu_im"))
    xr, xi = _s5_scan_fwd(w["abar_re"], w["abar_im"], bu_re, bu_im, tb=SCAN_TB, name=n("s5_scan"))
    s5c = [w["wc_re"], w["wc_im"], w["s5_d"], w["glu_w"], w["glu_b"]]
    (ya,) = _stage(f_s5_post, [xr, xi, u], s5c, tb=TB, name=n("s5_post"), out_dtypes=[BF16])
    xc = _conv_fwd(xbc, w["ssd_conv_w"], w["ssd_conv_b"], tb=TB, name=n("ssd_conv"))
    xdt, da, bm, cm = _stage(f_ssd_pre, [xc, dtr], [w["dt_bias"], w["a_log"]], tb=TB, name=n("ssd_pre"),
                             out_dtypes=[F32] * 4)
    y, ck = _ssd_fwd(xdt, da, bm, cm, name=n("ssd_scan"))
    (yb,) = _stage(f_ssd_post, [y, xc, z], [w["ssd_d"], w["ssd_norm"]], tb=TB, name=n("ssd_post"), out_dtypes=[BF16])
    mo = _mm(ya, w["out_a"], name=n("out_a"))
    mo = _mm(yb, w["out_b"], add=mo, name=n("out_b"))
    res = dict(u=u, z=z, xbc=xbc, dtr=dtr, xr=xr, xi=xi, ya=ya, xc=xc, xdt=xdt, da=da, bm=bm, cm=cm, y=y, ck=ck, yb=yb)
    return mo, res


def _even_bwd(dmo, hn, w, r, tag):
    n = lambda s: f"{tag}_{s}"
    g = {}
    g["out_a"] = _mm(r["ya"], dmo, ta=True, name=n("d_out_a"))
    g["out_b"] = _mm(r["yb"], dmo, ta=True, name=n("d_out_b"))
    dya = _mm(dmo, w["out_a"], tb=True, name=n("dya"))
    dyb = _mm(dmo, w["out_b"], tb=True, name=n("dyb"))
    dy, dxc1, dz, g["ssd_d"], g["ssd_norm"] = _stage_vjp(
        f_ssd_post, [r["y"], r["xc"], r["z"]], [w["ssd_d"], w["ssd_norm"]], [dyb], tb=TBH, name=n("ssd_post_b"),
        drow=[0, 1, 2], dconst=[0, 1])
    dxdt, dda, dbm, dcm = _ssd_bwd(r["xdt"], r["da"], r["bm"], r["cm"], r["ck"], dy, name=n("ssd_scan_b"))
    dxc, ddtr, g["dt_bias"], g["a_log"] = _stage_vjp(
        f_ssd_pre_pass, [r["xc"], r["dtr"]], [w["dt_bias"], w["a_log"]], [dxdt, dda, dbm, dcm, dxc1], tb=TBH,
        name=n("ssd_pre_b"), drow=[0, 1], dconst=[0, 1])
    dxbc, g["ssd_conv_w"], g["ssd_conv_b"] = _conv_bwd(r["xbc"], w["ssd_conv_w"], dxc, tb=TB, name=n("ssd_conv_b"))
    s5c = [w["wc_re"], w["wc_im"], w["s5_d"], w["glu_w"], w["glu_b"]]
    dxr, dxi, du1, g["wc_re"], g["wc_im"], g["s5_d"], g["glu_w"], g["glu_b"] = _stage_vjp(
        f_s5_post, [r["xr"], r["xi"], r["u"]], s5c, [dya], tb=TBH, name=n("s5_post_b"),
        drow=[0, 1, 2], dconst=[0, 1, 2, 3, 4])
    dbr, dbi, g["abar_re"], g["abar_im"] = _s5_scan_bwd(w["abar_re"], w["abar_im"], r["xr"], r["xi"], dxr, dxi,
                                                         tb=SCAN_TB, name=n("s5_scan_b"))
    g["wb_re"] = _mm(r["u"], dbr, ta=True, name=n("d_wb_re"))
    g["wb_im"] = _mm(r["u"], dbi, ta=True, name=n("d_wb_im"))
    du = _mm(dbr, w["wb_re"], tb=True, add=du1, name=n("du_re"))
    du = _mm(dbi, w["wb_im"], tb=True, add=du, name=n("du_im"))
    segs = (("in_u", du), ("in_z", dz), ("in_xbc", dxbc), ("in_dt", ddtr))
    dhn = None
    for key, dseg in segs:
        g[key] = _mm(hn, dseg, ta=True, name=n("d_" + key))
        dhn = _mm(dseg, w[key], tb=True, add=dhn, name=n("dhn_" + key))
    return dhn, g


def _odd_fwd(hn, w, tag):
    n = lambda s: f"{tag}_{s}"
    rw = _mm(hn, w["in_rw"], name=n("proj_rw"))
    xl = _mm(hn, w["in_xl"], name=n("proj_xl"))
    gl = _mm(hn, w["in_gl"], name=n("proj_gl"))
    f = _conv_fwd(rw, w["mix_w"], w["mix_b"], tb=TB, name=n("rwkv_shift"))
    rc = [w[k] for k in ("w0", "w_up", "a0", "a_up", "g_up", "k_k", "k_a")]
    r_, dec, k2, v, kkn, a, gate = _stage(f_rwkv_pre, [f], rc, tb=TB, name=n("rwkv_pre"), out_dtypes=[F32] * 7)
    t3 = lambda z: z.reshape(-1, RW_PAIRS, 128)
    y, ck = _rwkv_scan_fwd(t3(r_), t3(dec), t3(k2), t3(v), t3(kkn), t3(a), lc=RW_LC, name=n("rwkv_scan"))
    y = y.reshape(-1, RW_W)
    (yc,) = _stage(f_rwkv_post, [y, r_, k2, v, gate], [w["ln_g"], w["ln_b"], w["r_k"]], tb=TB, name=n("rwkv_post"),
                   out_dtypes=[BF16])
    xc = _conv_fwd(xl, w["lru_conv_w"], w["lru_conv_b"], tb=TB, name=n("lru_conv"))
    lc = [w[k] for k in ("lru_wa", "lru_b_a", "lru_wx", "lru_b_x", "lru_lam")]
    a_l, bx = _stage(f_lru_pre, [xc], lc, tb=TB, name=n("lru_pre"), out_dtypes=[F32] * 2, pos=True)
    h = _lru_scan_fwd(a_l, bx, tb=SCAN_TB, name=n("lru_scan"))
    (yd,) = _stage(f_lru_post, [h, gl], [], tb=TB, name=n("lru_post"), out_dtypes=[BF16])
    mo = _mm(yc, w["out_a"], name=n("out_a"))
    mo = _mm(yd, w["out_b"], add=mo, name=n("out_b"))
    res = dict(rw=rw, xl=xl, gl=gl, f=f, r=r_, dec=dec, k2=k2, v=v, kkn=kkn, a=a, gate=gate, y=y, ck=ck, yc=yc,
               xc=xc, a_l=a_l, h=h, yd=yd)
    return mo, res


def _odd_bwd(dmo, hn, w, r, tag):
    n = lambda s: f"{tag}_{s}"
    g = {}
    g["out_a"] = _mm(r["yc"], dmo, ta=True, name=n("d_out_a"))
    g["out_b"] = _mm(r["yd"], dmo, ta=True, name=n("d_out_b"))
    dyc = _mm(dmo, w["out_a"], tb=True, name=n("dyc"))
    dyd = _mm(dmo, w["out_b"], tb=True, name=n("dyd"))
    dh, dgl = _stage_vjp(f_lru_post, [r["h"], r["gl"]], [], [dyd], tb=TB, name=n("lru_post_b"), drow=[0, 1], dconst=[])
    da_l, dbx = _lru_scan_bwd(r["a_l"], r["h"], dh, tb=SCAN_TB, name=n("lru_scan_b"))
    lc = [w[k] for k in ("lru_wa", "lru_b_a", "lru_wx", "lru_b_x", "lru_lam")]
    dxc, g["lru_wa"], g["lru_b_a"], g["lru_wx"], g["lru_b_x"], g["lru_lam"] = _stage_vjp(
        f_lru_pre, [r["xc"]], lc, [da_l, dbx], tb=TBH, name=n("lru_pre_b"), drow=[0], dconst=[0, 1, 2, 3, 4], pos=True)
    dxl, g["lru_conv_w"], g["lru_conv_b"] = _conv_bwd(r["xl"], w["lru_conv_w"], dxc, tb=TB, name=n("lru_conv_b"))
    dy, dr1, dk1, dv1, dgate, g["ln_g"], g["ln_b"], g["r_k"] = _stage_vjp(
        f_rwkv_post, [r["y"], r["r"], r["k2"], r["v"], r["gate"]], [w["ln_g"], w["ln_b"], w["r_k"]], [dyc], tb=TBH,
        name=n("rwkv_post_b"), drow=[0, 1, 2, 3, 4], dconst=[0, 1, 2])
    t3 = lambda z: z.reshape(-1, RW_PAIRS, 128)
    dr2, ddec, dk2, dv2, dkkn, da = [z.reshape(-1, RW_W) for z in _rwkv_scan_bwd(
        t3(r["r"]), t3(r["dec"]), t3(r["k2"]), t3(r["v"]), t3(r["kkn"]), t3(r["a"]), r["ck"], t3(dy),
        lc=RW_LC, name=n("rwkv_scan_b"))]
    rc = [w[k] for k in ("w0", "w_up", "a0", "a_up", "g_up", "k_k", "k_a")]
    df, g["w0"], g["w_up"], g["a0"], g["a_up"], g["g_up"], g["k_k"], g["k_a"] = _stage_vjp(
        f_rwkv_pre_pass, [r["f"]], rc, [dr2, ddec, dk2, dv2, dkkn, da, dgate, dr1, dk1, dv1], tb=TBH,
        name=n("rwkv_pre_b"), drow=[0], dconst=[0, 1, 2, 3, 4, 5, 6])
    drw, g["mix_w"], _ = _conv_bwd(r["rw"], w["mix_w"], df, tb=TB, name=n("rwkv_shift_b"))
    segs = (("in_rw", drw), ("in_xl", dxl), ("in_gl", dgl))
    dhn = None
    for key, dseg in segs:
        g[key] = _mm(hn, dseg, ta=True, name=n("d_" + key))
        dhn = _mm(dseg, w[key], tb=True, add=dhn, name=n("dhn_" + key))
    return dhn, g


def _layer_fwd(h, p_i, w, odd, tag):
    n = lambda s: f"{tag}_{s}"
    (hn,) = _stage(f_norm, [h], [w["norm_mix"]], tb=TB, name=n("norm_mix"), out_dtypes=[BF16])
    mo, mres = (_odd_fwd if odd else _even_fwd)(hn, w, tag)
    h1, hf = _stage(f_add_norm, [h, mo], [w["norm_ffn"]], tb=TB, name=n("norm_ffn"), out_dtypes=[F32, BF16])
    u, act = _mm(hf, w["mlp_w1"], name=n("mlp_up"), epilogue=lambda acc: (acc,) + f_relu2(acc), out_dtypes=[F32, BF16])
    m2 = _mm(act, w["mlp_w2"], name=n("mlp_down"))
    h2, hp = _stage(f_add_norm, [h1, m2], [w["norm_pl"]], tb=TB, name=n("norm_pl"), out_dtypes=[F32, BF16])
    gl = _mm(hp, w["pl_gate"], name=n("pl_gate"))
    pp = _mm(p_i, w["pl_proj"], name=n("pl_proj"))
    (h3,) = _stage(f_plgate, [h2, gl, pp], [], tb=TB, name=n("pl_mix"), out_dtypes=[F32])
    res = dict(h=h, hn=hn, mo=mo, mix=mres, h1=h1, hf=hf, u=u, act=act, m2=m2, h2=h2, hp=hp, gl=gl, pp=pp)
    return h3, res


def _layer_bwd(dh3, p_i, w, r, odd, tag, stacks):
    n = lambda s: f"{tag}_{s}"
    g = {}
    wgrad = lambda key, x, dy, cols_cut, shard: _mm_grad(x, dy, layer=int(odd), cols_cut=cols_cut, shard=shard,
                                                        prev=stacks[key] if stacks else None, name=n("d_" + key))
    dh2, dgl, dpp = _stage_vjp(f_plgate, [r["h2"], r["gl"], r["pp"]], [], [dh3], tb=TB, name=n("pl_mix_b"),
                               drow=[0, 1, 2], dconst=[])
    g["pl_proj"] = wgrad("pl_proj", p_i, dpp, True, (PL_DIM, D // 4))
    g["pl_gate"] = wgrad("pl_gate", r["hp"], dgl, False, (D // 4, D))
    dhp = _mm(dgl, w["pl_gate"], tb=True, name=n("dhp"))
    dh1, dm2, g["norm_pl"] = _stage_vjp(f_add_norm, [r["h1"], r["m2"]], [w["norm_pl"]], [dh2, dhp], tb=TB,
                                        name=n("norm_pl_b"), drow=[0, 1], dconst=[0])
    g["mlp_w2"] = wgrad("mlp_w2", r["act"], dm2, False, (D_FF // 4, D))
    (du,) = _mm(dm2, w["mlp_w2"], tb=True, name=n("dact"), extra=[r["u"]], out_dtypes=[BF16],
                epilogue=lambda acc, u: (acc * (2.0 * jnp.maximum(u, 0.0)),))
    g["mlp_w1"] = wgrad("mlp_w1", r["hf"], du, True, (D, D_FF // 4))
    dhf = _mm(du, w["mlp_w1"], tb=True, name=n("dhf"))
    dh, dmo, g["norm_ffn"] = _stage_vjp(f_add_norm, [r["h"], r["mo"]], [w["norm_ffn"]], [dh1, dhf], tb=TB,
                                        name=n("norm_ffn_b"), drow=[0, 1], dconst=[0])
    dhn, gm = (_odd_bwd if odd else _even_bwd)(dmo, r["hn"], w, r["mix"], tag)
    g.update(gm)
    dh0, g["norm_mix"] = _stage_vjp(f_norm_pass, [r["h"]], [w["norm_mix"]], [dhn, dh], tb=TB, name=n("norm_mix_b"),
                                    drow=[0], dconst=[0])
    return dh0, g


def _pad_to(a, size, axis):
    pad = [(0, 0)] * a.ndim
    pad[axis] = (0, size - a.shape[axis])
    return jnp.pad(a, pad)


def _rw_pad(a):
    return jnp.concatenate([a[..., :3072], _pad_to(a[..., 3072:3168], 128, -1), _pad_to(a[..., 3168:3264], 128, -1),
                            a[..., 3264:3520]], axis=-1)


def _rw_unpad(a):
    return jnp.concatenate([a[..., :3072], a[..., 3072:3168], a[..., 3200:3296], a[..., 3328:3584]], axis=-1)


def _block_diag(w):
    nb, bs, _ = w.shape
    eye = jnp.eye(nb, dtype=w.dtype)
    return (w[:, :, None, :] * eye[:, None, :, None]).reshape(nb * bs, nb * bs)


def _diag_blocks(w):
    nb = LRU_B
    bs = w.shape[0] // nb
    return jnp.stack([w[h * bs:(h + 1) * bs, h * bs:(h + 1) * bs] for h in range(nb)])


def _s5_prep_inputs(fw):
    lstep = jnp.broadcast_to(_pad_to(fw["s5_log_step"].astype(F32), 128, 1), (8, 128))
    t16 = lambda b: jnp.transpose(b[0], (2, 0, 1)).reshape(S5_G, S5_N)
    tc = lambda c: jnp.transpose(c[0], (0, 2, 1)).reshape(S5_N, S5_G)
    return [fw["s5_lam_re"].reshape(1, S5_N), fw["s5_lam_im"].reshape(1, S5_N), lstep,
            t16(fw["s5_b_re"]), t16(fw["s5_b_im"]), tc(fw["s5_c_re"]), tc(fw["s5_c_im"])]


def _layer_weights(fw, i):
    w = {k: fw[k][i:i + 1] for k in ("norm_mix", "norm_ffn", "norm_pl")}
    for k in ("mlp_w1", "mlp_w2", "pl_proj", "pl_gate"):
        w[k] = fw[k][i]
    return w


def _even_weights(fw, prep):
    w = _layer_weights(fw, 0)
    ein, eout = fw["e_in_proj"][0], fw["e_out_proj"][0]
    w.update(in_u=ein[:, :512], in_z=ein[:, 512:2048], in_xbc=ein[:, 2048:4608], in_dt=_pad_to(ein[:, 4608:], 128, 1),
             out_a=eout[:512], out_b=eout[512:])
    abar_re, abar_im, wb_re, wb_im, wc_re, wc_im = prep
    w.update(abar_re=abar_re, abar_im=abar_im, wb_re=wb_re, wb_im=wb_im, wc_re=wc_re.astype(BF16), wc_im=wc_im.astype(BF16),
             s5_d=fw["s5_d"], glu_w=fw["s5_glu_w"][0], glu_b=fw["s5_glu_b"],
             ssd_conv_w=_pad_to(fw["ssd_conv_w"][0], 8, 0), ssd_conv_b=fw["ssd_conv_b"],
             dt_bias=_pad_to(fw["ssd_dt_bias"], 128, 1), a_log=_pad_to(fw["ssd_a_log"], 128, 1),
             ssd_d=_pad_to(fw["ssd_d"], 128, 1), ssd_norm=fw["ssd_norm"])
    return w


def _odd_weights(fw):
    w = _layer_weights(fw, 1)
    oin, oout = fw["o_in_proj"][0], fw["o_out_proj"][0]
    mu = _rw_pad(fw["rwkv_mu"])
    zero = jnp.zeros_like(mu)
    w.update(in_rw=_rw_pad(oin[:, :RW_IN]), in_xl=oin[:, RW_IN:RW_IN + LRU_W], in_gl=oin[:, RW_IN + LRU_W:],
             out_a=oout[:RW_W], out_b=oout[RW_W:],
             mix_w=jnp.concatenate([zero, zero, mu, 1.0 - mu, zero, zero, zero, zero], axis=0), mix_b=zero,
             w0=fw["rwkv_w0"], w_up=_pad_to(fw["rwkv_w_up"][0], 128, 0), a0=fw["rwkv_a0"],
             a_up=_pad_to(fw["rwkv_a_up"][0], 128, 0), g_up=fw["rwkv_g_up"][0], k_k=fw["rwkv_k_k"], k_a=fw["rwkv_k_a"],
             r_k=fw["rwkv_r_k"].reshape(1, RW_W), ln_g=fw["rwkv_ln_g"], ln_b=fw["rwkv_ln_b"],
             lru_conv_w=_pad_to(fw["lru_conv_w"][0], 8, 0), lru_conv_b=fw["lru_conv_b"],
             lru_wa=_block_diag(fw["lru_w_a"][0]).astype(BF16), lru_b_a=fw["lru_b_a"].reshape(1, LRU_W),
             lru_wx=_block_diag(fw["lru_w_x"][0]).astype(BF16), lru_b_x=fw["lru_b_x"].reshape(1, LRU_W),
             lru_lam=fw["lru_lam"].reshape(1, LRU_W))
    return w


def _global_grads(g0, g1, s5_grads, d_norm_final):
    out = {k: jnp.concatenate([g0[k], g1[k]], axis=0) for k in ("norm_mix", "norm_ffn", "norm_pl")}
    for k in STACKED:
        out[k] = g0[k]
    out["e_in_proj"] = jnp.concatenate([g0["in_u"], g0["in_z"], g0["in_xbc"], g0["in_dt"][:, :SSD_H]], axis=1)[None]
    out["e_out_proj"] = jnp.concatenate([g0["out_a"], g0["out_b"]], axis=0)[None]
    d_lam_re, d_lam_im, d_lstep, d_bre, d_bim, d_cre, d_cim = s5_grads
    out["s5_lam_re"] = d_lam_re.reshape(1, S5_GROUPS, S5_P)
    out["s5_lam_im"] = d_lam_im.reshape(1, S5_GROUPS, S5_P)
    out["s5_log_step"] = d_lstep[0:1, :S5_GROUPS]
    unb = lambda b: jnp.transpose(b.reshape(S5_G, S5_GROUPS, S5_P), (1, 2, 0))[None]
    unc = lambda c: jnp.transpose(c.reshape(S5_GROUPS, S5_P, S5_G), (0, 2, 1))[None]
    out.update(s5_b_re=unb(d_bre), s5_b_im=unb(d_bim), s5_c_re=unc(d_cre), s5_c_im=unc(d_cim),
               s5_d=g0["s5_d"], s5_glu_w=g0["glu_w"][None], s5_glu_b=g0["glu_b"],
               ssd_conv_w=g0["ssd_conv_w"][None, :4], ssd_conv_b=g0["ssd_conv_b"], ssd_dt_bias=g0["dt_bias"][:, :SSD_H],
               ssd_a_log=g0["a_log"][:, :SSD_H], ssd_d=g0["ssd_d"][:, :SSD_H], ssd_norm=g0["ssd_norm"])
    out["o_in_proj"] = jnp.concatenate([_rw_unpad(g1["in_rw"]), g1["in_xl"], g1["in_gl"]], axis=1)[None]
    out["o_out_proj"] = jnp.concatenate([g1["out_a"], g1["out_b"]], axis=0)[None]
    out.update(rwkv_mu=_rw_unpad(g1["mix_w"][2:3] - g1["mix_w"][3:4]), rwkv_w0=g1["w0"], rwkv_w_up=g1["w_up"][None, :RW_LORA],
               rwkv_a0=g1["a0"], rwkv_a_up=g1["a_up"][None, :RW_LORA], rwkv_g_up=g1["g_up"][None], rwkv_k_k=g1["k_k"],
               rwkv_k_a=g1["k_a"], rwkv_r_k=g1["r_k"].reshape(1, RW_H, RW_HD), rwkv_ln_g=g1["ln_g"], rwkv_ln_b=g1["ln_b"],
               lru_conv_w=g1["lru_conv_w"][None, :4], lru_conv_b=g1["lru_conv_b"],
               lru_w_a=_diag_blocks(g1["lru_wa"])[None], lru_b_a=g1["lru_b_a"].reshape(1, LRU_B, 64),
               lru_w_x=_diag_blocks(g1["lru_wx"])[None], lru_b_x=g1["lru_b_x"].reshape(1, LRU_B, 64),
               lru_lam=g1["lru_lam"].reshape(1, LRU_B, 64), norm_final=d_norm_final.reshape(D))
    return out


def _local_step(x, p, target, fw):
    prep_in = _s5_prep_inputs(fw)
    prep = _single(f_s5_prep, prep_in, name="s5_prep")
    w0, w1 = _even_weights(fw, prep), _odd_weights(fw)
    h1, r0 = _layer_fwd(x, p[0], w0, False, "l0")
    h2, r1 = _layer_fwd(h1, p[1], w1, True, "l1")
    gf = fw["norm_final"].reshape(1, D)
    (loss8,) = _stage(f_loss, [h2, target], [gf], tb=TB, name="loss", out_dtypes=[], n_acc=1)
    one = jnp.zeros((8, 128), F32).at[0, 0].set(1.0)
    dh2, d_gf = _stage_vjp(f_loss, [h2, target], [gf], [], tb=TB, name="loss_b", drow=[0], dconst=[0], acc_cots=[one])
    dh1, g1 = _layer_bwd(dh2, p[1], w1, r1, True, "l1", None)
    dx, g0 = _layer_bwd(dh1, p[0], w0, r0, False, "l0", g1)
    cots = [g0[k] for k in ("abar_re", "abar_im", "wb_re", "wb_im", "wc_re", "wc_im")]
    s5_grads = _single_vjp(f_s5_prep, prep_in, cots, name="s5_prep_b")
    return loss8[0, 0], dx, _global_grads(g0, g1, s5_grads, d_gf)


def _xyc():
    return lax.axis_index("x"), lax.axis_index("y"), lax.axis_index("c")


def _flip(v, bit):
    return 1 - v if bit else v


def _remote(src, dst, send_sems, recv_sems, k, dev):
    return pltpu.make_async_remote_copy(src_ref=src, dst_ref=dst, send_sem=send_sems.at[k], recv_sem=recv_sems.at[k],
                                        device_id=dev, device_id_type=MESH)


def _dma_scratch(n_remote, n_local):
    return [pltpu.SemaphoreType.DMA((n_remote,)), pltpu.SemaphoreType.DMA((n_remote,)), pltpu.SemaphoreType.DMA((n_local,))]


CHIP_FLIPS = ((1, 0), (0, 1), (1, 1))


def _gather_chips(arrs, out_shapes, places, *, name):
    n = len(arrs)

    def body(*refs):
        ins, outs = refs[:n], refs[n:2 * n]
        send_sems, recv_sems = refs[2 * n:]
        x, y, c = _xyc()
        chip, sib = 2 * x + y, (x, y, 1 - c)
        peers = [(_flip(x, fx), _flip(y, fy)) for fx, fy in CHIP_FLIPS]
        first = [_remote(ins[a].at[c], places[a](outs[a], chip, c), send_sems, recv_sems, 6 * a + j, (px, py, c))
                 for a in range(n) for j, (px, py) in enumerate(peers)]
        for cp in first:
            cp.start()
        passed = []
        for a in range(n):
            for j, (px, py) in enumerate(peers):
                landed = places[a](outs[a], 2 * px + py, c)
                _remote(ins[a].at[c], landed, send_sems, recv_sems, 6 * a + j, (px, py, c)).wait_recv()
                cp = _remote(landed, landed, send_sems, recv_sems, 6 * a + 3 + j, sib)
                cp.start()
                passed.append(cp)
        for a in range(n):
            for j, (px, py) in enumerate(peers):
                other = places[a](outs[a], 2 * px + py, 1 - c)
                _remote(other, other, send_sems, recv_sems, 6 * a + 3 + j, sib).wait_recv()
        for cp in first + passed:
            cp.wait_send()

    return pl.pallas_call(
        body, out_shape=[SDS(s, a.dtype) for s, a in zip(out_shapes, arrs)], in_specs=[ANY] * n, out_specs=[ANY] * n,
        scratch_shapes=_dma_sems(6 * n), name=name,
    )(*arrs)


def _gather_all(arr, *, name):
    def body(in_ref, out_ref, send_sems, recv_sems, loc_sems):
        x, y, c = _xyc()
        mine = out_ref.at[4 * x + 2 * y + c]
        lc = pltpu.make_async_copy(in_ref, mine, loc_sems.at[0])
        lc.start()
        sends = []
        for k in range(1, 8):
            dev = (_flip(x, k >> 2 & 1), _flip(y, k >> 1 & 1), _flip(c, k & 1))
            cp = _remote(in_ref, mine, send_sems, recv_sems, k - 1, dev)
            cp.start()
            sends.append(cp)
        for k in range(1, 8):
            px, py, pc = _flip(x, k >> 2 & 1), _flip(y, k >> 1 & 1), _flip(c, k & 1)
            _remote(in_ref, out_ref.at[4 * px + 2 * py + pc], send_sems, recv_sems, k - 1, (px, py, pc)).wait_recv()
        for cp in sends:
            cp.wait_send()
        lc.wait()

    return pl.pallas_call(
        body, out_shape=SDS((8,) + arr.shape, arr.dtype), in_specs=[ANY], out_specs=ANY,
        scratch_shapes=_dma_scratch(7, 1), name=name,
    )(arr)


def _dma_sems(n):
    return [pltpu.SemaphoreType.DMA((n,)), pltpu.SemaphoreType.DMA((n,))]


def _send_halves(arrs, *, name):
    n = len(arrs)

    def body(*refs):
        ins, outs = refs[:n], refs[n:2 * n]
        send_sems, recv_sems = refs[2 * n:]
        x, y, c = _xyc()
        copies = [_remote(ins[a].at[k, 1 - c], outs[a].at[k], send_sems, recv_sems, 4 * a + k, (x, y, 1 - c))
                  for a in range(n) for k in range(4)]
        for cp in copies:
            cp.start()
        for cp in copies:
            cp.wait_recv()
        for cp in copies:
            cp.wait_send()

    return pl.pallas_call(
        body, out_shape=[SDS((4,) + a.shape[2:], a.dtype) for a in arrs], in_specs=[ANY] * n, out_specs=[ANY] * n,
        scratch_shapes=_dma_sems(4 * n), name=name,
    )(*arrs)


def _add_half(g, recv, c_vec, *, tb, out_dtype, name):
    _, _, rh, cols = g.shape
    tb = min(tb, rh)

    def body(c_ref, g_ref, r_ref, o_ref):
        o_ref[...] = (g_ref[...] + r_ref[...]).astype(o_ref.dtype)

    return pl.pallas_call(
        body,
        grid_spec=pltpu.PrefetchScalarGridSpec(
            num_scalar_prefetch=1, grid=(4, rh // tb),
            in_specs=[pl.BlockSpec((None, None, tb, cols), lambda k, i, c_ref: (k, c_ref[0], i, 0)),
                      pl.BlockSpec((None, tb, cols), lambda k, i, c_ref: (k, i, 0))],
            out_specs=pl.BlockSpec((None, tb, cols), lambda k, i, c_ref: (k, i, 0))),
        out_shape=SDS((4, rh, cols), out_dtype), compiler_params=_cparams(("arbitrary", "arbitrary")), name=name,
    )(c_vec, g, recv)


def _scatter_chips(arrs, *, name):
    n = len(arrs)

    def body(*refs):
        ins, outs = refs[:n], refs[n:2 * n]
        send_sems, recv_sems = refs[2 * n:]
        x, y, c = _xyc()
        copies = []
        for a in range(n):
            for j, (fx, fy) in enumerate(CHIP_FLIPS):
                px, py = _flip(x, fx), _flip(y, fy)
                copies.append(_remote(ins[a].at[2 * px + py], outs[a].at[j], send_sems, recv_sems, 3 * a + j, (px, py, c)))
        for cp in copies:
            cp.start()
        for cp in copies:
            cp.wait_recv()
        for cp in copies:
            cp.wait_send()

    return pl.pallas_call(
        body, out_shape=[SDS((3,) + a.shape[1:], a.dtype) for a in arrs], in_specs=[ANY] * n, out_specs=[ANY] * n,
        scratch_shapes=_dma_sems(3 * n), name=name,
    )(*arrs)


def _sum_chips(p, landed, chip_vec, *, tb, name):
    _, rh, cols = p.shape
    tb = min(tb, rh)

    def body(chip_ref, p_ref, l_ref, o_ref):
        f = lambda z: z.astype(F32)
        o_ref[...] = ((f(p_ref[...]) + f(l_ref[0])) + f(l_ref[1])) + f(l_ref[2])

    return pl.pallas_call(
        body,
        grid_spec=pltpu.PrefetchScalarGridSpec(
            num_scalar_prefetch=1, grid=(rh // tb,),
            in_specs=[pl.BlockSpec((None, tb, cols), lambda i, chip_ref: (chip_ref[0], i, 0)),
                      pl.BlockSpec((3, tb, cols), lambda i, chip_ref: (0, i, 0))],
            out_specs=pl.BlockSpec((tb, cols), lambda i, chip_ref: (i, 0))),
        out_shape=SDS((rh, cols), F32), compiler_params=_cparams(), name=name,
    )(chip_vec, p, landed)


def _swap_halves(arrs, *, name):
    n = len(arrs)

    def body(*refs):
        ins, outs = refs[:n], refs[n:2 * n]
        send_sems, recv_sems = refs[2 * n:]
        x, y, c = _xyc()
        copies = [_remote(ins[a], outs[a], send_sems, recv_sems, a, (x, y, 1 - c)) for a in range(n)]
        for cp in copies:
            cp.start()
        for cp in copies:
            cp.wait_recv()
        for cp in copies:
            cp.wait_send()

    return pl.pallas_call(
        body, out_shape=[SDS(a.shape, a.dtype) for a in arrs], in_specs=[ANY] * n, out_specs=[ANY] * n,
        scratch_shapes=_dma_sems(n), name=name,
    )(*arrs)


def _join_halves(mine, theirs, c_vec, *, tb, name):
    rh, cols = mine.shape
    tb = min(tb, rh)

    def body(c_ref, m_ref, t_ref, o_ref):
        o_ref[...] = jnp.where(pl.program_id(0) == c_ref[0], m_ref[...], t_ref[...])

    blk = pl.BlockSpec((tb, cols), lambda h, i, c_ref: (i, 0))
    return pl.pallas_call(
        body,
        grid_spec=pltpu.PrefetchScalarGridSpec(
            num_scalar_prefetch=1, grid=(2, rh // tb), in_specs=[blk, blk],
            out_specs=pl.BlockSpec((None, tb, cols), lambda h, i, c_ref: (h, i, 0))),
        out_shape=SDS((2, rh, cols), mine.dtype), compiler_params=_cparams(("arbitrary", "arbitrary")), name=name,
    )(c_vec, mine, theirs)


def _sum_lead(x, *, tb, name):
    k, r, c = x.shape
    tb = min(tb, r)
    assert r % tb == 0

    def body(x_ref, o_ref):
        acc = x_ref[0]
        for q in range(1, k):
            acc = acc + x_ref[q]
        o_ref[...] = acc

    return pl.pallas_call(
        body, grid=(r // tb,), in_specs=[pl.BlockSpec((k, tb, c), lambda i: (0, i, 0))],
        out_specs=pl.BlockSpec((tb, c), lambda i: (i, 0)), out_shape=SDS((r, c), x.dtype),
        compiler_params=_cparams(), name=name,
    )(x)


def f_adamw(w, g, m, v):
    m = ADAM_B1 * m + (1.0 - ADAM_B1) * g
    v = ADAM_B2 * v + (1.0 - ADAM_B2) * (g * g)
    m_hat = m / (1.0 - ADAM_B1 ** ADAM_STEP)
    v_hat = v / (1.0 - ADAM_B2 ** ADAM_STEP)
    return -ADAM_LR * (m_hat / (jnp.sqrt(v_hat) + ADAM_EPS) + ADAM_WD * w), m, v


def _adamw(w, g, m, v, *, name):
    shape = w.shape
    two = lambda a: a.reshape(-1, shape[-1])
    rows = two(w).shape[0]
    tb = 256 if rows % 256 == 0 else rows
    outs = _stage(f_adamw, [two(w), two(g), two(m), two(v)], [], tb=tb, name=name, out_dtypes=[F32] * 3)
    return [o.reshape(shape) for o in outs]


def _pack(arrs, rows=8):
    flat = jnp.concatenate([a.astype(F32).reshape(-1) for a in arrs])
    size = -(-flat.shape[0] // (rows * 128)) * (rows * 128)
    return _pad_to(flat, size, 0).reshape(-1, 128)


def _unpack(buf, shapes):
    flat = buf.reshape(-1)
    out, off = [], 0
    for s in shapes:
        n = math.prod(s)
        out.append(flat[off:off + n].reshape(s))
        off += n
    return out


WEIGHTS = ("norm_mix", "norm_ffn", "norm_pl", "mlp_w1", "mlp_w2", "pl_proj", "pl_gate", "e_in_proj", "e_out_proj",
           "s5_lam_re", "s5_lam_im", "s5_log_step", "s5_b_re", "s5_b_im", "s5_c_re", "s5_c_im", "s5_d", "s5_glu_w",
           "s5_glu_b", "ssd_conv_w", "ssd_conv_b", "ssd_dt_bias", "ssd_a_log", "ssd_d", "ssd_norm", "o_in_proj",
           "o_out_proj", "rwkv_mu", "rwkv_w0", "rwkv_w_up", "rwkv_a0", "rwkv_a_up", "rwkv_g_up", "rwkv_k_k", "rwkv_k_a",
           "rwkv_r_k", "rwkv_ln_g", "rwkv_ln_b", "lru_conv_w", "lru_conv_b", "lru_w_a", "lru_b_a", "lru_w_x", "lru_b_x",
           "lru_lam", "norm_final")
BIG = ("mlp_w1", "mlp_w2", "pl_proj", "pl_gate", "e_in_proj", "e_out_proj", "o_in_proj", "o_out_proj")
STACKED = BIG[:4]
SHARD_AXIS = {"mlp_w1": 2, "mlp_w2": 1, "pl_proj": 2, "pl_gate": 1, "e_in_proj": 2, "e_out_proj": 1, "s5_glu_w": 1,
              "ssd_conv_w": 2, "o_in_proj": 2, "o_out_proj": 1, "rwkv_mu": 1, "rwkv_w0": 1, "rwkv_w_up": 2, "rwkv_a0": 1,
              "rwkv_a_up": 2, "rwkv_g_up": 2, "rwkv_k_k": 1, "rwkv_k_a": 1, "rwkv_ln_g": 1, "rwkv_ln_b": 1,
              "lru_conv_w": 2, "lru_conv_b": 1}
SMALL = tuple(n for n in WEIGHTS if n not in BIG)
SMALL_SHARDED = tuple(n for n in SMALL if n in SHARD_AXIS)


def _gather_weights(w):
    shapes = [w[n].shape for n in SMALL_SHARDED]
    chip = 2 * lax.axis_index("x") + lax.axis_index("y")
    mine = [w[n].astype(BF16) for n in BIG] + [_pack([w[n] for n in SMALL_SHARDED], rows=16)]
    out_shapes, places = [], []
    for n, a in zip(BIG + ("small",), mine):
        layers, rows, cols = a.shape if a.ndim == 3 else (1,) + a.shape
        ax = SHARD_AXIS.get(n)
        if ax == 1:
            step = rows if layers == 2 else rows // 2
            out_shapes.append((layers, 4 * rows, cols))
            places.append(lambda o, k, h, layers=layers, rows=rows, step=step: o.at[
                h if layers == 2 else 0, pl.ds(pl.multiple_of(k * rows + (0 if layers == 2 else h * step), 16), step), :])
        elif ax == 2 and layers == 2:
            out_shapes.append((layers, rows, 4 * cols))
            places.append(lambda o, k, h, cols=cols: o.at[h, :, pl.ds(pl.multiple_of(k * cols, 128), cols)])
        else:
            out_shapes.append((4, 2, layers * rows // 2, cols))
            places.append(lambda o, k, h: o.at[k, h])
    got = _gather_chips([a.reshape(2, -1, a.shape[-1]) for a in mine], out_shapes, places, name="gather_weights")
    fw = {n: w[n] for n in SMALL if n not in SHARD_AXIS}
    for n, g, a in zip(BIG, got[:-1], mine):
        if g.shape[0] == 4:
            g = lax.dynamic_update_index_in_dim(g.reshape((4,) + a.shape), a, chip, 0)
            fw[n] = jnp.concatenate([g[k] for k in range(4)], axis=SHARD_AXIS[n])
        else:
            ax = SHARD_AXIS[n]
            fw[n] = lax.dynamic_update_slice_in_dim(g, a, chip * a.shape[ax], axis=ax)
    small = lax.dynamic_update_index_in_dim(got[-1].reshape((4,) + mine[-1].shape), mine[-1], chip, 0)
    parts = [_unpack(small[k], shapes) for k in range(4)]
    for i, n in enumerate(SMALL_SHARDED):
        fw[n] = jnp.concatenate([parts[k][i] for k in range(4)], axis=SHARD_AXIS[n])
    return fw


def _reduce_big(grads, w):
    stacks = []
    for n in BIG:
        cols = w[n].shape[-1]
        stacks.append(grads[n] if n in STACKED else
                      jnp.stack(jnp.split(grads[n], 4, axis=SHARD_AXIS[n])).reshape(4, 2, -1, cols))
    c_vec = lax.axis_index("c").astype(jnp.int32).reshape(1)
    chip_vec = (2 * lax.axis_index("x") + lax.axis_index("y")).astype(jnp.int32).reshape(1)
    got = _send_halves(stacks, name="reduce_pair")
    sums = [_add_half(s, r, c_vec, tb=512, out_dtype=BF16, name=f"reduce_pair_sum_{n}")
            for n, s, r in zip(BIG, stacks, got)]
    landed = _scatter_chips(sums, name="reduce_chips")
    halves = [_sum_chips(p, l, chip_vec, tb=256, name=f"reduce_chips_sum_{n}") for n, p, l in zip(BIG, sums, landed)]
    theirs = _swap_halves(halves, name="reduce_swap")
    return {n: _join_halves(h, t, c_vec, tb=512, name=f"reduce_join_{n}").reshape(w[n].shape)
            for n, h, t in zip(BIG, halves, theirs)}


def _reduce_small(grads, w, chip, loss):
    shapes = [grads[n].shape for n in SMALL] + [(1,)]
    packed = _pack([grads[n] for n in SMALL] + [loss.reshape(1)])
    total = _sum_lead(_gather_all(packed, name="reduce_small"), tb=packed.shape[0], name="reduce_small_sum")
    *parts, loss_sum = _unpack(total, shapes)
    out = {}
    for n, g in zip(SMALL, parts):
        if n in SHARD_AXIS:
            ax = SHARD_AXIS[n]
            size = w[n].shape[ax]
            g = lax.dynamic_slice_in_dim(g, chip * size, size, axis=ax)
        out[n] = g
    return out, loss_sum[0]


def kernel(x, p, norm_mix, norm_ffn, norm_pl, mlp_w1, mlp_w2, pl_proj, pl_gate, e_in_proj, e_out_proj, s5_lam_re, s5_lam_im, s5_log_step, s5_b_re, s5_b_im, s5_c_re, s5_c_im, s5_d, s5_glu_w, s5_glu_b, ssd_conv_w, ssd_conv_b, ssd_dt_bias, ssd_a_log, ssd_d, ssd_norm, o_in_proj, o_out_proj, rwkv_mu, rwkv_w0, rwkv_w_up, rwkv_a0, rwkv_a_up, rwkv_g_up, rwkv_k_k, rwkv_k_a, rwkv_r_k, rwkv_ln_g, rwkv_ln_b, lru_conv_w, lru_conv_b, lru_w_a, lru_b_a, lru_w_x, lru_b_x, lru_lam, norm_final, loss_target, m_norm_mix, m_norm_ffn, m_norm_pl, m_mlp_w1, m_mlp_w2, m_pl_proj, m_pl_gate, m_e_in_proj, m_e_out_proj, m_s5_lam_re, m_s5_lam_im, m_s5_log_step, m_s5_b_re, m_s5_b_im, m_s5_c_re, m_s5_c_im, m_s5_d, m_s5_glu_w, m_s5_glu_b, m_ssd_conv_w, m_ssd_conv_b, m_ssd_dt_bias, m_ssd_a_log, m_ssd_d, m_ssd_norm, m_o_in_proj, m_o_out_proj, m_rwkv_mu, m_rwkv_w0, m_rwkv_w_up, m_rwkv_a0, m_rwkv_a_up, m_rwkv_g_up, m_rwkv_k_k, m_rwkv_k_a, m_rwkv_r_k, m_rwkv_ln_g, m_rwkv_ln_b, m_lru_conv_w, m_lru_conv_b, m_lru_w_a, m_lru_b_a, m_lru_w_x, m_lru_b_x, m_lru_lam, m_norm_final, v_norm_mix, v_norm_ffn, v_norm_pl, v_mlp_w1, v_mlp_w2, v_pl_proj, v_pl_gate, v_e_in_proj, v_e_out_proj, v_s5_lam_re, v_s5_lam_im, v_s5_log_step, v_s5_b_re, v_s5_b_im, v_s5_c_re, v_s5_c_im, v_s5_d, v_s5_glu_w, v_s5_glu_b, v_ssd_conv_w, v_ssd_conv_b, v_ssd_dt_bias, v_ssd_a_log, v_ssd_d, v_ssd_norm, v_o_in_proj, v_o_out_proj, v_rwkv_mu, v_rwkv_w0, v_rwkv_w_up, v_rwkv_a0, v_rwkv_a_up, v_rwkv_g_up, v_rwkv_k_k, v_rwkv_k_a, v_rwkv_r_k, v_rwkv_ln_g, v_rwkv_ln_b, v_lru_conv_w, v_lru_conv_b, v_lru_w_a, v_lru_b_a, v_lru_w_x, v_lru_b_x, v_lru_lam, v_norm_final):
    given = dict(locals())
    w = {n: given[n] for n in WEIGHTS}
    m = {n: given["m_" + n] for n in WEIGHTS}
    v = {n: given["v_" + n] for n in WEIGHTS}
    chip = 2 * lax.axis_index("x") + lax.axis_index("y")

    fw = _gather_weights(w)
    loss, dx, grads = _local_step(x[0], p[:, 0], loss_target[0], fw)
    g = _reduce_big(grads, w)
    g_small, loss = _reduce_small(grads, w, chip, loss)
    g.update(g_small)

    delta, new_m, new_v = {}, {}, {}
    for n in BIG:
        delta[n], new_m[n], new_v[n] = _adamw(w[n], g[n], m[n], v[n], name=f"adamw_{n}")
    shapes = [w[n].shape for n in SMALL]
    packed = [_pack([d[n] for n in SMALL]) for d in (w, g, m, v)]
    for d, buf in zip((delta, new_m, new_v), _adamw(*packed, name="adamw_small")):
        d.update(zip(SMALL, _unpack(buf, shapes)))
    return (loss, dx[None], *[g[n] for n in WEIGHTS], *[delta[n] for n in WEIGHTS],
            *[new_m[n] for n in WEIGHTS], *[new_v[n] for n in WEIGHTS])
```

```python
import functools
import math

import jax
import jax.numpy as jnp
from jax import lax
from jax.experimental import pallas as pl
from jax.experimental.pallas import tpu as pltpu

F32 = jnp.float32
BF16 = jnp.bfloat16
HI = lax.Precision.HIGHEST
MESH = pl.DeviceIdType.MESH
SDS = jax.ShapeDtypeStruct
VMEM_LIMIT = 56 * 1024 * 1024
ANY = pl.BlockSpec(memory_space=pl.ANY)

D = 2048
PL_DIM = 256
D_FF = 4 * D
EPS = 1e-6
S5_W, S5_G, S5_GROUPS, S5_P = 512, 16, 32, 64
S5_N = S5_GROUPS * S5_P
SSD_W, SSD_HD, SSD_H, SSD_NG, SSD_N, SSD_L = 1536, 64, 24, 4, 128, 128
SSD_CONV = SSD_W + 2 * SSD_NG * SSD_N
EVEN_IN = S5_W + SSD_W + SSD_CONV + SSD_H
EVEN_PAD = 5120
RW_W, RW_H, RW_HD = 1024, 16, 64
RW_LORA = 96
RW_GATE = 256
RW_IN = 3 * RW_W + 2 * RW_LORA + RW_GATE
RW_PAD = 3584
LRU_W, LRU_B = 1024, 16
ODD_IN = RW_IN + 2 * LRU_W
ODD_PAD = RW_PAD + 2 * LRU_W
GN_EPS = 64e-5
LRU_C = 8.0
ADAM_LR, ADAM_B1, ADAM_B2, ADAM_EPS, ADAM_WD, ADAM_STEP = 0.001, 0.9, 0.999, 1e-08, 0.01, 10


def _cparams(sem=("arbitrary",)):
    return pltpu.CompilerParams(dimension_semantics=sem, vmem_limit_bytes=VMEM_LIMIT)


def _dot16(a, b, dims=(((1,), (0,)), ((), ()))):
    return lax.dot_general(a.astype(BF16), b.astype(BF16), dims, preferred_element_type=F32)


def _dot32(a, b, dims=(((1,), (0,)), ((), ()))):
    return lax.dot_general(a.astype(F32), b.astype(F32), dims, precision=HI, preferred_element_type=F32)


NT = (((1,), (1,)), ((), ()))
TN = (((0,), (0,)), ((), ()))


def _tile(dim, target):
    if dim <= target:
        return dim
    t = target - target % 128
    while t > 128 and dim % t:
        t -= 128
    assert dim % t == 0, (dim, target)
    return t


def _mm(a, b, *, ta=False, tb=False, add=None, out_dtype=F32, tm=1024, tn=1024, tk=1024, name,
        epilogue=None, extra=(), out_dtypes=None):
    layer = None
    if isinstance(b, tuple):
        b, layer = b
    m, k = (a.shape[1], a.shape[0]) if ta else a.shape
    n = b.shape[-2] if tb else b.shape[-1]
    assert (b.shape[-1] if tb else b.shape[-2]) == k, (a.shape, b.shape, ta, tb)
    tm, tn, tk = _tile(m, tm), _tile(n, tn), _tile(k, tk)
    nk = k // tk
    dims = (((0 if ta else 1,), (1 if tb else 0,)), ((), ()))
    ins = [a, b] + ([add] if add is not None else []) + list(extra)
    out_dtypes = out_dtypes or [out_dtype]
    n_in, n_out = len(ins), len(out_dtypes)

    def body(*refs):
        a_ref, b_ref = refs[:2]
        out_refs, acc_ref = refs[n_in:n_in + n_out], refs[-1]
        kk = pl.program_id(2)

        @pl.when(kk == 0)
        def _():
            acc_ref[...] = refs[2][...].astype(F32) if add is not None else jnp.zeros_like(acc_ref)

        acc_ref[...] += _dot16(a_ref[...], b_ref[...], dims)

        @pl.when(kk == nk - 1)
        def _():
            acc = acc_ref[...]
            res = epilogue(acc, *[r[...] for r in refs[n_in - len(extra):n_in]]) if epilogue else (acc,)
            for o_ref, val in zip(out_refs, res):
                o_ref[...] = val.astype(o_ref.dtype)

    a_spec = pl.BlockSpec((tk, tm), lambda i, j, q: (q, i)) if ta else pl.BlockSpec((tm, tk), lambda i, j, q: (i, q))
    b_spec = pl.BlockSpec((tn, tk), lambda i, j, q: (j, q)) if tb else pl.BlockSpec((tk, tn), lambda i, j, q: (q, j))
    if layer is not None:
        b_spec = (pl.BlockSpec((None, tn, tk), lambda i, j, q: (layer, j, q)) if tb
                  else pl.BlockSpec((None, tk, tn), lambda i, j, q: (layer, q, j)))
    o_spec = pl.BlockSpec((tm, tn), lambda i, j, q: (i, j))
    outs = pl.pallas_call(
        body,
        grid=(m // tm, n // tn, nk),
        in_specs=[a_spec, b_spec] + [o_spec] * (n_in - 2),
        out_specs=[o_spec] * n_out,
        out_shape=[SDS((m, n), dt) for dt in out_dtypes],
        scratch_shapes=[pltpu.VMEM((tm, tn), F32)],
        compiler_params=_cparams(("parallel", "parallel", "arbitrary")),
        name=name,
    )(*ins)
    return outs if epilogue else outs[0]


def _mm_grad(x, dy, *, layer, cols_cut, shard, prev, name):
    t = x.shape[0]
    r, c = shard
    tm, tn, tk = _tile(r, 1024), _tile(c, 1024), _tile(t, 1024)
    nk = t // tk
    if cols_cut:
        assert x.shape[1] == r and dy.shape[1] == 4 * c
        per = c // tn
        omap = lambda i, j, q: (j // per, layer, i, j % per)
    else:
        assert x.shape[1] == 4 * r and dy.shape[1] == c
        per = r // tm
        omap = lambda i, j, q: (i // per, layer, i % per, j)

    def body(*refs):
        x_ref, dy_ref = refs[:2]
        o_ref, acc_ref = refs[-2:]
        kk = pl.program_id(2)

        @pl.when(kk == 0)
        def _():
            acc_ref[...] = jnp.zeros_like(acc_ref)

        acc_ref[...] += _dot16(x_ref[...], dy_ref[...], TN)

        @pl.when(kk == nk - 1)
        def _():
            o_ref[...] = acc_ref[...]

    return pl.pallas_call(
        body,
        grid=(x.shape[1] // tm, dy.shape[1] // tn, nk),
        in_specs=[pl.BlockSpec((tk, tm), lambda i, j, q: (q, i)), pl.BlockSpec((tk, tn), lambda i, j, q: (q, j))]
        + ([ANY] if prev is not None else []),
        out_specs=pl.BlockSpec((None, None, tm, tn), omap),
        out_shape=SDS((4, 2, r, c), F32),
        scratch_shapes=[pltpu.VMEM((tm, tn), F32)],
        input_output_aliases={2: 0} if prev is not None else {},
        compiler_params=_cparams(("parallel", "parallel", "arbitrary")),
        name=name,
    )(x, dy, *([prev] if prev is not None else []))


def _single(fn, consts, *, name):
    outs = jax.eval_shape(fn, *[SDS(c.shape, F32) for c in consts])
    n_in = len(consts)

    def body(*refs):
        res = fn(*[r[...] for r in refs[:n_in]])
        for o_ref, v in zip(refs[n_in:], res):
            o_ref[...] = v

    return pl.pallas_call(body, out_shape=[SDS(o.shape, F32) for o in outs],
                          compiler_params=pltpu.CompilerParams(vmem_limit_bytes=VMEM_LIMIT), name=name)(*consts)


def _single_vjp(fn, consts, cots, *, name):
    n_in = len(consts)

    def body(*refs):
        _, pull = jax.vjp(fn, *[r[...] for r in refs[:n_in]])
        grads = pull(tuple(r[...] for r in refs[n_in:n_in + len(cots)]))
        for o_ref, v in zip(refs[n_in + len(cots):], grads):
            o_ref[...] = v

    return pl.pallas_call(body, out_shape=[SDS(c.shape, F32) for c in consts],
                          compiler_params=pltpu.CompilerParams(vmem_limit_bytes=VMEM_LIMIT), name=name)(*consts, *cots)


def _full_spec(shape):
    nd = len(shape)
    return pl.BlockSpec(shape, lambda i, _n=nd: (0,) * _n)


def _stage_shapes(fn, rows, consts, tb, pos):
    rs = [SDS((tb, r.shape[1]), F32) for r in rows]
    cs = [SDS(c.shape, F32) for c in consts]
    f = (lambda *a: fn(jnp.int32(0), *a)) if pos else fn
    return jax.eval_shape(f, *rs, *cs)


def _stage(fn, rows, consts, *, tb, name, out_dtypes, n_acc=0, pos=False):
    t = rows[0].shape[0]
    assert t % tb == 0
    outs = _stage_shapes(fn, rows, consts, tb, pos)
    n_out = len(outs)
    n_row = n_out - n_acc
    n_in = len(rows) + len(consts)

    def body(*refs):
        i = pl.program_id(0)
        vals = [r[...].astype(F32) for r in refs[:n_in]]
        res = fn(i * tb, *vals) if pos else fn(*vals)
        out_refs = refs[n_in:]
        for q in range(n_row):
            out_refs[q][...] = res[q].astype(out_refs[q].dtype)
        for q in range(n_row, n_out):
            @pl.when(i == 0)
            def _(q=q):
                out_refs[q][...] = jnp.zeros_like(out_refs[q])

            out_refs[q][...] += res[q]

    in_specs = [pl.BlockSpec((tb, r.shape[1]), lambda i: (i, 0)) for r in rows] + [_full_spec(c.shape) for c in consts]
    out_specs = [pl.BlockSpec((tb, o.shape[1]), lambda i: (i, 0)) for o in outs[:n_row]] + [_full_spec(o.shape) for o in outs[n_row:]]
    out_shape = [SDS((t, o.shape[1]), dt) for o, dt in zip(outs[:n_row], out_dtypes)] + [SDS(o.shape, F32) for o in outs[n_row:]]
    return pl.pallas_call(
        body, grid=(t // tb,), in_specs=in_specs, out_specs=out_specs, out_shape=out_shape,
        compiler_params=_cparams(), name=name,
    )(*rows, *consts)


def _stage_vjp(fn, rows, consts, cots, *, tb, name, drow, dconst, drow_dtypes=None, acc_cots=(), pos=False):
    t = rows[0].shape[0]
    assert t % tb == 0
    n_rows, n_consts, n_cots, n_acc = len(rows), len(consts), len(cots), len(acc_cots)
    n_in = n_rows + n_consts + n_cots + n_acc
    drow_dtypes = drow_dtypes or [F32] * len(drow)

    def body(*refs):
        i = pl.program_id(0)
        vals = [r[...].astype(F32) for r in refs[:n_in]]
        rv, cv = vals[:n_rows], vals[n_rows:n_rows + n_consts]
        ct = tuple(vals[n_rows + n_consts:])

        def f(*dargs):
            r2, c2 = list(rv), list(cv)
            for q, idx in enumerate(drow):
                r2[idx] = dargs[q]
            for q, idx in enumerate(dconst):
                c2[idx] = dargs[len(drow) + q]
            return fn(i * tb, *r2, *c2) if pos else fn(*r2, *c2)

        _, pull = jax.vjp(f, *[rv[q] for q in drow], *[cv[q] for q in dconst])
        grads = pull(ct)
        out_refs = refs[n_in:]
        for q in range(len(drow)):
            out_refs[q][...] = grads[q].astype(out_refs[q].dtype)
        for q in range(len(drow), len(drow) + len(dconst)):
            @pl.when(i == 0)
            def _(q=q):
                out_refs[q][...] = jnp.zeros_like(out_refs[q])

            out_refs[q][...] += grads[q]

    in_specs = ([pl.BlockSpec((tb, r.shape[1]), lambda i: (i, 0)) for r in rows] + [_full_spec(c.shape) for c in consts]
                + [pl.BlockSpec((tb, c.shape[1]), lambda i: (i, 0)) for c in cots] + [_full_spec(c.shape) for c in acc_cots])
    out_specs = ([pl.BlockSpec((tb, rows[q].shape[1]), lambda i: (i, 0)) for q in drow]
                 + [_full_spec(consts[q].shape) for q in dconst])
    out_shape = ([SDS(rows[q].shape, dt) for q, dt in zip(drow, drow_dtypes)]
                 + [SDS(consts[q].shape, F32) for q in dconst])
    return pl.pallas_call(
        body, grid=(t // tb,), in_specs=in_specs, out_specs=out_specs, out_shape=out_shape,
        compiler_params=_cparams(), name=name,
    )(*rows, *consts, *cots, *acc_cots)


def _conv_fwd(x, w, b, *, tb, name):
    t, c = x.shape
    r8 = tb // 8

    def body(x_ref, p_ref, w_ref, b_ref, o_ref):
        i = pl.program_id(0)
        x_ = x_ref[...]
        p_ = jnp.where(i > 0, p_ref[...], 0.0)
        w_ = w_ref[...]
        row = lax.broadcasted_iota(jnp.int32, x_.shape, 0)
        row8 = lax.broadcasted_iota(jnp.int32, p_.shape, 0)
        acc = x_ * w_[3:4, :] + b_ref[...]
        head = jnp.zeros_like(p_)
        for j in (1, 2, 3):
            wj = w_[3 - j:4 - j, :]
            acc += jnp.where(row >= j, pltpu.roll(x_, j, 0), 0.0) * wj
            head += jnp.where(row8 < j, pltpu.roll(p_, j, 0), 0.0) * wj
        o_ref[...] = acc
        o_ref[0:8, :] += head

    return pl.pallas_call(
        body, grid=(t // tb,),
        in_specs=[pl.BlockSpec((tb, c), lambda i: (i, 0)),
                  pl.BlockSpec((8, c), lambda i: (jnp.maximum(i * r8 - 1, 0), 0)),
                  _full_spec(w.shape), _full_spec(b.shape)],
        out_specs=pl.BlockSpec((tb, c), lambda i: (i, 0)),
        out_shape=SDS((t, c), F32), compiler_params=_cparams(), name=name,
    )(x, x, w, b)


def _conv_bwd(x, w, dy, *, tb, name):
    t, c = x.shape
    r8 = tb // 8
    nb = t // tb

    def body(x_ref, p_ref, w_ref, g_ref, n_ref, dx_ref, dw_ref, db_ref):
        i = pl.program_id(0)
        x_ = x_ref[...]
        p_ = jnp.where(i > 0, p_ref[...], 0.0)
        g_ = g_ref[...]
        n_ = jnp.where(i < nb - 1, n_ref[...], 0.0)
        w_ = w_ref[...]
        row = lax.broadcasted_iota(jnp.int32, x_.shape, 0)
        row8 = lax.broadcasted_iota(jnp.int32, p_.shape, 0)
        g8 = g_[0:8, :]
        dx = g_ * w_[3:4, :]
        tail = jnp.zeros_like(n_)
        dws = [jnp.sum(g_ * x_, axis=0, keepdims=True)]
        for j in (1, 2, 3):
            wj = w_[3 - j:4 - j, :]
            dx += jnp.where(row < tb - j, pltpu.roll(g_, tb - j, 0), 0.0) * wj
            tail += jnp.where(row8 >= 8 - j, pltpu.roll(n_, 8 - j, 0), 0.0) * wj
            xs = jnp.where(row >= j, pltpu.roll(x_, j, 0), 0.0)
            ps = jnp.where(row8 < j, pltpu.roll(p_, j, 0), 0.0)
            dws.append(jnp.sum(g_ * xs, axis=0, keepdims=True) + jnp.sum(g8 * ps, axis=0, keepdims=True))
        dx_ref[...] = dx
        dx_ref[tb - 8:tb, :] += tail

        @pl.when(i == 0)
        def _():
            dw_ref[...] = jnp.zeros_like(dw_ref)
            db_ref[...] = jnp.zeros_like(db_ref)

        for j in range(4):
            dw_ref[3 - j:4 - j, :] += dws[j]
        db_ref[...] += jnp.sum(g_, axis=0, keepdims=True)

    return pl.pallas_call(
        body, grid=(nb,),
        in_specs=[pl.BlockSpec((tb, c), lambda i: (i, 0)),
                  pl.BlockSpec((8, c), lambda i: (jnp.maximum(i * r8 - 1, 0), 0)),
                  _full_spec(w.shape),
                  pl.BlockSpec((tb, c), lambda i: (i, 0)),
                  pl.BlockSpec((8, c), lambda i: (jnp.minimum((i + 1) * r8, t // 8 - 1), 0))],
        out_specs=[pl.BlockSpec((tb, c), lambda i: (i, 0)), _full_spec((8, c)), _full_spec((1, c))],
        out_shape=[SDS((t, c), F32), SDS((8, c), F32), SDS((1, c), F32)],
        compiler_params=_cparams(), name=name,
    )(x, x, w, dy, dy)


def _lru_scan_fwd(a, b, *, tb, name):
    t, c = a.shape

    def body(a_ref, b_ref, h_ref, st_ref):
        @pl.when(pl.program_id(0) == 0)
        def _():
            st_ref[...] = jnp.zeros_like(st_ref)

        def step(s, h):
            h = a_ref[pl.ds(s, 1), :] * h + b_ref[pl.ds(s, 1), :]
            h_ref[pl.ds(s, 1), :] = h
            return h

        st_ref[...] = lax.fori_loop(0, tb, step, st_ref[...], unroll=8)

    blk = pl.BlockSpec((tb, c), lambda i: (i, 0))
    return pl.pallas_call(
        body, grid=(t // tb,), in_specs=[blk, blk], out_specs=blk, out_shape=SDS((t, c), F32),
        scratch_shapes=[pltpu.VMEM((1, c), F32)], compiler_params=_cparams(), name=name,
    )(a, b)


def _lru_scan_bwd(a, h, dh, *, tb, name):
    t, c = a.shape
    nb = t // tb
    r8 = tb // 8

    def body(a_ref, h_ref, p_ref, g_ref, da_ref, db_ref, st_ref):
        i = pl.program_id(0)

        @pl.when(i == 0)
        def _():
            st_ref[...] = jnp.zeros_like(st_ref)

        hprev0 = jnp.where(i < nb - 1, p_ref[7:8, :], 0.0)

        def step(q, carry):
            s = tb - 1 - q
            g = g_ref[pl.ds(s, 1), :] + carry
            hp = h_ref[pl.ds(jnp.maximum(s - 1, 0), 1), :]
            hp = jnp.where(s > 0, hp, hprev0)
            db_ref[pl.ds(s, 1), :] = g
            da_ref[pl.ds(s, 1), :] = g * hp
            return a_ref[pl.ds(s, 1), :] * g

        st_ref[...] = lax.fori_loop(0, tb, step, st_ref[...], unroll=8)

    rev = pl.BlockSpec((tb, c), lambda i: (nb - 1 - i, 0))
    prev = pl.BlockSpec((8, c), lambda i: (jnp.maximum((nb - 1 - i) * r8 - 1, 0), 0))
    return pl.pallas_call(
        body, grid=(nb,), in_specs=[rev, rev, prev, rev], out_specs=[rev, rev],
        out_shape=[SDS((t, c), F32), SDS((t, c), F32)],
        scratch_shapes=[pltpu.VMEM((1, c), F32)], compiler_params=_cparams(), name=name,
    )(a, h, h, dh)


def _s5_scan_fwd(ar, ai, br, bi, *, tb, name):
    t, c = br.shape

    def body(ar_ref, ai_ref, br_ref, bi_ref, xr_ref, xi_ref, sr_ref, si_ref):
        @pl.when(pl.program_id(0) == 0)
        def _():
            sr_ref[...] = jnp.zeros_like(sr_ref)
            si_ref[...] = jnp.zeros_like(si_ref)

        ar_, ai_ = ar_ref[...], ai_ref[...]

        def step(s, carry):
            xr, xi = carry
            nr = ar_ * xr - ai_ * xi + br_ref[pl.ds(s, 1), :]
            ni = ar_ * xi + ai_ * xr + bi_ref[pl.ds(s, 1), :]
            xr_ref[pl.ds(s, 1), :] = nr
            xi_ref[pl.ds(s, 1), :] = ni
            return nr, ni

        xr, xi = lax.fori_loop(0, tb, step, (sr_ref[...], si_ref[...]), unroll=8)
        sr_ref[...] = xr
        si_ref[...] = xi

    blk = pl.BlockSpec((tb, c), lambda i: (i, 0))
    one = _full_spec((1, c))
    return pl.pallas_call(
        body, grid=(t // tb,), in_specs=[one, one, blk, blk], out_specs=[blk, blk],
        out_shape=[SDS((t, c), F32), SDS((t, c), F32)],
        scratch_shapes=[pltpu.VMEM((1, c), F32), pltpu.VMEM((1, c), F32)], compiler_params=_cparams(), name=name,
    )(ar, ai, br, bi)


def _s5_scan_bwd(ar, ai, xr, xi, dxr, dxi, *, tb, name):
    t, c = xr.shape
    nb = t // tb
    r8 = tb // 8

    def body(ar_ref, ai_ref, xr_ref, xi_ref, pr_ref, pi_ref, gr_ref, gi_ref,
             dbr_ref, dbi_ref, dar_ref, dai_ref, cr_ref, ci_ref):
        i = pl.program_id(0)

        @pl.when(i == 0)
        def _():
            cr_ref[...] = jnp.zeros_like(cr_ref)
            ci_ref[...] = jnp.zeros_like(ci_ref)
            dar_ref[...] = jnp.zeros_like(dar_ref)
            dai_ref[...] = jnp.zeros_like(dai_ref)

        ar_, ai_ = ar_ref[...], ai_ref[...]
        first = i == nb - 1
        pr0 = jnp.where(first, 0.0, pr_ref[7:8, :])
        pi0 = jnp.where(first, 0.0, pi_ref[7:8, :])

        def step(q, carry):
            cr, ci, dar, dai = carry
            s = tb - 1 - q
            gr = gr_ref[pl.ds(s, 1), :] + cr
            gi = gi_ref[pl.ds(s, 1), :] + ci
            sp = jnp.maximum(s - 1, 0)
            xpr = jnp.where(s > 0, xr_ref[pl.ds(sp, 1), :], pr0)
            xpi = jnp.where(s > 0, xi_ref[pl.ds(sp, 1), :], pi0)
            dbr_ref[pl.ds(s, 1), :] = gr
            dbi_ref[pl.ds(s, 1), :] = gi
            dar = dar + gr * xpr + gi * xpi
            dai = dai - gr * xpi + gi * xpr
            return ar_ * gr + ai_ * gi, ar_ * gi - ai_ * gr, dar, dai

        cr, ci, dar, dai = lax.fori_loop(0, tb, step, (cr_ref[...], ci_ref[...], dar_ref[...], dai_ref[...]), unroll=8)
        cr_ref[...] = cr
        ci_ref[...] = ci
        dar_ref[...] = dar
        dai_ref[...] = dai

    rev = pl.BlockSpec((tb, c), lambda i: (nb - 1 - i, 0))
    prev = pl.BlockSpec((8, c), lambda i: (jnp.maximum((nb - 1 - i) * r8 - 1, 0), 0))
    one = _full_spec((1, c))
    return pl.pallas_call(
        body, grid=(nb,), in_specs=[one, one, rev, rev, prev, prev, rev, rev], out_specs=[rev, rev, one, one],
        out_shape=[SDS((t, c), F32), SDS((t, c), F32), SDS((1, c), F32), SDS((1, c), F32)],
        scratch_shapes=[pltpu.VMEM((1, c), F32), pltpu.VMEM((1, c), F32)], compiler_params=_cparams(), name=name,
    )(ar, ai, xr, xi, xr, xi, dxr, dxi)


RW_PAIRS = RW_H // 2


def _pair_consts():
    sub = lax.broadcasted_iota(jnp.int32, (64, 128), 0)
    lane = lax.broadcasted_iota(jnp.int32, (64, 128), 1)
    eye2 = ((lane & 63) == sub).astype(F32)
    r2 = lax.broadcasted_iota(jnp.int32, (128, 128), 0)
    c2 = lax.broadcasted_iota(jnp.int32, (128, 128), 1)
    bsel = ((r2 >> 6) == (c2 >> 6)).astype(BF16)
    return eye2, bsel


def _segsum(x, bsel):
    rows = x.shape[0]
    bits = lax.bitcast_convert_type(x, jnp.int32)
    hi = lax.bitcast_convert_type(bits & jnp.int32(-65536), F32)
    both = jnp.concatenate([hi.astype(BF16), (x - hi).astype(BF16)], axis=0)
    res = jnp.dot(both, bsel, preferred_element_type=F32)
    return res[:rows] + res[rows:]


def _bc(x8):
    return jnp.stack([jnp.broadcast_to(x8[q:q + 1, :], (64, 128)) for q in range(RW_PAIRS)])


def _seg3(x3, bsel):
    return _segsum(x3.reshape(RW_PAIRS * 64, 128), bsel).reshape(RW_PAIRS, 64, 128)


def _seg3_lanes(x3):
    first = lax.broadcasted_iota(jnp.int32, x3.shape, 2) < 64
    lo = jnp.sum(jnp.where(first, x3, 0.0), axis=-1, keepdims=True)
    hi = jnp.sum(jnp.where(first, 0.0, x3), axis=-1, keepdims=True)
    return jnp.where(first, lo, hi)


def _rwkv_scan_fwd(r, w, k, v, kk, a, *, lc, name):
    t = r.shape[0]
    nc = t // lc

    def body(r_ref, w_ref, k_ref, v_ref, kk_ref, a_ref, y_ref, ck_ref, st_ref):
        @pl.when(pl.program_id(0) == 0)
        def _():
            st_ref[...] = jnp.zeros_like(st_ref)

        ck_ref[0] = st_ref[...]
        eye2, bsel = _pair_consts()
        column = lambda ref, s: _seg3(eye2[None] * _bc(ref[s]), bsel)
        read = lambda st, s: jnp.sum(eye2[None] * _seg3(st * _bc(r_ref[s]), bsel), axis=1)

        def step(s, carry):
            st, vb = carry
            kk8 = kk_ref[s]
            sa = -_seg3_lanes(st * _bc(kk8))
            vb_next = column(v_ref, jnp.minimum(s + 1, lc - 1))
            before = jnp.maximum(s - 1, 0)
            y_ref[before] = read(st, before)
            return st * _bc(w_ref[s]) + sa * _bc(kk8 * a_ref[s]) + vb * _bc(k_ref[s]), vb_next

        st, _ = lax.fori_loop(0, lc, step, (st_ref[...], column(v_ref, 0)))
        y_ref[lc - 1] = read(st, lc - 1)
        st_ref[...] = st

    blk = pl.BlockSpec((lc, RW_PAIRS, 128), lambda i: (i, 0, 0))
    return pl.pallas_call(
        body, grid=(nc,), in_specs=[blk] * 6,
        out_specs=[blk, pl.BlockSpec((1, RW_PAIRS, 64, 128), lambda i: (i, 0, 0, 0))],
        out_shape=[SDS((t, RW_PAIRS, 128), F32), SDS((nc, RW_PAIRS, 64, 128), F32)],
        scratch_shapes=[pltpu.VMEM((RW_PAIRS, 64, 128), F32)],
        compiler_params=_cparams(), name=name,
    )(r, w, k, v, kk, a)


def _rwkv_scan_bwd(r, w, k, v, kk, a, ck, dy, *, lc, name):
    t = r.shape[0]
    nc = t // lc

    def body(r_ref, w_ref, k_ref, v_ref, kk_ref, a_ref, ck_ref, dy_ref,
             dr_ref, dw_ref, dk_ref, dv_ref, dkk_ref, da_ref,
             ds_ref, vb_ref, dyb_ref, hist_ref, sa_ref):
        @pl.when(pl.program_id(0) == 0)
        def _():
            ds_ref[...] = jnp.zeros_like(ds_ref)

        eye2, bsel = _pair_consts()
        column = lambda ref, s: _seg3(eye2[None] * _bc(ref[s]), bsel)
        hist_ref[0] = ck_ref[0]

        def fwd(s, carry):
            st, vb = carry
            kk8 = kk_ref[s]
            sa = -_seg3_lanes(st * _bc(kk8))
            vb_next = column(v_ref, jnp.minimum(s + 1, lc - 1))
            dyb_ref[s] = column(dy_ref, s)
            vb_ref[s] = vb
            sa_ref[s] = sa
            st = st * _bc(w_ref[s]) + sa * _bc(kk8 * a_ref[s]) + vb * _bc(k_ref[s])
            hist_ref[s + 1] = st
            return st, vb_next

        lax.fori_loop(0, lc, fwd, (ck_ref[0], column(v_ref, 0)))

        def grads(s, d_s, dsa):
            s_prev, s_cur = hist_ref[s], hist_ref[s + 1]
            col = lambda z: jnp.sum(z, axis=1)
            db = col(d_s * sa_ref[s])
            dr_ref[s] = col(s_cur * dyb_ref[s])
            dw_ref[s] = col(d_s * s_prev)
            dv_ref[s] = col(eye2[None] * _seg3(d_s * _bc(k_ref[s]), bsel))
            dk_ref[s] = col(d_s * vb_ref[s])
            dkk_ref[s] = db * a_ref[s] - col(s_prev * dsa)
            da_ref[s] = db * kk_ref[s]

        def back(j, carry):
            ds, d_after, dsa_after = carry
            s = lc - 1 - j
            kk8 = kk_ref[s]
            d_s = ds + dyb_ref[s] * _bc(r_ref[s])
            dsa = _seg3_lanes(d_s * _bc(kk8 * a_ref[s]))
            grads(jnp.minimum(s + 1, lc - 1), d_after, dsa_after)
            return d_s * _bc(w_ref[s]) - dsa * _bc(kk8), d_s, dsa

        zero = jnp.zeros((RW_PAIRS, 64, 128), F32)
        ds, d_first, dsa_first = lax.fori_loop(0, lc, back, (ds_ref[...], zero, zero))
        grads(0, d_first, dsa_first)
        ds_ref[...] = ds

    rev = pl.BlockSpec((lc, RW_PAIRS, 128), lambda i: (nc - 1 - i, 0, 0))
    big = lambda n: pltpu.VMEM((n, RW_PAIRS, 64, 128), F32)
    return pl.pallas_call(
        body, grid=(nc,),
        in_specs=[rev] * 6 + [pl.BlockSpec((1, RW_PAIRS, 64, 128), lambda i: (nc - 1 - i, 0, 0, 0)), rev],
        out_specs=[rev] * 6, out_shape=[SDS((t, RW_PAIRS, 128), F32)] * 6,
        scratch_shapes=[pltpu.VMEM((RW_PAIRS, 64, 128), F32), big(lc), big(lc), big(lc + 1), big(lc)],
        compiler_params=_cparams(), name=name,
    )(r, w, k, v, kk, a, ck, dy)


SSD_PAIRS = SSD_H // 2


def _ssd_chunk(states, xdt, da, bm, cm):
    ln = SSD_L
    row = lax.broadcasted_iota(jnp.int32, (ln, ln), 0)
    col = lax.broadcasted_iota(jnp.int32, (ln, ln), 1)
    causal = row >= col
    acum = _dot32(causal.astype(F32), da)
    acum_t = _dot32(da, (row <= col).astype(F32), TN)
    sub = lax.broadcasted_iota(jnp.int32, (128, 128), 0)
    lane = lax.broadcasted_iota(jnp.int32, (128, 128), 1)
    ys, new_states = [], []
    for q in range(SSD_PAIRS):
        g = q // (SSD_PAIRS // SSD_NG)
        bg = bm[:, g * SSD_N:(g + 1) * SSD_N]
        cg = cm[:, g * SSD_N:(g + 1) * SSD_N]
        xq = xdt[:, q * 128:(q + 1) * 128]
        scores = _dot16(cg, bg, NT)
        aexp = _dot32(acum, (sub == 2 * q + (lane >> 6)).astype(F32))
        tot = aexp[ln - 1:ln, :]
        yh = []
        for h in (2 * q, 2 * q + 1):
            seg = _dot32(acum, (sub == h).astype(F32)) - acum_t[h:h + 1, :]
            yh.append(_dot16(scores * jnp.exp(jnp.where(causal, seg, -1e30)), xq))
        y = jnp.where(lane < 64, yh[0], yh[1]) + _dot16(cg, states[q]) * jnp.exp(aexp)
        new = _dot16(bg, xq * jnp.exp(tot - aexp), TN)
        ys.append(y)
        new_states.append(states[q] * jnp.exp(tot) + new)
    return jnp.concatenate(ys, axis=1), new_states


def _ssd_fwd(xdt, da, bm, cm, *, name):
    t = xdt.shape[0]
    nc = t // SSD_L

    def body(x_ref, a_ref, b_ref, c_ref, y_ref, ck_ref, st_ref):
        @pl.when(pl.program_id(0) == 0)
        def _():
            st_ref[...] = jnp.zeros_like(st_ref)

        ck_ref[0] = st_ref[...]
        y, new = _ssd_chunk([st_ref[q] for q in range(SSD_PAIRS)], x_ref[...], a_ref[...], b_ref[...], c_ref[...])
        y_ref[...] = y
        for q in range(SSD_PAIRS):
            st_ref[q] = new[q]

    blk = lambda wd: pl.BlockSpec((SSD_L, wd), lambda i: (i, 0))
    return pl.pallas_call(
        body, grid=(nc,), in_specs=[blk(SSD_W), blk(128), blk(512), blk(512)],
        out_specs=[blk(SSD_W), pl.BlockSpec((1, SSD_PAIRS, 128, 128), lambda i: (i, 0, 0, 0))],
        out_shape=[SDS((t, SSD_W), F32), SDS((nc, SSD_PAIRS, 128, 128), F32)],
        scratch_shapes=[pltpu.VMEM((SSD_PAIRS, 128, 128), F32)], compiler_params=_cparams(), name=name,
    )(xdt, da, bm, cm)


def _ssd_bwd(xdt, da, bm, cm, ck, dy, *, name):
    t = xdt.shape[0]
    nc = t // SSD_L

    def body(x_ref, a_ref, b_ref, c_ref, ck_ref, dy_ref, dx_ref, dda_ref, db_ref, dc_ref, ds_ref):
        @pl.when(pl.program_id(0) == 0)
        def _():
            ds_ref[...] = jnp.zeros_like(ds_ref)

        _, pull = jax.vjp(_ssd_chunk, [ck_ref[0, q] for q in range(SSD_PAIRS)], x_ref[...], a_ref[...], b_ref[...], c_ref[...])
        dst, dx, dda, db, dc = pull((dy_ref[...], [ds_ref[q] for q in range(SSD_PAIRS)]))
        dx_ref[...] = dx
        dda_ref[...] = dda
        db_ref[...] = db
        dc_ref[...] = dc
        for q in range(SSD_PAIRS):
            ds_ref[q] = dst[q]

    rev = lambda wd: pl.BlockSpec((SSD_L, wd), lambda i: (nc - 1 - i, 0))
    return pl.pallas_call(
        body, grid=(nc,),
        in_specs=[rev(SSD_W), rev(128), rev(512), rev(512),
                  pl.BlockSpec((1, SSD_PAIRS, 128, 128), lambda i: (nc - 1 - i, 0, 0, 0)), rev(SSD_W)],
        out_specs=[rev(SSD_W), rev(128), rev(512), rev(512)],
        out_shape=[SDS((t, SSD_W), F32), SDS((t, 128), F32), SDS((t, 512), F32), SDS((t, 512), F32)],
        scratch_shapes=[pltpu.VMEM((SSD_PAIRS, 128, 128), F32)], compiler_params=_cparams(), name=name,
    )(xdt, da, bm, cm, ck, dy)


def _iota(shape, dim):
    return lax.broadcasted_iota(jnp.int32, shape, dim)


def _rms(x, g):
    return x * lax.rsqrt(jnp.mean(x * x, axis=-1, keepdims=True) + EPS) * g


def _head_sel(width, shift):
    return ((_iota((width, 128), 0) >> shift) == _iota((width, 128), 1)).astype(F32)


def _head_sum(x, shift=6):
    sel = _head_sel(x.shape[1], shift)
    return _dot32(_dot32(x, sel), sel, NT)


def _head_expand(x, width, shift=6):
    return _dot32(x, _head_sel(width, shift), NT)


def f_norm(h, g):
    return (_rms(h, g),)


def f_norm_pass(h, g):
    return _rms(h, g), h


def f_add_norm(h, m, g):
    h1 = h + m
    return h1, _rms(h1, g)


def f_relu2(u):
    r = jnp.maximum(u, 0.0)
    return (r * r,)


def f_plgate(h2, gl, pp):
    return (h2 + jax.nn.sigmoid(gl) * pp,)


def f_loss(h, tgt, g):
    err = _rms(h, g) - tgt
    part = 0.5 * jnp.sum(jnp.mean(err * err, axis=-1, keepdims=True), axis=0, keepdims=True)
    return (jnp.broadcast_to(part, (8, 128)),)


def f_s5_prep(lam_re, lam_im, lstep, bre_t, bim_t, cre_t, cim_t):
    step = jnp.exp(_dot32(lstep, _head_sel(S5_N, 6), NT)[0:1, :])
    mag = jnp.exp(lam_re * step)
    abar_re, abar_im = mag * jnp.cos(lam_im * step), mag * jnp.sin(lam_im * step)
    den = lam_re * lam_re + lam_im * lam_im
    nr = abar_re - 1.0
    coef_re = (nr * lam_re + abar_im * lam_im) / den
    coef_im = (abar_im * lam_re - nr * lam_im) / den
    bbar_re = coef_re * bre_t - coef_im * bim_t
    bbar_im = coef_re * bim_t + coef_im * bre_t
    rep = ((_iota((S5_W, S5_G), 0) & (S5_G - 1)) == _iota((S5_W, S5_G), 1)).astype(F32)
    blk = ((_iota((S5_W, S5_N), 0) >> 4) == (_iota((S5_W, S5_N), 1) >> 6)).astype(F32)
    blk_t = ((_iota((S5_N, S5_W), 0) >> 6) == (_iota((S5_N, S5_W), 1) >> 4)).astype(F32)
    wb_re, wb_im = _dot32(rep, bbar_re) * blk, _dot32(rep, bbar_im) * blk
    wc_re, wc_im = _dot32(cre_t, rep, NT) * blk_t, _dot32(cim_t, rep, NT) * blk_t
    return abar_re, abar_im, wb_re, wb_im, wc_re, wc_im


def f_s5_post(xr, xi, u, wc_re, wc_im, d_skip, glu_w, glu_b):
    y = _dot16(xr, wc_re) - _dot16(xi, wc_im) + d_skip * u
    act = jax.nn.gelu(y)
    return (act * jax.nn.sigmoid(_dot16(act, glu_w) + glu_b),)


def f_ssd_pre(xc, dtr, dt_bias, a_log):
    act = jax.nn.silu(xc)
    heads = _iota(dtr.shape, 1) < SSD_H
    dt = jnp.where(heads, jax.nn.softplus(dtr + dt_bias), 0.0)
    da = dt * (-jnp.exp(a_log))
    xdt = act[:, :SSD_W] * _head_expand(dt, SSD_W)
    return xdt, da, act[:, SSD_W:SSD_W + 512], act[:, SSD_W + 512:]


def f_ssd_pre_pass(xc, dtr, dt_bias, a_log):
    return f_ssd_pre(xc, dtr, dt_bias, a_log) + (xc,)


def f_ssd_post(y, xc, z, d_skip, norm_g):
    xs = jax.nn.silu(xc[:, :SSD_W])
    y = (y + xs * _head_expand(d_skip, SSD_W)) * jax.nn.silu(z)
    gw = SSD_W // SSD_NG
    parts = []
    for g in range(SSD_NG):
        seg = y[:, g * gw:(g + 1) * gw]
        parts.append(seg * lax.rsqrt(jnp.mean(seg * seg, axis=-1, keepdims=True) + EPS))
    return (jnp.concatenate(parts, axis=1) * norm_g,)


def f_rwkv_pre(f, w0, w_up, a0, a_up, g_up, k_k, k_a):
    r, k, v = f[:, 0:1024], f[:, 1024:2048], f[:, 2048:3072]
    wl, al, gl = f[:, 3072:3200], f[:, 3200:3328], f[:, 3328:3584]
    w = -jax.nn.softplus(-(w0 + _dot16(jnp.tanh(wl), w_up))) - 0.5
    decay = jnp.exp(-jnp.exp(w))
    a = jax.nn.sigmoid(a0 + _dot16(al, a_up))
    g = _dot16(jax.nn.sigmoid(gl), g_up)
    kk = k * k_k
    k2 = k * (1.0 + (a - 1.0) * k_a)
    kkn = kk * lax.rsqrt(jnp.maximum(_head_sum(kk * kk), 1e-24))
    return r, decay, k2, v, kkn, a, g


def f_rwkv_pre_pass(f, w0, w_up, a0, a_up, g_up, k_k, k_a):
    out = f_rwkv_pre(f, w0, w_up, a0, a_up, g_up, k_k, k_a)
    return out + (out[0], out[2], out[3])


def f_rwkv_post(y, r, k2, v, g, ln_g, ln_b, r_k):
    mean = _head_sum(y) * (1.0 / RW_HD)
    yc = y - mean
    var = _head_sum(yc * yc) * (1.0 / RW_HD)
    yn = yc * lax.rsqrt(var + GN_EPS) * ln_g + ln_b
    bonus = _head_sum(r * k2 * r_k) * v
    return ((yn + bonus) * g,)


def _neg_expm1(y):
    series = -y * (1.0 + y * (0.5 + y * (1.0 / 6.0 + y * (1.0 / 24.0 + y * (1.0 / 120.0)))))
    return jnp.where(y > -0.1, series, 1.0 - jnp.exp(y))


def f_lru_pre(t0, xc, w_a, b_a, w_x, b_x, lam):
    gate_r = jax.nn.sigmoid(_dot16(xc, w_a) + b_a)
    gate_i = jax.nn.sigmoid(_dot16(xc, w_x) + b_x)
    log_a = -LRU_C * gate_r * jax.nn.softplus(-lam)
    mult = jnp.sqrt(jnp.maximum(_neg_expm1(2.0 * log_a), 0.0))
    mult = jnp.where(_iota(xc.shape, 0) + t0 == 0, 1.0, mult)
    return jnp.exp(log_a), xc * gate_i * mult


def f_lru_post(h, gl):
    return (h * jax.nn.gelu(gl),)


TB = 256
TBH = 128
SCAN_TB = 256
RW_LC = 32


def _even_fwd(hn, w, tag):
    n = lambda s: f"{tag}_{s}"
    u = _mm(hn, w["in_u"], name=n("proj_u"))
    z = _mm(hn, w["in_z"], name=n("proj_z"))
    xbc = _mm(hn, w["in_xbc"], name=n("proj_xbc"))
    dtr = _mm(hn, w["in_dt"], name=n("proj_dt"))
    bu_re = _mm(u, w["wb_re"], name=n("s5_bu_re"))
    bu_im = _mm(u, w["wb_im"], name=n("s5_bu_im"))
    xr, xi = _s5_scan_fwd(w["abar_re"], w["abar_im"], bu_re, bu_im, tb=SCAN_TB, name=n("s5_scan"))
    s5c = [w["wc_re"], w["wc_im"], w["s5_d"], w["glu_w"], w["glu_b"]]
    (ya,) = _stage(f_s5_post, [xr, xi, u], s5c, tb=TB, name=n("s5_post"), out_dtypes=[BF16])
    xc = _conv_fwd(xbc, w["ssd_conv_w"], w["ssd_conv_b"], tb=TB, name=n("ssd_conv"))
    xdt, da, bm, cm = _stage(f_ssd_pre, [xc, dtr], [w["dt_bias"], w["a_log"]], tb=TB, name=n("ssd_pre"),
                             out_dtypes=[F32] * 4)
    y, ck = _ssd_fwd(xdt, da, bm, cm, name=n("ssd_scan"))
    (yb,) = _stage(f_ssd_post, [y, xc, z], [w["ssd_d"], w["ssd_norm"]], tb=TB, name=n("ssd_post"), out_dtypes=[BF16])
    mo = _mm(ya, w["out_a"], name=n("out_a"))
    mo = _mm(yb, w["out_b"], add=mo, name=n("out_b"))
    res = dict(u=u, z=z, xbc=xbc, dtr=dtr, xr=xr, xi=xi, ya=ya, xc=xc, xdt=xdt, da=da, bm=bm, cm=cm, y=y, ck=ck, yb=yb)
    return mo, res


def _even_bwd(dmo, hn, w, r, tag):
    n = lambda s: f"{tag}_{s}"
    g = {}
    g["out_a"] = _mm(r["ya"], dmo, ta=True, name=n("d_out_a"))
    g["out_b"] = _mm(r["yb"], dmo, ta=True, name=n("d_out_b"))
    dya = _mm(dmo, w["out_a"], tb=True, name=n("dya"))
    dyb = _mm(dmo, w["out_b"], tb=True, name=n("dyb"))
    dy, dxc1, dz, g["ssd_d"], g["ssd_norm"] = _stage_vjp(
        f_ssd_post, [r["y"], r["xc"], r["z"]], [w["ssd_d"], w["ssd_norm"]], [dyb], tb=TBH, name=n("ssd_post_b"),
        drow=[0, 1, 2], dconst=[0, 1])
    dxdt, dda, dbm, dcm = _ssd_bwd(r["xdt"], r["da"], r["bm"], r["cm"], r["ck"], dy, name=n("ssd_scan_b"))
    dxc, ddtr, g["dt_bias"], g["a_log"] = _stage_vjp(
        f_ssd_pre_pass, [r["xc"], r["dtr"]], [w["dt_bias"], w["a_log"]], [dxdt, dda, dbm, dcm, dxc1], tb=TBH,
        name=n("ssd_pre_b"), drow=[0, 1], dconst=[0, 1])
    dxbc, g["ssd_conv_w"], g["ssd_conv_b"] = _conv_bwd(r["xbc"], w["ssd_conv_w"], dxc, tb=TB, name=n("ssd_conv_b"))
    s5c = [w["wc_re"], w["wc_im"], w["s5_d"], w["glu_w"], w["glu_b"]]
    dxr, dxi, du1, g["wc_re"], g["wc_im"], g["s5_d"], g["glu_w"], g["glu_b"] = _stage_vjp(
        f_s5_post, [r["xr"], r["xi"], r["u"]], s5c, [dya], tb=TBH, name=n("s5_post_b"),
        drow=[0, 1, 2], dconst=[0, 1, 2, 3, 4])
    dbr, dbi, g["abar_re"], g["abar_im"] = _s5_scan_bwd(w["abar_re"], w["abar_im"], r["xr"], r["xi"], dxr, dxi,
                                                         tb=SCAN_TB, name=n("s5_scan_b"))
    g["wb_re"] = _mm(r["u"], dbr, ta=True, name=n("d_wb_re"))
    g["wb_im"] = _mm(r["u"], dbi, ta=True, name=n("d_wb_im"))
    du = _mm(dbr, w["wb_re"], tb=True, add=du1, name=n("du_re"))
    du = _mm(dbi, w["wb_im"], tb=True, add=du, name=n("du_im"))
    segs = (("in_u", du), ("in_z", dz), ("in_xbc", dxbc), ("in_dt", ddtr))
    dhn = None
    for key, dseg in segs:
        g[key] = _mm(hn, dseg, ta=True, name=n("d_" + key))
        dhn = _mm(dseg, w[key], tb=True, add=dhn, name=n("dhn_" + key))
    return dhn, g


def _odd_fwd(hn, w, tag):
    n = lambda s: f"{tag}_{s}"
    rw = _mm(hn, w["in_rw"], name=n("proj_rw"))
    xl = _mm(hn, w["in_xl"], name=n("proj_xl"))
    gl = _mm(hn, w["in_gl"], name=n("proj_gl"))
    f = _conv_fwd(rw, w["mix_w"], w["mix_b"], tb=TB, name=n("rwkv_shift"))
    rc = [w[k] for k in ("w0", "w_up", "a0", "a_up", "g_up", "k_k", "k_a")]
    r_, dec, k2, v, kkn, a, gate = _stage(f_rwkv_pre, [f], rc, tb=TB, name=n("rwkv_pre"), out_dtypes=[F32] * 7)
    t3 = lambda z: z.reshape(-1, RW_PAIRS, 128)
    y, ck = _rwkv_scan_fwd(t3(r_), t3(dec), t3(k2), t3(v), t3(kkn), t3(a), lc=RW_LC, name=n("rwkv_scan"))
    y = y.reshape(-1, RW_W)
    (yc,) = _stage(f_rwkv_post, [y, r_, k2, v, gate], [w["ln_g"], w["ln_b"], w["r_k"]], tb=TB, name=n("rwkv_post"),
                   out_dtypes=[BF16])
    xc = _conv_fwd(xl, w["lru_conv_w"], w["lru_conv_b"], tb=TB, name=n("lru_conv"))
    lc = [w[k] for k in ("lru_wa", "lru_b_a", "lru_wx", "lru_b_x", "lru_lam")]
    a_l, bx = _stage(f_lru_pre, [xc], lc, tb=TB, name=n("lru_pre"), out_dtypes=[F32] * 2, pos=True)
    h = _lru_scan_fwd(a_l, bx, tb=SCAN_TB, name=n("lru_scan"))
    (yd,) = _stage(f_lru_post, [h, gl], [], tb=TB, name=n("lru_post"), out_dtypes=[BF16])
    mo = _mm(yc, w["out_a"], name=n("out_a"))
    mo = _mm(yd, w["out_b"], add=mo, name=n("out_b"))
    res = dict(rw=rw, xl=xl, gl=gl, f=f, r=r_, dec=dec, k2=k2, v=v, kkn=kkn, a=a, gate=gate, y=y, ck=ck, yc=yc,
               xc=xc, a_l=a_l, h=h, yd=yd)
    return mo, res


def _odd_bwd(dmo, hn, w, r, tag):
    n = lambda s: f"{tag}_{s}"
    g = {}
    g["out_a"] = _mm(r["yc"], dmo, ta=True, name=n("d_out_a"))
    g["out_b"] = _mm(r["yd"], dmo, ta=True, name=n("d_out_b"))
    dyc = _mm(dmo, w["out_a"], tb=True, name=n("dyc"))
    dyd = _mm(dmo, w["out_b"], tb=True, name=n("dyd"))
    dh, dgl = _stage_vjp(f_lru_post, [r["h"], r["gl"]], [], [dyd], tb=TB, name=n("lru_post_b"), drow=[0, 1], dconst=[])
    da_l, dbx = _lru_scan_bwd(r["a_l"], r["h"], dh, tb=SCAN_TB, name=n("lru_scan_b"))
    lc = [w[k] for k in ("lru_wa", "lru_b_a", "lru_wx", "lru_b_x", "lru_lam")]
    dxc, g["lru_wa"], g["lru_b_a"], g["lru_wx"], g["lru_b_x"], g["lru_lam"] = _stage_vjp(
        f_lru_pre, [r["xc"]], lc, [da_l, dbx], tb=TBH, name=n("lru_pre_b"), drow=[0], dconst=[0, 1, 2, 3, 4], pos=True)
    dxl, g["lru_conv_w"], g["lru_conv_b"] = _conv_bwd(r["xl"], w["lru_conv_w"], dxc, tb=TB, name=n("lru_conv_b"))
    dy, dr1, dk1, dv1, dgate, g["ln_g"], g["ln_b"], g["r_k"] = _stage_vjp(
        f_rwkv_post, [r["y"], r["r"], r["k2"], r["v"], r["gate"]], [w["ln_g"], w["ln_b"], w["r_k"]], [dyc], tb=TBH,
        name=n("rwkv_post_b"), drow=[0, 1, 2, 3, 4], dconst=[0, 1, 2])
    t3 = lambda z: z.reshape(-1, RW_PAIRS, 128)
    dr2, ddec, dk2, dv2, dkkn, da = [z.reshape(-1, RW_W) for z in _rwkv_scan_bwd(
        t3(r["r"]), t3(r["dec"]), t3(r["k2"]), t3(r["v"]), t3(r["kkn"]), t3(r["a"]), r["ck"], t3(dy),
        lc=RW_LC, name=n("rwkv_scan_b"))]
    rc = [w[k] for k in ("w0", "w_up", "a0", "a_up", "g_up", "k_k", "k_a")]
    df, g["w0"], g["w_up"], g["a0"], g["a_up"], g["g_up"], g["k_k"], g["k_a"] = _stage_vjp(
        f_rwkv_pre_pass, [r["f"]], rc, [dr2, ddec, dk2, dv2, dkkn, da, dgate, dr1, dk1, dv1], tb=TBH,
        name=n("rwkv_pre_b"), drow=[0], dconst=[0, 1, 2, 3, 4, 5, 6])
    drw, g["mix_w"], _ = _conv_bwd(r["rw"], w["mix_w"], df, tb=TB, name=n("rwkv_shift_b"))
    segs = (("in_rw", drw), ("in_xl", dxl), ("in_gl", dgl))
    dhn = None
    for key, dseg in segs:
        g[key] = _mm(hn, dseg, ta=True, name=n("d_" + key))
        dhn = _mm(dseg, w[key], tb=True, add=dhn, name=n("dhn_" + key))
    return dhn, g


def _layer_fwd(h, p_i, w, odd, tag):
    n = lambda s: f"{tag}_{s}"
    (hn,) = _stage(f_norm, [h], [w["norm_mix"]], tb=TB, name=n("norm_mix"), out_dtypes=[BF16])
    mo, mres = (_odd_fwd if odd else _even_fwd)(hn, w, tag)
    h1, hf = _stage(f_add_norm, [h, mo], [w["norm_ffn"]], tb=TB, name=n("norm_ffn"), out_dtypes=[F32, BF16])
    u, act = _mm(hf, w["mlp_w1"], name=n("mlp_up"), epilogue=lambda acc: (acc,) + f_relu2(acc), out_dtypes=[F32, BF16])
    m2 = _mm(act, w["mlp_w2"], name=n("mlp_down"))
    h2, hp = _stage(f_add_norm, [h1, m2], [w["norm_pl"]], tb=TB, name=n("norm_pl"), out_dtypes=[F32, BF16])
    gl = _mm(hp, w["pl_gate"], name=n("pl_gate"))
    pp = _mm(p_i, w["pl_proj"], name=n("pl_proj"))
    (h3,) = _stage(f_plgate, [h2, gl, pp], [], tb=TB, name=n("pl_mix"), out_dtypes=[F32])
    res = dict(h=h, hn=hn, mo=mo, mix=mres, h1=h1, hf=hf, u=u, act=act, m2=m2, h2=h2, hp=hp, gl=gl, pp=pp)
    return h3, res


def _layer_bwd(dh3, p_i, w, r, odd, tag, stacks):
    n = lambda s: f"{tag}_{s}"
    g = {}
    wgrad = lambda key, x, dy, cols_cut, shard: _mm_grad(x, dy, layer=int(odd), cols_cut=cols_cut, shard=shard,
                                                        prev=stacks[key] if stacks else None, name=n("d_" + key))
    dh2, dgl, dpp = _stage_vjp(f_plgate, [r["h2"], r["gl"], r["pp"]], [], [dh3], tb=TB, name=n("pl_mix_b"),
                               drow=[0, 1, 2], dconst=[])
    g["pl_proj"] = wgrad("pl_proj", p_i, dpp, True, (PL_DIM, D // 4))
    g["pl_gate"] = wgrad("pl_gate", r["hp"], dgl, False, (D // 4, D))
    dhp = _mm(dgl, w["pl_gate"], tb=True, name=n("dhp"))
    dh1, dm2, g["norm_pl"] = _stage_vjp(f_add_norm, [r["h1"], r["m2"]], [w["norm_pl"]], [dh2, dhp], tb=TB,
                                        name=n("norm_pl_b"), drow=[0, 1], dconst=[0])
    g["mlp_w2"] = wgrad("mlp_w2", r["act"], dm2, False, (D_FF // 4, D))
    (du,) = _mm(dm2, w["mlp_w2"], tb=True, name=n("dact"), extra=[r["u"]], out_dtypes=[BF16],
                epilogue=lambda acc, u: (acc * (2.0 * jnp.maximum(u, 0.0)),))
    g["mlp_w1"] = wgrad("mlp_w1", r["hf"], du, True, (D, D_FF // 4))
    dhf = _mm(du, w["mlp_w1"], tb=True, name=n("dhf"))
    dh, dmo, g["norm_ffn"] = _stage_vjp(f_add_norm, [r["h"], r["mo"]], [w["norm_ffn"]], [dh1, dhf], tb=TB,
                                        name=n("norm_ffn_b"), drow=[0, 1], dconst=[0])
    dhn, gm = (_odd_bwd if odd else _even_bwd)(dmo, r["hn"], w, r["mix"], tag)
    g.update(gm)
    dh0, g["norm_mix"] = _stage_vjp(f_norm_pass, [r["h"]], [w["norm_mix"]], [dhn, dh], tb=TB, name=n("norm_mix_b"),
                                    drow=[0], dconst=[0])
    return dh0, g


def _pad_to(a, size, axis):
    pad = [(0, 0)] * a.ndim
    pad[axis] = (0, size - a.shape[axis])
    return jnp.pad(a, pad)


def _rw_pad(a):
    return jnp.concatenate([a[..., :3072], _pad_to(a[..., 3072:3168], 128, -1), _pad_to(a[..., 3168:3264], 128, -1),
                            a[..., 3264:3520]], axis=-1)


def _rw_unpad(a):
    return jnp.concatenate([a[..., :3072], a[..., 3072:3168], a[..., 3200:3296], a[..., 3328:3584]], axis=-1)


def _block_diag(w):
    nb, bs, _ = w.shape
    eye = jnp.eye(nb, dtype=w.dtype)
    return (w[:, :, None, :] * eye[:, None, :, None]).reshape(nb * bs, nb * bs)


def _diag_blocks(w):
    nb = LRU_B
    bs = w.shape[0] // nb
    return jnp.stack([w[h * bs:(h + 1) * bs, h * bs:(h + 1) * bs] for h in range(nb)])


def _s5_prep_inputs(fw):
    lstep = jnp.broadcast_to(_pad_to(fw["s5_log_step"].astype(F32), 128, 1), (8, 128))
    t16 = lambda b: jnp.transpose(b[0], (2, 0, 1)).reshape(S5_G, S5_N)
    tc = lambda c: jnp.transpose(c[0], (0, 2, 1)).reshape(S5_N, S5_G)
    return [fw["s5_lam_re"].reshape(1, S5_N), fw["s5_lam_im"].reshape(1, S5_N), lstep,
            t16(fw["s5_b_re"]), t16(fw["s5_b_im"]), tc(fw["s5_c_re"]), tc(fw["s5_c_im"])]


def _layer_weights(fw, i):
    w = {k: fw[k][i:i + 1] for k in ("norm_mix", "norm_ffn", "norm_pl")}
    for k in ("mlp_w1", "mlp_w2", "pl_proj", "pl_gate"):
        w[k] = (fw[k], i)
    return w


def _even_weights(fw, prep):
    w = _layer_weights(fw, 0)
    ein, eout = fw["e_in_proj"][0], fw["e_out_proj"][0]
    w.update(in_u=ein[:, :512], in_z=ein[:, 512:2048], in_xbc=ein[:, 2048:4608], in_dt=_pad_to(ein[:, 4608:], 128, 1),
             out_a=eout[:512], out_b=eout[512:])
    abar_re, abar_im, wb_re, wb_im, wc_re, wc_im = prep
    w.update(abar_re=abar_re, abar_im=abar_im, wb_re=wb_re, wb_im=wb_im, wc_re=wc_re.astype(BF16), wc_im=wc_im.astype(BF16),
             s5_d=fw["s5_d"], glu_w=fw["s5_glu_w"][0], glu_b=fw["s5_glu_b"],
             ssd_conv_w=_pad_to(fw["ssd_conv_w"][0], 8, 0), ssd_conv_b=fw["ssd_conv_b"],
             dt_bias=_pad_to(fw["ssd_dt_bias"], 128, 1), a_log=_pad_to(fw["ssd_a_log"], 128, 1),
             ssd_d=_pad_to(fw["ssd_d"], 128, 1), ssd_norm=fw["ssd_norm"])
    return w


def _odd_weights(fw):
    w = _layer_weights(fw, 1)
    oin, oout = fw["o_in_proj"][0], fw["o_out_proj"][0]
    mu = _rw_pad(fw["rwkv_mu"])
    zero = jnp.zeros_like(mu)
    w.update(in_rw=_rw_pad(oin[:, :RW_IN]), in_xl=oin[:, RW_IN:RW_IN + LRU_W], in_gl=oin[:, RW_IN + LRU_W:],
             out_a=oout[:RW_W], out_b=oout[RW_W:],
             mix_w=jnp.concatenate([zero, zero, mu, 1.0 - mu, zero, zero, zero, zero], axis=0), mix_b=zero,
             w0=fw["rwkv_w0"], w_up=_pad_to(fw["rwkv_w_up"][0], 128, 0), a0=fw["rwkv_a0"],
             a_up=_pad_to(fw["rwkv_a_up"][0], 128, 0), g_up=fw["rwkv_g_up"][0], k_k=fw["rwkv_k_k"], k_a=fw["rwkv_k_a"],
             r_k=fw["rwkv_r_k"].reshape(1, RW_W), ln_g=fw["rwkv_ln_g"], ln_b=fw["rwkv_ln_b"],
             lru_conv_w=_pad_to(fw["lru_conv_w"][0], 8, 0), lru_conv_b=fw["lru_conv_b"],
             lru_wa=_block_diag(fw["lru_w_a"][0]).astype(BF16), lru_b_a=fw["lru_b_a"].reshape(1, LRU_W),
             lru_wx=_block_diag(fw["lru_w_x"][0]).astype(BF16), lru_b_x=fw["lru_b_x"].reshape(1, LRU_W),
             lru_lam=fw["lru_lam"].reshape(1, LRU_W))
    return w


def _global_grads(g0, g1, s5_grads, d_norm_final):
    out = {k: jnp.concatenate([g0[k], g1[k]], axis=0) for k in ("norm_mix", "norm_ffn", "norm_pl")}
    for k in STACKED:
        out[k] = g0[k]
    out["e_in_proj"] = jnp.concatenate([g0["in_u"], g0["in_z"], g0["in_xbc"], g0["in_dt"][:, :SSD_H]], axis=1)[None]
    out["e_out_proj"] = jnp.concatenate([g0["out_a"], g0["out_b"]], axis=0)[None]
    d_lam_re, d_lam_im, d_lstep, d_bre, d_bim, d_cre, d_cim = s5_grads
    out["s5_lam_re"] = d_lam_re.reshape(1, S5_GROUPS, S5_P)
    out["s5_lam_im"] = d_lam_im.reshape(1, S5_GROUPS, S5_P)
    out["s5_log_step"] = d_lstep[0:1, :S5_GROUPS]
    unb = lambda b: jnp.transpose(b.reshape(S5_G, S5_GROUPS, S5_P), (1, 2, 0))[None]
    unc = lambda c: jnp.transpose(c.reshape(S5_GROUPS, S5_P, S5_G), (0, 2, 1))[None]
    out.update(s5_b_re=unb(d_bre), s5_b_im=unb(d_bim), s5_c_re=unc(d_cre), s5_c_im=unc(d_cim),
               s5_d=g0["s5_d"], s5_glu_w=g0["glu_w"][None], s5_glu_b=g0["glu_b"],
               ssd_conv_w=g0["ssd_conv_w"][None, :4], ssd_conv_b=g0["ssd_conv_b"], ssd_dt_bias=g0["dt_bias"][:, :SSD_H],
               ssd_a_log=g0["a_log"][:, :SSD_H], ssd_d=g0["ssd_d"][:, :SSD_H], ssd_norm=g0["ssd_norm"])
    out["o_in_proj"] = jnp.concatenate([_rw_unpad(g1["in_rw"]), g1["in_xl"], g1["in_gl"]], axis=1)[None]
    out["o_out_proj"] = jnp.concatenate([g1["out_a"], g1["out_b"]], axis=0)[None]
    out.update(rwkv_mu=_rw_unpad(g1["mix_w"][2:3] - g1["mix_w"][3:4]), rwkv_w0=g1["w0"], rwkv_w_up=g1["w_up"][None, :RW_LORA],
               rwkv_a0=g1["a0"], rwkv_a_up=g1["a_up"][None, :RW_LORA], rwkv_g_up=g1["g_up"][None], rwkv_k_k=g1["k_k"],
               rwkv_k_a=g1["k_a"], rwkv_r_k=g1["r_k"].reshape(1, RW_H, RW_HD), rwkv_ln_g=g1["ln_g"], rwkv_ln_b=g1["ln_b"],
               lru_conv_w=g1["lru_conv_w"][None, :4], lru_conv_b=g1["lru_conv_b"],
               lru_w_a=_diag_blocks(g1["lru_wa"])[None], lru_b_a=g1["lru_b_a"].reshape(1, LRU_B, 64),
               lru_w_x=_diag_blocks(g1["lru_wx"])[None], lru_b_x=g1["lru_b_x"].reshape(1, LRU_B, 64),
               lru_lam=g1["lru_lam"].reshape(1, LRU_B, 64), norm_final=d_norm_final.reshape(D))
    return out


def _local_step(x, p, target, fw):
    prep_in = _s5_prep_inputs(fw)
    prep = _single(f_s5_prep, prep_in, name="s5_prep")
    w0, w1 = _even_weights(fw, prep), _odd_weights(fw)
    h1, r0 = _layer_fwd(x, p[0], w0, False, "l0")
    h2, r1 = _layer_fwd(h1, p[1], w1, True, "l1")
    gf = fw["norm_final"].reshape(1, D)
    (loss8,) = _stage(f_loss, [h2, target], [gf], tb=TB, name="loss", out_dtypes=[], n_acc=1)
    one = jnp.zeros((8, 128), F32).at[0, 0].set(1.0)
    dh2, d_gf = _stage_vjp(f_loss, [h2, target], [gf], [], tb=TB, name="loss_b", drow=[0], dconst=[0], acc_cots=[one])
    dh1, g1 = _layer_bwd(dh2, p[1], w1, r1, True, "l1", None)
    dx, g0 = _layer_bwd(dh1, p[0], w0, r0, False, "l0", g1)
    cots = [g0[k] for k in ("abar_re", "abar_im", "wb_re", "wb_im", "wc_re", "wc_im")]
    s5_grads = _single_vjp(f_s5_prep, prep_in, cots, name="s5_prep_b")
    return loss8[0, 0], dx, _global_grads(g0, g1, s5_grads, d_gf)


def _xyc():
    return lax.axis_index("x"), lax.axis_index("y"), lax.axis_index("c")


def _flip(v, bit):
    return 1 - v if bit else v


def _remote(src, dst, send_sems, recv_sems, k, dev):
    return pltpu.make_async_remote_copy(src_ref=src, dst_ref=dst, send_sem=send_sems.at[k], recv_sem=recv_sems.at[k],
                                        device_id=dev, device_id_type=MESH)


def _dma_scratch(n_remote, n_local):
    return [pltpu.SemaphoreType.DMA((n_remote,)), pltpu.SemaphoreType.DMA((n_remote,)), pltpu.SemaphoreType.DMA((n_local,))]


CHIP_FLIPS = ((1, 0), (0, 1), (1, 1))


def _gather_chips(arrs, out_shapes, places, *, name):
    n = len(arrs)

    def body(*refs):
        ins, outs = refs[:n], refs[n:2 * n]
        send_sems, recv_sems = refs[2 * n:]
        x, y, c = _xyc()
        chip, sib = 2 * x + y, (x, y, 1 - c)
        peers = [(_flip(x, fx), _flip(y, fy)) for fx, fy in CHIP_FLIPS]
        first = [_remote(ins[a].at[c], places[a](outs[a], chip, c), send_sems, recv_sems, 6 * a + j, (px, py, c))
                 for a in range(n) for j, (px, py) in enumerate(peers)]
        for cp in first:
            cp.start()
        passed = []
        for a in range(n):
            for j, (px, py) in enumerate(peers):
                landed = places[a](outs[a], 2 * px + py, c)
                _remote(ins[a].at[c], landed, send_sems, recv_sems, 6 * a + j, (px, py, c)).wait_recv()
                cp = _remote(landed, landed, send_sems, recv_sems, 6 * a + 3 + j, sib)
                cp.start()
                passed.append(cp)
        for a in range(n):
            for j, (px, py) in enumerate(peers):
                other = places[a](outs[a], 2 * px + py, 1 - c)
                _remote(other, other, send_sems, recv_sems, 6 * a + 3 + j, sib).wait_recv()
        for cp in first + passed:
            cp.wait_send()

    return pl.pallas_call(
        body, out_shape=[SDS(s, a.dtype) for s, a in zip(out_shapes, arrs)], in_specs=[ANY] * n, out_specs=[ANY] * n,
        scratch_shapes=_dma_sems(6 * n), name=name,
    )(*arrs)


def _place_own(full, own, chip_vec, axis, *, name):
    layers, rows, cols = own.shape
    tb = min(rows, 512)
    per = rows // tb
    omap = ((lambda l, i, chip_ref: (l, chip_ref[0] * per + i, 0)) if axis == 1
            else (lambda l, i, chip_ref: (l, i, chip_ref[0])))

    def body(chip_ref, own_ref, full_ref, o_ref):
        o_ref[...] = own_ref[...]

    return pl.pallas_call(
        body,
        grid_spec=pltpu.PrefetchScalarGridSpec(
            num_scalar_prefetch=1, grid=(layers, per),
            in_specs=[pl.BlockSpec((None, tb, cols), lambda l, i, chip_ref: (l, i, 0)), ANY],
            out_specs=pl.BlockSpec((None, tb, cols), omap)),
        out_shape=SDS(full.shape, full.dtype), input_output_aliases={2: 0},
        compiler_params=_cparams(("arbitrary", "arbitrary")), name=name,
    )(chip_vec, own, full)


def _gather_all(arr, *, name):
    def body(in_ref, out_ref, send_sems, recv_sems, loc_sems):
        x, y, c = _xyc()
        mine = out_ref.at[4 * x + 2 * y + c]
        lc = pltpu.make_async_copy(in_ref, mine, loc_sems.at[0])
        lc.start()
        sends = []
        for k in range(1, 8):
            dev = (_flip(x, k >> 2 & 1), _flip(y, k >> 1 & 1), _flip(c, k & 1))
            cp = _remote(in_ref, mine, send_sems, recv_sems, k - 1, dev)
            cp.start()
            sends.append(cp)
        for k in range(1, 8):
            px, py, pc = _flip(x, k >> 2 & 1), _flip(y, k >> 1 & 1), _flip(c, k & 1)
            _remote(in_ref, out_ref.at[4 * px + 2 * py + pc], send_sems, recv_sems, k - 1, (px, py, pc)).wait_recv()
        for cp in sends:
            cp.wait_send()
        lc.wait()

    return pl.pallas_call(
        body, out_shape=SDS((8,) + arr.shape, arr.dtype), in_specs=[ANY], out_specs=ANY,
        scratch_shapes=_dma_scratch(7, 1), name=name,
    )(arr)


def _dma_sems(n):
    return [pltpu.SemaphoreType.DMA((n,)), pltpu.SemaphoreType.DMA((n,))]


def _send_halves(arrs, *, name):
    n = len(arrs)

    def body(*refs):
        ins, outs = refs[:n], refs[n:2 * n]
        send_sems, recv_sems = refs[2 * n:]
        x, y, c = _xyc()
        copies = [_remote(ins[a].at[k, 1 - c], outs[a].at[k], send_sems, recv_sems, 4 * a + k, (x, y, 1 - c))
                  for a in range(n) for k in range(4)]
        for cp in copies:
            cp.start()
        for cp in copies:
            cp.wait_recv()
        for cp in copies:
            cp.wait_send()

    return pl.pallas_call(
        body, out_shape=[SDS((4,) + a.shape[2:], a.dtype) for a in arrs], in_specs=[ANY] * n, out_specs=[ANY] * n,
        scratch_shapes=_dma_sems(4 * n), name=name,
    )(*arrs)


def _add_half(g, recv, c_vec, *, tb, out_dtype, name):
    _, _, rh, cols = g.shape
    tb = min(tb, rh)

    def body(c_ref, g_ref, r_ref, o_ref):
        o_ref[...] = (g_ref[...] + r_ref[...]).astype(o_ref.dtype)

    return pl.pallas_call(
        body,
        grid_spec=pltpu.PrefetchScalarGridSpec(
            num_scalar_prefetch=1, grid=(4, rh // tb),
            in_specs=[pl.BlockSpec((None, None, tb, cols), lambda k, i, c_ref: (k, c_ref[0], i, 0)),
                      pl.BlockSpec((None, tb, cols), lambda k, i, c_ref: (k, i, 0))],
            out_specs=pl.BlockSpec((None, tb, cols), lambda k, i, c_ref: (k, i, 0))),
        out_shape=SDS((4, rh, cols), out_dtype), compiler_params=_cparams(("arbitrary", "arbitrary")), name=name,
    )(c_vec, g, recv)


def _scatter_chips(arrs, *, name):
    n = len(arrs)

    def body(*refs):
        ins, outs = refs[:n], refs[n:2 * n]
        send_sems, recv_sems = refs[2 * n:]
        x, y, c = _xyc()
        copies = []
        for a in range(n):
            for j, (fx, fy) in enumerate(CHIP_FLIPS):
                px, py = _flip(x, fx), _flip(y, fy)
                copies.append(_remote(ins[a].at[2 * px + py], outs[a].at[j], send_sems, recv_sems, 3 * a + j, (px, py, c)))
        for cp in copies:
            cp.start()
        for cp in copies:
            cp.wait_recv()
        for cp in copies:
            cp.wait_send()

    return pl.pallas_call(
        body, out_shape=[SDS((3,) + a.shape[1:], a.dtype) for a in arrs], in_specs=[ANY] * n, out_specs=[ANY] * n,
        scratch_shapes=_dma_sems(3 * n), name=name,
    )(*arrs)


def _sum_chips(p, landed, chip_vec, *, tb, name):
    _, rh, cols = p.shape
    tb = min(tb, rh)

    def body(chip_ref, p_ref, l_ref, o_ref):
        f = lambda z: z.astype(F32)
        o_ref[...] = ((f(p_ref[...]) + f(l_ref[0])) + f(l_ref[1])) + f(l_ref[2])

    return pl.pallas_call(
        body,
        grid_spec=pltpu.PrefetchScalarGridSpec(
            num_scalar_prefetch=1, grid=(rh // tb,),
            in_specs=[pl.BlockSpec((None, tb, cols), lambda i, chip_ref: (chip_ref[0], i, 0)),
                      pl.BlockSpec((3, tb, cols), lambda i, chip_ref: (0, i, 0))],
            out_specs=pl.BlockSpec((tb, cols), lambda i, chip_ref: (i, 0))),
        out_shape=SDS((rh, cols), F32), compiler_params=_cparams(), name=name,
    )(chip_vec, p, landed)


def _swap_halves(arrs, *, name):
    n = len(arrs)

    def body(*refs):
        ins, outs = refs[:n], refs[n:2 * n]
        send_sems, recv_sems = refs[2 * n:]
        x, y, c = _xyc()
        copies = [_remote(ins[a], outs[a], send_sems, recv_sems, a, (x, y, 1 - c)) for a in range(n)]
        for cp in copies:
            cp.start()
        for cp in copies:
            cp.wait_recv()
        for cp in copies:
            cp.wait_send()

    return pl.pallas_call(
        body, out_shape=[SDS(a.shape, a.dtype) for a in arrs], in_specs=[ANY] * n, out_specs=[ANY] * n,
        scratch_shapes=_dma_sems(n), name=name,
    )(*arrs)


def _join_halves(mine, theirs, c_vec, *, tb, name):
    rh, cols = mine.shape
    tb = min(tb, rh)

    def body(c_ref, m_ref, t_ref, o_ref):
        o_ref[...] = jnp.where(pl.program_id(0) == c_ref[0], m_ref[...], t_ref[...])

    blk = pl.BlockSpec((tb, cols), lambda h, i, c_ref: (i, 0))
    return pl.pallas_call(
        body,
        grid_spec=pltpu.PrefetchScalarGridSpec(
            num_scalar_prefetch=1, grid=(2, rh // tb), in_specs=[blk, blk],
            out_specs=pl.BlockSpec((None, tb, cols), lambda h, i, c_ref: (h, i, 0))),
        out_shape=SDS((2, rh, cols), mine.dtype), compiler_params=_cparams(("arbitrary", "arbitrary")), name=name,
    )(c_vec, mine, theirs)


def _sum_lead(x, *, tb, name):
    k, r, c = x.shape
    tb = min(tb, r)
    assert r % tb == 0

    def body(x_ref, o_ref):
        acc = x_ref[0]
        for q in range(1, k):
            acc = acc + x_ref[q]
        o_ref[...] = acc

    return pl.pallas_call(
        body, grid=(r // tb,), in_specs=[pl.BlockSpec((k, tb, c), lambda i: (0, i, 0))],
        out_specs=pl.BlockSpec((tb, c), lambda i: (i, 0)), out_shape=SDS((r, c), x.dtype),
        compiler_params=_cparams(), name=name,
    )(x)


def f_adamw(w, g, m, v):
    m = ADAM_B1 * m + (1.0 - ADAM_B1) * g
    v = ADAM_B2 * v + (1.0 - ADAM_B2) * (g * g)
    m_hat = m / (1.0 - ADAM_B1 ** ADAM_STEP)
    v_hat = v / (1.0 - ADAM_B2 ** ADAM_STEP)
    return -ADAM_LR * (m_hat / (jnp.sqrt(v_hat) + ADAM_EPS) + ADAM_WD * w), m, v


def _adamw(w, g, m, v, *, name):
    shape = w.shape
    two = lambda a: a.reshape(-1, shape[-1])
    rows = two(w).shape[0]
    tb = 256 if rows % 256 == 0 else rows
    outs = _stage(f_adamw, [two(w), two(g), two(m), two(v)], [], tb=tb, name=name, out_dtypes=[F32] * 3)
    return [o.reshape(shape) for o in outs]


def _pack(arrs, rows=8):
    flat = jnp.concatenate([a.astype(F32).reshape(-1) for a in arrs])
    size = -(-flat.shape[0] // (rows * 128)) * (rows * 128)
    return _pad_to(flat, size, 0).reshape(-1, 128)


def _unpack(buf, shapes):
    flat = buf.reshape(-1)
    out, off = [], 0
    for s in shapes:
        n = math.prod(s)
        out.append(flat[off:off + n].reshape(s))
        off += n
    return out


WEIGHTS = ("norm_mix", "norm_ffn", "norm_pl", "mlp_w1", "mlp_w2", "pl_proj", "pl_gate", "e_in_proj", "e_out_proj",
           "s5_lam_re", "s5_lam_im", "s5_log_step", "s5_b_re", "s5_b_im", "s5_c_re", "s5_c_im", "s5_d", "s5_glu_w",
           "s5_glu_b", "ssd_conv_w", "ssd_conv_b", "ssd_dt_bias", "ssd_a_log", "ssd_d", "ssd_norm", "o_in_proj",
           "o_out_proj", "rwkv_mu", "rwkv_w0", "rwkv_w_up", "rwkv_a0", "rwkv_a_up", "rwkv_g_up", "rwkv_k_k", "rwkv_k_a",
           "rwkv_r_k", "rwkv_ln_g", "rwkv_ln_b", "lru_conv_w", "lru_conv_b", "lru_w_a", "lru_b_a", "lru_w_x", "lru_b_x",
           "lru_lam", "norm_final")
BIG = ("mlp_w1", "mlp_w2", "pl_proj", "pl_gate", "e_in_proj", "e_out_proj", "o_in_proj", "o_out_proj")
STACKED = BIG[:4]
SHARD_AXIS = {"mlp_w1": 2, "mlp_w2": 1, "pl_proj": 2, "pl_gate": 1, "e_in_proj": 2, "e_out_proj": 1, "s5_glu_w": 1,
              "ssd_conv_w": 2, "o_in_proj": 2, "o_out_proj": 1, "rwkv_mu": 1, "rwkv_w0": 1, "rwkv_w_up": 2, "rwkv_a0": 1,
              "rwkv_a_up": 2, "rwkv_g_up": 2, "rwkv_k_k": 1, "rwkv_k_a": 1, "rwkv_ln_g": 1, "rwkv_ln_b": 1,
              "lru_conv_w": 2, "lru_conv_b": 1}
SMALL = tuple(n for n in WEIGHTS if n not in BIG)
SMALL_SHARDED = tuple(n for n in SMALL if n in SHARD_AXIS)


def _gather_weights(w):
    shapes = [w[n].shape for n in SMALL_SHARDED]
    chip = 2 * lax.axis_index("x") + lax.axis_index("y")
    mine = [w[n].astype(BF16) for n in BIG] + [_pack([w[n] for n in SMALL_SHARDED], rows=16)]
    out_shapes, places = [], []
    for n, a in zip(BIG + ("small",), mine):
        layers, rows, cols = a.shape if a.ndim == 3 else (1,) + a.shape
        ax = SHARD_AXIS.get(n)
        if ax == 1:
            step = rows if layers == 2 else rows // 2
            out_shapes.append((layers, 4 * rows, cols))
            places.append(lambda o, k, h, layers=layers, rows=rows, step=step: o.at[
                h if layers == 2 else 0, pl.ds(pl.multiple_of(k * rows + (0 if layers == 2 else h * step), 16), step), :])
        elif ax == 2 and layers == 2:
            out_shapes.append((layers, rows, 4 * cols))
            places.append(lambda o, k, h, cols=cols: o.at[h, :, pl.ds(pl.multiple_of(k * cols, 128), cols)])
        else:
            out_shapes.append((4, 2, layers * rows // 2, cols))
            places.append(lambda o, k, h: o.at[k, h])
    got = _gather_chips([a.reshape(2, -1, a.shape[-1]) for a in mine], out_shapes, places, name="gather_weights")
    fw = {n: w[n] for n in SMALL if n not in SHARD_AXIS}
    for n, g, a in zip(BIG, got[:-1], mine):
        if g.shape[0] == 4:
            g = lax.dynamic_update_index_in_dim(g.reshape((4,) + a.shape), a, chip, 0)
            fw[n] = jnp.concatenate([g[k] for k in range(4)], axis=SHARD_AXIS[n])
        else:
            fw[n] = _place_own(g, a, chip.astype(jnp.int32).reshape(1), SHARD_AXIS[n], name=f"place_{n}")
    small = lax.dynamic_update_index_in_dim(got[-1].reshape((4,) + mine[-1].shape), mine[-1], chip, 0)
    parts = [_unpack(small[k], shapes) for k in range(4)]
    for i, n in enumerate(SMALL_SHARDED):
        fw[n] = jnp.concatenate([parts[k][i] for k in range(4)], axis=SHARD_AXIS[n])
    return fw


def _reduce_big(grads, w):
    stacks = []
    for n in BIG:
        cols = w[n].shape[-1]
        stacks.append(grads[n] if n in STACKED else
                      jnp.stack(jnp.split(grads[n], 4, axis=SHARD_AXIS[n])).reshape(4, 2, -1, cols))
    c_vec = lax.axis_index("c").astype(jnp.int32).reshape(1)
    chip_vec = (2 * lax.axis_index("x") + lax.axis_index("y")).astype(jnp.int32).reshape(1)
    got = _send_halves(stacks, name="reduce_pair")
    sums = [_add_half(s, r, c_vec, tb=512, out_dtype=BF16, name=f"reduce_pair_sum_{n}")
            for n, s, r in zip(BIG, stacks, got)]
    landed = _scatter_chips(sums, name="reduce_chips")
    halves = [_sum_chips(p, l, chip_vec, tb=256, name=f"reduce_chips_sum_{n}") for n, p, l in zip(BIG, sums, landed)]
    theirs = _swap_halves(halves, name="reduce_swap")
    return {n: _join_halves(h, t, c_vec, tb=512, name=f"reduce_join_{n}").reshape(w[n].shape)
            for n, h, t in zip(BIG, halves, theirs)}


def _reduce_small(grads, w, chip, loss):
    shapes = [grads[n].shape for n in SMALL] + [(1,)]
    packed = _pack([grads[n] for n in SMALL] + [loss.reshape(1)])
    total = _sum_lead(_gather_all(packed, name="reduce_small"), tb=packed.shape[0], name="reduce_small_sum")
    *parts, loss_sum = _unpack(total, shapes)
    out = {}
    for n, g in zip(SMALL, parts):
        if n in SHARD_AXIS:
            ax = SHARD_AXIS[n]
            size = w[n].shape[ax]
            g = lax.dynamic_slice_in_dim(g, chip * size, size, axis=ax)
        out[n] = g
    return out, loss_sum[0]


def kernel(x, p, norm_mix, norm_ffn, norm_pl, mlp_w1, mlp_w2, pl_proj, pl_gate, e_in_proj, e_out_proj, s5_lam_re, s5_lam_im, s5_log_step, s5_b_re, s5_b_im, s5_c_re, s5_c_im, s5_d, s5_glu_w, s5_glu_b, ssd_conv_w, ssd_conv_b, ssd_dt_bias, ssd_a_log, ssd_d, ssd_norm, o_in_proj, o_out_proj, rwkv_mu, rwkv_w0, rwkv_w_up, rwkv_a0, rwkv_a_up, rwkv_g_up, rwkv_k_k, rwkv_k_a, rwkv_r_k, rwkv_ln_g, rwkv_ln_b, lru_conv_w, lru_conv_b, lru_w_a, lru_b_a, lru_w_x, lru_b_x, lru_lam, norm_final, loss_target, m_norm_mix, m_norm_ffn, m_norm_pl, m_mlp_w1, m_mlp_w2, m_pl_proj, m_pl_gate, m_e_in_proj, m_e_out_proj, m_s5_lam_re, m_s5_lam_im, m_s5_log_step, m_s5_b_re, m_s5_b_im, m_s5_c_re, m_s5_c_im, m_s5_d, m_s5_glu_w, m_s5_glu_b, m_ssd_conv_w, m_ssd_conv_b, m_ssd_dt_bias, m_ssd_a_log, m_ssd_d, m_ssd_norm, m_o_in_proj, m_o_out_proj, m_rwkv_mu, m_rwkv_w0, m_rwkv_w_up, m_rwkv_a0, m_rwkv_a_up, m_rwkv_g_up, m_rwkv_k_k, m_rwkv_k_a, m_rwkv_r_k, m_rwkv_ln_g, m_rwkv_ln_b, m_lru_conv_w, m_lru_conv_b, m_lru_w_a, m_lru_b_a, m_lru_w_x, m_lru_b_x, m_lru_lam, m_norm_final, v_norm_mix, v_norm_ffn, v_norm_pl, v_mlp_w1, v_mlp_w2, v_pl_proj, v_pl_gate, v_e_in_proj, v_e_out_proj, v_s5_lam_re, v_s5_lam_im, v_s5_log_step, v_s5_b_re, v_s5_b_im, v_s5_c_re, v_s5_c_im, v_s5_d, v_s5_glu_w, v_s5_glu_b, v_ssd_conv_w, v_ssd_conv_b, v_ssd_dt_bias, v_ssd_a_log, v_ssd_d, v_ssd_norm, v_o_in_proj, v_o_out_proj, v_rwkv_mu, v_rwkv_w0, v_rwkv_w_up, v_rwkv_a0, v_rwkv_a_up, v_rwkv_g_up, v_rwkv_k_k, v_rwkv_k_a, v_rwkv_r_k, v_rwkv_ln_g, v_rwkv_ln_b, v_lru_conv_w, v_lru_conv_b, v_lru_w_a, v_lru_b_a, v_lru_w_x, v_lru_b_x, v_lru_lam, v_norm_final):
    given = dict(locals())
    w = {n: given[n] for n in WEIGHTS}
    m = {n: given["m_" + n] for n in WEIGHTS}
    v = {n: given["v_" + n] for n in WEIGHTS}
    chip = 2 * lax.axis_index("x") + lax.axis_index("y")

    fw = _gather_weights(w)
    loss, dx, grads = _local_step(x[0], p[:, 0], loss_target[0], fw)
    g = _reduce_big(grads, w)
    g_small, loss = _reduce_small(grads, w, chip, loss)
    g.update(g_small)

    delta, new_m, new_v = {}, {}, {}
    for n in BIG:
        delta[n], new_m[n], new_v[n] = _adamw(w[n], g[n], m[n], v[n], name=f"adamw_{n}")
    shapes = [w[n].shape for n in SMALL]
    packed = [_pack([d[n] for n in SMALL]) for d in (w, g, m, v)]
    for d, buf in zip((delta, new_m, new_v), _adamw(*packed, name="adamw_small")):
        d.update(zip(SMALL, _unpack(buf, shapes)))
    return (loss, dx[None], *[g[n] for n in WEIGHTS], *[delta[n] for n in WEIGHTS],
            *[new_m[n] for n in WEIGHTS], *[new_v[n] for n in WEIGHTS])
```

```python
import functools
import math

import jax
import jax.numpy as jnp
from jax import lax
from jax.experimental import pallas as pl
from jax.experimental.pallas import tpu as pltpu

F32 = jnp.float32
BF16 = jnp.bfloat16
HI = lax.Precision.HIGHEST
MESH = pl.DeviceIdType.MESH
SDS = jax.ShapeDtypeStruct
VMEM_LIMIT = 56 * 1024 * 1024
MM_VMEM_BUDGET = 40 * 1024 * 1024
ANY = pl.BlockSpec(memory_space=pl.ANY)

D = 2048
PL_DIM = 256
D_FF = 4 * D
EPS = 1e-6
S5_W, S5_G, S5_GROUPS, S5_P = 512, 16, 32, 64
S5_N = S5_GROUPS * S5_P
SSD_W, SSD_HD, SSD_H, SSD_NG, SSD_N, SSD_L = 1536, 64, 24, 4, 128, 128
SSD_CONV = SSD_W + 2 * SSD_NG * SSD_N
EVEN_IN = S5_W + SSD_W + SSD_CONV + SSD_H
EVEN_PAD = 5120
RW_W, RW_H, RW_HD = 1024, 16, 64
RW_LORA = 96
RW_GATE = 256
RW_IN = 3 * RW_W + 2 * RW_LORA + RW_GATE
RW_PAD = 3584
LRU_W, LRU_B = 1024, 16
ODD_IN = RW_IN + 2 * LRU_W
ODD_PAD = RW_PAD + 2 * LRU_W
GN_EPS = 64e-5
LRU_C = 8.0
ADAM_LR, ADAM_B1, ADAM_B2, ADAM_EPS, ADAM_WD, ADAM_STEP = 0.001, 0.9, 0.999, 1e-08, 0.01, 10


def _cparams(sem=("arbitrary",)):
    return pltpu.CompilerParams(dimension_semantics=sem, vmem_limit_bytes=VMEM_LIMIT)


def _dot16(a, b, dims=(((1,), (0,)), ((), ()))):
    return lax.dot_general(a.astype(BF16), b.astype(BF16), dims, preferred_element_type=F32)


NN = (((1,), (0,)), ((), ()))
NT = (((1,), (1,)), ((), ()))
TN = (((0,), (0,)), ((), ()))


def _split3(x):
    top = lambda z: lax.bitcast_convert_type(lax.bitcast_convert_type(z, jnp.int32) & jnp.int32(-65536), F32)
    hi = top(x)
    rest = x - hi
    mid = top(rest)
    return hi.astype(BF16), mid.astype(BF16), (rest - mid).astype(BF16)


def _sel_raw(a, b, dims, data):
    parts = _split3((a, b)[data].astype(F32))
    mask = (a, b)[1 - data].astype(BF16)
    acc = None
    for part in reversed(parts):
        ops = (part, mask) if data == 0 else (mask, part)
        term = lax.dot_general(*ops, dims, preferred_element_type=F32)
        acc = term if acc is None else acc + term
    return acc


_SEL_BACK = {(NN, 0): ("g", "m", NT, 0), (NT, 0): ("g", "m", NN, 0), (TN, 0): ("m", "g", NT, 1),
             (NN, 1): ("m", "g", TN, 1), (NT, 1): ("g", "m", TN, 0), (TN, 1): ("m", "g", NN, 1)}


@functools.partial(jax.custom_vjp, nondiff_argnums=(2, 3))
def _sel_dot(a, b, dims, data):
    return _sel_raw(a, b, dims, data)


def _sel_dot_fwd(a, b, dims, data):
    return _sel_raw(a, b, dims, data), (a, b)


def _sel_dot_bwd(dims, data, res, g):
    mask = res[1 - data]
    left, right, dims2, data2 = _SEL_BACK[(dims, data)]
    grad = _sel_raw(g if left == "g" else mask, g if right == "g" else mask, dims2, data2)
    zero = jnp.zeros_like(mask)
    return (grad, zero) if data == 0 else (zero, grad)


_sel_dot.defvjp(_sel_dot_fwd, _sel_dot_bwd)


def _tile(dim, target):
    if dim <= target:
        return dim
    t = target - target % 128
    while t > 128 and dim % t:
        t -= 128
    assert dim % t == 0, (dim, target)
    return t


def _mm(a, b, *, ta=False, tb=False, add=None, out_dtype=F32, tm=1024, tn=1024, tk=1024, name,
        epilogue=None, extra=(), out_dtypes=None):
    layer = None
    if isinstance(b, tuple):
        b, layer = b
    m, k = (a.shape[1], a.shape[0]) if ta else a.shape
    n = b.shape[-2] if tb else b.shape[-1]
    assert (b.shape[-1] if tb else b.shape[-2]) == k, (a.shape, b.shape, ta, tb)
    ins = [a, b] + ([add] if add is not None else []) + list(extra)
    out_dtypes = out_dtypes or [out_dtype]
    n_in, n_out = len(ins), len(out_dtypes)
    tm, tn = _tile(m, tm), _tile(n, tn)
    tiles = 2 * tm * tn * sum(jnp.dtype(x.dtype).itemsize for x in ins[2:]) + 2 * tm * tn * sum(
        jnp.dtype(dt).itemsize for dt in out_dtypes) + 4 * tm * tn
    per_k = 2 * (tm * a.dtype.itemsize + tn * b.dtype.itemsize)
    tk = _tile(k, max(tk, min(2048, (MM_VMEM_BUDGET - tiles) // per_k // 128 * 128)))
    nk = k // tk
    dims = (((0 if ta else 1,), (1 if tb else 0,)), ((), ()))

    def finish(acc, refs):
        res = epilogue(acc, *[r[...] for r in refs[n_in - len(extra):n_in]]) if epilogue else (acc,)
        for o_ref, val in zip(refs[n_in:n_in + n_out], res):
            o_ref[...] = val.astype(o_ref.dtype)

    def body(*refs):
        a_ref, b_ref, acc_ref = refs[0], refs[1], refs[-1]
        if nk == 1:
            acc = _dot16(a_ref[...], b_ref[...], dims)
            finish(acc + refs[2][...].astype(F32) if add is not None else acc, refs)
            return
        kk = pl.program_id(2)

        @pl.when(kk == 0)
        def _():
            acc_ref[...] = refs[2][...].astype(F32) if add is not None else jnp.zeros_like(acc_ref)

        acc_ref[...] += _dot16(a_ref[...], b_ref[...], dims)

        @pl.when(kk == nk - 1)
        def _():
            finish(acc_ref[...], refs)

    a_spec = pl.BlockSpec((tk, tm), lambda i, j, q: (q, i)) if ta else pl.BlockSpec((tm, tk), lambda i, j, q: (i, q))
    b_spec = pl.BlockSpec((tn, tk), lambda i, j, q: (j, q)) if tb else pl.BlockSpec((tk, tn), lambda i, j, q: (q, j))
    if layer is not None:
        b_spec = (pl.BlockSpec((None, tn, tk), lambda i, j, q: (layer, j, q)) if tb
                  else pl.BlockSpec((None, tk, tn), lambda i, j, q: (layer, q, j)))
    o_spec = pl.BlockSpec((tm, tn), lambda i, j, q: (i, j))
    outs = pl.pallas_call(
        body,
        grid=(m // tm, n // tn, nk),
        in_specs=[a_spec, b_spec] + [o_spec] * (n_in - 2),
        out_specs=[o_spec] * n_out,
        out_shape=[SDS((m, n), dt) for dt in out_dtypes],
        scratch_shapes=[pltpu.VMEM((tm, tn) if nk > 1 else (8, 128), F32)],
        compiler_params=_cparams(("parallel", "parallel", "arbitrary")),
        name=name,
    )(*ins)
    return outs if epilogue else outs[0]


def _mm_grad(x, dy, *, layer, cols_cut, shard, prev, name):
    t = x.shape[0]
    r, c = shard
    tm, tn = _tile(r, 1024), _tile(c, 1024)
    per_k = 2 * (tm * x.dtype.itemsize + tn * dy.dtype.itemsize)
    tk = _tile(t, max(1024, min(2048, (MM_VMEM_BUDGET - 12 * tm * tn) // per_k // 128 * 128)))
    nk = t // tk
    if cols_cut:
        assert x.shape[1] == r and dy.shape[1] == 4 * c
        per = c // tn
        omap = lambda i, j, q: (j // per, layer, i, j % per)
    else:
        assert x.shape[1] == 4 * r and dy.shape[1] == c
        per = r // tm
        omap = lambda i, j, q: (i // per, layer, i % per, j)

    def body(*refs):
        x_ref, dy_ref = refs[:2]
        o_ref, acc_ref = refs[-2:]
        kk = pl.program_id(2)

        @pl.when(kk == 0)
        def _():
            acc_ref[...] = jnp.zeros_like(acc_ref)

        acc_ref[...] += _dot16(x_ref[...], dy_ref[...], TN)

        @pl.when(kk == nk - 1)
        def _():
            o_ref[...] = acc_ref[...]

    return pl.pallas_call(
        body,
        grid=(x.shape[1] // tm, dy.shape[1] // tn, nk),
        in_specs=[pl.BlockSpec((tk, tm), lambda i, j, q: (q, i)), pl.BlockSpec((tk, tn), lambda i, j, q: (q, j))]
        + ([ANY] if prev is not None else []),
        out_specs=pl.BlockSpec((None, None, tm, tn), omap),
        out_shape=SDS((4, 2, r, c), F32),
        scratch_shapes=[pltpu.VMEM((tm, tn), F32)],
        input_output_aliases={2: 0} if prev is not None else {},
        compiler_params=_cparams(("parallel", "parallel", "arbitrary")),
        name=name,
    )(x, dy, *([prev] if prev is not None else []))


def _single(fn, consts, *, name):
    outs = jax.eval_shape(fn, *[SDS(c.shape, F32) for c in consts])
    n_in = len(consts)

    def body(*refs):
        res = fn(*[r[...] for r in refs[:n_in]])
        for o_ref, v in zip(refs[n_in:], res):
            o_ref[...] = v

    return pl.pallas_call(body, out_shape=[SDS(o.shape, F32) for o in outs],
                          compiler_params=pltpu.CompilerParams(vmem_limit_bytes=VMEM_LIMIT), name=name)(*consts)


def _single_vjp(fn, consts, cots, *, name):
    n_in = len(consts)

    def body(*refs):
        _, pull = jax.vjp(fn, *[r[...] for r in refs[:n_in]])
        grads = pull(tuple(r[...] for r in refs[n_in:n_in + len(cots)]))
        for o_ref, v in zip(refs[n_in + len(cots):], grads):
            o_ref[...] = v

    return pl.pallas_call(body, out_shape=[SDS(c.shape, F32) for c in consts],
                          compiler_params=pltpu.CompilerParams(vmem_limit_bytes=VMEM_LIMIT), name=name)(*consts, *cots)


def _full_spec(shape):
    nd = len(shape)
    return pl.BlockSpec(shape, lambda i, _n=nd: (0,) * _n)


def _stage_shapes(fn, rows, consts, tb, pos):
    rs = [SDS((tb, r.shape[1]), F32) for r in rows]
    cs = [SDS(c.shape, F32) for c in consts]
    f = (lambda *a: fn(jnp.int32(0), *a)) if pos else fn
    return jax.eval_shape(f, *rs, *cs)


def _stage(fn, rows, consts, *, tb, name, out_dtypes, n_acc=0, pos=False):
    t = rows[0].shape[0]
    assert t % tb == 0
    outs = _stage_shapes(fn, rows, consts, tb, pos)
    n_out = len(outs)
    n_row = n_out - n_acc
    n_in = len(rows) + len(consts)

    def body(*refs):
        i = pl.program_id(0)
        vals = [r[...].astype(F32) for r in refs[:n_in]]
        res = fn(i * tb, *vals) if pos else fn(*vals)
        out_refs = refs[n_in:]
        for q in range(n_row):
            out_refs[q][...] = res[q].astype(out_refs[q].dtype)
        for q in range(n_row, n_out):
            @pl.when(i == 0)
            def _(q=q):
                out_refs[q][...] = jnp.zeros_like(out_refs[q])

            out_refs[q][...] += res[q]

    in_specs = [pl.BlockSpec((tb, r.shape[1]), lambda i: (i, 0)) for r in rows] + [_full_spec(c.shape) for c in consts]
    out_specs = [pl.BlockSpec((tb, o.shape[1]), lambda i: (i, 0)) for o in outs[:n_row]] + [_full_spec(o.shape) for o in outs[n_row:]]
    out_shape = [SDS((t, o.shape[1]), dt) for o, dt in zip(outs[:n_row], out_dtypes)] + [SDS(o.shape, F32) for o in outs[n_row:]]
    return pl.pallas_call(
        body, grid=(t // tb,), in_specs=in_specs, out_specs=out_specs, out_shape=out_shape,
        compiler_params=_cparams(), name=name,
    )(*rows, *consts)


def _stage_vjp(fn, rows, consts, cots, *, tb, name, drow, dconst, drow_dtypes=None, acc_cots=(), pos=False):
    t = rows[0].shape[0]
    assert t % tb == 0
    n_rows, n_consts, n_cots, n_acc = len(rows), len(consts), len(cots), len(acc_cots)
    n_in = n_rows + n_consts + n_cots + n_acc
    drow_dtypes = drow_dtypes or [F32] * len(drow)

    def body(*refs):
        i = pl.program_id(0)
        vals = [r[...].astype(F32) for r in refs[:n_in]]
        rv, cv = vals[:n_rows], vals[n_rows:n_rows + n_consts]
        ct = tuple(vals[n_rows + n_consts:])

        def f(*dargs):
            r2, c2 = list(rv), list(cv)
            for q, idx in enumerate(drow):
                r2[idx] = dargs[q]
            for q, idx in enumerate(dconst):
                c2[idx] = dargs[len(drow) + q]
            return fn(i * tb, *r2, *c2) if pos else fn(*r2, *c2)

        _, pull = jax.vjp(f, *[rv[q] for q in drow], *[cv[q] for q in dconst])
        grads = pull(ct)
        out_refs = refs[n_in:]
        for q in range(len(drow)):
            out_refs[q][...] = grads[q].astype(out_refs[q].dtype)
        for q in range(len(drow), len(drow) + len(dconst)):
            @pl.when(i == 0)
            def _(q=q):
                out_refs[q][...] = jnp.zeros_like(out_refs[q])

            out_refs[q][...] += grads[q]

    in_specs = ([pl.BlockSpec((tb, r.shape[1]), lambda i: (i, 0)) for r in rows] + [_full_spec(c.shape) for c in consts]
                + [pl.BlockSpec((tb, c.shape[1]), lambda i: (i, 0)) for c in cots] + [_full_spec(c.shape) for c in acc_cots])
    out_specs = ([pl.BlockSpec((tb, rows[q].shape[1]), lambda i: (i, 0)) for q in drow]
                 + [_full_spec(consts[q].shape) for q in dconst])
    out_shape = ([SDS(rows[q].shape, dt) for q, dt in zip(drow, drow_dtypes)]
                 + [SDS(consts[q].shape, F32) for q in dconst])
    return pl.pallas_call(
        body, grid=(t // tb,), in_specs=in_specs, out_specs=out_specs, out_shape=out_shape,
        compiler_params=_cparams(), name=name,
    )(*rows, *consts, *cots, *acc_cots)


def _conv_fwd(x, w, b, *, tb, name):
    t, c = x.shape
    r8 = tb // 8

    def body(x_ref, p_ref, w_ref, b_ref, o_ref):
        i = pl.program_id(0)
        x_ = x_ref[...]
        p_ = jnp.where(i > 0, p_ref[...], 0.0)
        w_ = w_ref[...]
        row = lax.broadcasted_iota(jnp.int32, x_.shape, 0)
        row8 = lax.broadcasted_iota(jnp.int32, p_.shape, 0)
        acc = x_ * w_[3:4, :] + b_ref[...]
        head = jnp.zeros_like(p_)
        for j in (1, 2, 3):
            wj = w_[3 - j:4 - j, :]
            acc += jnp.where(row >= j, pltpu.roll(x_, j, 0), 0.0) * wj
            head += jnp.where(row8 < j, pltpu.roll(p_, j, 0), 0.0) * wj
        o_ref[...] = acc
        o_ref[0:8, :] += head

    return pl.pallas_call(
        body, grid=(t // tb,),
        in_specs=[pl.BlockSpec((tb, c), lambda i: (i, 0)),
                  pl.BlockSpec((8, c), lambda i: (jnp.maximum(i * r8 - 1, 0), 0)),
                  _full_spec(w.shape), _full_spec(b.shape)],
        out_specs=pl.BlockSpec((tb, c), lambda i: (i, 0)),
        out_shape=SDS((t, c), F32), compiler_params=_cparams(), name=name,
    )(x, x, w, b)


def _conv_bwd(x, w, dy, *, tb, name):
    t, c = x.shape
    r8 = tb // 8
    nb = t // tb

    def body(x_ref, p_ref, w_ref, g_ref, n_ref, dx_ref, dw_ref, db_ref):
        i = pl.program_id(0)
        x_ = x_ref[...]
        p_ = jnp.where(i > 0, p_ref[...], 0.0)
        g_ = g_ref[...]
        n_ = jnp.where(i < nb - 1, n_ref[...], 0.0)
        w_ = w_ref[...]
        row = lax.broadcasted_iota(jnp.int32, x_.shape, 0)
        row8 = lax.broadcasted_iota(jnp.int32, p_.shape, 0)
        g8 = g_[0:8, :]
        dx = g_ * w_[3:4, :]
        tail = jnp.zeros_like(n_)
        dws = [jnp.sum(g_ * x_, axis=0, keepdims=True)]
        for j in (1, 2, 3):
            wj = w_[3 - j:4 - j, :]
            dx += jnp.where(row < tb - j, pltpu.roll(g_, tb - j, 0), 0.0) * wj
            tail += jnp.where(row8 >= 8 - j, pltpu.roll(n_, 8 - j, 0), 0.0) * wj
            xs = jnp.where(row >= j, pltpu.roll(x_, j, 0), 0.0)
            ps = jnp.where(row8 < j, pltpu.roll(p_, j, 0), 0.0)
            dws.append(jnp.sum(g_ * xs, axis=0, keepdims=True) + jnp.sum(g8 * ps, axis=0, keepdims=True))
        dx_ref[...] = dx
        dx_ref[tb - 8:tb, :] += tail

        @pl.when(i == 0)
        def _():
            dw_ref[...] = jnp.zeros_like(dw_ref)
            db_ref[...] = jnp.zeros_like(db_ref)

        for j in range(4):
            dw_ref[3 - j:4 - j, :] += dws[j]
        db_ref[...] += jnp.sum(g_, axis=0, keepdims=True)

    return pl.pallas_call(
        body, grid=(nb,),
        in_specs=[pl.BlockSpec((tb, c), lambda i: (i, 0)),
                  pl.BlockSpec((8, c), lambda i: (jnp.maximum(i * r8 - 1, 0), 0)),
                  _full_spec(w.shape),
                  pl.BlockSpec((tb, c), lambda i: (i, 0)),
                  pl.BlockSpec((8, c), lambda i: (jnp.minimum((i + 1) * r8, t // 8 - 1), 0))],
        out_specs=[pl.BlockSpec((tb, c), lambda i: (i, 0)), _full_spec((8, c)), _full_spec((1, c))],
        out_shape=[SDS((t, c), F32), SDS((8, c), F32), SDS((1, c), F32)],
        compiler_params=_cparams(), name=name,
    )(x, x, w, dy, dy)


def _lru_scan_fwd(a, b, *, tb, name):
    t, c = a.shape

    def body(a_ref, b_ref, h_ref, st_ref):
        @pl.when(pl.program_id(0) == 0)
        def _():
            st_ref[...] = jnp.zeros_like(st_ref)

        def step(s, h):
            h = a_ref[pl.ds(s, 1), :] * h + b_ref[pl.ds(s, 1), :]
            h_ref[pl.ds(s, 1), :] = h
            return h

        st_ref[...] = lax.fori_loop(0, tb, step, st_ref[...], unroll=8)

    blk = pl.BlockSpec((tb, c), lambda i: (i, 0))
    return pl.pallas_call(
        body, grid=(t // tb,), in_specs=[blk, blk], out_specs=blk, out_shape=SDS((t, c), F32),
        scratch_shapes=[pltpu.VMEM((1, c), F32)], compiler_params=_cparams(), name=name,
    )(a, b)


def _lru_scan_bwd(a, h, dh, *, tb, name):
    t, c = a.shape
    nb = t // tb
    r8 = tb // 8

    def body(a_ref, h_ref, p_ref, g_ref, da_ref, db_ref, st_ref):
        i = pl.program_id(0)

        @pl.when(i == 0)
        def _():
            st_ref[...] = jnp.zeros_like(st_ref)

        hprev0 = jnp.where(i < nb - 1, p_ref[7:8, :], 0.0)

        def step(q, carry):
            s = tb - 1 - q
            g = g_ref[pl.ds(s, 1), :] + carry
            hp = h_ref[pl.ds(jnp.maximum(s - 1, 0), 1), :]
            hp = jnp.where(s > 0, hp, hprev0)
            db_ref[pl.ds(s, 1), :] = g
            da_ref[pl.ds(s, 1), :] = g * hp
            return a_ref[pl.ds(s, 1), :] * g

        st_ref[...] = lax.fori_loop(0, tb, step, st_ref[...], unroll=8)

    rev = pl.BlockSpec((tb, c), lambda i: (nb - 1 - i, 0))
    prev = pl.BlockSpec((8, c), lambda i: (jnp.maximum((nb - 1 - i) * r8 - 1, 0), 0))
    return pl.pallas_call(
        body, grid=(nb,), in_specs=[rev, rev, prev, rev], out_specs=[rev, rev],
        out_shape=[SDS((t, c), F32), SDS((t, c), F32)],
        scratch_shapes=[pltpu.VMEM((1, c), F32)], compiler_params=_cparams(), name=name,
    )(a, h, h, dh)


def _s5_scan_fwd(ar, ai, br, bi, *, tb, name):
    t, c = br.shape

    def body(ar_ref, ai_ref, br_ref, bi_ref, xr_ref, xi_ref, sr_ref, si_ref):
        @pl.when(pl.program_id(0) == 0)
        def _():
            sr_ref[...] = jnp.zeros_like(sr_ref)
            si_ref[...] = jnp.zeros_like(si_ref)

        ar_, ai_ = ar_ref[...], ai_ref[...]

        def step(s, carry):
            xr, xi = carry
            nr = ar_ * xr - ai_ * xi + br_ref[pl.ds(s, 1), :]
            ni = ar_ * xi + ai_ * xr + bi_ref[pl.ds(s, 1), :]
            xr_ref[pl.ds(s, 1), :] = nr
            xi_ref[pl.ds(s, 1), :] = ni
            return nr, ni

        xr, xi = lax.fori_loop(0, tb, step, (sr_ref[...], si_ref[...]), unroll=8)
        sr_ref[...] = xr
        si_ref[...] = xi

    blk = pl.BlockSpec((tb, c), lambda i: (i, 0))
    one = _full_spec((1, c))
    return pl.pallas_call(
        body, grid=(t // tb,), in_specs=[one, one, blk, blk], out_specs=[blk, blk],
        out_shape=[SDS((t, c), F32), SDS((t, c), F32)],
        scratch_shapes=[pltpu.VMEM((1, c), F32), pltpu.VMEM((1, c), F32)], compiler_params=_cparams(), name=name,
    )(ar, ai, br, bi)


def _s5_scan_bwd(ar, ai, xr, xi, dxr, dxi, *, tb, name):
    t, c = xr.shape
    nb = t // tb
    r8 = tb // 8

    def body(ar_ref, ai_ref, xr_ref, xi_ref, pr_ref, pi_ref, gr_ref, gi_ref,
             dbr_ref, dbi_ref, dar_ref, dai_ref, cr_ref, ci_ref):
        i = pl.program_id(0)

        @pl.when(i == 0)
        def _():
            cr_ref[...] = jnp.zeros_like(cr_ref)
            ci_ref[...] = jnp.zeros_like(ci_ref)
            dar_ref[...] = jnp.zeros_like(dar_ref)
            dai_ref[...] = jnp.zeros_like(dai_ref)

        ar_, ai_ = ar_ref[...], ai_ref[...]
        first = i == nb - 1
        pr0 = jnp.where(first, 0.0, pr_ref[7:8, :])
        pi0 = jnp.where(first, 0.0, pi_ref[7:8, :])

        def step(q, carry):
            cr, ci, dar, dai = carry
            s = tb - 1 - q
            gr = gr_ref[pl.ds(s, 1), :] + cr
            gi = gi_ref[pl.ds(s, 1), :] + ci
            sp = jnp.maximum(s - 1, 0)
            xpr = jnp.where(s > 0, xr_ref[pl.ds(sp, 1), :], pr0)
            xpi = jnp.where(s > 0, xi_ref[pl.ds(sp, 1), :], pi0)
            dbr_ref[pl.ds(s, 1), :] = gr
            dbi_ref[pl.ds(s, 1), :] = gi
            dar = dar + gr * xpr + gi * xpi
            dai = dai - gr * xpi + gi * xpr
            return ar_ * gr + ai_ * gi, ar_ * gi - ai_ * gr, dar, dai

        cr, ci, dar, dai = lax.fori_loop(0, tb, step, (cr_ref[...], ci_ref[...], dar_ref[...], dai_ref[...]), unroll=8)
        cr_ref[...] = cr
        ci_ref[...] = ci
        dar_ref[...] = dar
        dai_ref[...] = dai

    rev = pl.BlockSpec((tb, c), lambda i: (nb - 1 - i, 0))
    prev = pl.BlockSpec((8, c), lambda i: (jnp.maximum((nb - 1 - i) * r8 - 1, 0), 0))
    one = _full_spec((1, c))
    return pl.pallas_call(
        body, grid=(nb,), in_specs=[one, one, rev, rev, prev, prev, rev, rev], out_specs=[rev, rev, one, one],
        out_shape=[SDS((t, c), F32), SDS((t, c), F32), SDS((1, c), F32), SDS((1, c), F32)],
        scratch_shapes=[pltpu.VMEM((1, c), F32), pltpu.VMEM((1, c), F32)], compiler_params=_cparams(), name=name,
    )(ar, ai, xr, xi, xr, xi, dxr, dxi)


RW_PAIRS = RW_H // 2


def _pair_consts():
    sub = lax.broadcasted_iota(jnp.int32, (64, 128), 0)
    lane = lax.broadcasted_iota(jnp.int32, (64, 128), 1)
    eye2 = ((lane & 63) == sub).astype(F32)
    r2 = lax.broadcasted_iota(jnp.int32, (128, 128), 0)
    c2 = lax.broadcasted_iota(jnp.int32, (128, 128), 1)
    bsel = ((r2 >> 6) == (c2 >> 6)).astype(BF16)
    return eye2, bsel


def _segsum(x, bsel):
    rows = x.shape[0]
    bits = lax.bitcast_convert_type(x, jnp.int32)
    hi = lax.bitcast_convert_type(bits & jnp.int32(-65536), F32)
    both = jnp.concatenate([hi.astype(BF16), (x - hi).astype(BF16)], axis=0)
    res = jnp.dot(both, bsel, preferred_element_type=F32)
    return res[:rows] + res[rows:]


def _bc(x8):
    return jnp.stack([jnp.broadcast_to(x8[q:q + 1, :], (64, 128)) for q in range(RW_PAIRS)])


def _seg3(x3, bsel):
    return _segsum(x3.reshape(RW_PAIRS * 64, 128), bsel).reshape(RW_PAIRS, 64, 128)


def _seg3_lanes(x3):
    first = lax.broadcasted_iota(jnp.int32, x3.shape, 2) < 64
    lo = jnp.sum(jnp.where(first, x3, 0.0), axis=-1, keepdims=True)
    hi = jnp.sum(jnp.where(first, 0.0, x3), axis=-1, keepdims=True)
    return jnp.where(first, lo, hi)


def _rwkv_scan_fwd(r, w, k, v, kk, a, *, lc, name):
    t = r.shape[0]
    nc = t // lc

    def body(r_ref, w_ref, k_ref, v_ref, kk_ref, a_ref, y_ref, ck_ref, st_ref):
        @pl.when(pl.program_id(0) == 0)
        def _():
            st_ref[...] = jnp.zeros_like(st_ref)

        ck_ref[0] = st_ref[...]
        eye2, bsel = _pair_consts()
        column = lambda ref, s: _seg3(eye2[None] * _bc(ref[s]), bsel)
        read = lambda st, s: jnp.sum(eye2[None] * _seg3(st * _bc(r_ref[s]), bsel), axis=1)

        def step(s, carry):
            st, vb = carry
            kk8 = kk_ref[s]
            sa = -_seg3_lanes(st * _bc(kk8))
            vb_next = column(v_ref, jnp.minimum(s + 1, lc - 1))
            before = jnp.maximum(s - 1, 0)
            y_ref[before] = read(st, before)
            return st * _bc(w_ref[s]) + sa * _bc(kk8 * a_ref[s]) + vb * _bc(k_ref[s]), vb_next

        st, _ = lax.fori_loop(0, lc, step, (st_ref[...], column(v_ref, 0)))
        y_ref[lc - 1] = read(st, lc - 1)
        st_ref[...] = st

    blk = pl.BlockSpec((lc, RW_PAIRS, 128), lambda i: (i, 0, 0))
    return pl.pallas_call(
        body, grid=(nc,), in_specs=[blk] * 6,
        out_specs=[blk, pl.BlockSpec((1, RW_PAIRS, 64, 128), lambda i: (i, 0, 0, 0))],
        out_shape=[SDS((t, RW_PAIRS, 128), F32), SDS((nc, RW_PAIRS, 64, 128), F32)],
        scratch_shapes=[pltpu.VMEM((RW_PAIRS, 64, 128), F32)],
        compiler_params=_cparams(), name=name,
    )(r, w, k, v, kk, a)


def _rwkv_scan_bwd(r, w, k, v, kk, a, ck, dy, *, lc, name):
    t = r.shape[0]
    nc = t // lc

    def body(r_ref, w_ref, k_ref, v_ref, kk_ref, a_ref, ck_ref, dy_ref,
             dr_ref, dw_ref, dk_ref, dv_ref, dkk_ref, da_ref,
             ds_ref, vb_ref, dyb_ref, hist_ref, sa_ref):
        @pl.when(pl.program_id(0) == 0)
        def _():
            ds_ref[...] = jnp.zeros_like(ds_ref)

        eye2, bsel = _pair_consts()
        column = lambda ref, s: _seg3(eye2[None] * _bc(ref[s]), bsel)
        hist_ref[0] = ck_ref[0]

        def fwd(s, carry):
            st, vb = carry
            kk8 = kk_ref[s]
            sa = -_seg3_lanes(st * _bc(kk8))
            vb_next = column(v_ref, jnp.minimum(s + 1, lc - 1))
            dyb_ref[s] = column(dy_ref, s)
            vb_ref[s] = vb
            sa_ref[s] = sa
            st = st * _bc(w_ref[s]) + sa * _bc(kk8 * a_ref[s]) + vb * _bc(k_ref[s])
            hist_ref[s + 1] = st
            return st, vb_next

        lax.fori_loop(0, lc, fwd, (ck_ref[0], column(v_ref, 0)))

        def grads(s, d_s, dsa):
            s_prev, s_cur = hist_ref[s], hist_ref[s + 1]
            col = lambda z: jnp.sum(z, axis=1)
            db = col(d_s * sa_ref[s])
            dr_ref[s] = col(s_cur * dyb_ref[s])
            dw_ref[s] = col(d_s * s_prev)
            dv_ref[s] = col(eye2[None] * _seg3(d_s * _bc(k_ref[s]), bsel))
            dk_ref[s] = col(d_s * vb_ref[s])
            dkk_ref[s] = db * a_ref[s] - col(s_prev * dsa)
            da_ref[s] = db * kk_ref[s]

        def back(j, carry):
            ds, d_after, dsa_after = carry
            s = lc - 1 - j
            kk8 = kk_ref[s]
            d_s = ds + dyb_ref[s] * _bc(r_ref[s])
            dsa = _seg3_lanes(d_s * _bc(kk8 * a_ref[s]))
            grads(jnp.minimum(s + 1, lc - 1), d_after, dsa_after)
            return d_s * _bc(w_ref[s]) - dsa * _bc(kk8), d_s, dsa

        zero = jnp.zeros((RW_PAIRS, 64, 128), F32)
        ds, d_first, dsa_first = lax.fori_loop(0, lc, back, (ds_ref[...], zero, zero))
        grads(0, d_first, dsa_first)
        ds_ref[...] = ds

    rev = pl.BlockSpec((lc, RW_PAIRS, 128), lambda i: (nc - 1 - i, 0, 0))
    big = lambda n: pltpu.VMEM((n, RW_PAIRS, 64, 128), F32)
    return pl.pallas_call(
        body, grid=(nc,),
        in_specs=[rev] * 6 + [pl.BlockSpec((1, RW_PAIRS, 64, 128), lambda i: (nc - 1 - i, 0, 0, 0)), rev],
        out_specs=[rev] * 6, out_shape=[SDS((t, RW_PAIRS, 128), F32)] * 6,
        scratch_shapes=[pltpu.VMEM((RW_PAIRS, 64, 128), F32), big(lc), big(lc), big(lc + 1), big(lc)],
        compiler_params=_cparams(), name=name,
    )(r, w, k, v, kk, a, ck, dy)


SSD_PAIRS = SSD_H // 2


def _ssd_chunk(states, xdt, da, bm, cm):
    ln = SSD_L
    row = lax.broadcasted_iota(jnp.int32, (ln, ln), 0)
    col = lax.broadcasted_iota(jnp.int32, (ln, ln), 1)
    causal = row >= col
    acum = _sel_dot(causal.astype(F32), da, NN, 1)
    acum_t = _sel_dot(da, (row <= col).astype(F32), TN, 0)
    sub = lax.broadcasted_iota(jnp.int32, (128, 128), 0)
    lane = lax.broadcasted_iota(jnp.int32, (128, 128), 1)
    ys, new_states = [], []
    for q in range(SSD_PAIRS):
        g = q // (SSD_PAIRS // SSD_NG)
        bg = bm[:, g * SSD_N:(g + 1) * SSD_N]
        cg = cm[:, g * SSD_N:(g + 1) * SSD_N]
        xq = xdt[:, q * 128:(q + 1) * 128]
        scores = _dot16(cg, bg, NT)
        aexp = _sel_dot(acum, (sub == 2 * q + (lane >> 6)).astype(F32), NN, 0)
        tot = aexp[ln - 1:ln, :]
        yh = []
        for h in (2 * q, 2 * q + 1):
            seg = _sel_dot(acum, (sub == h).astype(F32), NN, 0) - acum_t[h:h + 1, :]
            yh.append(_dot16(scores * jnp.exp(jnp.where(causal, seg, -1e30)), xq))
        y = jnp.where(lane < 64, yh[0], yh[1]) + _dot16(cg, states[q]) * jnp.exp(aexp)
        new = _dot16(bg, xq * jnp.exp(tot - aexp), TN)
        ys.append(y)
        new_states.append(states[q] * jnp.exp(tot) + new)
    return jnp.concatenate(ys, axis=1), new_states


def _ssd_fwd(xdt, da, bm, cm, *, name):
    t = xdt.shape[0]
    nc = t // SSD_L

    def body(x_ref, a_ref, b_ref, c_ref, y_ref, ck_ref, st_ref):
        @pl.when(pl.program_id(0) == 0)
        def _():
            st_ref[...] = jnp.zeros_like(st_ref)

        ck_ref[0] = st_ref[...]
        y, new = _ssd_chunk([st_ref[q] for q in range(SSD_PAIRS)], x_ref[...], a_ref[...], b_ref[...], c_ref[...])
        y_ref[...] = y
        for q in range(SSD_PAIRS):
            st_ref[q] = new[q]

    blk = lambda wd: pl.BlockSpec((SSD_L, wd), lambda i: (i, 0))
    return pl.pallas_call(
        body, grid=(nc,), in_specs=[blk(SSD_W), blk(128), blk(512), blk(512)],
        out_specs=[blk(SSD_W), pl.BlockSpec((1, SSD_PAIRS, 128, 128), lambda i: (i, 0, 0, 0))],
        out_shape=[SDS((t, SSD_W), F32), SDS((nc, SSD_PAIRS, 128, 128), F32)],
        scratch_shapes=[pltpu.VMEM((SSD_PAIRS, 128, 128), F32)], compiler_params=_cparams(), name=name,
    )(xdt, da, bm, cm)


def _ssd_bwd(xdt, da, bm, cm, ck, dy, *, name):
    t = xdt.shape[0]
    nc = t // SSD_L

    def body(x_ref, a_ref, b_ref, c_ref, ck_ref, dy_ref, dx_ref, dda_ref, db_ref, dc_ref, ds_ref):
        @pl.when(pl.program_id(0) == 0)
        def _():
            ds_ref[...] = jnp.zeros_like(ds_ref)

        _, pull = jax.vjp(_ssd_chunk, [ck_ref[0, q] for q in range(SSD_PAIRS)], x_ref[...], a_ref[...], b_ref[...], c_ref[...])
        dst, dx, dda, db, dc = pull((dy_ref[...], [ds_ref[q] for q in range(SSD_PAIRS)]))
        dx_ref[...] = dx
        dda_ref[...] = dda
        db_ref[...] = db
        dc_ref[...] = dc
        for q in range(SSD_PAIRS):
            ds_ref[q] = dst[q]

    rev = lambda wd: pl.BlockSpec((SSD_L, wd), lambda i: (nc - 1 - i, 0))
    return pl.pallas_call(
        body, grid=(nc,),
        in_specs=[rev(SSD_W), rev(128), rev(512), rev(512),
                  pl.BlockSpec((1, SSD_PAIRS, 128, 128), lambda i: (nc - 1 - i, 0, 0, 0)), rev(SSD_W)],
        out_specs=[rev(SSD_W), rev(128), rev(512), rev(512)],
        out_shape=[SDS((t, SSD_W), F32), SDS((t, 128), F32), SDS((t, 512), F32), SDS((t, 512), F32)],
        scratch_shapes=[pltpu.VMEM((SSD_PAIRS, 128, 128), F32)], compiler_params=_cparams(), name=name,
    )(xdt, da, bm, cm, ck, dy)


def _iota(shape, dim):
    return lax.broadcasted_iota(jnp.int32, shape, dim)


def _rms(x, g):
    return x * lax.rsqrt(jnp.mean(x * x, axis=-1, keepdims=True) + EPS) * g


def _head_sel(width, shift):
    return ((_iota((width, 128), 0) >> shift) == _iota((width, 128), 1)).astype(F32)


def _head_sum(x, shift=6):
    sel = _head_sel(x.shape[1], shift)
    return _sel_dot(_sel_dot(x, sel, NN, 0), sel, NT, 0)


def _head_expand(x, width, shift=6):
    return _sel_dot(x, _head_sel(width, shift), NT, 0)


def f_norm(h, g):
    return (_rms(h, g),)


def f_norm_pass(h, g):
    return _rms(h, g), h


def f_add_norm(h, m, g):
    h1 = h + m
    return h1, _rms(h1, g)


def f_relu2(u):
    r = jnp.maximum(u, 0.0)
    return (r * r,)


def f_plgate(h2, gl, pp):
    return (h2 + jax.nn.sigmoid(gl) * pp,)


def f_loss(h, tgt, g):
    err = _rms(h, g) - tgt
    part = 0.5 * jnp.sum(jnp.mean(err * err, axis=-1, keepdims=True), axis=0, keepdims=True)
    return (jnp.broadcast_to(part, (8, 128)),)


def f_s5_prep(lam_re, lam_im, lstep, bre_t, bim_t, cre_t, cim_t):
    step = jnp.exp(_sel_dot(lstep, _head_sel(S5_N, 6), NT, 0)[0:1, :])
    mag = jnp.exp(lam_re * step)
    abar_re, abar_im = mag * jnp.cos(lam_im * step), mag * jnp.sin(lam_im * step)
    den = lam_re * lam_re + lam_im * lam_im
    nr = abar_re - 1.0
    coef_re = (nr * lam_re + abar_im * lam_im) / den
    coef_im = (abar_im * lam_re - nr * lam_im) / den
    bbar_re = coef_re * bre_t - coef_im * bim_t
    bbar_im = coef_re * bim_t + coef_im * bre_t
    rep = ((_iota((S5_W, S5_G), 0) & (S5_G - 1)) == _iota((S5_W, S5_G), 1)).astype(F32)
    blk = ((_iota((S5_W, S5_N), 0) >> 4) == (_iota((S5_W, S5_N), 1) >> 6)).astype(F32)
    blk_t = ((_iota((S5_N, S5_W), 0) >> 6) == (_iota((S5_N, S5_W), 1) >> 4)).astype(F32)
    wb_re, wb_im = _sel_dot(rep, bbar_re, NN, 1) * blk, _sel_dot(rep, bbar_im, NN, 1) * blk
    wc_re, wc_im = _sel_dot(cre_t, rep, NT, 0) * blk_t, _sel_dot(cim_t, rep, NT, 0) * blk_t
    return abar_re, abar_im, wb_re, wb_im, wc_re, wc_im


def f_s5_post(xr, xi, u, wc_re, wc_im, d_skip, glu_w, glu_b):
    y = _dot16(xr, wc_re) - _dot16(xi, wc_im) + d_skip * u
    act = jax.nn.gelu(y)
    return (act * jax.nn.sigmoid(_dot16(act, glu_w) + glu_b),)


def f_ssd_pre(xc, dtr, dt_bias, a_log):
    act = jax.nn.silu(xc)
    heads = _iota(dtr.shape, 1) < SSD_H
    dt = jnp.where(heads, jax.nn.softplus(dtr + dt_bias), 0.0)
    da = dt * (-jnp.exp(a_log))
    xdt = act[:, :SSD_W] * _head_expand(dt, SSD_W)
    return xdt, da, act[:, SSD_W:SSD_W + 512], act[:, SSD_W + 512:]


def f_ssd_pre_pass(xc, dtr, dt_bias, a_log):
    return f_ssd_pre(xc, dtr, dt_bias, a_log) + (xc,)


def f_ssd_post(y, xc, z, d_skip, norm_g):
    xs = jax.nn.silu(xc[:, :SSD_W])
    y = (y + xs * _head_expand(d_skip, SSD_W)) * jax.nn.silu(z)
    gw = SSD_W // SSD_NG
    parts = []
    for g in range(SSD_NG):
        seg = y[:, g * gw:(g + 1) * gw]
        parts.append(seg * lax.rsqrt(jnp.mean(seg * seg, axis=-1, keepdims=True) + EPS))
    return (jnp.concatenate(parts, axis=1) * norm_g,)


def f_rwkv_pre(f, w0, w_up, a0, a_up, g_up, k_k, k_a):
    r, k, v = f[:, 0:1024], f[:, 1024:2048], f[:, 2048:3072]
    wl, al, gl = f[:, 3072:3200], f[:, 3200:3328], f[:, 3328:3584]
    w = -jax.nn.softplus(-(w0 + _dot16(jnp.tanh(wl), w_up))) - 0.5
    decay = jnp.exp(-jnp.exp(w))
    a = jax.nn.sigmoid(a0 + _dot16(al, a_up))
    g = _dot16(jax.nn.sigmoid(gl), g_up)
    kk = k * k_k
    k2 = k * (1.0 + (a - 1.0) * k_a)
    kkn = kk * lax.rsqrt(jnp.maximum(_head_sum(kk * kk), 1e-24))
    return r, decay, k2, v, kkn, a, g


def f_rwkv_pre_pass(f, w0, w_up, a0, a_up, g_up, k_k, k_a):
    out = f_rwkv_pre(f, w0, w_up, a0, a_up, g_up, k_k, k_a)
    return out + (out[0], out[2], out[3])


def f_rwkv_post(y, r, k2, v, g, ln_g, ln_b, r_k):
    mean = _head_sum(y) * (1.0 / RW_HD)
    yc = y - mean
    var = _head_sum(yc * yc) * (1.0 / RW_HD)
    yn = yc * lax.rsqrt(var + GN_EPS) * ln_g + ln_b
    bonus = _head_sum(r * k2 * r_k) * v
    return ((yn + bonus) * g,)


def _neg_expm1(y):
    series = -y * (1.0 + y * (0.5 + y * (1.0 / 6.0 + y * (1.0 / 24.0 + y * (1.0 / 120.0)))))
    return jnp.where(y > -0.1, series, 1.0 - jnp.exp(y))


def f_lru_pre(t0, xc, w_a, b_a, w_x, b_x, lam):
    gate_r = jax.nn.sigmoid(_dot16(xc, w_a) + b_a)
    gate_i = jax.nn.sigmoid(_dot16(xc, w_x) + b_x)
    log_a = -LRU_C * gate_r * jax.nn.softplus(-lam)
    mult = jnp.sqrt(jnp.maximum(_neg_expm1(2.0 * log_a), 0.0))
    mult = jnp.where(_iota(xc.shape, 0) + t0 == 0, 1.0, mult)
    return jnp.exp(log_a), xc * gate_i * mult


def f_lru_post(h, gl):
    return (h * jax.nn.gelu(gl),)


TB = 256
TBH = 128
SCAN_TB = 256
RW_LC = 32


def _even_fwd(hn, w, tag):
    n = lambda s: f"{tag}_{s}"
    u = _mm(hn, w["in_u"], name=n("proj_u"))
    z = _mm(hn, w["in_z"], name=n("proj_z"))
    xbc = _mm(hn, w["in_xbc"], name=n("proj_xbc"))
    dtr = _mm(hn, w["in_dt"], name=n("proj_dt"))
    bu_re = _mm(u, w["wb_re"], name=n("s5_bu_re"))
    bu_im = _mm(u, w["wb_im"], name=n("s5_bu_im"))
    xr, xi = _s5_scan_fwd(w["abar_re"], w["abar_im"], bu_re, bu_im, tb=SCAN_TB, name=n("s5_scan"))
    s5c = [w["wc_re"], w["wc_im"], w["s5_d"], w["glu_w"], w["glu_b"]]
    (ya,) = _stage(f_s5_post, [xr, xi, u], s5c, tb=TB, name=n("s5_post"), out_dtypes=[BF16])
    xc = _conv_fwd(xbc, w["ssd_conv_w"], w["ssd_conv_b"], tb=TB, name=n("ssd_conv"))
    xdt, da, bm, cm = _stage(f_ssd_pre, [xc, dtr], [w["dt_bias"], w["a_log"]], tb=TB, name=n("ssd_pre"),
                             out_dtypes=[F32] * 4)
    y, ck = _ssd_fwd(xdt, da, bm, cm, name=n("ssd_scan"))
    (yb,) = _stage(f_ssd_post, [y, xc, z], [w["ssd_d"], w["ssd_norm"]], tb=TB, name=n("ssd_post"), out_dtypes=[BF16])
    mo = _mm(ya, w["out_a"], name=n("out_a"))
    mo = _mm(yb, w["out_b"], add=mo, name=n("out_b"))
    res = dict(u=u, z=z, xbc=xbc, dtr=dtr, xr=xr, xi=xi, ya=ya, xc=xc, xdt=xdt, da=da, bm=bm, cm=cm, y=y, ck=ck, yb=yb)
    return mo, res


def _even_bwd(dmo, hn, w, r, tag):
    n = lambda s: f"{tag}_{s}"
    g = {}
    g["out_a"] = _mm(r["ya"], dmo, ta=True, name=n("d_out_a"))
    g["out_b"] = _mm(r["yb"], dmo, ta=True, name=n("d_out_b"))
    dya = _mm(dmo, w["out_a"], tb=True, name=n("dya"))
    dyb = _mm(dmo, w["out_b"], tb=True, name=n("dyb"))
    dy, dxc1, dz, g["ssd_d"], g["ssd_norm"] = _stage_vjp(
        f_ssd_post, [r["y"], r["xc"], r["z"]], [w["ssd_d"], w["ssd_norm"]], [dyb], tb=TBH, name=n("ssd_post_b"),
        drow=[0, 1, 2], dconst=[0, 1])
    dxdt, dda, dbm, dcm = _ssd_bwd(r["xdt"], r["da"], r["bm"], r["cm"], r["ck"], dy, name=n("ssd_scan_b"))
    dxc, ddtr, g["dt_bias"], g["a_log"] = _stage_vjp(
        f_ssd_pre_pass, [r["xc"], r["dtr"]], [w["dt_bias"], w["a_log"]], [dxdt, dda, dbm, dcm, dxc1], tb=TBH,
        name=n("ssd_pre_b"), drow=[0, 1], dconst=[0, 1])
    dxbc, g["ssd_conv_w"], g["ssd_conv_b"] = _conv_bwd(r["xbc"], w["ssd_conv_w"], dxc, tb=TB, name=n("ssd_conv_b"))
    s5c = [w["wc_re"], w["wc_im"], w["s5_d"], w["glu_w"], w["glu_b"]]
    dxr, dxi, du1, g["wc_re"], g["wc_im"], g["s5_d"], g["glu_w"], g["glu_b"] = _stage_vjp(
        f_s5_post, [r["xr"], r["xi"], r["u"]], s5c, [dya], tb=TBH, name=n("s5_post_b"),
        drow=[0, 1, 2], dconst=[0, 1, 2, 3, 4])
    dbr, dbi, g["abar_re"], g["abar_im"] = _s5_scan_bwd(w["abar_re"], w["abar_im"], r["xr"], r["xi"], dxr, dxi,
                                                         tb=SCAN_TB, name=n("s5_scan_b"))
    g["wb_re"] = _mm(r["u"], dbr, ta=True, name=n("d_wb_re"))
    g["wb_im"] = _mm(r["u"], dbi, ta=True, name=n("d_wb_im"))
    du = _mm(dbr, w["wb_re"], tb=True, add=du1, name=n("du_re"))
    du = _mm(dbi, w["wb_im"], tb=True, add=du, name=n("du_im"))
    segs = (("in_u", du), ("in_z", dz), ("in_xbc", dxbc), ("in_dt", ddtr))
    dhn = None
    for key, dseg in segs:
        g[key] = _mm(hn, dseg, ta=True, name=n("d_" + key))
        dhn = _mm(dseg, w[key], tb=True, add=dhn, name=n("dhn_" + key))
    return dhn, g


def _odd_fwd(hn, w, tag):
    n = lambda s: f"{tag}_{s}"
    rw = _mm(hn, w["in_rw"], name=n("proj_rw"))
    xl = _mm(hn, w["in_xl"], name=n("proj_xl"))
    gl = _mm(hn, w["in_gl"], name=n("proj_gl"))
    f = _conv_fwd(rw, w["mix_w"], w["mix_b"], tb=TB, name=n("rwkv_shift"))
    rc = [w[k] for k in ("w0", "w_up", "a0", "a_up", "g_up", "k_k", "k_a")]
    r_, dec, k2, v, kkn, a, gate = _stage(f_rwkv_pre, [f], rc, tb=TB, name=n("rwkv_pre"), out_dtypes=[F32] * 7)
    t3 = lambda z: z.reshape(-1, RW_PAIRS, 128)
    y, ck = _rwkv_scan_fwd(t3(r_), t3(dec), t3(k2), t3(v), t3(kkn), t3(a), lc=RW_LC, name=n("rwkv_scan"))
    y = y.reshape(-1, RW_W)
    (yc,) = _stage(f_rwkv_post, [y, r_, k2, v, gate], [w["ln_g"], w["ln_b"], w["r_k"]], tb=TB, name=n("rwkv_post"),
                   out_dtypes=[BF16])
    xc = _conv_fwd(xl, w["lru_conv_w"], w["lru_conv_b"], tb=TB, name=n("lru_conv"))
    lc = [w[k] for k in ("lru_wa", "lru_b_a", "lru_wx", "lru_b_x", "lru_lam")]
    a_l, bx = _stage(f_lru_pre, [xc], lc, tb=TB, name=n("lru_pre"), out_dtypes=[F32] * 2, pos=True)
    h = _lru_scan_fwd(a_l, bx, tb=SCAN_TB, name=n("lru_scan"))
    (yd,) = _stage(f_lru_post, [h, gl], [], tb=TB, name=n("lru_post"), out_dtypes=[BF16])
    mo = _mm(yc, w["out_a"], name=n("out_a"))
    mo = _mm(yd, w["out_b"], add=mo, name=n("out_b"))
    res = dict(rw=rw, xl=xl, gl=gl, f=f, r=r_, dec=dec, k2=k2, v=v, kkn=kkn, a=a, gate=gate, y=y, ck=ck, yc=yc,
               xc=xc, a_l=a_l, h=h, yd=yd)
    return mo, res


def _odd_bwd(dmo, hn, w, r, tag):
    n = lambda s: f"{tag}_{s}"
    g = {}
    g["out_a"] = _mm(r["yc"], dmo, ta=True, name=n("d_out_a"))
    g["out_b"] = _mm(r["yd"], dmo, ta=True, name=n("d_out_b"))
    dyc = _mm(dmo, w["out_a"], tb=True, name=n("dyc"))
    dyd = _mm(dmo, w["out_b"], tb=True, name=n("dyd"))
    dh, dgl = _stage_vjp(f_lru_post, [r["h"], r["gl"]], [], [dyd], tb=TB, name=n("lru_post_b"), drow=[0, 1], dconst=[])
    da_l, dbx = _lru_scan_bwd(r["a_l"], r["h"], dh, tb=SCAN_TB, name=n("lru_scan_b"))
    lc = [w[k] for k in ("lru_wa", "lru_b_a", "lru_wx", "lru_b_x", "lru_lam")]
    dxc, g["lru_wa"], g["lru_b_a"], g["lru_wx"], g["lru_b_x"], g["lru_lam"] = _stage_vjp(
        f_lru_pre, [r["xc"]], lc, [da_l, dbx], tb=TBH, name=n("lru_pre_b"), drow=[0], dconst=[0, 1, 2, 3, 4], pos=True)
    dxl, g["lru_conv_w"], g["lru_conv_b"] = _conv_bwd(r["xl"], w["lru_conv_w"], dxc, tb=TB, name=n("lru_conv_b"))
    dy, dr1, dk1, dv1, dgate, g["ln_g"], g["ln_b"], g["r_k"] = _stage_vjp(
        f_rwkv_post, [r["y"], r["r"], r["k2"], r["v"], r["gate"]], [w["ln_g"], w["ln_b"], w["r_k"]], [dyc], tb=TBH,
        name=n("rwkv_post_b"), drow=[0, 1, 2, 3, 4], dconst=[0, 1, 2])
    t3 = lambda z: z.reshape(-1, RW_PAIRS, 128)
    dr2, ddec, dk2, dv2, dkkn, da = [z.reshape(-1, RW_W) for z in _rwkv_scan_bwd(
        t3(r["r"]), t3(r["dec"]), t3(r["k2"]), t3(r["v"]), t3(r["kkn"]), t3(r["a"]), r["ck"], t3(dy),
        lc=RW_LC, name=n("rwkv_scan_b"))]
    rc = [w[k] for k in ("w0", "w_up", "a0", "a_up", "g_up", "k_k", "k_a")]
    df, g["w0"], g["w_up"], g["a0"], g["a_up"], g["g_up"], g["k_k"], g["k_a"] = _stage_vjp(
        f_rwkv_pre_pass, [r["f"]], rc, [dr2, ddec, dk2, dv2, dkkn, da, dgate, dr1, dk1, dv1], tb=TBH,
        name=n("rwkv_pre_b"), drow=[0], dconst=[0, 1, 2, 3, 4, 5, 6])
    drw, g["mix_w"], _ = _conv_bwd(r["rw"], w["mix_w"], df, tb=TB, name=n("rwkv_shift_b"))
    segs = (("in_rw", drw), ("in_xl", dxl), ("in_gl", dgl))
    dhn = None
    for key, dseg in segs:
        g[key] = _mm(hn, dseg, ta=True, name=n("d_" + key))
        dhn = _mm(dseg, w[key], tb=True, add=dhn, name=n("dhn_" + key))
    return dhn, g


def _layer_fwd(h, p_i, w, odd, tag):
    n = lambda s: f"{tag}_{s}"
    (hn,) = _stage(f_norm, [h], [w["norm_mix"]], tb=TB, name=n("norm_mix"), out_dtypes=[BF16])
    mo, mres = (_odd_fwd if odd else _even_fwd)(hn, w, tag)
    h1, hf = _stage(f_add_norm, [h, mo], [w["norm_ffn"]], tb=TB, name=n("norm_ffn"), out_dtypes=[F32, BF16])
    u, act = _mm(hf, w["mlp_w1"], name=n("mlp_up"), epilogue=lambda acc: (acc,) + f_relu2(acc), out_dtypes=[F32, BF16])
    m2 = _mm(act, w["mlp_w2"], name=n("mlp_down"))
    h2, hp = _stage(f_add_norm, [h1, m2], [w["norm_pl"]], tb=TB, name=n("norm_pl"), out_dtypes=[F32, BF16])
    gl = _mm(hp, w["pl_gate"], name=n("pl_gate"))
    pp = _mm(p_i, w["pl_proj"], name=n("pl_proj"))
    (h3,) = _stage(f_plgate, [h2, gl, pp], [], tb=TB, name=n("pl_mix"), out_dtypes=[F32])
    res = dict(h=h, hn=hn, mo=mo, mix=mres, h1=h1, hf=hf, u=u, act=act, m2=m2, h2=h2, hp=hp, gl=gl, pp=pp)
    return h3, res


def _layer_bwd(dh3, p_i, w, r, odd, tag, stacks):
    n = lambda s: f"{tag}_{s}"
    g = {}
    wgrad = lambda key, x, dy, cols_cut, shard: _mm_grad(x, dy, layer=int(odd), cols_cut=cols_cut, shard=shard,
                                                        prev=stacks[key] if stacks else None, name=n("d_" + key))
    dh2, dgl, dpp = _stage_vjp(f_plgate, [r["h2"], r["gl"], r["pp"]], [], [dh3], tb=TB, name=n("pl_mix_b"),
                               drow=[0, 1, 2], dconst=[])
    g["pl_proj"] = wgrad("pl_proj", p_i, dpp, True, (PL_DIM, D // 4))
    g["pl_gate"] = wgrad("pl_gate", r["hp"], dgl, False, (D // 4, D))
    dhp = _mm(dgl, w["pl_gate"], tb=True, name=n("dhp"))
    dh1, dm2, g["norm_pl"] = _stage_vjp(f_add_norm, [r["h1"], r["m2"]], [w["norm_pl"]], [dh2, dhp], tb=TB,
                                        name=n("norm_pl_b"), drow=[0, 1], dconst=[0])
    g["mlp_w2"] = wgrad("mlp_w2", r["act"], dm2, False, (D_FF // 4, D))
    (du,) = _mm(dm2, w["mlp_w2"], tb=True, name=n("dact"), extra=[r["u"]], out_dtypes=[BF16],
                epilogue=lambda acc, u: (acc * (2.0 * jnp.maximum(u, 0.0)),))
    g["mlp_w1"] = wgrad("mlp_w1", r["hf"], du, True, (D, D_FF // 4))
    dhf = _mm(du, w["mlp_w1"], tb=True, name=n("dhf"))
    dh, dmo, g["norm_ffn"] = _stage_vjp(f_add_norm, [r["h"], r["mo"]], [w["norm_ffn"]], [dh1, dhf], tb=TB,
                                        name=n("norm_ffn_b"), drow=[0, 1], dconst=[0])
    dhn, gm = (_odd_bwd if odd else _even_bwd)(dmo, r["hn"], w, r["mix"], tag)
    g.update(gm)
    dh0, g["norm_mix"] = _stage_vjp(f_norm_pass, [r["h"]], [w["norm_mix"]], [dhn, dh], tb=TB, name=n("norm_mix_b"),
                                    drow=[0], dconst=[0])
    return dh0, g


def _pad_to(a, size, axis):
    pad = [(0, 0)] * a.ndim
    pad[axis] = (0, size - a.shape[axis])
    return jnp.pad(a, pad)


def _rw_pad(a):
    return jnp.concatenate([a[..., :3072], _pad_to(a[..., 3072:3168], 128, -1), _pad_to(a[..., 3168:3264], 128, -1),
                            a[..., 3264:3520]], axis=-1)


def _rw_unpad(a):
    return jnp.concatenate([a[..., :3072], a[..., 3072:3168], a[..., 3200:3296], a[..., 3328:3584]], axis=-1)


def _block_diag(w):
    nb, bs, _ = w.shape
    eye = jnp.eye(nb, dtype=w.dtype)
    return (w[:, :, None, :] * eye[:, None, :, None]).reshape(nb * bs, nb * bs)


def _diag_blocks(w):
    nb = LRU_B
    bs = w.shape[0] // nb
    return jnp.stack([w[h * bs:(h + 1) * bs, h * bs:(h + 1) * bs] for h in range(nb)])


def _s5_prep_inputs(fw):
    lstep = jnp.broadcast_to(_pad_to(fw["s5_log_step"].astype(F32), 128, 1), (8, 128))
    t16 = lambda b: jnp.transpose(b[0], (2, 0, 1)).reshape(S5_G, S5_N)
    tc = lambda c: jnp.transpose(c[0], (0, 2, 1)).reshape(S5_N, S5_G)
    return [fw["s5_lam_re"].reshape(1, S5_N), fw["s5_lam_im"].reshape(1, S5_N), lstep,
            t16(fw["s5_b_re"]), t16(fw["s5_b_im"]), tc(fw["s5_c_re"]), tc(fw["s5_c_im"])]


def _layer_weights(fw, i):
    w = {k: fw[k][i:i + 1] for k in ("norm_mix", "norm_ffn", "norm_pl")}
    for k in ("mlp_w1", "mlp_w2", "pl_proj", "pl_gate"):
        w[k] = (fw[k], i)
    return w


def _even_weights(fw, prep):
    w = _layer_weights(fw, 0)
    ein, eout = fw["e_in_proj"][0], fw["e_out_proj"][0]
    w.update(in_u=ein[:, :512], in_z=ein[:, 512:2048], in_xbc=ein[:, 2048:4608], in_dt=_pad_to(ein[:, 4608:], 128, 1),
             out_a=eout[:512], out_b=eout[512:])
    abar_re, abar_im, wb_re, wb_im, wc_re, wc_im = prep
    w.update(abar_re=abar_re, abar_im=abar_im, wb_re=wb_re, wb_im=wb_im, wc_re=wc_re.astype(BF16), wc_im=wc_im.astype(BF16),
             s5_d=fw["s5_d"], glu_w=fw["s5_glu_w"][0], glu_b=fw["s5_glu_b"],
             ssd_conv_w=_pad_to(fw["ssd_conv_w"][0], 8, 0), ssd_conv_b=fw["ssd_conv_b"],
             dt_bias=_pad_to(fw["ssd_dt_bias"], 128, 1), a_log=_pad_to(fw["ssd_a_log"], 128, 1),
             ssd_d=_pad_to(fw["ssd_d"], 128, 1), ssd_norm=fw["ssd_norm"])
    return w


def _odd_weights(fw):
    w = _layer_weights(fw, 1)
    oin, oout = fw["o_in_proj"][0], fw["o_out_proj"][0]
    mu = _rw_pad(fw["rwkv_mu"])
    zero = jnp.zeros_like(mu)
    w.update(in_rw=_rw_pad(oin[:, :RW_IN]), in_xl=oin[:, RW_IN:RW_IN + LRU_W], in_gl=oin[:, RW_IN + LRU_W:],
             out_a=oout[:RW_W], out_b=oout[RW_W:],
             mix_w=jnp.concatenate([zero, zero, mu, 1.0 - mu, zero, zero, zero, zero], axis=0), mix_b=zero,
             w0=fw["rwkv_w0"], w_up=_pad_to(fw["rwkv_w_up"][0], 128, 0), a0=fw["rwkv_a0"],
             a_up=_pad_to(fw["rwkv_a_up"][0], 128, 0), g_up=fw["rwkv_g_up"][0], k_k=fw["rwkv_k_k"], k_a=fw["rwkv_k_a"],
             r_k=fw["rwkv_r_k"].reshape(1, RW_W), ln_g=fw["rwkv_ln_g"], ln_b=fw["rwkv_ln_b"],
             lru_conv_w=_pad_to(fw["lru_conv_w"][0], 8, 0), lru_conv_b=fw["lru_conv_b"],
             lru_wa=_block_diag(fw["lru_w_a"][0]).astype(BF16), lru_b_a=fw["lru_b_a"].reshape(1, LRU_W),
             lru_wx=_block_diag(fw["lru_w_x"][0]).astype(BF16), lru_b_x=fw["lru_b_x"].reshape(1, LRU_W),
             lru_lam=fw["lru_lam"].reshape(1, LRU_W))
    return w


def _global_grads(g0, g1, s5_grads, d_norm_final):
    out = {k: jnp.concatenate([g0[k], g1[k]], axis=0) for k in ("norm_mix", "norm_ffn", "norm_pl")}
    for k in STACKED:
        out[k] = g0[k]
    out["e_in_proj"] = jnp.concatenate([g0["in_u"], g0["in_z"], g0["in_xbc"], g0["in_dt"][:, :SSD_H]], axis=1)[None]
    out["e_out_proj"] = jnp.concatenate([g0["out_a"], g0["out_b"]], axis=0)[None]
    d_lam_re, d_lam_im, d_lstep, d_bre, d_bim, d_cre, d_cim = s5_grads
    out["s5_lam_re"] = d_lam_re.reshape(1, S5_GROUPS, S5_P)
    out["s5_lam_im"] = d_lam_im.reshape(1, S5_GROUPS, S5_P)
    out["s5_log_step"] = d_lstep[0:1, :S5_GROUPS]
    unb = lambda b: jnp.transpose(b.reshape(S5_G, S5_GROUPS, S5_P), (1, 2, 0))[None]
    unc = lambda c: jnp.transpose(c.reshape(S5_GROUPS, S5_P, S5_G), (0, 2, 1))[None]
    out.update(s5_b_re=unb(d_bre), s5_b_im=unb(d_bim), s5_c_re=unc(d_cre), s5_c_im=unc(d_cim),
               s5_d=g0["s5_d"], s5_glu_w=g0["glu_w"][None], s5_glu_b=g0["glu_b"],
               ssd_conv_w=g0["ssd_conv_w"][None, :4], ssd_conv_b=g0["ssd_conv_b"], ssd_dt_bias=g0["dt_bias"][:, :SSD_H],
               ssd_a_log=g0["a_log"][:, :SSD_H], ssd_d=g0["ssd_d"][:, :SSD_H], ssd_norm=g0["ssd_norm"])
    out["o_in_proj"] = jnp.concatenate([_rw_unpad(g1["in_rw"]), g1["in_xl"], g1["in_gl"]], axis=1)[None]
    out["o_out_proj"] = jnp.concatenate([g1["out_a"], g1["out_b"]], axis=0)[None]
    out.update(rwkv_mu=_rw_unpad(g1["mix_w"][2:3] - g1["mix_w"][3:4]), rwkv_w0=g1["w0"], rwkv_w_up=g1["w_up"][None, :RW_LORA],
               rwkv_a0=g1["a0"], rwkv_a_up=g1["a_up"][None, :RW_LORA], rwkv_g_up=g1["g_up"][None], rwkv_k_k=g1["k_k"],
               rwkv_k_a=g1["k_a"], rwkv_r_k=g1["r_k"].reshape(1, RW_H, RW_HD), rwkv_ln_g=g1["ln_g"], rwkv_ln_b=g1["ln_b"],
               lru_conv_w=g1["lru_conv_w"][None, :4], lru_conv_b=g1["lru_conv_b"],
               lru_w_a=_diag_blocks(g1["lru_wa"])[None], lru_b_a=g1["lru_b_a"].reshape(1, LRU_B, 64),
               lru_w_x=_diag_blocks(g1["lru_wx"])[None], lru_b_x=g1["lru_b_x"].reshape(1, LRU_B, 64),
               lru_lam=g1["lru_lam"].reshape(1, LRU_B, 64), norm_final=d_norm_final.reshape(D))
    return out


def _local_step(x, p, target, fw):
    prep_in = _s5_prep_inputs(fw)
    prep = _single(f_s5_prep, prep_in, name="s5_prep")
    w0, w1 = _even_weights(fw, prep), _odd_weights(fw)
    h1, r0 = _layer_fwd(x, p[0], w0, False, "l0")
    h2, r1 = _layer_fwd(h1, p[1], w1, True, "l1")
    gf = fw["norm_final"].reshape(1, D)
    (loss8,) = _stage(f_loss, [h2, target], [gf], tb=TB, name="loss", out_dtypes=[], n_acc=1)
    one = jnp.zeros((8, 128), F32).at[0, 0].set(1.0)
    dh2, d_gf = _stage_vjp(f_loss, [h2, target], [gf], [], tb=TB, name="loss_b", drow=[0], dconst=[0], acc_cots=[one])
    dh1, g1 = _layer_bwd(dh2, p[1], w1, r1, True, "l1", None)
    dx, g0 = _layer_bwd(dh1, p[0], w0, r0, False, "l0", g1)
    cots = [g0[k] for k in ("abar_re", "abar_im", "wb_re", "wb_im", "wc_re", "wc_im")]
    s5_grads = _single_vjp(f_s5_prep, prep_in, cots, name="s5_prep_b")
    return loss8[0, 0], dx, _global_grads(g0, g1, s5_grads, d_gf)


def _xyc():
    return lax.axis_index("x"), lax.axis_index("y"), lax.axis_index("c")


def _flip(v, bit):
    return 1 - v if bit else v


def _remote(src, dst, send_sems, recv_sems, k, dev):
    return pltpu.make_async_remote_copy(src_ref=src, dst_ref=dst, send_sem=send_sems.at[k], recv_sem=recv_sems.at[k],
                                        device_id=dev, device_id_type=MESH)


def _dma_scratch(n_remote, n_local):
    return [pltpu.SemaphoreType.DMA((n_remote,)), pltpu.SemaphoreType.DMA((n_remote,)), pltpu.SemaphoreType.DMA((n_local,))]


CHIP_FLIPS = ((1, 0), (0, 1), (1, 1))


def _gather_chips(arrs, out_shapes, places, *, name):
    n = len(arrs)

    def body(*refs):
        ins, outs = refs[:n], refs[n:2 * n]
        send_sems, recv_sems = refs[2 * n:]
        x, y, c = _xyc()
        chip, sib = 2 * x + y, (x, y, 1 - c)
        peers = [(_flip(x, fx), _flip(y, fy)) for fx, fy in CHIP_FLIPS]
        first = [_remote(ins[a].at[c], places[a](outs[a], chip, c), send_sems, recv_sems, 6 * a + j, (px, py, c))
                 for a in range(n) for j, (px, py) in enumerate(peers)]
        for cp in first:
            cp.start()
        passed = []
        for a in range(n):
            for j, (px, py) in enumerate(peers):
                landed = places[a](outs[a], 2 * px + py, c)
                _remote(ins[a].at[c], landed, send_sems, recv_sems, 6 * a + j, (px, py, c)).wait_recv()
                cp = _remote(landed, landed, send_sems, recv_sems, 6 * a + 3 + j, sib)
                cp.start()
                passed.append(cp)
        for a in range(n):
            for j, (px, py) in enumerate(peers):
                other = places[a](outs[a], 2 * px + py, 1 - c)
                _remote(other, other, send_sems, recv_sems, 6 * a + 3 + j, sib).wait_recv()
        for cp in first + passed:
            cp.wait_send()

    return pl.pallas_call(
        body, out_shape=[SDS(s, a.dtype) for s, a in zip(out_shapes, arrs)], in_specs=[ANY] * n, out_specs=[ANY] * n,
        scratch_shapes=_dma_sems(6 * n), name=name,
    )(*arrs)


def _place_own(full, own, chip_vec, axis, *, name):
    layers, rows, cols = own.shape
    tb = min(rows, 512)
    per = rows // tb
    omap = ((lambda l, i, chip_ref: (l, chip_ref[0] * per + i, 0)) if axis == 1
            else (lambda l, i, chip_ref: (l, i, chip_ref[0])))

    def body(chip_ref, own_ref, full_ref, o_ref):
        o_ref[...] = own_ref[...]

    return pl.pallas_call(
        body,
        grid_spec=pltpu.PrefetchScalarGridSpec(
            num_scalar_prefetch=1, grid=(layers, per),
            in_specs=[pl.BlockSpec((None, tb, cols), lambda l, i, chip_ref: (l, i, 0)), ANY],
            out_specs=pl.BlockSpec((None, tb, cols), omap)),
        out_shape=SDS(full.shape, full.dtype), input_output_aliases={2: 0},
        compiler_params=_cparams(("arbitrary", "arbitrary")), name=name,
    )(chip_vec, own, full)


def _gather_all(arr, *, name):
    def body(in_ref, out_ref, send_sems, recv_sems, loc_sems):
        x, y, c = _xyc()
        mine = out_ref.at[4 * x + 2 * y + c]
        lc = pltpu.make_async_copy(in_ref, mine, loc_sems.at[0])
        lc.start()
        sends = []
        for k in range(1, 8):
            dev = (_flip(x, k >> 2 & 1), _flip(y, k >> 1 & 1), _flip(c, k & 1))
            cp = _remote(in_ref, mine, send_sems, recv_sems, k - 1, dev)
            cp.start()
            sends.append(cp)
        for k in range(1, 8):
            px, py, pc = _flip(x, k >> 2 & 1), _flip(y, k >> 1 & 1), _flip(c, k & 1)
            _remote(in_ref, out_ref.at[4 * px + 2 * py + pc], send_sems, recv_sems, k - 1, (px, py, pc)).wait_recv()
        for cp in sends:
            cp.wait_send()
        lc.wait()

    return pl.pallas_call(
        body, out_shape=SDS((8,) + arr.shape, arr.dtype), in_specs=[ANY], out_specs=ANY,
        scratch_shapes=_dma_scratch(7, 1), name=name,
    )(arr)


def _dma_sems(n):
    return [pltpu.SemaphoreType.DMA((n,)), pltpu.SemaphoreType.DMA((n,))]


def _send_halves(arrs, *, name):
    n = len(arrs)

    def body(*refs):
        ins, outs = refs[:n], refs[n:2 * n]
        send_sems, recv_sems = refs[2 * n:]
        x, y, c = _xyc()
        copies = [_remote(ins[a].at[k, 1 - c], outs[a].at[k], send_sems, recv_sems, 4 * a + k, (x, y, 1 - c))
                  for a in range(n) for k in range(4)]
        for cp in copies:
            cp.start()
        for cp in copies:
            cp.wait_recv()
        for cp in copies:
            cp.wait_send()

    return pl.pallas_call(
        body, out_shape=[SDS((4,) + a.shape[2:], a.dtype) for a in arrs], in_specs=[ANY] * n, out_specs=[ANY] * n,
        scratch_shapes=_dma_sems(4 * n), name=name,
    )(*arrs)


def _add_half(g, recv, c_vec, *, tb, out_dtype, name):
    _, _, rh, cols = g.shape
    tb = min(tb, rh)

    def body(c_ref, g_ref, r_ref, o_ref):
        o_ref[...] = (g_ref[...] + r_ref[...]).astype(o_ref.dtype)

    return pl.pallas_call(
        body,
        grid_spec=pltpu.PrefetchScalarGridSpec(
            num_scalar_prefetch=1, grid=(4, rh // tb),
            in_specs=[pl.BlockSpec((None, None, tb, cols), lambda k, i, c_ref: (k, c_ref[0], i, 0)),
                      pl.BlockSpec((None, tb, cols), lambda k, i, c_ref: (k, i, 0))],
            out_specs=pl.BlockSpec((None, tb, cols), lambda k, i, c_ref: (k, i, 0))),
        out_shape=SDS((4, rh, cols), out_dtype), compiler_params=_cparams(("arbitrary", "arbitrary")), name=name,
    )(c_vec, g, recv)


def _scatter_chips(arrs, *, name):
    n = len(arrs)

    def body(*refs):
        ins, outs = refs[:n], refs[n:2 * n]
        send_sems, recv_sems = refs[2 * n:]
        x, y, c = _xyc()
        copies = []
        for a in range(n):
            for j, (fx, fy) in enumerate(CHIP_FLIPS):
                px, py = _flip(x, fx), _flip(y, fy)
                copies.append(_remote(ins[a].at[2 * px + py], outs[a].at[j], send_sems, recv_sems, 3 * a + j, (px, py, c)))
        for cp in copies:
            cp.start()
        for cp in copies:
            cp.wait_recv()
        for cp in copies:
            cp.wait_send()

    return pl.pallas_call(
        body, out_shape=[SDS((3,) + a.shape[1:], a.dtype) for a in arrs], in_specs=[ANY] * n, out_specs=[ANY] * n,
        scratch_shapes=_dma_sems(3 * n), name=name,
    )(*arrs)


def _sum_chips(p, landed, chip_vec, *, tb, name):
    _, rh, cols = p.shape
    tb = min(tb, rh)

    def body(chip_ref, p_ref, l_ref, o_ref):
        f = lambda z: z.astype(F32)
        o_ref[...] = ((f(p_ref[...]) + f(l_ref[0])) + f(l_ref[1])) + f(l_ref[2])

    return pl.pallas_call(
        body,
        grid_spec=pltpu.PrefetchScalarGridSpec(
            num_scalar_prefetch=1, grid=(rh // tb,),
            in_specs=[pl.BlockSpec((None, tb, cols), lambda i, chip_ref: (chip_ref[0], i, 0)),
                      pl.BlockSpec((3, tb, cols), lambda i, chip_ref: (0, i, 0))],
            out_specs=pl.BlockSpec((tb, cols), lambda i, chip_ref: (i, 0))),
        out_shape=SDS((rh, cols), F32), compiler_params=_cparams(), name=name,
    )(chip_vec, p, landed)


def _swap_halves(arrs, *, name):
    n = len(arrs)

    def body(*refs):
        ins, outs = refs[:n], refs[n:2 * n]
        send_sems, recv_sems = refs[2 * n:]
        x, y, c = _xyc()
        copies = [_remote(ins[a], outs[a], send_sems, recv_sems, a, (x, y, 1 - c)) for a in range(n)]
        for cp in copies:
            cp.start()
        for cp in copies:
            cp.wait_recv()
        for cp in copies:
            cp.wait_send()

    return pl.pallas_call(
        body, out_shape=[SDS(a.shape, a.dtype) for a in arrs], in_specs=[ANY] * n, out_specs=[ANY] * n,
        scratch_shapes=_dma_sems(n), name=name,
    )(*arrs)


def _join_halves(mine, theirs, c_vec, *, tb, name):
    rh, cols = mine.shape
    tb = min(tb, rh)

    def body(c_ref, m_ref, t_ref, o_ref):
        o_ref[...] = jnp.where(pl.program_id(0) == c_ref[0], m_ref[...], t_ref[...])

    blk = pl.BlockSpec((tb, cols), lambda h, i, c_ref: (i, 0))
    return pl.pallas_call(
        body,
        grid_spec=pltpu.PrefetchScalarGridSpec(
            num_scalar_prefetch=1, grid=(2, rh // tb), in_specs=[blk, blk],
            out_specs=pl.BlockSpec((None, tb, cols), lambda h, i, c_ref: (h, i, 0))),
        out_shape=SDS((2, rh, cols), mine.dtype), compiler_params=_cparams(("arbitrary", "arbitrary")), name=name,
    )(c_vec, mine, theirs)


def _sum_lead(x, *, tb, name):
    k, r, c = x.shape
    tb = min(tb, r)
    assert r % tb == 0

    def body(x_ref, o_ref):
        acc = x_ref[0]
        for q in range(1, k):
            acc = acc + x_ref[q]
        o_ref[...] = acc

    return pl.pallas_call(
        body, grid=(r // tb,), in_specs=[pl.BlockSpec((k, tb, c), lambda i: (0, i, 0))],
        out_specs=pl.BlockSpec((tb, c), lambda i: (i, 0)), out_shape=SDS((r, c), x.dtype),
        compiler_params=_cparams(), name=name,
    )(x)


def f_adamw(w, g, m, v):
    m = ADAM_B1 * m + (1.0 - ADAM_B1) * g
    v = ADAM_B2 * v + (1.0 - ADAM_B2) * (g * g)
    m_hat = m / (1.0 - ADAM_B1 ** ADAM_STEP)
    v_hat = v / (1.0 - ADAM_B2 ** ADAM_STEP)
    return -ADAM_LR * (m_hat / (jnp.sqrt(v_hat) + ADAM_EPS) + ADAM_WD * w), m, v


def _adamw(w, g, m, v, *, name):
    shape = w.shape
    two = lambda a: a.reshape(-1, shape[-1])
    rows = two(w).shape[0]
    tb = 256 if rows % 256 == 0 else rows
    outs = _stage(f_adamw, [two(w), two(g), two(m), two(v)], [], tb=tb, name=name, out_dtypes=[F32] * 3)
    return [o.reshape(shape) for o in outs]


def _pack(arrs, rows=8):
    flat = jnp.concatenate([a.astype(F32).reshape(-1) for a in arrs])
    size = -(-flat.shape[0] // (rows * 128)) * (rows * 128)
    return _pad_to(flat, size, 0).reshape(-1, 128)


def _unpack(buf, shapes):
    flat = buf.reshape(-1)
    out, off = [], 0
    for s in shapes:
        n = math.prod(s)
        out.append(flat[off:off + n].reshape(s))
        off += n
    return out


WEIGHTS = ("norm_mix", "norm_ffn", "norm_pl", "mlp_w1", "mlp_w2", "pl_proj", "pl_gate", "e_in_proj", "e_out_proj",
           "s5_lam_re", "s5_lam_im", "s5_log_step", "s5_b_re", "s5_b_im", "s5_c_re", "s5_c_im", "s5_d", "s5_glu_w",
           "s5_glu_b", "ssd_conv_w", "ssd_conv_b", "ssd_dt_bias", "ssd_a_log", "ssd_d", "ssd_norm", "o_in_proj",
           "o_out_proj", "rwkv_mu", "rwkv_w0", "rwkv_w_up", "rwkv_a0", "rwkv_a_up", "rwkv_g_up", "rwkv_k_k", "rwkv_k_a",
           "rwkv_r_k", "rwkv_ln_g", "rwkv_ln_b", "lru_conv_w", "lru_conv_b", "lru_w_a", "lru_b_a", "lru_w_x", "lru_b_x",
           "lru_lam", "norm_final")
BIG = ("mlp_w1", "mlp_w2", "pl_proj", "pl_gate", "e_in_proj", "e_out_proj", "o_in_proj", "o_out_proj")
STACKED = BIG[:4]
SHARD_AXIS = {"mlp_w1": 2, "mlp_w2": 1, "pl_proj": 2, "pl_gate": 1, "e_in_proj": 2, "e_out_proj": 1, "s5_glu_w": 1,
              "ssd_conv_w": 2, "o_in_proj": 2, "o_out_proj": 1, "rwkv_mu": 1, "rwkv_w0": 1, "rwkv_w_up": 2, "rwkv_a0": 1,
              "rwkv_a_up": 2, "rwkv_g_up": 2, "rwkv_k_k": 1, "rwkv_k_a": 1, "rwkv_ln_g": 1, "rwkv_ln_b": 1,
              "lru_conv_w": 2, "lru_conv_b": 1}
SMALL = tuple(n for n in WEIGHTS if n not in BIG)
SMALL_SHARDED = tuple(n for n in SMALL if n in SHARD_AXIS)


def _gather_weights(w):
    shapes = [w[n].shape for n in SMALL_SHARDED]
    chip = 2 * lax.axis_index("x") + lax.axis_index("y")
    mine = [w[n].astype(BF16) for n in BIG] + [_pack([w[n] for n in SMALL_SHARDED], rows=16)]
    out_shapes, places = [], []
    for n, a in zip(BIG + ("small",), mine):
        layers, rows, cols = a.shape if a.ndim == 3 else (1,) + a.shape
        ax = SHARD_AXIS.get(n)
        if ax == 1:
            step = rows if layers == 2 else rows // 2
            out_shapes.append((layers, 4 * rows, cols))
            places.append(lambda o, k, h, layers=layers, rows=rows, step=step: o.at[
                h if layers == 2 else 0, pl.ds(pl.multiple_of(k * rows + (0 if layers == 2 else h * step), 16), step), :])
        elif ax == 2 and layers == 2:
            out_shapes.append((layers, rows, 4 * cols))
            places.append(lambda o, k, h, cols=cols: o.at[h, :, pl.ds(pl.multiple_of(k * cols, 128), cols)])
        else:
            out_shapes.append((4, 2, layers * rows // 2, cols))
            places.append(lambda o, k, h: o.at[k, h])
    got = _gather_chips([a.reshape(2, -1, a.shape[-1]) for a in mine], out_shapes, places, name="gather_weights")
    fw = {n: w[n] for n in SMALL if n not in SHARD_AXIS}
    for n, g, a in zip(BIG, got[:-1], mine):
        if g.shape[0] == 4:
            g = lax.dynamic_update_index_in_dim(g.reshape((4,) + a.shape), a, chip, 0)
            fw[n] = jnp.concatenate([g[k] for k in range(4)], axis=SHARD_AXIS[n])
        else:
            fw[n] = _place_own(g, a, chip.astype(jnp.int32).reshape(1), SHARD_AXIS[n], name=f"place_{n}")
    small = lax.dynamic_update_index_in_dim(got[-1].reshape((4,) + mine[-1].shape), mine[-1], chip, 0)
    parts = [_unpack(small[k], shapes) for k in range(4)]
    for i, n in enumerate(SMALL_SHARDED):
        fw[n] = jnp.concatenate([parts[k][i] for k in range(4)], axis=SHARD_AXIS[n])
    return fw


def _reduce_big(grads, w):
    stacks = []
    for n in BIG:
        cols = w[n].shape[-1]
        stacks.append(grads[n] if n in STACKED else
                      jnp.stack(jnp.split(grads[n], 4, axis=SHARD_AXIS[n])).reshape(4, 2, -1, cols))
    c_vec = lax.axis_index("c").astype(jnp.int32).reshape(1)
    chip_vec = (2 * lax.axis_index("x") + lax.axis_index("y")).astype(jnp.int32).reshape(1)
    got = _send_halves(stacks, name="reduce_pair")
    sums = [_add_half(s, r, c_vec, tb=512, out_dtype=BF16, name=f"reduce_pair_sum_{n}")
            for n, s, r in zip(BIG, stacks, got)]
    landed = _scatter_chips(sums, name="reduce_chips")
    halves = [_sum_chips(p, l, chip_vec, tb=256, name=f"reduce_chips_sum_{n}") for n, p, l in zip(BIG, sums, landed)]
    theirs = _swap_halves(halves, name="reduce_swap")
    return {n: _join_halves(h, t, c_vec, tb=512, name=f"reduce_join_{n}").reshape(w[n].shape)
            for n, h, t in zip(BIG, halves, theirs)}


def _reduce_small(grads, w, chip, loss):
    shapes = [grads[n].shape for n in SMALL] + [(1,)]
    packed = _pack([grads[n] for n in SMALL] + [loss.reshape(1)])
    total = _sum_lead(_gather_all(packed, name="reduce_small"), tb=packed.shape[0], name="reduce_small_sum")
    *parts, loss_sum = _unpack(total, shapes)
    out = {}
    for n, g in zip(SMALL, parts):
        if n in SHARD_AXIS:
            ax = SHARD_AXIS[n]
            size = w[n].shape[ax]
            g = lax.dynamic_slice_in_dim(g, chip * size, size, axis=ax)
        out[n] = g
    return out, loss_sum[0]


def kernel(x, p, norm_mix, norm_ffn, norm_pl, mlp_w1, mlp_w2, pl_proj, pl_gate, e_in_proj, e_out_proj, s5_lam_re, s5_lam_im, s5_log_step, s5_b_re, s5_b_im, s5_c_re, s5_c_im, s5_d, s5_glu_w, s5_glu_b, ssd_conv_w, ssd_conv_b, ssd_dt_bias, ssd_a_log, ssd_d, ssd_norm, o_in_proj, o_out_proj, rwkv_mu, rwkv_w0, rwkv_w_up, rwkv_a0, rwkv_a_up, rwkv_g_up, rwkv_k_k, rwkv_k_a, rwkv_r_k, rwkv_ln_g, rwkv_ln_b, lru_conv_w, lru_conv_b, lru_w_a, lru_b_a, lru_w_x, lru_b_x, lru_lam, norm_final, loss_target, m_norm_mix, m_norm_ffn, m_norm_pl, m_mlp_w1, m_mlp_w2, m_pl_proj, m_pl_gate, m_e_in_proj, m_e_out_proj, m_s5_lam_re, m_s5_lam_im, m_s5_log_step, m_s5_b_re, m_s5_b_im, m_s5_c_re, m_s5_c_im, m_s5_d, m_s5_glu_w, m_s5_glu_b, m_ssd_conv_w, m_ssd_conv_b, m_ssd_dt_bias, m_ssd_a_log, m_ssd_d, m_ssd_norm, m_o_in_proj, m_o_out_proj, m_rwkv_mu, m_rwkv_w0, m_rwkv_w_up, m_rwkv_a0, m_rwkv_a_up, m_rwkv_g_up, m_rwkv_k_k, m_rwkv_k_a, m_rwkv_r_k, m_rwkv_ln_g, m_rwkv_ln_b, m_lru_conv_w, m_lru_conv_b, m_lru_w_a, m_lru_b_a, m_lru_w_x, m_lru_b_x, m_lru_lam, m_norm_final, v_norm_mix, v_norm_ffn, v_norm_pl, v_mlp_w1, v_mlp_w2, v_pl_proj, v_pl_gate, v_e_in_proj, v_e_out_proj, v_s5_lam_re, v_s5_lam_im, v_s5_log_step, v_s5_b_re, v_s5_b_im, v_s5_c_re, v_s5_c_im, v_s5_d, v_s5_glu_w, v_s5_glu_b, v_ssd_conv_w, v_ssd_conv_b, v_ssd_dt_bias, v_ssd_a_log, v_ssd_d, v_ssd_norm, v_o_in_proj, v_o_out_proj, v_rwkv_mu, v_rwkv_w0, v_rwkv_w_up, v_rwkv_a0, v_rwkv_a_up, v_rwkv_g_up, v_rwkv_k_k, v_rwkv_k_a, v_rwkv_r_k, v_rwkv_ln_g, v_rwkv_ln_b, v_lru_conv_w, v_lru_conv_b, v_lru_w_a, v_lru_b_a, v_lru_w_x, v_lru_b_x, v_lru_lam, v_norm_final):
    given = dict(locals())
    w = {n: given[n] for n in WEIGHTS}
    m = {n: given["m_" + n] for n in WEIGHTS}
    v = {n: given["v_" + n] for n in WEIGHTS}
    chip = 2 * lax.axis_index("x") + lax.axis_index("y")

    fw = _gather_weights(w)
    loss, dx, grads = _local_step(x[0], p[:, 0], loss_target[0], fw)
    g = _reduce_big(grads, w)
    g_small, loss = _reduce_small(grads, w, chip, loss)
    g.update(g_small)

    delta, new_m, new_v = {}, {}, {}
    for n in BIG:
        delta[n], new_m[n], new_v[n] = _adamw(w[n], g[n], m[n], v[n], name=f"adamw_{n}")
    shapes = [w[n].shape for n in SMALL]
    packed = [_pack([d[n] for n in SMALL]) for d in (w, g, m, v)]
    for d, buf in zip((delta, new_m, new_v), _adamw(*packed, name="adamw_small")):
        d.update(zip(SMALL, _unpack(buf, shapes)))
    return (loss, dx[None], *[g[n] for n in WEIGHTS], *[delta[n] for n in WEIGHTS],
            *[new_m[n] for n in WEIGHTS], *[new_v[n] for n in WEIGHTS])
```

```python
import functools
import math

import jax
import jax.numpy as jnp
from jax import lax
from jax.experimental import pallas as pl
from jax.experimental.pallas import tpu as pltpu

F32 = jnp.float32
BF16 = jnp.bfloat16
HI = lax.Precision.HIGHEST
MESH = pl.DeviceIdType.MESH
SDS = jax.ShapeDtypeStruct
VMEM_LIMIT = 56 * 1024 * 1024
MM_VMEM_BUDGET = 40 * 1024 * 1024
ANY = pl.BlockSpec(memory_space=pl.ANY)

D = 2048
PL_DIM = 256
D_FF = 4 * D
EPS = 1e-6
S5_W, S5_G, S5_GROUPS, S5_P = 512, 16, 32, 64
S5_N = S5_GROUPS * S5_P
SSD_W, SSD_HD, SSD_H, SSD_NG, SSD_N, SSD_L = 1536, 64, 24, 4, 128, 128
SSD_CONV = SSD_W + 2 * SSD_NG * SSD_N
EVEN_IN = S5_W + SSD_W + SSD_CONV + SSD_H
EVEN_PAD = 5120
RW_W, RW_H, RW_HD = 1024, 16, 64
RW_LORA = 96
RW_GATE = 256
RW_IN = 3 * RW_W + 2 * RW_LORA + RW_GATE
RW_PAD = 3584
LRU_W, LRU_B = 1024, 16
ODD_IN = RW_IN + 2 * LRU_W
ODD_PAD = RW_PAD + 2 * LRU_W
GN_EPS = 64e-5
LRU_C = 8.0
ADAM_LR, ADAM_B1, ADAM_B2, ADAM_EPS, ADAM_WD, ADAM_STEP = 0.001, 0.9, 0.999, 1e-08, 0.01, 10


def _cparams(sem=("arbitrary",)):
    return pltpu.CompilerParams(dimension_semantics=sem, vmem_limit_bytes=VMEM_LIMIT)


def _dot16(a, b, dims=(((1,), (0,)), ((), ()))):
    return lax.dot_general(a.astype(BF16), b.astype(BF16), dims, preferred_element_type=F32)


NN = (((1,), (0,)), ((), ()))
NT = (((1,), (1,)), ((), ()))
TN = (((0,), (0,)), ((), ()))


def _split3(x):
    top = lambda z: lax.bitcast_convert_type(lax.bitcast_convert_type(z, jnp.int32) & jnp.int32(-65536), F32)
    hi = top(x)
    rest = x - hi
    mid = top(rest)
    return hi.astype(BF16), mid.astype(BF16), (rest - mid).astype(BF16)


def _sel_raw(a, b, dims, data):
    parts = _split3((a, b)[data].astype(F32))
    mask = (a, b)[1 - data].astype(BF16)
    acc = None
    for part in reversed(parts):
        ops = (part, mask) if data == 0 else (mask, part)
        term = lax.dot_general(*ops, dims, preferred_element_type=F32)
        acc = term if acc is None else acc + term
    return acc


_SEL_BACK = {(NN, 0): ("g", "m", NT, 0), (NT, 0): ("g", "m", NN, 0), (TN, 0): ("m", "g", NT, 1),
             (NN, 1): ("m", "g", TN, 1), (NT, 1): ("g", "m", TN, 0), (TN, 1): ("m", "g", NN, 1)}


@functools.partial(jax.custom_vjp, nondiff_argnums=(2, 3))
def _sel_dot(a, b, dims, data):
    return _sel_raw(a, b, dims, data)


def _sel_dot_fwd(a, b, dims, data):
    return _sel_raw(a, b, dims, data), (a, b)


def _sel_dot_bwd(dims, data, res, g):
    mask = res[1 - data]
    left, right, dims2, data2 = _SEL_BACK[(dims, data)]
    grad = _sel_raw(g if left == "g" else mask, g if right == "g" else mask, dims2, data2)
    zero = jnp.zeros_like(mask)
    return (grad, zero) if data == 0 else (zero, grad)


_sel_dot.defvjp(_sel_dot_fwd, _sel_dot_bwd)


def _tile(dim, target):
    if dim <= target:
        return dim
    t = target - target % 128
    while t > 128 and dim % t:
        t -= 128
    assert dim % t == 0, (dim, target)
    return t


def _mm(a, b, *, ta=False, tb=False, add=None, out_dtype=F32, tm=1024, tn=1024, tk=1024, name,
        epilogue=None, extra=(), out_dtypes=None):
    layer = None
    if isinstance(b, tuple):
        b, layer = b
    m, k = (a.shape[1], a.shape[0]) if ta else a.shape
    n = b.shape[-2] if tb else b.shape[-1]
    assert (b.shape[-1] if tb else b.shape[-2]) == k, (a.shape, b.shape, ta, tb)
    ins = [a, b] + ([add] if add is not None else []) + list(extra)
    out_dtypes = out_dtypes or [out_dtype]
    n_in, n_out = len(ins), len(out_dtypes)
    tm, tn = _tile(m, tm), _tile(n, tn)
    tiles = 2 * tm * tn * sum(jnp.dtype(x.dtype).itemsize for x in ins[2:]) + 2 * tm * tn * sum(
        jnp.dtype(dt).itemsize for dt in out_dtypes) + 4 * tm * tn
    per_k = 2 * (tm * a.dtype.itemsize + tn * b.dtype.itemsize)
    tk = _tile(k, max(tk, min(2048, (MM_VMEM_BUDGET - tiles) // per_k // 128 * 128)))
    nk = k // tk
    dims = (((0 if ta else 1,), (1 if tb else 0,)), ((), ()))

    def finish(acc, refs):
        res = epilogue(acc, *[r[...] for r in refs[n_in - len(extra):n_in]]) if epilogue else (acc,)
        for o_ref, val in zip(refs[n_in:n_in + n_out], res):
            o_ref[...] = val.astype(o_ref.dtype)

    def body(*refs):
        a_ref, b_ref, acc_ref = refs[0], refs[1], refs[-1]
        if nk == 1:
            acc = _dot16(a_ref[...], b_ref[...], dims)
            finish(acc + refs[2][...].astype(F32) if add is not None else acc, refs)
            return
        kk = pl.program_id(2)

        @pl.when(kk == 0)
        def _():
            acc_ref[...] = refs[2][...].astype(F32) if add is not None else jnp.zeros_like(acc_ref)

        acc_ref[...] += _dot16(a_ref[...], b_ref[...], dims)

        @pl.when(kk == nk - 1)
        def _():
            finish(acc_ref[...], refs)

    a_spec = pl.BlockSpec((tk, tm), lambda i, j, q: (q, i)) if ta else pl.BlockSpec((tm, tk), lambda i, j, q: (i, q))
    b_spec = pl.BlockSpec((tn, tk), lambda i, j, q: (j, q)) if tb else pl.BlockSpec((tk, tn), lambda i, j, q: (q, j))
    if layer is not None:
        b_spec = (pl.BlockSpec((None, tn, tk), lambda i, j, q: (layer, j, q)) if tb
                  else pl.BlockSpec((None, tk, tn), lambda i, j, q: (layer, q, j)))
    o_spec = pl.BlockSpec((tm, tn), lambda i, j, q: (i, j))
    outs = pl.pallas_call(
        body,
        grid=(m // tm, n // tn, nk),
        in_specs=[a_spec, b_spec] + [o_spec] * (n_in - 2),
        out_specs=[o_spec] * n_out,
        out_shape=[SDS((m, n), dt) for dt in out_dtypes],
        scratch_shapes=[pltpu.VMEM((tm, tn) if nk > 1 else (8, 128), F32)],
        compiler_params=_cparams(("parallel", "parallel", "arbitrary")),
        name=name,
    )(*ins)
    return outs if epilogue else outs[0]


def _mm_grad(x, dy, *, layer, cols_cut, shard, prev, name):
    t = x.shape[0]
    r, c = shard
    tm, tn = _tile(r, 1024), _tile(c, 1024)
    per_k = 2 * (tm * x.dtype.itemsize + tn * dy.dtype.itemsize)
    tk = _tile(t, max(1024, min(2048, (MM_VMEM_BUDGET - 12 * tm * tn) // per_k // 128 * 128)))
    nk = t // tk
    if cols_cut:
        assert x.shape[1] == r and dy.shape[1] == 4 * c
        per = c // tn
        omap = lambda i, j, q: (j // per, layer, i, j % per)
    else:
        assert x.shape[1] == 4 * r and dy.shape[1] == c
        per = r // tm
        omap = lambda i, j, q: (i // per, layer, i % per, j)

    def body(*refs):
        x_ref, dy_ref = refs[:2]
        o_ref, acc_ref = refs[-2:]
        kk = pl.program_id(2)

        @pl.when(kk == 0)
        def _():
            acc_ref[...] = jnp.zeros_like(acc_ref)

        acc_ref[...] += _dot16(x_ref[...], dy_ref[...], TN)

        @pl.when(kk == nk - 1)
        def _():
            o_ref[...] = acc_ref[...]

    return pl.pallas_call(
        body,
        grid=(x.shape[1] // tm, dy.shape[1] // tn, nk),
        in_specs=[pl.BlockSpec((tk, tm), lambda i, j, q: (q, i)), pl.BlockSpec((tk, tn), lambda i, j, q: (q, j))]
        + ([ANY] if prev is not None else []),
        out_specs=pl.BlockSpec((None, None, tm, tn), omap),
        out_shape=SDS((4, 2, r, c), F32),
        scratch_shapes=[pltpu.VMEM((tm, tn), F32)],
        input_output_aliases={2: 0} if prev is not None else {},
        compiler_params=_cparams(("parallel", "parallel", "arbitrary")),
        name=name,
    )(x, dy, *([prev] if prev is not None else []))


def _single(fn, consts, *, name):
    outs = jax.eval_shape(fn, *[SDS(c.shape, F32) for c in consts])
    n_in = len(consts)

    def body(*refs):
        res = fn(*[r[...] for r in refs[:n_in]])
        for o_ref, v in zip(refs[n_in:], res):
            o_ref[...] = v

    return pl.pallas_call(body, out_shape=[SDS(o.shape, F32) for o in outs],
                          compiler_params=pltpu.CompilerParams(vmem_limit_bytes=VMEM_LIMIT), name=name)(*consts)


def _single_vjp(fn, consts, cots, *, name):
    n_in = len(consts)

    def body(*refs):
        _, pull = jax.vjp(fn, *[r[...] for r in refs[:n_in]])
        grads = pull(tuple(r[...] for r in refs[n_in:n_in + len(cots)]))
        for o_ref, v in zip(refs[n_in + len(cots):], grads):
            o_ref[...] = v

    return pl.pallas_call(body, out_shape=[SDS(c.shape, F32) for c in consts],
                          compiler_params=pltpu.CompilerParams(vmem_limit_bytes=VMEM_LIMIT), name=name)(*consts, *cots)


def _full_spec(shape):
    nd = len(shape)
    return pl.BlockSpec(shape, lambda i, _n=nd: (0,) * _n)


def _stage_shapes(fn, rows, consts, tb, pos):
    rs = [SDS((tb, r.shape[1]), F32) for r in rows]
    cs = [SDS(c.shape, F32) for c in consts]
    f = (lambda *a: fn(jnp.int32(0), *a)) if pos else fn
    return jax.eval_shape(f, *rs, *cs)


def _stage(fn, rows, consts, *, tb, name, out_dtypes, n_acc=0, pos=False):
    t = rows[0].shape[0]
    assert t % tb == 0
    outs = _stage_shapes(fn, rows, consts, tb, pos)
    n_out = len(outs)
    n_row = n_out - n_acc
    n_in = len(rows) + len(consts)

    def body(*refs):
        i = pl.program_id(0)
        vals = [r[...].astype(F32) for r in refs[:n_in]]
        res = fn(i * tb, *vals) if pos else fn(*vals)
        out_refs = refs[n_in:]
        for q in range(n_row):
            out_refs[q][...] = res[q].astype(out_refs[q].dtype)
        for q in range(n_row, n_out):
            @pl.when(i == 0)
            def _(q=q):
                out_refs[q][...] = jnp.zeros_like(out_refs[q])

            out_refs[q][...] += res[q]

    in_specs = [pl.BlockSpec((tb, r.shape[1]), lambda i: (i, 0)) for r in rows] + [_full_spec(c.shape) for c in consts]
    out_specs = [pl.BlockSpec((tb, o.shape[1]), lambda i: (i, 0)) for o in outs[:n_row]] + [_full_spec(o.shape) for o in outs[n_row:]]
    out_shape = [SDS((t, o.shape[1]), dt) for o, dt in zip(outs[:n_row], out_dtypes)] + [SDS(o.shape, F32) for o in outs[n_row:]]
    return pl.pallas_call(
        body, grid=(t // tb,), in_specs=in_specs, out_specs=out_specs, out_shape=out_shape,
        compiler_params=_cparams(), name=name,
    )(*rows, *consts)


def _stage_vjp(fn, rows, consts, cots, *, tb, name, drow, dconst, drow_dtypes=None, acc_cots=(), pos=False):
    t = rows[0].shape[0]
    assert t % tb == 0
    n_rows, n_consts, n_cots, n_acc = len(rows), len(consts), len(cots), len(acc_cots)
    n_in = n_rows + n_consts + n_cots + n_acc
    drow_dtypes = drow_dtypes or [F32] * len(drow)

    def body(*refs):
        i = pl.program_id(0)
        vals = [r[...].astype(F32) for r in refs[:n_in]]
        rv, cv = vals[:n_rows], vals[n_rows:n_rows + n_consts]
        ct = tuple(vals[n_rows + n_consts:])

        def f(*dargs):
            r2, c2 = list(rv), list(cv)
            for q, idx in enumerate(drow):
                r2[idx] = dargs[q]
            for q, idx in enumerate(dconst):
                c2[idx] = dargs[len(drow) + q]
            return fn(i * tb, *r2, *c2) if pos else fn(*r2, *c2)

        _, pull = jax.vjp(f, *[rv[q] for q in drow], *[cv[q] for q in dconst])
        grads = pull(ct)
        out_refs = refs[n_in:]
        for q in range(len(drow)):
            out_refs[q][...] = grads[q].astype(out_refs[q].dtype)
        for q in range(len(drow), len(drow) + len(dconst)):
            @pl.when(i == 0)
            def _(q=q):
                out_refs[q][...] = jnp.zeros_like(out_refs[q])

            out_refs[q][...] += grads[q]

    in_specs = ([pl.BlockSpec((tb, r.shape[1]), lambda i: (i, 0)) for r in rows] + [_full_spec(c.shape) for c in consts]
                + [pl.BlockSpec((tb, c.shape[1]), lambda i: (i, 0)) for c in cots] + [_full_spec(c.shape) for c in acc_cots])
    out_specs = ([pl.BlockSpec((tb, rows[q].shape[1]), lambda i: (i, 0)) for q in drow]
                 + [_full_spec(consts[q].shape) for q in dconst])
    out_shape = ([SDS(rows[q].shape, dt) for q, dt in zip(drow, drow_dtypes)]
                 + [SDS(consts[q].shape, F32) for q in dconst])
    return pl.pallas_call(
        body, grid=(t // tb,), in_specs=in_specs, out_specs=out_specs, out_shape=out_shape,
        compiler_params=_cparams(), name=name,
    )(*rows, *consts, *cots, *acc_cots)


def _conv_fwd(x, w, b, *, tb, name):
    t, c = x.shape
    r8 = tb // 8

    def body(x_ref, p_ref, w_ref, b_ref, o_ref):
        i = pl.program_id(0)
        x_ = x_ref[...]
        p_ = jnp.where(i > 0, p_ref[...], 0.0)
        w_ = w_ref[...]
        row = lax.broadcasted_iota(jnp.int32, x_.shape, 0)
        row8 = lax.broadcasted_iota(jnp.int32, p_.shape, 0)
        acc = x_ * w_[3:4, :] + b_ref[...]
        head = jnp.zeros_like(p_)
        for j in (1, 2, 3):
            wj = w_[3 - j:4 - j, :]
            acc += jnp.where(row >= j, pltpu.roll(x_, j, 0), 0.0) * wj
            head += jnp.where(row8 < j, pltpu.roll(p_, j, 0), 0.0) * wj
        o_ref[...] = acc
        o_ref[0:8, :] += head

    return pl.pallas_call(
        body, grid=(t // tb,),
        in_specs=[pl.BlockSpec((tb, c), lambda i: (i, 0)),
                  pl.BlockSpec((8, c), lambda i: (jnp.maximum(i * r8 - 1, 0), 0)),
                  _full_spec(w.shape), _full_spec(b.shape)],
        out_specs=pl.BlockSpec((tb, c), lambda i: (i, 0)),
        out_shape=SDS((t, c), F32), compiler_params=_cparams(), name=name,
    )(x, x, w, b)


def _conv_bwd(x, w, dy, *, tb, name):
    t, c = x.shape
    r8 = tb // 8
    nb = t // tb

    def body(x_ref, p_ref, w_ref, g_ref, n_ref, dx_ref, dw_ref, db_ref):
        i = pl.program_id(0)
        x_ = x_ref[...]
        p_ = jnp.where(i > 0, p_ref[...], 0.0)
        g_ = g_ref[...]
        n_ = jnp.where(i < nb - 1, n_ref[...], 0.0)
        w_ = w_ref[...]
        row = lax.broadcasted_iota(jnp.int32, x_.shape, 0)
        row8 = lax.broadcasted_iota(jnp.int32, p_.shape, 0)
        g8 = g_[0:8, :]
        dx = g_ * w_[3:4, :]
        tail = jnp.zeros_like(n_)
        dws = [jnp.sum(g_ * x_, axis=0, keepdims=True)]
        for j in (1, 2, 3):
            wj = w_[3 - j:4 - j, :]
            dx += jnp.where(row < tb - j, pltpu.roll(g_, tb - j, 0), 0.0) * wj
            tail += jnp.where(row8 >= 8 - j, pltpu.roll(n_, 8 - j, 0), 0.0) * wj
            xs = jnp.where(row >= j, pltpu.roll(x_, j, 0), 0.0)
            ps = jnp.where(row8 < j, pltpu.roll(p_, j, 0), 0.0)
            dws.append(jnp.sum(g_ * xs, axis=0, keepdims=True) + jnp.sum(g8 * ps, axis=0, keepdims=True))
        dx_ref[...] = dx
        dx_ref[tb - 8:tb, :] += tail

        @pl.when(i == 0)
        def _():
            dw_ref[...] = jnp.zeros_like(dw_ref)
            db_ref[...] = jnp.zeros_like(db_ref)

        for j in range(4):
            dw_ref[3 - j:4 - j, :] += dws[j]
        db_ref[...] += jnp.sum(g_, axis=0, keepdims=True)

    return pl.pallas_call(
        body, grid=(nb,),
        in_specs=[pl.BlockSpec((tb, c), lambda i: (i, 0)),
                  pl.BlockSpec((8, c), lambda i: (jnp.maximum(i * r8 - 1, 0), 0)),
                  _full_spec(w.shape),
                  pl.BlockSpec((tb, c), lambda i: (i, 0)),
                  pl.BlockSpec((8, c), lambda i: (jnp.minimum((i + 1) * r8, t // 8 - 1), 0))],
        out_specs=[pl.BlockSpec((tb, c), lambda i: (i, 0)), _full_spec((8, c)), _full_spec((1, c))],
        out_shape=[SDS((t, c), F32), SDS((8, c), F32), SDS((1, c), F32)],
        compiler_params=_cparams(), name=name,
    )(x, x, w, dy, dy)


def _lru_scan_fwd(a, b, *, tb, name):
    t, c = a.shape

    def body(a_ref, b_ref, h_ref, st_ref):
        @pl.when(pl.program_id(0) == 0)
        def _():
            st_ref[...] = jnp.zeros_like(st_ref)

        def step(s, h):
            h = a_ref[pl.ds(s, 1), :] * h + b_ref[pl.ds(s, 1), :]
            h_ref[pl.ds(s, 1), :] = h
            return h

        st_ref[...] = lax.fori_loop(0, tb, step, st_ref[...], unroll=8)

    blk = pl.BlockSpec((tb, c), lambda i: (i, 0))
    return pl.pallas_call(
        body, grid=(t // tb,), in_specs=[blk, blk], out_specs=blk, out_shape=SDS((t, c), F32),
        scratch_shapes=[pltpu.VMEM((1, c), F32)], compiler_params=_cparams(), name=name,
    )(a, b)


def _lru_scan_bwd(a, h, dh, *, tb, name):
    t, c = a.shape
    nb = t // tb
    r8 = tb // 8

    def body(a_ref, h_ref, p_ref, g_ref, da_ref, db_ref, st_ref):
        i = pl.program_id(0)

        @pl.when(i == 0)
        def _():
            st_ref[...] = jnp.zeros_like(st_ref)

        hprev0 = jnp.where(i < nb - 1, p_ref[7:8, :], 0.0)

        def step(q, carry):
            s = tb - 1 - q
            g = g_ref[pl.ds(s, 1), :] + carry
            hp = h_ref[pl.ds(jnp.maximum(s - 1, 0), 1), :]
            hp = jnp.where(s > 0, hp, hprev0)
            db_ref[pl.ds(s, 1), :] = g
            da_ref[pl.ds(s, 1), :] = g * hp
            return a_ref[pl.ds(s, 1), :] * g

        st_ref[...] = lax.fori_loop(0, tb, step, st_ref[...], unroll=8)

    rev = pl.BlockSpec((tb, c), lambda i: (nb - 1 - i, 0))
    prev = pl.BlockSpec((8, c), lambda i: (jnp.maximum((nb - 1 - i) * r8 - 1, 0), 0))
    return pl.pallas_call(
        body, grid=(nb,), in_specs=[rev, rev, prev, rev], out_specs=[rev, rev],
        out_shape=[SDS((t, c), F32), SDS((t, c), F32)],
        scratch_shapes=[pltpu.VMEM((1, c), F32)], compiler_params=_cparams(), name=name,
    )(a, h, h, dh)


def _s5_scan_fwd(ar, ai, br, bi, *, tb, name):
    t, c = br.shape

    def body(ar_ref, ai_ref, br_ref, bi_ref, xr_ref, xi_ref, sr_ref, si_ref):
        @pl.when(pl.program_id(0) == 0)
        def _():
            sr_ref[...] = jnp.zeros_like(sr_ref)
            si_ref[...] = jnp.zeros_like(si_ref)

        ar_, ai_ = ar_ref[...], ai_ref[...]

        def step(s, carry):
            xr, xi = carry
            nr = ar_ * xr - ai_ * xi + br_ref[pl.ds(s, 1), :]
            ni = ar_ * xi + ai_ * xr + bi_ref[pl.ds(s, 1), :]
            xr_ref[pl.ds(s, 1), :] = nr
            xi_ref[pl.ds(s, 1), :] = ni
            return nr, ni

        xr, xi = lax.fori_loop(0, tb, step, (sr_ref[...], si_ref[...]), unroll=8)
        sr_ref[...] = xr
        si_ref[...] = xi

    blk = pl.BlockSpec((tb, c), lambda i: (i, 0))
    one = _full_spec((1, c))
    return pl.pallas_call(
        body, grid=(t // tb,), in_specs=[one, one, blk, blk], out_specs=[blk, blk],
        out_shape=[SDS((t, c), F32), SDS((t, c), F32)],
        scratch_shapes=[pltpu.VMEM((1, c), F32), pltpu.VMEM((1, c), F32)], compiler_params=_cparams(), name=name,
    )(ar, ai, br, bi)


def _s5_scan_bwd(ar, ai, xr, xi, dxr, dxi, *, tb, name):
    t, c = xr.shape
    nb = t // tb
    r8 = tb // 8

    def body(ar_ref, ai_ref, xr_ref, xi_ref, pr_ref, pi_ref, gr_ref, gi_ref,
             dbr_ref, dbi_ref, dar_ref, dai_ref, cr_ref, ci_ref):
        i = pl.program_id(0)

        @pl.when(i == 0)
        def _():
            cr_ref[...] = jnp.zeros_like(cr_ref)
            ci_ref[...] = jnp.zeros_like(ci_ref)
            dar_ref[...] = jnp.zeros_like(dar_ref)
            dai_ref[...] = jnp.zeros_like(dai_ref)

        ar_, ai_ = ar_ref[...], ai_ref[...]
        first = i == nb - 1
        pr0 = jnp.where(first, 0.0, pr_ref[7:8, :])
        pi0 = jnp.where(first, 0.0, pi_ref[7:8, :])

        def step(q, carry):
            cr, ci, dar, dai = carry
            s = tb - 1 - q
            gr = gr_ref[pl.ds(s, 1), :] + cr
            gi = gi_ref[pl.ds(s, 1), :] + ci
            sp = jnp.maximum(s - 1, 0)
            xpr = jnp.where(s > 0, xr_ref[pl.ds(sp, 1), :], pr0)
            xpi = jnp.where(s > 0, xi_ref[pl.ds(sp, 1), :], pi0)
            dbr_ref[pl.ds(s, 1), :] = gr
            dbi_ref[pl.ds(s, 1), :] = gi
            dar = dar + gr * xpr + gi * xpi
            dai = dai - gr * xpi + gi * xpr
            return ar_ * gr + ai_ * gi, ar_ * gi - ai_ * gr, dar, dai

        cr, ci, dar, dai = lax.fori_loop(0, tb, step, (cr_ref[...], ci_ref[...], dar_ref[...], dai_ref[...]), unroll=8)
        cr_ref[...] = cr
        ci_ref[...] = ci
        dar_ref[...] = dar
        dai_ref[...] = dai

    rev = pl.BlockSpec((tb, c), lambda i: (nb - 1 - i, 0))
    prev = pl.BlockSpec((8, c), lambda i: (jnp.maximum((nb - 1 - i) * r8 - 1, 0), 0))
    one = _full_spec((1, c))
    return pl.pallas_call(
        body, grid=(nb,), in_specs=[one, one, rev, rev, prev, prev, rev, rev], out_specs=[rev, rev, one, one],
        out_shape=[SDS((t, c), F32), SDS((t, c), F32), SDS((1, c), F32), SDS((1, c), F32)],
        scratch_shapes=[pltpu.VMEM((1, c), F32), pltpu.VMEM((1, c), F32)], compiler_params=_cparams(), name=name,
    )(ar, ai, xr, xi, xr, xi, dxr, dxi)


RW_PAIRS = RW_H // 2


def _pair_consts():
    sub = lax.broadcasted_iota(jnp.int32, (64, 128), 0)
    lane = lax.broadcasted_iota(jnp.int32, (64, 128), 1)
    eye2 = ((lane & 63) == sub).astype(F32)
    r2 = lax.broadcasted_iota(jnp.int32, (128, 128), 0)
    c2 = lax.broadcasted_iota(jnp.int32, (128, 128), 1)
    bsel = ((r2 >> 6) == (c2 >> 6)).astype(BF16)
    return eye2, bsel


def _segsum(x, bsel):
    rows = x.shape[0]
    bits = lax.bitcast_convert_type(x, jnp.int32)
    hi = lax.bitcast_convert_type(bits & jnp.int32(-65536), F32)
    both = jnp.concatenate([hi.astype(BF16), (x - hi).astype(BF16)], axis=0)
    res = jnp.dot(both, bsel, preferred_element_type=F32)
    return res[:rows] + res[rows:]


def _bc(x8):
    return jnp.stack([jnp.broadcast_to(x8[q:q + 1, :], (64, 128)) for q in range(RW_PAIRS)])


def _seg3(x3, bsel):
    return _segsum(x3.reshape(RW_PAIRS * 64, 128), bsel).reshape(RW_PAIRS, 64, 128)


def _seg3_lanes(x3):
    first = lax.broadcasted_iota(jnp.int32, x3.shape, 2) < 64
    lo = jnp.sum(jnp.where(first, x3, 0.0), axis=-1, keepdims=True)
    hi = jnp.sum(jnp.where(first, 0.0, x3), axis=-1, keepdims=True)
    return jnp.where(first, lo, hi)


def _rwkv_scan_fwd(r, w, k, v, kk, a, *, lc, name):
    t = r.shape[0]
    nc = t // lc

    def body(r_ref, w_ref, k_ref, v_ref, kk_ref, a_ref, y_ref, ck_ref, st_ref):
        @pl.when(pl.program_id(0) == 0)
        def _():
            st_ref[...] = jnp.zeros_like(st_ref)

        ck_ref[0] = st_ref[...]
        eye2, bsel = _pair_consts()
        column = lambda ref, s: _seg3(eye2[None] * _bc(ref[s]), bsel)
        read = lambda st, s: jnp.sum(eye2[None] * _seg3(st * _bc(r_ref[s]), bsel), axis=1)

        def step(s, carry):
            st, vb = carry
            kk8 = kk_ref[s]
            sa = -_seg3_lanes(st * _bc(kk8))
            vb_next = column(v_ref, jnp.minimum(s + 1, lc - 1))
            before = jnp.maximum(s - 1, 0)
            y_ref[before] = read(st, before)
            return st * _bc(w_ref[s]) + sa * _bc(kk8 * a_ref[s]) + vb * _bc(k_ref[s]), vb_next

        st, _ = lax.fori_loop(0, lc, step, (st_ref[...], column(v_ref, 0)))
        y_ref[lc - 1] = read(st, lc - 1)
        st_ref[...] = st

    blk = pl.BlockSpec((lc, RW_PAIRS, 128), lambda i: (i, 0, 0))
    return pl.pallas_call(
        body, grid=(nc,), in_specs=[blk] * 6,
        out_specs=[blk, pl.BlockSpec((1, RW_PAIRS, 64, 128), lambda i: (i, 0, 0, 0))],
        out_shape=[SDS((t, RW_PAIRS, 128), F32), SDS((nc, RW_PAIRS, 64, 128), F32)],
        scratch_shapes=[pltpu.VMEM((RW_PAIRS, 64, 128), F32)],
        compiler_params=_cparams(), name=name,
    )(r, w, k, v, kk, a)


def _rwkv_scan_bwd(r, w, k, v, kk, a, ck, dy, *, lc, name):
    t = r.shape[0]
    nc = t // lc

    def body(r_ref, w_ref, k_ref, v_ref, kk_ref, a_ref, ck_ref, dy_ref,
             dr_ref, dw_ref, dk_ref, dv_ref, dkk_ref, da_ref,
             ds_ref, vb_ref, dyb_ref, hist_ref, sa_ref):
        @pl.when(pl.program_id(0) == 0)
        def _():
            ds_ref[...] = jnp.zeros_like(ds_ref)

        eye2, bsel = _pair_consts()
        column = lambda ref, s: _seg3(eye2[None] * _bc(ref[s]), bsel)
        hist_ref[0] = ck_ref[0]

        def fwd(s, carry):
            st, vb = carry
            kk8 = kk_ref[s]
            sa = -_seg3_lanes(st * _bc(kk8))
            vb_next = column(v_ref, jnp.minimum(s + 1, lc - 1))
            dyb_ref[s] = column(dy_ref, s)
            vb_ref[s] = vb
            sa_ref[s] = sa
            st = st * _bc(w_ref[s]) + sa * _bc(kk8 * a_ref[s]) + vb * _bc(k_ref[s])
            hist_ref[s + 1] = st
            return st, vb_next

        lax.fori_loop(0, lc, fwd, (ck_ref[0], column(v_ref, 0)))

        def grads(s, d_s, dsa):
            s_prev, s_cur = hist_ref[s], hist_ref[s + 1]
            col = lambda z: jnp.sum(z, axis=1)
            db = col(d_s * sa_ref[s])
            dr_ref[s] = col(s_cur * dyb_ref[s])
            dw_ref[s] = col(d_s * s_prev)
            dv_ref[s] = col(eye2[None] * _seg3(d_s * _bc(k_ref[s]), bsel))
            dk_ref[s] = col(d_s * vb_ref[s])
            dkk_ref[s] = db * a_ref[s] - col(s_prev * dsa)
            da_ref[s] = db * kk_ref[s]

        def back(j, carry):
            ds, d_after, dsa_after = carry
            s = lc - 1 - j
            kk8 = kk_ref[s]
            d_s = ds + dyb_ref[s] * _bc(r_ref[s])
            dsa = _seg3_lanes(d_s * _bc(kk8 * a_ref[s]))
            grads(jnp.minimum(s + 1, lc - 1), d_after, dsa_after)
            return d_s * _bc(w_ref[s]) - dsa * _bc(kk8), d_s, dsa

        zero = jnp.zeros((RW_PAIRS, 64, 128), F32)
        ds, d_first, dsa_first = lax.fori_loop(0, lc, back, (ds_ref[...], zero, zero))
        grads(0, d_first, dsa_first)
        ds_ref[...] = ds

    rev = pl.BlockSpec((lc, RW_PAIRS, 128), lambda i: (nc - 1 - i, 0, 0))
    big = lambda n: pltpu.VMEM((n, RW_PAIRS, 64, 128), F32)
    return pl.pallas_call(
        body, grid=(nc,),
        in_specs=[rev] * 6 + [pl.BlockSpec((1, RW_PAIRS, 64, 128), lambda i: (nc - 1 - i, 0, 0, 0)), rev],
        out_specs=[rev] * 6, out_shape=[SDS((t, RW_PAIRS, 128), F32)] * 6,
        scratch_shapes=[pltpu.VMEM((RW_PAIRS, 64, 128), F32), big(lc), big(lc), big(lc + 1), big(lc)],
        compiler_params=_cparams(), name=name,
    )(r, w, k, v, kk, a, ck, dy)


SSD_PAIRS = SSD_H // 2


def _ssd_chunk(states, xdt, da, bm, cm):
    ln = SSD_L
    row = lax.broadcasted_iota(jnp.int32, (ln, ln), 0)
    col = lax.broadcasted_iota(jnp.int32, (ln, ln), 1)
    causal = row >= col
    acum = _sel_dot(causal.astype(F32), da, NN, 1)
    acum_t = _sel_dot(da, (row <= col).astype(F32), TN, 0)
    sub = lax.broadcasted_iota(jnp.int32, (128, 128), 0)
    lane = lax.broadcasted_iota(jnp.int32, (128, 128), 1)
    ys, new_states = [], []
    for q in range(SSD_PAIRS):
        g = q // (SSD_PAIRS // SSD_NG)
        bg = bm[:, g * SSD_N:(g + 1) * SSD_N]
        cg = cm[:, g * SSD_N:(g + 1) * SSD_N]
        xq = xdt[:, q * 128:(q + 1) * 128]
        scores = _dot16(cg, bg, NT)
        aexp = _sel_dot(acum, (sub == 2 * q + (lane >> 6)).astype(F32), NN, 0)
        tot = aexp[ln - 1:ln, :]
        yh = []
        for h in (2 * q, 2 * q + 1):
            seg = _sel_dot(acum, (sub == h).astype(F32), NN, 0) - acum_t[h:h + 1, :]
            yh.append(_dot16(scores * jnp.exp(jnp.where(causal, seg, -1e30)), xq))
        y = jnp.where(lane < 64, yh[0], yh[1]) + _dot16(cg, states[q]) * jnp.exp(aexp)
        new = _dot16(bg, xq * jnp.exp(tot - aexp), TN)
        ys.append(y)
        new_states.append(states[q] * jnp.exp(tot) + new)
    return jnp.concatenate(ys, axis=1), new_states


def _ssd_fwd(xdt, da, bm, cm, *, name):
    t = xdt.shape[0]
    nc = t // SSD_L

    def body(x_ref, a_ref, b_ref, c_ref, y_ref, ck_ref, st_ref):
        @pl.when(pl.program_id(0) == 0)
        def _():
            st_ref[...] = jnp.zeros_like(st_ref)

        ck_ref[0] = st_ref[...]
        y, new = _ssd_chunk([st_ref[q] for q in range(SSD_PAIRS)], x_ref[...], a_ref[...], b_ref[...], c_ref[...])
        y_ref[...] = y
        for q in range(SSD_PAIRS):
            st_ref[q] = new[q]

    blk = lambda wd: pl.BlockSpec((SSD_L, wd), lambda i: (i, 0))
    return pl.pallas_call(
        body, grid=(nc,), in_specs=[blk(SSD_W), blk(128), blk(512), blk(512)],
        out_specs=[blk(SSD_W), pl.BlockSpec((1, SSD_PAIRS, 128, 128), lambda i: (i, 0, 0, 0))],
        out_shape=[SDS((t, SSD_W), F32), SDS((nc, SSD_PAIRS, 128, 128), F32)],
        scratch_shapes=[pltpu.VMEM((SSD_PAIRS, 128, 128), F32)], compiler_params=_cparams(), name=name,
    )(xdt, da, bm, cm)


def _ssd_bwd(xdt, da, bm, cm, ck, dy, *, name):
    t = xdt.shape[0]
    nc = t // SSD_L

    def body(x_ref, a_ref, b_ref, c_ref, ck_ref, dy_ref, dx_ref, dda_ref, db_ref, dc_ref, ds_ref):
        @pl.when(pl.program_id(0) == 0)
        def _():
            ds_ref[...] = jnp.zeros_like(ds_ref)

        _, pull = jax.vjp(_ssd_chunk, [ck_ref[0, q] for q in range(SSD_PAIRS)], x_ref[...], a_ref[...], b_ref[...], c_ref[...])
        dst, dx, dda, db, dc = pull((dy_ref[...], [ds_ref[q] for q in range(SSD_PAIRS)]))
        dx_ref[...] = dx
        dda_ref[...] = dda
        db_ref[...] = db
        dc_ref[...] = dc
        for q in range(SSD_PAIRS):
            ds_ref[q] = dst[q]

    rev = lambda wd: pl.BlockSpec((SSD_L, wd), lambda i: (nc - 1 - i, 0))
    return pl.pallas_call(
        body, grid=(nc,),
        in_specs=[rev(SSD_W), rev(128), rev(512), rev(512),
                  pl.BlockSpec((1, SSD_PAIRS, 128, 128), lambda i: (nc - 1 - i, 0, 0, 0)), rev(SSD_W)],
        out_specs=[rev(SSD_W), rev(128), rev(512), rev(512)],
        out_shape=[SDS((t, SSD_W), F32), SDS((t, 128), F32), SDS((t, 512), F32), SDS((t, 512), F32)],
        scratch_shapes=[pltpu.VMEM((SSD_PAIRS, 128, 128), F32)], compiler_params=_cparams(), name=name,
    )(xdt, da, bm, cm, ck, dy)


def _iota(shape, dim):
    return lax.broadcasted_iota(jnp.int32, shape, dim)


def _rms(x, g):
    return x * lax.rsqrt(jnp.mean(x * x, axis=-1, keepdims=True) + EPS) * g


def _head_sel(width, shift):
    return ((_iota((width, 128), 0) >> shift) == _iota((width, 128), 1)).astype(F32)


def _head_sum(x, shift=6):
    sel = _head_sel(x.shape[1], shift)
    return _sel_dot(_sel_dot(x, sel, NN, 0), sel, NT, 0)


def _head_expand(x, width, shift=6):
    return _sel_dot(x, _head_sel(width, shift), NT, 0)


def f_norm(h, g):
    return (_rms(h, g),)


def f_norm_pass(h, g):
    return _rms(h, g), h


def f_add_norm(h, m, g):
    h1 = h + m
    return h1, _rms(h1, g)


def f_relu2(u):
    r = jnp.maximum(u, 0.0)
    return (r * r,)


def f_plgate(h2, gl, pp):
    return (h2 + jax.nn.sigmoid(gl) * pp,)


def f_loss(h, tgt, g):
    err = _rms(h, g) - tgt
    part = 0.5 * jnp.sum(jnp.mean(err * err, axis=-1, keepdims=True), axis=0, keepdims=True)
    return (jnp.broadcast_to(part, (8, 128)),)


def f_s5_prep(lam_re, lam_im, lstep, bre_t, bim_t, cre_t, cim_t):
    step = jnp.exp(_sel_dot(lstep, _head_sel(S5_N, 6), NT, 0)[0:1, :])
    mag = jnp.exp(lam_re * step)
    abar_re, abar_im = mag * jnp.cos(lam_im * step), mag * jnp.sin(lam_im * step)
    den = lam_re * lam_re + lam_im * lam_im
    nr = abar_re - 1.0
    coef_re = (nr * lam_re + abar_im * lam_im) / den
    coef_im = (abar_im * lam_re - nr * lam_im) / den
    bbar_re = coef_re * bre_t - coef_im * bim_t
    bbar_im = coef_re * bim_t + coef_im * bre_t
    rep = ((_iota((S5_W, S5_G), 0) & (S5_G - 1)) == _iota((S5_W, S5_G), 1)).astype(F32)
    blk = ((_iota((S5_W, S5_N), 0) >> 4) == (_iota((S5_W, S5_N), 1) >> 6)).astype(F32)
    blk_t = ((_iota((S5_N, S5_W), 0) >> 6) == (_iota((S5_N, S5_W), 1) >> 4)).astype(F32)
    wb_re, wb_im = _sel_dot(rep, bbar_re, NN, 1) * blk, _sel_dot(rep, bbar_im, NN, 1) * blk
    wc_re, wc_im = _sel_dot(cre_t, rep, NT, 0) * blk_t, _sel_dot(cim_t, rep, NT, 0) * blk_t
    return abar_re, abar_im, wb_re, wb_im, wc_re, wc_im


def f_s5_post(xr, xi, u, wc_re, wc_im, d_skip, glu_w, glu_b):
    y = _dot16(xr, wc_re) - _dot16(xi, wc_im) + d_skip * u
    act = jax.nn.gelu(y)
    return (act * jax.nn.sigmoid(_dot16(act, glu_w) + glu_b),)


def f_ssd_pre(xc, dtr, dt_bias, a_log):
    act = jax.nn.silu(xc)
    heads = _iota(dtr.shape, 1) < SSD_H
    dt = jnp.where(heads, jax.nn.softplus(dtr + dt_bias), 0.0)
    da = dt * (-jnp.exp(a_log))
    xdt = act[:, :SSD_W] * _head_expand(dt, SSD_W)
    return xdt, da, act[:, SSD_W:SSD_W + 512], act[:, SSD_W + 512:]


def f_ssd_pre_pass(xc, dtr, dt_bias, a_log):
    return f_ssd_pre(xc, dtr, dt_bias, a_log) + (xc,)


def f_ssd_post(y, xc, z, d_skip, norm_g):
    xs = jax.nn.silu(xc[:, :SSD_W])
    y = (y + xs * _head_expand(d_skip, SSD_W)) * jax.nn.silu(z)
    gw = SSD_W // SSD_NG
    parts = []
    for g in range(SSD_NG):
        seg = y[:, g * gw:(g + 1) * gw]
        parts.append(seg * lax.rsqrt(jnp.mean(seg * seg, axis=-1, keepdims=True) + EPS))
    return (jnp.concatenate(parts, axis=1) * norm_g,)


def f_rwkv_pre(f, w0, w_up, a0, a_up, g_up, k_k, k_a):
    r, k, v = f[:, 0:1024], f[:, 1024:2048], f[:, 2048:3072]
    wl, al, gl = f[:, 3072:3200], f[:, 3200:3328], f[:, 3328:3584]
    w = -jax.nn.softplus(-(w0 + _dot16(jnp.tanh(wl), w_up))) - 0.5
    decay = jnp.exp(-jnp.exp(w))
    a = jax.nn.sigmoid(a0 + _dot16(al, a_up))
    g = _dot16(jax.nn.sigmoid(gl), g_up)
    kk = k * k_k
    k2 = k * (1.0 + (a - 1.0) * k_a)
    kkn = kk * lax.rsqrt(jnp.maximum(_head_sum(kk * kk), 1e-24))
    return r, decay, k2, v, kkn, a, g


def f_rwkv_pre_pass(f, w0, w_up, a0, a_up, g_up, k_k, k_a):
    out = f_rwkv_pre(f, w0, w_up, a0, a_up, g_up, k_k, k_a)
    return out + (out[0], out[2], out[3])


def f_rwkv_post(y, r, k2, v, g, ln_g, ln_b, r_k):
    mean = _head_sum(y) * (1.0 / RW_HD)
    yc = y - mean
    var = _head_sum(yc * yc) * (1.0 / RW_HD)
    yn = yc * lax.rsqrt(var + GN_EPS) * ln_g + ln_b
    bonus = _head_sum(r * k2 * r_k) * v
    return ((yn + bonus) * g,)


def _neg_expm1(y):
    series = -y * (1.0 + y * (0.5 + y * (1.0 / 6.0 + y * (1.0 / 24.0 + y * (1.0 / 120.0)))))
    return jnp.where(y > -0.1, series, 1.0 - jnp.exp(y))


def f_lru_pre(t0, xc, w_a, b_a, w_x, b_x, lam):
    gate_r = jax.nn.sigmoid(_dot16(xc, w_a) + b_a)
    gate_i = jax.nn.sigmoid(_dot16(xc, w_x) + b_x)
    log_a = -LRU_C * gate_r * jax.nn.softplus(-lam)
    mult = jnp.sqrt(jnp.maximum(_neg_expm1(2.0 * log_a), 0.0))
    mult = jnp.where(_iota(xc.shape, 0) + t0 == 0, 1.0, mult)
    return jnp.exp(log_a), xc * gate_i * mult


def f_lru_post(h, gl):
    return (h * jax.nn.gelu(gl),)


TB = 256
TBH = 128
SCAN_TB = 256
RW_LC = 32


def _even_fwd(hn, w, tag):
    n = lambda s: f"{tag}_{s}"
    u = _mm(hn, w["in_u"], name=n("proj_u"))
    z = _mm(hn, w["in_z"], name=n("proj_z"))
    xbc = _mm(hn, w["in_xbc"], name=n("proj_xbc"))
    dtr = _mm(hn, w["in_dt"], name=n("proj_dt"))
    bu_re = _mm(u, w["wb_re"], name=n("s5_bu_re"))
    bu_im = _mm(u, w["wb_im"], name=n("s5_bu_im"))
    xr, xi = _s5_scan_fwd(w["abar_re"], w["abar_im"], bu_re, bu_im, tb=SCAN_TB, name=n("s5_scan"))
    s5c = [w["wc_re"], w["wc_im"], w["s5_d"], w["glu_w"], w["glu_b"]]
    (ya,) = _stage(f_s5_post, [xr, xi, u], s5c, tb=TB, name=n("s5_post"), out_dtypes=[BF16])
    xc = _conv_fwd(xbc, w["ssd_conv_w"], w["ssd_conv_b"], tb=TB, name=n("ssd_conv"))
    xdt, da, bm, cm = _stage(f_ssd_pre, [xc, dtr], [w["dt_bias"], w["a_log"]], tb=TB, name=n("ssd_pre"),
                             out_dtypes=[F32] * 4)
    y, ck = _ssd_fwd(xdt, da, bm, cm, name=n("ssd_scan"))
    (yb,) = _stage(f_ssd_post, [y, xc, z], [w["ssd_d"], w["ssd_norm"]], tb=TB, name=n("ssd_post"), out_dtypes=[BF16])
    mo = _mm(ya, w["out_a"], name=n("out_a"))
    mo = _mm(yb, w["out_b"], add=mo, name=n("out_b"))
    res = dict(u=u, z=z, xbc=xbc, dtr=dtr, xr=xr, xi=xi, ya=ya, xc=xc, xdt=xdt, da=da, bm=bm, cm=cm, y=y, ck=ck, yb=yb)
    return mo, res


def _even_bwd(dmo, hn, w, r, tag):
    n = lambda s: f"{tag}_{s}"
    g = {}
    g["out_a"] = _mm(r["ya"], dmo, ta=True, name=n("d_out_a"))
    g["out_b"] = _mm(r["yb"], dmo, ta=True, name=n("d_out_b"))
    dya = _mm(dmo, w["out_a"], tb=True, name=n("dya"))
    dyb = _mm(dmo, w["out_b"], tb=True, name=n("dyb"))
    dy, dxc1, dz, g["ssd_d"], g["ssd_norm"] = _stage_vjp(
        f_ssd_post, [r["y"], r["xc"], r["z"]], [w["ssd_d"], w["ssd_norm"]], [dyb], tb=TBH, name=n("ssd_post_b"),
        drow=[0, 1, 2], dconst=[0, 1])
    dxdt, dda, dbm, dcm = _ssd_bwd(r["xdt"], r["da"], r["bm"], r["cm"], r["ck"], dy, name=n("ssd_scan_b"))
    dxc, ddtr, g["dt_bias"], g["a_log"] = _stage_vjp(
        f_ssd_pre_pass, [r["xc"], r["dtr"]], [w["dt_bias"], w["a_log"]], [dxdt, dda, dbm, dcm, dxc1], tb=TBH,
        name=n("ssd_pre_b"), drow=[0, 1], dconst=[0, 1])
    dxbc, g["ssd_conv_w"], g["ssd_conv_b"] = _conv_bwd(r["xbc"], w["ssd_conv_w"], dxc, tb=TB, name=n("ssd_conv_b"))
    s5c = [w["wc_re"], w["wc_im"], w["s5_d"], w["glu_w"], w["glu_b"]]
    dxr, dxi, du1, g["wc_re"], g["wc_im"], g["s5_d"], g["glu_w"], g["glu_b"] = _stage_vjp(
        f_s5_post, [r["xr"], r["xi"], r["u"]], s5c, [dya], tb=TBH, name=n("s5_post_b"),
        drow=[0, 1, 2], dconst=[0, 1, 2, 3, 4])
    dbr, dbi, g["abar_re"], g["abar_im"] = _s5_scan_bwd(w["abar_re"], w["abar_im"], r["xr"], r["xi"], dxr, dxi,
                                                         tb=SCAN_TB, name=n("s5_scan_b"))
    g["wb_re"] = _mm(r["u"], dbr, ta=True, name=n("d_wb_re"))
    g["wb_im"] = _mm(r["u"], dbi, ta=True, name=n("d_wb_im"))
    du = _mm(dbr, w["wb_re"], tb=True, add=du1, name=n("du_re"))
    du = _mm(dbi, w["wb_im"], tb=True, add=du, name=n("du_im"))
    segs = (("in_u", du), ("in_z", dz), ("in_xbc", dxbc), ("in_dt", ddtr))
    dhn = None
    for key, dseg in segs:
        g[key] = _mm(hn, dseg, ta=True, name=n("d_" + key))
        dhn = _mm(dseg, w[key], tb=True, add=dhn, name=n("dhn_" + key))
    return dhn, g


def _odd_fwd(hn, w, tag):
    n = lambda s: f"{tag}_{s}"
    rw = _mm(hn, w["in_rw"], name=n("proj_rw"))
    xl = _mm(hn, w["in_xl"], name=n("proj_xl"))
    gl = _mm(hn, w["in_gl"], name=n("proj_gl"))
    f = _conv_fwd(rw, w["mix_w"], w["mix_b"], tb=TB, name=n("rwkv_shift"))
    rc = [w[k] for k in ("w0", "w_up", "a0", "a_up", "g_up", "k_k", "k_a")]
    r_, dec, k2, v, kkn, a, gate = _stage(f_rwkv_pre, [f], rc, tb=TB, name=n("rwkv_pre"), out_dtypes=[F32] * 7)
    t3 = lambda z: z.reshape(-1, RW_PAIRS, 128)
    y, ck = _rwkv_scan_fwd(t3(r_), t3(dec), t3(k2), t3(v), t3(kkn), t3(a), lc=RW_LC, name=n("rwkv_scan"))
    y = y.reshape(-1, RW_W)
    (yc,) = _stage(f_rwkv_post, [y, r_, k2, v, gate], [w["ln_g"], w["ln_b"], w["r_k"]], tb=TB, name=n("rwkv_post"),
                   out_dtypes=[BF16])
    xc = _conv_fwd(xl, w["lru_conv_w"], w["lru_conv_b"], tb=TB, name=n("lru_conv"))
    lc = [w[k] for k in ("lru_wa", "lru_b_a", "lru_wx", "lru_b_x", "lru_lam")]
    a_l, bx = _stage(f_lru_pre, [xc], lc, tb=TB, name=n("lru_pre"), out_dtypes=[F32] * 2, pos=True)
    h = _lru_scan_fwd(a_l, bx, tb=SCAN_TB, name=n("lru_scan"))
    (yd,) = _stage(f_lru_post, [h, gl], [], tb=TB, name=n("lru_post"), out_dtypes=[BF16])
    mo = _mm(yc, w["out_a"], name=n("out_a"))
    mo = _mm(yd, w["out_b"], add=mo, name=n("out_b"))
    res = dict(rw=rw, xl=xl, gl=gl, f=f, r=r_, dec=dec, k2=k2, v=v, kkn=kkn, a=a, gate=gate, y=y, ck=ck, yc=yc,
               xc=xc, a_l=a_l, h=h, yd=yd)
    return mo, res


def _odd_bwd(dmo, hn, w, r, tag):
    n = lambda s: f"{tag}_{s}"
    g = {}
    g["out_a"] = _mm(r["yc"], dmo, ta=True, name=n("d_out_a"))
    g["out_b"] = _mm(r["yd"], dmo, ta=True, name=n("d_out_b"))
    dyc = _mm(dmo, w["out_a"], tb=True, name=n("dyc"))
    dyd = _mm(dmo, w["out_b"], tb=True, name=n("dyd"))
    dh, dgl = _stage_vjp(f_lru_post, [r["h"], r["gl"]], [], [dyd], tb=TB, name=n("lru_post_b"), drow=[0, 1], dconst=[])
    da_l, dbx = _lru_scan_bwd(r["a_l"], r["h"], dh, tb=SCAN_TB, name=n("lru_scan_b"))
    lc = [w[k] for k in ("lru_wa", "lru_b_a", "lru_wx", "lru_b_x", "lru_lam")]
    dxc, g["lru_wa"], g["lru_b_a"], g["lru_wx"], g["lru_b_x"], g["lru_lam"] = _stage_vjp(
        f_lru_pre, [r["xc"]], lc, [da_l, dbx], tb=TBH, name=n("lru_pre_b"), drow=[0], dconst=[0, 1, 2, 3, 4], pos=True)
    dxl, g["lru_conv_w"], g["lru_conv_b"] = _conv_bwd(r["xl"], w["lru_conv_w"], dxc, tb=TB, name=n("lru_conv_b"))
    dy, dr1, dk1, dv1, dgate, g["ln_g"], g["ln_b"], g["r_k"] = _stage_vjp(
        f_rwkv_post, [r["y"], r["r"], r["k2"], r["v"], r["gate"]], [w["ln_g"], w["ln_b"], w["r_k"]], [dyc], tb=TBH,
        name=n("rwkv_post_b"), drow=[0, 1, 2, 3, 4], dconst=[0, 1, 2])
    t3 = lambda z: z.reshape(-1, RW_PAIRS, 128)
    dr2, ddec, dk2, dv2, dkkn, da = [z.reshape(-1, RW_W) for z in _rwkv_scan_bwd(
        t3(r["r"]), t3(r["dec"]), t3(r["k2"]), t3(r["v"]), t3(r["kkn"]), t3(r["a"]), r["ck"], t3(dy),
        lc=RW_LC, name=n("rwkv_scan_b"))]
    rc = [w[k] for k in ("w0", "w_up", "a0", "a_up", "g_up", "k_k", "k_a")]
    df, g["w0"], g["w_up"], g["a0"], g["a_up"], g["g_up"], g["k_k"], g["k_a"] = _stage_vjp(
        f_rwkv_pre_pass, [r["f"]], rc, [dr2, ddec, dk2, dv2, dkkn, da, dgate, dr1, dk1, dv1], tb=TBH,
        name=n("rwkv_pre_b"), drow=[0], dconst=[0, 1, 2, 3, 4, 5, 6])
    drw, g["mix_w"], _ = _conv_bwd(r["rw"], w["mix_w"], df, tb=TB, name=n("rwkv_shift_b"))
    segs = (("in_rw", drw), ("in_xl", dxl), ("in_gl", dgl))
    dhn = None
    for key, dseg in segs:
        g[key] = _mm(hn, dseg, ta=True, name=n("d_" + key))
        dhn = _mm(dseg, w[key], tb=True, add=dhn, name=n("dhn_" + key))
    return dhn, g


def _layer_fwd(h, p_i, w, odd, tag):
    n = lambda s: f"{tag}_{s}"
    (hn,) = _stage(f_norm, [h], [w["norm_mix"]], tb=TB, name=n("norm_mix"), out_dtypes=[BF16])
    mo, mres = (_odd_fwd if odd else _even_fwd)(hn, w, tag)
    h1, hf = _stage(f_add_norm, [h, mo], [w["norm_ffn"]], tb=TB, name=n("norm_ffn"), out_dtypes=[F32, BF16])
    u, act = _mm(hf, w["mlp_w1"], name=n("mlp_up"), epilogue=lambda acc: (acc,) + f_relu2(acc), out_dtypes=[F32, BF16])
    m2 = _mm(act, w["mlp_w2"], name=n("mlp_down"))
    h2, hp = _stage(f_add_norm, [h1, m2], [w["norm_pl"]], tb=TB, name=n("norm_pl"), out_dtypes=[F32, BF16])
    gl = _mm(hp, w["pl_gate"], name=n("pl_gate"))
    pp = _mm(p_i, w["pl_proj"], name=n("pl_proj"))
    (h3,) = _stage(f_plgate, [h2, gl, pp], [], tb=TB, name=n("pl_mix"), out_dtypes=[F32])
    res = dict(h=h, hn=hn, mo=mo, mix=mres, h1=h1, hf=hf, u=u, act=act, m2=m2, h2=h2, hp=hp, gl=gl, pp=pp)
    return h3, res


def _layer_bwd(dh3, p_i, w, r, odd, tag, stacks):
    n = lambda s: f"{tag}_{s}"
    g = {}
    wgrad = lambda key, x, dy, cols_cut, shard: _mm_grad(x, dy, layer=int(odd), cols_cut=cols_cut, shard=shard,
                                                        prev=stacks[key] if stacks else None, name=n("d_" + key))
    dh2, dgl, dpp = _stage_vjp(f_plgate, [r["h2"], r["gl"], r["pp"]], [], [dh3], tb=TB, name=n("pl_mix_b"),
                               drow=[0, 1, 2], dconst=[])
    g["pl_proj"] = wgrad("pl_proj", p_i, dpp, True, (PL_DIM, D // 4))
    g["pl_gate"] = wgrad("pl_gate", r["hp"], dgl, False, (D // 4, D))
    dhp = _mm(dgl, w["pl_gate"], tb=True, name=n("dhp"))
    dh1, dm2, g["norm_pl"] = _stage_vjp(f_add_norm, [r["h1"], r["m2"]], [w["norm_pl"]], [dh2, dhp], tb=TB,
                                        name=n("norm_pl_b"), drow=[0, 1], dconst=[0])
    g["mlp_w2"] = wgrad("mlp_w2", r["act"], dm2, False, (D_FF // 4, D))
    (du,) = _mm(dm2, w["mlp_w2"], tb=True, name=n("dact"), extra=[r["u"]], out_dtypes=[BF16],
                epilogue=lambda acc, u: (acc * (2.0 * jnp.maximum(u, 0.0)),))
    g["mlp_w1"] = wgrad("mlp_w1", r["hf"], du, True, (D, D_FF // 4))
    dhf = _mm(du, w["mlp_w1"], tb=True, name=n("dhf"))
    dh, dmo, g["norm_ffn"] = _stage_vjp(f_add_norm, [r["h"], r["mo"]], [w["norm_ffn"]], [dh1, dhf], tb=TB,
                                        name=n("norm_ffn_b"), drow=[0, 1], dconst=[0])
    dhn, gm = (_odd_bwd if odd else _even_bwd)(dmo, r["hn"], w, r["mix"], tag)
    g.update(gm)
    dh0, g["norm_mix"] = _stage_vjp(f_norm_pass, [r["h"]], [w["norm_mix"]], [dhn, dh], tb=TB, name=n("norm_mix_b"),
                                    drow=[0], dconst=[0])
    return dh0, g


def _pad_to(a, size, axis):
    pad = [(0, 0)] * a.ndim
    pad[axis] = (0, size - a.shape[axis])
    return jnp.pad(a, pad)


def _rw_pad(a):
    return jnp.concatenate([a[..., :3072], _pad_to(a[..., 3072:3168], 128, -1), _pad_to(a[..., 3168:3264], 128, -1),
                            a[..., 3264:3520]], axis=-1)


def _rw_unpad(a):
    return jnp.concatenate([a[..., :3072], a[..., 3072:3168], a[..., 3200:3296], a[..., 3328:3584]], axis=-1)


def _block_diag(w):
    nb, bs, _ = w.shape
    eye = jnp.eye(nb, dtype=w.dtype)
    return (w[:, :, None, :] * eye[:, None, :, None]).reshape(nb * bs, nb * bs)


def _diag_blocks(w):
    nb = LRU_B
    bs = w.shape[0] // nb
    return jnp.stack([w[h * bs:(h + 1) * bs, h * bs:(h + 1) * bs] for h in range(nb)])


def _s5_prep_inputs(fw):
    lstep = jnp.broadcast_to(_pad_to(fw["s5_log_step"].astype(F32), 128, 1), (8, 128))
    t16 = lambda b: jnp.transpose(b[0], (2, 0, 1)).reshape(S5_G, S5_N)
    tc = lambda c: jnp.transpose(c[0], (0, 2, 1)).reshape(S5_N, S5_G)
    return [fw["s5_lam_re"].reshape(1, S5_N), fw["s5_lam_im"].reshape(1, S5_N), lstep,
            t16(fw["s5_b_re"]), t16(fw["s5_b_im"]), tc(fw["s5_c_re"]), tc(fw["s5_c_im"])]


def _layer_weights(fw, i):
    w = {k: fw[k][i:i + 1] for k in ("norm_mix", "norm_ffn", "norm_pl")}
    for k in ("mlp_w1", "mlp_w2", "pl_proj", "pl_gate"):
        w[k] = (fw[k], i)
    return w


def _even_weights(fw, prep):
    w = _layer_weights(fw, 0)
    ein, eout = fw["e_in_proj"][0], fw["e_out_proj"][0]
    w.update(in_u=ein[:, :512], in_z=ein[:, 512:2048], in_xbc=ein[:, 2048:4608], in_dt=_pad_to(ein[:, 4608:], 128, 1),
             out_a=eout[:512], out_b=eout[512:])
    abar_re, abar_im, wb_re, wb_im, wc_re, wc_im = prep
    w.update(abar_re=abar_re, abar_im=abar_im, wb_re=wb_re, wb_im=wb_im, wc_re=wc_re.astype(BF16), wc_im=wc_im.astype(BF16),
             s5_d=fw["s5_d"], glu_w=fw["s5_glu_w"][0], glu_b=fw["s5_glu_b"],
             ssd_conv_w=_pad_to(fw["ssd_conv_w"][0], 8, 0), ssd_conv_b=fw["ssd_conv_b"],
             dt_bias=_pad_to(fw["ssd_dt_bias"], 128, 1), a_log=_pad_to(fw["ssd_a_log"], 128, 1),
             ssd_d=_pad_to(fw["ssd_d"], 128, 1), ssd_norm=fw["ssd_norm"])
    return w


def _odd_weights(fw):
    w = _layer_weights(fw, 1)
    oin, oout = fw["o_in_proj"][0], fw["o_out_proj"][0]
    mu = _rw_pad(fw["rwkv_mu"])
    zero = jnp.zeros_like(mu)
    w.update(in_rw=_rw_pad(oin[:, :RW_IN]), in_xl=oin[:, RW_IN:RW_IN + LRU_W], in_gl=oin[:, RW_IN + LRU_W:],
             out_a=oout[:RW_W], out_b=oout[RW_W:],
             mix_w=jnp.concatenate([zero, zero, mu, 1.0 - mu, zero, zero, zero, zero], axis=0), mix_b=zero,
             w0=fw["rwkv_w0"], w_up=_pad_to(fw["rwkv_w_up"][0], 128, 0), a0=fw["rwkv_a0"],
             a_up=_pad_to(fw["rwkv_a_up"][0], 128, 0), g_up=fw["rwkv_g_up"][0], k_k=fw["rwkv_k_k"], k_a=fw["rwkv_k_a"],
             r_k=fw["rwkv_r_k"].reshape(1, RW_W), ln_g=fw["rwkv_ln_g"], ln_b=fw["rwkv_ln_b"],
             lru_conv_w=_pad_to(fw["lru_conv_w"][0], 8, 0), lru_conv_b=fw["lru_conv_b"],
             lru_wa=_block_diag(fw["lru_w_a"][0]).astype(BF16), lru_b_a=fw["lru_b_a"].reshape(1, LRU_W),
             lru_wx=_block_diag(fw["lru_w_x"][0]).astype(BF16), lru_b_x=fw["lru_b_x"].reshape(1, LRU_W),
             lru_lam=fw["lru_lam"].reshape(1, LRU_W))
    return w


def _global_grads(g0, g1, s5_grads, d_norm_final):
    out = {k: jnp.concatenate([g0[k], g1[k]], axis=0) for k in ("norm_mix", "norm_ffn", "norm_pl")}
    for k in STACKED:
        out[k] = g0[k]
    out["e_in_proj"] = jnp.concatenate([g0["in_u"], g0["in_z"], g0["in_xbc"], g0["in_dt"][:, :SSD_H]], axis=1)[None]
    out["e_out_proj"] = jnp.concatenate([g0["out_a"], g0["out_b"]], axis=0)[None]
    d_lam_re, d_lam_im, d_lstep, d_bre, d_bim, d_cre, d_cim = s5_grads
    out["s5_lam_re"] = d_lam_re.reshape(1, S5_GROUPS, S5_P)
    out["s5_lam_im"] = d_lam_im.reshape(1, S5_GROUPS, S5_P)
    out["s5_log_step"] = d_lstep[0:1, :S5_GROUPS]
    unb = lambda b: jnp.transpose(b.reshape(S5_G, S5_GROUPS, S5_P), (1, 2, 0))[None]
    unc = lambda c: jnp.transpose(c.reshape(S5_GROUPS, S5_P, S5_G), (0, 2, 1))[None]
    out.update(s5_b_re=unb(d_bre), s5_b_im=unb(d_bim), s5_c_re=unc(d_cre), s5_c_im=unc(d_cim),
               s5_d=g0["s5_d"], s5_glu_w=g0["glu_w"][None], s5_glu_b=g0["glu_b"],
               ssd_conv_w=g0["ssd_conv_w"][None, :4], ssd_conv_b=g0["ssd_conv_b"], ssd_dt_bias=g0["dt_bias"][:, :SSD_H],
               ssd_a_log=g0["a_log"][:, :SSD_H], ssd_d=g0["ssd_d"][:, :SSD_H], ssd_norm=g0["ssd_norm"])
    out["o_in_proj"] = jnp.concatenate([_rw_unpad(g1["in_rw"]), g1["in_xl"], g1["in_gl"]], axis=1)[None]
    out["o_out_proj"] = jnp.concatenate([g1["out_a"], g1["out_b"]], axis=0)[None]
    out.update(rwkv_mu=_rw_unpad(g1["mix_w"][2:3] - g1["mix_w"][3:4]), rwkv_w0=g1["w0"], rwkv_w_up=g1["w_up"][None, :RW_LORA],
               rwkv_a0=g1["a0"], rwkv_a_up=g1["a_up"][None, :RW_LORA], rwkv_g_up=g1["g_up"][None], rwkv_k_k=g1["k_k"],
               rwkv_k_a=g1["k_a"], rwkv_r_k=g1["r_k"].reshape(1, RW_H, RW_HD), rwkv_ln_g=g1["ln_g"], rwkv_ln_b=g1["ln_b"],
               lru_conv_w=g1["lru_conv_w"][None, :4], lru_conv_b=g1["lru_conv_b"],
               lru_w_a=_diag_blocks(g1["lru_wa"])[None], lru_b_a=g1["lru_b_a"].reshape(1, LRU_B, 64),
               lru_w_x=_diag_blocks(g1["lru_wx"])[None], lru_b_x=g1["lru_b_x"].reshape(1, LRU_B, 64),
               lru_lam=g1["lru_lam"].reshape(1, LRU_B, 64), norm_final=d_norm_final.reshape(D))
    return out


def _local_step(x, p, target, fw):
    prep_in = _s5_prep_inputs(fw)
    prep = _single(f_s5_prep, prep_in, name="s5_prep")
    w0, w1 = _even_weights(fw, prep), _odd_weights(fw)
    h1, r0 = _layer_fwd(x, p[0], w0, False, "l0")
    h2, r1 = _layer_fwd(h1, p[1], w1, True, "l1")
    gf = fw["norm_final"].reshape(1, D)
    (loss8,) = _stage(f_loss, [h2, target], [gf], tb=TB, name="loss", out_dtypes=[], n_acc=1)
    one = jnp.zeros((8, 128), F32).at[0, 0].set(1.0)
    dh2, d_gf = _stage_vjp(f_loss, [h2, target], [gf], [], tb=TB, name="loss_b", drow=[0], dconst=[0], acc_cots=[one])
    dh1, g1 = _layer_bwd(dh2, p[1], w1, r1, True, "l1", None)
    dx, g0 = _layer_bwd(dh1, p[0], w0, r0, False, "l0", g1)
    cots = [g0[k] for k in ("abar_re", "abar_im", "wb_re", "wb_im", "wc_re", "wc_im")]
    s5_grads = _single_vjp(f_s5_prep, prep_in, cots, name="s5_prep_b")
    return loss8[0, 0], dx, _global_grads(g0, g1, s5_grads, d_gf)


def _xyc():
    return lax.axis_index("x"), lax.axis_index("y"), lax.axis_index("c")


def _flip(v, bit):
    return 1 - v if bit else v


def _remote(src, dst, send_sems, recv_sems, k, dev):
    return pltpu.make_async_remote_copy(src_ref=src, dst_ref=dst, send_sem=send_sems.at[k], recv_sem=recv_sems.at[k],
                                        device_id=dev, device_id_type=MESH)


CHIP_FLIPS = ((1, 0), (0, 1), (1, 1))


def _gather_chips(arrs, out_shapes, places, *, name):
    n = len(arrs)

    def body(*refs):
        ins, outs = refs[:n], refs[n:2 * n]
        send_sems, recv_sems = refs[2 * n:]
        x, y, c = _xyc()
        chip, sib = 2 * x + y, (x, y, 1 - c)
        peers = [(_flip(x, fx), _flip(y, fy)) for fx, fy in CHIP_FLIPS]
        first = [_remote(ins[a].at[c], places[a](outs[a], chip, c), send_sems, recv_sems, 6 * a + j, (px, py, c))
                 for a in range(n) for j, (px, py) in enumerate(peers)]
        for cp in first:
            cp.start()
        passed = []
        for a in range(n):
            for j, (px, py) in enumerate(peers):
                landed = places[a](outs[a], 2 * px + py, c)
                _remote(ins[a].at[c], landed, send_sems, recv_sems, 6 * a + j, (px, py, c)).wait_recv()
                cp = _remote(landed, landed, send_sems, recv_sems, 6 * a + 3 + j, sib)
                cp.start()
                passed.append(cp)
        for a in range(n):
            for j, (px, py) in enumerate(peers):
                other = places[a](outs[a], 2 * px + py, 1 - c)
                _remote(other, other, send_sems, recv_sems, 6 * a + 3 + j, sib).wait_recv()
        for cp in first + passed:
            cp.wait_send()

    return pl.pallas_call(
        body, out_shape=[SDS(s, a.dtype) for s, a in zip(out_shapes, arrs)], in_specs=[ANY] * n, out_specs=[ANY] * n,
        scratch_shapes=_dma_sems(6 * n), name=name,
    )(*arrs)


def _place_own(full, own, chip_vec, axis, *, name):
    layers, rows, cols = own.shape
    tb = min(rows, 512)
    per = rows // tb
    omap = ((lambda l, i, chip_ref: (l, chip_ref[0] * per + i, 0)) if axis == 1
            else (lambda l, i, chip_ref: (l, i, chip_ref[0])))

    def body(chip_ref, own_ref, full_ref, o_ref):
        o_ref[...] = own_ref[...]

    return pl.pallas_call(
        body,
        grid_spec=pltpu.PrefetchScalarGridSpec(
            num_scalar_prefetch=1, grid=(layers, per),
            in_specs=[pl.BlockSpec((None, tb, cols), lambda l, i, chip_ref: (l, i, 0)), ANY],
            out_specs=pl.BlockSpec((None, tb, cols), omap)),
        out_shape=SDS(full.shape, full.dtype), input_output_aliases={2: 0},
        compiler_params=_cparams(("arbitrary", "arbitrary")), name=name,
    )(chip_vec, own, full)


def _dma_sems(n):
    return [pltpu.SemaphoreType.DMA((n,)), pltpu.SemaphoreType.DMA((n,))]


def _send_halves(arrs, *, name):
    n = len(arrs)

    def body(*refs):
        ins, outs = refs[:n], refs[n:2 * n]
        send_sems, recv_sems = refs[2 * n:]
        x, y, c = _xyc()
        copies = [_remote(ins[a].at[k, 1 - c], outs[a].at[k], send_sems, recv_sems, 4 * a + k, (x, y, 1 - c))
                  for a in range(n) for k in range(arrs[a].shape[0])]
        for cp in copies:
            cp.start()
        for cp in copies:
            cp.wait_recv()
        for cp in copies:
            cp.wait_send()

    return pl.pallas_call(
        body, out_shape=[SDS(a.shape[:1] + a.shape[2:], a.dtype) for a in arrs], in_specs=[ANY] * n, out_specs=[ANY] * n,
        scratch_shapes=_dma_sems(4 * n), name=name,
    )(*arrs)


def _add_half(g, recv, c_vec, *, tb, out_dtype, name):
    slots, _, rh, cols = g.shape
    tb = min(tb, rh)

    def body(c_ref, g_ref, r_ref, o_ref):
        o_ref[...] = (g_ref[...] + r_ref[...]).astype(o_ref.dtype)

    return pl.pallas_call(
        body,
        grid_spec=pltpu.PrefetchScalarGridSpec(
            num_scalar_prefetch=1, grid=(slots, rh // tb),
            in_specs=[pl.BlockSpec((None, None, tb, cols), lambda k, i, c_ref: (k, c_ref[0], i, 0)),
                      pl.BlockSpec((None, tb, cols), lambda k, i, c_ref: (k, i, 0))],
            out_specs=pl.BlockSpec((None, tb, cols), lambda k, i, c_ref: (k, i, 0))),
        out_shape=SDS((slots, rh, cols), out_dtype), compiler_params=_cparams(("arbitrary", "arbitrary")), name=name,
    )(c_vec, g, recv)


def _scatter_chips(arrs, *, name):
    n = len(arrs)

    def body(*refs):
        ins, outs = refs[:n], refs[n:2 * n]
        send_sems, recv_sems = refs[2 * n:]
        x, y, c = _xyc()
        copies = []
        for a in range(n):
            for j, (fx, fy) in enumerate(CHIP_FLIPS):
                px, py = _flip(x, fx), _flip(y, fy)
                mine = ins[a].at[2 * px + py if arrs[a].shape[0] == 4 else 0]
                copies.append(_remote(mine, outs[a].at[j], send_sems, recv_sems, 3 * a + j, (px, py, c)))
        for cp in copies:
            cp.start()
        for cp in copies:
            cp.wait_recv()
        for cp in copies:
            cp.wait_send()

    return pl.pallas_call(
        body, out_shape=[SDS((3,) + a.shape[1:], a.dtype) for a in arrs], in_specs=[ANY] * n, out_specs=[ANY] * n,
        scratch_shapes=_dma_sems(3 * n), name=name,
    )(*arrs)


def _sum_chips(p, landed, chip_vec, *, tb, name):
    _, rh, cols = p.shape
    tb = min(tb, rh)

    def body(chip_ref, p_ref, l_ref, o_ref):
        f = lambda z: z.astype(F32)
        o_ref[...] = ((f(p_ref[...]) + f(l_ref[0])) + f(l_ref[1])) + f(l_ref[2])

    return pl.pallas_call(
        body,
        grid_spec=pltpu.PrefetchScalarGridSpec(
            num_scalar_prefetch=1, grid=(rh // tb,),
            in_specs=[pl.BlockSpec((None, tb, cols), lambda i, chip_ref: (chip_ref[0], i, 0)),
                      pl.BlockSpec((3, tb, cols), lambda i, chip_ref: (0, i, 0))],
            out_specs=pl.BlockSpec((tb, cols), lambda i, chip_ref: (i, 0))),
        out_shape=SDS((rh, cols), F32), compiler_params=_cparams(), name=name,
    )(chip_vec, p, landed)


def _sum_chips_ordered(p, landed, chip_vec, *, tb, name):
    _, rh, cols = p.shape
    tb = min(tb, rh)

    def body(chip_ref, p_ref, l_ref, o_ref):
        chip = chip_ref[0]
        acc = None
        for k in range(4):
            away = k ^ chip
            slot = jnp.where(away == 2, 0, jnp.where(away == 1, 1, 2))
            term = jnp.where(k == chip, p_ref[...], l_ref[slot])
            acc = term if acc is None else acc + term
        o_ref[...] = acc

    return pl.pallas_call(
        body,
        grid_spec=pltpu.PrefetchScalarGridSpec(
            num_scalar_prefetch=1, grid=(rh // tb,),
            in_specs=[pl.BlockSpec((None, tb, cols), lambda i, chip_ref: (0, i, 0)),
                      pl.BlockSpec((3, tb, cols), lambda i, chip_ref: (0, i, 0))],
            out_specs=pl.BlockSpec((tb, cols), lambda i, chip_ref: (i, 0))),
        out_shape=SDS((rh, cols), F32), compiler_params=_cparams(), name=name,
    )(chip_vec, p, landed)


def _swap_halves(arrs, *, name):
    n = len(arrs)

    def body(*refs):
        ins, outs = refs[:n], refs[n:2 * n]
        send_sems, recv_sems = refs[2 * n:]
        x, y, c = _xyc()
        copies = [_remote(ins[a], outs[a], send_sems, recv_sems, a, (x, y, 1 - c)) for a in range(n)]
        for cp in copies:
            cp.start()
        for cp in copies:
            cp.wait_recv()
        for cp in copies:
            cp.wait_send()

    return pl.pallas_call(
        body, out_shape=[SDS(a.shape, a.dtype) for a in arrs], in_specs=[ANY] * n, out_specs=[ANY] * n,
        scratch_shapes=_dma_sems(n), name=name,
    )(*arrs)


def _join_halves(mine, theirs, c_vec, *, tb, name):
    rh, cols = mine.shape
    tb = min(tb, rh)

    def body(c_ref, m_ref, t_ref, o_ref):
        o_ref[...] = jnp.where(pl.program_id(0) == c_ref[0], m_ref[...], t_ref[...])

    blk = pl.BlockSpec((tb, cols), lambda h, i, c_ref: (i, 0))
    return pl.pallas_call(
        body,
        grid_spec=pltpu.PrefetchScalarGridSpec(
            num_scalar_prefetch=1, grid=(2, rh // tb), in_specs=[blk, blk],
            out_specs=pl.BlockSpec((None, tb, cols), lambda h, i, c_ref: (h, i, 0))),
        out_shape=SDS((2, rh, cols), mine.dtype), compiler_params=_cparams(("arbitrary", "arbitrary")), name=name,
    )(c_vec, mine, theirs)


def f_adamw(w, g, m, v):
    m = ADAM_B1 * m + (1.0 - ADAM_B1) * g
    v = ADAM_B2 * v + (1.0 - ADAM_B2) * (g * g)
    m_hat = m / (1.0 - ADAM_B1 ** ADAM_STEP)
    v_hat = v / (1.0 - ADAM_B2 ** ADAM_STEP)
    return -ADAM_LR * (m_hat / (jnp.sqrt(v_hat) + ADAM_EPS) + ADAM_WD * w), m, v


def _adamw(w, g, m, v, *, name):
    shape = w.shape
    two = lambda a: a.reshape(-1, shape[-1])
    rows = two(w).shape[0]
    tb = 256 if rows % 256 == 0 else rows
    outs = _stage(f_adamw, [two(w), two(g), two(m), two(v)], [], tb=tb, name=name, out_dtypes=[F32] * 3)
    return [o.reshape(shape) for o in outs]


def _pack(arrs, rows=8):
    flat = jnp.concatenate([a.astype(F32).reshape(-1) for a in arrs])
    size = -(-flat.shape[0] // (rows * 128)) * (rows * 128)
    return _pad_to(flat, size, 0).reshape(-1, 128)


def _unpack(buf, shapes):
    flat = buf.reshape(-1)
    out, off = [], 0
    for s in shapes:
        n = math.prod(s)
        out.append(flat[off:off + n].reshape(s))
        off += n
    return out


WEIGHTS = ("norm_mix", "norm_ffn", "norm_pl", "mlp_w1", "mlp_w2", "pl_proj", "pl_gate", "e_in_proj", "e_out_proj",
           "s5_lam_re", "s5_lam_im", "s5_log_step", "s5_b_re", "s5_b_im", "s5_c_re", "s5_c_im", "s5_d", "s5_glu_w",
           "s5_glu_b", "ssd_conv_w", "ssd_conv_b", "ssd_dt_bias", "ssd_a_log", "ssd_d", "ssd_norm", "o_in_proj",
           "o_out_proj", "rwkv_mu", "rwkv_w0", "rwkv_w_up", "rwkv_a0", "rwkv_a_up", "rwkv_g_up", "rwkv_k_k", "rwkv_k_a",
           "rwkv_r_k", "rwkv_ln_g", "rwkv_ln_b", "lru_conv_w", "lru_conv_b", "lru_w_a", "lru_b_a", "lru_w_x", "lru_b_x",
           "lru_lam", "norm_final")
BIG = ("mlp_w1", "mlp_w2", "pl_proj", "pl_gate", "e_in_proj", "e_out_proj", "o_in_proj", "o_out_proj")
STACKED = BIG[:4]
SHARD_AXIS = {"mlp_w1": 2, "mlp_w2": 1, "pl_proj": 2, "pl_gate": 1, "e_in_proj": 2, "e_out_proj": 1, "s5_glu_w": 1,
              "ssd_conv_w": 2, "o_in_proj": 2, "o_out_proj": 1, "rwkv_mu": 1, "rwkv_w0": 1, "rwkv_w_up": 2, "rwkv_a0": 1,
              "rwkv_a_up": 2, "rwkv_g_up": 2, "rwkv_k_k": 1, "rwkv_k_a": 1, "rwkv_ln_g": 1, "rwkv_ln_b": 1,
              "lru_conv_w": 2, "lru_conv_b": 1}
SMALL = tuple(n for n in WEIGHTS if n not in BIG)
SMALL_SHARDED = tuple(n for n in SMALL if n in SHARD_AXIS)


def _gather_weights(w):
    shapes = [w[n].shape for n in SMALL_SHARDED]
    chip = 2 * lax.axis_index("x") + lax.axis_index("y")
    mine = [w[n].astype(BF16) for n in BIG] + [_pack([w[n] for n in SMALL_SHARDED], rows=16)]
    out_shapes, places = [], []
    for n, a in zip(BIG + ("small",), mine):
        layers, rows, cols = a.shape if a.ndim == 3 else (1,) + a.shape
        ax = SHARD_AXIS.get(n)
        if ax == 1:
            step = rows if layers == 2 else rows // 2
            out_shapes.append((layers, 4 * rows, cols))
            places.append(lambda o, k, h, layers=layers, rows=rows, step=step: o.at[
                h if layers == 2 else 0, pl.ds(pl.multiple_of(k * rows + (0 if layers == 2 else h * step), 16), step), :])
        elif ax == 2 and layers == 2:
            out_shapes.append((layers, rows, 4 * cols))
            places.append(lambda o, k, h, cols=cols: o.at[h, :, pl.ds(pl.multiple_of(k * cols, 128), cols)])
        else:
            out_shapes.append((4, 2, layers * rows // 2, cols))
            places.append(lambda o, k, h: o.at[k, h])
    got = _gather_chips([a.reshape(2, -1, a.shape[-1]) for a in mine], out_shapes, places, name="gather_weights")
    fw = {n: w[n] for n in SMALL if n not in SHARD_AXIS}
    for n, g, a in zip(BIG, got[:-1], mine):
        if g.shape[0] == 4:
            g = lax.dynamic_update_index_in_dim(g.reshape((4,) + a.shape), a, chip, 0)
            fw[n] = jnp.concatenate([g[k] for k in range(4)], axis=SHARD_AXIS[n])
        else:
            fw[n] = _place_own(g, a, chip.astype(jnp.int32).reshape(1), SHARD_AXIS[n], name=f"place_{n}")
    small = lax.dynamic_update_index_in_dim(got[-1].reshape((4,) + mine[-1].shape), mine[-1], chip, 0)
    parts = [_unpack(small[k], shapes) for k in range(4)]
    for i, n in enumerate(SMALL_SHARDED):
        fw[n] = jnp.concatenate([parts[k][i] for k in range(4)], axis=SHARD_AXIS[n])
    return fw


def _reduce(grads, w, chip, loss):
    stacks = []
    for n in BIG:
        cols = w[n].shape[-1]
        stacks.append(grads[n] if n in STACKED else
                      jnp.stack(jnp.split(grads[n], 4, axis=SHARD_AXIS[n])).reshape(4, 2, -1, cols))
    shapes = [grads[n].shape for n in SMALL] + [(1,)]
    small = _pack([grads[n] for n in SMALL] + [loss.reshape(1)], rows=1024).reshape(1, 2, -1, 128)
    c_vec = lax.axis_index("c").astype(jnp.int32).reshape(1)
    chip_vec = chip.astype(jnp.int32).reshape(1)
    got = _send_halves(stacks + [small], name="reduce_pair")
    sums = [_add_half(s, r, c_vec, tb=512, out_dtype=BF16, name=f"reduce_pair_sum_{n}")
            for n, s, r in zip(BIG, stacks, got)]
    sums.append(_add_half(small, got[-1], c_vec, tb=512, out_dtype=F32, name="reduce_pair_sum_small"))
    landed = _scatter_chips(sums, name="reduce_chips")
    halves = [_sum_chips(p, l, chip_vec, tb=256, name=f"reduce_chips_sum_{n}") for n, p, l in zip(BIG, sums, landed)]
    halves.append(_sum_chips_ordered(sums[-1], landed[-1], chip_vec, tb=512, name="reduce_chips_sum_small"))
    theirs = _swap_halves(halves, name="reduce_swap")
    names = BIG + ("small",)
    whole = [_join_halves(h, t, c_vec, tb=512, name=f"reduce_join_{n}") for n, h, t in zip(names, halves, theirs)]
    out = {n: j.reshape(w[n].shape) for n, j in zip(BIG, whole)}
    *parts, loss_sum = _unpack(whole[-1], shapes)
    for n, g in zip(SMALL, parts):
        if n in SHARD_AXIS:
            ax = SHARD_AXIS[n]
            size = w[n].shape[ax]
            g = lax.dynamic_slice_in_dim(g, chip * size, size, axis=ax)
        out[n] = g
    return out, loss_sum[0]


def kernel(x, p, norm_mix, norm_ffn, norm_pl, mlp_w1, mlp_w2, pl_proj, pl_gate, e_in_proj, e_out_proj, s5_lam_re, s5_lam_im, s5_log_step, s5_b_re, s5_b_im, s5_c_re, s5_c_im, s5_d, s5_glu_w, s5_glu_b, ssd_conv_w, ssd_conv_b, ssd_dt_bias, ssd_a_log, ssd_d, ssd_norm, o_in_proj, o_out_proj, rwkv_mu, rwkv_w0, rwkv_w_up, rwkv_a0, rwkv_a_up, rwkv_g_up, rwkv_k_k, rwkv_k_a, rwkv_r_k, rwkv_ln_g, rwkv_ln_b, lru_conv_w, lru_conv_b, lru_w_a, lru_b_a, lru_w_x, lru_b_x, lru_lam, norm_final, loss_target, m_norm_mix, m_norm_ffn, m_norm_pl, m_mlp_w1, m_mlp_w2, m_pl_proj, m_pl_gate, m_e_in_proj, m_e_out_proj, m_s5_lam_re, m_s5_lam_im, m_s5_log_step, m_s5_b_re, m_s5_b_im, m_s5_c_re, m_s5_c_im, m_s5_d, m_s5_glu_w, m_s5_glu_b, m_ssd_conv_w, m_ssd_conv_b, m_ssd_dt_bias, m_ssd_a_log, m_ssd_d, m_ssd_norm, m_o_in_proj, m_o_out_proj, m_rwkv_mu, m_rwkv_w0, m_rwkv_w_up, m_rwkv_a0, m_rwkv_a_up, m_rwkv_g_up, m_rwkv_k_k, m_rwkv_k_a, m_rwkv_r_k, m_rwkv_ln_g, m_rwkv_ln_b, m_lru_conv_w, m_lru_conv_b, m_lru_w_a, m_lru_b_a, m_lru_w_x, m_lru_b_x, m_lru_lam, m_norm_final, v_norm_mix, v_norm_ffn, v_norm_pl, v_mlp_w1, v_mlp_w2, v_pl_proj, v_pl_gate, v_e_in_proj, v_e_out_proj, v_s5_lam_re, v_s5_lam_im, v_s5_log_step, v_s5_b_re, v_s5_b_im, v_s5_c_re, v_s5_c_im, v_s5_d, v_s5_glu_w, v_s5_glu_b, v_ssd_conv_w, v_ssd_conv_b, v_ssd_dt_bias, v_ssd_a_log, v_ssd_d, v_ssd_norm, v_o_in_proj, v_o_out_proj, v_rwkv_mu, v_rwkv_w0, v_rwkv_w_up, v_rwkv_a0, v_rwkv_a_up, v_rwkv_g_up, v_rwkv_k_k, v_rwkv_k_a, v_rwkv_r_k, v_rwkv_ln_g, v_rwkv_ln_b, v_lru_conv_w, v_lru_conv_b, v_lru_w_a, v_lru_b_a, v_lru_w_x, v_lru_b_x, v_lru_lam, v_norm_final):
    given = dict(locals())
    w = {n: given[n] for n in WEIGHTS}
    m = {n: given["m_" + n] for n in WEIGHTS}
    v = {n: given["v_" + n] for n in WEIGHTS}
    chip = 2 * lax.axis_index("x") + lax.axis_index("y")

    fw = _gather_weights(w)
    loss, dx, grads = _local_step(x[0], p[:, 0], loss_target[0], fw)
    g, loss = _reduce(grads, w, chip, loss)

    delta, new_m, new_v = {}, {}, {}
    for n in BIG:
        delta[n], new_m[n], new_v[n] = _adamw(w[n], g[n], m[n], v[n], name=f"adamw_{n}")
    shapes = [w[n].shape for n in SMALL]
    packed = [_pack([d[n] for n in SMALL]) for d in (w, g, m, v)]
    for d, buf in zip((delta, new_m, new_v), _adamw(*packed, name="adamw_small")):
        d.update(zip(SMALL, _unpack(buf, shapes)))
    return (loss, dx[None], *[g[n] for n in WEIGHTS], *[delta[n] for n in WEIGHTS],
            *[new_m[n] for n in WEIGHTS], *[new_v[n] for n in WEIGHTS])
```

```python
import functools
import math

import jax
import jax.numpy as jnp
from jax import lax
from jax.experimental import pallas as pl
from jax.experimental.pallas import tpu as pltpu

F32 = jnp.float32
BF16 = jnp.bfloat16
HI = lax.Precision.HIGHEST
MESH = pl.DeviceIdType.MESH
SDS = jax.ShapeDtypeStruct
VMEM_LIMIT = 56 * 1024 * 1024
MM_VMEM_BUDGET = 40 * 1024 * 1024
ANY = pl.BlockSpec(memory_space=pl.ANY)

D = 2048
PL_DIM = 256
D_FF = 4 * D
EPS = 1e-6
S5_W, S5_G, S5_GROUPS, S5_P = 512, 16, 32, 64
S5_N = S5_GROUPS * S5_P
SSD_W, SSD_HD, SSD_H, SSD_NG, SSD_N, SSD_L = 1536, 64, 24, 4, 128, 128
SSD_CONV = SSD_W + 2 * SSD_NG * SSD_N
EVEN_IN = S5_W + SSD_W + SSD_CONV + SSD_H
EVEN_PAD = 5120
RW_W, RW_H, RW_HD = 1024, 16, 64
RW_LORA = 96
RW_GATE = 256
RW_IN = 3 * RW_W + 2 * RW_LORA + RW_GATE
RW_PAD = 3584
LRU_W, LRU_B = 1024, 16
ODD_IN = RW_IN + 2 * LRU_W
ODD_PAD = RW_PAD + 2 * LRU_W
GN_EPS = 64e-5
LRU_C = 8.0
ADAM_LR, ADAM_B1, ADAM_B2, ADAM_EPS, ADAM_WD, ADAM_STEP = 0.001, 0.9, 0.999, 1e-08, 0.01, 10


def _cparams(sem=("arbitrary",)):
    return pltpu.CompilerParams(dimension_semantics=sem, vmem_limit_bytes=VMEM_LIMIT)


def _dot16(a, b, dims=(((1,), (0,)), ((), ()))):
    return lax.dot_general(a.astype(BF16), b.astype(BF16), dims, preferred_element_type=F32)


NN = (((1,), (0,)), ((), ()))
NT = (((1,), (1,)), ((), ()))
TN = (((0,), (0,)), ((), ()))


def _split3(x):
    top = lambda z: lax.bitcast_convert_type(lax.bitcast_convert_type(z, jnp.int32) & jnp.int32(-65536), F32)
    hi = top(x)
    rest = x - hi
    mid = top(rest)
    return hi.astype(BF16), mid.astype(BF16), (rest - mid).astype(BF16)


def _sel_raw(a, b, dims, data):
    parts = _split3((a, b)[data].astype(F32))
    mask = (a, b)[1 - data].astype(BF16)
    acc = None
    for part in reversed(parts):
        ops = (part, mask) if data == 0 else (mask, part)
        term = lax.dot_general(*ops, dims, preferred_element_type=F32)
        acc = term if acc is None else acc + term
    return acc


_SEL_BACK = {(NN, 0): ("g", "m", NT, 0), (NT, 0): ("g", "m", NN, 0), (TN, 0): ("m", "g", NT, 1),
             (NN, 1): ("m", "g", TN, 1), (NT, 1): ("g", "m", TN, 0), (TN, 1): ("m", "g", NN, 1)}


@functools.partial(jax.custom_vjp, nondiff_argnums=(2, 3))
def _sel_dot(a, b, dims, data):
    return _sel_raw(a, b, dims, data)


def _sel_dot_fwd(a, b, dims, data):
    return _sel_raw(a, b, dims, data), (a, b)


def _sel_dot_bwd(dims, data, res, g):
    mask = res[1 - data]
    left, right, dims2, data2 = _SEL_BACK[(dims, data)]
    grad = _sel_raw(g if left == "g" else mask, g if right == "g" else mask, dims2, data2)
    zero = jnp.zeros_like(mask)
    return (grad, zero) if data == 0 else (zero, grad)


_sel_dot.defvjp(_sel_dot_fwd, _sel_dot_bwd)


def _tile(dim, target):
    if dim <= target:
        return dim
    t = target - target % 128
    while t > 128 and dim % t:
        t -= 128
    assert dim % t == 0, (dim, target)
    return t


def _mm(a, b, *, ta=False, tb=False, add=None, out_dtype=F32, tm=1024, tn=1024, tk=1024, name,
        epilogue=None, extra=(), out_dtypes=None):
    layer = None
    if isinstance(b, tuple):
        b, layer = b
    m, k = (a.shape[1], a.shape[0]) if ta else a.shape
    n = b.shape[-2] if tb else b.shape[-1]
    assert (b.shape[-1] if tb else b.shape[-2]) == k, (a.shape, b.shape, ta, tb)
    ins = [a, b] + ([add] if add is not None else []) + list(extra)
    out_dtypes = out_dtypes or [out_dtype]
    n_in, n_out = len(ins), len(out_dtypes)
    tm, tn = _tile(m, tm), _tile(n, tn)
    tiles = 2 * tm * tn * sum(jnp.dtype(x.dtype).itemsize for x in ins[2:]) + 2 * tm * tn * sum(
        jnp.dtype(dt).itemsize for dt in out_dtypes) + 4 * tm * tn
    per_k = 2 * (tm * a.dtype.itemsize + tn * b.dtype.itemsize)
    tk = _tile(k, max(tk, min(2048, (MM_VMEM_BUDGET - tiles) // per_k // 128 * 128)))
    nk = k // tk
    dims = (((0 if ta else 1,), (1 if tb else 0,)), ((), ()))

    def finish(acc, refs):
        res = epilogue(acc, *[r[...] for r in refs[n_in - len(extra):n_in]]) if epilogue else (acc,)
        for o_ref, val in zip(refs[n_in:n_in + n_out], res):
            o_ref[...] = val.astype(o_ref.dtype)

    def body(*refs):
        a_ref, b_ref, acc_ref = refs[0], refs[1], refs[-1]
        if nk == 1:
            acc = _dot16(a_ref[...], b_ref[...], dims)
            finish(acc + refs[2][...].astype(F32) if add is not None else acc, refs)
            return
        kk = pl.program_id(2)

        @pl.when(kk == 0)
        def _():
            acc_ref[...] = refs[2][...].astype(F32) if add is not None else jnp.zeros_like(acc_ref)

        acc_ref[...] += _dot16(a_ref[...], b_ref[...], dims)

        @pl.when(kk == nk - 1)
        def _():
            finish(acc_ref[...], refs)

    a_spec = pl.BlockSpec((tk, tm), lambda i, j, q: (q, i)) if ta else pl.BlockSpec((tm, tk), lambda i, j, q: (i, q))
    b_spec = pl.BlockSpec((tn, tk), lambda i, j, q: (j, q)) if tb else pl.BlockSpec((tk, tn), lambda i, j, q: (q, j))
    if layer is not None:
        b_spec = (pl.BlockSpec((None, tn, tk), lambda i, j, q: (layer, j, q)) if tb
                  else pl.BlockSpec((None, tk, tn), lambda i, j, q: (layer, q, j)))
    o_spec = pl.BlockSpec((tm, tn), lambda i, j, q: (i, j))
    outs = pl.pallas_call(
        body,
        grid=(m // tm, n // tn, nk),
        in_specs=[a_spec, b_spec] + [o_spec] * (n_in - 2),
        out_specs=[o_spec] * n_out,
        out_shape=[SDS((m, n), dt) for dt in out_dtypes],
        scratch_shapes=[pltpu.VMEM((tm, tn) if nk > 1 else (8, 128), F32)],
        compiler_params=_cparams(("parallel", "parallel", "arbitrary")),
        name=name,
    )(*ins)
    return outs if epilogue else outs[0]


def _mm_grad(x, dy, *, layer, cols_cut, shard, prev, name):
    t = x.shape[0]
    r, c = shard
    tm, tn = _tile(r, 1024), _tile(c, 1024)
    per_k = 2 * (tm * x.dtype.itemsize + tn * dy.dtype.itemsize)
    tk = _tile(t, max(1024, min(2048, (MM_VMEM_BUDGET - 12 * tm * tn) // per_k // 128 * 128)))
    nk = t // tk
    if cols_cut:
        assert x.shape[1] == r and dy.shape[1] == 4 * c
        per = c // tn
        omap = lambda i, j, q: (j // per, layer, i, j % per)
    else:
        assert x.shape[1] == 4 * r and dy.shape[1] == c
        per = r // tm
        omap = lambda i, j, q: (i // per, layer, i % per, j)

    def body(*refs):
        x_ref, dy_ref = refs[:2]
        o_ref, acc_ref = refs[-2:]
        kk = pl.program_id(2)

        @pl.when(kk == 0)
        def _():
            acc_ref[...] = jnp.zeros_like(acc_ref)

        acc_ref[...] += _dot16(x_ref[...], dy_ref[...], TN)

        @pl.when(kk == nk - 1)
        def _():
            o_ref[...] = acc_ref[...]

    return pl.pallas_call(
        body,
        grid=(x.shape[1] // tm, dy.shape[1] // tn, nk),
        in_specs=[pl.BlockSpec((tk, tm), lambda i, j, q: (q, i)), pl.BlockSpec((tk, tn), lambda i, j, q: (q, j))]
        + ([ANY] if prev is not None else []),
        out_specs=pl.BlockSpec((None, None, tm, tn), omap),
        out_shape=SDS((4, 2, r, c), F32),
        scratch_shapes=[pltpu.VMEM((tm, tn), F32)],
        input_output_aliases={2: 0} if prev is not None else {},
        compiler_params=_cparams(("parallel", "parallel", "arbitrary")),
        name=name,
    )(x, dy, *([prev] if prev is not None else []))


def _single(fn, consts, *, name):
    outs = jax.eval_shape(fn, *[SDS(c.shape, F32) for c in consts])
    n_in = len(consts)

    def body(*refs):
        res = fn(*[r[...] for r in refs[:n_in]])
        for o_ref, v in zip(refs[n_in:], res):
            o_ref[...] = v

    return pl.pallas_call(body, out_shape=[SDS(o.shape, F32) for o in outs],
                          compiler_params=pltpu.CompilerParams(vmem_limit_bytes=VMEM_LIMIT), name=name)(*consts)


def _single_vjp(fn, consts, cots, *, name):
    n_in = len(consts)

    def body(*refs):
        _, pull = jax.vjp(fn, *[r[...] for r in refs[:n_in]])
        grads = pull(tuple(r[...] for r in refs[n_in:n_in + len(cots)]))
        for o_ref, v in zip(refs[n_in + len(cots):], grads):
            o_ref[...] = v

    return pl.pallas_call(body, out_shape=[SDS(c.shape, F32) for c in consts],
                          compiler_params=pltpu.CompilerParams(vmem_limit_bytes=VMEM_LIMIT), name=name)(*consts, *cots)


def _full_spec(shape):
    nd = len(shape)
    return pl.BlockSpec(shape, lambda i, _n=nd: (0,) * _n)


def _stage_shapes(fn, rows, consts, tb, pos):
    rs = [SDS((tb, r.shape[1]), F32) for r in rows]
    cs = [SDS(c.shape, F32) for c in consts]
    f = (lambda *a: fn(jnp.int32(0), *a)) if pos else fn
    return jax.eval_shape(f, *rs, *cs)


def _stage(fn, rows, consts, *, tb, name, out_dtypes, n_acc=0, pos=False):
    t = rows[0].shape[0]
    assert t % tb == 0
    outs = _stage_shapes(fn, rows, consts, tb, pos)
    n_out = len(outs)
    n_row = n_out - n_acc
    n_in = len(rows) + len(consts)

    def body(*refs):
        i = pl.program_id(0)
        vals = [r[...].astype(F32) for r in refs[:n_in]]
        res = fn(i * tb, *vals) if pos else fn(*vals)
        out_refs = refs[n_in:]
        for q in range(n_row):
            out_refs[q][...] = res[q].astype(out_refs[q].dtype)
        for q in range(n_row, n_out):
            @pl.when(i == 0)
            def _(q=q):
                out_refs[q][...] = jnp.zeros_like(out_refs[q])

            out_refs[q][...] += res[q]

    in_specs = [pl.BlockSpec((tb, r.shape[1]), lambda i: (i, 0)) for r in rows] + [_full_spec(c.shape) for c in consts]
    out_specs = [pl.BlockSpec((tb, o.shape[1]), lambda i: (i, 0)) for o in outs[:n_row]] + [_full_spec(o.shape) for o in outs[n_row:]]
    out_shape = [SDS((t, o.shape[1]), dt) for o, dt in zip(outs[:n_row], out_dtypes)] + [SDS(o.shape, F32) for o in outs[n_row:]]
    return pl.pallas_call(
        body, grid=(t // tb,), in_specs=in_specs, out_specs=out_specs, out_shape=out_shape,
        compiler_params=_cparams(), name=name,
    )(*rows, *consts)


def _stage_vjp(fn, rows, consts, cots, *, tb, name, drow, dconst, drow_dtypes=None, acc_cots=(), pos=False):
    t = rows[0].shape[0]
    assert t % tb == 0
    n_rows, n_consts, n_cots, n_acc = len(rows), len(consts), len(cots), len(acc_cots)
    n_in = n_rows + n_consts + n_cots + n_acc
    drow_dtypes = drow_dtypes or [F32] * len(drow)

    def body(*refs):
        i = pl.program_id(0)
        vals = [r[...].astype(F32) for r in refs[:n_in]]
        rv, cv = vals[:n_rows], vals[n_rows:n_rows + n_consts]
        ct = tuple(vals[n_rows + n_consts:])

        def f(*dargs):
            r2, c2 = list(rv), list(cv)
            for q, idx in enumerate(drow):
                r2[idx] = dargs[q]
            for q, idx in enumerate(dconst):
                c2[idx] = dargs[len(drow) + q]
            return fn(i * tb, *r2, *c2) if pos else fn(*r2, *c2)

        _, pull = jax.vjp(f, *[rv[q] for q in drow], *[cv[q] for q in dconst])
        grads = pull(ct)
        out_refs = refs[n_in:]
        for q in range(len(drow)):
            out_refs[q][...] = grads[q].astype(out_refs[q].dtype)
        for q in range(len(drow), len(drow) + len(dconst)):
            @pl.when(i == 0)
            def _(q=q):
                out_refs[q][...] = jnp.zeros_like(out_refs[q])

            out_refs[q][...] += grads[q]

    in_specs = ([pl.BlockSpec((tb, r.shape[1]), lambda i: (i, 0)) for r in rows] + [_full_spec(c.shape) for c in consts]
                + [pl.BlockSpec((tb, c.shape[1]), lambda i: (i, 0)) for c in cots] + [_full_spec(c.shape) for c in acc_cots])
    out_specs = ([pl.BlockSpec((tb, rows[q].shape[1]), lambda i: (i, 0)) for q in drow]
                 + [_full_spec(consts[q].shape) for q in dconst])
    out_shape = ([SDS(rows[q].shape, dt) for q, dt in zip(drow, drow_dtypes)]
                 + [SDS(consts[q].shape, F32) for q in dconst])
    return pl.pallas_call(
        body, grid=(t // tb,), in_specs=in_specs, out_specs=out_specs, out_shape=out_shape,
        compiler_params=_cparams(), name=name,
    )(*rows, *consts, *cots, *acc_cots)


def _conv_fwd(x, w, b, *, tb, name):
    t, c = x.shape
    r8 = tb // 8

    def body(x_ref, p_ref, w_ref, b_ref, o_ref):
        i = pl.program_id(0)
        x_ = x_ref[...]
        p_ = jnp.where(i > 0, p_ref[...], 0.0)
        w_ = w_ref[...]
        row = lax.broadcasted_iota(jnp.int32, x_.shape, 0)
        row8 = lax.broadcasted_iota(jnp.int32, p_.shape, 0)
        acc = x_ * w_[3:4, :] + b_ref[...]
        head = jnp.zeros_like(p_)
        for j in (1, 2, 3):
            wj = w_[3 - j:4 - j, :]
            acc += jnp.where(row >= j, pltpu.roll(x_, j, 0), 0.0) * wj
            head += jnp.where(row8 < j, pltpu.roll(p_, j, 0), 0.0) * wj
        o_ref[...] = acc
        o_ref[0:8, :] += head

    return pl.pallas_call(
        body, grid=(t // tb,),
        in_specs=[pl.BlockSpec((tb, c), lambda i: (i, 0)),
                  pl.BlockSpec((8, c), lambda i: (jnp.maximum(i * r8 - 1, 0), 0)),
                  _full_spec(w.shape), _full_spec(b.shape)],
        out_specs=pl.BlockSpec((tb, c), lambda i: (i, 0)),
        out_shape=SDS((t, c), F32), compiler_params=_cparams(), name=name,
    )(x, x, w, b)


def _conv_bwd(x, w, dy, *, tb, name):
    t, c = x.shape
    r8 = tb // 8
    nb = t // tb

    def body(x_ref, p_ref, w_ref, g_ref, n_ref, dx_ref, dw_ref, db_ref):
        i = pl.program_id(0)
        x_ = x_ref[...]
        p_ = jnp.where(i > 0, p_ref[...], 0.0)
        g_ = g_ref[...]
        n_ = jnp.where(i < nb - 1, n_ref[...], 0.0)
        w_ = w_ref[...]
        row = lax.broadcasted_iota(jnp.int32, x_.shape, 0)
        row8 = lax.broadcasted_iota(jnp.int32, p_.shape, 0)
        g8 = g_[0:8, :]
        dx = g_ * w_[3:4, :]
        tail = jnp.zeros_like(n_)
        dws = [jnp.sum(g_ * x_, axis=0, keepdims=True)]
        for j in (1, 2, 3):
            wj = w_[3 - j:4 - j, :]
            dx += jnp.where(row < tb - j, pltpu.roll(g_, tb - j, 0), 0.0) * wj
            tail += jnp.where(row8 >= 8 - j, pltpu.roll(n_, 8 - j, 0), 0.0) * wj
            xs = jnp.where(row >= j, pltpu.roll(x_, j, 0), 0.0)
            ps = jnp.where(row8 < j, pltpu.roll(p_, j, 0), 0.0)
            dws.append(jnp.sum(g_ * xs, axis=0, keepdims=True) + jnp.sum(g8 * ps, axis=0, keepdims=True))
        dx_ref[...] = dx
        dx_ref[tb - 8:tb, :] += tail

        @pl.when(i == 0)
        def _():
            dw_ref[...] = jnp.zeros_like(dw_ref)
            db_ref[...] = jnp.zeros_like(db_ref)

        for j in range(4):
            dw_ref[3 - j:4 - j, :] += dws[j]
        db_ref[...] += jnp.sum(g_, axis=0, keepdims=True)

    return pl.pallas_call(
        body, grid=(nb,),
        in_specs=[pl.BlockSpec((tb, c), lambda i: (i, 0)),
                  pl.BlockSpec((8, c), lambda i: (jnp.maximum(i * r8 - 1, 0), 0)),
                  _full_spec(w.shape),
                  pl.BlockSpec((tb, c), lambda i: (i, 0)),
                  pl.BlockSpec((8, c), lambda i: (jnp.minimum((i + 1) * r8, t // 8 - 1), 0))],
        out_specs=[pl.BlockSpec((tb, c), lambda i: (i, 0)), _full_spec((8, c)), _full_spec((1, c))],
        out_shape=[SDS((t, c), F32), SDS((8, c), F32), SDS((1, c), F32)],
        compiler_params=_cparams(), name=name,
    )(x, x, w, dy, dy)


def _lru_scan_fwd(a, b, *, tb, name):
    t, c = a.shape

    def body(a_ref, b_ref, h_ref, st_ref):
        @pl.when(pl.program_id(0) == 0)
        def _():
            st_ref[...] = jnp.zeros_like(st_ref)

        def step(s, h):
            h = a_ref[pl.ds(s, 1), :] * h + b_ref[pl.ds(s, 1), :]
            h_ref[pl.ds(s, 1), :] = h
            return h

        st_ref[...] = lax.fori_loop(0, tb, step, st_ref[...], unroll=8)

    blk = pl.BlockSpec((tb, c), lambda i: (i, 0))
    return pl.pallas_call(
        body, grid=(t // tb,), in_specs=[blk, blk], out_specs=blk, out_shape=SDS((t, c), F32),
        scratch_shapes=[pltpu.VMEM((1, c), F32)], compiler_params=_cparams(), name=name,
    )(a, b)


def _lru_scan_bwd(a, h, dh, *, tb, name):
    t, c = a.shape
    nb = t // tb
    r8 = tb // 8

    def body(a_ref, h_ref, p_ref, g_ref, da_ref, db_ref, st_ref):
        i = pl.program_id(0)

        @pl.when(i == 0)
        def _():
            st_ref[...] = jnp.zeros_like(st_ref)

        hprev0 = jnp.where(i < nb - 1, p_ref[7:8, :], 0.0)

        def step(q, carry):
            s = tb - 1 - q
            g = g_ref[pl.ds(s, 1), :] + carry
            hp = h_ref[pl.ds(jnp.maximum(s - 1, 0), 1), :]
            hp = jnp.where(s > 0, hp, hprev0)
            db_ref[pl.ds(s, 1), :] = g
            da_ref[pl.ds(s, 1), :] = g * hp
            return a_ref[pl.ds(s, 1), :] * g

        st_ref[...] = lax.fori_loop(0, tb, step, st_ref[...], unroll=8)

    rev = pl.BlockSpec((tb, c), lambda i: (nb - 1 - i, 0))
    prev = pl.BlockSpec((8, c), lambda i: (jnp.maximum((nb - 1 - i) * r8 - 1, 0), 0))
    return pl.pallas_call(
        body, grid=(nb,), in_specs=[rev, rev, prev, rev], out_specs=[rev, rev],
        out_shape=[SDS((t, c), F32), SDS((t, c), F32)],
        scratch_shapes=[pltpu.VMEM((1, c), F32)], compiler_params=_cparams(), name=name,
    )(a, h, h, dh)


def _s5_scan_fwd(ar, ai, br, bi, *, tb, name):
    t, c = br.shape

    def body(ar_ref, ai_ref, br_ref, bi_ref, xr_ref, xi_ref, sr_ref, si_ref):
        @pl.when(pl.program_id(0) == 0)
        def _():
            sr_ref[...] = jnp.zeros_like(sr_ref)
            si_ref[...] = jnp.zeros_like(si_ref)

        ar_, ai_ = ar_ref[...], ai_ref[...]

        def step(s, carry):
            xr, xi = carry
            nr = ar_ * xr - ai_ * xi + br_ref[pl.ds(s, 1), :]
            ni = ar_ * xi + ai_ * xr + bi_ref[pl.ds(s, 1), :]
            xr_ref[pl.ds(s, 1), :] = nr
            xi_ref[pl.ds(s, 1), :] = ni
            return nr, ni

        xr, xi = lax.fori_loop(0, tb, step, (sr_ref[...], si_ref[...]), unroll=8)
        sr_ref[...] = xr
        si_ref[...] = xi

    blk = pl.BlockSpec((tb, c), lambda i: (i, 0))
    one = _full_spec((1, c))
    return pl.pallas_call(
        body, grid=(t // tb,), in_specs=[one, one, blk, blk], out_specs=[blk, blk],
        out_shape=[SDS((t, c), F32), SDS((t, c), F32)],
        scratch_shapes=[pltpu.VMEM((1, c), F32), pltpu.VMEM((1, c), F32)], compiler_params=_cparams(), name=name,
    )(ar, ai, br, bi)


def _s5_scan_bwd(ar, ai, xr, xi, dxr, dxi, *, tb, name):
    t, c = xr.shape
    nb = t // tb
    r8 = tb // 8

    def body(ar_ref, ai_ref, xr_ref, xi_ref, pr_ref, pi_ref, gr_ref, gi_ref,
             dbr_ref, dbi_ref, dar_ref, dai_ref, cr_ref, ci_ref):
        i = pl.program_id(0)

        @pl.when(i == 0)
        def _():
            cr_ref[...] = jnp.zeros_like(cr_ref)
            ci_ref[...] = jnp.zeros_like(ci_ref)
            dar_ref[...] = jnp.zeros_like(dar_ref)
            dai_ref[...] = jnp.zeros_like(dai_ref)

        ar_, ai_ = ar_ref[...], ai_ref[...]
        first = i == nb - 1
        pr0 = jnp.where(first, 0.0, pr_ref[7:8, :])
        pi0 = jnp.where(first, 0.0, pi_ref[7:8, :])

        def step(q, carry):
            cr, ci, dar, dai = carry
            s = tb - 1 - q
            gr = gr_ref[pl.ds(s, 1), :] + cr
            gi = gi_ref[pl.ds(s, 1), :] + ci
            sp = jnp.maximum(s - 1, 0)
            xpr = jnp.where(s > 0, xr_ref[pl.ds(sp, 1), :], pr0)
            xpi = jnp.where(s > 0, xi_ref[pl.ds(sp, 1), :], pi0)
            dbr_ref[pl.ds(s, 1), :] = gr
            dbi_ref[pl.ds(s, 1), :] = gi
            dar = dar + gr * xpr + gi * xpi
            dai = dai - gr * xpi + gi * xpr
            return ar_ * gr + ai_ * gi, ar_ * gi - ai_ * gr, dar, dai

        cr, ci, dar, dai = lax.fori_loop(0, tb, step, (cr_ref[...], ci_ref[...], dar_ref[...], dai_ref[...]), unroll=8)
        cr_ref[...] = cr
        ci_ref[...] = ci
        dar_ref[...] = dar
        dai_ref[...] = dai

    rev = pl.BlockSpec((tb, c), lambda i: (nb - 1 - i, 0))
    prev = pl.BlockSpec((8, c), lambda i: (jnp.maximum((nb - 1 - i) * r8 - 1, 0), 0))
    one = _full_spec((1, c))
    return pl.pallas_call(
        body, grid=(nb,), in_specs=[one, one, rev, rev, prev, prev, rev, rev], out_specs=[rev, rev, one, one],
        out_shape=[SDS((t, c), F32), SDS((t, c), F32), SDS((1, c), F32), SDS((1, c), F32)],
        scratch_shapes=[pltpu.VMEM((1, c), F32), pltpu.VMEM((1, c), F32)], compiler_params=_cparams(), name=name,
    )(ar, ai, xr, xi, xr, xi, dxr, dxi)


RW_PAIRS = RW_H // 2


def _pair_consts():
    sub = lax.broadcasted_iota(jnp.int32, (64, 128), 0)
    lane = lax.broadcasted_iota(jnp.int32, (64, 128), 1)
    eye2 = ((lane & 63) == sub).astype(F32)
    r2 = lax.broadcasted_iota(jnp.int32, (128, 128), 0)
    c2 = lax.broadcasted_iota(jnp.int32, (128, 128), 1)
    bsel = ((r2 >> 6) == (c2 >> 6)).astype(BF16)
    return eye2, bsel


def _segsum(x, bsel):
    rows = x.shape[0]
    bits = lax.bitcast_convert_type(x, jnp.int32)
    hi = lax.bitcast_convert_type(bits & jnp.int32(-65536), F32)
    both = jnp.concatenate([hi.astype(BF16), (x - hi).astype(BF16)], axis=0)
    res = jnp.dot(both, bsel, preferred_element_type=F32)
    return res[:rows] + res[rows:]


def _bc(x8):
    return jnp.stack([jnp.broadcast_to(x8[q:q + 1, :], (64, 128)) for q in range(RW_PAIRS)])


def _seg3(x3, bsel):
    return _segsum(x3.reshape(RW_PAIRS * 64, 128), bsel).reshape(RW_PAIRS, 64, 128)


def _seg3_lanes(x3):
    first = lax.broadcasted_iota(jnp.int32, x3.shape, 2) < 64
    lo = jnp.sum(jnp.where(first, x3, 0.0), axis=-1, keepdims=True)
    hi = jnp.sum(jnp.where(first, 0.0, x3), axis=-1, keepdims=True)
    return jnp.where(first, lo, hi)


def _rwkv_scan_fwd(r, w, k, v, kk, a, *, lc, name):
    t = r.shape[0]
    nc = t // lc

    def body(r_ref, w_ref, k_ref, v_ref, kk_ref, a_ref, y_ref, ck_ref, hist_ref, st_ref):
        @pl.when(pl.program_id(0) == 0)
        def _():
            st_ref[...] = jnp.zeros_like(st_ref)

        ck_ref[0] = st_ref[...]
        eye2, bsel = _pair_consts()
        column = lambda ref, s: _seg3(eye2[None] * _bc(ref[s]), bsel)
        read = lambda st, s: jnp.sum(eye2[None] * _seg3(st * _bc(r_ref[s]), bsel), axis=1)

        def step(s, carry):
            st, vb = carry
            kk8 = kk_ref[s]
            sa = -_seg3_lanes(st * _bc(kk8))
            vb_next = column(v_ref, jnp.minimum(s + 1, lc - 1))
            before = jnp.maximum(s - 1, 0)
            y_ref[before] = read(st, before)
            st = st * _bc(w_ref[s]) + sa * _bc(kk8 * a_ref[s]) + vb * _bc(k_ref[s])
            hist_ref[s] = st
            return st, vb_next

        st, _ = lax.fori_loop(0, lc, step, (st_ref[...], column(v_ref, 0)))
        y_ref[lc - 1] = read(st, lc - 1)
        st_ref[...] = st

    blk = pl.BlockSpec((lc, RW_PAIRS, 128), lambda i: (i, 0, 0))
    return pl.pallas_call(
        body, grid=(nc,), in_specs=[blk] * 6,
        out_specs=[blk, pl.BlockSpec((1, RW_PAIRS, 64, 128), lambda i: (i, 0, 0, 0)),
                   pl.BlockSpec((lc, RW_PAIRS, 64, 128), lambda i: (i, 0, 0, 0))],
        out_shape=[SDS((t, RW_PAIRS, 128), F32), SDS((nc, RW_PAIRS, 64, 128), F32), SDS((t, RW_PAIRS, 64, 128), F32)],
        scratch_shapes=[pltpu.VMEM((RW_PAIRS, 64, 128), F32)],
        compiler_params=_cparams(), name=name,
    )(r, w, k, v, kk, a)


def _rwkv_scan_bwd(r, w, k, v, kk, a, ck, hist, dy, *, lc, name):
    t = r.shape[0]
    nc = t // lc

    def body(r_ref, w_ref, k_ref, v_ref, kk_ref, a_ref, ck_ref, hist_ref, dy_ref,
             dr_ref, dw_ref, dk_ref, dv_ref, dkk_ref, da_ref, ds_ref):
        @pl.when(pl.program_id(0) == 0)
        def _():
            ds_ref[...] = jnp.zeros_like(ds_ref)

        eye2, bsel = _pair_consts()
        column = lambda ref, s: _seg3(eye2[None] * _bc(ref[s]), bsel)

        col = lambda z: jnp.sum(z, axis=1)

        def grads(s, s_prev, d_s, dsa):
            kk8 = kk_ref[s]
            sa = -_seg3(s_prev * _bc(kk8), bsel)
            db = col(d_s * sa)
            dw_ref[s] = col(d_s * s_prev)
            dv_ref[s] = col(eye2[None] * _seg3(d_s * _bc(k_ref[s]), bsel))
            dk_ref[s] = col(d_s * column(v_ref, s))
            dkk_ref[s] = db * a_ref[s] - col(s_prev * dsa)
            da_ref[s] = db * kk8

        def back(j, carry):
            ds, d_after, dsa_after, dyb = carry
            s = lc - 1 - j
            kk8 = kk_ref[s]
            d_s = ds + dyb * _bc(r_ref[s])
            dsa = _seg3_lanes(d_s * _bc(kk8 * a_ref[s]))
            dr_ref[s] = col(hist_ref[s] * dyb)
            dyb_before = column(dy_ref, jnp.maximum(s - 1, 0))
            after = jnp.minimum(s + 1, lc - 1)
            grads(after, hist_ref[after - 1], d_after, dsa_after)
            return d_s * _bc(w_ref[s]) - dsa * _bc(kk8), d_s, dsa, dyb_before

        zero = jnp.zeros((RW_PAIRS, 64, 128), F32)
        ds, d_first, dsa_first, _ = lax.fori_loop(0, lc, back, (ds_ref[...], zero, zero, column(dy_ref, lc - 1)))
        grads(0, ck_ref[0], d_first, dsa_first)
        ds_ref[...] = ds

    rev = pl.BlockSpec((lc, RW_PAIRS, 128), lambda i: (nc - 1 - i, 0, 0))
    return pl.pallas_call(
        body, grid=(nc,),
        in_specs=[rev] * 6 + [pl.BlockSpec((1, RW_PAIRS, 64, 128), lambda i: (nc - 1 - i, 0, 0, 0)),
                              pl.BlockSpec((lc, RW_PAIRS, 64, 128), lambda i: (nc - 1 - i, 0, 0, 0)), rev],
        out_specs=[rev] * 6, out_shape=[SDS((t, RW_PAIRS, 128), F32)] * 6,
        scratch_shapes=[pltpu.VMEM((RW_PAIRS, 64, 128), F32)],
        compiler_params=_cparams(), name=name,
    )(r, w, k, v, kk, a, ck, hist, dy)


SSD_PAIRS = SSD_H // 2


def _ssd_chunk(states, xdt, da, bm, cm):
    ln = SSD_L
    row = lax.broadcasted_iota(jnp.int32, (ln, ln), 0)
    col = lax.broadcasted_iota(jnp.int32, (ln, ln), 1)
    causal = row >= col
    acum = _sel_dot(causal.astype(F32), da, NN, 1)
    acum_t = _sel_dot(da, (row <= col).astype(F32), TN, 0)
    sub = lax.broadcasted_iota(jnp.int32, (128, 128), 0)
    lane = lax.broadcasted_iota(jnp.int32, (128, 128), 1)
    ys, new_states = [], []
    for q in range(SSD_PAIRS):
        g = q // (SSD_PAIRS // SSD_NG)
        bg = bm[:, g * SSD_N:(g + 1) * SSD_N]
        cg = cm[:, g * SSD_N:(g + 1) * SSD_N]
        xq = xdt[:, q * 128:(q + 1) * 128]
        scores = _dot16(cg, bg, NT)
        aexp = _sel_dot(acum, (sub == 2 * q + (lane >> 6)).astype(F32), NN, 0)
        tot = aexp[ln - 1:ln, :]
        yh = []
        for h in (2 * q, 2 * q + 1):
            seg = _sel_dot(acum, (sub == h).astype(F32), NN, 0) - acum_t[h:h + 1, :]
            yh.append(_dot16(scores * jnp.exp(jnp.where(causal, seg, -1e30)), xq))
        y = jnp.where(lane < 64, yh[0], yh[1]) + _dot16(cg, states[q]) * jnp.exp(aexp)
        new = _dot16(bg, xq * jnp.exp(tot - aexp), TN)
        ys.append(y)
        new_states.append(states[q] * jnp.exp(tot) + new)
    return jnp.concatenate(ys, axis=1), new_states


def _ssd_fwd(xdt, da, bm, cm, *, name):
    t = xdt.shape[0]
    nc = t // SSD_L

    def body(x_ref, a_ref, b_ref, c_ref, y_ref, ck_ref, st_ref):
        @pl.when(pl.program_id(0) == 0)
        def _():
            st_ref[...] = jnp.zeros_like(st_ref)

        ck_ref[0] = st_ref[...]
        y, new = _ssd_chunk([st_ref[q] for q in range(SSD_PAIRS)], x_ref[...], a_ref[...], b_ref[...], c_ref[...])
        y_ref[...] = y
        for q in range(SSD_PAIRS):
            st_ref[q] = new[q]

    blk = lambda wd: pl.BlockSpec((SSD_L, wd), lambda i: (i, 0))
    return pl.pallas_call(
        body, grid=(nc,), in_specs=[blk(SSD_W), blk(128), blk(512), blk(512)],
        out_specs=[blk(SSD_W), pl.BlockSpec((1, SSD_PAIRS, 128, 128), lambda i: (i, 0, 0, 0))],
        out_shape=[SDS((t, SSD_W), F32), SDS((nc, SSD_PAIRS, 128, 128), F32)],
        scratch_shapes=[pltpu.VMEM((SSD_PAIRS, 128, 128), F32)], compiler_params=_cparams(), name=name,
    )(xdt, da, bm, cm)


def _ssd_bwd(xdt, da, bm, cm, ck, dy, *, name):
    t = xdt.shape[0]
    nc = t // SSD_L

    def body(x_ref, a_ref, b_ref, c_ref, ck_ref, dy_ref, dx_ref, dda_ref, db_ref, dc_ref, ds_ref):
        @pl.when(pl.program_id(0) == 0)
        def _():
            ds_ref[...] = jnp.zeros_like(ds_ref)

        _, pull = jax.vjp(_ssd_chunk, [ck_ref[0, q] for q in range(SSD_PAIRS)], x_ref[...], a_ref[...], b_ref[...], c_ref[...])
        dst, dx, dda, db, dc = pull((dy_ref[...], [ds_ref[q] for q in range(SSD_PAIRS)]))
        dx_ref[...] = dx
        dda_ref[...] = dda
        db_ref[...] = db
        dc_ref[...] = dc
        for q in range(SSD_PAIRS):
            ds_ref[q] = dst[q]

    rev = lambda wd: pl.BlockSpec((SSD_L, wd), lambda i: (nc - 1 - i, 0))
    return pl.pallas_call(
        body, grid=(nc,),
        in_specs=[rev(SSD_W), rev(128), rev(512), rev(512),
                  pl.BlockSpec((1, SSD_PAIRS, 128, 128), lambda i: (nc - 1 - i, 0, 0, 0)), rev(SSD_W)],
        out_specs=[rev(SSD_W), rev(128), rev(512), rev(512)],
        out_shape=[SDS((t, SSD_W), F32), SDS((t, 128), F32), SDS((t, 512), F32), SDS((t, 512), F32)],
        scratch_shapes=[pltpu.VMEM((SSD_PAIRS, 128, 128), F32)], compiler_params=_cparams(), name=name,
    )(xdt, da, bm, cm, ck, dy)


def _iota(shape, dim):
    return lax.broadcasted_iota(jnp.int32, shape, dim)


def _rms(x, g):
    return x * lax.rsqrt(jnp.mean(x * x, axis=-1, keepdims=True) + EPS) * g


def _head_sel(width, shift):
    return ((_iota((width, 128), 0) >> shift) == _iota((width, 128), 1)).astype(F32)


def _head_sum(x, shift=6):
    sel = _head_sel(x.shape[1], shift)
    return _sel_dot(_sel_dot(x, sel, NN, 0), sel, NT, 0)


def _head_expand(x, width, shift=6):
    return _sel_dot(x, _head_sel(width, shift), NT, 0)


def f_norm(h, g):
    return (_rms(h, g),)


def f_norm_pass(h, g):
    return _rms(h, g), h


def f_add_norm(h, m, g):
    h1 = h + m
    return h1, _rms(h1, g)


def f_relu2(u):
    r = jnp.maximum(u, 0.0)
    return (r * r,)


def f_plgate(h2, gl, pp):
    return (h2 + jax.nn.sigmoid(gl) * pp,)


def f_loss(h, tgt, g):
    err = _rms(h, g) - tgt
    part = 0.5 * jnp.sum(jnp.mean(err * err, axis=-1, keepdims=True), axis=0, keepdims=True)
    return (jnp.broadcast_to(part, (8, 128)),)


def f_s5_prep(lam_re, lam_im, lstep, bre_t, bim_t, cre_t, cim_t):
    step = jnp.exp(_sel_dot(lstep, _head_sel(S5_N, 6), NT, 0)[0:1, :])
    mag = jnp.exp(lam_re * step)
    abar_re, abar_im = mag * jnp.cos(lam_im * step), mag * jnp.sin(lam_im * step)
    den = lam_re * lam_re + lam_im * lam_im
    nr = abar_re - 1.0
    coef_re = (nr * lam_re + abar_im * lam_im) / den
    coef_im = (abar_im * lam_re - nr * lam_im) / den
    bbar_re = coef_re * bre_t - coef_im * bim_t
    bbar_im = coef_re * bim_t + coef_im * bre_t
    rep = ((_iota((S5_W, S5_G), 0) & (S5_G - 1)) == _iota((S5_W, S5_G), 1)).astype(F32)
    blk = ((_iota((S5_W, S5_N), 0) >> 4) == (_iota((S5_W, S5_N), 1) >> 6)).astype(F32)
    blk_t = ((_iota((S5_N, S5_W), 0) >> 6) == (_iota((S5_N, S5_W), 1) >> 4)).astype(F32)
    wb_re, wb_im = _sel_dot(rep, bbar_re, NN, 1) * blk, _sel_dot(rep, bbar_im, NN, 1) * blk
    wc_re, wc_im = _sel_dot(cre_t, rep, NT, 0) * blk_t, _sel_dot(cim_t, rep, NT, 0) * blk_t
    return abar_re, abar_im, wb_re, wb_im, wc_re, wc_im


def f_s5_post(xr, xi, u, wc_re, wc_im, d_skip, glu_w, glu_b):
    y = _dot16(xr, wc_re) - _dot16(xi, wc_im) + d_skip * u
    act = jax.nn.gelu(y)
    return (act * jax.nn.sigmoid(_dot16(act, glu_w) + glu_b),)


def f_ssd_pre(xc, dtr, dt_bias, a_log):
    act = jax.nn.silu(xc)
    heads = _iota(dtr.shape, 1) < SSD_H
    dt = jnp.where(heads, jax.nn.softplus(dtr + dt_bias), 0.0)
    da = dt * (-jnp.exp(a_log))
    xdt = act[:, :SSD_W] * _head_expand(dt, SSD_W)
    return xdt, da, act[:, SSD_W:SSD_W + 512], act[:, SSD_W + 512:]


def f_ssd_pre_pass(xc, dtr, dt_bias, a_log):
    return f_ssd_pre(xc, dtr, dt_bias, a_log) + (xc,)


def f_ssd_post(y, xc, z, d_skip, norm_g):
    xs = jax.nn.silu(xc[:, :SSD_W])
    y = (y + xs * _head_expand(d_skip, SSD_W)) * jax.nn.silu(z)
    gw = SSD_W // SSD_NG
    parts = []
    for g in range(SSD_NG):
        seg = y[:, g * gw:(g + 1) * gw]
        parts.append(seg * lax.rsqrt(jnp.mean(seg * seg, axis=-1, keepdims=True) + EPS))
    return (jnp.concatenate(parts, axis=1) * norm_g,)


def f_rwkv_pre(f, w0, w_up, a0, a_up, g_up, k_k, k_a):
    r, k, v = f[:, 0:1024], f[:, 1024:2048], f[:, 2048:3072]
    wl, al, gl = f[:, 3072:3200], f[:, 3200:3328], f[:, 3328:3584]
    w = -jax.nn.softplus(-(w0 + _dot16(jnp.tanh(wl), w_up))) - 0.5
    decay = jnp.exp(-jnp.exp(w))
    a = jax.nn.sigmoid(a0 + _dot16(al, a_up))
    g = _dot16(jax.nn.sigmoid(gl), g_up)
    kk = k * k_k
    k2 = k * (1.0 + (a - 1.0) * k_a)
    kkn = kk * lax.rsqrt(jnp.maximum(_head_sum(kk * kk), 1e-24))
    return r, decay, k2, v, kkn, a, g


def f_rwkv_pre_pass(f, w0, w_up, a0, a_up, g_up, k_k, k_a):
    out = f_rwkv_pre(f, w0, w_up, a0, a_up, g_up, k_k, k_a)
    return out + (out[0], out[2], out[3])


def f_rwkv_post(y, r, k2, v, g, ln_g, ln_b, r_k):
    mean = _head_sum(y) * (1.0 / RW_HD)
    yc = y - mean
    var = _head_sum(yc * yc) * (1.0 / RW_HD)
    yn = yc * lax.rsqrt(var + GN_EPS) * ln_g + ln_b
    bonus = _head_sum(r * k2 * r_k) * v
    return ((yn + bonus) * g,)


def _neg_expm1(y):
    series = -y * (1.0 + y * (0.5 + y * (1.0 / 6.0 + y * (1.0 / 24.0 + y * (1.0 / 120.0)))))
    return jnp.where(y > -0.1, series, 1.0 - jnp.exp(y))


def f_lru_pre(t0, xc, w_a, b_a, w_x, b_x, lam):
    gate_r = jax.nn.sigmoid(_dot16(xc, w_a) + b_a)
    gate_i = jax.nn.sigmoid(_dot16(xc, w_x) + b_x)
    log_a = -LRU_C * gate_r * jax.nn.softplus(-lam)
    mult = jnp.sqrt(jnp.maximum(_neg_expm1(2.0 * log_a), 0.0))
    mult = jnp.where(_iota(xc.shape, 0) + t0 == 0, 1.0, mult)
    return jnp.exp(log_a), xc * gate_i * mult


def f_lru_post(h, gl):
    return (h * jax.nn.gelu(gl),)


TB = 256
TBH = 128
SCAN_TB = 256
RW_LC = 32


def _even_fwd(hn, w, tag):
    n = lambda s: f"{tag}_{s}"
    u = _mm(hn, w["in_u"], name=n("proj_u"))
    z = _mm(hn, w["in_z"], name=n("proj_z"))
    xbc = _mm(hn, w["in_xbc"], name=n("proj_xbc"))
    dtr = _mm(hn, w["in_dt"], name=n("proj_dt"))
    bu_re = _mm(u, w["wb_re"], name=n("s5_bu_re"))
    bu_im = _mm(u, w["wb_im"], name=n("s5_bu_im"))
    xr, xi = _s5_scan_fwd(w["abar_re"], w["abar_im"], bu_re, bu_im, tb=SCAN_TB, name=n("s5_scan"))
    s5c = [w["wc_re"], w["wc_im"], w["s5_d"], w["glu_w"], w["glu_b"]]
    (ya,) = _stage(f_s5_post, [xr, xi, u], s5c, tb=TB, name=n("s5_post"), out_dtypes=[BF16])
    xc = _conv_fwd(xbc, w["ssd_conv_w"], w["ssd_conv_b"], tb=TB, name=n("ssd_conv"))
    xdt, da, bm, cm = _stage(f_ssd_pre, [xc, dtr], [w["dt_bias"], w["a_log"]], tb=TB, name=n("ssd_pre"),
                             out_dtypes=[F32] * 4)
    y, ck = _ssd_fwd(xdt, da, bm, cm, name=n("ssd_scan"))
    (yb,) = _stage(f_ssd_post, [y, xc, z], [w["ssd_d"], w["ssd_norm"]], tb=TB, name=n("ssd_post"), out_dtypes=[BF16])
    mo = _mm(ya, w["out_a"], name=n("out_a"))
    mo = _mm(yb, w["out_b"], add=mo, name=n("out_b"))
    res = dict(u=u, z=z, xbc=xbc, dtr=dtr, xr=xr, xi=xi, ya=ya, xc=xc, xdt=xdt, da=da, bm=bm, cm=cm, y=y, ck=ck, yb=yb)
    return mo, res


def _even_bwd(dmo, hn, w, r, tag):
    n = lambda s: f"{tag}_{s}"
    g = {}
    g["out_a"] = _mm(r["ya"], dmo, ta=True, name=n("d_out_a"))
    g["out_b"] = _mm(r["yb"], dmo, ta=True, name=n("d_out_b"))
    dya = _mm(dmo, w["out_a"], tb=True, name=n("dya"))
    dyb = _mm(dmo, w["out_b"], tb=True, name=n("dyb"))
    dy, dxc1, dz, g["ssd_d"], g["ssd_norm"] = _stage_vjp(
        f_ssd_post, [r["y"], r["xc"], r["z"]], [w["ssd_d"], w["ssd_norm"]], [dyb], tb=TBH, name=n("ssd_post_b"),
        drow=[0, 1, 2], dconst=[0, 1])
    dxdt, dda, dbm, dcm = _ssd_bwd(r["xdt"], r["da"], r["bm"], r["cm"], r["ck"], dy, name=n("ssd_scan_b"))
    dxc, ddtr, g["dt_bias"], g["a_log"] = _stage_vjp(
        f_ssd_pre_pass, [r["xc"], r["dtr"]], [w["dt_bias"], w["a_log"]], [dxdt, dda, dbm, dcm, dxc1], tb=TBH,
        name=n("ssd_pre_b"), drow=[0, 1], dconst=[0, 1])
    dxbc, g["ssd_conv_w"], g["ssd_conv_b"] = _conv_bwd(r["xbc"], w["ssd_conv_w"], dxc, tb=TB, name=n("ssd_conv_b"))
    s5c = [w["wc_re"], w["wc_im"], w["s5_d"], w["glu_w"], w["glu_b"]]
    dxr, dxi, du1, g["wc_re"], g["wc_im"], g["s5_d"], g["glu_w"], g["glu_b"] = _stage_vjp(
        f_s5_post, [r["xr"], r["xi"], r["u"]], s5c, [dya], tb=TBH, name=n("s5_post_b"),
        drow=[0, 1, 2], dconst=[0, 1, 2, 3, 4])
    dbr, dbi, g["abar_re"], g["abar_im"] = _s5_scan_bwd(w["abar_re"], w["abar_im"], r["xr"], r["xi"], dxr, dxi,
                                                         tb=SCAN_TB, name=n("s5_scan_b"))
    g["wb_re"] = _mm(r["u"], dbr, ta=True, name=n("d_wb_re"))
    g["wb_im"] = _mm(r["u"], dbi, ta=True, name=n("d_wb_im"))
    du = _mm(dbr, w["wb_re"], tb=True, add=du1, name=n("du_re"))
    du = _mm(dbi, w["wb_im"], tb=True, add=du, name=n("du_im"))
    segs = (("in_u", du), ("in_z", dz), ("in_xbc", dxbc), ("in_dt", ddtr))
    dhn = None
    for key, dseg in segs:
        g[key] = _mm(hn, dseg, ta=True, name=n("d_" + key))
        dhn = _mm(dseg, w[key], tb=True, add=dhn, name=n("dhn_" + key))
    return dhn, g


def _odd_fwd(hn, w, tag):
    n = lambda s: f"{tag}_{s}"
    rw = _mm(hn, w["in_rw"], name=n("proj_rw"))
    xl = _mm(hn, w["in_xl"], name=n("proj_xl"))
    gl = _mm(hn, w["in_gl"], name=n("proj_gl"))
    f = _conv_fwd(rw, w["mix_w"], w["mix_b"], tb=TB, name=n("rwkv_shift"))
    rc = [w[k] for k in ("w0", "w_up", "a0", "a_up", "g_up", "k_k", "k_a")]
    r_, dec, k2, v, kkn, a, gate = _stage(f_rwkv_pre, [f], rc, tb=TB, name=n("rwkv_pre"), out_dtypes=[F32] * 7)
    t3 = lambda z: z.reshape(-1, RW_PAIRS, 128)
    y, ck, hist = _rwkv_scan_fwd(t3(r_), t3(dec), t3(k2), t3(v), t3(kkn), t3(a), lc=RW_LC, name=n("rwkv_scan"))
    y = y.reshape(-1, RW_W)
    (yc,) = _stage(f_rwkv_post, [y, r_, k2, v, gate], [w["ln_g"], w["ln_b"], w["r_k"]], tb=TB, name=n("rwkv_post"),
                   out_dtypes=[BF16])
    xc = _conv_fwd(xl, w["lru_conv_w"], w["lru_conv_b"], tb=TB, name=n("lru_conv"))
    lc = [w[k] for k in ("lru_wa", "lru_b_a", "lru_wx", "lru_b_x", "lru_lam")]
    a_l, bx = _stage(f_lru_pre, [xc], lc, tb=TB, name=n("lru_pre"), out_dtypes=[F32] * 2, pos=True)
    h = _lru_scan_fwd(a_l, bx, tb=SCAN_TB, name=n("lru_scan"))
    (yd,) = _stage(f_lru_post, [h, gl], [], tb=TB, name=n("lru_post"), out_dtypes=[BF16])
    mo = _mm(yc, w["out_a"], name=n("out_a"))
    mo = _mm(yd, w["out_b"], add=mo, name=n("out_b"))
    res = dict(rw=rw, xl=xl, gl=gl, f=f, r=r_, dec=dec, k2=k2, v=v, kkn=kkn, a=a, gate=gate, y=y, ck=ck, hist=hist, yc=yc,
               xc=xc, a_l=a_l, h=h, yd=yd)
    return mo, res


def _odd_bwd(dmo, hn, w, r, tag):
    n = lambda s: f"{tag}_{s}"
    g = {}
    g["out_a"] = _mm(r["yc"], dmo, ta=True, name=n("d_out_a"))
    g["out_b"] = _mm(r["yd"], dmo, ta=True, name=n("d_out_b"))
    dyc = _mm(dmo, w["out_a"], tb=True, name=n("dyc"))
    dyd = _mm(dmo, w["out_b"], tb=True, name=n("dyd"))
    dh, dgl = _stage_vjp(f_lru_post, [r["h"], r["gl"]], [], [dyd], tb=TB, name=n("lru_post_b"), drow=[0, 1], dconst=[])
    da_l, dbx = _lru_scan_bwd(r["a_l"], r["h"], dh, tb=SCAN_TB, name=n("lru_scan_b"))
    lc = [w[k] for k in ("lru_wa", "lru_b_a", "lru_wx", "lru_b_x", "lru_lam")]
    dxc, g["lru_wa"], g["lru_b_a"], g["lru_wx"], g["lru_b_x"], g["lru_lam"] = _stage_vjp(
        f_lru_pre, [r["xc"]], lc, [da_l, dbx], tb=TBH, name=n("lru_pre_b"), drow=[0], dconst=[0, 1, 2, 3, 4], pos=True)
    dxl, g["lru_conv_w"], g["lru_conv_b"] = _conv_bwd(r["xl"], w["lru_conv_w"], dxc, tb=TB, name=n("lru_conv_b"))
    dy, dr1, dk1, dv1, dgate, g["ln_g"], g["ln_b"], g["r_k"] = _stage_vjp(
        f_rwkv_post, [r["y"], r["r"], r["k2"], r["v"], r["gate"]], [w["ln_g"], w["ln_b"], w["r_k"]], [dyc], tb=TBH,
        name=n("rwkv_post_b"), drow=[0, 1, 2, 3, 4], dconst=[0, 1, 2])
    t3 = lambda z: z.reshape(-1, RW_PAIRS, 128)
    dr2, ddec, dk2, dv2, dkkn, da = [z.reshape(-1, RW_W) for z in _rwkv_scan_bwd(
        t3(r["r"]), t3(r["dec"]), t3(r["k2"]), t3(r["v"]), t3(r["kkn"]), t3(r["a"]), r["ck"], r["hist"], t3(dy),
        lc=RW_LC, name=n("rwkv_scan_b"))]
    rc = [w[k] for k in ("w0", "w_up", "a0", "a_up", "g_up", "k_k", "k_a")]
    df, g["w0"], g["w_up"], g["a0"], g["a_up"], g["g_up"], g["k_k"], g["k_a"] = _stage_vjp(
        f_rwkv_pre_pass, [r["f"]], rc, [dr2, ddec, dk2, dv2, dkkn, da, dgate, dr1, dk1, dv1], tb=TBH,
        name=n("rwkv_pre_b"), drow=[0], dconst=[0, 1, 2, 3, 4, 5, 6])
    drw, g["mix_w"], _ = _conv_bwd(r["rw"], w["mix_w"], df, tb=TB, name=n("rwkv_shift_b"))
    segs = (("in_rw", drw), ("in_xl", dxl), ("in_gl", dgl))
    dhn = None
    for key, dseg in segs:
        g[key] = _mm(hn, dseg, ta=True, name=n("d_" + key))
        dhn = _mm(dseg, w[key], tb=True, add=dhn, name=n("dhn_" + key))
    return dhn, g


def _layer_fwd(h, p_i, w, odd, tag):
    n = lambda s: f"{tag}_{s}"
    (hn,) = _stage(f_norm, [h], [w["norm_mix"]], tb=TB, name=n("norm_mix"), out_dtypes=[BF16])
    mo, mres = (_odd_fwd if odd else _even_fwd)(hn, w, tag)
    h1, hf = _stage(f_add_norm, [h, mo], [w["norm_ffn"]], tb=TB, name=n("norm_ffn"), out_dtypes=[F32, BF16])
    u, act = _mm(hf, w["mlp_w1"], name=n("mlp_up"), epilogue=lambda acc: (acc,) + f_relu2(acc), out_dtypes=[F32, BF16])
    m2 = _mm(act, w["mlp_w2"], name=n("mlp_down"))
    h2, hp = _stage(f_add_norm, [h1, m2], [w["norm_pl"]], tb=TB, name=n("norm_pl"), out_dtypes=[F32, BF16])
    gl = _mm(hp, w["pl_gate"], name=n("pl_gate"))
    pp = _mm(p_i, w["pl_proj"], name=n("pl_proj"))
    (h3,) = _stage(f_plgate, [h2, gl, pp], [], tb=TB, name=n("pl_mix"), out_dtypes=[F32])
    res = dict(h=h, hn=hn, mo=mo, mix=mres, h1=h1, hf=hf, u=u, act=act, m2=m2, h2=h2, hp=hp, gl=gl, pp=pp)
    return h3, res


def _layer_bwd(dh3, p_i, w, r, odd, tag, stacks):
    n = lambda s: f"{tag}_{s}"
    g = {}
    wgrad = lambda key, x, dy, cols_cut, shard: _mm_grad(x, dy, layer=int(odd), cols_cut=cols_cut, shard=shard,
                                                        prev=stacks[key] if stacks else None, name=n("d_" + key))
    dh2, dgl, dpp = _stage_vjp(f_plgate, [r["h2"], r["gl"], r["pp"]], [], [dh3], tb=TB, name=n("pl_mix_b"),
                               drow=[0, 1, 2], dconst=[])
    g["pl_proj"] = wgrad("pl_proj", p_i, dpp, True, (PL_DIM, D // 4))
    g["pl_gate"] = wgrad("pl_gate", r["hp"], dgl, False, (D // 4, D))
    dhp = _mm(dgl, w["pl_gate"], tb=True, name=n("dhp"))
    dh1, dm2, g["norm_pl"] = _stage_vjp(f_add_norm, [r["h1"], r["m2"]], [w["norm_pl"]], [dh2, dhp], tb=TB,
                                        name=n("norm_pl_b"), drow=[0, 1], dconst=[0])
    g["mlp_w2"] = wgrad("mlp_w2", r["act"], dm2, False, (D_FF // 4, D))
    (du,) = _mm(dm2, w["mlp_w2"], tb=True, name=n("dact"), extra=[r["u"]], out_dtypes=[BF16],
                epilogue=lambda acc, u: (acc * (2.0 * jnp.maximum(u, 0.0)),))
    g["mlp_w1"] = wgrad("mlp_w1", r["hf"], du, True, (D, D_FF // 4))
    dhf = _mm(du, w["mlp_w1"], tb=True, name=n("dhf"))
    dh, dmo, g["norm_ffn"] = _stage_vjp(f_add_norm, [r["h"], r["mo"]], [w["norm_ffn"]], [dh1, dhf], tb=TB,
                                        name=n("norm_ffn_b"), drow=[0, 1], dconst=[0])
    dhn, gm = (_odd_bwd if odd else _even_bwd)(dmo, r["hn"], w, r["mix"], tag)
    g.update(gm)
    dh0, g["norm_mix"] = _stage_vjp(f_norm_pass, [r["h"]], [w["norm_mix"]], [dhn, dh], tb=TB, name=n("norm_mix_b"),
                                    drow=[0], dconst=[0])
    return dh0, g


def _pad_to(a, size, axis):
    pad = [(0, 0)] * a.ndim
    pad[axis] = (0, size - a.shape[axis])
    return jnp.pad(a, pad)


def _rw_pad(a):
    return jnp.concatenate([a[..., :3072], _pad_to(a[..., 3072:3168], 128, -1), _pad_to(a[..., 3168:3264], 128, -1),
                            a[..., 3264:3520]], axis=-1)


def _rw_unpad(a):
    return jnp.concatenate([a[..., :3072], a[..., 3072:3168], a[..., 3200:3296], a[..., 3328:3584]], axis=-1)


def _block_diag(w):
    nb, bs, _ = w.shape
    eye = jnp.eye(nb, dtype=w.dtype)
    return (w[:, :, None, :] * eye[:, None, :, None]).reshape(nb * bs, nb * bs)


def _diag_blocks(w):
    nb = LRU_B
    bs = w.shape[0] // nb
    return jnp.stack([w[h * bs:(h + 1) * bs, h * bs:(h + 1) * bs] for h in range(nb)])


def _s5_prep_inputs(fw):
    lstep = jnp.broadcast_to(_pad_to(fw["s5_log_step"].astype(F32), 128, 1), (8, 128))
    t16 = lambda b: jnp.transpose(b[0], (2, 0, 1)).reshape(S5_G, S5_N)
    tc = lambda c: jnp.transpose(c[0], (0, 2, 1)).reshape(S5_N, S5_G)
    return [fw["s5_lam_re"].reshape(1, S5_N), fw["s5_lam_im"].reshape(1, S5_N), lstep,
            t16(fw["s5_b_re"]), t16(fw["s5_b_im"]), tc(fw["s5_c_re"]), tc(fw["s5_c_im"])]


def _layer_weights(fw, i):
    w = {k: fw[k][i:i + 1] for k in ("norm_mix", "norm_ffn", "norm_pl")}
    for k in ("mlp_w1", "mlp_w2", "pl_proj", "pl_gate"):
        w[k] = (fw[k], i)
    return w


def _even_weights(fw, prep):
    w = _layer_weights(fw, 0)
    ein, eout = fw["e_in_proj"][0], fw["e_out_proj"][0]
    w.update(in_u=ein[:, :512], in_z=ein[:, 512:2048], in_xbc=ein[:, 2048:4608], in_dt=_pad_to(ein[:, 4608:], 128, 1),
             out_a=eout[:512], out_b=eout[512:])
    abar_re, abar_im, wb_re, wb_im, wc_re, wc_im = prep
    w.update(abar_re=abar_re, abar_im=abar_im, wb_re=wb_re, wb_im=wb_im, wc_re=wc_re.astype(BF16), wc_im=wc_im.astype(BF16),
             s5_d=fw["s5_d"], glu_w=fw["s5_glu_w"][0], glu_b=fw["s5_glu_b"],
             ssd_conv_w=_pad_to(fw["ssd_conv_w"][0], 8, 0), ssd_conv_b=fw["ssd_conv_b"],
             dt_bias=_pad_to(fw["ssd_dt_bias"], 128, 1), a_log=_pad_to(fw["ssd_a_log"], 128, 1),
             ssd_d=_pad_to(fw["ssd_d"], 128, 1), ssd_norm=fw["ssd_norm"])
    return w


def _odd_weights(fw):
    w = _layer_weights(fw, 1)
    oin, oout = fw["o_in_proj"][0], fw["o_out_proj"][0]
    mu = _rw_pad(fw["rwkv_mu"])
    zero = jnp.zeros_like(mu)
    w.update(in_rw=_rw_pad(oin[:, :RW_IN]), in_xl=oin[:, RW_IN:RW_IN + LRU_W], in_gl=oin[:, RW_IN + LRU_W:],
             out_a=oout[:RW_W], out_b=oout[RW_W:],
             mix_w=jnp.concatenate([zero, zero, mu, 1.0 - mu, zero, zero, zero, zero], axis=0), mix_b=zero,
             w0=fw["rwkv_w0"], w_up=_pad_to(fw["rwkv_w_up"][0], 128, 0), a0=fw["rwkv_a0"],
             a_up=_pad_to(fw["rwkv_a_up"][0], 128, 0), g_up=fw["rwkv_g_up"][0], k_k=fw["rwkv_k_k"], k_a=fw["rwkv_k_a"],
             r_k=fw["rwkv_r_k"].reshape(1, RW_W), ln_g=fw["rwkv_ln_g"], ln_b=fw["rwkv_ln_b"],
             lru_conv_w=_pad_to(fw["lru_conv_w"][0], 8, 0), lru_conv_b=fw["lru_conv_b"],
             lru_wa=_block_diag(fw["lru_w_a"][0]).astype(BF16), lru_b_a=fw["lru_b_a"].reshape(1, LRU_W),
             lru_wx=_block_diag(fw["lru_w_x"][0]).astype(BF16), lru_b_x=fw["lru_b_x"].reshape(1, LRU_W),
             lru_lam=fw["lru_lam"].reshape(1, LRU_W))
    return w


def _global_grads(g0, g1, s5_grads, d_norm_final):
    out = {k: jnp.concatenate([g0[k], g1[k]], axis=0) for k in ("norm_mix", "norm_ffn", "norm_pl")}
    for k in STACKED:
        out[k] = g0[k]
    out["e_in_proj"] = jnp.concatenate([g0["in_u"], g0["in_z"], g0["in_xbc"], g0["in_dt"][:, :SSD_H]], axis=1)[None]
    out["e_out_proj"] = jnp.concatenate([g0["out_a"], g0["out_b"]], axis=0)[None]
    d_lam_re, d_lam_im, d_lstep, d_bre, d_bim, d_cre, d_cim = s5_grads
    out["s5_lam_re"] = d_lam_re.reshape(1, S5_GROUPS, S5_P)
    out["s5_lam_im"] = d_lam_im.reshape(1, S5_GROUPS, S5_P)
    out["s5_log_step"] = d_lstep[0:1, :S5_GROUPS]
    unb = lambda b: jnp.transpose(b.reshape(S5_G, S5_GROUPS, S5_P), (1, 2, 0))[None]
    unc = lambda c: jnp.transpose(c.reshape(S5_GROUPS, S5_P, S5_G), (0, 2, 1))[None]
    out.update(s5_b_re=unb(d_bre), s5_b_im=unb(d_bim), s5_c_re=unc(d_cre), s5_c_im=unc(d_cim),
               s5_d=g0["s5_d"], s5_glu_w=g0["glu_w"][None], s5_glu_b=g0["glu_b"],
               ssd_conv_w=g0["ssd_conv_w"][None, :4], ssd_conv_b=g0["ssd_conv_b"], ssd_dt_bias=g0["dt_bias"][:, :SSD_H],
               ssd_a_log=g0["a_log"][:, :SSD_H], ssd_d=g0["ssd_d"][:, :SSD_H], ssd_norm=g0["ssd_norm"])
    out["o_in_proj"] = jnp.concatenate([_rw_unpad(g1["in_rw"]), g1["in_xl"], g1["in_gl"]], axis=1)[None]
    out["o_out_proj"] = jnp.concatenate([g1["out_a"], g1["out_b"]], axis=0)[None]
    out.update(rwkv_mu=_rw_unpad(g1["mix_w"][2:3] - g1["mix_w"][3:4]), rwkv_w0=g1["w0"], rwkv_w_up=g1["w_up"][None, :RW_LORA],
               rwkv_a0=g1["a0"], rwkv_a_up=g1["a_up"][None, :RW_LORA], rwkv_g_up=g1["g_up"][None], rwkv_k_k=g1["k_k"],
               rwkv_k_a=g1["k_a"], rwkv_r_k=g1["r_k"].reshape(1, RW_H, RW_HD), rwkv_ln_g=g1["ln_g"], rwkv_ln_b=g1["ln_b"],
               lru_conv_w=g1["lru_conv_w"][None, :4], lru_conv_b=g1["lru_conv_b"],
               lru_w_a=_diag_blocks(g1["lru_wa"])[None], lru_b_a=g1["lru_b_a"].reshape(1, LRU_B, 64),
               lru_w_x=_diag_blocks(g1["lru_wx"])[None], lru_b_x=g1["lru_b_x"].reshape(1, LRU_B, 64),
               lru_lam=g1["lru_lam"].reshape(1, LRU_B, 64), norm_final=d_norm_final.reshape(D))
    return out


def _local_step(x, p, target, fw):
    prep_in = _s5_prep_inputs(fw)
    prep = _single(f_s5_prep, prep_in, name="s5_prep")
    w0, w1 = _even_weights(fw, prep), _odd_weights(fw)
    h1, r0 = _layer_fwd(x, p[0], w0, False, "l0")
    h2, r1 = _layer_fwd(h1, p[1], w1, True, "l1")
    gf = fw["norm_final"].reshape(1, D)
    (loss8,) = _stage(f_loss, [h2, target], [gf], tb=TB, name="loss", out_dtypes=[], n_acc=1)
    one = jnp.zeros((8, 128), F32).at[0, 0].set(1.0)
    dh2, d_gf = _stage_vjp(f_loss, [h2, target], [gf], [], tb=TB, name="loss_b", drow=[0], dconst=[0], acc_cots=[one])
    dh1, g1 = _layer_bwd(dh2, p[1], w1, r1, True, "l1", None)
    dx, g0 = _layer_bwd(dh1, p[0], w0, r0, False, "l0", g1)
    cots = [g0[k] for k in ("abar_re", "abar_im", "wb_re", "wb_im", "wc_re", "wc_im")]
    s5_grads = _single_vjp(f_s5_prep, prep_in, cots, name="s5_prep_b")
    return loss8[0, 0], dx, _global_grads(g0, g1, s5_grads, d_gf)


def _xyc():
    return lax.axis_index("x"), lax.axis_index("y"), lax.axis_index("c")


def _flip(v, bit):
    return 1 - v if bit else v


def _remote(src, dst, send_sems, recv_sems, k, dev):
    return pltpu.make_async_remote_copy(src_ref=src, dst_ref=dst, send_sem=send_sems.at[k], recv_sem=recv_sems.at[k],
                                        device_id=dev, device_id_type=MESH)


CHIP_FLIPS = ((1, 0), (0, 1), (1, 1))


def _gather_chips(arrs, out_shapes, places, *, name):
    n = len(arrs)

    def body(*refs):
        ins, outs = refs[:n], refs[n:2 * n]
        send_sems, recv_sems = refs[2 * n:]
        x, y, c = _xyc()
        chip, sib = 2 * x + y, (x, y, 1 - c)
        peers = [(_flip(x, fx), _flip(y, fy)) for fx, fy in CHIP_FLIPS]
        first = [_remote(ins[a].at[c], places[a](outs[a], chip, c), send_sems, recv_sems, 6 * a + j, (px, py, c))
                 for a in range(n) for j, (px, py) in enumerate(peers)]
        for cp in first:
            cp.start()
        passed = []
        for a in range(n):
            for j, (px, py) in enumerate(peers):
                landed = places[a](outs[a], 2 * px + py, c)
                _remote(ins[a].at[c], landed, send_sems, recv_sems, 6 * a + j, (px, py, c)).wait_recv()
                cp = _remote(landed, landed, send_sems, recv_sems, 6 * a + 3 + j, sib)
                cp.start()
                passed.append(cp)
        for a in range(n):
            for j, (px, py) in enumerate(peers):
                other = places[a](outs[a], 2 * px + py, 1 - c)
                _remote(other, other, send_sems, recv_sems, 6 * a + 3 + j, sib).wait_recv()
        for cp in first + passed:
            cp.wait_send()

    return pl.pallas_call(
        body, out_shape=[SDS(s, a.dtype) for s, a in zip(out_shapes, arrs)], in_specs=[ANY] * n, out_specs=[ANY] * n,
        scratch_shapes=_dma_sems(6 * n), name=name,
    )(*arrs)


def _place_own(full, own, chip_vec, axis, *, name):
    layers, rows, cols = own.shape
    tb = min(rows, 512)
    per = rows // tb
    omap = ((lambda l, i, chip_ref: (l, chip_ref[0] * per + i, 0)) if axis == 1
            else (lambda l, i, chip_ref: (l, i, chip_ref[0])))

    def body(chip_ref, own_ref, full_ref, o_ref):
        o_ref[...] = own_ref[...]

    return pl.pallas_call(
        body,
        grid_spec=pltpu.PrefetchScalarGridSpec(
            num_scalar_prefetch=1, grid=(layers, per),
            in_specs=[pl.BlockSpec((None, tb, cols), lambda l, i, chip_ref: (l, i, 0)), ANY],
            out_specs=pl.BlockSpec((None, tb, cols), omap)),
        out_shape=SDS(full.shape, full.dtype), input_output_aliases={2: 0},
        compiler_params=_cparams(("arbitrary", "arbitrary")), name=name,
    )(chip_vec, own, full)


def _dma_sems(n):
    return [pltpu.SemaphoreType.DMA((n,)), pltpu.SemaphoreType.DMA((n,))]


def _send_halves(arrs, *, name):
    n = len(arrs)

    def body(*refs):
        ins, outs = refs[:n], refs[n:2 * n]
        send_sems, recv_sems = refs[2 * n:]
        x, y, c = _xyc()
        copies = [_remote(ins[a].at[k, 1 - c], outs[a].at[k], send_sems, recv_sems, 4 * a + k, (x, y, 1 - c))
                  for a in range(n) for k in range(arrs[a].shape[0])]
        for cp in copies:
            cp.start()
        for cp in copies:
            cp.wait_recv()
        for cp in copies:
            cp.wait_send()

    return pl.pallas_call(
        body, out_shape=[SDS(a.shape[:1] + a.shape[2:], a.dtype) for a in arrs], in_specs=[ANY] * n, out_specs=[ANY] * n,
        scratch_shapes=_dma_sems(4 * n), name=name,
    )(*arrs)


def _add_half(g, recv, c_vec, *, tb, out_dtype, name):
    slots, _, rh, cols = g.shape
    tb = min(tb, rh)

    def body(c_ref, g_ref, r_ref, o_ref):
        o_ref[...] = (g_ref[...] + r_ref[...]).astype(o_ref.dtype)

    return pl.pallas_call(
        body,
        grid_spec=pltpu.PrefetchScalarGridSpec(
            num_scalar_prefetch=1, grid=(slots, rh // tb),
            in_specs=[pl.BlockSpec((None, None, tb, cols), lambda k, i, c_ref: (k, c_ref[0], i, 0)),
                      pl.BlockSpec((None, tb, cols), lambda k, i, c_ref: (k, i, 0))],
            out_specs=pl.BlockSpec((None, tb, cols), lambda k, i, c_ref: (k, i, 0))),
        out_shape=SDS((slots, rh, cols), out_dtype), compiler_params=_cparams(("arbitrary", "arbitrary")), name=name,
    )(c_vec, g, recv)


def _scatter_chips(arrs, *, name):
    n = len(arrs)

    def body(*refs):
        ins, outs = refs[:n], refs[n:2 * n]
        send_sems, recv_sems = refs[2 * n:]
        x, y, c = _xyc()
        copies = []
        for a in range(n):
            for j, (fx, fy) in enumerate(CHIP_FLIPS):
                px, py = _flip(x, fx), _flip(y, fy)
                mine = ins[a].at[2 * px + py if arrs[a].shape[0] == 4 else 0]
                copies.append(_remote(mine, outs[a].at[j], send_sems, recv_sems, 3 * a + j, (px, py, c)))
        for cp in copies:
            cp.start()
        for cp in copies:
            cp.wait_recv()
        for cp in copies:
            cp.wait_send()

    return pl.pallas_call(
        body, out_shape=[SDS((3,) + a.shape[1:], a.dtype) for a in arrs], in_specs=[ANY] * n, out_specs=[ANY] * n,
        scratch_shapes=_dma_sems(3 * n), name=name,
    )(*arrs)


def _sum_chips(p, landed, chip_vec, *, tb, name):
    _, rh, cols = p.shape
    tb = min(tb, rh)

    def body(chip_ref, p_ref, l_ref, o_ref):
        f = lambda z: z.astype(F32)
        o_ref[...] = ((f(p_ref[...]) + f(l_ref[0])) + f(l_ref[1])) + f(l_ref[2])

    return pl.pallas_call(
        body,
        grid_spec=pltpu.PrefetchScalarGridSpec(
            num_scalar_prefetch=1, grid=(rh // tb,),
            in_specs=[pl.BlockSpec((None, tb, cols), lambda i, chip_ref: (chip_ref[0], i, 0)),
                      pl.BlockSpec((3, tb, cols), lambda i, chip_ref: (0, i, 0))],
            out_specs=pl.BlockSpec((tb, cols), lambda i, chip_ref: (i, 0))),
        out_shape=SDS((rh, cols), F32), compiler_params=_cparams(), name=name,
    )(chip_vec, p, landed)


def _sum_chips_ordered(p, landed, chip_vec, *, tb, name):
    _, rh, cols = p.shape
    tb = min(tb, rh)

    def body(chip_ref, p_ref, l_ref, o_ref):
        chip = chip_ref[0]
        acc = None
        for k in range(4):
            away = k ^ chip
            slot = jnp.where(away == 2, 0, jnp.where(away == 1, 1, 2))
            term = jnp.where(k == chip, p_ref[...], l_ref[slot])
            acc = term if acc is None else acc + term
        o_ref[...] = acc

    return pl.pallas_call(
        body,
        grid_spec=pltpu.PrefetchScalarGridSpec(
            num_scalar_prefetch=1, grid=(rh // tb,),
            in_specs=[pl.BlockSpec((None, tb, cols), lambda i, chip_ref: (0, i, 0)),
                      pl.BlockSpec((3, tb, cols), lambda i, chip_ref: (0, i, 0))],
            out_specs=pl.BlockSpec((tb, cols), lambda i, chip_ref: (i, 0))),
        out_shape=SDS((rh, cols), F32), compiler_params=_cparams(), name=name,
    )(chip_vec, p, landed)


def _swap_halves(arrs, *, name):
    n = len(arrs)

    def body(*refs):
        ins, outs = refs[:n], refs[n:2 * n]
        send_sems, recv_sems = refs[2 * n:]
        x, y, c = _xyc()
        copies = [_remote(ins[a], outs[a], send_sems, recv_sems, a, (x, y, 1 - c)) for a in range(n)]
        for cp in copies:
            cp.start()
        for cp in copies:
            cp.wait_recv()
        for cp in copies:
            cp.wait_send()

    return pl.pallas_call(
        body, out_shape=[SDS(a.shape, a.dtype) for a in arrs], in_specs=[ANY] * n, out_specs=[ANY] * n,
        scratch_shapes=_dma_sems(n), name=name,
    )(*arrs)


def _join_halves(mine, theirs, c_vec, *, tb, name):
    rh, cols = mine.shape
    tb = min(tb, rh)

    def body(c_ref, m_ref, t_ref, o_ref):
        o_ref[...] = jnp.where(pl.program_id(0) == c_ref[0], m_ref[...], t_ref[...])

    blk = pl.BlockSpec((tb, cols), lambda h, i, c_ref: (i, 0))
    return pl.pallas_call(
        body,
        grid_spec=pltpu.PrefetchScalarGridSpec(
            num_scalar_prefetch=1, grid=(2, rh // tb), in_specs=[blk, blk],
            out_specs=pl.BlockSpec((None, tb, cols), lambda h, i, c_ref: (h, i, 0))),
        out_shape=SDS((2, rh, cols), mine.dtype), compiler_params=_cparams(("arbitrary", "arbitrary")), name=name,
    )(c_vec, mine, theirs)


def f_adamw(w, g, m, v):
    m = ADAM_B1 * m + (1.0 - ADAM_B1) * g
    v = ADAM_B2 * v + (1.0 - ADAM_B2) * (g * g)
    m_hat = m / (1.0 - ADAM_B1 ** ADAM_STEP)
    v_hat = v / (1.0 - ADAM_B2 ** ADAM_STEP)
    return -ADAM_LR * (m_hat / (jnp.sqrt(v_hat) + ADAM_EPS) + ADAM_WD * w), m, v


def _adamw(w, g, m, v, *, name):
    shape = w.shape
    two = lambda a: a.reshape(-1, shape[-1])
    rows = two(w).shape[0]
    tb = 256 if rows % 256 == 0 else rows
    outs = _stage(f_adamw, [two(w), two(g), two(m), two(v)], [], tb=tb, name=name, out_dtypes=[F32] * 3)
    return [o.reshape(shape) for o in outs]


def _pack(arrs, rows=8):
    flat = jnp.concatenate([a.astype(F32).reshape(-1) for a in arrs])
    size = -(-flat.shape[0] // (rows * 128)) * (rows * 128)
    return _pad_to(flat, size, 0).reshape(-1, 128)


def _unpack(buf, shapes):
    flat = buf.reshape(-1)
    out, off = [], 0
    for s in shapes:
        n = math.prod(s)
        out.append(flat[off:off + n].reshape(s))
        off += n
    return out


WEIGHTS = ("norm_mix", "norm_ffn", "norm_pl", "mlp_w1", "mlp_w2", "pl_proj", "pl_gate", "e_in_proj", "e_out_proj",
           "s5_lam_re", "s5_lam_im", "s5_log_step", "s5_b_re", "s5_b_im", "s5_c_re", "s5_c_im", "s5_d", "s5_glu_w",
           "s5_glu_b", "ssd_conv_w", "ssd_conv_b", "ssd_dt_bias", "ssd_a_log", "ssd_d", "ssd_norm", "o_in_proj",
           "o_out_proj", "rwkv_mu", "rwkv_w0", "rwkv_w_up", "rwkv_a0", "rwkv_a_up", "rwkv_g_up", "rwkv_k_k", "rwkv_k_a",
           "rwkv_r_k", "rwkv_ln_g", "rwkv_ln_b", "lru_conv_w", "lru_conv_b", "lru_w_a", "lru_b_a", "lru_w_x", "lru_b_x",
           "lru_lam", "norm_final")
BIG = ("mlp_w1", "mlp_w2", "pl_proj", "pl_gate", "e_in_proj", "e_out_proj", "o_in_proj", "o_out_proj")
STACKED = BIG[:4]
SHARD_AXIS = {"mlp_w1": 2, "mlp_w2": 1, "pl_proj": 2, "pl_gate": 1, "e_in_proj": 2, "e_out_proj": 1, "s5_glu_w": 1,
              "ssd_conv_w": 2, "o_in_proj": 2, "o_out_proj": 1, "rwkv_mu": 1, "rwkv_w0": 1, "rwkv_w_up": 2, "rwkv_a0": 1,
              "rwkv_a_up": 2, "rwkv_g_up": 2, "rwkv_k_k": 1, "rwkv_k_a": 1, "rwkv_ln_g": 1, "rwkv_ln_b": 1,
              "lru_conv_w": 2, "lru_conv_b": 1}
SMALL = tuple(n for n in WEIGHTS if n not in BIG)
SMALL_SHARDED = tuple(n for n in SMALL if n in SHARD_AXIS)


def _gather_weights(w):
    shapes = [w[n].shape for n in SMALL_SHARDED]
    chip = 2 * lax.axis_index("x") + lax.axis_index("y")
    mine = [w[n].astype(BF16) for n in BIG] + [_pack([w[n] for n in SMALL_SHARDED], rows=16)]
    out_shapes, places = [], []
    for n, a in zip(BIG + ("small",), mine):
        layers, rows, cols = a.shape if a.ndim == 3 else (1,) + a.shape
        ax = SHARD_AXIS.get(n)
        if ax == 1:
            step = rows if layers == 2 else rows // 2
            out_shapes.append((layers, 4 * rows, cols))
            places.append(lambda o, k, h, layers=layers, rows=rows, step=step: o.at[
                h if layers == 2 else 0, pl.ds(pl.multiple_of(k * rows + (0 if layers == 2 else h * step), 16), step), :])
        elif ax == 2 and layers == 2:
            out_shapes.append((layers, rows, 4 * cols))
            places.append(lambda o, k, h, cols=cols: o.at[h, :, pl.ds(pl.multiple_of(k * cols, 128), cols)])
        else:
            out_shapes.append((4, 2, layers * rows // 2, cols))
            places.append(lambda o, k, h: o.at[k, h])
    got = _gather_chips([a.reshape(2, -1, a.shape[-1]) for a in mine], out_shapes, places, name="gather_weights")
    fw = {n: w[n] for n in SMALL if n not in SHARD_AXIS}
    for n, g, a in zip(BIG, got[:-1], mine):
        if g.shape[0] == 4:
            g = lax.dynamic_update_index_in_dim(g.reshape((4,) + a.shape), a, chip, 0)
            fw[n] = jnp.concatenate([g[k] for k in range(4)], axis=SHARD_AXIS[n])
        else:
            fw[n] = _place_own(g, a, chip.astype(jnp.int32).reshape(1), SHARD_AXIS[n], name=f"place_{n}")
    small = lax.dynamic_update_index_in_dim(got[-1].reshape((4,) + mine[-1].shape), mine[-1], chip, 0)
    parts = [_unpack(small[k], shapes) for k in range(4)]
    for i, n in enumerate(SMALL_SHARDED):
        fw[n] = jnp.concatenate([parts[k][i] for k in range(4)], axis=SHARD_AXIS[n])
    return fw


def _reduce(grads, w, chip, loss):
    stacks = []
    for n in BIG:
        cols = w[n].shape[-1]
        stacks.append(grads[n] if n in STACKED else
                      jnp.stack(jnp.split(grads[n], 4, axis=SHARD_AXIS[n])).reshape(4, 2, -1, cols))
    shapes = [grads[n].shape for n in SMALL] + [(1,)]
    small = _pack([grads[n] for n in SMALL] + [loss.reshape(1)], rows=1024).reshape(1, 2, -1, 128)
    c_vec = lax.axis_index("c").astype(jnp.int32).reshape(1)
    chip_vec = chip.astype(jnp.int32).reshape(1)
    got = _send_halves(stacks + [small], name="reduce_pair")
    sums = [_add_half(s, r, c_vec, tb=512, out_dtype=BF16, name=f"reduce_pair_sum_{n}")
            for n, s, r in zip(BIG, stacks, got)]
    sums.append(_add_half(small, got[-1], c_vec, tb=512, out_dtype=F32, name="reduce_pair_sum_small"))
    landed = _scatter_chips(sums, name="reduce_chips")
    halves = [_sum_chips(p, l, chip_vec, tb=256, name=f"reduce_chips_sum_{n}") for n, p, l in zip(BIG, sums, landed)]
    halves.append(_sum_chips_ordered(sums[-1], landed[-1], chip_vec, tb=512, name="reduce_chips_sum_small"))
    theirs = _swap_halves(halves, name="reduce_swap")
    names = BIG + ("small",)
    whole = [_join_halves(h, t, c_vec, tb=512, name=f"reduce_join_{n}") for n, h, t in zip(names, halves, theirs)]
    out = {n: j.reshape(w[n].shape) for n, j in zip(BIG, whole)}
    *parts, loss_sum = _unpack(whole[-1], shapes)
    for n, g in zip(SMALL, parts):
        if n in SHARD_AXIS:
            ax = SHARD_AXIS[n]
            size = w[n].shape[ax]
            g = lax.dynamic_slice_in_dim(g, chip * size, size, axis=ax)
        out[n] = g
    return out, loss_sum[0]


def kernel(x, p, norm_mix, norm_ffn, norm_pl, mlp_w1, mlp_w2, pl_proj, pl_gate, e_in_proj, e_out_proj, s5_lam_re, s5_lam_im, s5_log_step, s5_b_re, s5_b_im, s5_c_re, s5_c_im, s5_d, s5_glu_w, s5_glu_b, ssd_conv_w, ssd_conv_b, ssd_dt_bias, ssd_a_log, ssd_d, ssd_norm, o_in_proj, o_out_proj, rwkv_mu, rwkv_w0, rwkv_w_up, rwkv_a0, rwkv_a_up, rwkv_g_up, rwkv_k_k, rwkv_k_a, rwkv_r_k, rwkv_ln_g, rwkv_ln_b, lru_conv_w, lru_conv_b, lru_w_a, lru_b_a, lru_w_x, lru_b_x, lru_lam, norm_final, loss_target, m_norm_mix, m_norm_ffn, m_norm_pl, m_mlp_w1, m_mlp_w2, m_pl_proj, m_pl_gate, m_e_in_proj, m_e_out_proj, m_s5_lam_re, m_s5_lam_im, m_s5_log_step, m_s5_b_re, m_s5_b_im, m_s5_c_re, m_s5_c_im, m_s5_d, m_s5_glu_w, m_s5_glu_b, m_ssd_conv_w, m_ssd_conv_b, m_ssd_dt_bias, m_ssd_a_log, m_ssd_d, m_ssd_norm, m_o_in_proj, m_o_out_proj, m_rwkv_mu, m_rwkv_w0, m_rwkv_w_up, m_rwkv_a0, m_rwkv_a_up, m_rwkv_g_up, m_rwkv_k_k, m_rwkv_k_a, m_rwkv_r_k, m_rwkv_ln_g, m_rwkv_ln_b, m_lru_conv_w, m_lru_conv_b, m_lru_w_a, m_lru_b_a, m_lru_w_x, m_lru_b_x, m_lru_lam, m_norm_final, v_norm_mix, v_norm_ffn, v_norm_pl, v_mlp_w1, v_mlp_w2, v_pl_proj, v_pl_gate, v_e_in_proj, v_e_out_proj, v_s5_lam_re, v_s5_lam_im, v_s5_log_step, v_s5_b_re, v_s5_b_im, v_s5_c_re, v_s5_c_im, v_s5_d, v_s5_glu_w, v_s5_glu_b, v_ssd_conv_w, v_ssd_conv_b, v_ssd_dt_bias, v_ssd_a_log, v_ssd_d, v_ssd_norm, v_o_in_proj, v_o_out_proj, v_rwkv_mu, v_rwkv_w0, v_rwkv_w_up, v_rwkv_a0, v_rwkv_a_up, v_rwkv_g_up, v_rwkv_k_k, v_rwkv_k_a, v_rwkv_r_k, v_rwkv_ln_g, v_rwkv_ln_b, v_lru_conv_w, v_lru_conv_b, v_lru_w_a, v_lru_b_a, v_lru_w_x, v_lru_b_x, v_lru_lam, v_norm_final):
    given = dict(locals())
    w = {n: given[n] for n in WEIGHTS}
    m = {n: given["m_" + n] for n in WEIGHTS}
    v = {n: given["v_" + n] for n in WEIGHTS}
    chip = 2 * lax.axis_index("x") + lax.axis_index("y")

    fw = _gather_weights(w)
    loss, dx, grads = _local_step(x[0], p[:, 0], loss_target[0], fw)
    g, loss = _reduce(grads, w, chip, loss)

    delta, new_m, new_v = {}, {}, {}
    for n in BIG:
        delta[n], new_m[n], new_v[n] = _adamw(w[n], g[n], m[n], v[n], name=f"adamw_{n}")
    shapes = [w[n].shape for n in SMALL]
    packed = [_pack([d[n] for n in SMALL]) for d in (w, g, m, v)]
    for d, buf in zip((delta, new_m, new_v), _adamw(*packed, name="adamw_small")):
        d.update(zip(SMALL, _unpack(buf, shapes)))
    return (loss, dx[None], *[g[n] for n in WEIGHTS], *[delta[n] for n in WEIGHTS],
            *[new_m[n] for n in WEIGHTS], *[new_v[n] for n in WEIGHTS])
```

```python
import functools
import math

import jax
import jax.numpy as jnp
from jax import lax
from jax.experimental import pallas as pl
from jax.experimental.pallas import tpu as pltpu

F32 = jnp.float32
BF16 = jnp.bfloat16
HI = lax.Precision.HIGHEST
MESH = pl.DeviceIdType.MESH
SDS = jax.ShapeDtypeStruct
VMEM_LIMIT = 56 * 1024 * 1024
MM_VMEM_BUDGET = 40 * 1024 * 1024
ANY = pl.BlockSpec(memory_space=pl.ANY)

D = 2048
PL_DIM = 256
D_FF = 4 * D
EPS = 1e-6
S5_W, S5_G, S5_GROUPS, S5_P = 512, 16, 32, 64
S5_N = S5_GROUPS * S5_P
SSD_W, SSD_HD, SSD_H, SSD_NG, SSD_N, SSD_L = 1536, 64, 24, 4, 128, 128
SSD_CONV = SSD_W + 2 * SSD_NG * SSD_N
EVEN_IN = S5_W + SSD_W + SSD_CONV + SSD_H
EVEN_PAD = 5120
RW_W, RW_H, RW_HD = 1024, 16, 64
RW_LORA = 96
RW_GATE = 256
RW_IN = 3 * RW_W + 2 * RW_LORA + RW_GATE
RW_PAD = 3584
LRU_W, LRU_B = 1024, 16
ODD_IN = RW_IN + 2 * LRU_W
ODD_PAD = RW_PAD + 2 * LRU_W
GN_EPS = 64e-5
LRU_C = 8.0
ADAM_LR, ADAM_B1, ADAM_B2, ADAM_EPS, ADAM_WD, ADAM_STEP = 0.001, 0.9, 0.999, 1e-08, 0.01, 10


def _cparams(sem=("arbitrary",)):
    return pltpu.CompilerParams(dimension_semantics=sem, vmem_limit_bytes=VMEM_LIMIT)


def _dot16(a, b, dims=(((1,), (0,)), ((), ()))):
    return lax.dot_general(a.astype(BF16), b.astype(BF16), dims, preferred_element_type=F32)


NN = (((1,), (0,)), ((), ()))
NT = (((1,), (1,)), ((), ()))
TN = (((0,), (0,)), ((), ()))


def _split3(x):
    top = lambda z: lax.bitcast_convert_type(lax.bitcast_convert_type(z, jnp.int32) & jnp.int32(-65536), F32)
    hi = top(x)
    rest = x - hi
    mid = top(rest)
    return hi.astype(BF16), mid.astype(BF16), (rest - mid).astype(BF16)


def _sel_raw(a, b, dims, data):
    parts = _split3((a, b)[data].astype(F32))
    mask = (a, b)[1 - data].astype(BF16)
    acc = None
    for part in reversed(parts):
        ops = (part, mask) if data == 0 else (mask, part)
        term = lax.dot_general(*ops, dims, preferred_element_type=F32)
        acc = term if acc is None else acc + term
    return acc


_SEL_BACK = {(NN, 0): ("g", "m", NT, 0), (NT, 0): ("g", "m", NN, 0), (TN, 0): ("m", "g", NT, 1),
             (NN, 1): ("m", "g", TN, 1), (NT, 1): ("g", "m", TN, 0), (TN, 1): ("m", "g", NN, 1)}


@functools.partial(jax.custom_vjp, nondiff_argnums=(2, 3))
def _sel_dot(a, b, dims, data):
    return _sel_raw(a, b, dims, data)


def _sel_dot_fwd(a, b, dims, data):
    return _sel_raw(a, b, dims, data), (a, b)


def _sel_dot_bwd(dims, data, res, g):
    mask = res[1 - data]
    left, right, dims2, data2 = _SEL_BACK[(dims, data)]
    grad = _sel_raw(g if left == "g" else mask, g if right == "g" else mask, dims2, data2)
    zero = jnp.zeros_like(mask)
    return (grad, zero) if data == 0 else (zero, grad)


_sel_dot.defvjp(_sel_dot_fwd, _sel_dot_bwd)


def _tile(dim, target):
    if dim <= target:
        return dim
    t = target - target % 128
    while t > 128 and dim % t:
        t -= 128
    assert dim % t == 0, (dim, target)
    return t


def _mm(a, b, *, ta=False, tb=False, add=None, out_dtype=F32, tm=1024, tn=1024, tk=1024, name,
        epilogue=None, extra=(), out_dtypes=None):
    layer = None
    if isinstance(b, tuple):
        b, layer = b
    m, k = (a.shape[1], a.shape[0]) if ta else a.shape
    n = b.shape[-2] if tb else b.shape[-1]
    assert (b.shape[-1] if tb else b.shape[-2]) == k, (a.shape, b.shape, ta, tb)
    ins = [a, b] + ([add] if add is not None else []) + list(extra)
    out_dtypes = out_dtypes or [out_dtype]
    n_in, n_out = len(ins), len(out_dtypes)
    tm, tn = _tile(m, tm), _tile(n, tn)
    tiles = 2 * tm * tn * sum(jnp.dtype(x.dtype).itemsize for x in ins[2:]) + 2 * tm * tn * sum(
        jnp.dtype(dt).itemsize for dt in out_dtypes) + 4 * tm * tn
    per_k = 2 * (tm * a.dtype.itemsize + tn * b.dtype.itemsize)
    tk = _tile(k, max(tk, min(2048, (MM_VMEM_BUDGET - tiles) // per_k // 128 * 128)))
    nk = k // tk
    dims = (((0 if ta else 1,), (1 if tb else 0,)), ((), ()))

    def finish(acc, refs):
        res = epilogue(acc, *[r[...] for r in refs[n_in - len(extra):n_in]]) if epilogue else (acc,)
        for o_ref, val in zip(refs[n_in:n_in + n_out], res):
            o_ref[...] = val.astype(o_ref.dtype)

    def body(*refs):
        a_ref, b_ref, acc_ref = refs[0], refs[1], refs[-1]
        if nk == 1:
            acc = _dot16(a_ref[...], b_ref[...], dims)
            finish(acc + refs[2][...].astype(F32) if add is not None else acc, refs)
            return
        kk = pl.program_id(2)

        @pl.when(kk == 0)
        def _():
            acc_ref[...] = refs[2][...].astype(F32) if add is not None else jnp.zeros_like(acc_ref)

        acc_ref[...] += _dot16(a_ref[...], b_ref[...], dims)

        @pl.when(kk == nk - 1)
        def _():
            finish(acc_ref[...], refs)

    a_spec = pl.BlockSpec((tk, tm), lambda i, j, q: (q, i)) if ta else pl.BlockSpec((tm, tk), lambda i, j, q: (i, q))
    b_spec = pl.BlockSpec((tn, tk), lambda i, j, q: (j, q)) if tb else pl.BlockSpec((tk, tn), lambda i, j, q: (q, j))
    if layer is not None:
        b_spec = (pl.BlockSpec((None, tn, tk), lambda i, j, q: (layer, j, q)) if tb
                  else pl.BlockSpec((None, tk, tn), lambda i, j, q: (layer, q, j)))
    o_spec = pl.BlockSpec((tm, tn), lambda i, j, q: (i, j))
    outs = pl.pallas_call(
        body,
        grid=(m // tm, n // tn, nk),
        in_specs=[a_spec, b_spec] + [o_spec] * (n_in - 2),
        out_specs=[o_spec] * n_out,
        out_shape=[SDS((m, n), dt) for dt in out_dtypes],
        scratch_shapes=[pltpu.VMEM((tm, tn) if nk > 1 else (8, 128), F32)],
        compiler_params=_cparams(("parallel", "parallel", "arbitrary")),
        name=name,
    )(*ins)
    return outs if epilogue else outs[0]


def _mm_grad(x, dy, *, layer, cols_cut, shard, prev, name):
    t = x.shape[0]
    r, c = shard
    tm, tn = _tile(r, 1024), _tile(c, 1024)
    per_k = 2 * (tm * x.dtype.itemsize + tn * dy.dtype.itemsize)
    tk = _tile(t, max(1024, min(2048, (MM_VMEM_BUDGET - 12 * tm * tn) // per_k // 128 * 128)))
    nk = t // tk
    if cols_cut:
        assert x.shape[1] == r and dy.shape[1] == 4 * c
        per = c // tn
        omap = lambda i, j, q: (j // per, layer, i, j % per)
    else:
        assert x.shape[1] == 4 * r and dy.shape[1] == c
        per = r // tm
        omap = lambda i, j, q: (i // per, layer, i % per, j)

    def body(*refs):
        x_ref, dy_ref = refs[:2]
        o_ref, acc_ref = refs[-2:]
        kk = pl.program_id(2)

        @pl.when(kk == 0)
        def _():
            acc_ref[...] = jnp.zeros_like(acc_ref)

        acc_ref[...] += _dot16(x_ref[...], dy_ref[...], TN)

        @pl.when(kk == nk - 1)
        def _():
            o_ref[...] = acc_ref[...]

    return pl.pallas_call(
        body,
        grid=(x.shape[1] // tm, dy.shape[1] // tn, nk),
        in_specs=[pl.BlockSpec((tk, tm), lambda i, j, q: (q, i)), pl.BlockSpec((tk, tn), lambda i, j, q: (q, j))]
        + ([ANY] if prev is not None else []),
        out_specs=pl.BlockSpec((None, None, tm, tn), omap),
        out_shape=SDS((4, 2, r, c), F32),
        scratch_shapes=[pltpu.VMEM((tm, tn), F32)],
        input_output_aliases={2: 0} if prev is not None else {},
        compiler_params=_cparams(("parallel", "parallel", "arbitrary")),
        name=name,
    )(x, dy, *([prev] if prev is not None else []))


def _single(fn, consts, *, name):
    outs = jax.eval_shape(fn, *[SDS(c.shape, F32) for c in consts])
    n_in = len(consts)

    def body(*refs):
        res = fn(*[r[...] for r in refs[:n_in]])
        for o_ref, v in zip(refs[n_in:], res):
            o_ref[...] = v

    return pl.pallas_call(body, out_shape=[SDS(o.shape, F32) for o in outs],
                          compiler_params=pltpu.CompilerParams(vmem_limit_bytes=VMEM_LIMIT), name=name)(*consts)


def _single_vjp(fn, consts, cots, *, name):
    n_in = len(consts)

    def body(*refs):
        _, pull = jax.vjp(fn, *[r[...] for r in refs[:n_in]])
        grads = pull(tuple(r[...] for r in refs[n_in:n_in + len(cots)]))
        for o_ref, v in zip(refs[n_in + len(cots):], grads):
            o_ref[...] = v

    return pl.pallas_call(body, out_shape=[SDS(c.shape, F32) for c in consts],
                          compiler_params=pltpu.CompilerParams(vmem_limit_bytes=VMEM_LIMIT), name=name)(*consts, *cots)


def _full_spec(shape):
    nd = len(shape)
    return pl.BlockSpec(shape, lambda i, _n=nd: (0,) * _n)


def _stage_shapes(fn, rows, consts, tb, pos):
    rs = [SDS((tb, r.shape[1]), F32) for r in rows]
    cs = [SDS(c.shape, F32) for c in consts]
    f = (lambda *a: fn(jnp.int32(0), *a)) if pos else fn
    return jax.eval_shape(f, *rs, *cs)


def _stage(fn, rows, consts, *, tb, name, out_dtypes, n_acc=0, pos=False):
    t = rows[0].shape[0]
    assert t % tb == 0
    outs = _stage_shapes(fn, rows, consts, tb, pos)
    n_out = len(outs)
    n_row = n_out - n_acc
    n_in = len(rows) + len(consts)

    def body(*refs):
        i = pl.program_id(0)
        vals = [r[...].astype(F32) for r in refs[:n_in]]
        res = fn(i * tb, *vals) if pos else fn(*vals)
        out_refs = refs[n_in:]
        for q in range(n_row):
            out_refs[q][...] = res[q].astype(out_refs[q].dtype)
        for q in range(n_row, n_out):
            @pl.when(i == 0)
            def _(q=q):
                out_refs[q][...] = jnp.zeros_like(out_refs[q])

            out_refs[q][...] += res[q]

    in_specs = [pl.BlockSpec((tb, r.shape[1]), lambda i: (i, 0)) for r in rows] + [_full_spec(c.shape) for c in consts]
    out_specs = [pl.BlockSpec((tb, o.shape[1]), lambda i: (i, 0)) for o in outs[:n_row]] + [_full_spec(o.shape) for o in outs[n_row:]]
    out_shape = [SDS((t, o.shape[1]), dt) for o, dt in zip(outs[:n_row], out_dtypes)] + [SDS(o.shape, F32) for o in outs[n_row:]]
    return pl.pallas_call(
        body, grid=(t // tb,), in_specs=in_specs, out_specs=out_specs, out_shape=out_shape,
        compiler_params=_cparams(), name=name,
    )(*rows, *consts)


def _stage_vjp(fn, rows, consts, cots, *, tb, name, drow, dconst, drow_dtypes=None, acc_cots=(), pos=False):
    t = rows[0].shape[0]
    assert t % tb == 0
    n_rows, n_consts, n_cots, n_acc = len(rows), len(consts), len(cots), len(acc_cots)
    n_in = n_rows + n_consts + n_cots + n_acc
    drow_dtypes = drow_dtypes or [F32] * len(drow)

    def body(*refs):
        i = pl.program_id(0)
        vals = [r[...].astype(F32) for r in refs[:n_in]]
        rv, cv = vals[:n_rows], vals[n_rows:n_rows + n_consts]
        ct = tuple(vals[n_rows + n_consts:])

        def f(*dargs):
            r2, c2 = list(rv), list(cv)
            for q, idx in enumerate(drow):
                r2[idx] = dargs[q]
            for q, idx in enumerate(dconst):
                c2[idx] = dargs[len(drow) + q]
            return fn(i * tb, *r2, *c2) if pos else fn(*r2, *c2)

        _, pull = jax.vjp(f, *[rv[q] for q in drow], *[cv[q] for q in dconst])
        grads = pull(ct)
        out_refs = refs[n_in:]
        for q in range(len(drow)):
            out_refs[q][...] = grads[q].astype(out_refs[q].dtype)
        for q in range(len(drow), len(drow) + len(dconst)):
            @pl.when(i == 0)
            def _(q=q):
                out_refs[q][...] = jnp.zeros_like(out_refs[q])

            out_refs[q][...] += grads[q]

    in_specs = ([pl.BlockSpec((tb, r.shape[1]), lambda i: (i, 0)) for r in rows] + [_full_spec(c.shape) for c in consts]
                + [pl.BlockSpec((tb, c.shape[1]), lambda i: (i, 0)) for c in cots] + [_full_spec(c.shape) for c in acc_cots])
    out_specs = ([pl.BlockSpec((tb, rows[q].shape[1]), lambda i: (i, 0)) for q in drow]
                 + [_full_spec(consts[q].shape) for q in dconst])
    out_shape = ([SDS(rows[q].shape, dt) for q, dt in zip(drow, drow_dtypes)]
                 + [SDS(consts[q].shape, F32) for q in dconst])
    return pl.pallas_call(
        body, grid=(t // tb,), in_specs=in_specs, out_specs=out_specs, out_shape=out_shape,
        compiler_params=_cparams(), name=name,
    )(*rows, *consts, *cots, *acc_cots)


def _conv_fwd(x, w, b, *, tb, name):
    t, c = x.shape
    r8 = tb // 8

    def body(x_ref, p_ref, w_ref, b_ref, o_ref):
        i = pl.program_id(0)
        x_ = x_ref[...]
        p_ = jnp.where(i > 0, p_ref[...], 0.0)
        w_ = w_ref[...]
        row = lax.broadcasted_iota(jnp.int32, x_.shape, 0)
        row8 = lax.broadcasted_iota(jnp.int32, p_.shape, 0)
        acc = x_ * w_[3:4, :] + b_ref[...]
        head = jnp.zeros_like(p_)
        for j in (1, 2, 3):
            wj = w_[3 - j:4 - j, :]
            acc += jnp.where(row >= j, pltpu.roll(x_, j, 0), 0.0) * wj
            head += jnp.where(row8 < j, pltpu.roll(p_, j, 0), 0.0) * wj
        o_ref[...] = acc
        o_ref[0:8, :] += head

    return pl.pallas_call(
        body, grid=(t // tb,),
        in_specs=[pl.BlockSpec((tb, c), lambda i: (i, 0)),
                  pl.BlockSpec((8, c), lambda i: (jnp.maximum(i * r8 - 1, 0), 0)),
                  _full_spec(w.shape), _full_spec(b.shape)],
        out_specs=pl.BlockSpec((tb, c), lambda i: (i, 0)),
        out_shape=SDS((t, c), F32), compiler_params=_cparams(), name=name,
    )(x, x, w, b)


def _conv_bwd(x, w, dy, *, tb, name):
    t, c = x.shape
    r8 = tb // 8
    nb = t // tb

    def body(x_ref, p_ref, w_ref, g_ref, n_ref, dx_ref, dw_ref, db_ref):
        i = pl.program_id(0)
        x_ = x_ref[...]
        p_ = jnp.where(i > 0, p_ref[...], 0.0)
        g_ = g_ref[...]
        n_ = jnp.where(i < nb - 1, n_ref[...], 0.0)
        w_ = w_ref[...]
        row = lax.broadcasted_iota(jnp.int32, x_.shape, 0)
        row8 = lax.broadcasted_iota(jnp.int32, p_.shape, 0)
        g8 = g_[0:8, :]
        dx = g_ * w_[3:4, :]
        tail = jnp.zeros_like(n_)
        dws = [jnp.sum(g_ * x_, axis=0, keepdims=True)]
        for j in (1, 2, 3):
            wj = w_[3 - j:4 - j, :]
            dx += jnp.where(row < tb - j, pltpu.roll(g_, tb - j, 0), 0.0) * wj
            tail += jnp.where(row8 >= 8 - j, pltpu.roll(n_, 8 - j, 0), 0.0) * wj
            xs = jnp.where(row >= j, pltpu.roll(x_, j, 0), 0.0)
            ps = jnp.where(row8 < j, pltpu.roll(p_, j, 0), 0.0)
            dws.append(jnp.sum(g_ * xs, axis=0, keepdims=True) + jnp.sum(g8 * ps, axis=0, keepdims=True))
        dx_ref[...] = dx
        dx_ref[tb - 8:tb, :] += tail

        @pl.when(i == 0)
        def _():
            dw_ref[...] = jnp.zeros_like(dw_ref)
            db_ref[...] = jnp.zeros_like(db_ref)

        for j in range(4):
            dw_ref[3 - j:4 - j, :] += dws[j]
        db_ref[...] += jnp.sum(g_, axis=0, keepdims=True)

    return pl.pallas_call(
        body, grid=(nb,),
        in_specs=[pl.BlockSpec((tb, c), lambda i: (i, 0)),
                  pl.BlockSpec((8, c), lambda i: (jnp.maximum(i * r8 - 1, 0), 0)),
                  _full_spec(w.shape),
                  pl.BlockSpec((tb, c), lambda i: (i, 0)),
                  pl.BlockSpec((8, c), lambda i: (jnp.minimum((i + 1) * r8, t // 8 - 1), 0))],
        out_specs=[pl.BlockSpec((tb, c), lambda i: (i, 0)), _full_spec((8, c)), _full_spec((1, c))],
        out_shape=[SDS((t, c), F32), SDS((8, c), F32), SDS((1, c), F32)],
        compiler_params=_cparams(), name=name,
    )(x, x, w, dy, dy)


def _lru_scan_fwd(a, b, *, tb, name):
    t, c = a.shape

    def body(a_ref, b_ref, h_ref, st_ref):
        @pl.when(pl.program_id(0) == 0)
        def _():
            st_ref[...] = jnp.zeros_like(st_ref)

        def step(s, h):
            h = a_ref[pl.ds(s, 1), :] * h + b_ref[pl.ds(s, 1), :]
            h_ref[pl.ds(s, 1), :] = h
            return h

        st_ref[...] = lax.fori_loop(0, tb, step, st_ref[...], unroll=8)

    blk = pl.BlockSpec((tb, c), lambda i: (i, 0))
    return pl.pallas_call(
        body, grid=(t // tb,), in_specs=[blk, blk], out_specs=blk, out_shape=SDS((t, c), F32),
        scratch_shapes=[pltpu.VMEM((1, c), F32)], compiler_params=_cparams(), name=name,
    )(a, b)


def _lru_scan_bwd(a, h, dh, *, tb, name):
    t, c = a.shape
    nb = t // tb
    r8 = tb // 8

    def body(a_ref, h_ref, p_ref, g_ref, da_ref, db_ref, st_ref):
        i = pl.program_id(0)

        @pl.when(i == 0)
        def _():
            st_ref[...] = jnp.zeros_like(st_ref)

        hprev0 = jnp.where(i < nb - 1, p_ref[7:8, :], 0.0)

        def step(q, carry):
            s = tb - 1 - q
            g = g_ref[pl.ds(s, 1), :] + carry
            hp = h_ref[pl.ds(jnp.maximum(s - 1, 0), 1), :]
            hp = jnp.where(s > 0, hp, hprev0)
            db_ref[pl.ds(s, 1), :] = g
            da_ref[pl.ds(s, 1), :] = g * hp
            return a_ref[pl.ds(s, 1), :] * g

        st_ref[...] = lax.fori_loop(0, tb, step, st_ref[...], unroll=8)

    rev = pl.BlockSpec((tb, c), lambda i: (nb - 1 - i, 0))
    prev = pl.BlockSpec((8, c), lambda i: (jnp.maximum((nb - 1 - i) * r8 - 1, 0), 0))
    return pl.pallas_call(
        body, grid=(nb,), in_specs=[rev, rev, prev, rev], out_specs=[rev, rev],
        out_shape=[SDS((t, c), F32), SDS((t, c), F32)],
        scratch_shapes=[pltpu.VMEM((1, c), F32)], compiler_params=_cparams(), name=name,
    )(a, h, h, dh)


def _s5_scan_fwd(ar, ai, br, bi, *, tb, name):
    t, c = br.shape

    def body(ar_ref, ai_ref, br_ref, bi_ref, xr_ref, xi_ref, sr_ref, si_ref):
        @pl.when(pl.program_id(0) == 0)
        def _():
            sr_ref[...] = jnp.zeros_like(sr_ref)
            si_ref[...] = jnp.zeros_like(si_ref)

        ar_, ai_ = ar_ref[...], ai_ref[...]

        def step(s, carry):
            xr, xi = carry
            nr = ar_ * xr - ai_ * xi + br_ref[pl.ds(s, 1), :]
            ni = ar_ * xi + ai_ * xr + bi_ref[pl.ds(s, 1), :]
            xr_ref[pl.ds(s, 1), :] = nr
            xi_ref[pl.ds(s, 1), :] = ni
            return nr, ni

        xr, xi = lax.fori_loop(0, tb, step, (sr_ref[...], si_ref[...]), unroll=8)
        sr_ref[...] = xr
        si_ref[...] = xi

    blk = pl.BlockSpec((tb, c), lambda i: (i, 0))
    one = _full_spec((1, c))
    return pl.pallas_call(
        body, grid=(t // tb,), in_specs=[one, one, blk, blk], out_specs=[blk, blk],
        out_shape=[SDS((t, c), F32), SDS((t, c), F32)],
        scratch_shapes=[pltpu.VMEM((1, c), F32), pltpu.VMEM((1, c), F32)], compiler_params=_cparams(), name=name,
    )(ar, ai, br, bi)


def _s5_scan_bwd(ar, ai, xr, xi, dxr, dxi, *, tb, name):
    t, c = xr.shape
    nb = t // tb
    r8 = tb // 8

    def body(ar_ref, ai_ref, xr_ref, xi_ref, pr_ref, pi_ref, gr_ref, gi_ref,
             dbr_ref, dbi_ref, dar_ref, dai_ref, cr_ref, ci_ref):
        i = pl.program_id(0)

        @pl.when(i == 0)
        def _():
            cr_ref[...] = jnp.zeros_like(cr_ref)
            ci_ref[...] = jnp.zeros_like(ci_ref)
            dar_ref[...] = jnp.zeros_like(dar_ref)
            dai_ref[...] = jnp.zeros_like(dai_ref)

        ar_, ai_ = ar_ref[...], ai_ref[...]
        first = i == nb - 1
        pr0 = jnp.where(first, 0.0, pr_ref[7:8, :])
        pi0 = jnp.where(first, 0.0, pi_ref[7:8, :])

        def step(q, carry):
            cr, ci, dar, dai = carry
            s = tb - 1 - q
            gr = gr_ref[pl.ds(s, 1), :] + cr
            gi = gi_ref[pl.ds(s, 1), :] + ci
            sp = jnp.maximum(s - 1, 0)
            xpr = jnp.where(s > 0, xr_ref[pl.ds(sp, 1), :], pr0)
            xpi = jnp.where(s > 0, xi_ref[pl.ds(sp, 1), :], pi0)
            dbr_ref[pl.ds(s, 1), :] = gr
            dbi_ref[pl.ds(s, 1), :] = gi
            dar = dar + gr * xpr + gi * xpi
            dai = dai - gr * xpi + gi * xpr
            return ar_ * gr + ai_ * gi, ar_ * gi - ai_ * gr, dar, dai

        cr, ci, dar, dai = lax.fori_loop(0, tb, step, (cr_ref[...], ci_ref[...], dar_ref[...], dai_ref[...]), unroll=8)
        cr_ref[...] = cr
        ci_ref[...] = ci
        dar_ref[...] = dar
        dai_ref[...] = dai

    rev = pl.BlockSpec((tb, c), lambda i: (nb - 1 - i, 0))
    prev = pl.BlockSpec((8, c), lambda i: (jnp.maximum((nb - 1 - i) * r8 - 1, 0), 0))
    one = _full_spec((1, c))
    return pl.pallas_call(
        body, grid=(nb,), in_specs=[one, one, rev, rev, prev, prev, rev, rev], out_specs=[rev, rev, one, one],
        out_shape=[SDS((t, c), F32), SDS((t, c), F32), SDS((1, c), F32), SDS((1, c), F32)],
        scratch_shapes=[pltpu.VMEM((1, c), F32), pltpu.VMEM((1, c), F32)], compiler_params=_cparams(), name=name,
    )(ar, ai, xr, xi, xr, xi, dxr, dxi)


RW_PAIRS = RW_H // 2


def _pair_consts():
    sub = lax.broadcasted_iota(jnp.int32, (64, 128), 0)
    lane = lax.broadcasted_iota(jnp.int32, (64, 128), 1)
    eye2 = ((lane & 63) == sub).astype(F32)
    r2 = lax.broadcasted_iota(jnp.int32, (128, 128), 0)
    c2 = lax.broadcasted_iota(jnp.int32, (128, 128), 1)
    bsel = ((r2 >> 6) == (c2 >> 6)).astype(BF16)
    return eye2, bsel


def _segsum(x, bsel):
    rows = x.shape[0]
    bits = lax.bitcast_convert_type(x, jnp.int32)
    hi = lax.bitcast_convert_type(bits & jnp.int32(-65536), F32)
    both = jnp.concatenate([hi.astype(BF16), (x - hi).astype(BF16)], axis=0)
    res = jnp.dot(both, bsel, preferred_element_type=F32)
    return res[:rows] + res[rows:]


def _bc(x8):
    return jnp.stack([jnp.broadcast_to(x8[q:q + 1, :], (64, 128)) for q in range(RW_PAIRS)])


def _seg3(x3, bsel):
    return _segsum(x3.reshape(RW_PAIRS * 64, 128), bsel).reshape(RW_PAIRS, 64, 128)


def _seg3_lanes(x3):
    first = lax.broadcasted_iota(jnp.int32, x3.shape, 2) < 64
    lo = jnp.sum(jnp.where(first, x3, 0.0), axis=-1, keepdims=True)
    hi = jnp.sum(jnp.where(first, 0.0, x3), axis=-1, keepdims=True)
    return jnp.where(first, lo, hi)


def _rwkv_scan_fwd(r, w, k, v, kk, a, *, lc, name):
    t = r.shape[0]
    nc = t // lc

    def body(r_ref, w_ref, k_ref, v_ref, kk_ref, a_ref, y_ref, ck_ref, hist_ref, st_ref):
        @pl.when(pl.program_id(0) == 0)
        def _():
            st_ref[...] = jnp.zeros_like(st_ref)

        ck_ref[0] = st_ref[...]
        eye2, bsel = _pair_consts()
        column = lambda ref, s: _seg3(eye2[None] * _bc(ref[s]), bsel)
        read = lambda st, s: jnp.sum(eye2[None] * _seg3(st * _bc(r_ref[s]), bsel), axis=1)

        def step(s, carry):
            st, vb = carry
            kk8 = kk_ref[s]
            sa = -_seg3_lanes(st * _bc(kk8))
            vb_next = column(v_ref, jnp.minimum(s + 1, lc - 1))
            before = jnp.maximum(s - 1, 0)
            y_ref[before] = read(st, before)
            st = st * _bc(w_ref[s]) + sa * _bc(kk8 * a_ref[s]) + vb * _bc(k_ref[s])
            hist_ref[s] = st
            return st, vb_next

        st, _ = lax.fori_loop(0, lc, step, (st_ref[...], column(v_ref, 0)))
        y_ref[lc - 1] = read(st, lc - 1)
        st_ref[...] = st

    blk = pl.BlockSpec((lc, RW_PAIRS, 128), lambda i: (i, 0, 0))
    return pl.pallas_call(
        body, grid=(nc,), in_specs=[blk] * 6,
        out_specs=[blk, pl.BlockSpec((1, RW_PAIRS, 64, 128), lambda i: (i, 0, 0, 0)),
                   pl.BlockSpec((lc, RW_PAIRS, 64, 128), lambda i: (i, 0, 0, 0))],
        out_shape=[SDS((t, RW_PAIRS, 128), F32), SDS((nc, RW_PAIRS, 64, 128), F32), SDS((t, RW_PAIRS, 64, 128), F32)],
        scratch_shapes=[pltpu.VMEM((RW_PAIRS, 64, 128), F32)],
        compiler_params=_cparams(), name=name,
    )(r, w, k, v, kk, a)


def _rwkv_scan_bwd(r, w, k, v, kk, a, ck, hist, dy, *, lc, name):
    t = r.shape[0]
    nc = t // lc

    def body(r_ref, w_ref, k_ref, v_ref, kk_ref, a_ref, ck_ref, hist_ref, dy_ref,
             dr_ref, dw_ref, dk_ref, dv_ref, dkk_ref, da_ref, ds_ref):
        @pl.when(pl.program_id(0) == 0)
        def _():
            ds_ref[...] = jnp.zeros_like(ds_ref)

        eye2, bsel = _pair_consts()
        column = lambda ref, s: _seg3(eye2[None] * _bc(ref[s]), bsel)

        col = lambda z: jnp.sum(z, axis=1)

        def grads(s, s_prev, d_s, dsa):
            kk8 = kk_ref[s]
            sa = -_seg3(s_prev * _bc(kk8), bsel)
            db = col(d_s * sa)
            dw_ref[s] = col(d_s * s_prev)
            dv_ref[s] = col(eye2[None] * _seg3(d_s * _bc(k_ref[s]), bsel))
            dk_ref[s] = col(d_s * column(v_ref, s))
            dkk_ref[s] = db * a_ref[s] - col(s_prev * dsa)
            da_ref[s] = db * kk8

        def back(j, carry):
            ds, d_after, dsa_after, dyb = carry
            s = lc - 1 - j
            kk8 = kk_ref[s]
            d_s = ds + dyb * _bc(r_ref[s])
            dsa = _seg3_lanes(d_s * _bc(kk8 * a_ref[s]))
            dr_ref[s] = col(hist_ref[s] * dyb)
            dyb_before = column(dy_ref, jnp.maximum(s - 1, 0))
            after = jnp.minimum(s + 1, lc - 1)
            grads(after, hist_ref[after - 1], d_after, dsa_after)
            return d_s * _bc(w_ref[s]) - dsa * _bc(kk8), d_s, dsa, dyb_before

        zero = jnp.zeros((RW_PAIRS, 64, 128), F32)
        ds, d_first, dsa_first, _ = lax.fori_loop(0, lc, back, (ds_ref[...], zero, zero, column(dy_ref, lc - 1)))
        grads(0, ck_ref[0], d_first, dsa_first)
        ds_ref[...] = ds

    rev = pl.BlockSpec((lc, RW_PAIRS, 128), lambda i: (nc - 1 - i, 0, 0))
    return pl.pallas_call(
        body, grid=(nc,),
        in_specs=[rev] * 6 + [pl.BlockSpec((1, RW_PAIRS, 64, 128), lambda i: (nc - 1 - i, 0, 0, 0)),
                              pl.BlockSpec((lc, RW_PAIRS, 64, 128), lambda i: (nc - 1 - i, 0, 0, 0)), rev],
        out_specs=[rev] * 6, out_shape=[SDS((t, RW_PAIRS, 128), F32)] * 6,
        scratch_shapes=[pltpu.VMEM((RW_PAIRS, 64, 128), F32)],
        compiler_params=_cparams(), name=name,
    )(r, w, k, v, kk, a, ck, hist, dy)


SSD_PAIRS = SSD_H // 2


def _ssd_chunk(states, xdt, da, bm, cm):
    ln = SSD_L
    row = lax.broadcasted_iota(jnp.int32, (ln, ln), 0)
    col = lax.broadcasted_iota(jnp.int32, (ln, ln), 1)
    causal = row >= col
    acum = _sel_dot(causal.astype(F32), da, NN, 1)
    acum_t = _sel_dot(da, (row <= col).astype(F32), TN, 0)
    sub = lax.broadcasted_iota(jnp.int32, (128, 128), 0)
    lane = lax.broadcasted_iota(jnp.int32, (128, 128), 1)
    ys, new_states = [], []
    for q in range(SSD_PAIRS):
        g = q // (SSD_PAIRS // SSD_NG)
        bg = bm[:, g * SSD_N:(g + 1) * SSD_N]
        cg = cm[:, g * SSD_N:(g + 1) * SSD_N]
        xq = xdt[:, q * 128:(q + 1) * 128]
        scores = _dot16(cg, bg, NT)
        aexp = _sel_dot(acum, (sub == 2 * q + (lane >> 6)).astype(F32), NN, 0)
        tot = aexp[ln - 1:ln, :]
        yh = []
        for h in (2 * q, 2 * q + 1):
            seg = _sel_dot(acum, (sub == h).astype(F32), NN, 0) - acum_t[h:h + 1, :]
            yh.append(_dot16(scores * jnp.exp(jnp.where(causal, seg, -1e30)), xq))
        y = jnp.where(lane < 64, yh[0], yh[1]) + _dot16(cg, states[q]) * jnp.exp(aexp)
        new = _dot16(bg, xq * jnp.exp(tot - aexp), TN)
        ys.append(y)
        new_states.append(states[q] * jnp.exp(tot) + new)
    return jnp.concatenate(ys, axis=1), new_states


def _ssd_fwd(xdt, da, bm, cm, *, name):
    t = xdt.shape[0]
    nc = t // SSD_L

    def body(x_ref, a_ref, b_ref, c_ref, y_ref, ck_ref, st_ref):
        @pl.when(pl.program_id(0) == 0)
        def _():
            st_ref[...] = jnp.zeros_like(st_ref)

        ck_ref[0] = st_ref[...]
        y, new = _ssd_chunk([st_ref[q] for q in range(SSD_PAIRS)], x_ref[...], a_ref[...], b_ref[...], c_ref[...])
        y_ref[...] = y
        for q in range(SSD_PAIRS):
            st_ref[q] = new[q]

    blk = lambda wd: pl.BlockSpec((SSD_L, wd), lambda i: (i, 0))
    return pl.pallas_call(
        body, grid=(nc,), in_specs=[blk(SSD_W), blk(128), blk(512), blk(512)],
        out_specs=[blk(SSD_W), pl.BlockSpec((1, SSD_PAIRS, 128, 128), lambda i: (i, 0, 0, 0))],
        out_shape=[SDS((t, SSD_W), F32), SDS((nc, SSD_PAIRS, 128, 128), F32)],
        scratch_shapes=[pltpu.VMEM((SSD_PAIRS, 128, 128), F32)], compiler_params=_cparams(), name=name,
    )(xdt, da, bm, cm)


def _ssd_bwd(xdt, da, bm, cm, ck, dy, *, name):
    t = xdt.shape[0]
    nc = t // SSD_L

    def body(x_ref, a_ref, b_ref, c_ref, ck_ref, dy_ref, dx_ref, dda_ref, db_ref, dc_ref, ds_ref):
        @pl.when(pl.program_id(0) == 0)
        def _():
            ds_ref[...] = jnp.zeros_like(ds_ref)

        _, pull = jax.vjp(_ssd_chunk, [ck_ref[0, q] for q in range(SSD_PAIRS)], x_ref[...], a_ref[...], b_ref[...], c_ref[...])
        dst, dx, dda, db, dc = pull((dy_ref[...], [ds_ref[q] for q in range(SSD_PAIRS)]))
        dx_ref[...] = dx
        dda_ref[...] = dda
        db_ref[...] = db
        dc_ref[...] = dc
        for q in range(SSD_PAIRS):
            ds_ref[q] = dst[q]

    rev = lambda wd: pl.BlockSpec((SSD_L, wd), lambda i: (nc - 1 - i, 0))
    return pl.pallas_call(
        body, grid=(nc,),
        in_specs=[rev(SSD_W), rev(128), rev(512), rev(512),
                  pl.BlockSpec((1, SSD_PAIRS, 128, 128), lambda i: (nc - 1 - i, 0, 0, 0)), rev(SSD_W)],
        out_specs=[rev(SSD_W), rev(128), rev(512), rev(512)],
        out_shape=[SDS((t, SSD_W), F32), SDS((t, 128), F32), SDS((t, 512), F32), SDS((t, 512), F32)],
        scratch_shapes=[pltpu.VMEM((SSD_PAIRS, 128, 128), F32)], compiler_params=_cparams(), name=name,
    )(xdt, da, bm, cm, ck, dy)


def _iota(shape, dim):
    return lax.broadcasted_iota(jnp.int32, shape, dim)


def _rms(x, g):
    return x * lax.rsqrt(jnp.mean(x * x, axis=-1, keepdims=True) + EPS) * g


def _head_sel(width, shift):
    return ((_iota((width, 128), 0) >> shift) == _iota((width, 128), 1)).astype(F32)


def _head_sum(x, shift=6):
    sel = _head_sel(x.shape[1], shift)
    return _sel_dot(_sel_dot(x, sel, NN, 0), sel, NT, 0)


def _head_expand(x, width, shift=6):
    return _sel_dot(x, _head_sel(width, shift), NT, 0)


def f_norm(h, g):
    return (_rms(h, g),)


def f_norm_pass(h, g):
    return _rms(h, g), h


def f_add_norm(h, m, g):
    h1 = h + m
    return h1, _rms(h1, g)


def f_relu2(u):
    r = jnp.maximum(u, 0.0)
    return (r * r,)


def f_plgate(h2, gl, pp):
    return (h2 + jax.nn.sigmoid(gl) * pp,)


def f_loss(h, tgt, g):
    err = _rms(h, g) - tgt
    part = 0.5 * jnp.sum(jnp.mean(err * err, axis=-1, keepdims=True), axis=0, keepdims=True)
    return (jnp.broadcast_to(part, (8, 128)),)


def f_s5_prep(lam_re, lam_im, lstep, bre_t, bim_t, cre_t, cim_t):
    step = jnp.exp(_sel_dot(lstep, _head_sel(S5_N, 6), NT, 0)[0:1, :])
    mag = jnp.exp(lam_re * step)
    abar_re, abar_im = mag * jnp.cos(lam_im * step), mag * jnp.sin(lam_im * step)
    den = lam_re * lam_re + lam_im * lam_im
    nr = abar_re - 1.0
    coef_re = (nr * lam_re + abar_im * lam_im) / den
    coef_im = (abar_im * lam_re - nr * lam_im) / den
    bbar_re = coef_re * bre_t - coef_im * bim_t
    bbar_im = coef_re * bim_t + coef_im * bre_t
    rep = ((_iota((S5_W, S5_G), 0) & (S5_G - 1)) == _iota((S5_W, S5_G), 1)).astype(F32)
    blk = ((_iota((S5_W, S5_N), 0) >> 4) == (_iota((S5_W, S5_N), 1) >> 6)).astype(F32)
    blk_t = ((_iota((S5_N, S5_W), 0) >> 6) == (_iota((S5_N, S5_W), 1) >> 4)).astype(F32)
    wb_re, wb_im = _sel_dot(rep, bbar_re, NN, 1) * blk, _sel_dot(rep, bbar_im, NN, 1) * blk
    wc_re, wc_im = _sel_dot(cre_t, rep, NT, 0) * blk_t, _sel_dot(cim_t, rep, NT, 0) * blk_t
    return abar_re, abar_im, wb_re, wb_im, wc_re, wc_im


def f_s5_post(xr, xi, u, wc_re, wc_im, d_skip, glu_w, glu_b):
    y = _dot16(xr, wc_re) - _dot16(xi, wc_im) + d_skip * u
    act = jax.nn.gelu(y)
    return (act * jax.nn.sigmoid(_dot16(act, glu_w) + glu_b),)


def f_ssd_pre(xc, dtr, dt_bias, a_log):
    act = jax.nn.silu(xc)
    heads = _iota(dtr.shape, 1) < SSD_H
    dt = jnp.where(heads, jax.nn.softplus(dtr + dt_bias), 0.0)
    da = dt * (-jnp.exp(a_log))
    xdt = act[:, :SSD_W] * _head_expand(dt, SSD_W)
    return xdt, da, act[:, SSD_W:SSD_W + 512], act[:, SSD_W + 512:]


def f_ssd_pre_pass(xc, dtr, dt_bias, a_log):
    return f_ssd_pre(xc, dtr, dt_bias, a_log) + (xc,)


def f_ssd_post(y, xc, z, d_skip, norm_g):
    xs = jax.nn.silu(xc[:, :SSD_W])
    y = (y + xs * _head_expand(d_skip, SSD_W)) * jax.nn.silu(z)
    gw = SSD_W // SSD_NG
    parts = []
    for g in range(SSD_NG):
        seg = y[:, g * gw:(g + 1) * gw]
        parts.append(seg * lax.rsqrt(jnp.mean(seg * seg, axis=-1, keepdims=True) + EPS))
    return (jnp.concatenate(parts, axis=1) * norm_g,)


def f_rwkv_pre(f, w0, w_up, a0, a_up, g_up, k_k, k_a):
    r, k, v = f[:, 0:1024], f[:, 1024:2048], f[:, 2048:3072]
    wl, al, gl = f[:, 3072:3200], f[:, 3200:3328], f[:, 3328:3584]
    w = -jax.nn.softplus(-(w0 + _dot16(jnp.tanh(wl), w_up))) - 0.5
    decay = jnp.exp(-jnp.exp(w))
    a = jax.nn.sigmoid(a0 + _dot16(al, a_up))
    g = _dot16(jax.nn.sigmoid(gl), g_up)
    kk = k * k_k
    k2 = k * (1.0 + (a - 1.0) * k_a)
    kkn = kk * lax.rsqrt(jnp.maximum(_head_sum(kk * kk), 1e-24))
    return r, decay, k2, v, kkn, a, g


def f_rwkv_pre_pass(f, w0, w_up, a0, a_up, g_up, k_k, k_a):
    out = f_rwkv_pre(f, w0, w_up, a0, a_up, g_up, k_k, k_a)
    return out + (out[0], out[2], out[3])


def f_rwkv_post(y, r, k2, v, g, ln_g, ln_b, r_k):
    mean = _head_sum(y) * (1.0 / RW_HD)
    yc = y - mean
    var = _head_sum(yc * yc) * (1.0 / RW_HD)
    yn = yc * lax.rsqrt(var + GN_EPS) * ln_g + ln_b
    bonus = _head_sum(r * k2 * r_k) * v
    return ((yn + bonus) * g,)


def _neg_expm1(y):
    series = -y * (1.0 + y * (0.5 + y * (1.0 / 6.0 + y * (1.0 / 24.0 + y * (1.0 / 120.0)))))
    return jnp.where(y > -0.1, series, 1.0 - jnp.exp(y))


def f_lru_pre(t0, xc, w_a, b_a, w_x, b_x, lam):
    gate_r = jax.nn.sigmoid(_dot16(xc, w_a) + b_a)
    gate_i = jax.nn.sigmoid(_dot16(xc, w_x) + b_x)
    log_a = -LRU_C * gate_r * jax.nn.softplus(-lam)
    mult = jnp.sqrt(jnp.maximum(_neg_expm1(2.0 * log_a), 0.0))
    mult = jnp.where(_iota(xc.shape, 0) + t0 == 0, 1.0, mult)
    return jnp.exp(log_a), xc * gate_i * mult


def f_lru_post(h, gl):
    return (h * jax.nn.gelu(gl),)


TB = 256
TBH = 128
SCAN_TB = 256
RW_LC = 64


def _even_fwd(hn, w, tag):
    n = lambda s: f"{tag}_{s}"
    u = _mm(hn, w["in_u"], name=n("proj_u"))
    z = _mm(hn, w["in_z"], name=n("proj_z"))
    xbc = _mm(hn, w["in_xbc"], name=n("proj_xbc"))
    dtr = _mm(hn, w["in_dt"], name=n("proj_dt"))
    bu_re = _mm(u, w["wb_re"], name=n("s5_bu_re"))
    bu_im = _mm(u, w["wb_im"], name=n("s5_bu_im"))
    xr, xi = _s5_scan_fwd(w["abar_re"], w["abar_im"], bu_re, bu_im, tb=SCAN_TB, name=n("s5_scan"))
    s5c = [w["wc_re"], w["wc_im"], w["s5_d"], w["glu_w"], w["glu_b"]]
    (ya,) = _stage(f_s5_post, [xr, xi, u], s5c, tb=TB, name=n("s5_post"), out_dtypes=[BF16])
    xc = _conv_fwd(xbc, w["ssd_conv_w"], w["ssd_conv_b"], tb=TB, name=n("ssd_conv"))
    xdt, da, bm, cm = _stage(f_ssd_pre, [xc, dtr], [w["dt_bias"], w["a_log"]], tb=TB, name=n("ssd_pre"),
                             out_dtypes=[F32] * 4)
    y, ck = _ssd_fwd(xdt, da, bm, cm, name=n("ssd_scan"))
    (yb,) = _stage(f_ssd_post, [y, xc, z], [w["ssd_d"], w["ssd_norm"]], tb=TB, name=n("ssd_post"), out_dtypes=[BF16])
    mo = _mm(ya, w["out_a"], name=n("out_a"))
    mo = _mm(yb, w["out_b"], add=mo, name=n("out_b"))
    res = dict(u=u, z=z, xbc=xbc, dtr=dtr, xr=xr, xi=xi, ya=ya, xc=xc, xdt=xdt, da=da, bm=bm, cm=cm, y=y, ck=ck, yb=yb)
    return mo, res


def _even_bwd(dmo, hn, w, r, tag):
    n = lambda s: f"{tag}_{s}"
    g = {}
    g["out_a"] = _mm(r["ya"], dmo, ta=True, name=n("d_out_a"))
    g["out_b"] = _mm(r["yb"], dmo, ta=True, name=n("d_out_b"))
    dya = _mm(dmo, w["out_a"], tb=True, name=n("dya"))
    dyb = _mm(dmo, w["out_b"], tb=True, name=n("dyb"))
    dy, dxc1, dz, g["ssd_d"], g["ssd_norm"] = _stage_vjp(
        f_ssd_post, [r["y"], r["xc"], r["z"]], [w["ssd_d"], w["ssd_norm"]], [dyb], tb=TBH, name=n("ssd_post_b"),
        drow=[0, 1, 2], dconst=[0, 1])
    dxdt, dda, dbm, dcm = _ssd_bwd(r["xdt"], r["da"], r["bm"], r["cm"], r["ck"], dy, name=n("ssd_scan_b"))
    dxc, ddtr, g["dt_bias"], g["a_log"] = _stage_vjp(
        f_ssd_pre_pass, [r["xc"], r["dtr"]], [w["dt_bias"], w["a_log"]], [dxdt, dda, dbm, dcm, dxc1], tb=TBH,
        name=n("ssd_pre_b"), drow=[0, 1], dconst=[0, 1])
    dxbc, g["ssd_conv_w"], g["ssd_conv_b"] = _conv_bwd(r["xbc"], w["ssd_conv_w"], dxc, tb=TB, name=n("ssd_conv_b"))
    s5c = [w["wc_re"], w["wc_im"], w["s5_d"], w["glu_w"], w["glu_b"]]
    dxr, dxi, du1, g["wc_re"], g["wc_im"], g["s5_d"], g["glu_w"], g["glu_b"] = _stage_vjp(
        f_s5_post, [r["xr"], r["xi"], r["u"]], s5c, [dya], tb=TBH, name=n("s5_post_b"),
        drow=[0, 1, 2], dconst=[0, 1, 2, 3, 4])
    dbr, dbi, g["abar_re"], g["abar_im"] = _s5_scan_bwd(w["abar_re"], w["abar_im"], r["xr"], r["xi"], dxr, dxi,
                                                         tb=SCAN_TB, name=n("s5_scan_b"))
    g["wb_re"] = _mm(r["u"], dbr, ta=True, name=n("d_wb_re"))
    g["wb_im"] = _mm(r["u"], dbi, ta=True, name=n("d_wb_im"))
    du = _mm(dbr, w["wb_re"], tb=True, add=du1, name=n("du_re"))
    du = _mm(dbi, w["wb_im"], tb=True, add=du, name=n("du_im"))
    segs = (("in_u", du), ("in_z", dz), ("in_xbc", dxbc), ("in_dt", ddtr))
    dhn = None
    for key, dseg in segs:
        g[key] = _mm(hn, dseg, ta=True, name=n("d_" + key))
        dhn = _mm(dseg, w[key], tb=True, add=dhn, name=n("dhn_" + key))
    return dhn, g


def _odd_fwd(hn, w, tag):
    n = lambda s: f"{tag}_{s}"
    rw = _mm(hn, w["in_rw"], name=n("proj_rw"))
    xl = _mm(hn, w["in_xl"], name=n("proj_xl"))
    gl = _mm(hn, w["in_gl"], name=n("proj_gl"))
    f = _conv_fwd(rw, w["mix_w"], w["mix_b"], tb=TB, name=n("rwkv_shift"))
    rc = [w[k] for k in ("w0", "w_up", "a0", "a_up", "g_up", "k_k", "k_a")]
    r_, dec, k2, v, kkn, a, gate = _stage(f_rwkv_pre, [f], rc, tb=TB, name=n("rwkv_pre"), out_dtypes=[F32] * 7)
    t3 = lambda z: z.reshape(-1, RW_PAIRS, 128)
    y, ck, hist = _rwkv_scan_fwd(t3(r_), t3(dec), t3(k2), t3(v), t3(kkn), t3(a), lc=RW_LC, name=n("rwkv_scan"))
    y = y.reshape(-1, RW_W)
    (yc,) = _stage(f_rwkv_post, [y, r_, k2, v, gate], [w["ln_g"], w["ln_b"], w["r_k"]], tb=TB, name=n("rwkv_post"),
                   out_dtypes=[BF16])
    xc = _conv_fwd(xl, w["lru_conv_w"], w["lru_conv_b"], tb=TB, name=n("lru_conv"))
    lc = [w[k] for k in ("lru_wa", "lru_b_a", "lru_wx", "lru_b_x", "lru_lam")]
    a_l, bx = _stage(f_lru_pre, [xc], lc, tb=TB, name=n("lru_pre"), out_dtypes=[F32] * 2, pos=True)
    h = _lru_scan_fwd(a_l, bx, tb=SCAN_TB, name=n("lru_scan"))
    (yd,) = _stage(f_lru_post, [h, gl], [], tb=TB, name=n("lru_post"), out_dtypes=[BF16])
    mo = _mm(yc, w["out_a"], name=n("out_a"))
    mo = _mm(yd, w["out_b"], add=mo, name=n("out_b"))
    res = dict(rw=rw, xl=xl, gl=gl, f=f, r=r_, dec=dec, k2=k2, v=v, kkn=kkn, a=a, gate=gate, y=y, ck=ck, hist=hist, yc=yc,
               xc=xc, a_l=a_l, h=h, yd=yd)
    return mo, res


def _odd_bwd(dmo, hn, w, r, tag):
    n = lambda s: f"{tag}_{s}"
    g = {}
    g["out_a"] = _mm(r["yc"], dmo, ta=True, name=n("d_out_a"))
    g["out_b"] = _mm(r["yd"], dmo, ta=True, name=n("d_out_b"))
    dyc = _mm(dmo, w["out_a"], tb=True, name=n("dyc"))
    dyd = _mm(dmo, w["out_b"], tb=True, name=n("dyd"))
    dh, dgl = _stage_vjp(f_lru_post, [r["h"], r["gl"]], [], [dyd], tb=TB, name=n("lru_post_b"), drow=[0, 1], dconst=[])
    da_l, dbx = _lru_scan_bwd(r["a_l"], r["h"], dh, tb=SCAN_TB, name=n("lru_scan_b"))
    lc = [w[k] for k in ("lru_wa", "lru_b_a", "lru_wx", "lru_b_x", "lru_lam")]
    dxc, g["lru_wa"], g["lru_b_a"], g["lru_wx"], g["lru_b_x"], g["lru_lam"] = _stage_vjp(
        f_lru_pre, [r["xc"]], lc, [da_l, dbx], tb=TBH, name=n("lru_pre_b"), drow=[0], dconst=[0, 1, 2, 3, 4], pos=True)
    dxl, g["lru_conv_w"], g["lru_conv_b"] = _conv_bwd(r["xl"], w["lru_conv_w"], dxc, tb=TB, name=n("lru_conv_b"))
    dy, dr1, dk1, dv1, dgate, g["ln_g"], g["ln_b"], g["r_k"] = _stage_vjp(
        f_rwkv_post, [r["y"], r["r"], r["k2"], r["v"], r["gate"]], [w["ln_g"], w["ln_b"], w["r_k"]], [dyc], tb=TBH,
        name=n("rwkv_post_b"), drow=[0, 1, 2, 3, 4], dconst=[0, 1, 2])
    t3 = lambda z: z.reshape(-1, RW_PAIRS, 128)
    dr2, ddec, dk2, dv2, dkkn, da = [z.reshape(-1, RW_W) for z in _rwkv_scan_bwd(
        t3(r["r"]), t3(r["dec"]), t3(r["k2"]), t3(r["v"]), t3(r["kkn"]), t3(r["a"]), r["ck"], r["hist"], t3(dy),
        lc=RW_LC, name=n("rwkv_scan_b"))]
    rc = [w[k] for k in ("w0", "w_up", "a0", "a_up", "g_up", "k_k", "k_a")]
    df, g["w0"], g["w_up"], g["a0"], g["a_up"], g["g_up"], g["k_k"], g["k_a"] = _stage_vjp(
        f_rwkv_pre_pass, [r["f"]], rc, [dr2, ddec, dk2, dv2, dkkn, da, dgate, dr1, dk1, dv1], tb=TBH,
        name=n("rwkv_pre_b"), drow=[0], dconst=[0, 1, 2, 3, 4, 5, 6])
    drw, g["mix_w"], _ = _conv_bwd(r["rw"], w["mix_w"], df, tb=TB, name=n("rwkv_shift_b"))
    segs = (("in_rw", drw), ("in_xl", dxl), ("in_gl", dgl))
    dhn = None
    for key, dseg in segs:
        g[key] = _mm(hn, dseg, ta=True, name=n("d_" + key))
        dhn = _mm(dseg, w[key], tb=True, add=dhn, name=n("dhn_" + key))
    return dhn, g


def _layer_fwd(h, p_i, w, odd, tag):
    n = lambda s: f"{tag}_{s}"
    (hn,) = _stage(f_norm, [h], [w["norm_mix"]], tb=TB, name=n("norm_mix"), out_dtypes=[BF16])
    mo, mres = (_odd_fwd if odd else _even_fwd)(hn, w, tag)
    h1, hf = _stage(f_add_norm, [h, mo], [w["norm_ffn"]], tb=TB, name=n("norm_ffn"), out_dtypes=[F32, BF16])
    u, act = _mm(hf, w["mlp_w1"], name=n("mlp_up"), epilogue=lambda acc: (acc,) + f_relu2(acc), out_dtypes=[F32, BF16])
    m2 = _mm(act, w["mlp_w2"], name=n("mlp_down"))
    h2, hp = _stage(f_add_norm, [h1, m2], [w["norm_pl"]], tb=TB, name=n("norm_pl"), out_dtypes=[F32, BF16])
    gl = _mm(hp, w["pl_gate"], name=n("pl_gate"))
    pp = _mm(p_i, w["pl_proj"], name=n("pl_proj"))
    (h3,) = _stage(f_plgate, [h2, gl, pp], [], tb=TB, name=n("pl_mix"), out_dtypes=[F32])
    res = dict(h=h, hn=hn, mo=mo, mix=mres, h1=h1, hf=hf, u=u, act=act, m2=m2, h2=h2, hp=hp, gl=gl, pp=pp)
    return h3, res


def _layer_bwd(dh3, p_i, w, r, odd, tag, stacks):
    n = lambda s: f"{tag}_{s}"
    g = {}
    wgrad = lambda key, x, dy, cols_cut, shard: _mm_grad(x, dy, layer=int(odd), cols_cut=cols_cut, shard=shard,
                                                        prev=stacks[key] if stacks else None, name=n("d_" + key))
    dh2, dgl, dpp = _stage_vjp(f_plgate, [r["h2"], r["gl"], r["pp"]], [], [dh3], tb=TB, name=n("pl_mix_b"),
                               drow=[0, 1, 2], dconst=[])
    g["pl_proj"] = wgrad("pl_proj", p_i, dpp, True, (PL_DIM, D // 4))
    g["pl_gate"] = wgrad("pl_gate", r["hp"], dgl, False, (D // 4, D))
    dhp = _mm(dgl, w["pl_gate"], tb=True, name=n("dhp"))
    dh1, dm2, g["norm_pl"] = _stage_vjp(f_add_norm, [r["h1"], r["m2"]], [w["norm_pl"]], [dh2, dhp], tb=TB,
                                        name=n("norm_pl_b"), drow=[0, 1], dconst=[0])
    g["mlp_w2"] = wgrad("mlp_w2", r["act"], dm2, False, (D_FF // 4, D))
    (du,) = _mm(dm2, w["mlp_w2"], tb=True, name=n("dact"), extra=[r["u"]], out_dtypes=[BF16],
                epilogue=lambda acc, u: (acc * (2.0 * jnp.maximum(u, 0.0)),))
    g["mlp_w1"] = wgrad("mlp_w1", r["hf"], du, True, (D, D_FF // 4))
    dhf = _mm(du, w["mlp_w1"], tb=True, name=n("dhf"))
    dh, dmo, g["norm_ffn"] = _stage_vjp(f_add_norm, [r["h"], r["mo"]], [w["norm_ffn"]], [dh1, dhf], tb=TB,
                                        name=n("norm_ffn_b"), drow=[0, 1], dconst=[0])
    dhn, gm = (_odd_bwd if odd else _even_bwd)(dmo, r["hn"], w, r["mix"], tag)
    g.update(gm)
    dh0, g["norm_mix"] = _stage_vjp(f_norm_pass, [r["h"]], [w["norm_mix"]], [dhn, dh], tb=TB, name=n("norm_mix_b"),
                                    drow=[0], dconst=[0])
    return dh0, g


def _pad_to(a, size, axis):
    pad = [(0, 0)] * a.ndim
    pad[axis] = (0, size - a.shape[axis])
    return jnp.pad(a, pad)


def _rw_pad(a):
    return jnp.concatenate([a[..., :3072], _pad_to(a[..., 3072:3168], 128, -1), _pad_to(a[..., 3168:3264], 128, -1),
                            a[..., 3264:3520]], axis=-1)


def _rw_unpad(a):
    return jnp.concatenate([a[..., :3072], a[..., 3072:3168], a[..., 3200:3296], a[..., 3328:3584]], axis=-1)


def _block_diag(w):
    nb, bs, _ = w.shape
    eye = jnp.eye(nb, dtype=w.dtype)
    return (w[:, :, None, :] * eye[:, None, :, None]).reshape(nb * bs, nb * bs)


def _diag_blocks(w):
    nb = LRU_B
    bs = w.shape[0] // nb
    return jnp.stack([w[h * bs:(h + 1) * bs, h * bs:(h + 1) * bs] for h in range(nb)])


def _s5_prep_inputs(fw):
    lstep = jnp.broadcast_to(_pad_to(fw["s5_log_step"].astype(F32), 128, 1), (8, 128))
    t16 = lambda b: jnp.transpose(b[0], (2, 0, 1)).reshape(S5_G, S5_N)
    tc = lambda c: jnp.transpose(c[0], (0, 2, 1)).reshape(S5_N, S5_G)
    return [fw["s5_lam_re"].reshape(1, S5_N), fw["s5_lam_im"].reshape(1, S5_N), lstep,
            t16(fw["s5_b_re"]), t16(fw["s5_b_im"]), tc(fw["s5_c_re"]), tc(fw["s5_c_im"])]


def _layer_weights(fw, i):
    w = {k: fw[k][i:i + 1] for k in ("norm_mix", "norm_ffn", "norm_pl")}
    for k in ("mlp_w1", "mlp_w2", "pl_proj", "pl_gate"):
        w[k] = (fw[k], i)
    return w


def _even_weights(fw, prep):
    w = _layer_weights(fw, 0)
    ein, eout = fw["e_in_proj"][0], fw["e_out_proj"][0]
    w.update(in_u=ein[:, :512], in_z=ein[:, 512:2048], in_xbc=ein[:, 2048:4608], in_dt=_pad_to(ein[:, 4608:], 128, 1),
             out_a=eout[:512], out_b=eout[512:])
    abar_re, abar_im, wb_re, wb_im, wc_re, wc_im = prep
    w.update(abar_re=abar_re, abar_im=abar_im, wb_re=wb_re, wb_im=wb_im, wc_re=wc_re.astype(BF16), wc_im=wc_im.astype(BF16),
             s5_d=fw["s5_d"], glu_w=fw["s5_glu_w"][0], glu_b=fw["s5_glu_b"],
             ssd_conv_w=_pad_to(fw["ssd_conv_w"][0], 8, 0), ssd_conv_b=fw["ssd_conv_b"],
             dt_bias=_pad_to(fw["ssd_dt_bias"], 128, 1), a_log=_pad_to(fw["ssd_a_log"], 128, 1),
             ssd_d=_pad_to(fw["ssd_d"], 128, 1), ssd_norm=fw["ssd_norm"])
    return w


def _odd_weights(fw):
    w = _layer_weights(fw, 1)
    oin, oout = fw["o_in_proj"][0], fw["o_out_proj"][0]
    mu = _rw_pad(fw["rwkv_mu"])
    zero = jnp.zeros_like(mu)
    w.update(in_rw=_rw_pad(oin[:, :RW_IN]), in_xl=oin[:, RW_IN:RW_IN + LRU_W], in_gl=oin[:, RW_IN + LRU_W:],
             out_a=oout[:RW_W], out_b=oout[RW_W:],
             mix_w=jnp.concatenate([zero, zero, mu, 1.0 - mu, zero, zero, zero, zero], axis=0), mix_b=zero,
             w0=fw["rwkv_w0"], w_up=_pad_to(fw["rwkv_w_up"][0], 128, 0), a0=fw["rwkv_a0"],
             a_up=_pad_to(fw["rwkv_a_up"][0], 128, 0), g_up=fw["rwkv_g_up"][0], k_k=fw["rwkv_k_k"], k_a=fw["rwkv_k_a"],
             r_k=fw["rwkv_r_k"].reshape(1, RW_W), ln_g=fw["rwkv_ln_g"], ln_b=fw["rwkv_ln_b"],
             lru_conv_w=_pad_to(fw["lru_conv_w"][0], 8, 0), lru_conv_b=fw["lru_conv_b"],
             lru_wa=_block_diag(fw["lru_w_a"][0]).astype(BF16), lru_b_a=fw["lru_b_a"].reshape(1, LRU_W),
             lru_wx=_block_diag(fw["lru_w_x"][0]).astype(BF16), lru_b_x=fw["lru_b_x"].reshape(1, LRU_W),
             lru_lam=fw["lru_lam"].reshape(1, LRU_W))
    return w


def _global_grads(g0, g1, s5_grads, d_norm_final):
    out = {k: jnp.concatenate([g0[k], g1[k]], axis=0) for k in ("norm_mix", "norm_ffn", "norm_pl")}
    for k in STACKED:
        out[k] = g0[k]
    out["e_in_proj"] = jnp.concatenate([g0["in_u"], g0["in_z"], g0["in_xbc"], g0["in_dt"][:, :SSD_H]], axis=1)[None]
    out["e_out_proj"] = jnp.concatenate([g0["out_a"], g0["out_b"]], axis=0)[None]
    d_lam_re, d_lam_im, d_lstep, d_bre, d_bim, d_cre, d_cim = s5_grads
    out["s5_lam_re"] = d_lam_re.reshape(1, S5_GROUPS, S5_P)
    out["s5_lam_im"] = d_lam_im.reshape(1, S5_GROUPS, S5_P)
    out["s5_log_step"] = d_lstep[0:1, :S5_GROUPS]
    unb = lambda b: jnp.transpose(b.reshape(S5_G, S5_GROUPS, S5_P), (1, 2, 0))[None]
    unc = lambda c: jnp.transpose(c.reshape(S5_GROUPS, S5_P, S5_G), (0, 2, 1))[None]
    out.update(s5_b_re=unb(d_bre), s5_b_im=unb(d_bim), s5_c_re=unc(d_cre), s5_c_im=unc(d_cim),
               s5_d=g0["s5_d"], s5_glu_w=g0["glu_w"][None], s5_glu_b=g0["glu_b"],
               ssd_conv_w=g0["ssd_conv_w"][None, :4], ssd_conv_b=g0["ssd_conv_b"], ssd_dt_bias=g0["dt_bias"][:, :SSD_H],
               ssd_a_log=g0["a_log"][:, :SSD_H], ssd_d=g0["ssd_d"][:, :SSD_H], ssd_norm=g0["ssd_norm"])
    out["o_in_proj"] = jnp.concatenate([_rw_unpad(g1["in_rw"]), g1["in_xl"], g1["in_gl"]], axis=1)[None]
    out["o_out_proj"] = jnp.concatenate([g1["out_a"], g1["out_b"]], axis=0)[None]
    out.update(rwkv_mu=_rw_unpad(g1["mix_w"][2:3] - g1["mix_w"][3:4]), rwkv_w0=g1["w0"], rwkv_w_up=g1["w_up"][None, :RW_LORA],
               rwkv_a0=g1["a0"], rwkv_a_up=g1["a_up"][None, :RW_LORA], rwkv_g_up=g1["g_up"][None], rwkv_k_k=g1["k_k"],
               rwkv_k_a=g1["k_a"], rwkv_r_k=g1["r_k"].reshape(1, RW_H, RW_HD), rwkv_ln_g=g1["ln_g"], rwkv_ln_b=g1["ln_b"],
               lru_conv_w=g1["lru_conv_w"][None, :4], lru_conv_b=g1["lru_conv_b"],
               lru_w_a=_diag_blocks(g1["lru_wa"])[None], lru_b_a=g1["lru_b_a"].reshape(1, LRU_B, 64),
               lru_w_x=_diag_blocks(g1["lru_wx"])[None], lru_b_x=g1["lru_b_x"].reshape(1, LRU_B, 64),
               lru_lam=g1["lru_lam"].reshape(1, LRU_B, 64), norm_final=d_norm_final.reshape(D))
    return out


def _local_step(x, p, target, fw):
    prep_in = _s5_prep_inputs(fw)
    prep = _single(f_s5_prep, prep_in, name="s5_prep")
    w0, w1 = _even_weights(fw, prep), _odd_weights(fw)
    h1, r0 = _layer_fwd(x, p[0], w0, False, "l0")
    h2, r1 = _layer_fwd(h1, p[1], w1, True, "l1")
    gf = fw["norm_final"].reshape(1, D)
    (loss8,) = _stage(f_loss, [h2, target], [gf], tb=TB, name="loss", out_dtypes=[], n_acc=1)
    one = jnp.zeros((8, 128), F32).at[0, 0].set(1.0)
    dh2, d_gf = _stage_vjp(f_loss, [h2, target], [gf], [], tb=TB, name="loss_b", drow=[0], dconst=[0], acc_cots=[one])
    dh1, g1 = _layer_bwd(dh2, p[1], w1, r1, True, "l1", None)
    dx, g0 = _layer_bwd(dh1, p[0], w0, r0, False, "l0", g1)
    cots = [g0[k] for k in ("abar_re", "abar_im", "wb_re", "wb_im", "wc_re", "wc_im")]
    s5_grads = _single_vjp(f_s5_prep, prep_in, cots, name="s5_prep_b")
    return loss8[0, 0], dx, _global_grads(g0, g1, s5_grads, d_gf)


def _xyc():
    return lax.axis_index("x"), lax.axis_index("y"), lax.axis_index("c")


def _flip(v, bit):
    return 1 - v if bit else v


def _remote(src, dst, send_sems, recv_sems, k, dev):
    return pltpu.make_async_remote_copy(src_ref=src, dst_ref=dst, send_sem=send_sems.at[k], recv_sem=recv_sems.at[k],
                                        device_id=dev, device_id_type=MESH)


CHIP_FLIPS = ((1, 0), (0, 1), (1, 1))


def _gather_chips(arrs, out_shapes, places, *, name):
    n = len(arrs)

    def body(*refs):
        ins, outs = refs[:n], refs[n:2 * n]
        send_sems, recv_sems = refs[2 * n:]
        x, y, c = _xyc()
        chip, sib = 2 * x + y, (x, y, 1 - c)
        peers = [(_flip(x, fx), _flip(y, fy)) for fx, fy in CHIP_FLIPS]
        first = [_remote(ins[a].at[c], places[a](outs[a], chip, c), send_sems, recv_sems, 6 * a + j, (px, py, c))
                 for a in range(n) for j, (px, py) in enumerate(peers)]
        for cp in first:
            cp.start()
        passed = []
        for a in range(n):
            for j, (px, py) in enumerate(peers):
                landed = places[a](outs[a], 2 * px + py, c)
                _remote(ins[a].at[c], landed, send_sems, recv_sems, 6 * a + j, (px, py, c)).wait_recv()
                cp = _remote(landed, landed, send_sems, recv_sems, 6 * a + 3 + j, sib)
                cp.start()
                passed.append(cp)
        for a in range(n):
            for j, (px, py) in enumerate(peers):
                other = places[a](outs[a], 2 * px + py, 1 - c)
                _remote(other, other, send_sems, recv_sems, 6 * a + 3 + j, sib).wait_recv()
        for cp in first + passed:
            cp.wait_send()

    return pl.pallas_call(
        body, out_shape=[SDS(s, a.dtype) for s, a in zip(out_shapes, arrs)], in_specs=[ANY] * n, out_specs=[ANY] * n,
        scratch_shapes=_dma_sems(6 * n), name=name,
    )(*arrs)


def _place_own(full, own, chip_vec, axis, *, name):
    layers, rows, cols = own.shape
    tb = min(rows, 512)
    per = rows // tb
    omap = ((lambda l, i, chip_ref: (l, chip_ref[0] * per + i, 0)) if axis == 1
            else (lambda l, i, chip_ref: (l, i, chip_ref[0])))

    def body(chip_ref, own_ref, full_ref, o_ref):
        o_ref[...] = own_ref[...]

    return pl.pallas_call(
        body,
        grid_spec=pltpu.PrefetchScalarGridSpec(
            num_scalar_prefetch=1, grid=(layers, per),
            in_specs=[pl.BlockSpec((None, tb, cols), lambda l, i, chip_ref: (l, i, 0)), ANY],
            out_specs=pl.BlockSpec((None, tb, cols), omap)),
        out_shape=SDS(full.shape, full.dtype), input_output_aliases={2: 0},
        compiler_params=_cparams(("arbitrary", "arbitrary")), name=name,
    )(chip_vec, own, full)


def _dma_sems(n):
    return [pltpu.SemaphoreType.DMA((n,)), pltpu.SemaphoreType.DMA((n,))]


def _send_halves(arrs, *, name):
    n = len(arrs)

    def body(*refs):
        ins, outs = refs[:n], refs[n:2 * n]
        send_sems, recv_sems = refs[2 * n:]
        x, y, c = _xyc()
        copies = [_remote(ins[a].at[k, 1 - c], outs[a].at[k], send_sems, recv_sems, 4 * a + k, (x, y, 1 - c))
                  for a in range(n) for k in range(arrs[a].shape[0])]
        for cp in copies:
            cp.start()
        for cp in copies:
            cp.wait_recv()
        for cp in copies:
            cp.wait_send()

    return pl.pallas_call(
        body, out_shape=[SDS(a.shape[:1] + a.shape[2:], a.dtype) for a in arrs], in_specs=[ANY] * n, out_specs=[ANY] * n,
        scratch_shapes=_dma_sems(4 * n), name=name,
    )(*arrs)


def _add_half(g, recv, c_vec, *, tb, out_dtype, name):
    slots, _, rh, cols = g.shape
    tb = min(tb, rh)

    def body(c_ref, g_ref, r_ref, o_ref):
        o_ref[...] = (g_ref[...] + r_ref[...]).astype(o_ref.dtype)

    return pl.pallas_call(
        body,
        grid_spec=pltpu.PrefetchScalarGridSpec(
            num_scalar_prefetch=1, grid=(slots, rh // tb),
            in_specs=[pl.BlockSpec((None, None, tb, cols), lambda k, i, c_ref: (k, c_ref[0], i, 0)),
                      pl.BlockSpec((None, tb, cols), lambda k, i, c_ref: (k, i, 0))],
            out_specs=pl.BlockSpec((None, tb, cols), lambda k, i, c_ref: (k, i, 0))),
        out_shape=SDS((slots, rh, cols), out_dtype), compiler_params=_cparams(("arbitrary", "arbitrary")), name=name,
    )(c_vec, g, recv)


def _scatter_chips(arrs, *, name):
    n = len(arrs)

    def body(*refs):
        ins, outs = refs[:n], refs[n:2 * n]
        send_sems, recv_sems = refs[2 * n:]
        x, y, c = _xyc()
        copies = []
        for a in range(n):
            for j, (fx, fy) in enumerate(CHIP_FLIPS):
                px, py = _flip(x, fx), _flip(y, fy)
                mine = ins[a].at[2 * px + py if arrs[a].shape[0] == 4 else 0]
                copies.append(_remote(mine, outs[a].at[j], send_sems, recv_sems, 3 * a + j, (px, py, c)))
        for cp in copies:
            cp.start()
        for cp in copies:
            cp.wait_recv()
        for cp in copies:
            cp.wait_send()

    return pl.pallas_call(
        body, out_shape=[SDS((3,) + a.shape[1:], a.dtype) for a in arrs], in_specs=[ANY] * n, out_specs=[ANY] * n,
        scratch_shapes=_dma_sems(3 * n), name=name,
    )(*arrs)


def _sum_chips(p, landed, chip_vec, *, tb, name):
    _, rh, cols = p.shape
    tb = min(tb, rh)

    def body(chip_ref, p_ref, l_ref, o_ref):
        f = lambda z: z.astype(F32)
        o_ref[...] = ((f(p_ref[...]) + f(l_ref[0])) + f(l_ref[1])) + f(l_ref[2])

    return pl.pallas_call(
        body,
        grid_spec=pltpu.PrefetchScalarGridSpec(
            num_scalar_prefetch=1, grid=(rh // tb,),
            in_specs=[pl.BlockSpec((None, tb, cols), lambda i, chip_ref: (chip_ref[0], i, 0)),
                      pl.BlockSpec((3, tb, cols), lambda i, chip_ref: (0, i, 0))],
            out_specs=pl.BlockSpec((tb, cols), lambda i, chip_ref: (i, 0))),
        out_shape=SDS((rh, cols), F32), compiler_params=_cparams(), name=name,
    )(chip_vec, p, landed)


def _sum_chips_ordered(p, landed, chip_vec, *, tb, name):
    _, rh, cols = p.shape
    tb = min(tb, rh)

    def body(chip_ref, p_ref, l_ref, o_ref):
        chip = chip_ref[0]
        acc = None
        for k in range(4):
            away = k ^ chip
            slot = jnp.where(away == 2, 0, jnp.where(away == 1, 1, 2))
            term = jnp.where(k == chip, p_ref[...], l_ref[slot])
            acc = term if acc is None else acc + term
        o_ref[...] = acc

    return pl.pallas_call(
        body,
        grid_spec=pltpu.PrefetchScalarGridSpec(
            num_scalar_prefetch=1, grid=(rh // tb,),
            in_specs=[pl.BlockSpec((None, tb, cols), lambda i, chip_ref: (0, i, 0)),
                      pl.BlockSpec((3, tb, cols), lambda i, chip_ref: (0, i, 0))],
            out_specs=pl.BlockSpec((tb, cols), lambda i, chip_ref: (i, 0))),
        out_shape=SDS((rh, cols), F32), compiler_params=_cparams(), name=name,
    )(chip_vec, p, landed)


def _swap_halves(arrs, *, name):
    n = len(arrs)

    def body(*refs):
        ins, outs = refs[:n], refs[n:2 * n]
        send_sems, recv_sems = refs[2 * n:]
        x, y, c = _xyc()
        copies = [_remote(ins[a], outs[a], send_sems, recv_sems, a, (x, y, 1 - c)) for a in range(n)]
        for cp in copies:
            cp.start()
        for cp in copies:
            cp.wait_recv()
        for cp in copies:
            cp.wait_send()

    return pl.pallas_call(
        body, out_shape=[SDS(a.shape, a.dtype) for a in arrs], in_specs=[ANY] * n, out_specs=[ANY] * n,
        scratch_shapes=_dma_sems(n), name=name,
    )(*arrs)


def _join_halves(mine, theirs, c_vec, *, tb, name):
    rh, cols = mine.shape
    tb = min(tb, rh)

    def body(c_ref, m_ref, t_ref, o_ref):
        o_ref[...] = jnp.where(pl.program_id(0) == c_ref[0], m_ref[...], t_ref[...])

    blk = pl.BlockSpec((tb, cols), lambda h, i, c_ref: (i, 0))
    return pl.pallas_call(
        body,
        grid_spec=pltpu.PrefetchScalarGridSpec(
            num_scalar_prefetch=1, grid=(2, rh // tb), in_specs=[blk, blk],
            out_specs=pl.BlockSpec((None, tb, cols), lambda h, i, c_ref: (h, i, 0))),
        out_shape=SDS((2, rh, cols), mine.dtype), compiler_params=_cparams(("arbitrary", "arbitrary")), name=name,
    )(c_vec, mine, theirs)


def f_adamw(w, g, m, v):
    m = ADAM_B1 * m + (1.0 - ADAM_B1) * g
    v = ADAM_B2 * v + (1.0 - ADAM_B2) * (g * g)
    m_hat = m / (1.0 - ADAM_B1 ** ADAM_STEP)
    v_hat = v / (1.0 - ADAM_B2 ** ADAM_STEP)
    return -ADAM_LR * (m_hat / (jnp.sqrt(v_hat) + ADAM_EPS) + ADAM_WD * w), m, v


def _adamw(w, g, m, v, *, name):
    shape = w.shape
    two = lambda a: a.reshape(-1, shape[-1])
    rows = two(w).shape[0]
    tb = 256 if rows % 256 == 0 else rows
    outs = _stage(f_adamw, [two(w), two(g), two(m), two(v)], [], tb=tb, name=name, out_dtypes=[F32] * 3)
    return [o.reshape(shape) for o in outs]


def _pack(arrs, rows=8):
    flat = jnp.concatenate([a.astype(F32).reshape(-1) for a in arrs])
    size = -(-flat.shape[0] // (rows * 128)) * (rows * 128)
    return _pad_to(flat, size, 0).reshape(-1, 128)


def _unpack(buf, shapes):
    flat = buf.reshape(-1)
    out, off = [], 0
    for s in shapes:
        n = math.prod(s)
        out.append(flat[off:off + n].reshape(s))
        off += n
    return out


WEIGHTS = ("norm_mix", "norm_ffn", "norm_pl", "mlp_w1", "mlp_w2", "pl_proj", "pl_gate", "e_in_proj", "e_out_proj",
           "s5_lam_re", "s5_lam_im", "s5_log_step", "s5_b_re", "s5_b_im", "s5_c_re", "s5_c_im", "s5_d", "s5_glu_w",
           "s5_glu_b", "ssd_conv_w", "ssd_conv_b", "ssd_dt_bias", "ssd_a_log", "ssd_d", "ssd_norm", "o_in_proj",
           "o_out_proj", "rwkv_mu", "rwkv_w0", "rwkv_w_up", "rwkv_a0", "rwkv_a_up", "rwkv_g_up", "rwkv_k_k", "rwkv_k_a",
           "rwkv_r_k", "rwkv_ln_g", "rwkv_ln_b", "lru_conv_w", "lru_conv_b", "lru_w_a", "lru_b_a", "lru_w_x", "lru_b_x",
           "lru_lam", "norm_final")
BIG = ("mlp_w1", "mlp_w2", "pl_proj", "pl_gate", "e_in_proj", "e_out_proj", "o_in_proj", "o_out_proj")
STACKED = BIG[:4]
SHARD_AXIS = {"mlp_w1": 2, "mlp_w2": 1, "pl_proj": 2, "pl_gate": 1, "e_in_proj": 2, "e_out_proj": 1, "s5_glu_w": 1,
              "ssd_conv_w": 2, "o_in_proj": 2, "o_out_proj": 1, "rwkv_mu": 1, "rwkv_w0": 1, "rwkv_w_up": 2, "rwkv_a0": 1,
              "rwkv_a_up": 2, "rwkv_g_up": 2, "rwkv_k_k": 1, "rwkv_k_a": 1, "rwkv_ln_g": 1, "rwkv_ln_b": 1,
              "lru_conv_w": 2, "lru_conv_b": 1}
SMALL = tuple(n for n in WEIGHTS if n not in BIG)
SMALL_SHARDED = tuple(n for n in SMALL if n in SHARD_AXIS)


def _gather_weights(w):
    shapes = [w[n].shape for n in SMALL_SHARDED]
    chip = 2 * lax.axis_index("x") + lax.axis_index("y")
    mine = [w[n].astype(BF16) for n in BIG] + [_pack([w[n] for n in SMALL_SHARDED], rows=16)]
    out_shapes, places = [], []
    for n, a in zip(BIG + ("small",), mine):
        layers, rows, cols = a.shape if a.ndim == 3 else (1,) + a.shape
        ax = SHARD_AXIS.get(n)
        if ax == 1:
            step = rows if layers == 2 else rows // 2
            out_shapes.append((layers, 4 * rows, cols))
            places.append(lambda o, k, h, layers=layers, rows=rows, step=step: o.at[
                h if layers == 2 else 0, pl.ds(pl.multiple_of(k * rows + (0 if layers == 2 else h * step), 16), step), :])
        elif ax == 2 and layers == 2:
            out_shapes.append((layers, rows, 4 * cols))
            places.append(lambda o, k, h, cols=cols: o.at[h, :, pl.ds(pl.multiple_of(k * cols, 128), cols)])
        else:
            out_shapes.append((4, 2, layers * rows // 2, cols))
            places.append(lambda o, k, h: o.at[k, h])
    got = _gather_chips([a.reshape(2, -1, a.shape[-1]) for a in mine], out_shapes, places, name="gather_weights")
    fw = {n: w[n] for n in SMALL if n not in SHARD_AXIS}
    for n, g, a in zip(BIG, got[:-1], mine):
        if g.shape[0] == 4:
            g = lax.dynamic_update_index_in_dim(g.reshape((4,) + a.shape), a, chip, 0)
            fw[n] = jnp.concatenate([g[k] for k in range(4)], axis=SHARD_AXIS[n])
        else:
            fw[n] = _place_own(g, a, chip.astype(jnp.int32).reshape(1), SHARD_AXIS[n], name=f"place_{n}")
    small = lax.dynamic_update_index_in_dim(got[-1].reshape((4,) + mine[-1].shape), mine[-1], chip, 0)
    parts = [_unpack(small[k], shapes) for k in range(4)]
    for i, n in enumerate(SMALL_SHARDED):
        fw[n] = jnp.concatenate([parts[k][i] for k in range(4)], axis=SHARD_AXIS[n])
    return fw


def _reduce(grads, w, chip, loss):
    stacks = []
    for n in BIG:
        cols = w[n].shape[-1]
        stacks.append(grads[n] if n in STACKED else
                      jnp.stack(jnp.split(grads[n], 4, axis=SHARD_AXIS[n])).reshape(4, 2, -1, cols))
    shapes = [grads[n].shape for n in SMALL] + [(1,)]
    small = _pack([grads[n] for n in SMALL] + [loss.reshape(1)], rows=1024).reshape(1, 2, -1, 128)
    c_vec = lax.axis_index("c").astype(jnp.int32).reshape(1)
    chip_vec = chip.astype(jnp.int32).reshape(1)
    got = _send_halves(stacks + [small], name="reduce_pair")
    sums = [_add_half(s, r, c_vec, tb=512, out_dtype=BF16, name=f"reduce_pair_sum_{n}")
            for n, s, r in zip(BIG, stacks, got)]
    sums.append(_add_half(small, got[-1], c_vec, tb=512, out_dtype=F32, name="reduce_pair_sum_small"))
    landed = _scatter_chips(sums, name="reduce_chips")
    halves = [_sum_chips(p, l, chip_vec, tb=256, name=f"reduce_chips_sum_{n}") for n, p, l in zip(BIG, sums, landed)]
    halves.append(_sum_chips_ordered(sums[-1], landed[-1], chip_vec, tb=512, name="reduce_chips_sum_small"))
    theirs = _swap_halves(halves, name="reduce_swap")
    names = BIG + ("small",)
    whole = [_join_halves(h, t, c_vec, tb=512, name=f"reduce_join_{n}") for n, h, t in zip(names, halves, theirs)]
    out = {n: j.reshape(w[n].shape) for n, j in zip(BIG, whole)}
    *parts, loss_sum = _unpack(whole[-1], shapes)
    for n, g in zip(SMALL, parts):
        if n in SHARD_AXIS:
            ax = SHARD_AXIS[n]
            size = w[n].shape[ax]
            g = lax.dynamic_slice_in_dim(g, chip * size, size, axis=ax)
        out[n] = g
    return out, loss_sum[0]


def kernel(x, p, norm_mix, norm_ffn, norm_pl, mlp_w1, mlp_w2, pl_proj, pl_gate, e_in_proj, e_out_proj, s5_lam_re, s5_lam_im, s5_log_step, s5_b_re, s5_b_im, s5_c_re, s5_c_im, s5_d, s5_glu_w, s5_glu_b, ssd_conv_w, ssd_conv_b, ssd_dt_bias, ssd_a_log, ssd_d, ssd_norm, o_in_proj, o_out_proj, rwkv_mu, rwkv_w0, rwkv_w_up, rwkv_a0, rwkv_a_up, rwkv_g_up, rwkv_k_k, rwkv_k_a, rwkv_r_k, rwkv_ln_g, rwkv_ln_b, lru_conv_w, lru_conv_b, lru_w_a, lru_b_a, lru_w_x, lru_b_x, lru_lam, norm_final, loss_target, m_norm_mix, m_norm_ffn, m_norm_pl, m_mlp_w1, m_mlp_w2, m_pl_proj, m_pl_gate, m_e_in_proj, m_e_out_proj, m_s5_lam_re, m_s5_lam_im, m_s5_log_step, m_s5_b_re, m_s5_b_im, m_s5_c_re, m_s5_c_im, m_s5_d, m_s5_glu_w, m_s5_glu_b, m_ssd_conv_w, m_ssd_conv_b, m_ssd_dt_bias, m_ssd_a_log, m_ssd_d, m_ssd_norm, m_o_in_proj, m_o_out_proj, m_rwkv_mu, m_rwkv_w0, m_rwkv_w_up, m_rwkv_a0, m_rwkv_a_up, m_rwkv_g_up, m_rwkv_k_k, m_rwkv_k_a, m_rwkv_r_k, m_rwkv_ln_g, m_rwkv_ln_b, m_lru_conv_w, m_lru_conv_b, m_lru_w_a, m_lru_b_a, m_lru_w_x, m_lru_b_x, m_lru_lam, m_norm_final, v_norm_mix, v_norm_ffn, v_norm_pl, v_mlp_w1, v_mlp_w2, v_pl_proj, v_pl_gate, v_e_in_proj, v_e_out_proj, v_s5_lam_re, v_s5_lam_im, v_s5_log_step, v_s5_b_re, v_s5_b_im, v_s5_c_re, v_s5_c_im, v_s5_d, v_s5_glu_w, v_s5_glu_b, v_ssd_conv_w, v_ssd_conv_b, v_ssd_dt_bias, v_ssd_a_log, v_ssd_d, v_ssd_norm, v_o_in_proj, v_o_out_proj, v_rwkv_mu, v_rwkv_w0, v_rwkv_w_up, v_rwkv_a0, v_rwkv_a_up, v_rwkv_g_up, v_rwkv_k_k, v_rwkv_k_a, v_rwkv_r_k, v_rwkv_ln_g, v_rwkv_ln_b, v_lru_conv_w, v_lru_conv_b, v_lru_w_a, v_lru_b_a, v_lru_w_x, v_lru_b_x, v_lru_lam, v_norm_final):
    given = dict(locals())
    w = {n: given[n] for n in WEIGHTS}
    m = {n: given["m_" + n] for n in WEIGHTS}
    v = {n: given["v_" + n] for n in WEIGHTS}
    chip = 2 * lax.axis_index("x") + lax.axis_index("y")

    fw = _gather_weights(w)
    loss, dx, grads = _local_step(x[0], p[:, 0], loss_target[0], fw)
    g, loss = _reduce(grads, w, chip, loss)

    delta, new_m, new_v = {}, {}, {}
    for n in BIG:
        delta[n], new_m[n], new_v[n] = _adamw(w[n], g[n], m[n], v[n], name=f"adamw_{n}")
    shapes = [w[n].shape for n in SMALL]
    packed = [_pack([d[n] for n in SMALL]) for d in (w, g, m, v)]
    for d, buf in zip((delta, new_m, new_v), _adamw(*packed, name="adamw_small")):
        d.update(zip(SMALL, _unpack(buf, shapes)))
    return (loss, dx[None], *[g[n] for n in WEIGHTS], *[delta[n] for n in WEIGHTS],
            *[new_m[n] for n in WEIGHTS], *[new_v[n] for n in WEIGHTS])
```

```python
import functools
import math

import jax
import jax.numpy as jnp
from jax import lax
from jax.experimental import pallas as pl
from jax.experimental.pallas import tpu as pltpu

F32 = jnp.float32
BF16 = jnp.bfloat16
HI = lax.Precision.HIGHEST
MESH = pl.DeviceIdType.MESH
SDS = jax.ShapeDtypeStruct
VMEM_LIMIT = 56 * 1024 * 1024
MM_VMEM_BUDGET = 40 * 1024 * 1024
ANY = pl.BlockSpec(memory_space=pl.ANY)

D = 2048
PL_DIM = 256
D_FF = 4 * D
EPS = 1e-6
S5_W, S5_G, S5_GROUPS, S5_P = 512, 16, 32, 64
S5_N = S5_GROUPS * S5_P
SSD_W, SSD_HD, SSD_H, SSD_NG, SSD_N, SSD_L = 1536, 64, 24, 4, 128, 128
SSD_CONV = SSD_W + 2 * SSD_NG * SSD_N
EVEN_IN = S5_W + SSD_W + SSD_CONV + SSD_H
EVEN_PAD = 5120
RW_W, RW_H, RW_HD = 1024, 16, 64
RW_LORA = 96
RW_GATE = 256
RW_IN = 3 * RW_W + 2 * RW_LORA + RW_GATE
RW_PAD = 3584
LRU_W, LRU_B = 1024, 16
ODD_IN = RW_IN + 2 * LRU_W
ODD_PAD = RW_PAD + 2 * LRU_W
GN_EPS = 64e-5
LRU_C = 8.0
ADAM_LR, ADAM_B1, ADAM_B2, ADAM_EPS, ADAM_WD, ADAM_STEP = 0.001, 0.9, 0.999, 1e-08, 0.01, 10


def _cparams(sem=("arbitrary",)):
    return pltpu.CompilerParams(dimension_semantics=sem, vmem_limit_bytes=VMEM_LIMIT)


def _dot16(a, b, dims=(((1,), (0,)), ((), ()))):
    return lax.dot_general(a.astype(BF16), b.astype(BF16), dims, preferred_element_type=F32)


NN = (((1,), (0,)), ((), ()))
NT = (((1,), (1,)), ((), ()))
TN = (((0,), (0,)), ((), ()))


def _split3(x):
    top = lambda z: lax.bitcast_convert_type(lax.bitcast_convert_type(z, jnp.int32) & jnp.int32(-65536), F32)
    hi = top(x)
    rest = x - hi
    mid = top(rest)
    return hi.astype(BF16), mid.astype(BF16), (rest - mid).astype(BF16)


def _sel_raw(a, b, dims, data):
    parts = _split3((a, b)[data].astype(F32))
    mask = (a, b)[1 - data].astype(BF16)
    acc = None
    for part in reversed(parts):
        ops = (part, mask) if data == 0 else (mask, part)
        term = lax.dot_general(*ops, dims, preferred_element_type=F32)
        acc = term if acc is None else acc + term
    return acc


_SEL_BACK = {(NN, 0): ("g", "m", NT, 0), (NT, 0): ("g", "m", NN, 0), (TN, 0): ("m", "g", NT, 1),
             (NN, 1): ("m", "g", TN, 1), (NT, 1): ("g", "m", TN, 0), (TN, 1): ("m", "g", NN, 1)}


@functools.partial(jax.custom_vjp, nondiff_argnums=(2, 3))
def _sel_dot(a, b, dims, data):
    return _sel_raw(a, b, dims, data)


def _sel_dot_fwd(a, b, dims, data):
    return _sel_raw(a, b, dims, data), (a, b)


def _sel_dot_bwd(dims, data, res, g):
    mask = res[1 - data]
    left, right, dims2, data2 = _SEL_BACK[(dims, data)]
    grad = _sel_raw(g if left == "g" else mask, g if right == "g" else mask, dims2, data2)
    zero = jnp.zeros_like(mask)
    return (grad, zero) if data == 0 else (zero, grad)


_sel_dot.defvjp(_sel_dot_fwd, _sel_dot_bwd)


def _tile(dim, target):
    if dim <= target:
        return dim
    t = target - target % 128
    while t > 128 and dim % t:
        t -= 128
    assert dim % t == 0, (dim, target)
    return t


def _mm(a, b, *, ta=False, tb=False, add=None, out_dtype=F32, tm=1024, tn=1024, tk=1024, name,
        epilogue=None, extra=(), out_dtypes=None):
    layer = None
    if isinstance(b, tuple):
        b, layer = b
    m, k = (a.shape[1], a.shape[0]) if ta else a.shape
    n = b.shape[-2] if tb else b.shape[-1]
    assert (b.shape[-1] if tb else b.shape[-2]) == k, (a.shape, b.shape, ta, tb)
    ins = [a, b] + ([add] if add is not None else []) + list(extra)
    out_dtypes = out_dtypes or [out_dtype]
    n_in, n_out = len(ins), len(out_dtypes)
    tm, tn = _tile(m, tm), _tile(n, tn)
    tiles = 2 * tm * tn * sum(jnp.dtype(x.dtype).itemsize for x in ins[2:]) + 2 * tm * tn * sum(
        jnp.dtype(dt).itemsize for dt in out_dtypes) + 4 * tm * tn
    per_k = 2 * (tm * a.dtype.itemsize + tn * b.dtype.itemsize)
    tk = _tile(k, max(tk, min(2048, (MM_VMEM_BUDGET - tiles) // per_k // 128 * 128)))
    nk = k // tk
    dims = (((0 if ta else 1,), (1 if tb else 0,)), ((), ()))

    def finish(acc, refs):
        res = epilogue(acc, *[r[...] for r in refs[n_in - len(extra):n_in]]) if epilogue else (acc,)
        for o_ref, val in zip(refs[n_in:n_in + n_out], res):
            o_ref[...] = val.astype(o_ref.dtype)

    def body(*refs):
        a_ref, b_ref, acc_ref = refs[0], refs[1], refs[-1]
        if nk == 1:
            acc = _dot16(a_ref[...], b_ref[...], dims)
            finish(acc + refs[2][...].astype(F32) if add is not None else acc, refs)
            return
        kk = pl.program_id(2)

        @pl.when(kk == 0)
        def _():
            acc_ref[...] = refs[2][...].astype(F32) if add is not None else jnp.zeros_like(acc_ref)

        acc_ref[...] += _dot16(a_ref[...], b_ref[...], dims)

        @pl.when(kk == nk - 1)
        def _():
            finish(acc_ref[...], refs)

    a_spec = pl.BlockSpec((tk, tm), lambda i, j, q: (q, i)) if ta else pl.BlockSpec((tm, tk), lambda i, j, q: (i, q))
    b_spec = pl.BlockSpec((tn, tk), lambda i, j, q: (j, q)) if tb else pl.BlockSpec((tk, tn), lambda i, j, q: (q, j))
    if layer is not None:
        b_spec = (pl.BlockSpec((None, tn, tk), lambda i, j, q: (layer, j, q)) if tb
                  else pl.BlockSpec((None, tk, tn), lambda i, j, q: (layer, q, j)))
    o_spec = pl.BlockSpec((tm, tn), lambda i, j, q: (i, j))
    outs = pl.pallas_call(
        body,
        grid=(m // tm, n // tn, nk),
        in_specs=[a_spec, b_spec] + [o_spec] * (n_in - 2),
        out_specs=[o_spec] * n_out,
        out_shape=[SDS((m, n), dt) for dt in out_dtypes],
        scratch_shapes=[pltpu.VMEM((tm, tn) if nk > 1 else (8, 128), F32)],
        compiler_params=_cparams(("parallel", "parallel", "arbitrary")),
        name=name,
    )(*ins)
    return outs if epilogue else outs[0]


def _mm_grad(x, dy, *, layer, cols_cut, shard, prev, name):
    t = x.shape[0]
    r, c = shard
    tm, tn = _tile(r, 1024), _tile(c, 1024)
    per_k = 2 * (tm * x.dtype.itemsize + tn * dy.dtype.itemsize)
    tk = _tile(t, max(1024, min(2048, (MM_VMEM_BUDGET - 12 * tm * tn) // per_k // 128 * 128)))
    nk = t // tk
    if cols_cut:
        assert x.shape[1] == r and dy.shape[1] == 4 * c
        per = c // tn
        omap = lambda i, j, q: (j // per, layer, i, j % per)
    else:
        assert x.shape[1] == 4 * r and dy.shape[1] == c
        per = r // tm
        omap = lambda i, j, q: (i // per, layer, i % per, j)

    def body(*refs):
        x_ref, dy_ref = refs[:2]
        o_ref, acc_ref = refs[-2:]
        kk = pl.program_id(2)

        @pl.when(kk == 0)
        def _():
            acc_ref[...] = jnp.zeros_like(acc_ref)

        acc_ref[...] += _dot16(x_ref[...], dy_ref[...], TN)

        @pl.when(kk == nk - 1)
        def _():
            o_ref[...] = acc_ref[...]

    return pl.pallas_call(
        body,
        grid=(x.shape[1] // tm, dy.shape[1] // tn, nk),
        in_specs=[pl.BlockSpec((tk, tm), lambda i, j, q: (q, i)), pl.BlockSpec((tk, tn), lambda i, j, q: (q, j))]
        + ([ANY] if prev is not None else []),
        out_specs=pl.BlockSpec((None, None, tm, tn), omap),
        out_shape=SDS((4, 2, r, c), F32),
        scratch_shapes=[pltpu.VMEM((tm, tn), F32)],
        input_output_aliases={2: 0} if prev is not None else {},
        compiler_params=_cparams(("parallel", "parallel", "arbitrary")),
        name=name,
    )(x, dy, *([prev] if prev is not None else []))


def _single(fn, consts, *, name):
    outs = jax.eval_shape(fn, *[SDS(c.shape, F32) for c in consts])
    n_in = len(consts)

    def body(*refs):
        res = fn(*[r[...] for r in refs[:n_in]])
        for o_ref, v in zip(refs[n_in:], res):
            o_ref[...] = v

    return pl.pallas_call(body, out_shape=[SDS(o.shape, F32) for o in outs],
                          compiler_params=pltpu.CompilerParams(vmem_limit_bytes=VMEM_LIMIT), name=name)(*consts)


def _single_vjp(fn, consts, cots, *, name):
    n_in = len(consts)

    def body(*refs):
        _, pull = jax.vjp(fn, *[r[...] for r in refs[:n_in]])
        grads = pull(tuple(r[...] for r in refs[n_in:n_in + len(cots)]))
        for o_ref, v in zip(refs[n_in + len(cots):], grads):
            o_ref[...] = v

    return pl.pallas_call(body, out_shape=[SDS(c.shape, F32) for c in consts],
                          compiler_params=pltpu.CompilerParams(vmem_limit_bytes=VMEM_LIMIT), name=name)(*consts, *cots)


def _full_spec(shape):
    nd = len(shape)
    return pl.BlockSpec(shape, lambda i, _n=nd: (0,) * _n)


def _stage_shapes(fn, rows, consts, tb, pos):
    rs = [SDS((tb, r.shape[1]), F32) for r in rows]
    cs = [SDS(c.shape, F32) for c in consts]
    f = (lambda *a: fn(jnp.int32(0), *a)) if pos else fn
    return jax.eval_shape(f, *rs, *cs)


def _stage(fn, rows, consts, *, tb, name, out_dtypes, n_acc=0, pos=False):
    t = rows[0].shape[0]
    assert t % tb == 0
    outs = _stage_shapes(fn, rows, consts, tb, pos)
    n_out = len(outs)
    n_row = n_out - n_acc
    n_in = len(rows) + len(consts)

    def body(*refs):
        i = pl.program_id(0)
        vals = [r[...].astype(F32) for r in refs[:n_in]]
        res = fn(i * tb, *vals) if pos else fn(*vals)
        out_refs = refs[n_in:]
        for q in range(n_row):
            out_refs[q][...] = res[q].astype(out_refs[q].dtype)
        for q in range(n_row, n_out):
            @pl.when(i == 0)
            def _(q=q):
                out_refs[q][...] = jnp.zeros_like(out_refs[q])

            out_refs[q][...] += res[q]

    in_specs = [pl.BlockSpec((tb, r.shape[1]), lambda i: (i, 0)) for r in rows] + [_full_spec(c.shape) for c in consts]
    out_specs = [pl.BlockSpec((tb, o.shape[1]), lambda i: (i, 0)) for o in outs[:n_row]] + [_full_spec(o.shape) for o in outs[n_row:]]
    out_shape = [SDS((t, o.shape[1]), dt) for o, dt in zip(outs[:n_row], out_dtypes)] + [SDS(o.shape, F32) for o in outs[n_row:]]
    return pl.pallas_call(
        body, grid=(t // tb,), in_specs=in_specs, out_specs=out_specs, out_shape=out_shape,
        compiler_params=_cparams(), name=name,
    )(*rows, *consts)


def _stage_vjp(fn, rows, consts, cots, *, tb, name, drow, dconst, drow_dtypes=None, acc_cots=(), pos=False):
    t = rows[0].shape[0]
    assert t % tb == 0
    n_rows, n_consts, n_cots, n_acc = len(rows), len(consts), len(cots), len(acc_cots)
    n_in = n_rows + n_consts + n_cots + n_acc
    drow_dtypes = drow_dtypes or [F32] * len(drow)

    def body(*refs):
        i = pl.program_id(0)
        vals = [r[...].astype(F32) for r in refs[:n_in]]
        rv, cv = vals[:n_rows], vals[n_rows:n_rows + n_consts]
        ct = tuple(vals[n_rows + n_consts:])

        def f(*dargs):
            r2, c2 = list(rv), list(cv)
            for q, idx in enumerate(drow):
                r2[idx] = dargs[q]
            for q, idx in enumerate(dconst):
                c2[idx] = dargs[len(drow) + q]
            return fn(i * tb, *r2, *c2) if pos else fn(*r2, *c2)

        _, pull = jax.vjp(f, *[rv[q] for q in drow], *[cv[q] for q in dconst])
        grads = pull(ct)
        out_refs = refs[n_in:]
        for q in range(len(drow)):
            out_refs[q][...] = grads[q].astype(out_refs[q].dtype)
        for q in range(len(drow), len(drow) + len(dconst)):
            @pl.when(i == 0)
            def _(q=q):
                out_refs[q][...] = jnp.zeros_like(out_refs[q])

            out_refs[q][...] += grads[q]

    in_specs = ([pl.BlockSpec((tb, r.shape[1]), lambda i: (i, 0)) for r in rows] + [_full_spec(c.shape) for c in consts]
                + [pl.BlockSpec((tb, c.shape[1]), lambda i: (i, 0)) for c in cots] + [_full_spec(c.shape) for c in acc_cots])
    out_specs = ([pl.BlockSpec((tb, rows[q].shape[1]), lambda i: (i, 0)) for q in drow]
                 + [_full_spec(consts[q].shape) for q in dconst])
    out_shape = ([SDS(rows[q].shape, dt) for q, dt in zip(drow, drow_dtypes)]
                 + [SDS(consts[q].shape, F32) for q in dconst])
    return pl.pallas_call(
        body, grid=(t // tb,), in_specs=in_specs, out_specs=out_specs, out_shape=out_shape,
        compiler_params=_cparams(), name=name,
    )(*rows, *consts, *cots, *acc_cots)


def _conv_fwd(x, w, b, *, tb, name):
    t, c = x.shape
    r8 = tb // 8

    def body(x_ref, p_ref, w_ref, b_ref, o_ref):
        i = pl.program_id(0)
        x_ = x_ref[...]
        p_ = jnp.where(i > 0, p_ref[...], 0.0)
        w_ = w_ref[...]
        row = lax.broadcasted_iota(jnp.int32, x_.shape, 0)
        row8 = lax.broadcasted_iota(jnp.int32, p_.shape, 0)
        acc = x_ * w_[3:4, :] + b_ref[...]
        head = jnp.zeros_like(p_)
        for j in (1, 2, 3):
            wj = w_[3 - j:4 - j, :]
            acc += jnp.where(row >= j, pltpu.roll(x_, j, 0), 0.0) * wj
            head += jnp.where(row8 < j, pltpu.roll(p_, j, 0), 0.0) * wj
        o_ref[...] = acc
        o_ref[0:8, :] += head

    return pl.pallas_call(
        body, grid=(t // tb,),
        in_specs=[pl.BlockSpec((tb, c), lambda i: (i, 0)),
                  pl.BlockSpec((8, c), lambda i: (jnp.maximum(i * r8 - 1, 0), 0)),
                  _full_spec(w.shape), _full_spec(b.shape)],
        out_specs=pl.BlockSpec((tb, c), lambda i: (i, 0)),
        out_shape=SDS((t, c), F32), compiler_params=_cparams(), name=name,
    )(x, x, w, b)


def _conv_bwd(x, w, dy, *, tb, name):
    t, c = x.shape
    r8 = tb // 8
    nb = t // tb

    def body(x_ref, p_ref, w_ref, g_ref, n_ref, dx_ref, dw_ref, db_ref):
        i = pl.program_id(0)
        x_ = x_ref[...]
        p_ = jnp.where(i > 0, p_ref[...], 0.0)
        g_ = g_ref[...]
        n_ = jnp.where(i < nb - 1, n_ref[...], 0.0)
        w_ = w_ref[...]
        row = lax.broadcasted_iota(jnp.int32, x_.shape, 0)
        row8 = lax.broadcasted_iota(jnp.int32, p_.shape, 0)
        g8 = g_[0:8, :]
        dx = g_ * w_[3:4, :]
        tail = jnp.zeros_like(n_)
        dws = [jnp.sum(g_ * x_, axis=0, keepdims=True)]
        for j in (1, 2, 3):
            wj = w_[3 - j:4 - j, :]
            dx += jnp.where(row < tb - j, pltpu.roll(g_, tb - j, 0), 0.0) * wj
            tail += jnp.where(row8 >= 8 - j, pltpu.roll(n_, 8 - j, 0), 0.0) * wj
            xs = jnp.where(row >= j, pltpu.roll(x_, j, 0), 0.0)
            ps = jnp.where(row8 < j, pltpu.roll(p_, j, 0), 0.0)
            dws.append(jnp.sum(g_ * xs, axis=0, keepdims=True) + jnp.sum(g8 * ps, axis=0, keepdims=True))
        dx_ref[...] = dx
        dx_ref[tb - 8:tb, :] += tail

        @pl.when(i == 0)
        def _():
            dw_ref[...] = jnp.zeros_like(dw_ref)
            db_ref[...] = jnp.zeros_like(db_ref)

        for j in range(4):
            dw_ref[3 - j:4 - j, :] += dws[j]
        db_ref[...] += jnp.sum(g_, axis=0, keepdims=True)

    return pl.pallas_call(
        body, grid=(nb,),
        in_specs=[pl.BlockSpec((tb, c), lambda i: (i, 0)),
                  pl.BlockSpec((8, c), lambda i: (jnp.maximum(i * r8 - 1, 0), 0)),
                  _full_spec(w.shape),
                  pl.BlockSpec((tb, c), lambda i: (i, 0)),
                  pl.BlockSpec((8, c), lambda i: (jnp.minimum((i + 1) * r8, t // 8 - 1), 0))],
        out_specs=[pl.BlockSpec((tb, c), lambda i: (i, 0)), _full_spec((8, c)), _full_spec((1, c))],
        out_shape=[SDS((t, c), F32), SDS((8, c), F32), SDS((1, c), F32)],
        compiler_params=_cparams(), name=name,
    )(x, x, w, dy, dy)


def _lru_scan_fwd(a, b, *, tb, name):
    t, c = a.shape

    def body(a_ref, b_ref, h_ref, st_ref):
        @pl.when(pl.program_id(0) == 0)
        def _():
            st_ref[...] = jnp.zeros_like(st_ref)

        def step(s, h):
            h = a_ref[pl.ds(s, 1), :] * h + b_ref[pl.ds(s, 1), :]
            h_ref[pl.ds(s, 1), :] = h
            return h

        st_ref[...] = lax.fori_loop(0, tb, step, st_ref[...], unroll=8)

    blk = pl.BlockSpec((tb, c), lambda i: (i, 0))
    return pl.pallas_call(
        body, grid=(t // tb,), in_specs=[blk, blk], out_specs=blk, out_shape=SDS((t, c), F32),
        scratch_shapes=[pltpu.VMEM((1, c), F32)], compiler_params=_cparams(), name=name,
    )(a, b)


def _lru_scan_bwd(a, h, dh, *, tb, name):
    t, c = a.shape
    nb = t // tb
    r8 = tb // 8

    def body(a_ref, h_ref, p_ref, g_ref, da_ref, db_ref, st_ref):
        i = pl.program_id(0)

        @pl.when(i == 0)
        def _():
            st_ref[...] = jnp.zeros_like(st_ref)

        hprev0 = jnp.where(i < nb - 1, p_ref[7:8, :], 0.0)

        def step(q, carry):
            s = tb - 1 - q
            g = g_ref[pl.ds(s, 1), :] + carry
            hp = h_ref[pl.ds(jnp.maximum(s - 1, 0), 1), :]
            hp = jnp.where(s > 0, hp, hprev0)
            db_ref[pl.ds(s, 1), :] = g
            da_ref[pl.ds(s, 1), :] = g * hp
            return a_ref[pl.ds(s, 1), :] * g

        st_ref[...] = lax.fori_loop(0, tb, step, st_ref[...], unroll=8)

    rev = pl.BlockSpec((tb, c), lambda i: (nb - 1 - i, 0))
    prev = pl.BlockSpec((8, c), lambda i: (jnp.maximum((nb - 1 - i) * r8 - 1, 0), 0))
    return pl.pallas_call(
        body, grid=(nb,), in_specs=[rev, rev, prev, rev], out_specs=[rev, rev],
        out_shape=[SDS((t, c), F32), SDS((t, c), F32)],
        scratch_shapes=[pltpu.VMEM((1, c), F32)], compiler_params=_cparams(), name=name,
    )(a, h, h, dh)


def _s5_scan_fwd(ar, ai, br, bi, *, tb, name):
    t, c = br.shape

    def body(ar_ref, ai_ref, br_ref, bi_ref, xr_ref, xi_ref, sr_ref, si_ref):
        @pl.when(pl.program_id(0) == 0)
        def _():
            sr_ref[...] = jnp.zeros_like(sr_ref)
            si_ref[...] = jnp.zeros_like(si_ref)

        ar_, ai_ = ar_ref[...], ai_ref[...]

        def step(s, carry):
            xr, xi = carry
            nr = ar_ * xr - ai_ * xi + br_ref[pl.ds(s, 1), :]
            ni = ar_ * xi + ai_ * xr + bi_ref[pl.ds(s, 1), :]
            xr_ref[pl.ds(s, 1), :] = nr
            xi_ref[pl.ds(s, 1), :] = ni
            return nr, ni

        xr, xi = lax.fori_loop(0, tb, step, (sr_ref[...], si_ref[...]), unroll=8)
        sr_ref[...] = xr
        si_ref[...] = xi

    blk = pl.BlockSpec((tb, c), lambda i: (i, 0))
    one = _full_spec((1, c))
    return pl.pallas_call(
        body, grid=(t // tb,), in_specs=[one, one, blk, blk], out_specs=[blk, blk],
        out_shape=[SDS((t, c), F32), SDS((t, c), F32)],
        scratch_shapes=[pltpu.VMEM((1, c), F32), pltpu.VMEM((1, c), F32)], compiler_params=_cparams(), name=name,
    )(ar, ai, br, bi)


def _s5_scan_bwd(ar, ai, xr, xi, dxr, dxi, *, tb, name):
    t, c = xr.shape
    nb = t // tb
    r8 = tb // 8

    def body(ar_ref, ai_ref, xr_ref, xi_ref, pr_ref, pi_ref, gr_ref, gi_ref,
             dbr_ref, dbi_ref, dar_ref, dai_ref, cr_ref, ci_ref):
        i = pl.program_id(0)

        @pl.when(i == 0)
        def _():
            cr_ref[...] = jnp.zeros_like(cr_ref)
            ci_ref[...] = jnp.zeros_like(ci_ref)
            dar_ref[...] = jnp.zeros_like(dar_ref)
            dai_ref[...] = jnp.zeros_like(dai_ref)

        ar_, ai_ = ar_ref[...], ai_ref[...]
        first = i == nb - 1
        pr0 = jnp.where(first, 0.0, pr_ref[7:8, :])
        pi0 = jnp.where(first, 0.0, pi_ref[7:8, :])

        def step(q, carry):
            cr, ci, dar, dai = carry
            s = tb - 1 - q
            gr = gr_ref[pl.ds(s, 1), :] + cr
            gi = gi_ref[pl.ds(s, 1), :] + ci
            sp = jnp.maximum(s - 1, 0)
            xpr = jnp.where(s > 0, xr_ref[pl.ds(sp, 1), :], pr0)
            xpi = jnp.where(s > 0, xi_ref[pl.ds(sp, 1), :], pi0)
            dbr_ref[pl.ds(s, 1), :] = gr
            dbi_ref[pl.ds(s, 1), :] = gi
            dar = dar + gr * xpr + gi * xpi
            dai = dai - gr * xpi + gi * xpr
            return ar_ * gr + ai_ * gi, ar_ * gi - ai_ * gr, dar, dai

        cr, ci, dar, dai = lax.fori_loop(0, tb, step, (cr_ref[...], ci_ref[...], dar_ref[...], dai_ref[...]), unroll=8)
        cr_ref[...] = cr
        ci_ref[...] = ci
        dar_ref[...] = dar
        dai_ref[...] = dai

    rev = pl.BlockSpec((tb, c), lambda i: (nb - 1 - i, 0))
    prev = pl.BlockSpec((8, c), lambda i: (jnp.maximum((nb - 1 - i) * r8 - 1, 0), 0))
    one = _full_spec((1, c))
    return pl.pallas_call(
        body, grid=(nb,), in_specs=[one, one, rev, rev, prev, prev, rev, rev], out_specs=[rev, rev, one, one],
        out_shape=[SDS((t, c), F32), SDS((t, c), F32), SDS((1, c), F32), SDS((1, c), F32)],
        scratch_shapes=[pltpu.VMEM((1, c), F32), pltpu.VMEM((1, c), F32)], compiler_params=_cparams(), name=name,
    )(ar, ai, xr, xi, xr, xi, dxr, dxi)


RW_PAIRS = RW_H // 2


def _pair_consts():
    sub = lax.broadcasted_iota(jnp.int32, (64, 128), 0)
    lane = lax.broadcasted_iota(jnp.int32, (64, 128), 1)
    eye2 = ((lane & 63) == sub).astype(F32)
    r2 = lax.broadcasted_iota(jnp.int32, (128, 128), 0)
    c2 = lax.broadcasted_iota(jnp.int32, (128, 128), 1)
    bsel = ((r2 >> 6) == (c2 >> 6)).astype(BF16)
    return eye2, bsel


def _segsum(x, bsel):
    rows = x.shape[0]
    bits = lax.bitcast_convert_type(x, jnp.int32)
    hi = lax.bitcast_convert_type(bits & jnp.int32(-65536), F32)
    both = jnp.concatenate([hi.astype(BF16), (x - hi).astype(BF16)], axis=0)
    res = jnp.dot(both, bsel, preferred_element_type=F32)
    return res[:rows] + res[rows:]


def _bc(x8):
    return jnp.stack([jnp.broadcast_to(x8[q:q + 1, :], (64, 128)) for q in range(RW_PAIRS)])


def _seg3(x3, bsel):
    return _segsum(x3.reshape(RW_PAIRS * 64, 128), bsel).reshape(RW_PAIRS, 64, 128)


def _seg3_lanes(x3):
    first = lax.broadcasted_iota(jnp.int32, x3.shape, 2) < 64
    lo = jnp.sum(jnp.where(first, x3, 0.0), axis=-1, keepdims=True)
    hi = jnp.sum(jnp.where(first, 0.0, x3), axis=-1, keepdims=True)
    return jnp.where(first, lo, hi)


def _rwkv_scan_fwd(r, w, k, v, kk, a, *, lc, name):
    t = r.shape[0]
    nc = t // lc

    def body(r_ref, w_ref, k_ref, v_ref, kk_ref, a_ref, y_ref, ck_ref, hist_ref, st_ref):
        @pl.when(pl.program_id(0) == 0)
        def _():
            st_ref[...] = jnp.zeros_like(st_ref)

        ck_ref[0] = st_ref[...]
        eye2, bsel = _pair_consts()
        column = lambda ref, s: _seg3(eye2[None] * _bc(ref[s]), bsel)
        read = lambda st, s: jnp.sum(eye2[None] * _seg3(st * _bc(r_ref[s]), bsel), axis=1)

        def step(s, carry):
            st, vb = carry
            kk8 = kk_ref[s]
            sa = -_seg3_lanes(st * _bc(kk8))
            vb_next = column(v_ref, jnp.minimum(s + 1, lc - 1))
            before = jnp.maximum(s - 1, 0)
            y_ref[before] = read(st, before)
            st = st * _bc(w_ref[s]) + sa * _bc(kk8 * a_ref[s]) + vb * _bc(k_ref[s])
            hist_ref[s] = st
            return st, vb_next

        st, _ = lax.fori_loop(0, lc, step, (st_ref[...], column(v_ref, 0)))
        y_ref[lc - 1] = read(st, lc - 1)
        st_ref[...] = st

    blk = pl.BlockSpec((lc, RW_PAIRS, 128), lambda i: (i, 0, 0))
    return pl.pallas_call(
        body, grid=(nc,), in_specs=[blk] * 6,
        out_specs=[blk, pl.BlockSpec((1, RW_PAIRS, 64, 128), lambda i: (i, 0, 0, 0)),
                   pl.BlockSpec((lc, RW_PAIRS, 64, 128), lambda i: (i, 0, 0, 0))],
        out_shape=[SDS((t, RW_PAIRS, 128), F32), SDS((nc, RW_PAIRS, 64, 128), F32), SDS((t, RW_PAIRS, 64, 128), F32)],
        scratch_shapes=[pltpu.VMEM((RW_PAIRS, 64, 128), F32)],
        compiler_params=_cparams(), name=name,
    )(r, w, k, v, kk, a)


def _rwkv_scan_bwd(r, w, k, v, kk, a, ck, hist, dy, *, lc, name):
    t = r.shape[0]
    nc = t // lc

    def body(r_ref, w_ref, k_ref, v_ref, kk_ref, a_ref, ck_ref, hist_ref, dy_ref,
             dr_ref, dw_ref, dk_ref, dv_ref, dkk_ref, da_ref, ds_ref):
        @pl.when(pl.program_id(0) == 0)
        def _():
            ds_ref[...] = jnp.zeros_like(ds_ref)

        eye2, bsel = _pair_consts()
        column = lambda ref, s: _seg3(eye2[None] * _bc(ref[s]), bsel)

        col = lambda z: jnp.sum(z, axis=1)

        def grads(s, s_prev, d_s, dsa):
            kk8 = kk_ref[s]
            sa = -_seg3(s_prev * _bc(kk8), bsel)
            db = col(d_s * sa)
            dw_ref[s] = col(d_s * s_prev)
            dv_ref[s] = col(eye2[None] * _seg3(d_s * _bc(k_ref[s]), bsel))
            dk_ref[s] = col(d_s * column(v_ref, s))
            dkk_ref[s] = db * a_ref[s] - col(s_prev * dsa)
            da_ref[s] = db * kk8

        def back(j, carry):
            ds, d_after, dsa_after, dyb = carry
            s = lc - 1 - j
            kk8 = kk_ref[s]
            d_s = ds + dyb * _bc(r_ref[s])
            dsa = _seg3_lanes(d_s * _bc(kk8 * a_ref[s]))
            dr_ref[s] = col(hist_ref[s] * dyb)
            dyb_before = column(dy_ref, jnp.maximum(s - 1, 0))
            after = jnp.minimum(s + 1, lc - 1)
            grads(after, hist_ref[after - 1], d_after, dsa_after)
            return d_s * _bc(w_ref[s]) - dsa * _bc(kk8), d_s, dsa, dyb_before

        zero = jnp.zeros((RW_PAIRS, 64, 128), F32)
        ds, d_first, dsa_first, _ = lax.fori_loop(0, lc, back, (ds_ref[...], zero, zero, column(dy_ref, lc - 1)))
        grads(0, ck_ref[0], d_first, dsa_first)
        ds_ref[...] = ds

    rev = pl.BlockSpec((lc, RW_PAIRS, 128), lambda i: (nc - 1 - i, 0, 0))
    return pl.pallas_call(
        body, grid=(nc,),
        in_specs=[rev] * 6 + [pl.BlockSpec((1, RW_PAIRS, 64, 128), lambda i: (nc - 1 - i, 0, 0, 0)),
                              pl.BlockSpec((lc, RW_PAIRS, 64, 128), lambda i: (nc - 1 - i, 0, 0, 0)), rev],
        out_specs=[rev] * 6, out_shape=[SDS((t, RW_PAIRS, 128), F32)] * 6,
        scratch_shapes=[pltpu.VMEM((RW_PAIRS, 64, 128), F32)],
        compiler_params=_cparams(), name=name,
    )(r, w, k, v, kk, a, ck, hist, dy)


SSD_PAIRS = SSD_H // 2


def _ssd_chunk(states, xdt, da, bm, cm):
    ln = SSD_L
    row = lax.broadcasted_iota(jnp.int32, (ln, ln), 0)
    col = lax.broadcasted_iota(jnp.int32, (ln, ln), 1)
    causal = row >= col
    acum = _sel_dot(causal.astype(F32), da, NN, 1)
    acum_t = _sel_dot(da, (row <= col).astype(F32), TN, 0)
    sub = lax.broadcasted_iota(jnp.int32, (128, 128), 0)
    lane = lax.broadcasted_iota(jnp.int32, (128, 128), 1)
    ys, new_states = [], []
    for q in range(SSD_PAIRS):
        g = q // (SSD_PAIRS // SSD_NG)
        bg = bm[:, g * SSD_N:(g + 1) * SSD_N]
        cg = cm[:, g * SSD_N:(g + 1) * SSD_N]
        xq = xdt[:, q * 128:(q + 1) * 128]
        scores = _dot16(cg, bg, NT)
        aexp = _sel_dot(acum, (sub == 2 * q + (lane >> 6)).astype(F32), NN, 0)
        tot = aexp[ln - 1:ln, :]
        yh = []
        for h in (2 * q, 2 * q + 1):
            seg = _sel_dot(acum, (sub == h).astype(F32), NN, 0) - acum_t[h:h + 1, :]
            yh.append(_dot16(scores * jnp.exp(jnp.where(causal, seg, -1e30)), xq))
        y = jnp.where(lane < 64, yh[0], yh[1]) + _dot16(cg, states[q]) * jnp.exp(aexp)
        new = _dot16(bg, xq * jnp.exp(tot - aexp), TN)
        ys.append(y)
        new_states.append(states[q] * jnp.exp(tot) + new)
    return jnp.concatenate(ys, axis=1), new_states


def _ssd_fwd(xdt, da, bm, cm, *, name):
    t = xdt.shape[0]
    nc = t // SSD_L

    def body(x_ref, a_ref, b_ref, c_ref, y_ref, ck_ref, st_ref):
        @pl.when(pl.program_id(0) == 0)
        def _():
            st_ref[...] = jnp.zeros_like(st_ref)

        ck_ref[0] = st_ref[...]
        y, new = _ssd_chunk([st_ref[q] for q in range(SSD_PAIRS)], x_ref[...], a_ref[...], b_ref[...], c_ref[...])
        y_ref[...] = y
        for q in range(SSD_PAIRS):
            st_ref[q] = new[q]

    blk = lambda wd: pl.BlockSpec((SSD_L, wd), lambda i: (i, 0))
    return pl.pallas_call(
        body, grid=(nc,), in_specs=[blk(SSD_W), blk(128), blk(512), blk(512)],
        out_specs=[blk(SSD_W), pl.BlockSpec((1, SSD_PAIRS, 128, 128), lambda i: (i, 0, 0, 0))],
        out_shape=[SDS((t, SSD_W), F32), SDS((nc, SSD_PAIRS, 128, 128), F32)],
        scratch_shapes=[pltpu.VMEM((SSD_PAIRS, 128, 128), F32)], compiler_params=_cparams(), name=name,
    )(xdt, da, bm, cm)


def _ssd_bwd(xdt, da, bm, cm, ck, dy, *, name):
    t = xdt.shape[0]
    nc = t // SSD_L

    def body(x_ref, a_ref, b_ref, c_ref, ck_ref, dy_ref, dx_ref, dda_ref, db_ref, dc_ref, ds_ref):
        @pl.when(pl.program_id(0) == 0)
        def _():
            ds_ref[...] = jnp.zeros_like(ds_ref)

        _, pull = jax.vjp(_ssd_chunk, [ck_ref[0, q] for q in range(SSD_PAIRS)], x_ref[...], a_ref[...], b_ref[...], c_ref[...])
        dst, dx, dda, db, dc = pull((dy_ref[...], [ds_ref[q] for q in range(SSD_PAIRS)]))
        dx_ref[...] = dx
        dda_ref[...] = dda
        db_ref[...] = db
        dc_ref[...] = dc
        for q in range(SSD_PAIRS):
            ds_ref[q] = dst[q]

    rev = lambda wd: pl.BlockSpec((SSD_L, wd), lambda i: (nc - 1 - i, 0))
    return pl.pallas_call(
        body, grid=(nc,),
        in_specs=[rev(SSD_W), rev(128), rev(512), rev(512),
                  pl.BlockSpec((1, SSD_PAIRS, 128, 128), lambda i: (nc - 1 - i, 0, 0, 0)), rev(SSD_W)],
        out_specs=[rev(SSD_W), rev(128), rev(512), rev(512)],
        out_shape=[SDS((t, SSD_W), F32), SDS((t, 128), F32), SDS((t, 512), F32), SDS((t, 512), F32)],
        scratch_shapes=[pltpu.VMEM((SSD_PAIRS, 128, 128), F32)], compiler_params=_cparams(), name=name,
    )(xdt, da, bm, cm, ck, dy)


def _iota(shape, dim):
    return lax.broadcasted_iota(jnp.int32, shape, dim)


def _rms(x, g):
    return x * lax.rsqrt(jnp.mean(x * x, axis=-1, keepdims=True) + EPS) * g


def _head_sel(width, shift):
    return ((_iota((width, 128), 0) >> shift) == _iota((width, 128), 1)).astype(F32)


def _head_sum(x, shift=6):
    sel = _head_sel(x.shape[1], shift)
    return _sel_dot(_sel_dot(x, sel, NN, 0), sel, NT, 0)


def _head_expand(x, width, shift=6):
    return _sel_dot(x, _head_sel(width, shift), NT, 0)


def f_norm(h, g):
    return (_rms(h, g),)


def f_norm_pass(h, g):
    return _rms(h, g), h


def f_add_norm(h, m, g):
    h1 = h + m
    return h1, _rms(h1, g)


def f_relu2(u):
    r = jnp.maximum(u, 0.0)
    return (r * r,)


def f_plgate(h2, gl, pp):
    return (h2 + jax.nn.sigmoid(gl) * pp,)


def f_loss(h, tgt, g):
    err = _rms(h, g) - tgt
    part = 0.5 * jnp.sum(jnp.mean(err * err, axis=-1, keepdims=True), axis=0, keepdims=True)
    return (jnp.broadcast_to(part, (8, 128)),)


def f_s5_prep(lam_re, lam_im, lstep, bre_t, bim_t, cre_t, cim_t):
    step = jnp.exp(_sel_dot(lstep, _head_sel(S5_N, 6), NT, 0)[0:1, :])
    mag = jnp.exp(lam_re * step)
    abar_re, abar_im = mag * jnp.cos(lam_im * step), mag * jnp.sin(lam_im * step)
    den = lam_re * lam_re + lam_im * lam_im
    nr = abar_re - 1.0
    coef_re = (nr * lam_re + abar_im * lam_im) / den
    coef_im = (abar_im * lam_re - nr * lam_im) / den
    bbar_re = coef_re * bre_t - coef_im * bim_t
    bbar_im = coef_re * bim_t + coef_im * bre_t
    rep = ((_iota((S5_W, S5_G), 0) & (S5_G - 1)) == _iota((S5_W, S5_G), 1)).astype(F32)
    blk = ((_iota((S5_W, S5_N), 0) >> 4) == (_iota((S5_W, S5_N), 1) >> 6)).astype(F32)
    blk_t = ((_iota((S5_N, S5_W), 0) >> 6) == (_iota((S5_N, S5_W), 1) >> 4)).astype(F32)
    wb_re, wb_im = _sel_dot(rep, bbar_re, NN, 1) * blk, _sel_dot(rep, bbar_im, NN, 1) * blk
    wc_re, wc_im = _sel_dot(cre_t, rep, NT, 0) * blk_t, _sel_dot(cim_t, rep, NT, 0) * blk_t
    return abar_re, abar_im, wb_re, wb_im, wc_re, wc_im


def f_s5_post(xr, xi, u, wc_re, wc_im, d_skip, glu_w, glu_b):
    y = _dot16(xr, wc_re) - _dot16(xi, wc_im) + d_skip * u
    act = jax.nn.gelu(y)
    return (act * jax.nn.sigmoid(_dot16(act, glu_w) + glu_b),)


def f_ssd_pre(xc, dtr, dt_bias, a_log):
    act = jax.nn.silu(xc)
    heads = _iota(dtr.shape, 1) < SSD_H
    dt = jnp.where(heads, jax.nn.softplus(dtr + dt_bias), 0.0)
    da = dt * (-jnp.exp(a_log))
    xdt = act[:, :SSD_W] * _head_expand(dt, SSD_W)
    return xdt, da, act[:, SSD_W:SSD_W + 512], act[:, SSD_W + 512:]


def f_ssd_pre_pass(xc, dtr, dt_bias, a_log):
    return f_ssd_pre(xc, dtr, dt_bias, a_log) + (xc,)


def f_ssd_post(y, xc, z, d_skip, norm_g):
    xs = jax.nn.silu(xc[:, :SSD_W])
    y = (y + xs * _head_expand(d_skip, SSD_W)) * jax.nn.silu(z)
    gw = SSD_W // SSD_NG
    parts = []
    for g in range(SSD_NG):
        seg = y[:, g * gw:(g + 1) * gw]
        parts.append(seg * lax.rsqrt(jnp.mean(seg * seg, axis=-1, keepdims=True) + EPS))
    return (jnp.concatenate(parts, axis=1) * norm_g,)


def f_rwkv_pre(f, w0, w_up, a0, a_up, g_up, k_k, k_a):
    r, k, v = f[:, 0:1024], f[:, 1024:2048], f[:, 2048:3072]
    wl, al, gl = f[:, 3072:3200], f[:, 3200:3328], f[:, 3328:3584]
    w = -jax.nn.softplus(-(w0 + _dot16(jnp.tanh(wl), w_up))) - 0.5
    decay = jnp.exp(-jnp.exp(w))
    a = jax.nn.sigmoid(a0 + _dot16(al, a_up))
    g = _dot16(jax.nn.sigmoid(gl), g_up)
    kk = k * k_k
    k2 = k * (1.0 + (a - 1.0) * k_a)
    kkn = kk * lax.rsqrt(jnp.maximum(_head_sum(kk * kk), 1e-24))
    return r, decay, k2, v, kkn, a, g


def f_rwkv_pre_pass(f, w0, w_up, a0, a_up, g_up, k_k, k_a):
    out = f_rwkv_pre(f, w0, w_up, a0, a_up, g_up, k_k, k_a)
    return out + (out[0], out[2], out[3])


def f_rwkv_post(y, r, k2, v, g, ln_g, ln_b, r_k):
    mean = _head_sum(y) * (1.0 / RW_HD)
    yc = y - mean
    var = _head_sum(yc * yc) * (1.0 / RW_HD)
    yn = yc * lax.rsqrt(var + GN_EPS) * ln_g + ln_b
    bonus = _head_sum(r * k2 * r_k) * v
    return ((yn + bonus) * g,)


def _neg_expm1(y):
    series = -y * (1.0 + y * (0.5 + y * (1.0 / 6.0 + y * (1.0 / 24.0 + y * (1.0 / 120.0)))))
    return jnp.where(y > -0.1, series, 1.0 - jnp.exp(y))


def f_lru_pre(t0, xc, w_a, b_a, w_x, b_x, lam):
    gate_r = jax.nn.sigmoid(_dot16(xc, w_a) + b_a)
    gate_i = jax.nn.sigmoid(_dot16(xc, w_x) + b_x)
    log_a = -LRU_C * gate_r * jax.nn.softplus(-lam)
    mult = jnp.sqrt(jnp.maximum(_neg_expm1(2.0 * log_a), 0.0))
    mult = jnp.where(_iota(xc.shape, 0) + t0 == 0, 1.0, mult)
    return jnp.exp(log_a), xc * gate_i * mult


def f_lru_post(h, gl):
    return (h * jax.nn.gelu(gl),)


TB = 256
TBH = 256
SCAN_TB = 256
RW_LC = 64


def _even_fwd(hn, w, tag):
    n = lambda s: f"{tag}_{s}"
    u = _mm(hn, w["in_u"], name=n("proj_u"))
    z = _mm(hn, w["in_z"], name=n("proj_z"))
    xbc = _mm(hn, w["in_xbc"], name=n("proj_xbc"))
    dtr = _mm(hn, w["in_dt"], name=n("proj_dt"))
    bu_re = _mm(u, w["wb_re"], name=n("s5_bu_re"))
    bu_im = _mm(u, w["wb_im"], name=n("s5_bu_im"))
    xr, xi = _s5_scan_fwd(w["abar_re"], w["abar_im"], bu_re, bu_im, tb=SCAN_TB, name=n("s5_scan"))
    s5c = [w["wc_re"], w["wc_im"], w["s5_d"], w["glu_w"], w["glu_b"]]
    (ya,) = _stage(f_s5_post, [xr, xi, u], s5c, tb=TB, name=n("s5_post"), out_dtypes=[BF16])
    xc = _conv_fwd(xbc, w["ssd_conv_w"], w["ssd_conv_b"], tb=TB, name=n("ssd_conv"))
    xdt, da, bm, cm = _stage(f_ssd_pre, [xc, dtr], [w["dt_bias"], w["a_log"]], tb=TB, name=n("ssd_pre"),
                             out_dtypes=[F32] * 4)
    y, ck = _ssd_fwd(xdt, da, bm, cm, name=n("ssd_scan"))
    (yb,) = _stage(f_ssd_post, [y, xc, z], [w["ssd_d"], w["ssd_norm"]], tb=TB, name=n("ssd_post"), out_dtypes=[BF16])
    mo = _mm(ya, w["out_a"], name=n("out_a"))
    mo = _mm(yb, w["out_b"], add=mo, name=n("out_b"))
    res = dict(u=u, z=z, xbc=xbc, dtr=dtr, xr=xr, xi=xi, ya=ya, xc=xc, xdt=xdt, da=da, bm=bm, cm=cm, y=y, ck=ck, yb=yb)
    return mo, res


def _even_bwd(dmo, hn, w, r, tag):
    n = lambda s: f"{tag}_{s}"
    g = {}
    g["out_a"] = _mm(r["ya"], dmo, ta=True, name=n("d_out_a"))
    g["out_b"] = _mm(r["yb"], dmo, ta=True, name=n("d_out_b"))
    dya = _mm(dmo, w["out_a"], tb=True, name=n("dya"))
    dyb = _mm(dmo, w["out_b"], tb=True, name=n("dyb"))
    dy, dxc1, dz, g["ssd_d"], g["ssd_norm"] = _stage_vjp(
        f_ssd_post, [r["y"], r["xc"], r["z"]], [w["ssd_d"], w["ssd_norm"]], [dyb], tb=TBH, name=n("ssd_post_b"),
        drow=[0, 1, 2], dconst=[0, 1])
    dxdt, dda, dbm, dcm = _ssd_bwd(r["xdt"], r["da"], r["bm"], r["cm"], r["ck"], dy, name=n("ssd_scan_b"))
    dxc, ddtr, g["dt_bias"], g["a_log"] = _stage_vjp(
        f_ssd_pre_pass, [r["xc"], r["dtr"]], [w["dt_bias"], w["a_log"]], [dxdt, dda, dbm, dcm, dxc1], tb=TBH,
        name=n("ssd_pre_b"), drow=[0, 1], dconst=[0, 1])
    dxbc, g["ssd_conv_w"], g["ssd_conv_b"] = _conv_bwd(r["xbc"], w["ssd_conv_w"], dxc, tb=TB, name=n("ssd_conv_b"))
    s5c = [w["wc_re"], w["wc_im"], w["s5_d"], w["glu_w"], w["glu_b"]]
    dxr, dxi, du1, g["wc_re"], g["wc_im"], g["s5_d"], g["glu_w"], g["glu_b"] = _stage_vjp(
        f_s5_post, [r["xr"], r["xi"], r["u"]], s5c, [dya], tb=TBH, name=n("s5_post_b"),
        drow=[0, 1, 2], dconst=[0, 1, 2, 3, 4])
    dbr, dbi, g["abar_re"], g["abar_im"] = _s5_scan_bwd(w["abar_re"], w["abar_im"], r["xr"], r["xi"], dxr, dxi,
                                                         tb=SCAN_TB, name=n("s5_scan_b"))
    g["wb_re"] = _mm(r["u"], dbr, ta=True, name=n("d_wb_re"))
    g["wb_im"] = _mm(r["u"], dbi, ta=True, name=n("d_wb_im"))
    du = _mm(dbr, w["wb_re"], tb=True, add=du1, name=n("du_re"))
    du = _mm(dbi, w["wb_im"], tb=True, add=du, name=n("du_im"))
    segs = (("in_u", du), ("in_z", dz), ("in_xbc", dxbc), ("in_dt", ddtr))
    dhn = None
    for key, dseg in segs:
        g[key] = _mm(hn, dseg, ta=True, name=n("d_" + key))
        dhn = _mm(dseg, w[key], tb=True, add=dhn, name=n("dhn_" + key))
    return dhn, g


def _odd_fwd(hn, w, tag):
    n = lambda s: f"{tag}_{s}"
    rw = _mm(hn, w["in_rw"], name=n("proj_rw"))
    xl = _mm(hn, w["in_xl"], name=n("proj_xl"))
    gl = _mm(hn, w["in_gl"], name=n("proj_gl"))
    f = _conv_fwd(rw, w["mix_w"], w["mix_b"], tb=TB, name=n("rwkv_shift"))
    rc = [w[k] for k in ("w0", "w_up", "a0", "a_up", "g_up", "k_k", "k_a")]
    r_, dec, k2, v, kkn, a, gate = _stage(f_rwkv_pre, [f], rc, tb=TB, name=n("rwkv_pre"), out_dtypes=[F32] * 7)
    t3 = lambda z: z.reshape(-1, RW_PAIRS, 128)
    y, ck, hist = _rwkv_scan_fwd(t3(r_), t3(dec), t3(k2), t3(v), t3(kkn), t3(a), lc=RW_LC, name=n("rwkv_scan"))
    y = y.reshape(-1, RW_W)
    (yc,) = _stage(f_rwkv_post, [y, r_, k2, v, gate], [w["ln_g"], w["ln_b"], w["r_k"]], tb=TB, name=n("rwkv_post"),
                   out_dtypes=[BF16])
    xc = _conv_fwd(xl, w["lru_conv_w"], w["lru_conv_b"], tb=TB, name=n("lru_conv"))
    lc = [w[k] for k in ("lru_wa", "lru_b_a", "lru_wx", "lru_b_x", "lru_lam")]
    a_l, bx = _stage(f_lru_pre, [xc], lc, tb=TB, name=n("lru_pre"), out_dtypes=[F32] * 2, pos=True)
    h = _lru_scan_fwd(a_l, bx, tb=SCAN_TB, name=n("lru_scan"))
    (yd,) = _stage(f_lru_post, [h, gl], [], tb=TB, name=n("lru_post"), out_dtypes=[BF16])
    mo = _mm(yc, w["out_a"], name=n("out_a"))
    mo = _mm(yd, w["out_b"], add=mo, name=n("out_b"))
    res = dict(rw=rw, xl=xl, gl=gl, f=f, r=r_, dec=dec, k2=k2, v=v, kkn=kkn, a=a, gate=gate, y=y, ck=ck, hist=hist, yc=yc,
               xc=xc, a_l=a_l, h=h, yd=yd)
    return mo, res


def _odd_bwd(dmo, hn, w, r, tag):
    n = lambda s: f"{tag}_{s}"
    g = {}
    g["out_a"] = _mm(r["yc"], dmo, ta=True, name=n("d_out_a"))
    g["out_b"] = _mm(r["yd"], dmo, ta=True, name=n("d_out_b"))
    dyc = _mm(dmo, w["out_a"], tb=True, name=n("dyc"))
    dyd = _mm(dmo, w["out_b"], tb=True, name=n("dyd"))
    dh, dgl = _stage_vjp(f_lru_post, [r["h"], r["gl"]], [], [dyd], tb=TB, name=n("lru_post_b"), drow=[0, 1], dconst=[])
    da_l, dbx = _lru_scan_bwd(r["a_l"], r["h"], dh, tb=SCAN_TB, name=n("lru_scan_b"))
    lc = [w[k] for k in ("lru_wa", "lru_b_a", "lru_wx", "lru_b_x", "lru_lam")]
    dxc, g["lru_wa"], g["lru_b_a"], g["lru_wx"], g["lru_b_x"], g["lru_lam"] = _stage_vjp(
        f_lru_pre, [r["xc"]], lc, [da_l, dbx], tb=TBH, name=n("lru_pre_b"), drow=[0], dconst=[0, 1, 2, 3, 4], pos=True)
    dxl, g["lru_conv_w"], g["lru_conv_b"] = _conv_bwd(r["xl"], w["lru_conv_w"], dxc, tb=TB, name=n("lru_conv_b"))
    dy, dr1, dk1, dv1, dgate, g["ln_g"], g["ln_b"], g["r_k"] = _stage_vjp(
        f_rwkv_post, [r["y"], r["r"], r["k2"], r["v"], r["gate"]], [w["ln_g"], w["ln_b"], w["r_k"]], [dyc], tb=TBH,
        name=n("rwkv_post_b"), drow=[0, 1, 2, 3, 4], dconst=[0, 1, 2])
    t3 = lambda z: z.reshape(-1, RW_PAIRS, 128)
    dr2, ddec, dk2, dv2, dkkn, da = [z.reshape(-1, RW_W) for z in _rwkv_scan_bwd(
        t3(r["r"]), t3(r["dec"]), t3(r["k2"]), t3(r["v"]), t3(r["kkn"]), t3(r["a"]), r["ck"], r["hist"], t3(dy),
        lc=RW_LC, name=n("rwkv_scan_b"))]
    rc = [w[k] for k in ("w0", "w_up", "a0", "a_up", "g_up", "k_k", "k_a")]
    df, g["w0"], g["w_up"], g["a0"], g["a_up"], g["g_up"], g["k_k"], g["k_a"] = _stage_vjp(
        f_rwkv_pre_pass, [r["f"]], rc, [dr2, ddec, dk2, dv2, dkkn, da, dgate, dr1, dk1, dv1], tb=TBH,
        name=n("rwkv_pre_b"), drow=[0], dconst=[0, 1, 2, 3, 4, 5, 6])
    drw, g["mix_w"], _ = _conv_bwd(r["rw"], w["mix_w"], df, tb=TB, name=n("rwkv_shift_b"))
    segs = (("in_rw", drw), ("in_xl", dxl), ("in_gl", dgl))
    dhn = None
    for key, dseg in segs:
        g[key] = _mm(hn, dseg, ta=True, name=n("d_" + key))
        dhn = _mm(dseg, w[key], tb=True, add=dhn, name=n("dhn_" + key))
    return dhn, g


def _layer_fwd(h, p_i, w, odd, tag):
    n = lambda s: f"{tag}_{s}"
    (hn,) = _stage(f_norm, [h], [w["norm_mix"]], tb=TB, name=n("norm_mix"), out_dtypes=[BF16])
    mo, mres = (_odd_fwd if odd else _even_fwd)(hn, w, tag)
    h1, hf = _stage(f_add_norm, [h, mo], [w["norm_ffn"]], tb=TB, name=n("norm_ffn"), out_dtypes=[F32, BF16])
    u, act = _mm(hf, w["mlp_w1"], name=n("mlp_up"), epilogue=lambda acc: (acc,) + f_relu2(acc), out_dtypes=[F32, BF16])
    m2 = _mm(act, w["mlp_w2"], name=n("mlp_down"))
    h2, hp = _stage(f_add_norm, [h1, m2], [w["norm_pl"]], tb=TB, name=n("norm_pl"), out_dtypes=[F32, BF16])
    gl = _mm(hp, w["pl_gate"], name=n("pl_gate"))
    pp = _mm(p_i, w["pl_proj"], name=n("pl_proj"))
    (h3,) = _stage(f_plgate, [h2, gl, pp], [], tb=TB, name=n("pl_mix"), out_dtypes=[F32])
    res = dict(h=h, hn=hn, mo=mo, mix=mres, h1=h1, hf=hf, u=u, act=act, m2=m2, h2=h2, hp=hp, gl=gl, pp=pp)
    return h3, res


def _layer_bwd(dh3, p_i, w, r, odd, tag, stacks):
    n = lambda s: f"{tag}_{s}"
    g = {}
    wgrad = lambda key, x, dy, cols_cut, shard: _mm_grad(x, dy, layer=int(odd), cols_cut=cols_cut, shard=shard,
                                                        prev=stacks[key] if stacks else None, name=n("d_" + key))
    dh2, dgl, dpp = _stage_vjp(f_plgate, [r["h2"], r["gl"], r["pp"]], [], [dh3], tb=TB, name=n("pl_mix_b"),
                               drow=[0, 1, 2], dconst=[])
    g["pl_proj"] = wgrad("pl_proj", p_i, dpp, True, (PL_DIM, D // 4))
    g["pl_gate"] = wgrad("pl_gate", r["hp"], dgl, False, (D // 4, D))
    dhp = _mm(dgl, w["pl_gate"], tb=True, name=n("dhp"))
    dh1, dm2, g["norm_pl"] = _stage_vjp(f_add_norm, [r["h1"], r["m2"]], [w["norm_pl"]], [dh2, dhp], tb=TB,
                                        name=n("norm_pl_b"), drow=[0, 1], dconst=[0])
    g["mlp_w2"] = wgrad("mlp_w2", r["act"], dm2, False, (D_FF // 4, D))
    (du,) = _mm(dm2, w["mlp_w2"], tb=True, name=n("dact"), extra=[r["u"]], out_dtypes=[BF16],
                epilogue=lambda acc, u: (acc * (2.0 * jnp.maximum(u, 0.0)),))
    g["mlp_w1"] = wgrad("mlp_w1", r["hf"], du, True, (D, D_FF // 4))
    dhf = _mm(du, w["mlp_w1"], tb=True, name=n("dhf"))
    dh, dmo, g["norm_ffn"] = _stage_vjp(f_add_norm, [r["h"], r["mo"]], [w["norm_ffn"]], [dh1, dhf], tb=TB,
                                        name=n("norm_ffn_b"), drow=[0, 1], dconst=[0])
    dhn, gm = (_odd_bwd if odd else _even_bwd)(dmo, r["hn"], w, r["mix"], tag)
    g.update(gm)
    dh0, g["norm_mix"] = _stage_vjp(f_norm_pass, [r["h"]], [w["norm_mix"]], [dhn, dh], tb=TB, name=n("norm_mix_b"),
                                    drow=[0], dconst=[0])
    return dh0, g


def _pad_to(a, size, axis):
    pad = [(0, 0)] * a.ndim
    pad[axis] = (0, size - a.shape[axis])
    return jnp.pad(a, pad)


def _rw_pad(a):
    return jnp.concatenate([a[..., :3072], _pad_to(a[..., 3072:3168], 128, -1), _pad_to(a[..., 3168:3264], 128, -1),
                            a[..., 3264:3520]], axis=-1)


def _rw_unpad(a):
    return jnp.concatenate([a[..., :3072], a[..., 3072:3168], a[..., 3200:3296], a[..., 3328:3584]], axis=-1)


def _block_diag(w):
    nb, bs, _ = w.shape
    eye = jnp.eye(nb, dtype=w.dtype)
    return (w[:, :, None, :] * eye[:, None, :, None]).reshape(nb * bs, nb * bs)


def _diag_blocks(w):
    nb = LRU_B
    bs = w.shape[0] // nb
    return jnp.stack([w[h * bs:(h + 1) * bs, h * bs:(h + 1) * bs] for h in range(nb)])


def _s5_prep_inputs(fw):
    lstep = jnp.broadcast_to(_pad_to(fw["s5_log_step"].astype(F32), 128, 1), (8, 128))
    t16 = lambda b: jnp.transpose(b[0], (2, 0, 1)).reshape(S5_G, S5_N)
    tc = lambda c: jnp.transpose(c[0], (0, 2, 1)).reshape(S5_N, S5_G)
    return [fw["s5_lam_re"].reshape(1, S5_N), fw["s5_lam_im"].reshape(1, S5_N), lstep,
            t16(fw["s5_b_re"]), t16(fw["s5_b_im"]), tc(fw["s5_c_re"]), tc(fw["s5_c_im"])]


def _layer_weights(fw, i):
    w = {k: fw[k][i:i + 1] for k in ("norm_mix", "norm_ffn", "norm_pl")}
    for k in ("mlp_w1", "mlp_w2", "pl_proj", "pl_gate"):
        w[k] = (fw[k], i)
    return w


def _even_weights(fw, prep):
    w = _layer_weights(fw, 0)
    ein, eout = fw["e_in_proj"][0], fw["e_out_proj"][0]
    w.update(in_u=ein[:, :512], in_z=ein[:, 512:2048], in_xbc=ein[:, 2048:4608], in_dt=_pad_to(ein[:, 4608:], 128, 1),
             out_a=eout[:512], out_b=eout[512:])
    abar_re, abar_im, wb_re, wb_im, wc_re, wc_im = prep
    w.update(abar_re=abar_re, abar_im=abar_im, wb_re=wb_re, wb_im=wb_im, wc_re=wc_re.astype(BF16), wc_im=wc_im.astype(BF16),
             s5_d=fw["s5_d"], glu_w=fw["s5_glu_w"][0], glu_b=fw["s5_glu_b"],
             ssd_conv_w=_pad_to(fw["ssd_conv_w"][0], 8, 0), ssd_conv_b=fw["ssd_conv_b"],
             dt_bias=_pad_to(fw["ssd_dt_bias"], 128, 1), a_log=_pad_to(fw["ssd_a_log"], 128, 1),
             ssd_d=_pad_to(fw["ssd_d"], 128, 1), ssd_norm=fw["ssd_norm"])
    return w


def _odd_weights(fw):
    w = _layer_weights(fw, 1)
    oin, oout = fw["o_in_proj"][0], fw["o_out_proj"][0]
    mu = _rw_pad(fw["rwkv_mu"])
    zero = jnp.zeros_like(mu)
    w.update(in_rw=_rw_pad(oin[:, :RW_IN]), in_xl=oin[:, RW_IN:RW_IN + LRU_W], in_gl=oin[:, RW_IN + LRU_W:],
             out_a=oout[:RW_W], out_b=oout[RW_W:],
             mix_w=jnp.concatenate([zero, zero, mu, 1.0 - mu, zero, zero, zero, zero], axis=0), mix_b=zero,
             w0=fw["rwkv_w0"], w_up=_pad_to(fw["rwkv_w_up"][0], 128, 0), a0=fw["rwkv_a0"],
             a_up=_pad_to(fw["rwkv_a_up"][0], 128, 0), g_up=fw["rwkv_g_up"][0], k_k=fw["rwkv_k_k"], k_a=fw["rwkv_k_a"],
             r_k=fw["rwkv_r_k"].reshape(1, RW_W), ln_g=fw["rwkv_ln_g"], ln_b=fw["rwkv_ln_b"],
             lru_conv_w=_pad_to(fw["lru_conv_w"][0], 8, 0), lru_conv_b=fw["lru_conv_b"],
             lru_wa=_block_diag(fw["lru_w_a"][0]).astype(BF16), lru_b_a=fw["lru_b_a"].reshape(1, LRU_W),
             lru_wx=_block_diag(fw["lru_w_x"][0]).astype(BF16), lru_b_x=fw["lru_b_x"].reshape(1, LRU_W),
             lru_lam=fw["lru_lam"].reshape(1, LRU_W))
    return w


def _global_grads(g0, g1, s5_grads, d_norm_final):
    out = {k: jnp.concatenate([g0[k], g1[k]], axis=0) for k in ("norm_mix", "norm_ffn", "norm_pl")}
    for k in STACKED:
        out[k] = g0[k]
    out["e_in_proj"] = jnp.concatenate([g0["in_u"], g0["in_z"], g0["in_xbc"], g0["in_dt"][:, :SSD_H]], axis=1)[None]
    out["e_out_proj"] = jnp.concatenate([g0["out_a"], g0["out_b"]], axis=0)[None]
    d_lam_re, d_lam_im, d_lstep, d_bre, d_bim, d_cre, d_cim = s5_grads
    out["s5_lam_re"] = d_lam_re.reshape(1, S5_GROUPS, S5_P)
    out["s5_lam_im"] = d_lam_im.reshape(1, S5_GROUPS, S5_P)
    out["s5_log_step"] = d_lstep[0:1, :S5_GROUPS]
    unb = lambda b: jnp.transpose(b.reshape(S5_G, S5_GROUPS, S5_P), (1, 2, 0))[None]
    unc = lambda c: jnp.transpose(c.reshape(S5_GROUPS, S5_P, S5_G), (0, 2, 1))[None]
    out.update(s5_b_re=unb(d_bre), s5_b_im=unb(d_bim), s5_c_re=unc(d_cre), s5_c_im=unc(d_cim),
               s5_d=g0["s5_d"], s5_glu_w=g0["glu_w"][None], s5_glu_b=g0["glu_b"],
               ssd_conv_w=g0["ssd_conv_w"][None, :4], ssd_conv_b=g0["ssd_conv_b"], ssd_dt_bias=g0["dt_bias"][:, :SSD_H],
               ssd_a_log=g0["a_log"][:, :SSD_H], ssd_d=g0["ssd_d"][:, :SSD_H], ssd_norm=g0["ssd_norm"])
    out["o_in_proj"] = jnp.concatenate([_rw_unpad(g1["in_rw"]), g1["in_xl"], g1["in_gl"]], axis=1)[None]
    out["o_out_proj"] = jnp.concatenate([g1["out_a"], g1["out_b"]], axis=0)[None]
    out.update(rwkv_mu=_rw_unpad(g1["mix_w"][2:3] - g1["mix_w"][3:4]), rwkv_w0=g1["w0"], rwkv_w_up=g1["w_up"][None, :RW_LORA],
               rwkv_a0=g1["a0"], rwkv_a_up=g1["a_up"][None, :RW_LORA], rwkv_g_up=g1["g_up"][None], rwkv_k_k=g1["k_k"],
               rwkv_k_a=g1["k_a"], rwkv_r_k=g1["r_k"].reshape(1, RW_H, RW_HD), rwkv_ln_g=g1["ln_g"], rwkv_ln_b=g1["ln_b"],
               lru_conv_w=g1["lru_conv_w"][None, :4], lru_conv_b=g1["lru_conv_b"],
               lru_w_a=_diag_blocks(g1["lru_wa"])[None], lru_b_a=g1["lru_b_a"].reshape(1, LRU_B, 64),
               lru_w_x=_diag_blocks(g1["lru_wx"])[None], lru_b_x=g1["lru_b_x"].reshape(1, LRU_B, 64),
               lru_lam=g1["lru_lam"].reshape(1, LRU_B, 64), norm_final=d_norm_final.reshape(D))
    return out


def _local_step(x, p, target, fw):
    prep_in = _s5_prep_inputs(fw)
    prep = _single(f_s5_prep, prep_in, name="s5_prep")
    w0, w1 = _even_weights(fw, prep), _odd_weights(fw)
    h1, r0 = _layer_fwd(x, p[0], w0, False, "l0")
    h2, r1 = _layer_fwd(h1, p[1], w1, True, "l1")
    gf = fw["norm_final"].reshape(1, D)
    (loss8,) = _stage(f_loss, [h2, target], [gf], tb=TB, name="loss", out_dtypes=[], n_acc=1)
    one = jnp.zeros((8, 128), F32).at[0, 0].set(1.0)
    dh2, d_gf = _stage_vjp(f_loss, [h2, target], [gf], [], tb=TB, name="loss_b", drow=[0], dconst=[0], acc_cots=[one])
    dh1, g1 = _layer_bwd(dh2, p[1], w1, r1, True, "l1", None)
    dx, g0 = _layer_bwd(dh1, p[0], w0, r0, False, "l0", g1)
    cots = [g0[k] for k in ("abar_re", "abar_im", "wb_re", "wb_im", "wc_re", "wc_im")]
    s5_grads = _single_vjp(f_s5_prep, prep_in, cots, name="s5_prep_b")
    return loss8[0, 0], dx, _global_grads(g0, g1, s5_grads, d_gf)


def _xyc():
    return lax.axis_index("x"), lax.axis_index("y"), lax.axis_index("c")


def _flip(v, bit):
    return 1 - v if bit else v


def _remote(src, dst, send_sems, recv_sems, k, dev):
    return pltpu.make_async_remote_copy(src_ref=src, dst_ref=dst, send_sem=send_sems.at[k], recv_sem=recv_sems.at[k],
                                        device_id=dev, device_id_type=MESH)


CHIP_FLIPS = ((1, 0), (0, 1), (1, 1))


def _gather_chips(arrs, out_shapes, places, *, name):
    n = len(arrs)

    def body(*refs):
        ins, outs = refs[:n], refs[n:2 * n]
        send_sems, recv_sems = refs[2 * n:]
        x, y, c = _xyc()
        chip, sib = 2 * x + y, (x, y, 1 - c)
        peers = [(_flip(x, fx), _flip(y, fy)) for fx, fy in CHIP_FLIPS]
        first = [_remote(ins[a].at[c], places[a](outs[a], chip, c), send_sems, recv_sems, 6 * a + j, (px, py, c))
                 for a in range(n) for j, (px, py) in enumerate(peers)]
        for cp in first:
            cp.start()
        passed = []
        for a in range(n):
            for j, (px, py) in enumerate(peers):
                landed = places[a](outs[a], 2 * px + py, c)
                _remote(ins[a].at[c], landed, send_sems, recv_sems, 6 * a + j, (px, py, c)).wait_recv()
                cp = _remote(landed, landed, send_sems, recv_sems, 6 * a + 3 + j, sib)
                cp.start()
                passed.append(cp)
        for a in range(n):
            for j, (px, py) in enumerate(peers):
                other = places[a](outs[a], 2 * px + py, 1 - c)
                _remote(other, other, send_sems, recv_sems, 6 * a + 3 + j, sib).wait_recv()
        for cp in first + passed:
            cp.wait_send()

    return pl.pallas_call(
        body, out_shape=[SDS(s, a.dtype) for s, a in zip(out_shapes, arrs)], in_specs=[ANY] * n, out_specs=[ANY] * n,
        scratch_shapes=_dma_sems(6 * n), name=name,
    )(*arrs)


def _place_own(full, own, chip_vec, axis, *, name):
    layers, rows, cols = own.shape
    tb = min(rows, 512)
    per = rows // tb
    omap = ((lambda l, i, chip_ref: (l, chip_ref[0] * per + i, 0)) if axis == 1
            else (lambda l, i, chip_ref: (l, i, chip_ref[0])))

    def body(chip_ref, own_ref, full_ref, o_ref):
        o_ref[...] = own_ref[...]

    return pl.pallas_call(
        body,
        grid_spec=pltpu.PrefetchScalarGridSpec(
            num_scalar_prefetch=1, grid=(layers, per),
            in_specs=[pl.BlockSpec((None, tb, cols), lambda l, i, chip_ref: (l, i, 0)), ANY],
            out_specs=pl.BlockSpec((None, tb, cols), omap)),
        out_shape=SDS(full.shape, full.dtype), input_output_aliases={2: 0},
        compiler_params=_cparams(("arbitrary", "arbitrary")), name=name,
    )(chip_vec, own, full)


def _dma_sems(n):
    return [pltpu.SemaphoreType.DMA((n,)), pltpu.SemaphoreType.DMA((n,))]


def _send_halves(arrs, *, name):
    n = len(arrs)

    def body(*refs):
        ins, outs = refs[:n], refs[n:2 * n]
        send_sems, recv_sems = refs[2 * n:]
        x, y, c = _xyc()
        copies = [_remote(ins[a].at[k, 1 - c], outs[a].at[k], send_sems, recv_sems, 4 * a + k, (x, y, 1 - c))
                  for a in range(n) for k in range(arrs[a].shape[0])]
        for cp in copies:
            cp.start()
        for cp in copies:
            cp.wait_recv()
        for cp in copies:
            cp.wait_send()

    return pl.pallas_call(
        body, out_shape=[SDS(a.shape[:1] + a.shape[2:], a.dtype) for a in arrs], in_specs=[ANY] * n, out_specs=[ANY] * n,
        scratch_shapes=_dma_sems(4 * n), name=name,
    )(*arrs)


def _add_half(g, recv, c_vec, *, tb, out_dtype, name):
    slots, _, rh, cols = g.shape
    tb = min(tb, rh)

    def body(c_ref, g_ref, r_ref, o_ref):
        o_ref[...] = (g_ref[...] + r_ref[...]).astype(o_ref.dtype)

    return pl.pallas_call(
        body,
        grid_spec=pltpu.PrefetchScalarGridSpec(
            num_scalar_prefetch=1, grid=(slots, rh // tb),
            in_specs=[pl.BlockSpec((None, None, tb, cols), lambda k, i, c_ref: (k, c_ref[0], i, 0)),
                      pl.BlockSpec((None, tb, cols), lambda k, i, c_ref: (k, i, 0))],
            out_specs=pl.BlockSpec((None, tb, cols), lambda k, i, c_ref: (k, i, 0))),
        out_shape=SDS((slots, rh, cols), out_dtype), compiler_params=_cparams(("arbitrary", "arbitrary")), name=name,
    )(c_vec, g, recv)


def _scatter_chips(arrs, *, name):
    n = len(arrs)

    def body(*refs):
        ins, outs = refs[:n], refs[n:2 * n]
        send_sems, recv_sems = refs[2 * n:]
        x, y, c = _xyc()
        copies = []
        for a in range(n):
            for j, (fx, fy) in enumerate(CHIP_FLIPS):
                px, py = _flip(x, fx), _flip(y, fy)
                mine = ins[a].at[2 * px + py if arrs[a].shape[0] == 4 else 0]
                copies.append(_remote(mine, outs[a].at[j], send_sems, recv_sems, 3 * a + j, (px, py, c)))
        for cp in copies:
            cp.start()
        for cp in copies:
            cp.wait_recv()
        for cp in copies:
            cp.wait_send()

    return pl.pallas_call(
        body, out_shape=[SDS((3,) + a.shape[1:], a.dtype) for a in arrs], in_specs=[ANY] * n, out_specs=[ANY] * n,
        scratch_shapes=_dma_sems(3 * n), name=name,
    )(*arrs)


def _sum_chips(p, landed, chip_vec, *, tb, name):
    _, rh, cols = p.shape
    tb = min(tb, rh)

    def body(chip_ref, p_ref, l_ref, o_ref):
        f = lambda z: z.astype(F32)
        o_ref[...] = ((f(p_ref[...]) + f(l_ref[0])) + f(l_ref[1])) + f(l_ref[2])

    return pl.pallas_call(
        body,
        grid_spec=pltpu.PrefetchScalarGridSpec(
            num_scalar_prefetch=1, grid=(rh // tb,),
            in_specs=[pl.BlockSpec((None, tb, cols), lambda i, chip_ref: (chip_ref[0], i, 0)),
                      pl.BlockSpec((3, tb, cols), lambda i, chip_ref: (0, i, 0))],
            out_specs=pl.BlockSpec((tb, cols), lambda i, chip_ref: (i, 0))),
        out_shape=SDS((rh, cols), F32), compiler_params=_cparams(), name=name,
    )(chip_vec, p, landed)


def _sum_chips_ordered(p, landed, chip_vec, *, tb, name):
    _, rh, cols = p.shape
    tb = min(tb, rh)

    def body(chip_ref, p_ref, l_ref, o_ref):
        chip = chip_ref[0]
        acc = None
        for k in range(4):
            away = k ^ chip
            slot = jnp.where(away == 2, 0, jnp.where(away == 1, 1, 2))
            term = jnp.where(k == chip, p_ref[...], l_ref[slot])
            acc = term if acc is None else acc + term
        o_ref[...] = acc

    return pl.pallas_call(
        body,
        grid_spec=pltpu.PrefetchScalarGridSpec(
            num_scalar_prefetch=1, grid=(rh // tb,),
            in_specs=[pl.BlockSpec((None, tb, cols), lambda i, chip_ref: (0, i, 0)),
                      pl.BlockSpec((3, tb, cols), lambda i, chip_ref: (0, i, 0))],
            out_specs=pl.BlockSpec((tb, cols), lambda i, chip_ref: (i, 0))),
        out_shape=SDS((rh, cols), F32), compiler_params=_cparams(), name=name,
    )(chip_vec, p, landed)


def _swap_halves(arrs, *, name):
    n = len(arrs)

    def body(*refs):
        ins, outs = refs[:n], refs[n:2 * n]
        send_sems, recv_sems = refs[2 * n:]
        x, y, c = _xyc()
        copies = [_remote(ins[a], outs[a], send_sems, recv_sems, a, (x, y, 1 - c)) for a in range(n)]
        for cp in copies:
            cp.start()
        for cp in copies:
            cp.wait_recv()
        for cp in copies:
            cp.wait_send()

    return pl.pallas_call(
        body, out_shape=[SDS(a.shape, a.dtype) for a in arrs], in_specs=[ANY] * n, out_specs=[ANY] * n,
        scratch_shapes=_dma_sems(n), name=name,
    )(*arrs)


def _join_halves(mine, theirs, c_vec, *, tb, name):
    rh, cols = mine.shape
    tb = min(tb, rh)

    def body(c_ref, m_ref, t_ref, o_ref):
        o_ref[...] = jnp.where(pl.program_id(0) == c_ref[0], m_ref[...], t_ref[...])

    blk = pl.BlockSpec((tb, cols), lambda h, i, c_ref: (i, 0))
    return pl.pallas_call(
        body,
        grid_spec=pltpu.PrefetchScalarGridSpec(
            num_scalar_prefetch=1, grid=(2, rh // tb), in_specs=[blk, blk],
            out_specs=pl.BlockSpec((None, tb, cols), lambda h, i, c_ref: (h, i, 0))),
        out_shape=SDS((2, rh, cols), mine.dtype), compiler_params=_cparams(("arbitrary", "arbitrary")), name=name,
    )(c_vec, mine, theirs)


def f_adamw(w, g, m, v):
    m = ADAM_B1 * m + (1.0 - ADAM_B1) * g
    v = ADAM_B2 * v + (1.0 - ADAM_B2) * (g * g)
    m_hat = m / (1.0 - ADAM_B1 ** ADAM_STEP)
    v_hat = v / (1.0 - ADAM_B2 ** ADAM_STEP)
    return -ADAM_LR * (m_hat / (jnp.sqrt(v_hat) + ADAM_EPS) + ADAM_WD * w), m, v


def _adamw(w, g, m, v, *, name):
    shape = w.shape
    two = lambda a: a.reshape(-1, shape[-1])
    rows = two(w).shape[0]
    tb = 256 if rows % 256 == 0 else rows
    outs = _stage(f_adamw, [two(w), two(g), two(m), two(v)], [], tb=tb, name=name, out_dtypes=[F32] * 3)
    return [o.reshape(shape) for o in outs]


def _pack(arrs, rows=8):
    flat = jnp.concatenate([a.astype(F32).reshape(-1) for a in arrs])
    size = -(-flat.shape[0] // (rows * 128)) * (rows * 128)
    return _pad_to(flat, size, 0).reshape(-1, 128)


def _unpack(buf, shapes):
    flat = buf.reshape(-1)
    out, off = [], 0
    for s in shapes:
        n = math.prod(s)
        out.append(flat[off:off + n].reshape(s))
        off += n
    return out


WEIGHTS = ("norm_mix", "norm_ffn", "norm_pl", "mlp_w1", "mlp_w2", "pl_proj", "pl_gate", "e_in_proj", "e_out_proj",
           "s5_lam_re", "s5_lam_im", "s5_log_step", "s5_b_re", "s5_b_im", "s5_c_re", "s5_c_im", "s5_d", "s5_glu_w",
           "s5_glu_b", "ssd_conv_w", "ssd_conv_b", "ssd_dt_bias", "ssd_a_log", "ssd_d", "ssd_norm", "o_in_proj",
           "o_out_proj", "rwkv_mu", "rwkv_w0", "rwkv_w_up", "rwkv_a0", "rwkv_a_up", "rwkv_g_up", "rwkv_k_k", "rwkv_k_a",
           "rwkv_r_k", "rwkv_ln_g", "rwkv_ln_b", "lru_conv_w", "lru_conv_b", "lru_w_a", "lru_b_a", "lru_w_x", "lru_b_x",
           "lru_lam", "norm_final")
BIG = ("mlp_w1", "mlp_w2", "pl_proj", "pl_gate", "e_in_proj", "e_out_proj", "o_in_proj", "o_out_proj")
STACKED = BIG[:4]
SHARD_AXIS = {"mlp_w1": 2, "mlp_w2": 1, "pl_proj": 2, "pl_gate": 1, "e_in_proj": 2, "e_out_proj": 1, "s5_glu_w": 1,
              "ssd_conv_w": 2, "o_in_proj": 2, "o_out_proj": 1, "rwkv_mu": 1, "rwkv_w0": 1, "rwkv_w_up": 2, "rwkv_a0": 1,
              "rwkv_a_up": 2, "rwkv_g_up": 2, "rwkv_k_k": 1, "rwkv_k_a": 1, "rwkv_ln_g": 1, "rwkv_ln_b": 1,
              "lru_conv_w": 2, "lru_conv_b": 1}
SMALL = tuple(n for n in WEIGHTS if n not in BIG)
SMALL_SHARDED = tuple(n for n in SMALL if n in SHARD_AXIS)


def _gather_weights(w):
    shapes = [w[n].shape for n in SMALL_SHARDED]
    chip = 2 * lax.axis_index("x") + lax.axis_index("y")
    mine = [w[n].astype(BF16) for n in BIG] + [_pack([w[n] for n in SMALL_SHARDED], rows=16)]
    out_shapes, places = [], []
    for n, a in zip(BIG + ("small",), mine):
        layers, rows, cols = a.shape if a.ndim == 3 else (1,) + a.shape
        ax = SHARD_AXIS.get(n)
        if ax == 1:
            step = rows if layers == 2 else rows // 2
            out_shapes.append((layers, 4 * rows, cols))
            places.append(lambda o, k, h, layers=layers, rows=rows, step=step: o.at[
                h if layers == 2 else 0, pl.ds(pl.multiple_of(k * rows + (0 if layers == 2 else h * step), 16), step), :])
        elif ax == 2 and layers == 2:
            out_shapes.append((layers, rows, 4 * cols))
            places.append(lambda o, k, h, cols=cols: o.at[h, :, pl.ds(pl.multiple_of(k * cols, 128), cols)])
        else:
            out_shapes.append((4, 2, layers * rows // 2, cols))
            places.append(lambda o, k, h: o.at[k, h])
    got = _gather_chips([a.reshape(2, -1, a.shape[-1]) for a in mine], out_shapes, places, name="gather_weights")
    fw = {n: w[n] for n in SMALL if n not in SHARD_AXIS}
    for n, g, a in zip(BIG, got[:-1], mine):
        if g.shape[0] == 4:
            g = lax.dynamic_update_index_in_dim(g.reshape((4,) + a.shape), a, chip, 0)
            fw[n] = jnp.concatenate([g[k] for k in range(4)], axis=SHARD_AXIS[n])
        else:
            fw[n] = _place_own(g, a, chip.astype(jnp.int32).reshape(1), SHARD_AXIS[n], name=f"place_{n}")
    small = lax.dynamic_update_index_in_dim(got[-1].reshape((4,) + mine[-1].shape), mine[-1], chip, 0)
    parts = [_unpack(small[k], shapes) for k in range(4)]
    for i, n in enumerate(SMALL_SHARDED):
        fw[n] = jnp.concatenate([parts[k][i] for k in range(4)], axis=SHARD_AXIS[n])
    return fw


def _reduce(grads, w, chip, loss):
    stacks = []
    for n in BIG:
        cols = w[n].shape[-1]
        stacks.append(grads[n] if n in STACKED else
                      jnp.stack(jnp.split(grads[n], 4, axis=SHARD_AXIS[n])).reshape(4, 2, -1, cols))
    shapes = [grads[n].shape for n in SMALL] + [(1,)]
    small = _pack([grads[n] for n in SMALL] + [loss.reshape(1)], rows=1024).reshape(1, 2, -1, 128)
    c_vec = lax.axis_index("c").astype(jnp.int32).reshape(1)
    chip_vec = chip.astype(jnp.int32).reshape(1)
    got = _send_halves(stacks + [small], name="reduce_pair")
    sums = [_add_half(s, r, c_vec, tb=512, out_dtype=BF16, name=f"reduce_pair_sum_{n}")
            for n, s, r in zip(BIG, stacks, got)]
    sums.append(_add_half(small, got[-1], c_vec, tb=512, out_dtype=F32, name="reduce_pair_sum_small"))
    landed = _scatter_chips(sums, name="reduce_chips")
    halves = [_sum_chips(p, l, chip_vec, tb=256, name=f"reduce_chips_sum_{n}") for n, p, l in zip(BIG, sums, landed)]
    halves.append(_sum_chips_ordered(sums[-1], landed[-1], chip_vec, tb=512, name="reduce_chips_sum_small"))
    theirs = _swap_halves(halves, name="reduce_swap")
    names = BIG + ("small",)
    whole = [_join_halves(h, t, c_vec, tb=512, name=f"reduce_join_{n}") for n, h, t in zip(names, halves, theirs)]
    out = {n: j.reshape(w[n].shape) for n, j in zip(BIG, whole)}
    *parts, loss_sum = _unpack(whole[-1], shapes)
    for n, g in zip(SMALL, parts):
        if n in SHARD_AXIS:
            ax = SHARD_AXIS[n]
            size = w[n].shape[ax]
            g = lax.dynamic_slice_in_dim(g, chip * size, size, axis=ax)
        out[n] = g
    return out, loss_sum[0]


def kernel(x, p, norm_mix, norm_ffn, norm_pl, mlp_w1, mlp_w2, pl_proj, pl_gate, e_in_proj, e_out_proj, s5_lam_re, s5_lam_im, s5_log_step, s5_b_re, s5_b_im, s5_c_re, s5_c_im, s5_d, s5_glu_w, s5_glu_b, ssd_conv_w, ssd_conv_b, ssd_dt_bias, ssd_a_log, ssd_d, ssd_norm, o_in_proj, o_out_proj, rwkv_mu, rwkv_w0, rwkv_w_up, rwkv_a0, rwkv_a_up, rwkv_g_up, rwkv_k_k, rwkv_k_a, rwkv_r_k, rwkv_ln_g, rwkv_ln_b, lru_conv_w, lru_conv_b, lru_w_a, lru_b_a, lru_w_x, lru_b_x, lru_lam, norm_final, loss_target, m_norm_mix, m_norm_ffn, m_norm_pl, m_mlp_w1, m_mlp_w2, m_pl_proj, m_pl_gate, m_e_in_proj, m_e_out_proj, m_s5_lam_re, m_s5_lam_im, m_s5_log_step, m_s5_b_re, m_s5_b_im, m_s5_c_re, m_s5_c_im, m_s5_d, m_s5_glu_w, m_s5_glu_b, m_ssd_conv_w, m_ssd_conv_b, m_ssd_dt_bias, m_ssd_a_log, m_ssd_d, m_ssd_norm, m_o_in_proj, m_o_out_proj, m_rwkv_mu, m_rwkv_w0, m_rwkv_w_up, m_rwkv_a0, m_rwkv_a_up, m_rwkv_g_up, m_rwkv_k_k, m_rwkv_k_a, m_rwkv_r_k, m_rwkv_ln_g, m_rwkv_ln_b, m_lru_conv_w, m_lru_conv_b, m_lru_w_a, m_lru_b_a, m_lru_w_x, m_lru_b_x, m_lru_lam, m_norm_final, v_norm_mix, v_norm_ffn, v_norm_pl, v_mlp_w1, v_mlp_w2, v_pl_proj, v_pl_gate, v_e_in_proj, v_e_out_proj, v_s5_lam_re, v_s5_lam_im, v_s5_log_step, v_s5_b_re, v_s5_b_im, v_s5_c_re, v_s5_c_im, v_s5_d, v_s5_glu_w, v_s5_glu_b, v_ssd_conv_w, v_ssd_conv_b, v_ssd_dt_bias, v_ssd_a_log, v_ssd_d, v_ssd_norm, v_o_in_proj, v_o_out_proj, v_rwkv_mu, v_rwkv_w0, v_rwkv_w_up, v_rwkv_a0, v_rwkv_a_up, v_rwkv_g_up, v_rwkv_k_k, v_rwkv_k_a, v_rwkv_r_k, v_rwkv_ln_g, v_rwkv_ln_b, v_lru_conv_w, v_lru_conv_b, v_lru_w_a, v_lru_b_a, v_lru_w_x, v_lru_b_x, v_lru_lam, v_norm_final):
    given = dict(locals())
    w = {n: given[n] for n in WEIGHTS}
    m = {n: given["m_" + n] for n in WEIGHTS}
    v = {n: given["v_" + n] for n in WEIGHTS}
    chip = 2 * lax.axis_index("x") + lax.axis_index("y")

    fw = _gather_weights(w)
    loss, dx, grads = _local_step(x[0], p[:, 0], loss_target[0], fw)
    g, loss = _reduce(grads, w, chip, loss)

    delta, new_m, new_v = {}, {}, {}
    for n in BIG:
        delta[n], new_m[n], new_v[n] = _adamw(w[n], g[n], m[n], v[n], name=f"adamw_{n}")
    shapes = [w[n].shape for n in SMALL]
    packed = [_pack([d[n] for n in SMALL]) for d in (w, g, m, v)]
    for d, buf in zip((delta, new_m, new_v), _adamw(*packed, name="adamw_small")):
        d.update(zip(SMALL, _unpack(buf, shapes)))
    return (loss, dx[None], *[g[n] for n in WEIGHTS], *[delta[n] for n in WEIGHTS],
            *[new_m[n] for n in WEIGHTS], *[new_v[n] for n in WEIGHTS])
```

```python
import functools
import math

import jax
import jax.numpy as jnp
from jax import lax
from jax.experimental import pallas as pl
from jax.experimental.pallas import tpu as pltpu

F32 = jnp.float32
BF16 = jnp.bfloat16
HI = lax.Precision.HIGHEST
MESH = pl.DeviceIdType.MESH
SDS = jax.ShapeDtypeStruct
VMEM_LIMIT = 56 * 1024 * 1024
MM_VMEM_BUDGET = 40 * 1024 * 1024
ANY = pl.BlockSpec(memory_space=pl.ANY)

D = 2048
PL_DIM = 256
D_FF = 4 * D
EPS = 1e-6
S5_W, S5_G, S5_GROUPS, S5_P = 512, 16, 32, 64
S5_N = S5_GROUPS * S5_P
SSD_W, SSD_HD, SSD_H, SSD_NG, SSD_N, SSD_L = 1536, 64, 24, 4, 128, 128
SSD_CONV = SSD_W + 2 * SSD_NG * SSD_N
EVEN_IN = S5_W + SSD_W + SSD_CONV + SSD_H
EVEN_PAD = 5120
RW_W, RW_H, RW_HD = 1024, 16, 64
RW_LORA = 96
RW_GATE = 256
RW_IN = 3 * RW_W + 2 * RW_LORA + RW_GATE
RW_PAD = 3584
LRU_W, LRU_B = 1024, 16
ODD_IN = RW_IN + 2 * LRU_W
ODD_PAD = RW_PAD + 2 * LRU_W
GN_EPS = 64e-5
LRU_C = 8.0
ADAM_LR, ADAM_B1, ADAM_B2, ADAM_EPS, ADAM_WD, ADAM_STEP = 0.001, 0.9, 0.999, 1e-08, 0.01, 10


def _cparams(sem=("arbitrary",)):
    return pltpu.CompilerParams(dimension_semantics=sem, vmem_limit_bytes=VMEM_LIMIT)


def _dot16(a, b, dims=(((1,), (0,)), ((), ()))):
    return lax.dot_general(a.astype(BF16), b.astype(BF16), dims, preferred_element_type=F32)


NN = (((1,), (0,)), ((), ()))
NT = (((1,), (1,)), ((), ()))
TN = (((0,), (0,)), ((), ()))


def _split3(x):
    top = lambda z: lax.bitcast_convert_type(lax.bitcast_convert_type(z, jnp.int32) & jnp.int32(-65536), F32)
    hi = top(x)
    rest = x - hi
    mid = top(rest)
    return hi.astype(BF16), mid.astype(BF16), (rest - mid).astype(BF16)


def _sel_raw(a, b, dims, data):
    parts = _split3((a, b)[data].astype(F32))
    mask = (a, b)[1 - data].astype(BF16)
    acc = None
    for part in reversed(parts):
        ops = (part, mask) if data == 0 else (mask, part)
        term = lax.dot_general(*ops, dims, preferred_element_type=F32)
        acc = term if acc is None else acc + term
    return acc


_SEL_BACK = {(NN, 0): ("g", "m", NT, 0), (NT, 0): ("g", "m", NN, 0), (TN, 0): ("m", "g", NT, 1),
             (NN, 1): ("m", "g", TN, 1), (NT, 1): ("g", "m", TN, 0), (TN, 1): ("m", "g", NN, 1)}


@functools.partial(jax.custom_vjp, nondiff_argnums=(2, 3))
def _sel_dot(a, b, dims, data):
    return _sel_raw(a, b, dims, data)


def _sel_dot_fwd(a, b, dims, data):
    return _sel_raw(a, b, dims, data), (a, b)


def _sel_dot_bwd(dims, data, res, g):
    mask = res[1 - data]
    left, right, dims2, data2 = _SEL_BACK[(dims, data)]
    grad = _sel_raw(g if left == "g" else mask, g if right == "g" else mask, dims2, data2)
    zero = jnp.zeros_like(mask)
    return (grad, zero) if data == 0 else (zero, grad)


_sel_dot.defvjp(_sel_dot_fwd, _sel_dot_bwd)


def _tile(dim, target):
    if dim <= target:
        return dim
    t = target - target % 128
    while t > 128 and dim % t:
        t -= 128
    assert dim % t == 0, (dim, target)
    return t


def _mm(a, b, *, ta=False, tb=False, add=None, out_dtype=F32, tm=1024, tn=1024, tk=1024, name,
        epilogue=None, extra=(), out_dtypes=None):
    layer = None
    if isinstance(b, tuple):
        b, layer = b
    m, k = (a.shape[1], a.shape[0]) if ta else a.shape
    n = b.shape[-2] if tb else b.shape[-1]
    assert (b.shape[-1] if tb else b.shape[-2]) == k, (a.shape, b.shape, ta, tb)
    ins = [a, b] + ([add] if add is not None else []) + list(extra)
    out_dtypes = out_dtypes or [out_dtype]
    n_in, n_out = len(ins), len(out_dtypes)
    tm, tn = _tile(m, tm), _tile(n, tn)
    tiles = 2 * tm * tn * sum(jnp.dtype(x.dtype).itemsize for x in ins[2:]) + 2 * tm * tn * sum(
        jnp.dtype(dt).itemsize for dt in out_dtypes) + 4 * tm * tn
    per_k = 2 * (tm * a.dtype.itemsize + tn * b.dtype.itemsize)
    tk = _tile(k, max(tk, min(2048, (MM_VMEM_BUDGET - tiles) // per_k // 128 * 128)))
    nk = k // tk
    dims = (((0 if ta else 1,), (1 if tb else 0,)), ((), ()))

    def finish(acc, refs):
        res = epilogue(acc, *[r[...] for r in refs[n_in - len(extra):n_in]]) if epilogue else (acc,)
        for o_ref, val in zip(refs[n_in:n_in + n_out], res):
            o_ref[...] = val.astype(o_ref.dtype)

    def body(*refs):
        a_ref, b_ref, acc_ref = refs[0], refs[1], refs[-1]
        if nk == 1:
            acc = _dot16(a_ref[...], b_ref[...], dims)
            finish(acc + refs[2][...].astype(F32) if add is not None else acc, refs)
            return
        kk = pl.program_id(2)

        @pl.when(kk == 0)
        def _():
            acc_ref[...] = refs[2][...].astype(F32) if add is not None else jnp.zeros_like(acc_ref)

        acc_ref[...] += _dot16(a_ref[...], b_ref[...], dims)

        @pl.when(kk == nk - 1)
        def _():
            finish(acc_ref[...], refs)

    a_spec = pl.BlockSpec((tk, tm), lambda i, j, q: (q, i)) if ta else pl.BlockSpec((tm, tk), lambda i, j, q: (i, q))
    b_spec = pl.BlockSpec((tn, tk), lambda i, j, q: (j, q)) if tb else pl.BlockSpec((tk, tn), lambda i, j, q: (q, j))
    if layer is not None:
        b_spec = (pl.BlockSpec((None, tn, tk), lambda i, j, q: (layer, j, q)) if tb
                  else pl.BlockSpec((None, tk, tn), lambda i, j, q: (layer, q, j)))
    o_spec = pl.BlockSpec((tm, tn), lambda i, j, q: (i, j))
    outs = pl.pallas_call(
        body,
        grid=(m // tm, n // tn, nk),
        in_specs=[a_spec, b_spec] + [o_spec] * (n_in - 2),
        out_specs=[o_spec] * n_out,
        out_shape=[SDS((m, n), dt) for dt in out_dtypes],
        scratch_shapes=[pltpu.VMEM((tm, tn) if nk > 1 else (8, 128), F32)],
        compiler_params=_cparams(("parallel", "parallel", "arbitrary")),
        name=name,
    )(*ins)
    return outs if epilogue else outs[0]


def _mm_grad(x, dy, *, layer, cols_cut, shard, prev, name):
    t = x.shape[0]
    r, c = shard
    tm, tn = _tile(r, 1024), _tile(c, 1024)
    per_k = 2 * (tm * x.dtype.itemsize + tn * dy.dtype.itemsize)
    tk = _tile(t, max(1024, min(2048, (MM_VMEM_BUDGET - 12 * tm * tn) // per_k // 128 * 128)))
    nk = t // tk
    if cols_cut:
        assert x.shape[1] == r and dy.shape[1] == 4 * c
        per = c // tn
        omap = lambda i, j, q: (j // per, layer, i, j % per)
    else:
        assert x.shape[1] == 4 * r and dy.shape[1] == c
        per = r // tm
        omap = lambda i, j, q: (i // per, layer, i % per, j)

    def body(*refs):
        x_ref, dy_ref = refs[:2]
        o_ref, acc_ref = refs[-2:]
        kk = pl.program_id(2)

        @pl.when(kk == 0)
        def _():
            acc_ref[...] = jnp.zeros_like(acc_ref)

        acc_ref[...] += _dot16(x_ref[...], dy_ref[...], TN)

        @pl.when(kk == nk - 1)
        def _():
            o_ref[...] = acc_ref[...]

    return pl.pallas_call(
        body,
        grid=(x.shape[1] // tm, dy.shape[1] // tn, nk),
        in_specs=[pl.BlockSpec((tk, tm), lambda i, j, q: (q, i)), pl.BlockSpec((tk, tn), lambda i, j, q: (q, j))]
        + ([ANY] if prev is not None else []),
        out_specs=pl.BlockSpec((None, None, tm, tn), omap),
        out_shape=SDS((4, 2, r, c), F32),
        scratch_shapes=[pltpu.VMEM((tm, tn), F32)],
        input_output_aliases={2: 0} if prev is not None else {},
        compiler_params=_cparams(("parallel", "parallel", "arbitrary")),
        name=name,
    )(x, dy, *([prev] if prev is not None else []))


def _single(fn, consts, *, name):
    outs = jax.eval_shape(fn, *[SDS(c.shape, F32) for c in consts])
    n_in = len(consts)

    def body(*refs):
        res = fn(*[r[...] for r in refs[:n_in]])
        for o_ref, v in zip(refs[n_in:], res):
            o_ref[...] = v

    return pl.pallas_call(body, out_shape=[SDS(o.shape, F32) for o in outs],
                          compiler_params=pltpu.CompilerParams(vmem_limit_bytes=VMEM_LIMIT), name=name)(*consts)


def _single_vjp(fn, consts, cots, *, name):
    n_in = len(consts)

    def body(*refs):
        _, pull = jax.vjp(fn, *[r[...] for r in refs[:n_in]])
        grads = pull(tuple(r[...] for r in refs[n_in:n_in + len(cots)]))
        for o_ref, v in zip(refs[n_in + len(cots):], grads):
            o_ref[...] = v

    return pl.pallas_call(body, out_shape=[SDS(c.shape, F32) for c in consts],
                          compiler_params=pltpu.CompilerParams(vmem_limit_bytes=VMEM_LIMIT), name=name)(*consts, *cots)


def _full_spec(shape):
    nd = len(shape)
    return pl.BlockSpec(shape, lambda i, _n=nd: (0,) * _n)


def _stage_shapes(fn, rows, consts, tb, pos):
    rs = [SDS((tb, r.shape[1]), F32) for r in rows]
    cs = [SDS(c.shape, F32) for c in consts]
    f = (lambda *a: fn(jnp.int32(0), *a)) if pos else fn
    return jax.eval_shape(f, *rs, *cs)


def _stage(fn, rows, consts, *, tb, name, out_dtypes, n_acc=0, pos=False):
    t = rows[0].shape[0]
    assert t % tb == 0
    outs = _stage_shapes(fn, rows, consts, tb, pos)
    n_out = len(outs)
    n_row = n_out - n_acc
    n_in = len(rows) + len(consts)

    def body(*refs):
        i = pl.program_id(0)
        vals = [r[...].astype(F32) for r in refs[:n_in]]
        res = fn(i * tb, *vals) if pos else fn(*vals)
        out_refs = refs[n_in:]
        for q in range(n_row):
            out_refs[q][...] = res[q].astype(out_refs[q].dtype)
        for q in range(n_row, n_out):
            @pl.when(i == 0)
            def _(q=q):
                out_refs[q][...] = jnp.zeros_like(out_refs[q])

            out_refs[q][...] += res[q]

    in_specs = [pl.BlockSpec((tb, r.shape[1]), lambda i: (i, 0)) for r in rows] + [_full_spec(c.shape) for c in consts]
    out_specs = [pl.BlockSpec((tb, o.shape[1]), lambda i: (i, 0)) for o in outs[:n_row]] + [_full_spec(o.shape) for o in outs[n_row:]]
    out_shape = [SDS((t, o.shape[1]), dt) for o, dt in zip(outs[:n_row], out_dtypes)] + [SDS(o.shape, F32) for o in outs[n_row:]]
    return pl.pallas_call(
        body, grid=(t // tb,), in_specs=in_specs, out_specs=out_specs, out_shape=out_shape,
        compiler_params=_cparams(), name=name,
    )(*rows, *consts)


def _stage_vjp(fn, rows, consts, cots, *, tb, name, drow, dconst, drow_dtypes=None, acc_cots=(), pos=False):
    t = rows[0].shape[0]
    assert t % tb == 0
    n_rows, n_consts, n_cots, n_acc = len(rows), len(consts), len(cots), len(acc_cots)
    n_in = n_rows + n_consts + n_cots + n_acc
    drow_dtypes = drow_dtypes or [F32] * len(drow)

    def body(*refs):
        i = pl.program_id(0)
        vals = [r[...].astype(F32) for r in refs[:n_in]]
        rv, cv = vals[:n_rows], vals[n_rows:n_rows + n_consts]
        ct = tuple(vals[n_rows + n_consts:])

        def f(*dargs):
            r2, c2 = list(rv), list(cv)
            for q, idx in enumerate(drow):
                r2[idx] = dargs[q]
            for q, idx in enumerate(dconst):
                c2[idx] = dargs[len(drow) + q]
            return fn(i * tb, *r2, *c2) if pos else fn(*r2, *c2)

        _, pull = jax.vjp(f, *[rv[q] for q in drow], *[cv[q] for q in dconst])
        grads = pull(ct)
        out_refs = refs[n_in:]
        for q in range(len(drow)):
            out_refs[q][...] = grads[q].astype(out_refs[q].dtype)
        for q in range(len(drow), len(drow) + len(dconst)):
            @pl.when(i == 0)
            def _(q=q):
                out_refs[q][...] = jnp.zeros_like(out_refs[q])

            out_refs[q][...] += grads[q]

    in_specs = ([pl.BlockSpec((tb, r.shape[1]), lambda i: (i, 0)) for r in rows] + [_full_spec(c.shape) for c in consts]
                + [pl.BlockSpec((tb, c.shape[1]), lambda i: (i, 0)) for c in cots] + [_full_spec(c.shape) for c in acc_cots])
    out_specs = ([pl.BlockSpec((tb, rows[q].shape[1]), lambda i: (i, 0)) for q in drow]
                 + [_full_spec(consts[q].shape) for q in dconst])
    out_shape = ([SDS(rows[q].shape, dt) for q, dt in zip(drow, drow_dtypes)]
                 + [SDS(consts[q].shape, F32) for q in dconst])
    return pl.pallas_call(
        body, grid=(t // tb,), in_specs=in_specs, out_specs=out_specs, out_shape=out_shape,
        compiler_params=_cparams(), name=name,
    )(*rows, *consts, *cots, *acc_cots)


def _conv_fwd(x, w, b, *, tb, name):
    t, c = x.shape
    r8 = tb // 8

    def body(x_ref, p_ref, w_ref, b_ref, o_ref):
        i = pl.program_id(0)
        x_ = x_ref[...]
        p_ = jnp.where(i > 0, p_ref[...], 0.0)
        w_ = w_ref[...]
        row = lax.broadcasted_iota(jnp.int32, x_.shape, 0)
        row8 = lax.broadcasted_iota(jnp.int32, p_.shape, 0)
        acc = x_ * w_[3:4, :] + b_ref[...]
        head = jnp.zeros_like(p_)
        for j in (1, 2, 3):
            wj = w_[3 - j:4 - j, :]
            acc += jnp.where(row >= j, pltpu.roll(x_, j, 0), 0.0) * wj
            head += jnp.where(row8 < j, pltpu.roll(p_, j, 0), 0.0) * wj
        o_ref[...] = acc
        o_ref[0:8, :] += head

    return pl.pallas_call(
        body, grid=(t // tb,),
        in_specs=[pl.BlockSpec((tb, c), lambda i: (i, 0)),
                  pl.BlockSpec((8, c), lambda i: (jnp.maximum(i * r8 - 1, 0), 0)),
                  _full_spec(w.shape), _full_spec(b.shape)],
        out_specs=pl.BlockSpec((tb, c), lambda i: (i, 0)),
        out_shape=SDS((t, c), F32), compiler_params=_cparams(), name=name,
    )(x, x, w, b)


def _conv_bwd(x, w, dy, *, tb, name):
    t, c = x.shape
    r8 = tb // 8
    nb = t // tb

    def body(x_ref, p_ref, w_ref, g_ref, n_ref, dx_ref, dw_ref, db_ref):
        i = pl.program_id(0)
        x_ = x_ref[...]
        p_ = jnp.where(i > 0, p_ref[...], 0.0)
        g_ = g_ref[...]
        n_ = jnp.where(i < nb - 1, n_ref[...], 0.0)
        w_ = w_ref[...]
        row = lax.broadcasted_iota(jnp.int32, x_.shape, 0)
        row8 = lax.broadcasted_iota(jnp.int32, p_.shape, 0)
        g8 = g_[0:8, :]
        dx = g_ * w_[3:4, :]
        tail = jnp.zeros_like(n_)
        dws = [jnp.sum(g_ * x_, axis=0, keepdims=True)]
        for j in (1, 2, 3):
            wj = w_[3 - j:4 - j, :]
            dx += jnp.where(row < tb - j, pltpu.roll(g_, tb - j, 0), 0.0) * wj
            tail += jnp.where(row8 >= 8 - j, pltpu.roll(n_, 8 - j, 0), 0.0) * wj
            xs = jnp.where(row >= j, pltpu.roll(x_, j, 0), 0.0)
            ps = jnp.where(row8 < j, pltpu.roll(p_, j, 0), 0.0)
            dws.append(jnp.sum(g_ * xs, axis=0, keepdims=True) + jnp.sum(g8 * ps, axis=0, keepdims=True))
        dx_ref[...] = dx
        dx_ref[tb - 8:tb, :] += tail

        @pl.when(i == 0)
        def _():
            dw_ref[...] = jnp.zeros_like(dw_ref)
            db_ref[...] = jnp.zeros_like(db_ref)

        for j in range(4):
            dw_ref[3 - j:4 - j, :] += dws[j]
        db_ref[...] += jnp.sum(g_, axis=0, keepdims=True)

    return pl.pallas_call(
        body, grid=(nb,),
        in_specs=[pl.BlockSpec((tb, c), lambda i: (i, 0)),
                  pl.BlockSpec((8, c), lambda i: (jnp.maximum(i * r8 - 1, 0), 0)),
                  _full_spec(w.shape),
                  pl.BlockSpec((tb, c), lambda i: (i, 0)),
                  pl.BlockSpec((8, c), lambda i: (jnp.minimum((i + 1) * r8, t // 8 - 1), 0))],
        out_specs=[pl.BlockSpec((tb, c), lambda i: (i, 0)), _full_spec((8, c)), _full_spec((1, c))],
        out_shape=[SDS((t, c), F32), SDS((8, c), F32), SDS((1, c), F32)],
        compiler_params=_cparams(), name=name,
    )(x, x, w, dy, dy)


def _lru_scan_fwd(a, b, *, tb, name):
    t, c = a.shape

    def body(a_ref, b_ref, h_ref, st_ref):
        @pl.when(pl.program_id(0) == 0)
        def _():
            st_ref[...] = jnp.zeros_like(st_ref)

        def step(s, h):
            h = a_ref[pl.ds(s, 1), :] * h + b_ref[pl.ds(s, 1), :]
            h_ref[pl.ds(s, 1), :] = h
            return h

        st_ref[...] = lax.fori_loop(0, tb, step, st_ref[...], unroll=8)

    blk = pl.BlockSpec((tb, c), lambda i: (i, 0))
    return pl.pallas_call(
        body, grid=(t // tb,), in_specs=[blk, blk], out_specs=blk, out_shape=SDS((t, c), F32),
        scratch_shapes=[pltpu.VMEM((1, c), F32)], compiler_params=_cparams(), name=name,
    )(a, b)


def _lru_scan_bwd(a, h, dh, *, tb, name):
    t, c = a.shape
    nb = t // tb
    r8 = tb // 8

    def body(a_ref, h_ref, p_ref, g_ref, da_ref, db_ref, st_ref):
        i = pl.program_id(0)

        @pl.when(i == 0)
        def _():
            st_ref[...] = jnp.zeros_like(st_ref)

        hprev0 = jnp.where(i < nb - 1, p_ref[7:8, :], 0.0)

        def step(q, carry):
            s = tb - 1 - q
            g = g_ref[pl.ds(s, 1), :] + carry
            hp = h_ref[pl.ds(jnp.maximum(s - 1, 0), 1), :]
            hp = jnp.where(s > 0, hp, hprev0)
            db_ref[pl.ds(s, 1), :] = g
            da_ref[pl.ds(s, 1), :] = g * hp
            return a_ref[pl.ds(s, 1), :] * g

        st_ref[...] = lax.fori_loop(0, tb, step, st_ref[...], unroll=8)

    rev = pl.BlockSpec((tb, c), lambda i: (nb - 1 - i, 0))
    prev = pl.BlockSpec((8, c), lambda i: (jnp.maximum((nb - 1 - i) * r8 - 1, 0), 0))
    return pl.pallas_call(
        body, grid=(nb,), in_specs=[rev, rev, prev, rev], out_specs=[rev, rev],
        out_shape=[SDS((t, c), F32), SDS((t, c), F32)],
        scratch_shapes=[pltpu.VMEM((1, c), F32)], compiler_params=_cparams(), name=name,
    )(a, h, h, dh)


def _s5_scan_fwd(ar, ai, br, bi, *, tb, name):
    t, c = br.shape

    def body(ar_ref, ai_ref, br_ref, bi_ref, xr_ref, xi_ref, sr_ref, si_ref):
        @pl.when(pl.program_id(0) == 0)
        def _():
            sr_ref[...] = jnp.zeros_like(sr_ref)
            si_ref[...] = jnp.zeros_like(si_ref)

        ar_, ai_ = ar_ref[...], ai_ref[...]

        def step(s, carry):
            xr, xi = carry
            nr = ar_ * xr - ai_ * xi + br_ref[pl.ds(s, 1), :]
            ni = ar_ * xi + ai_ * xr + bi_ref[pl.ds(s, 1), :]
            xr_ref[pl.ds(s, 1), :] = nr
            xi_ref[pl.ds(s, 1), :] = ni
            return nr, ni

        xr, xi = lax.fori_loop(0, tb, step, (sr_ref[...], si_ref[...]), unroll=8)
        sr_ref[...] = xr
        si_ref[...] = xi

    blk = pl.BlockSpec((tb, c), lambda i: (i, 0))
    one = _full_spec((1, c))
    return pl.pallas_call(
        body, grid=(t // tb,), in_specs=[one, one, blk, blk], out_specs=[blk, blk],
        out_shape=[SDS((t, c), F32), SDS((t, c), F32)],
        scratch_shapes=[pltpu.VMEM((1, c), F32), pltpu.VMEM((1, c), F32)], compiler_params=_cparams(), name=name,
    )(ar, ai, br, bi)


def _s5_scan_bwd(ar, ai, xr, xi, dxr, dxi, *, tb, name):
    t, c = xr.shape
    nb = t // tb
    r8 = tb // 8

    def body(ar_ref, ai_ref, xr_ref, xi_ref, pr_ref, pi_ref, gr_ref, gi_ref,
             dbr_ref, dbi_ref, dar_ref, dai_ref, cr_ref, ci_ref):
        i = pl.program_id(0)

        @pl.when(i == 0)
        def _():
            cr_ref[...] = jnp.zeros_like(cr_ref)
            ci_ref[...] = jnp.zeros_like(ci_ref)
            dar_ref[...] = jnp.zeros_like(dar_ref)
            dai_ref[...] = jnp.zeros_like(dai_ref)

        ar_, ai_ = ar_ref[...], ai_ref[...]
        first = i == nb - 1
        pr0 = jnp.where(first, 0.0, pr_ref[7:8, :])
        pi0 = jnp.where(first, 0.0, pi_ref[7:8, :])

        def step(q, carry):
            cr, ci, dar, dai = carry
            s = tb - 1 - q
            gr = gr_ref[pl.ds(s, 1), :] + cr
            gi = gi_ref[pl.ds(s, 1), :] + ci
            sp = jnp.maximum(s - 1, 0)
            xpr = jnp.where(s > 0, xr_ref[pl.ds(sp, 1), :], pr0)
            xpi = jnp.where(s > 0, xi_ref[pl.ds(sp, 1), :], pi0)
            dbr_ref[pl.ds(s, 1), :] = gr
            dbi_ref[pl.ds(s, 1), :] = gi
            dar = dar + gr * xpr + gi * xpi
            dai = dai - gr * xpi + gi * xpr
            return ar_ * gr + ai_ * gi, ar_ * gi - ai_ * gr, dar, dai

        cr, ci, dar, dai = lax.fori_loop(0, tb, step, (cr_ref[...], ci_ref[...], dar_ref[...], dai_ref[...]), unroll=8)
        cr_ref[...] = cr
        ci_ref[...] = ci
        dar_ref[...] = dar
        dai_ref[...] = dai

    rev = pl.BlockSpec((tb, c), lambda i: (nb - 1 - i, 0))
    prev = pl.BlockSpec((8, c), lambda i: (jnp.maximum((nb - 1 - i) * r8 - 1, 0), 0))
    one = _full_spec((1, c))
    return pl.pallas_call(
        body, grid=(nb,), in_specs=[one, one, rev, rev, prev, prev, rev, rev], out_specs=[rev, rev, one, one],
        out_shape=[SDS((t, c), F32), SDS((t, c), F32), SDS((1, c), F32), SDS((1, c), F32)],
        scratch_shapes=[pltpu.VMEM((1, c), F32), pltpu.VMEM((1, c), F32)], compiler_params=_cparams(), name=name,
    )(ar, ai, xr, xi, xr, xi, dxr, dxi)


RW_PAIRS = RW_H // 2


def _pair_consts():
    sub = lax.broadcasted_iota(jnp.int32, (64, 128), 0)
    lane = lax.broadcasted_iota(jnp.int32, (64, 128), 1)
    eye2 = ((lane & 63) == sub).astype(F32)
    r2 = lax.broadcasted_iota(jnp.int32, (128, 128), 0)
    c2 = lax.broadcasted_iota(jnp.int32, (128, 128), 1)
    bsel = ((r2 >> 6) == (c2 >> 6)).astype(BF16)
    return eye2, bsel


def _segsum(x, bsel):
    rows = x.shape[0]
    bits = lax.bitcast_convert_type(x, jnp.int32)
    hi = lax.bitcast_convert_type(bits & jnp.int32(-65536), F32)
    both = jnp.concatenate([hi.astype(BF16), (x - hi).astype(BF16)], axis=0)
    res = jnp.dot(both, bsel, preferred_element_type=F32)
    return res[:rows] + res[rows:]


def _bc(x8):
    return jnp.stack([jnp.broadcast_to(x8[q:q + 1, :], (64, 128)) for q in range(RW_PAIRS)])


def _seg3(x3, bsel):
    return _segsum(x3.reshape(RW_PAIRS * 64, 128), bsel).reshape(RW_PAIRS, 64, 128)


def _seg3_lanes(x3):
    first = lax.broadcasted_iota(jnp.int32, x3.shape, 2) < 64
    lo = jnp.sum(jnp.where(first, x3, 0.0), axis=-1, keepdims=True)
    hi = jnp.sum(jnp.where(first, 0.0, x3), axis=-1, keepdims=True)
    return jnp.where(first, lo, hi)


def _rwkv_scan_fwd(r, w, k, v, kk, a, *, lc, name):
    t = r.shape[0]
    nc = t // lc

    def body(r_ref, w_ref, k_ref, v_ref, kk_ref, a_ref, y_ref, ck_ref, hist_ref, st_ref):
        @pl.when(pl.program_id(0) == 0)
        def _():
            st_ref[...] = jnp.zeros_like(st_ref)

        ck_ref[0] = st_ref[...]
        eye2, bsel = _pair_consts()
        column = lambda ref, s: _seg3(eye2[None] * _bc(ref[s]), bsel)
        read = lambda st, s: jnp.sum(eye2[None] * _seg3(st * _bc(r_ref[s]), bsel), axis=1)

        def step(s, carry):
            st, vb = carry
            kk8 = kk_ref[s]
            sa = -_seg3_lanes(st * _bc(kk8))
            vb_next = column(v_ref, jnp.minimum(s + 1, lc - 1))
            before = jnp.maximum(s - 1, 0)
            y_ref[before] = read(st, before)
            st = st * _bc(w_ref[s]) + sa * _bc(kk8 * a_ref[s]) + vb * _bc(k_ref[s])
            hist_ref[s] = st
            return st, vb_next

        st, _ = lax.fori_loop(0, lc, step, (st_ref[...], column(v_ref, 0)))
        y_ref[lc - 1] = read(st, lc - 1)
        st_ref[...] = st

    blk = pl.BlockSpec((lc, RW_PAIRS, 128), lambda i: (i, 0, 0))
    return pl.pallas_call(
        body, grid=(nc,), in_specs=[blk] * 6,
        out_specs=[blk, pl.BlockSpec((1, RW_PAIRS, 64, 128), lambda i: (i, 0, 0, 0)),
                   pl.BlockSpec((lc, RW_PAIRS, 64, 128), lambda i: (i, 0, 0, 0))],
        out_shape=[SDS((t, RW_PAIRS, 128), F32), SDS((nc, RW_PAIRS, 64, 128), F32), SDS((t, RW_PAIRS, 64, 128), F32)],
        scratch_shapes=[pltpu.VMEM((RW_PAIRS, 64, 128), F32)],
        compiler_params=_cparams(), name=name,
    )(r, w, k, v, kk, a)


def _rwkv_scan_bwd(r, w, k, v, kk, a, ck, hist, dy, *, lc, name):
    t = r.shape[0]
    nc = t // lc

    def body(r_ref, w_ref, k_ref, v_ref, kk_ref, a_ref, ck_ref, hist_ref, dy_ref,
             dr_ref, dw_ref, dk_ref, dv_ref, dkk_ref, da_ref, ds_ref):
        @pl.when(pl.program_id(0) == 0)
        def _():
            ds_ref[...] = jnp.zeros_like(ds_ref)

        eye2, bsel = _pair_consts()
        column = lambda ref, s: _seg3(eye2[None] * _bc(ref[s]), bsel)

        col = lambda z: jnp.sum(z, axis=1)

        def grads(s, s_prev, d_s, dsa):
            kk8 = kk_ref[s]
            sa = -_seg3(s_prev * _bc(kk8), bsel)
            db = col(d_s * sa)
            dw_ref[s] = col(d_s * s_prev)
            dv_ref[s] = col(eye2[None] * _seg3(d_s * _bc(k_ref[s]), bsel))
            dk_ref[s] = col(d_s * column(v_ref, s))
            dkk_ref[s] = db * a_ref[s] - col(s_prev * dsa)
            da_ref[s] = db * kk8

        def back(j, carry):
            ds, d_after, dsa_after, dyb = carry
            s = lc - 1 - j
            kk8 = kk_ref[s]
            d_s = ds + dyb * _bc(r_ref[s])
            dsa = _seg3_lanes(d_s * _bc(kk8 * a_ref[s]))
            dr_ref[s] = col(hist_ref[s] * dyb)
            dyb_before = column(dy_ref, jnp.maximum(s - 1, 0))
            after = jnp.minimum(s + 1, lc - 1)
            grads(after, hist_ref[after - 1], d_after, dsa_after)
            return d_s * _bc(w_ref[s]) - dsa * _bc(kk8), d_s, dsa, dyb_before

        zero = jnp.zeros((RW_PAIRS, 64, 128), F32)
        ds, d_first, dsa_first, _ = lax.fori_loop(0, lc, back, (ds_ref[...], zero, zero, column(dy_ref, lc - 1)))
        grads(0, ck_ref[0], d_first, dsa_first)
        ds_ref[...] = ds

    rev = pl.BlockSpec((lc, RW_PAIRS, 128), lambda i: (nc - 1 - i, 0, 0))
    return pl.pallas_call(
        body, grid=(nc,),
        in_specs=[rev] * 6 + [pl.BlockSpec((1, RW_PAIRS, 64, 128), lambda i: (nc - 1 - i, 0, 0, 0)),
                              pl.BlockSpec((lc, RW_PAIRS, 64, 128), lambda i: (nc - 1 - i, 0, 0, 0)), rev],
        out_specs=[rev] * 6, out_shape=[SDS((t, RW_PAIRS, 128), F32)] * 6,
        scratch_shapes=[pltpu.VMEM((RW_PAIRS, 64, 128), F32)],
        compiler_params=_cparams(), name=name,
    )(r, w, k, v, kk, a, ck, hist, dy)


SSD_PAIRS = SSD_H // 2


def _ssd_chunk(states, xdt, da, bm, cm):
    ln = SSD_L
    row = lax.broadcasted_iota(jnp.int32, (ln, ln), 0)
    col = lax.broadcasted_iota(jnp.int32, (ln, ln), 1)
    causal = row >= col
    acum = _sel_dot(causal.astype(F32), da, NN, 1)
    acum_t = _sel_dot(da, (row <= col).astype(F32), TN, 0)
    sub = lax.broadcasted_iota(jnp.int32, (128, 128), 0)
    lane = lax.broadcasted_iota(jnp.int32, (128, 128), 1)
    ys, new_states = [], []
    for q in range(SSD_PAIRS):
        g = q // (SSD_PAIRS // SSD_NG)
        bg = bm[:, g * SSD_N:(g + 1) * SSD_N]
        cg = cm[:, g * SSD_N:(g + 1) * SSD_N]
        xq = xdt[:, q * 128:(q + 1) * 128]
        scores = _dot16(cg, bg, NT)
        aexp = _sel_dot(acum, (sub == 2 * q + (lane >> 6)).astype(F32), NN, 0)
        tot = aexp[ln - 1:ln, :]
        yh = []
        for h in (2 * q, 2 * q + 1):
            seg = _sel_dot(acum, (sub == h).astype(F32), NN, 0) - acum_t[h:h + 1, :]
            yh.append(_dot16(scores * jnp.exp(jnp.where(causal, seg, -1e30)), xq))
        y = jnp.where(lane < 64, yh[0], yh[1]) + _dot16(cg, states[q]) * jnp.exp(aexp)
        new = _dot16(bg, xq * jnp.exp(tot - aexp), TN)
        ys.append(y)
        new_states.append(states[q] * jnp.exp(tot) + new)
    return jnp.concatenate(ys, axis=1), new_states


def _ssd_fwd(xdt, da, bm, cm, *, name):
    t = xdt.shape[0]
    nc = t // SSD_L

    def body(x_ref, a_ref, b_ref, c_ref, y_ref, ck_ref, st_ref):
        @pl.when(pl.program_id(0) == 0)
        def _():
            st_ref[...] = jnp.zeros_like(st_ref)

        ck_ref[0] = st_ref[...]
        y, new = _ssd_chunk([st_ref[q] for q in range(SSD_PAIRS)], x_ref[...], a_ref[...], b_ref[...], c_ref[...])
        y_ref[...] = y
        for q in range(SSD_PAIRS):
            st_ref[q] = new[q]

    blk = lambda wd: pl.BlockSpec((SSD_L, wd), lambda i: (i, 0))
    return pl.pallas_call(
        body, grid=(nc,), in_specs=[blk(SSD_W), blk(128), blk(512), blk(512)],
        out_specs=[blk(SSD_W), pl.BlockSpec((1, SSD_PAIRS, 128, 128), lambda i: (i, 0, 0, 0))],
        out_shape=[SDS((t, SSD_W), F32), SDS((nc, SSD_PAIRS, 128, 128), F32)],
        scratch_shapes=[pltpu.VMEM((SSD_PAIRS, 128, 128), F32)], compiler_params=_cparams(), name=name,
    )(xdt, da, bm, cm)


def _ssd_bwd(xdt, da, bm, cm, ck, dy, *, name):
    t = xdt.shape[0]
    nc = t // SSD_L

    def body(x_ref, a_ref, b_ref, c_ref, ck_ref, dy_ref, dx_ref, dda_ref, db_ref, dc_ref, ds_ref):
        @pl.when(pl.program_id(0) == 0)
        def _():
            ds_ref[...] = jnp.zeros_like(ds_ref)

        _, pull = jax.vjp(_ssd_chunk, [ck_ref[0, q] for q in range(SSD_PAIRS)], x_ref[...], a_ref[...], b_ref[...], c_ref[...])
        dst, dx, dda, db, dc = pull((dy_ref[...], [ds_ref[q] for q in range(SSD_PAIRS)]))
        dx_ref[...] = dx
        dda_ref[...] = dda
        db_ref[...] = db
        dc_ref[...] = dc
        for q in range(SSD_PAIRS):
            ds_ref[q] = dst[q]

    rev = lambda wd: pl.BlockSpec((SSD_L, wd), lambda i: (nc - 1 - i, 0))
    return pl.pallas_call(
        body, grid=(nc,),
        in_specs=[rev(SSD_W), rev(128), rev(512), rev(512),
                  pl.BlockSpec((1, SSD_PAIRS, 128, 128), lambda i: (nc - 1 - i, 0, 0, 0)), rev(SSD_W)],
        out_specs=[rev(SSD_W), rev(128), rev(512), rev(512)],
        out_shape=[SDS((t, SSD_W), F32), SDS((t, 128), F32), SDS((t, 512), F32), SDS((t, 512), F32)],
        scratch_shapes=[pltpu.VMEM((SSD_PAIRS, 128, 128), F32)], compiler_params=_cparams(), name=name,
    )(xdt, da, bm, cm, ck, dy)


def _iota(shape, dim):
    return lax.broadcasted_iota(jnp.int32, shape, dim)


def _rms(x, g):
    return x * lax.rsqrt(jnp.mean(x * x, axis=-1, keepdims=True) + EPS) * g


def _head_sel(width, shift):
    return ((_iota((width, 128), 0) >> shift) == _iota((width, 128), 1)).astype(F32)


def _head_sum(x, shift=6):
    sel = _head_sel(x.shape[1], shift)
    return _sel_dot(_sel_dot(x, sel, NN, 0), sel, NT, 0)


def _head_expand(x, width, shift=6):
    return _sel_dot(x, _head_sel(width, shift), NT, 0)


def f_norm(h, g):
    return (_rms(h, g),)


def f_norm_pass(h, g):
    return _rms(h, g), h


def f_add_norm(h, m, g):
    h1 = h + m
    return h1, _rms(h1, g)


def f_relu2(u):
    r = jnp.maximum(u, 0.0)
    return (r * r,)


def f_plgate(h2, gl, pp):
    return (h2 + jax.nn.sigmoid(gl) * pp,)


def f_loss(h, tgt, g):
    err = _rms(h, g) - tgt
    part = 0.5 * jnp.sum(jnp.mean(err * err, axis=-1, keepdims=True), axis=0, keepdims=True)
    return (jnp.broadcast_to(part, (8, 128)),)


def f_s5_prep(lam_re, lam_im, lstep, bre_t, bim_t, cre_t, cim_t):
    step = jnp.exp(_sel_dot(lstep, _head_sel(S5_N, 6), NT, 0)[0:1, :])
    mag = jnp.exp(lam_re * step)
    abar_re, abar_im = mag * jnp.cos(lam_im * step), mag * jnp.sin(lam_im * step)
    den = lam_re * lam_re + lam_im * lam_im
    nr = abar_re - 1.0
    coef_re = (nr * lam_re + abar_im * lam_im) / den
    coef_im = (abar_im * lam_re - nr * lam_im) / den
    bbar_re = coef_re * bre_t - coef_im * bim_t
    bbar_im = coef_re * bim_t + coef_im * bre_t
    rep = ((_iota((S5_W, S5_G), 0) & (S5_G - 1)) == _iota((S5_W, S5_G), 1)).astype(F32)
    blk = ((_iota((S5_W, S5_N), 0) >> 4) == (_iota((S5_W, S5_N), 1) >> 6)).astype(F32)
    blk_t = ((_iota((S5_N, S5_W), 0) >> 6) == (_iota((S5_N, S5_W), 1) >> 4)).astype(F32)
    wb_re, wb_im = _sel_dot(rep, bbar_re, NN, 1) * blk, _sel_dot(rep, bbar_im, NN, 1) * blk
    wc_re, wc_im = _sel_dot(cre_t, rep, NT, 0) * blk_t, _sel_dot(cim_t, rep, NT, 0) * blk_t
    return abar_re, abar_im, wb_re, wb_im, wc_re, wc_im


def f_s5_post(xr, xi, u, wc_re, wc_im, d_skip, glu_w, glu_b):
    y = _dot16(xr, wc_re) - _dot16(xi, wc_im) + d_skip * u
    act = jax.nn.gelu(y)
    return (act * jax.nn.sigmoid(_dot16(act, glu_w) + glu_b),)


def f_ssd_pre(xc, dtr, dt_bias, a_log):
    act = jax.nn.silu(xc)
    heads = _iota(dtr.shape, 1) < SSD_H
    dt = jnp.where(heads, jax.nn.softplus(dtr + dt_bias), 0.0)
    da = dt * (-jnp.exp(a_log))
    xdt = act[:, :SSD_W] * _head_expand(dt, SSD_W)
    return xdt, da, act[:, SSD_W:SSD_W + 512], act[:, SSD_W + 512:]


def f_ssd_pre_pass(xc, dtr, dt_bias, a_log):
    return f_ssd_pre(xc, dtr, dt_bias, a_log) + (xc,)


def f_ssd_post(y, xc, z, d_skip, norm_g):
    xs = jax.nn.silu(xc[:, :SSD_W])
    y = (y + xs * _head_expand(d_skip, SSD_W)) * jax.nn.silu(z)
    gw = SSD_W // SSD_NG
    parts = []
    for g in range(SSD_NG):
        seg = y[:, g * gw:(g + 1) * gw]
        parts.append(seg * lax.rsqrt(jnp.mean(seg * seg, axis=-1, keepdims=True) + EPS))
    return (jnp.concatenate(parts, axis=1) * norm_g,)


def f_rwkv_pre(f, w0, w_up, a0, a_up, g_up, k_k, k_a):
    r, k, v = f[:, 0:1024], f[:, 1024:2048], f[:, 2048:3072]
    wl, al, gl = f[:, 3072:3200], f[:, 3200:3328], f[:, 3328:3584]
    w = -jax.nn.softplus(-(w0 + _dot16(jnp.tanh(wl), w_up))) - 0.5
    decay = jnp.exp(-jnp.exp(w))
    a = jax.nn.sigmoid(a0 + _dot16(al, a_up))
    g = _dot16(jax.nn.sigmoid(gl), g_up)
    kk = k * k_k
    k2 = k * (1.0 + (a - 1.0) * k_a)
    kkn = kk * lax.rsqrt(jnp.maximum(_head_sum(kk * kk), 1e-24))
    return r, decay, k2, v, kkn, a, g


def f_rwkv_pre_pass(f, w0, w_up, a0, a_up, g_up, k_k, k_a):
    out = f_rwkv_pre(f, w0, w_up, a0, a_up, g_up, k_k, k_a)
    return out + (out[0], out[2], out[3])


def f_rwkv_post(y, r, k2, v, g, ln_g, ln_b, r_k):
    mean = _head_sum(y) * (1.0 / RW_HD)
    yc = y - mean
    var = _head_sum(yc * yc) * (1.0 / RW_HD)
    yn = yc * lax.rsqrt(var + GN_EPS) * ln_g + ln_b
    bonus = _head_sum(r * k2 * r_k) * v
    return ((yn + bonus) * g,)


def _neg_expm1(y):
    series = -y * (1.0 + y * (0.5 + y * (1.0 / 6.0 + y * (1.0 / 24.0 + y * (1.0 / 120.0)))))
    return jnp.where(y > -0.1, series, 1.0 - jnp.exp(y))


def f_lru_pre(t0, xc, w_a, b_a, w_x, b_x, lam):
    gate_r = jax.nn.sigmoid(_dot16(xc, w_a) + b_a)
    gate_i = jax.nn.sigmoid(_dot16(xc, w_x) + b_x)
    log_a = -LRU_C * gate_r * jax.nn.softplus(-lam)
    mult = jnp.sqrt(jnp.maximum(_neg_expm1(2.0 * log_a), 0.0))
    mult = jnp.where(_iota(xc.shape, 0) + t0 == 0, 1.0, mult)
    return jnp.exp(log_a), xc * gate_i * mult


def f_lru_post(h, gl):
    return (h * jax.nn.gelu(gl),)


TB = 256
TBH = 256
SCAN_TB = 256
RW_LC = 64


def _even_fwd(hn, w, tag):
    n = lambda s: f"{tag}_{s}"
    u = _mm(hn, w["in_u"], name=n("proj_u"))
    z = _mm(hn, w["in_z"], name=n("proj_z"))
    xbc = _mm(hn, w["in_xbc"], name=n("proj_xbc"))
    dtr = _mm(hn, w["in_dt"], name=n("proj_dt"))
    bu_re = _mm(u, w["wb_re"], name=n("s5_bu_re"))
    bu_im = _mm(u, w["wb_im"], name=n("s5_bu_im"))
    xr, xi = _s5_scan_fwd(w["abar_re"], w["abar_im"], bu_re, bu_im, tb=SCAN_TB, name=n("s5_scan"))
    s5c = [w["wc_re"], w["wc_im"], w["s5_d"], w["glu_w"], w["glu_b"]]
    (ya,) = _stage(f_s5_post, [xr, xi, u], s5c, tb=TB, name=n("s5_post"), out_dtypes=[BF16])
    xc = _conv_fwd(xbc, w["ssd_conv_w"], w["ssd_conv_b"], tb=TB, name=n("ssd_conv"))
    xdt, da, bm, cm = _stage(f_ssd_pre, [xc, dtr], [w["dt_bias"], w["a_log"]], tb=TB, name=n("ssd_pre"),
                             out_dtypes=[F32] * 4)
    y, ck = _ssd_fwd(xdt, da, bm, cm, name=n("ssd_scan"))
    (yb,) = _stage(f_ssd_post, [y, xc, z], [w["ssd_d"], w["ssd_norm"]], tb=TB, name=n("ssd_post"), out_dtypes=[BF16])
    mo = _mm(ya, w["out_a"], name=n("out_a"))
    mo = _mm(yb, w["out_b"], add=mo, name=n("out_b"))
    res = dict(u=u, z=z, xbc=xbc, dtr=dtr, xr=xr, xi=xi, ya=ya, xc=xc, xdt=xdt, da=da, bm=bm, cm=cm, y=y, ck=ck, yb=yb)
    return mo, res


def _even_bwd(dmo, hn, w, r, tag):
    n = lambda s: f"{tag}_{s}"
    g = {}
    g["out_a"] = _mm(r["ya"], dmo, ta=True, name=n("d_out_a"))
    g["out_b"] = _mm(r["yb"], dmo, ta=True, name=n("d_out_b"))
    dya = _mm(dmo, w["out_a"], tb=True, name=n("dya"))
    dyb = _mm(dmo, w["out_b"], tb=True, name=n("dyb"))
    dy, dxc1, dz, g["ssd_d"], g["ssd_norm"] = _stage_vjp(
        f_ssd_post, [r["y"], r["xc"], r["z"]], [w["ssd_d"], w["ssd_norm"]], [dyb], tb=TBH, name=n("ssd_post_b"),
        drow=[0, 1, 2], dconst=[0, 1])
    dxdt, dda, dbm, dcm = _ssd_bwd(r["xdt"], r["da"], r["bm"], r["cm"], r["ck"], dy, name=n("ssd_scan_b"))
    dxc, ddtr, g["dt_bias"], g["a_log"] = _stage_vjp(
        f_ssd_pre_pass, [r["xc"], r["dtr"]], [w["dt_bias"], w["a_log"]], [dxdt, dda, dbm, dcm, dxc1], tb=TBH,
        name=n("ssd_pre_b"), drow=[0, 1], dconst=[0, 1])
    dxbc, g["ssd_conv_w"], g["ssd_conv_b"] = _conv_bwd(r["xbc"], w["ssd_conv_w"], dxc, tb=TB, name=n("ssd_conv_b"))
    s5c = [w["wc_re"], w["wc_im"], w["s5_d"], w["glu_w"], w["glu_b"]]
    dxr, dxi, du1, g["wc_re"], g["wc_im"], g["s5_d"], g["glu_w"], g["glu_b"] = _stage_vjp(
        f_s5_post, [r["xr"], r["xi"], r["u"]], s5c, [dya], tb=TBH, name=n("s5_post_b"),
        drow=[0, 1, 2], dconst=[0, 1, 2, 3, 4])
    dbr, dbi, g["abar_re"], g["abar_im"] = _s5_scan_bwd(w["abar_re"], w["abar_im"], r["xr"], r["xi"], dxr, dxi,
                                                         tb=SCAN_TB, name=n("s5_scan_b"))
    g["wb_re"] = _mm(r["u"], dbr, ta=True, name=n("d_wb_re"))
    g["wb_im"] = _mm(r["u"], dbi, ta=True, name=n("d_wb_im"))
    du = _mm(dbr, w["wb_re"], tb=True, add=du1, name=n("du_re"))
    du = _mm(dbi, w["wb_im"], tb=True, add=du, name=n("du_im"))
    segs = (("in_u", du), ("in_z", dz), ("in_xbc", dxbc), ("in_dt", ddtr))
    dhn = None
    for key, dseg in segs:
        g[key] = _mm(hn, dseg, ta=True, name=n("d_" + key))
        dhn = _mm(dseg, w[key], tb=True, add=dhn, name=n("dhn_" + key))
    return dhn, g


def _odd_fwd(hn, w, tag):
    n = lambda s: f"{tag}_{s}"
    rw = _mm(hn, w["in_rw"], name=n("proj_rw"))
    xl = _mm(hn, w["in_xl"], name=n("proj_xl"))
    gl = _mm(hn, w["in_gl"], name=n("proj_gl"))
    f = _conv_fwd(rw, w["mix_w"], w["mix_b"], tb=TB, name=n("rwkv_shift"))
    rc = [w[k] for k in ("w0", "w_up", "a0", "a_up", "g_up", "k_k", "k_a")]
    r_, dec, k2, v, kkn, a, gate = _stage(f_rwkv_pre, [f], rc, tb=TB, name=n("rwkv_pre"), out_dtypes=[F32] * 7)
    t3 = lambda z: z.reshape(-1, RW_PAIRS, 128)
    y, ck, hist = _rwkv_scan_fwd(t3(r_), t3(dec), t3(k2), t3(v), t3(kkn), t3(a), lc=RW_LC, name=n("rwkv_scan"))
    y = y.reshape(-1, RW_W)
    (yc,) = _stage(f_rwkv_post, [y, r_, k2, v, gate], [w["ln_g"], w["ln_b"], w["r_k"]], tb=TB, name=n("rwkv_post"),
                   out_dtypes=[BF16])
    xc = _conv_fwd(xl, w["lru_conv_w"], w["lru_conv_b"], tb=TB, name=n("lru_conv"))
    lc = [w[k] for k in ("lru_wa", "lru_b_a", "lru_wx", "lru_b_x", "lru_lam")]
    a_l, bx = _stage(f_lru_pre, [xc], lc, tb=TB, name=n("lru_pre"), out_dtypes=[F32] * 2, pos=True)
    h = _lru_scan_fwd(a_l, bx, tb=SCAN_TB, name=n("lru_scan"))
    (yd,) = _stage(f_lru_post, [h, gl], [], tb=TB, name=n("lru_post"), out_dtypes=[BF16])
    mo = _mm(yc, w["out_a"], name=n("out_a"))
    mo = _mm(yd, w["out_b"], add=mo, name=n("out_b"))
    res = dict(rw=rw, xl=xl, gl=gl, f=f, r=r_, dec=dec, k2=k2, v=v, kkn=kkn, a=a, gate=gate, y=y, ck=ck, hist=hist, yc=yc,
               xc=xc, a_l=a_l, h=h, yd=yd)
    return mo, res


def _odd_bwd(dmo, hn, w, r, tag):
    n = lambda s: f"{tag}_{s}"
    g = {}
    g["out_a"] = _mm(r["yc"], dmo, ta=True, name=n("d_out_a"))
    g["out_b"] = _mm(r["yd"], dmo, ta=True, name=n("d_out_b"))
    dyc = _mm(dmo, w["out_a"], tb=True, name=n("dyc"))
    dyd = _mm(dmo, w["out_b"], tb=True, name=n("dyd"))
    dh, dgl = _stage_vjp(f_lru_post, [r["h"], r["gl"]], [], [dyd], tb=TB, name=n("lru_post_b"), drow=[0, 1], dconst=[])
    da_l, dbx = _lru_scan_bwd(r["a_l"], r["h"], dh, tb=SCAN_TB, name=n("lru_scan_b"))
    lc = [w[k] for k in ("lru_wa", "lru_b_a", "lru_wx", "lru_b_x", "lru_lam")]
    dxc, g["lru_wa"], g["lru_b_a"], g["lru_wx"], g["lru_b_x"], g["lru_lam"] = _stage_vjp(
        f_lru_pre, [r["xc"]], lc, [da_l, dbx], tb=TBH, name=n("lru_pre_b"), drow=[0], dconst=[0, 1, 2, 3, 4], pos=True)
    dxl, g["lru_conv_w"], g["lru_conv_b"] = _conv_bwd(r["xl"], w["lru_conv_w"], dxc, tb=TB, name=n("lru_conv_b"))
    dy, dr1, dk1, dv1, dgate, g["ln_g"], g["ln_b"], g["r_k"] = _stage_vjp(
        f_rwkv_post, [r["y"], r["r"], r["k2"], r["v"], r["gate"]], [w["ln_g"], w["ln_b"], w["r_k"]], [dyc], tb=TBH,
        name=n("rwkv_post_b"), drow=[0, 1, 2, 3, 4], dconst=[0, 1, 2])
    t3 = lambda z: z.reshape(-1, RW_PAIRS, 128)
    dr2, ddec, dk2, dv2, dkkn, da = [z.reshape(-1, RW_W) for z in _rwkv_scan_bwd(
        t3(r["r"]), t3(r["dec"]), t3(r["k2"]), t3(r["v"]), t3(r["kkn"]), t3(r["a"]), r["ck"], r["hist"], t3(dy),
        lc=RW_LC, name=n("rwkv_scan_b"))]
    rc = [w[k] for k in ("w0", "w_up", "a0", "a_up", "g_up", "k_k", "k_a")]
    df, g["w0"], g["w_up"], g["a0"], g["a_up"], g["g_up"], g["k_k"], g["k_a"] = _stage_vjp(
        f_rwkv_pre_pass, [r["f"]], rc, [dr2, ddec, dk2, dv2, dkkn, da, dgate, dr1, dk1, dv1], tb=TBH,
        name=n("rwkv_pre_b"), drow=[0], dconst=[0, 1, 2, 3, 4, 5, 6])
    drw, g["mix_w"], _ = _conv_bwd(r["rw"], w["mix_w"], df, tb=TB, name=n("rwkv_shift_b"))
    segs = (("in_rw", drw), ("in_xl", dxl), ("in_gl", dgl))
    dhn = None
    for key, dseg in segs:
        g[key] = _mm(hn, dseg, ta=True, name=n("d_" + key))
        dhn = _mm(dseg, w[key], tb=True, add=dhn, name=n("dhn_" + key))
    return dhn, g


def _layer_fwd(h, p_i, w, odd, tag):
    n = lambda s: f"{tag}_{s}"
    (hn,) = _stage(f_norm, [h], [w["norm_mix"]], tb=TB, name=n("norm_mix"), out_dtypes=[BF16])
    mo, mres = (_odd_fwd if odd else _even_fwd)(hn, w, tag)
    h1, hf = _stage(f_add_norm, [h, mo], [w["norm_ffn"]], tb=TB, name=n("norm_ffn"), out_dtypes=[F32, BF16])
    u, act = _mm(hf, w["mlp_w1"], name=n("mlp_up"), epilogue=lambda acc: (acc,) + f_relu2(acc), out_dtypes=[F32, BF16])
    m2 = _mm(act, w["mlp_w2"], name=n("mlp_down"))
    h2, hp = _stage(f_add_norm, [h1, m2], [w["norm_pl"]], tb=TB, name=n("norm_pl"), out_dtypes=[F32, BF16])
    gl = _mm(hp, w["pl_gate"], name=n("pl_gate"))
    pp = _mm(p_i, w["pl_proj"], name=n("pl_proj"))
    (h3,) = _stage(f_plgate, [h2, gl, pp], [], tb=TB, name=n("pl_mix"), out_dtypes=[F32])
    res = dict(h=h, hn=hn, mo=mo, mix=mres, h1=h1, hf=hf, u=u, act=act, m2=m2, h2=h2, hp=hp, gl=gl, pp=pp)
    return h3, res


def _layer_bwd(dh3, p_i, w, r, odd, tag, stacks):
    n = lambda s: f"{tag}_{s}"
    g = {}
    wgrad = lambda key, x, dy, cols_cut, shard: _mm_grad(x, dy, layer=int(odd), cols_cut=cols_cut, shard=shard,
                                                        prev=stacks[key] if stacks else None, name=n("d_" + key))
    dh2, dgl, dpp = _stage_vjp(f_plgate, [r["h2"], r["gl"], r["pp"]], [], [dh3], tb=TB, name=n("pl_mix_b"),
                               drow=[0, 1, 2], dconst=[])
    g["pl_proj"] = wgrad("pl_proj", p_i, dpp, True, (PL_DIM, D // 4))
    g["pl_gate"] = wgrad("pl_gate", r["hp"], dgl, False, (D // 4, D))
    dhp = _mm(dgl, w["pl_gate"], tb=True, name=n("dhp"))
    dh1, dm2, g["norm_pl"] = _stage_vjp(f_add_norm, [r["h1"], r["m2"]], [w["norm_pl"]], [dh2, dhp], tb=TB,
                                        name=n("norm_pl_b"), drow=[0, 1], dconst=[0])
    g["mlp_w2"] = wgrad("mlp_w2", r["act"], dm2, False, (D_FF // 4, D))
    (du,) = _mm(dm2, w["mlp_w2"], tb=True, name=n("dact"), extra=[r["u"]], out_dtypes=[BF16],
                epilogue=lambda acc, u: (acc * (2.0 * jnp.maximum(u, 0.0)),))
    g["mlp_w1"] = wgrad("mlp_w1", r["hf"], du, True, (D, D_FF // 4))
    dhf = _mm(du, w["mlp_w1"], tb=True, name=n("dhf"))
    dh, dmo, g["norm_ffn"] = _stage_vjp(f_add_norm, [r["h"], r["mo"]], [w["norm_ffn"]], [dh1, dhf], tb=TB,
                                        name=n("norm_ffn_b"), drow=[0, 1], dconst=[0])
    dhn, gm = (_odd_bwd if odd else _even_bwd)(dmo, r["hn"], w, r["mix"], tag)
    g.update(gm)
    dh0, g["norm_mix"] = _stage_vjp(f_norm_pass, [r["h"]], [w["norm_mix"]], [dhn, dh], tb=TB, name=n("norm_mix_b"),
                                    drow=[0], dconst=[0])
    return dh0, g


def _pad_to(a, size, axis):
    pad = [(0, 0)] * a.ndim
    pad[axis] = (0, size - a.shape[axis])
    return jnp.pad(a, pad)


def _rw_pad(a):
    return jnp.concatenate([a[..., :3072], _pad_to(a[..., 3072:3168], 128, -1), _pad_to(a[..., 3168:3264], 128, -1),
                            a[..., 3264:3520]], axis=-1)


def _rw_unpad(a):
    return jnp.concatenate([a[..., :3072], a[..., 3072:3168], a[..., 3200:3296], a[..., 3328:3584]], axis=-1)


def _block_diag(w):
    nb, bs, _ = w.shape
    eye = jnp.eye(nb, dtype=w.dtype)
    return (w[:, :, None, :] * eye[:, None, :, None]).reshape(nb * bs, nb * bs)


def _diag_blocks(w):
    nb = LRU_B
    bs = w.shape[0] // nb
    return jnp.stack([w[h * bs:(h + 1) * bs, h * bs:(h + 1) * bs] for h in range(nb)])


def _s5_prep_inputs(fw):
    lstep = jnp.broadcast_to(_pad_to(fw["s5_log_step"].astype(F32), 128, 1), (8, 128))
    t16 = lambda b: jnp.transpose(b[0], (2, 0, 1)).reshape(S5_G, S5_N)
    tc = lambda c: jnp.transpose(c[0], (0, 2, 1)).reshape(S5_N, S5_G)
    return [fw["s5_lam_re"].reshape(1, S5_N), fw["s5_lam_im"].reshape(1, S5_N), lstep,
            t16(fw["s5_b_re"]), t16(fw["s5_b_im"]), tc(fw["s5_c_re"]), tc(fw["s5_c_im"])]


def _layer_weights(fw, i):
    w = {k: fw[k][i:i + 1] for k in ("norm_mix", "norm_ffn", "norm_pl")}
    for k in ("mlp_w1", "mlp_w2", "pl_proj", "pl_gate"):
        w[k] = (fw[k], i)
    return w


def _even_weights(fw, prep):
    w = _layer_weights(fw, 0)
    ein, eout = fw["e_in_proj"][0], fw["e_out_proj"][0]
    w.update(in_u=ein[:, :512], in_z=ein[:, 512:2048], in_xbc=ein[:, 2048:4608], in_dt=_pad_to(ein[:, 4608:], 128, 1),
             out_a=eout[:512], out_b=eout[512:])
    abar_re, abar_im, wb_re, wb_im, wc_re, wc_im = prep
    w.update(abar_re=abar_re, abar_im=abar_im, wb_re=wb_re, wb_im=wb_im, wc_re=wc_re.astype(BF16), wc_im=wc_im.astype(BF16),
             s5_d=fw["s5_d"], glu_w=fw["s5_glu_w"][0], glu_b=fw["s5_glu_b"],
             ssd_conv_w=_pad_to(fw["ssd_conv_w"][0], 8, 0), ssd_conv_b=fw["ssd_conv_b"],
             dt_bias=_pad_to(fw["ssd_dt_bias"], 128, 1), a_log=_pad_to(fw["ssd_a_log"], 128, 1),
             ssd_d=_pad_to(fw["ssd_d"], 128, 1), ssd_norm=fw["ssd_norm"])
    return w


def _odd_weights(fw):
    w = _layer_weights(fw, 1)
    oin, oout = fw["o_in_proj"][0], fw["o_out_proj"][0]
    mu = _rw_pad(fw["rwkv_mu"])
    zero = jnp.zeros_like(mu)
    w.update(in_rw=_rw_pad(oin[:, :RW_IN]), in_xl=oin[:, RW_IN:RW_IN + LRU_W], in_gl=oin[:, RW_IN + LRU_W:],
             out_a=oout[:RW_W], out_b=oout[RW_W:],
             mix_w=jnp.concatenate([zero, zero, mu, 1.0 - mu, zero, zero, zero, zero], axis=0), mix_b=zero,
             w0=fw["rwkv_w0"], w_up=_pad_to(fw["rwkv_w_up"][0], 128, 0), a0=fw["rwkv_a0"],
             a_up=_pad_to(fw["rwkv_a_up"][0], 128, 0), g_up=fw["rwkv_g_up"][0], k_k=fw["rwkv_k_k"], k_a=fw["rwkv_k_a"],
             r_k=fw["rwkv_r_k"].reshape(1, RW_W), ln_g=fw["rwkv_ln_g"], ln_b=fw["rwkv_ln_b"],
             lru_conv_w=_pad_to(fw["lru_conv_w"][0], 8, 0), lru_conv_b=fw["lru_conv_b"],
             lru_wa=_block_diag(fw["lru_w_a"][0]).astype(BF16), lru_b_a=fw["lru_b_a"].reshape(1, LRU_W),
             lru_wx=_block_diag(fw["lru_w_x"][0]).astype(BF16), lru_b_x=fw["lru_b_x"].reshape(1, LRU_W),
             lru_lam=fw["lru_lam"].reshape(1, LRU_W))
    return w


def _global_grads(g0, g1, s5_grads, d_norm_final):
    out = {k: jnp.concatenate([g0[k], g1[k]], axis=0) for k in ("norm_mix", "norm_ffn", "norm_pl")}
    for k in STACKED:
        out[k] = g0[k]
    out["e_in_proj"] = jnp.concatenate([g0["in_u"], g0["in_z"], g0["in_xbc"], g0["in_dt"][:, :SSD_H]], axis=1)[None]
    out["e_out_proj"] = jnp.concatenate([g0["out_a"], g0["out_b"]], axis=0)[None]
    d_lam_re, d_lam_im, d_lstep, d_bre, d_bim, d_cre, d_cim = s5_grads
    out["s5_lam_re"] = d_lam_re.reshape(1, S5_GROUPS, S5_P)
    out["s5_lam_im"] = d_lam_im.reshape(1, S5_GROUPS, S5_P)
    out["s5_log_step"] = d_lstep[0:1, :S5_GROUPS]
    unb = lambda b: jnp.transpose(b.reshape(S5_G, S5_GROUPS, S5_P), (1, 2, 0))[None]
    unc = lambda c: jnp.transpose(c.reshape(S5_GROUPS, S5_P, S5_G), (0, 2, 1))[None]
    out.update(s5_b_re=unb(d_bre), s5_b_im=unb(d_bim), s5_c_re=unc(d_cre), s5_c_im=unc(d_cim),
               s5_d=g0["s5_d"], s5_glu_w=g0["glu_w"][None], s5_glu_b=g0["glu_b"],
               ssd_conv_w=g0["ssd_conv_w"][None, :4], ssd_conv_b=g0["ssd_conv_b"], ssd_dt_bias=g0["dt_bias"][:, :SSD_H],
               ssd_a_log=g0["a_log"][:, :SSD_H], ssd_d=g0["ssd_d"][:, :SSD_H], ssd_norm=g0["ssd_norm"])
    out["o_in_proj"] = jnp.concatenate([_rw_unpad(g1["in_rw"]), g1["in_xl"], g1["in_gl"]], axis=1)[None]
    out["o_out_proj"] = jnp.concatenate([g1["out_a"], g1["out_b"]], axis=0)[None]
    out.update(rwkv_mu=_rw_unpad(g1["mix_w"][2:3] - g1["mix_w"][3:4]), rwkv_w0=g1["w0"], rwkv_w_up=g1["w_up"][None, :RW_LORA],
               rwkv_a0=g1["a0"], rwkv_a_up=g1["a_up"][None, :RW_LORA], rwkv_g_up=g1["g_up"][None], rwkv_k_k=g1["k_k"],
               rwkv_k_a=g1["k_a"], rwkv_r_k=g1["r_k"].reshape(1, RW_H, RW_HD), rwkv_ln_g=g1["ln_g"], rwkv_ln_b=g1["ln_b"],
               lru_conv_w=g1["lru_conv_w"][None, :4], lru_conv_b=g1["lru_conv_b"],
               lru_w_a=_diag_blocks(g1["lru_wa"])[None], lru_b_a=g1["lru_b_a"].reshape(1, LRU_B, 64),
               lru_w_x=_diag_blocks(g1["lru_wx"])[None], lru_b_x=g1["lru_b_x"].reshape(1, LRU_B, 64),
               lru_lam=g1["lru_lam"].reshape(1, LRU_B, 64), norm_final=d_norm_final.reshape(D))
    return out


def _local_step(x, p, target, fw):
    prep_in = _s5_prep_inputs(fw)
    prep = _single(f_s5_prep, prep_in, name="s5_prep")
    w0, w1 = _even_weights(fw, prep), _odd_weights(fw)
    h1, r0 = _layer_fwd(x, p[0], w0, False, "l0")
    h2, r1 = _layer_fwd(h1, p[1], w1, True, "l1")
    gf = fw["norm_final"].reshape(1, D)
    (loss8,) = _stage(f_loss, [h2, target], [gf], tb=TB, name="loss", out_dtypes=[], n_acc=1)
    one = jnp.zeros((8, 128), F32).at[0, 0].set(1.0)
    dh2, d_gf = _stage_vjp(f_loss, [h2, target], [gf], [], tb=TB, name="loss_b", drow=[0], dconst=[0], acc_cots=[one])
    dh1, g1 = _layer_bwd(dh2, p[1], w1, r1, True, "l1", None)
    dx, g0 = _layer_bwd(dh1, p[0], w0, r0, False, "l0", g1)
    cots = [g0[k] for k in ("abar_re", "abar_im", "wb_re", "wb_im", "wc_re", "wc_im")]
    s5_grads = _single_vjp(f_s5_prep, prep_in, cots, name="s5_prep_b")
    return loss8[0, 0], dx, _global_grads(g0, g1, s5_grads, d_gf)


def _xyc():
    return lax.axis_index("x"), lax.axis_index("y"), lax.axis_index("c")


def _flip(v, bit):
    return 1 - v if bit else v


def _remote(src, dst, send_sems, recv_sems, k, dev):
    return pltpu.make_async_remote_copy(src_ref=src, dst_ref=dst, send_sem=send_sems.at[k], recv_sem=recv_sems.at[k],
                                        device_id=dev, device_id_type=MESH)


CHIP_FLIPS = ((1, 0), (0, 1), (1, 1))


def _gather_chips(arrs, out_shapes, places, *, name):
    n = len(arrs)

    def body(*refs):
        ins, outs = refs[:n], refs[n:2 * n]
        send_sems, recv_sems = refs[2 * n:]
        x, y, c = _xyc()
        chip, sib = 2 * x + y, (x, y, 1 - c)
        peers = [(_flip(x, fx), _flip(y, fy)) for fx, fy in CHIP_FLIPS]
        first = [_remote(ins[a].at[c], places[a](outs[a], chip, c), send_sems, recv_sems, 6 * a + j, (px, py, c))
                 for a in range(n) for j, (px, py) in enumerate(peers)]
        for cp in first:
            cp.start()
        passed = []
        for a in range(n):
            for j, (px, py) in enumerate(peers):
                landed = places[a](outs[a], 2 * px + py, c)
                _remote(ins[a].at[c], landed, send_sems, recv_sems, 6 * a + j, (px, py, c)).wait_recv()
                cp = _remote(landed, landed, send_sems, recv_sems, 6 * a + 3 + j, sib)
                cp.start()
                passed.append(cp)
        for a in range(n):
            for j, (px, py) in enumerate(peers):
                other = places[a](outs[a], 2 * px + py, 1 - c)
                _remote(other, other, send_sems, recv_sems, 6 * a + 3 + j, sib).wait_recv()
        for cp in first + passed:
            cp.wait_send()

    return pl.pallas_call(
        body, out_shape=[SDS(s, a.dtype) for s, a in zip(out_shapes, arrs)], in_specs=[ANY] * n, out_specs=[ANY] * n,
        scratch_shapes=_dma_sems(6 * n), name=name,
    )(*arrs)


def _place_own(full, own, chip_vec, axis, *, name):
    layers, rows, cols = own.shape
    tb = min(rows, 512)
    per = rows // tb
    omap = ((lambda l, i, chip_ref: (l, chip_ref[0] * per + i, 0)) if axis == 1
            else (lambda l, i, chip_ref: (l, i, chip_ref[0])))

    def body(chip_ref, own_ref, full_ref, o_ref):
        o_ref[...] = own_ref[...]

    return pl.pallas_call(
        body,
        grid_spec=pltpu.PrefetchScalarGridSpec(
            num_scalar_prefetch=1, grid=(layers, per),
            in_specs=[pl.BlockSpec((None, tb, cols), lambda l, i, chip_ref: (l, i, 0)), ANY],
            out_specs=pl.BlockSpec((None, tb, cols), omap)),
        out_shape=SDS(full.shape, full.dtype), input_output_aliases={2: 0},
        compiler_params=_cparams(("arbitrary", "arbitrary")), name=name,
    )(chip_vec, own, full)


def _dma_sems(n):
    return [pltpu.SemaphoreType.DMA((n,)), pltpu.SemaphoreType.DMA((n,))]


def _send_halves(arrs, *, name):
    n = len(arrs)

    def body(*refs):
        ins, outs = refs[:n], refs[n:2 * n]
        send_sems, recv_sems = refs[2 * n:]
        x, y, c = _xyc()
        copies = [_remote(ins[a].at[k, 1 - c], outs[a].at[k], send_sems, recv_sems, 4 * a + k, (x, y, 1 - c))
                  for a in range(n) for k in range(arrs[a].shape[0])]
        for cp in copies:
            cp.start()
        for cp in copies:
            cp.wait_recv()
        for cp in copies:
            cp.wait_send()

    return pl.pallas_call(
        body, out_shape=[SDS(a.shape[:1] + a.shape[2:], a.dtype) for a in arrs], in_specs=[ANY] * n, out_specs=[ANY] * n,
        scratch_shapes=_dma_sems(4 * n), name=name,
    )(*arrs)


def _add_half(g, recv, c_vec, *, tb, out_dtype, name):
    slots, _, rh, cols = g.shape
    tb = min(tb, rh)

    def body(c_ref, g_ref, r_ref, o_ref):
        o_ref[...] = (g_ref[...] + r_ref[...]).astype(o_ref.dtype)

    return pl.pallas_call(
        body,
        grid_spec=pltpu.PrefetchScalarGridSpec(
            num_scalar_prefetch=1, grid=(slots, rh // tb),
            in_specs=[pl.BlockSpec((None, None, tb, cols), lambda k, i, c_ref: (k, c_ref[0], i, 0)),
                      pl.BlockSpec((None, tb, cols), lambda k, i, c_ref: (k, i, 0))],
            out_specs=pl.BlockSpec((None, tb, cols), lambda k, i, c_ref: (k, i, 0))),
        out_shape=SDS((slots, rh, cols), out_dtype), compiler_params=_cparams(("arbitrary", "arbitrary")), name=name,
    )(c_vec, g, recv)


def _scatter_chips(arrs, *, name):
    n = len(arrs)

    def body(*refs):
        ins, outs = refs[:n], refs[n:2 * n]
        send_sems, recv_sems = refs[2 * n:]
        x, y, c = _xyc()
        copies = []
        for a in range(n):
            for j, (fx, fy) in enumerate(CHIP_FLIPS):
                px, py = _flip(x, fx), _flip(y, fy)
                mine = ins[a].at[2 * px + py if arrs[a].shape[0] == 4 else 0]
                copies.append(_remote(mine, outs[a].at[j], send_sems, recv_sems, 3 * a + j, (px, py, c)))
        for cp in copies:
            cp.start()
        for cp in copies:
            cp.wait_recv()
        for cp in copies:
            cp.wait_send()

    return pl.pallas_call(
        body, out_shape=[SDS((3,) + a.shape[1:], a.dtype) for a in arrs], in_specs=[ANY] * n, out_specs=[ANY] * n,
        scratch_shapes=_dma_sems(3 * n), name=name,
    )(*arrs)


def _sum_chips(p, landed, chip_vec, *, tb, name):
    _, rh, cols = p.shape
    tb = min(tb, rh)

    def body(chip_ref, p_ref, l_ref, o_ref):
        f = lambda z: z.astype(F32)
        o_ref[...] = ((f(p_ref[...]) + f(l_ref[0])) + f(l_ref[1])) + f(l_ref[2])

    return pl.pallas_call(
        body,
        grid_spec=pltpu.PrefetchScalarGridSpec(
            num_scalar_prefetch=1, grid=(rh // tb,),
            in_specs=[pl.BlockSpec((None, tb, cols), lambda i, chip_ref: (chip_ref[0], i, 0)),
                      pl.BlockSpec((3, tb, cols), lambda i, chip_ref: (0, i, 0))],
            out_specs=pl.BlockSpec((tb, cols), lambda i, chip_ref: (i, 0))),
        out_shape=SDS((rh, cols), F32), compiler_params=_cparams(), name=name,
    )(chip_vec, p, landed)


def _sum_chips_ordered(p, landed, chip_vec, *, tb, name):
    _, rh, cols = p.shape
    tb = min(tb, rh)

    def body(chip_ref, p_ref, l_ref, o_ref):
        chip = chip_ref[0]
        acc = None
        for k in range(4):
            away = k ^ chip
            slot = jnp.where(away == 2, 0, jnp.where(away == 1, 1, 2))
            term = jnp.where(k == chip, p_ref[...], l_ref[slot])
            acc = term if acc is None else acc + term
        o_ref[...] = acc

    return pl.pallas_call(
        body,
        grid_spec=pltpu.PrefetchScalarGridSpec(
            num_scalar_prefetch=1, grid=(rh // tb,),
            in_specs=[pl.BlockSpec((None, tb, cols), lambda i, chip_ref: (0, i, 0)),
                      pl.BlockSpec((3, tb, cols), lambda i, chip_ref: (0, i, 0))],
            out_specs=pl.BlockSpec((tb, cols), lambda i, chip_ref: (i, 0))),
        out_shape=SDS((rh, cols), F32), compiler_params=_cparams(), name=name,
    )(chip_vec, p, landed)


def _swap_halves(arrs, *, name):
    n = len(arrs)

    def body(*refs):
        ins, outs = refs[:n], refs[n:2 * n]
        send_sems, recv_sems = refs[2 * n:]
        x, y, c = _xyc()
        copies = [_remote(ins[a], outs[a], send_sems, recv_sems, a, (x, y, 1 - c)) for a in range(n)]
        for cp in copies:
            cp.start()
        for cp in copies:
            cp.wait_recv()
        for cp in copies:
            cp.wait_send()

    return pl.pallas_call(
        body, out_shape=[SDS(a.shape, a.dtype) for a in arrs], in_specs=[ANY] * n, out_specs=[ANY] * n,
        scratch_shapes=_dma_sems(n), name=name,
    )(*arrs)


def _join_halves(mine, theirs, c_vec, *, tb, name):
    rh, cols = mine.shape
    tb = min(tb, rh)

    def body(c_ref, m_ref, t_ref, o_ref):
        o_ref[...] = jnp.where(pl.program_id(0) == c_ref[0], m_ref[...], t_ref[...])

    blk = pl.BlockSpec((tb, cols), lambda h, i, c_ref: (i, 0))
    return pl.pallas_call(
        body,
        grid_spec=pltpu.PrefetchScalarGridSpec(
            num_scalar_prefetch=1, grid=(2, rh // tb), in_specs=[blk, blk],
            out_specs=pl.BlockSpec((None, tb, cols), lambda h, i, c_ref: (h, i, 0))),
        out_shape=SDS((2, rh, cols), mine.dtype), compiler_params=_cparams(("arbitrary", "arbitrary")), name=name,
    )(c_vec, mine, theirs)


def f_adamw(w, g, m, v):
    m = ADAM_B1 * m + (1.0 - ADAM_B1) * g
    v = ADAM_B2 * v + (1.0 - ADAM_B2) * (g * g)
    m_hat = m / (1.0 - ADAM_B1 ** ADAM_STEP)
    v_hat = v / (1.0 - ADAM_B2 ** ADAM_STEP)
    return -ADAM_LR * (m_hat / (jnp.sqrt(v_hat) + ADAM_EPS) + ADAM_WD * w), m, v


def _adamw(w, g, m, v, *, name):
    shape = w.shape
    two = lambda a: a.reshape(-1, shape[-1])
    rows = two(w).shape[0]
    tb = 256 if rows % 256 == 0 else rows
    outs = _stage(f_adamw, [two(w), two(g), two(m), two(v)], [], tb=tb, name=name, out_dtypes=[F32] * 3)
    return [o.reshape(shape) for o in outs]


def _pack(arrs, rows=8):
    flat = jnp.concatenate([a.astype(F32).reshape(-1) for a in arrs])
    size = -(-flat.shape[0] // (rows * 128)) * (rows * 128)
    return _pad_to(flat, size, 0).reshape(-1, 128)


def _unpack(buf, shapes):
    flat = buf.reshape(-1)
    out, off = [], 0
    for s in shapes:
        n = math.prod(s)
        out.append(flat[off:off + n].reshape(s))
        off += n
    return out


WEIGHTS = ("norm_mix", "norm_ffn", "norm_pl", "mlp_w1", "mlp_w2", "pl_proj", "pl_gate", "e_in_proj", "e_out_proj",
           "s5_lam_re", "s5_lam_im", "s5_log_step", "s5_b_re", "s5_b_im", "s5_c_re", "s5_c_im", "s5_d", "s5_glu_w",
           "s5_glu_b", "ssd_conv_w", "ssd_conv_b", "ssd_dt_bias", "ssd_a_log", "ssd_d", "ssd_norm", "o_in_proj",
           "o_out_proj", "rwkv_mu", "rwkv_w0", "rwkv_w_up", "rwkv_a0", "rwkv_a_up", "rwkv_g_up", "rwkv_k_k", "rwkv_k_a",
           "rwkv_r_k", "rwkv_ln_g", "rwkv_ln_b", "lru_conv_w", "lru_conv_b", "lru_w_a", "lru_b_a", "lru_w_x", "lru_b_x",
           "lru_lam", "norm_final")
BIG = ("mlp_w1", "mlp_w2", "pl_proj", "pl_gate", "e_in_proj", "e_out_proj", "o_in_proj", "o_out_proj")
STACKED = BIG[:4]
SHARD_AXIS = {"mlp_w1": 2, "mlp_w2": 1, "pl_proj": 2, "pl_gate": 1, "e_in_proj": 2, "e_out_proj": 1, "s5_glu_w": 1,
              "ssd_conv_w": 2, "o_in_proj": 2, "o_out_proj": 1, "rwkv_mu": 1, "rwkv_w0": 1, "rwkv_w_up": 2, "rwkv_a0": 1,
              "rwkv_a_up": 2, "rwkv_g_up": 2, "rwkv_k_k": 1, "rwkv_k_a": 1, "rwkv_ln_g": 1, "rwkv_ln_b": 1,
              "lru_conv_w": 2, "lru_conv_b": 1}
SMALL = tuple(n for n in WEIGHTS if n not in BIG)
SMALL_SHARDED = tuple(n for n in SMALL if n in SHARD_AXIS)


def _gather_weights(w):
    shapes = [w[n].shape for n in SMALL_SHARDED]
    chip = 2 * lax.axis_index("x") + lax.axis_index("y")
    mine = [w[n].astype(BF16) for n in BIG] + [_pack([w[n] for n in SMALL_SHARDED], rows=16)]
    out_shapes, places = [], []
    for n, a in zip(BIG + ("small",), mine):
        layers, rows, cols = a.shape if a.ndim == 3 else (1,) + a.shape
        ax = SHARD_AXIS.get(n)
        if ax == 1:
            step = rows if layers == 2 else rows // 2
            out_shapes.append((layers, 4 * rows, cols))
            places.append(lambda o, k, h, layers=layers, rows=rows, step=step: o.at[
                h if layers == 2 else 0, pl.ds(pl.multiple_of(k * rows + (0 if layers == 2 else h * step), 16), step), :])
        elif ax == 2 and layers == 2:
            out_shapes.append((layers, rows, 4 * cols))
            places.append(lambda o, k, h, cols=cols: o.at[h, :, pl.ds(pl.multiple_of(k * cols, 128), cols)])
        else:
            out_shapes.append((4, 2, layers * rows // 2, cols))
            places.append(lambda o, k, h: o.at[k, h])
    got = _gather_chips([a.reshape(2, -1, a.shape[-1]) for a in mine], out_shapes, places, name="gather_weights")
    fw = {n: w[n] for n in SMALL if n not in SHARD_AXIS}
    for n, g, a in zip(BIG, got[:-1], mine):
        if g.shape[0] == 4:
            g = lax.dynamic_update_index_in_dim(g.reshape((4,) + a.shape), a, chip, 0)
            fw[n] = jnp.concatenate([g[k] for k in range(4)], axis=SHARD_AXIS[n])
        else:
            fw[n] = _place_own(g, a, chip.astype(jnp.int32).reshape(1), SHARD_AXIS[n], name=f"place_{n}")
    small = lax.dynamic_update_index_in_dim(got[-1].reshape((4,) + mine[-1].shape), mine[-1], chip, 0)
    parts = [_unpack(small[k], shapes) for k in range(4)]
    for i, n in enumerate(SMALL_SHARDED):
        fw[n] = jnp.concatenate([parts[k][i] for k in range(4)], axis=SHARD_AXIS[n])
    return fw


def _reduce(grads, w, chip, loss):
    stacks = []
    for n in BIG:
        cols = w[n].shape[-1]
        stacks.append(grads[n] if n in STACKED else
                      jnp.stack(jnp.split(grads[n], 4, axis=SHARD_AXIS[n])).reshape(4, 2, -1, cols))
    shapes = [grads[n].shape for n in SMALL] + [(1,)]
    small = _pack([grads[n] for n in SMALL] + [loss.reshape(1)], rows=1024).reshape(1, 2, -1, 128)
    c_vec = lax.axis_index("c").astype(jnp.int32).reshape(1)
    chip_vec = chip.astype(jnp.int32).reshape(1)
    got = _send_halves(stacks + [small], name="reduce_pair")
    sums = [_add_half(s, r, c_vec, tb=512, out_dtype=BF16, name=f"reduce_pair_sum_{n}")
            for n, s, r in zip(BIG, stacks, got)]
    sums.append(_add_half(small, got[-1], c_vec, tb=512, out_dtype=F32, name="reduce_pair_sum_small"))
    landed = _scatter_chips(sums, name="reduce_chips")
    halves = [_sum_chips(p, l, chip_vec, tb=256, name=f"reduce_chips_sum_{n}") for n, p, l in zip(BIG, sums, landed)]
    halves.append(_sum_chips_ordered(sums[-1], landed[-1], chip_vec, tb=512, name="reduce_chips_sum_small"))
    theirs = _swap_halves(halves, name="reduce_swap")
    names = BIG + ("small",)
    whole = [_join_halves(h, t, c_vec, tb=512, name=f"reduce_join_{n}") for n, h, t in zip(names, halves, theirs)]
    out = {n: j.reshape(w[n].shape) for n, j in zip(BIG, whole)}
    *parts, loss_sum = _unpack(whole[-1], shapes)
    for n, g in zip(SMALL, parts):
        if n in SHARD_AXIS:
            ax = SHARD_AXIS[n]
            size = w[n].shape[ax]
            g = lax.dynamic_slice_in_dim(g, chip * size, size, axis=ax)
        out[n] = g
    return out, loss_sum[0]


def kernel(x, p, norm_mix, norm_ffn, norm_pl, mlp_w1, mlp_w2, pl_proj, pl_gate, e_in_proj, e_out_proj, s5_lam_re, s5_lam_im, s5_log_step, s5_b_re, s5_b_im, s5_c_re, s5_c_im, s5_d, s5_glu_w, s5_glu_b, ssd_conv_w, ssd_conv_b, ssd_dt_bias, ssd_a_log, ssd_d, ssd_norm, o_in_proj, o_out_proj, rwkv_mu, rwkv_w0, rwkv_w_up, rwkv_a0, rwkv_a_up, rwkv_g_up, rwkv_k_k, rwkv_k_a, rwkv_r_k, rwkv_ln_g, rwkv_ln_b, lru_conv_w, lru_conv_b, lru_w_a, lru_b_a, lru_w_x, lru_b_x, lru_lam, norm_final, loss_target, m_norm_mix, m_norm_ffn, m_norm_pl, m_mlp_w1, m_mlp_w2, m_pl_proj, m_pl_gate, m_e_in_proj, m_e_out_proj, m_s5_lam_re, m_s5_lam_im, m_s5_log_step, m_s5_b_re, m_s5_b_im, m_s5_c_re, m_s5_c_im, m_s5_d, m_s5_glu_w, m_s5_glu_b, m_ssd_conv_w, m_ssd_conv_b, m_ssd_dt_bias, m_ssd_a_log, m_ssd_d, m_ssd_norm, m_o_in_proj, m_o_out_proj, m_rwkv_mu, m_rwkv_w0, m_rwkv_w_up, m_rwkv_a0, m_rwkv_a_up, m_rwkv_g_up, m_rwkv_k_k, m_rwkv_k_a, m_rwkv_r_k, m_rwkv_ln_g, m_rwkv_ln_b, m_lru_conv_w, m_lru_conv_b, m_lru_w_a, m_lru_b_a, m_lru_w_x, m_lru_b_x, m_lru_lam, m_norm_final, v_norm_mix, v_norm_ffn, v_norm_pl, v_mlp_w1, v_mlp_w2, v_pl_proj, v_pl_gate, v_e_in_proj, v_e_out_proj, v_s5_lam_re, v_s5_lam_im, v_s5_log_step, v_s5_b_re, v_s5_b_im, v_s5_c_re, v_s5_c_im, v_s5_d, v_s5_glu_w, v_s5_glu_b, v_ssd_conv_w, v_ssd_conv_b, v_ssd_dt_bias, v_ssd_a_log, v_ssd_d, v_ssd_norm, v_o_in_proj, v_o_out_proj, v_rwkv_mu, v_rwkv_w0, v_rwkv_w_up, v_rwkv_a0, v_rwkv_a_up, v_rwkv_g_up, v_rwkv_k_k, v_rwkv_k_a, v_rwkv_r_k, v_rwkv_ln_g, v_rwkv_ln_b, v_lru_conv_w, v_lru_conv_b, v_lru_w_a, v_lru_b_a, v_lru_w_x, v_lru_b_x, v_lru_lam, v_norm_final):
    given = dict(locals())
    w = {n: given[n] for n in WEIGHTS}
    m = {n: given["m_" + n] for n in WEIGHTS}
    v = {n: given["v_" + n] for n in WEIGHTS}
    chip = 2 * lax.axis_index("x") + lax.axis_index("y")

    fw = _gather_weights(w)
    loss, dx, grads = _local_step(x[0], p[:, 0], loss_target[0], fw)
    g, loss = _reduce(grads, w, chip, loss)

    delta, new_m, new_v = {}, {}, {}
    for n in WEIGHTS:
        delta[n], new_m[n], new_v[n] = _adamw(w[n], g[n], m[n], v[n], name=f"adamw_{n}")
    return (loss, dx[None], *[g[n] for n in WEIGHTS], *[delta[n] for n in WEIGHTS],
            *[new_m[n] for n in WEIGHTS], *[new_v[n] for n in WEIGHTS])
```
